```python
import jax, jax.numpy as jnp
from jax import lax
import numpy as np

D_MODEL = 1024
BATCH = 8
SEQ = 4096
DEPTH = 4

HEAD_DIM = 64
N_HEADS_TOTAL = D_MODEL // HEAD_DIM
N_MEM_HEADS = 4
N_MIX_HEADS = N_HEADS_TOTAL - N_MEM_HEADS
D_MIX = N_MIX_HEADS * HEAD_DIM
D_MEMQ = N_MEM_HEADS * HEAD_DIM
MEM_TOKENS = 256
D_FF = 2816
CONV_WIDTH = 3
ROPE_THETA = 10000.0
Q_BLOCK = 128
DILATED_BRANCHES = ((128, 1), (512, 4), (2048, 16))
N_MIXERS = 2
N_A_LAYERS = (DEPTH + 1) // 2
N_B_LAYERS = DEPTH // 2
FOX_IN = 3 * D_MIX + N_MIX_HEADS + D_MEMQ
DIL_IN = 3 * D_MIX + D_MEMQ
NORM_EPS = 1e-6
NEG = -1e30

kernel_name = 'hybrid_fox_dilated_memory_convffn'


def rmsnorm(x, g):
    xf = x.astype(jnp.float32)
    y = xf * lax.rsqrt(jnp.mean(xf * xf, axis=-1, keepdims=True) + NORM_EPS)
    return (y * g.astype(jnp.float32)).astype(x.dtype)


def split_heads(t, n_heads):
    b, s, _ = t.shape
    return t.reshape(b, s, n_heads, HEAD_DIM).transpose(0, 2, 1, 3)


def merge_heads(t):
    b, h, s, d = t.shape
    return t.transpose(0, 2, 1, 3).reshape(b, s, h * d)


def rope_tables(seq):
    inv = 1.0 / (ROPE_THETA ** (jnp.arange(0, HEAD_DIM, 2, dtype=jnp.float32) / HEAD_DIM))
    ang = jnp.arange(seq, dtype=jnp.float32)[:, None] * inv[None, :]
    return jnp.cos(ang), jnp.sin(ang)


def apply_rope(x, cos, sin):
    x1, x2 = jnp.split(x.astype(jnp.float32), 2, axis=-1)
    y = jnp.concatenate([x1 * cos - x2 * sin, x2 * cos + x1 * sin], axis=-1)
    return y.astype(x.dtype)


def fox_attention(q, k, v, log_f):
    b, h, s, dh = q.shape
    c = lax.cumsum(log_f, axis=2)
    nb = s // Q_BLOCK
    scale = dh ** -0.5
    qb = q.reshape(b, h, nb, Q_BLOCK, dh).transpose(2, 0, 1, 3, 4)
    cb = c.reshape(b, h, nb, Q_BLOCK).transpose(2, 0, 1, 3)
    starts = jnp.arange(nb, dtype=jnp.int32) * Q_BLOCK
    kpos = jnp.arange(s, dtype=jnp.int32)

    def one_block(args):
        qi, ci, st = args
        sc = jnp.einsum('bhqd,bhkd->bhqk', qi, k, preferred_element_type=jnp.float32) * scale
        sc = sc + ci[..., :, None] - c[..., None, :]
        qpos = st + jnp.arange(Q_BLOCK, dtype=jnp.int32)
        sc = jnp.where(kpos[None, :] <= qpos[:, None], sc, NEG)
        p = jax.nn.softmax(sc, axis=-1)
        return jnp.einsum('bhqk,bhkd->bhqd', p.astype(v.dtype), v)

    out = lax.map(one_block, (qb, cb, starts))
    return out.transpose(1, 2, 0, 3, 4).reshape(b, h, s, dh)


def dilated_branch(q, k, v, window, dilation):
    b, h, s, dh = q.shape
    L = window // dilation
    chunk = dilation * L
    sp = -(-s // chunk) * chunk
    n = sp // dilation
    nb = n // L
    scale = dh ** -0.5

    def strided(t):
        t = jnp.pad(t, ((0, 0), (0, 0), (0, sp - s), (0, 0)))
        t = t.reshape(b, h, n, dilation, dh).transpose(0, 1, 3, 2, 4)
        return t.reshape(b, h, dilation, nb, L, dh)

    def with_prev(t):
        prev = jnp.pad(t, ((0, 0), (0, 0), (0, 0), (1, 0), (0, 0), (0, 0)))[:, :, :, :-1]
        return jnp.concatenate([prev, t], axis=4)

    qs = strided(q)
    kb = with_prev(strided(k))
    vb = with_prev(strided(v))
    sc = jnp.einsum('bhrnqd,bhrnkd->bhrnqk', qs, kb, preferred_element_type=jnp.float32) * scale
    a = jnp.arange(L, dtype=jnp.int32)
    cidx = jnp.arange(2 * L, dtype=jnp.int32)
    blk = jnp.arange(nb, dtype=jnp.int32)
    dist = a[:, None] + L - cidx[None, :]
    valid = (dist >= 0) & (dist <= L)
    valid = valid[None] & ((blk[:, None, None] > 0) | (cidx[None, None, :] >= L))
    sc = jnp.where(valid, sc, NEG)
    m = jnp.max(sc, axis=-1, keepdims=True)
    e = jnp.exp(sc - m)
    den = jnp.sum(e, axis=-1, keepdims=True)
    o = jnp.einsum('bhrnqk,bhrnkd->bhrnqd', (e / den).astype(v.dtype), vb)
    lse = (m + jnp.log(den))[..., 0]
    o = o.reshape(b, h, dilation, n, dh).transpose(0, 1, 3, 2, 4).reshape(b, h, sp, dh)[:, :, :s]
    lse = lse.reshape(b, h, dilation, n).transpose(0, 1, 3, 2).reshape(b, h, sp)[:, :, :s]
    return o, lse


def dilated_attention(q, k, v):
    outs, lses = [], []
    for window, dilation in DILATED_BRANCHES:
        o, lse = dilated_branch(q, k, v, window, dilation)
        outs.append(o)
        lses.append(lse)
    wts = jax.nn.softmax(jnp.stack(lses, axis=0), axis=0)
    return jnp.einsum('gbhs,gbhsd->bhsd', wts.astype(v.dtype), jnp.stack(outs, axis=0))


def memory_attention(qm, mem_n, w_mem_kv):
    km, vm = jnp.split(mem_n @ w_mem_kv, 2, axis=-1)
    km = split_heads(km, N_MEM_HEADS)
    vm = split_heads(vm, N_MEM_HEADS)
    sc = jnp.einsum('bhqd,bhkd->bhqk', qm, km, preferred_element_type=jnp.float32) * (HEAD_DIM ** -0.5)
    p = jax.nn.softmax(sc, axis=-1)
    return jnp.einsum('bhqk,bhkd->bhqd', p.astype(vm.dtype), vm)


def conv_ffn(xn, w_up, conv_w, conv_b, w_down):
    u = xn @ w_up
    s = u.shape[1]
    up = jnp.pad(u, ((0, 0), (CONV_WIDTH - 1, 0), (0, 0)))
    c = conv_b
    for j in range(CONV_WIDTH):
        c = c + conv_w[j] * up[:, j:j + s]
    val, gate = jnp.split(c, 2, axis=-1)
    return (jax.nn.silu(gate) * val) @ w_down


def _fwd_setup_inputs(seed: int = 0) -> dict:
    key = jax.random.key(seed)
    ks = jax.random.split(key, 16)
    f32 = jnp.float32
    nrm = lambda k, shape, scale: jax.random.normal(k, shape, f32) * scale
    forget_bias_init = 3.0
    return {
        'x': nrm(ks[0], (BATCH, SEQ, D_MODEL), 1.0),
        'mem': nrm(ks[1], (BATCH, MEM_TOKENS, D_MODEL), 1.0),
        'norm_mix': 1.0 + nrm(ks[2], (DEPTH, D_MODEL), 0.1),
        'norm_mem': 1.0 + nrm(ks[3], (DEPTH, D_MODEL), 0.1),
        'norm_ffn': 1.0 + nrm(ks[4], (DEPTH, D_MODEL), 0.1),
        'w_in_fox': nrm(ks[5], (N_A_LAYERS, D_MODEL, FOX_IN), D_MODEL ** -0.5),
        'b_forget': forget_bias_init + nrm(ks[6], (N_A_LAYERS, N_MIX_HEADS), 0.5),
        'w_in_dil': nrm(ks[7], (N_B_LAYERS, D_MODEL, DIL_IN), D_MODEL ** -0.5),
        'w_mem_kv': nrm(ks[8], (DEPTH, D_MODEL, 2 * D_MEMQ), D_MODEL ** -0.5),
        'w_out': nrm(ks[9], (DEPTH, D_MODEL, D_MODEL), D_MODEL ** -0.5),
        'w_up': nrm(ks[10], (DEPTH, D_MODEL, 2 * D_FF), D_MODEL ** -0.5),
        'conv_w': nrm(ks[11], (DEPTH, CONV_WIDTH, 2 * D_FF), CONV_WIDTH ** -0.5),
        'conv_b': nrm(ks[12], (DEPTH, 2 * D_FF), 0.02),
        'w_down': nrm(ks[13], (DEPTH, D_FF, D_MODEL), D_FF ** -0.5),
        'norm_final': 1.0 + nrm(ks[14], (D_MODEL,), 0.1),
    }


def _fwd_reference(x, mem, norm_mix, norm_mem, norm_ffn, w_in_fox, b_forget, w_in_dil,
              w_mem_kv, w_out, w_up, conv_w, conv_b, w_down, norm_final):
    s = x.shape[1]
    cos, sin = rope_tables(s)
    h = x
    for layer in range(DEPTH):
        kind = layer % N_MIXERS
        slot = layer // N_MIXERS
        xn = rmsnorm(h, norm_mix[layer])
        mn = rmsnorm(mem, norm_mem[layer])
        if kind == 0:
            proj = xn @ w_in_fox[slot]
            q, k, v, f_logit, qm = jnp.split(
                proj, [D_MIX, 2 * D_MIX, 3 * D_MIX, 3 * D_MIX + N_MIX_HEADS], axis=-1)
            log_f = jax.nn.log_sigmoid(
                (f_logit + b_forget[slot]).astype(jnp.float32)).transpose(0, 2, 1)
            mix = fox_attention(split_heads(q, N_MIX_HEADS), split_heads(k, N_MIX_HEADS),
                                split_heads(v, N_MIX_HEADS), log_f)
        else:
            proj = xn @ w_in_dil[slot]
            q, k, v, qm = jnp.split(proj, [D_MIX, 2 * D_MIX, 3 * D_MIX], axis=-1)
            qh = apply_rope(split_heads(q, N_MIX_HEADS), cos, sin)
            kh = apply_rope(split_heads(k, N_MIX_HEADS), cos, sin)
            mix = dilated_attention(qh, kh, split_heads(v, N_MIX_HEADS))
        mem_out = memory_attention(split_heads(qm, N_MEM_HEADS), mn, w_mem_kv[layer])
        heads = jnp.concatenate([mix, mem_out], axis=1)
        h = h + merge_heads(heads) @ w_out[layer]
        h = h + conv_ffn(rmsnorm(h, norm_ffn[layer]), w_up[layer], conv_w[layer],
                         conv_b[layer], w_down[layer])
    return rmsnorm(h, norm_final)


import jax as _jax
import jax.numpy as _jnp

TWIN_FORMAT = 'train_step'
FWD_PARAMS = ['x', 'mem', 'norm_mix', 'norm_mem', 'norm_ffn', 'w_in_fox', 'b_forget', 'w_in_dil', 'w_mem_kv', 'w_out', 'w_up', 'conv_w', 'conv_b', 'w_down', 'norm_final']
TWIN_WEIGHTS = ['norm_mix', 'norm_mem', 'norm_ffn', 'w_in_fox', 'b_forget', 'w_in_dil', 'w_mem_kv', 'w_out', 'w_up', 'conv_w', 'conv_b', 'w_down', 'norm_final']
TWIN_DIFF_INPUT = 'x'
TWIN_INPUTS = ['x', 'mem', 'norm_mix', 'norm_mem', 'norm_ffn', 'w_in_fox', 'b_forget', 'w_in_dil', 'w_mem_kv', 'w_out', 'w_up', 'conv_w', 'conv_b', 'w_down', 'norm_final', 'loss_target', 'm_norm_mix', 'm_norm_mem', 'm_norm_ffn', 'm_w_in_fox', 'm_b_forget', 'm_w_in_dil', 'm_w_mem_kv', 'm_w_out', 'm_w_up', 'm_conv_w', 'm_conv_b', 'm_w_down', 'm_norm_final', 'v_norm_mix', 'v_norm_mem', 'v_norm_ffn', 'v_w_in_fox', 'v_b_forget', 'v_w_in_dil', 'v_w_mem_kv', 'v_w_out', 'v_w_up', 'v_conv_w', 'v_conv_b', 'v_w_down', 'v_norm_final']
TWIN_OUTPUTS = ['loss', 'grad_x', 'grad_norm_mix', 'grad_norm_mem', 'grad_norm_ffn', 'grad_w_in_fox', 'grad_b_forget', 'grad_w_in_dil', 'grad_w_mem_kv', 'grad_w_out', 'grad_w_up', 'grad_conv_w', 'grad_conv_b', 'grad_w_down', 'grad_norm_final', 'delta_norm_mix', 'delta_norm_mem', 'delta_norm_ffn', 'delta_w_in_fox', 'delta_b_forget', 'delta_w_in_dil', 'delta_w_mem_kv', 'delta_w_out', 'delta_w_up', 'delta_conv_w', 'delta_conv_b', 'delta_w_down', 'delta_norm_final', 'new_m_norm_mix', 'new_m_norm_mem', 'new_m_norm_ffn', 'new_m_w_in_fox', 'new_m_b_forget', 'new_m_w_in_dil', 'new_m_w_mem_kv', 'new_m_w_out', 'new_m_w_up', 'new_m_conv_w', 'new_m_conv_b', 'new_m_w_down', 'new_m_norm_final', 'new_v_norm_mix', 'new_v_norm_mem', 'new_v_norm_ffn', 'new_v_w_in_fox', 'new_v_b_forget', 'new_v_w_in_dil', 'new_v_w_mem_kv', 'new_v_w_out', 'new_v_w_up', 'new_v_conv_w', 'new_v_conv_b', 'new_v_w_down', 'new_v_norm_final']
TWIN_LEAF_KINDS = {'loss': 'loss', 'grad_x': 'grad_x', 'grad_norm_mix': 'grad_w', 'grad_norm_mem': 'grad_w', 'grad_norm_ffn': 'grad_w', 'grad_w_in_fox': 'grad_w', 'grad_b_forget': 'grad_w', 'grad_w_in_dil': 'grad_w', 'grad_w_mem_kv': 'grad_w', 'grad_w_out': 'grad_w', 'grad_w_up': 'grad_w', 'grad_conv_w': 'grad_w', 'grad_conv_b': 'grad_w', 'grad_w_down': 'grad_w', 'grad_norm_final': 'grad_w', 'delta_norm_mix': 'delta_w', 'delta_norm_mem': 'delta_w', 'delta_norm_ffn': 'delta_w', 'delta_w_in_fox': 'delta_w', 'delta_b_forget': 'delta_w', 'delta_w_in_dil': 'delta_w', 'delta_w_mem_kv': 'delta_w', 'delta_w_out': 'delta_w', 'delta_w_up': 'delta_w', 'delta_conv_w': 'delta_w', 'delta_conv_b': 'delta_w', 'delta_w_down': 'delta_w', 'delta_norm_final': 'delta_w', 'new_m_norm_mix': 'new_m', 'new_m_norm_mem': 'new_m', 'new_m_norm_ffn': 'new_m', 'new_m_w_in_fox': 'new_m', 'new_m_b_forget': 'new_m', 'new_m_w_in_dil': 'new_m', 'new_m_w_mem_kv': 'new_m', 'new_m_w_out': 'new_m', 'new_m_w_up': 'new_m', 'new_m_conv_w': 'new_m', 'new_m_conv_b': 'new_m', 'new_m_w_down': 'new_m', 'new_m_norm_final': 'new_m', 'new_v_norm_mix': 'new_v', 'new_v_norm_mem': 'new_v', 'new_v_norm_ffn': 'new_v', 'new_v_w_in_fox': 'new_v', 'new_v_b_forget': 'new_v', 'new_v_w_in_dil': 'new_v', 'new_v_w_mem_kv': 'new_v', 'new_v_w_out': 'new_v', 'new_v_w_up': 'new_v', 'new_v_conv_w': 'new_v', 'new_v_conv_b': 'new_v', 'new_v_w_down': 'new_v', 'new_v_norm_final': 'new_v'}


def _forward(args):
    return _fwd_reference(*[args[k] for k in FWD_PARAMS])


def _output_shape():
    def fwd():
        inp = _fwd_setup_inputs(0)
        return _fwd_reference(*[inp[k] for k in FWD_PARAMS])
    out = _jax.eval_shape(fwd)
    return out.shape, out.dtype

N_MICROBATCH = 1
ADAM_LR = 0.001
ADAM_B1 = 0.9
ADAM_B2 = 0.999
ADAM_EPS = 1e-08
ADAM_WD = 0.01
ADAM_STEP = 10
PER_EXAMPLE_BATCH_AXIS = {'x': 0, 'mem': 0, 'loss_target': 0}
SHARED_INPUTS = []
_WEIGHT_DTYPES = {'norm_mix': _jnp.float32, 'norm_mem': _jnp.float32, 'norm_ffn': _jnp.float32, 'w_in_fox': _jnp.float32, 'b_forget': _jnp.float32, 'w_in_dil': _jnp.float32, 'w_mem_kv': _jnp.float32, 'w_out': _jnp.float32, 'w_up': _jnp.float32, 'conv_w': _jnp.float32, 'conv_b': _jnp.float32, 'w_down': _jnp.float32, 'norm_final': _jnp.float32}
MOMENT_SCALE = {'norm_mix': 7.704439e-02, 'norm_mem': 1.583832e-02, 'norm_ffn': 1.363253e-01, 'w_in_fox': 5.920782e-02, 'b_forget': 2.969185e-01, 'w_in_dil': 3.480404e-02, 'w_mem_kv': 2.208548e-02, 'w_out': 5.428580e-02, 'w_up': 5.849831e-02, 'conv_w': 5.954074e-02, 'conv_b': 6.176829e-02, 'w_down': 9.777554e-02, 'norm_final': 3.219778e+01}


def _to_microbatches(a, axis):
    t = _jnp.moveaxis(a, axis, 0)
    t = t.reshape((N_MICROBATCH, t.shape[0] // N_MICROBATCH) + t.shape[1:])
    return _jnp.moveaxis(t, 1, axis + 1)


def setup_inputs(seed: int = 0) -> dict:
    inp = _fwd_setup_inputs(seed)
    key = _jax.random.fold_in(_jax.random.key(seed), 7919)
    shape, _ = _output_shape()
    out = dict(inp)
    out["loss_target"] = _jax.random.normal(_jax.random.fold_in(key, 0), shape, _jnp.float32)
    for i, name in enumerate(TWIN_WEIGHTS):
        w = inp[name].astype(_jnp.float32)
        if MOMENT_SCALE is None:
            s = _jnp.sqrt(_jnp.mean(_jnp.square(w)) + 1e-30)
        else:
            s = MOMENT_SCALE[name]
        km, kv = _jax.random.split(_jax.random.fold_in(key, i + 1))
        out[name] = w
        out["m_" + name] = s * _jax.random.normal(km, w.shape, _jnp.float32)
        out["v_" + name] = (s * s) * _jax.random.uniform(kv, w.shape, _jnp.float32, 0.5, 1.5)
    if N_MICROBATCH > 1:
        for name, axis in PER_EXAMPLE_BATCH_AXIS.items():
            out[name] = _to_microbatches(out[name], axis)
    return {'x': out['x'], 'mem': out['mem'], 'norm_mix': out['norm_mix'], 'norm_mem': out['norm_mem'], 'norm_ffn': out['norm_ffn'], 'w_in_fox': out['w_in_fox'], 'b_forget': out['b_forget'], 'w_in_dil': out['w_in_dil'], 'w_mem_kv': out['w_mem_kv'], 'w_out': out['w_out'], 'w_up': out['w_up'], 'conv_w': out['conv_w'], 'conv_b': out['conv_b'], 'w_down': out['w_down'], 'norm_final': out['norm_final'], 'loss_target': out['loss_target'], 'm_norm_mix': out['m_norm_mix'], 'm_norm_mem': out['m_norm_mem'], 'm_norm_ffn': out['m_norm_ffn'], 'm_w_in_fox': out['m_w_in_fox'], 'm_b_forget': out['m_b_forget'], 'm_w_in_dil': out['m_w_in_dil'], 'm_w_mem_kv': out['m_w_mem_kv'], 'm_w_out': out['m_w_out'], 'm_w_up': out['m_w_up'], 'm_conv_w': out['m_conv_w'], 'm_conv_b': out['m_conv_b'], 'm_w_down': out['m_w_down'], 'm_norm_final': out['m_norm_final'], 'v_norm_mix': out['v_norm_mix'], 'v_norm_mem': out['v_norm_mem'], 'v_norm_ffn': out['v_norm_ffn'], 'v_w_in_fox': out['v_w_in_fox'], 'v_b_forget': out['v_b_forget'], 'v_w_in_dil': out['v_w_in_dil'], 'v_w_mem_kv': out['v_w_mem_kv'], 'v_w_out': out['v_w_out'], 'v_w_up': out['v_w_up'], 'v_conv_w': out['v_conv_w'], 'v_conv_b': out['v_conv_b'], 'v_w_down': out['v_w_down'], 'v_norm_final': out['v_norm_final']}


def _loss(weights, diff, rest, loss_target):
    with _jax.named_scope("forward"):
        args = {**rest, TWIN_DIFF_INPUT: diff, **{k: w.astype(_WEIGHT_DTYPES[k]) for k, w in weights.items()}}
        y = _forward(args)
    with _jax.named_scope("loss_head"):
        err = _jnp.square(y.astype(_jnp.float32) - loss_target)
        return 0.5 * _jnp.sum(_jnp.mean(err, axis=-1)) if err.ndim else 0.5 * err


def _adamw(w, g, m, v):
    m = ADAM_B1 * m + (1.0 - ADAM_B1) * g
    v = ADAM_B2 * v + (1.0 - ADAM_B2) * _jnp.square(g)
    m_hat = m / (1.0 - ADAM_B1 ** ADAM_STEP)
    v_hat = v / (1.0 - ADAM_B2 ** ADAM_STEP)
    delta = -ADAM_LR * (m_hat / (_jnp.sqrt(v_hat) + ADAM_EPS) + ADAM_WD * w)
    return delta, m, v


def reference(x, mem, norm_mix, norm_mem, norm_ffn, w_in_fox, b_forget, w_in_dil, w_mem_kv, w_out, w_up, conv_w, conv_b, w_down, norm_final, loss_target, m_norm_mix, m_norm_mem, m_norm_ffn, m_w_in_fox, m_b_forget, m_w_in_dil, m_w_mem_kv, m_w_out, m_w_up, m_conv_w, m_conv_b, m_w_down, m_norm_final, v_norm_mix, v_norm_mem, v_norm_ffn, v_w_in_fox, v_b_forget, v_w_in_dil, v_w_mem_kv, v_w_out, v_w_up, v_conv_w, v_conv_b, v_w_down, v_norm_final):
    given = dict(x=x, mem=mem, norm_mix=norm_mix, norm_mem=norm_mem, norm_ffn=norm_ffn, w_in_fox=w_in_fox, b_forget=b_forget, w_in_dil=w_in_dil, w_mem_kv=w_mem_kv, w_out=w_out, w_up=w_up, conv_w=conv_w, conv_b=conv_b, w_down=w_down, norm_final=norm_final, loss_target=loss_target, m_norm_mix=m_norm_mix, m_norm_mem=m_norm_mem, m_norm_ffn=m_norm_ffn, m_w_in_fox=m_w_in_fox, m_b_forget=m_b_forget, m_w_in_dil=m_w_in_dil, m_w_mem_kv=m_w_mem_kv, m_w_out=m_w_out, m_w_up=m_w_up, m_conv_w=m_conv_w, m_conv_b=m_conv_b, m_w_down=m_w_down, m_norm_final=m_norm_final, v_norm_mix=v_norm_mix, v_norm_mem=v_norm_mem, v_norm_ffn=v_norm_ffn, v_w_in_fox=v_w_in_fox, v_b_forget=v_b_forget, v_w_in_dil=v_w_in_dil, v_w_mem_kv=v_w_mem_kv, v_w_out=v_w_out, v_w_up=v_w_up, v_conv_w=v_conv_w, v_conv_b=v_conv_b, v_w_down=v_w_down, v_norm_final=v_norm_final)
    weights = {n: given[n] for n in TWIN_WEIGHTS}
    shared = {n: given[n] for n in SHARED_INPUTS}
    per_example = {n: given[n] for n in ['x', 'mem']}
    grad_fn = _jax.value_and_grad(_loss, argnums=(0, 1))

    def one_microbatch(ex, loss_target):
        ex = dict(ex)
        diff = ex.pop(TWIN_DIFF_INPUT)
        return grad_fn(weights, diff, {**shared, **ex}, loss_target)

    if N_MICROBATCH == 1:
        loss, (grad_w, grad_x) = one_microbatch(per_example, given["loss_target"])
    else:
        def body(carry, xs):
            loss_sum, grad_sum = carry
            l_k, (gw_k, gx_k) = one_microbatch(xs[0], xs[1])
            with _jax.named_scope("update"):
                return (loss_sum + l_k, _jax.tree.map(_jnp.add, grad_sum, gw_k)), gx_k

        init = (_jnp.zeros((), _jnp.float32), _jax.tree.map(_jnp.zeros_like, weights))
        (loss, grad_w), grad_x = _jax.lax.scan(body, init, (per_example, given["loss_target"]))
    with _jax.named_scope("update"):
        delta_w, new_m, new_v = {}, {}, {}
        for n in TWIN_WEIGHTS:
            delta_w[n], new_m[n], new_v[n] = _adamw(weights[n], grad_w[n], given["m_" + n], given["v_" + n])
    return (loss, grad_x, *[grad_w[n] for n in TWIN_WEIGHTS], *[delta_w[n] for n in TWIN_WEIGHTS],
            *[new_m[n] for n in TWIN_WEIGHTS], *[new_v[n] for n in TWIN_WEIGHTS])
```

```python
import functools
import math

import jax
import jax.numpy as jnp
from jax import lax
from jax.experimental import pallas as pl
from jax.experimental.pallas import tpu as pltpu

F32 = jnp.float32
BF16 = jnp.bfloat16

D_MODEL = 1024
HEAD_DIM = 64
N_MIX_HEADS = 12
N_MEM_HEADS = 4
D_MIX = N_MIX_HEADS * HEAD_DIM
D_MEMQ = N_MEM_HEADS * HEAD_DIM
D_FF = 2816
DEPTH = 4
FOX_IN = 3 * D_MIX + N_MIX_HEADS + D_MEMQ
DIL_IN = 3 * D_MIX + D_MEMQ
LANES = 128
FOX_P = DIL_IN + LANES
N_MIX_HP = D_MIX // LANES
N_MEM_HP = D_MEMQ // LANES
QM_COL = 3 * N_MIX_HP
F_COL = DIL_IN // LANES
DILATED_BRANCHES = ((128, 1), (512, 4), (2048, 16))
DIL_L = 128
ROPE_THETA = 10000.0
NORM_EPS = 1e-6
NEG = -1e30
SCALE = HEAD_DIM ** -0.5
N_DEV = 8

ADAM_LR = 0.001
ADAM_B1 = 0.9
ADAM_B2 = 0.999
ADAM_EPS = 1e-08
ADAM_WD = 0.01
ADAM_STEP = 10

VMEM_LIMIT = 56 * 1024 * 1024
PACK_W = 1024
PACK_ROW_ALIGN = 256

MESH = pl.DeviceIdType.MESH
NT = (((1,), (1,)), ((), ()))
NN = (((1,), (0,)), ((), ()))
TN = (((0,), (0,)), ((), ()))


def _params(*sem):
    return pltpu.CompilerParams(dimension_semantics=sem, vmem_limit_bytes=VMEM_LIMIT)


def _lane_lo(shape):
    return lax.broadcasted_iota(jnp.int32, shape, len(shape) - 1) < HEAD_DIM


def _pair(lo, a, b):
    return jnp.where(lo, a, b)


def _mm(a, b, mode, *, tm, tn, name, out_dtype=F32, res=None):
    if mode == "nn":
        (m, k), (k2, n) = a.shape, b.shape
    elif mode == "nt":
        (m, k), (n, k2) = a.shape, b.shape
    else:
        (k, m), (k2, n) = a.shape, b.shape
    assert k == k2 and m % tm == 0 and n % tn == 0, (name, a.shape, b.shape, tm, tn)
    a_spec = pl.BlockSpec((k, tm), lambda i, j: (0, i)) if mode == "tn" else pl.BlockSpec((tm, k), lambda i, j: (i, 0))
    b_spec = pl.BlockSpec((tn, k), lambda i, j: (j, 0)) if mode == "nt" else pl.BlockSpec((k, tn), lambda i, j: (0, j))
    dims = {"nn": NN, "nt": NT, "tn": TN}[mode]
    o_spec = pl.BlockSpec((tm, tn), lambda i, j: (i, j))

    def body(*refs):
        a_ref, b_ref = refs[0], refs[1]
        o_ref = refs[-1]
        acc = lax.dot_general(a_ref[...].astype(BF16), b_ref[...].astype(BF16), dims, preferred_element_type=F32)
        if res is not None:
            acc = acc + refs[2][...]
        o_ref[...] = acc.astype(o_ref.dtype)

    ins = [a, b] + ([res] if res is not None else [])
    specs = [a_spec, b_spec] + ([o_spec] if res is not None else [])
    return pl.pallas_call(body, out_shape=jax.ShapeDtypeStruct((m, n), out_dtype), grid=(m // tm, n // tn),
                          in_specs=specs, out_specs=o_spec, compiler_params=_params("parallel", "parallel"), name=name)(*ins)


def _rmsnorm_fwd(x, g, *, br, name):
    r, d = x.shape

    def body(x_ref, g_ref, o_ref):
        xf = x_ref[...]
        rs = lax.rsqrt(jnp.mean(xf * xf, axis=-1, keepdims=True) + NORM_EPS)
        o_ref[...] = (xf * rs * g_ref[...]).astype(BF16)

    return pl.pallas_call(body, out_shape=jax.ShapeDtypeStruct((r, d), BF16), grid=(r // br,),
                          in_specs=[pl.BlockSpec((br, d), lambda i: (i, 0)), pl.BlockSpec((1, d), lambda i: (0, 0))],
                          out_specs=pl.BlockSpec((br, d), lambda i: (i, 0)), compiler_params=_params("parallel"), name=name)(x, g)


def _rms_bwd_math(x, dy, g):
    d = x.shape[-1]
    rs = lax.rsqrt(jnp.mean(x * x, axis=-1, keepdims=True) + NORM_EPS)
    gy = dy * g
    proj = jnp.sum(x * gy, axis=-1, keepdims=True) * (1.0 / d)
    dx = rs * gy - x * (rs * rs * rs) * proj
    dg = jnp.sum(dy * (x * rs), axis=0, keepdims=True)
    return dx, dg


def _rmsnorm_bwd(x, dy, g, res, *, br, name):
    r, d = x.shape
    has_res = res is not None

    def body(*refs):
        x_ref, dy_ref, g_ref = refs[:3]
        dx_ref, dxb_ref, dg_ref = refs[-3:]
        dx, dg = _rms_bwd_math(x_ref[...], dy_ref[...], g_ref[...])
        if has_res:
            dx = dx + refs[3][...]
        dx_ref[...] = dx
        dxb_ref[...] = dx.astype(BF16)

        @pl.when(pl.program_id(0) == 0)
        def _():
            dg_ref[...] = jnp.zeros_like(dg_ref)

        dg_ref[0:1, :] += dg

    row = pl.BlockSpec((br, d), lambda i: (i, 0))
    ins = [x, dy, g] + ([res] if has_res else [])
    specs = [row, row, pl.BlockSpec((1, d), lambda i: (0, 0))] + ([row] if has_res else [])
    return pl.pallas_call(
        body, out_shape=(jax.ShapeDtypeStruct((r, d), F32), jax.ShapeDtypeStruct((r, d), BF16), jax.ShapeDtypeStruct((8, d), F32)),
        grid=(r // br,), in_specs=specs, out_specs=(row, row, pl.BlockSpec((8, d), lambda i: (0, 0))),
        compiler_params=_params("arbitrary"), name=name)(*ins)


def _loss_head(h, target, g, *, br, name):
    r, d = h.shape

    def body(x_ref, t_ref, g_ref, dx_ref, dxb_ref, dg_ref, loss_ref):
        x = x_ref[...]
        gg = g_ref[...]
        rs = lax.rsqrt(jnp.mean(x * x, axis=-1, keepdims=True) + NORM_EPS)
        err = x * rs * gg - t_ref[...]
        part = jnp.sum(jnp.sum(err * err, axis=1, keepdims=True), axis=0, keepdims=True) * (0.5 / d)
        dx, dg = _rms_bwd_math(x, err * (1.0 / d), gg)
        dx_ref[...] = dx
        dxb_ref[...] = dx.astype(BF16)

        @pl.when(pl.program_id(0) == 0)
        def _():
            dg_ref[...] = jnp.zeros_like(dg_ref)
            loss_ref[...] = jnp.zeros_like(loss_ref)

        dg_ref[0:1, :] += dg
        loss_ref[...] += jnp.broadcast_to(part, loss_ref.shape)

    row = pl.BlockSpec((br, d), lambda i: (i, 0))
    return pl.pallas_call(
        body, out_shape=(jax.ShapeDtypeStruct((r, d), F32), jax.ShapeDtypeStruct((r, d), BF16),
                         jax.ShapeDtypeStruct((8, d), F32), jax.ShapeDtypeStruct((8, LANES), F32)),
        grid=(r // br,), in_specs=[row, row, pl.BlockSpec((1, d), lambda i: (0, 0))],
        out_specs=(row, row, pl.BlockSpec((8, d), lambda i: (0, 0)), pl.BlockSpec((8, LANES), lambda i: (0, 0))),
        compiler_params=_params("arbitrary"), name=name)(h, target, g)


def _split3(x):
    hi = x.astype(BF16)
    r1 = x - hi.astype(F32)
    mid = r1.astype(BF16)
    lo = (r1 - mid.astype(F32)).astype(BF16)
    return hi, mid, lo


def _tri_sum(tri, x):
    hi, mid, lo = _split3(x)
    dot = lambda t: jnp.dot(tri, t, preferred_element_type=F32)
    return dot(hi) + dot(mid) + dot(lo)


def _forget_cumsum(proj, b_pad, *, name):
    s = proj.shape[0]
    blk = LANES

    def body(f_ref, b_ref, c_ref):
        ri = lax.broadcasted_iota(jnp.int32, (blk, blk), 0)
        ci = lax.broadcasted_iota(jnp.int32, (blk, blk), 1)
        tri = (ci <= ri).astype(BF16)
        bias = b_ref[...]

        def step(t, carry):
            rows = pl.ds(pl.multiple_of(t * blk, blk), blk)
            z = f_ref[rows, :] + bias
            lf = jnp.minimum(z, 0.0) - jnp.log(1.0 + jnp.exp(-jnp.abs(z)))
            cs = _tri_sum(tri, lf) + carry
            c_ref[rows, :] = cs
            return cs[blk - 1:blk, :]

        lax.fori_loop(0, s // blk, step, jnp.zeros((1, blk), F32))

    return pl.pallas_call(body, out_shape=jax.ShapeDtypeStruct((s, LANES), F32), grid=(1,),
                          in_specs=[pl.BlockSpec((s, LANES), lambda i: (0, F_COL)), pl.BlockSpec((1, LANES), lambda i: (0, 0))],
                          out_specs=pl.BlockSpec((s, LANES), lambda i: (0, 0)), compiler_params=_params("arbitrary"), name=name)(proj, b_pad)


def _forget_cumsum_bwd(proj, b_pad, dcq, dck, *, name):
    s = proj.shape[0]
    blk = LANES
    nblk = s // blk

    def body(f_ref, b_ref, dcq_ref, dck_ref, dz_ref, db_ref):
        ri = lax.broadcasted_iota(jnp.int32, (blk, blk), 0)
        ci = lax.broadcasted_iota(jnp.int32, (blk, blk), 1)
        triu = (ci >= ri).astype(BF16)
        bias = b_ref[...]

        def step(t, carry):
            tail, dbs = carry
            rows = pl.ds(pl.multiple_of((nblk - 1 - t) * blk, blk), blk)
            dc = dcq_ref[rows, :] - dck_ref[rows, :]
            dlf = _tri_sum(triu, dc) + tail
            z = f_ref[rows, :] + bias
            e = jnp.exp(-jnp.abs(z))
            sig_neg = jnp.where(z >= 0.0, e, 1.0) / (1.0 + e)
            dz = dlf * sig_neg
            dz_ref[rows, :] = dz.astype(BF16)
            return dlf[0:1, :], dbs + jnp.sum(dz, axis=0, keepdims=True)

        _, dbs = lax.fori_loop(0, nblk, step, (jnp.zeros((1, blk), F32), jnp.zeros((1, blk), F32)))
        db_ref[...] = jnp.broadcast_to(dbs, db_ref.shape)

    full = pl.BlockSpec((s, LANES), lambda i: (0, 0))
    return pl.pallas_call(body, out_shape=(jax.ShapeDtypeStruct((s, LANES), BF16), jax.ShapeDtypeStruct((8, LANES), F32)), grid=(1,),
                          in_specs=[pl.BlockSpec((s, LANES), lambda i: (0, F_COL)), pl.BlockSpec((1, LANES), lambda i: (0, 0)), full, full],
                          out_specs=(full, pl.BlockSpec((8, LANES), lambda i: (0, 0))), compiler_params=_params("arbitrary"), name=name)(proj, b_pad, dcq, dck)


def _attn_fwd(q_arr, kv_arr, cq6, ck6, *, q_col, k_col, v_col, n_hp, causal, bq, bk, name):
    s = q_arr.shape[0]
    skv = kv_arr.shape[0]
    bias = cq6 is not None
    nq = s // bq

    def body(*refs):
        q_ref, k_ref, v_ref = refs[:3]
        o_ref, lse_ref = refs[-2:]
        i = pl.program_id(1)
        lo = _lane_lo((bq, LANES))
        q = q_ref[...]
        q0 = jnp.where(lo, q, 0.0).astype(BF16)
        q1 = jnp.where(lo, 0.0, q).astype(BF16)
        if bias:
            cq = refs[3][0]
            cq0, cq1 = cq[:, 0:1], cq[:, HEAD_DIM:HEAD_DIM + 1]
            ck_ref = refs[4]
        if causal:
            rowpos = i * bq + lax.broadcasted_iota(jnp.int32, (bq, bk), 0)
            colid = lax.broadcasted_iota(jnp.int32, (bq, bk), 1)

        def step(j, carry):
            m0, l0, a0, m1, l1, a1 = carry
            ks = pl.ds(pl.multiple_of(j * bk, bk), bk)
            k = k_ref[ks, :].astype(BF16)
            v = v_ref[ks, :].astype(BF16)
            s0 = lax.dot_general(q0, k, NT, preferred_element_type=F32) * SCALE
            s1 = lax.dot_general(q1, k, NT, preferred_element_type=F32) * SCALE
            if bias:
                s0 = s0 + cq0 - ck_ref[0, 0:1, ks]
                s1 = s1 + cq1 - ck_ref[0, 1:2, ks]
            if causal:
                ok = (j * bk + colid) <= rowpos
                s0 = jnp.where(ok, s0, NEG)
                s1 = jnp.where(ok, s1, NEG)

            def upd(sc, m, l, a):
                mn = jnp.maximum(m, jnp.max(sc, axis=1, keepdims=True))
                p = jnp.exp(sc - mn)
                al = jnp.exp(m - mn)
                return mn, al * l + jnp.sum(p, axis=1, keepdims=True), al * a + jnp.dot(p.astype(BF16), v, preferred_element_type=F32)

            m0, l0, a0 = upd(s0, m0, l0, a0)
            m1, l1, a1 = upd(s1, m1, l1, a1)
            return m0, l0, a0, m1, l1, a1

        nk = ((i + 1) * bq + bk - 1) // bk if causal else skv // bk
        col = lambda v_: jnp.full((bq, 1), v_, F32)
        init = (col(NEG), col(0.0), jnp.zeros((bq, LANES), F32), col(NEG), col(0.0), jnp.zeros((bq, LANES), F32))
        m0, l0, a0, m1, l1, a1 = lax.fori_loop(0, nk, step, init)
        o_ref[...] = _pair(lo, a0 / l0, a1 / l1)
        lse_ref[0] = _pair(lo, m0 + jnp.log(l0), m1 + jnp.log(l1))

    specs = [pl.BlockSpec((bq, LANES), lambda h, i: (i, q_col + h)),
             pl.BlockSpec((skv, LANES), lambda h, i: (0, k_col + h)),
             pl.BlockSpec((skv, LANES), lambda h, i: (0, v_col + h))]
    ins = [q_arr, kv_arr, kv_arr]
    if bias:
        specs += [pl.BlockSpec((1, bq, LANES), lambda h, i: (h, i, 0)), pl.BlockSpec((1, 8, skv), lambda h, i: (h, 0, 0))]
        ins += [cq6, ck6]
    return pl.pallas_call(
        body, out_shape=(jax.ShapeDtypeStruct((s, n_hp * LANES), F32), jax.ShapeDtypeStruct((n_hp, s, LANES), F32)),
        grid=(n_hp, nq), in_specs=specs,
        out_specs=(pl.BlockSpec((bq, LANES), lambda h, i: (i, h)), pl.BlockSpec((1, bq, LANES), lambda h, i: (h, i, 0))),
        compiler_params=_params("parallel", "parallel"), name=name)(*ins)


def _attn_bwd(q_arr, kv_arr, o_arr, do_arr, lse, cq6, ck6, *, q_col, k_col, v_col, o_col, n_hp, causal, bq, bk, name):
    s = q_arr.shape[0]
    skv = kv_arr.shape[0]
    bias = cq6 is not None
    nq = s // bq

    def body(*refs):
        q_ref, k_ref, v_ref, o_ref, do_ref, lse_ref = refs[:6]
        if bias:
            cq_ref, ck_ref = refs[6:8]
            dq_ref, dk_ref, dv_ref, dcq_ref, dck_ref = refs[-5:]
        else:
            dq_ref, dk_ref, dv_ref = refs[-3:]
        j = pl.program_id(1)
        lo_q = _lane_lo((bq, LANES))
        lo_k = _lane_lo((bk, LANES))
        k = k_ref[...]
        v = v_ref[...].astype(BF16)
        kb = k.astype(BF16)
        k0 = jnp.where(lo_k, k, 0.0).astype(BF16)
        k1 = jnp.where(lo_k, 0.0, k).astype(BF16)
        if bias:
            ck0 = ck_ref[0, 0:1, :]
            ck1 = ck_ref[0, 1:2, :]
        if causal:
            colpos = j * bk + lax.broadcasted_iota(jnp.int32, (bq, bk), 1)
            rowid = lax.broadcasted_iota(jnp.int32, (bq, bk), 0)

        @pl.when(j == 0)
        def _():
            dq_ref[...] = jnp.zeros_like(dq_ref)
            if bias:
                dcq_ref[...] = jnp.zeros_like(dcq_ref)

        def step(i, carry):
            dk_acc, dv_acc, cs0, cs1 = carry
            qs = pl.ds(pl.multiple_of(i * bq, bq), bq)
            q = q_ref[qs, :]
            do = do_ref[qs, :]
            dd = do * o_ref[qs, :]
            lse_i = lse_ref[0, qs, :]
            q0 = jnp.where(lo_q, q, 0.0).astype(BF16)
            q1 = jnp.where(lo_q, 0.0, q).astype(BF16)
            do0 = jnp.where(lo_q, do, 0.0).astype(BF16)
            do1 = jnp.where(lo_q, 0.0, do).astype(BF16)
            d0 = jnp.sum(jnp.where(lo_q, dd, 0.0), axis=1, keepdims=True)
            d1 = jnp.sum(jnp.where(lo_q, 0.0, dd), axis=1, keepdims=True)
            s0 = lax.dot_general(q0, kb, NT, preferred_element_type=F32) * SCALE
            s1 = lax.dot_general(q1, kb, NT, preferred_element_type=F32) * SCALE
            if bias:
                cq = cq_ref[0, qs, :]
                s0 = s0 + cq[:, 0:1] - ck0
                s1 = s1 + cq[:, HEAD_DIM:HEAD_DIM + 1] - ck1
            if causal:
                ok = colpos <= (i * bq + rowid)
                s0 = jnp.where(ok, s0, NEG)
                s1 = jnp.where(ok, s1, NEG)
            p0 = jnp.exp(s0 - lse_i[:, 0:1])
            p1 = jnp.exp(s1 - lse_i[:, HEAD_DIM:HEAD_DIM + 1])
            dp0 = lax.dot_general(do0, v, NT, preferred_element_type=F32)
            dp1 = lax.dot_general(do1, v, NT, preferred_element_type=F32)
            ds0 = p0 * (dp0 - d0)
            ds1 = p1 * (dp1 - d1)
            ds0b, ds1b = ds0.astype(BF16), ds1.astype(BF16)
            dv_acc = dv_acc + lax.dot_general(p0.astype(BF16), do0, TN, preferred_element_type=F32) \
                + lax.dot_general(p1.astype(BF16), do1, TN, preferred_element_type=F32)
            dk_acc = dk_acc + lax.dot_general(ds0b, q0, TN, preferred_element_type=F32) \
                + lax.dot_general(ds1b, q1, TN, preferred_element_type=F32)
            dq_ref[qs, :] += (jnp.dot(ds0b, k0, preferred_element_type=F32) + jnp.dot(ds1b, k1, preferred_element_type=F32)) * SCALE
            if bias:
                dcq_ref[0, qs, :] += _pair(lo_q, jnp.sum(ds0, axis=1, keepdims=True), jnp.sum(ds1, axis=1, keepdims=True))
                cs0 = cs0 + jnp.sum(ds0, axis=0, keepdims=True)
                cs1 = cs1 + jnp.sum(ds1, axis=0, keepdims=True)
            return dk_acc, dv_acc, cs0, cs1

        i0 = (j * bk) // bq if causal else 0
        init = (jnp.zeros((bk, LANES), F32), jnp.zeros((bk, LANES), F32), jnp.zeros((1, bk), F32), jnp.zeros((1, bk), F32))
        dk_acc, dv_acc, cs0, cs1 = lax.fori_loop(i0, nq, step, init)
        dk_ref[...] = dk_acc * SCALE
        dv_ref[...] = dv_acc
        if bias:
            r8 = lax.broadcasted_iota(jnp.int32, (8, bk), 0)
            dck_ref[0] = jnp.where(r8 == 0, cs0, jnp.where(r8 == 1, cs1, 0.0))

    full_q = lambda c: pl.BlockSpec((s, LANES), lambda h, j: (0, c + h))
    specs = [full_q(q_col),
             pl.BlockSpec((bk, LANES), lambda h, j: (j, k_col + h)),
             pl.BlockSpec((bk, LANES), lambda h, j: (j, v_col + h)),
             full_q(0), full_q(o_col),
             pl.BlockSpec((1, s, LANES), lambda h, j: (h, 0, 0))]
    ins = [q_arr, kv_arr, kv_arr, o_arr, do_arr, lse]
    out_shape = [jax.ShapeDtypeStruct((s, n_hp * LANES), F32), jax.ShapeDtypeStruct((skv, n_hp * LANES), F32),
                 jax.ShapeDtypeStruct((skv, n_hp * LANES), F32)]
    out_specs = [full_q(0), pl.BlockSpec((bk, LANES), lambda h, j: (j, h)), pl.BlockSpec((bk, LANES), lambda h, j: (j, h))]
    if bias:
        specs += [pl.BlockSpec((1, s, LANES), lambda h, j: (h, 0, 0)), pl.BlockSpec((1, 8, bk), lambda h, j: (h, 0, j))]
        ins += [cq6, ck6]
        out_shape += [jax.ShapeDtypeStruct((n_hp, s, LANES), F32), jax.ShapeDtypeStruct((n_hp, 8, skv), F32)]
        out_specs += [pl.BlockSpec((1, s, LANES), lambda h, j: (h, 0, 0)), pl.BlockSpec((1, 8, bk), lambda h, j: (h, 0, j))]
    return pl.pallas_call(body, out_shape=tuple(out_shape), grid=(n_hp, skv // bk), in_specs=specs, out_specs=tuple(out_specs),
                          compiler_params=_params("parallel", "arbitrary"), name=name)(*ins)


def _rope_tables(s):
    inv = 1.0 / (ROPE_THETA ** (jnp.arange(0, HEAD_DIM, 2, dtype=F32) / HEAD_DIM))
    ang = jnp.arange(s, dtype=F32)[:, None] * inv[None, :]
    cos, sin = jnp.cos(ang), jnp.sin(ang)
    return jnp.tile(cos, (1, 4)), jnp.concatenate([-sin, sin, -sin, sin], axis=1)


def _rope(x_arr, cos_t, sin_t, *, n_cols, out_dtype, br, name):
    s = x_arr.shape[0]

    def body(x_ref, c_ref, s_ref, o_ref):
        x = x_ref[...].astype(F32)
        first = (lax.broadcasted_iota(jnp.int32, x.shape, 1) % HEAD_DIM) < (HEAD_DIM // 2)
        swapped = jnp.where(first, pltpu.roll(x, LANES - HEAD_DIM // 2, 1), pltpu.roll(x, HEAD_DIM // 2, 1))
        o_ref[...] = (x * c_ref[...] + swapped * s_ref[...]).astype(o_ref.dtype)

    tab = pl.BlockSpec((br, LANES), lambda i, j: (i, 0))
    blk = pl.BlockSpec((br, LANES), lambda i, j: (i, j))
    return pl.pallas_call(body, out_shape=jax.ShapeDtypeStruct((s, n_cols * LANES), out_dtype), grid=(s // br, n_cols),
                          in_specs=[blk, tab, tab], out_specs=blk, compiler_params=_params("parallel", "parallel"), name=name)(x_arr, cos_t, sin_t)


def _dil_masks(n):
    a = lax.broadcasted_iota(jnp.int32, (DIL_L, DIL_L), 0)
    c = lax.broadcasted_iota(jnp.int32, (DIL_L, DIL_L), 1)
    return (c >= a) & (n > 0), c <= a


def _dil_scores(q0, q1, kp, kc, n):
    ok_p, ok_c = _dil_masks(n)
    sc = lambda q, k, ok: jnp.where(ok, lax.dot_general(q, k, NT, preferred_element_type=F32) * SCALE, NEG)
    return sc(q0, kp, ok_p), sc(q0, kc, ok_c), sc(q1, kp, ok_p), sc(q1, kc, ok_c)


def _dil_branch_fwd(qk_r, proj, dil, *, name):
    s = qk_r.shape[0]
    n_sub = s // dil
    nb = n_sub // DIL_L
    qk2 = qk_r.reshape(n_sub, dil * 2 * D_MIX)
    pr2 = proj.reshape(n_sub, dil * DIL_IN)
    qk_cols = 2 * N_MIX_HP
    pr_cols = DIL_IN // LANES

    def body(q_ref, kp_ref, kc_ref, vp_ref, vc_ref, o_ref, lse_ref):
        n = pl.program_id(2)
        lo = _lane_lo((DIL_L, LANES))
        q = q_ref[...]
        zero = jnp.zeros_like(q)
        q0, q1 = jnp.where(lo, q, zero), jnp.where(lo, zero, q)
        s0p, s0c, s1p, s1c = _dil_scores(q0, q1, kp_ref[...], kc_ref[...], n)
        vp = vp_ref[...].astype(BF16)
        vc = vc_ref[...].astype(BF16)

        def head(sp, sc):
            m = jnp.maximum(jnp.max(sp, axis=1, keepdims=True), jnp.max(sc, axis=1, keepdims=True))
            ep, ec = jnp.exp(sp - m), jnp.exp(sc - m)
            den = jnp.sum(ep, axis=1, keepdims=True) + jnp.sum(ec, axis=1, keepdims=True)
            inv = 1.0 / den
            o = jnp.dot((ep * inv).astype(BF16), vp, preferred_element_type=F32) + jnp.dot((ec * inv).astype(BF16), vc, preferred_element_type=F32)
            return o, m + jnp.log(den)

        o0, l0 = head(s0p, s0c)
        o1, l1 = head(s1p, s1c)
        o_ref[...] = _pair(lo, o0, o1)
        lse_ref[...] = _pair(lo, l0, l1)

    blk = lambda f: pl.BlockSpec((DIL_L, LANES), f)
    prev = lambda n: jnp.maximum(n - 1, 0)
    specs = [blk(lambda r, h, n: (n, r * qk_cols + h)),
             blk(lambda r, h, n: (prev(n), r * qk_cols + N_MIX_HP + h)),
             blk(lambda r, h, n: (n, r * qk_cols + N_MIX_HP + h)),
             blk(lambda r, h, n: (prev(n), r * pr_cols + 2 * N_MIX_HP + h)),
             blk(lambda r, h, n: (n, r * pr_cols + 2 * N_MIX_HP + h))]
    out = blk(lambda r, h, n: (n, r * N_MIX_HP + h))
    o, lse = pl.pallas_call(
        body, out_shape=(jax.ShapeDtypeStruct((n_sub, dil * D_MIX), F32), jax.ShapeDtypeStruct((n_sub, dil * D_MIX), F32)),
        grid=(dil, N_MIX_HP, nb), in_specs=specs, out_specs=(out, out),
        compiler_params=_params("parallel", "parallel", "parallel"), name=name)(qk2, qk2, qk2, pr2, pr2)
    return o.reshape(s, D_MIX), lse.reshape(s, D_MIX)


def _dil_merge(outs, lses, *, br, name):
    s = outs[0].shape[0]

    def body(o1, o2, o3, l1, l2, l3, mix_ref, tot_ref):
        a, b, c = l1[...], l2[...], l3[...]
        m = jnp.maximum(jnp.maximum(a, b), c)
        ea, eb, ec = jnp.exp(a - m), jnp.exp(b - m), jnp.exp(c - m)
        den = ea + eb + ec
        inv = 1.0 / den
        mix_ref[...] = (ea * inv) * o1[...] + (eb * inv) * o2[...] + (ec * inv) * o3[...]
        tot_ref[...] = m + jnp.log(den)

    blk = pl.BlockSpec((br, D_MIX), lambda i: (i, 0))
    return pl.pallas_call(body, out_shape=(jax.ShapeDtypeStruct((s, D_MIX), F32), jax.ShapeDtypeStruct((s, D_MIX), F32)),
                          grid=(s // br,), in_specs=[blk] * 6, out_specs=(blk, blk), compiler_params=_params("parallel"), name=name)(*outs, *lses)


def _dil_branch_bwd(qk_r, proj, mix, dheads, lse_g, lse_tot, dil, *, name):
    s = qk_r.shape[0]
    n_sub = s // dil
    nb = n_sub // DIL_L
    qk2 = qk_r.reshape(n_sub, dil * 2 * D_MIX)
    pr2 = proj.reshape(n_sub, dil * DIL_IN)
    mix2 = mix.reshape(n_sub, dil * D_MIX)
    dh2 = dheads.reshape(n_sub, dil * D_MODEL)
    lg2 = lse_g.reshape(n_sub, dil * D_MIX)
    lt2 = lse_tot.reshape(n_sub, dil * D_MIX)
    qk_cols = 2 * N_MIX_HP
    pr_cols = DIL_IN // LANES
    dh_cols = D_MODEL // LANES

    def body(q_ref, kp_ref, kc_ref, vp_ref, vc_ref, mix_ref, dm_ref, lg_ref, lt_ref, dq_ref, dk_ref, dv_ref, ck_ref, cv_ref):
        n = pl.program_id(2)

        @pl.when(n == 0)
        def _():
            ck_ref[...] = jnp.zeros_like(ck_ref)
            cv_ref[...] = jnp.zeros_like(cv_ref)

        @pl.when(n < nb)
        def _():
            lo = _lane_lo((DIL_L, LANES))
            q = q_ref[...]
            zero = jnp.zeros_like(q)
            q0, q1 = jnp.where(lo, q, zero), jnp.where(lo, zero, q)
            kp, kc = kp_ref[...], kc_ref[...]
            s0p, s0c, s1p, s1c = _dil_scores(q0, q1, kp, kc, n)
            vp = vp_ref[...].astype(BF16)
            vc = vc_ref[...].astype(BF16)
            lg, lt = lg_ref[...], lt_ref[...]
            w = jnp.exp(lg - lt)
            dm = dm_ref[...]
            dd = dm * mix_ref[...]
            dog = w * dm
            do0 = jnp.where(lo, dog, 0.0).astype(BF16)
            do1 = jnp.where(lo, 0.0, dog).astype(BF16)
            kp0, kp1 = jnp.where(lo, kp, zero), jnp.where(lo, zero, kp)
            kc0, kc1 = jnp.where(lo, kc, zero), jnp.where(lo, zero, kc)

            def head(sp, sc, lcol, do_h, dsum):
                lse_h = lg[:, lcol:lcol + 1]
                wd = w[:, lcol:lcol + 1] * dsum
                pp, pc = jnp.exp(sp - lse_h), jnp.exp(sc - lse_h)
                dsp = pp * (lax.dot_general(do_h, vp, NT, preferred_element_type=F32) - wd)
                dsc = pc * (lax.dot_general(do_h, vc, NT, preferred_element_type=F32) - wd)
                return pp.astype(BF16), pc.astype(BF16), dsp.astype(BF16), dsc.astype(BF16)

            d0 = jnp.sum(jnp.where(lo, dd, 0.0), axis=1, keepdims=True)
            d1 = jnp.sum(jnp.where(lo, 0.0, dd), axis=1, keepdims=True)
            p0p, p0c, ds0p, ds0c = head(s0p, s0c, 0, do0, d0)
            p1p, p1c, ds1p, ds1c = head(s1p, s1c, HEAD_DIM, do1, d1)
            dot = lambda a, b: jnp.dot(a, b, preferred_element_type=F32)
            dott = lambda a, b: lax.dot_general(a, b, TN, preferred_element_type=F32)
            dq_ref[...] = (dot(ds0p, kp0) + dot(ds0c, kc0) + dot(ds1p, kp1) + dot(ds1c, kc1)) * SCALE
            dk_ref[...] = ck_ref[...] + (dott(ds0p, q0) + dott(ds1p, q1)) * SCALE
            dv_ref[...] = cv_ref[...] + dott(p0p, do0) + dott(p1p, do1)
            ck_ref[...] = (dott(ds0c, q0) + dott(ds1c, q1)) * SCALE
            cv_ref[...] = dott(p0c, do0) + dott(p1c, do1)

        @pl.when(n == nb)
        def _():
            dk_ref[...] = ck_ref[...]
            dv_ref[...] = cv_ref[...]

    blk = lambda f: pl.BlockSpec((DIL_L, LANES), f)
    cur = lambda n: jnp.minimum(n, nb - 1)
    prev = lambda n: jnp.maximum(jnp.minimum(n, nb - 1) - 1, 0)
    late = lambda n: jnp.maximum(n - 1, 0)
    specs = [blk(lambda r, h, n: (cur(n), r * qk_cols + h)),
             blk(lambda r, h, n: (prev(n), r * qk_cols + N_MIX_HP + h)),
             blk(lambda r, h, n: (cur(n), r * qk_cols + N_MIX_HP + h)),
             blk(lambda r, h, n: (prev(n), r * pr_cols + 2 * N_MIX_HP + h)),
             blk(lambda r, h, n: (cur(n), r * pr_cols + 2 * N_MIX_HP + h)),
             blk(lambda r, h, n: (cur(n), r * N_MIX_HP + h)),
             blk(lambda r, h, n: (cur(n), r * dh_cols + h)),
             blk(lambda r, h, n: (cur(n), r * N_MIX_HP + h)),
             blk(lambda r, h, n: (cur(n), r * N_MIX_HP + h))]
    o_cur = blk(lambda r, h, n: (cur(n), r * N_MIX_HP + h))
    o_late = blk(lambda r, h, n: (late(n), r * N_MIX_HP + h))
    shp = jax.ShapeDtypeStruct((n_sub, dil * D_MIX), F32)
    dq, dk, dv = pl.pallas_call(
        body, out_shape=(shp, shp, shp), grid=(dil, N_MIX_HP, nb + 1), in_specs=specs, out_specs=(o_cur, o_late, o_late),
        scratch_shapes=[pltpu.VMEM((DIL_L, LANES), F32), pltpu.VMEM((DIL_L, LANES), F32)],
        compiler_params=_params("parallel", "parallel", "arbitrary"), name=name)(qk2, qk2, qk2, pr2, pr2, mix2, dh2, lg2, lt2)
    return dq.reshape(s, D_MIX), dk.reshape(s, D_MIX), dv.reshape(s, D_MIX)


def _add3(a, b, c, *, br, out_dtype, name):
    s, d = a[0].shape if isinstance(a, tuple) else a.shape

    def body(a_ref, b_ref, c_ref, o_ref):
        o_ref[...] = (a_ref[...] + b_ref[...] + c_ref[...]).astype(o_ref.dtype)

    blk = pl.BlockSpec((br, d), lambda i: (i, 0))
    return pl.pallas_call(body, out_shape=jax.ShapeDtypeStruct((s, d), out_dtype), grid=(s // br,), in_specs=[blk] * 3, out_specs=blk,
                          compiler_params=_params("parallel"), name=name)(a, b, c)


CONV_BR = 256
CONV_BC = D_FF // 2
CONV_NJ = D_FF // CONV_BC
HALO = 8


def _shift_down(x, halo, k):
    row = lax.broadcasted_iota(jnp.int32, x.shape, 0)
    y = pltpu.roll(x, k, 0)
    for r in range(k):
        y = jnp.where(row == r, halo[HALO - k + r:HALO - k + r + 1, :], y)
    return y


def _shift_up(x, halo, k):
    n = x.shape[0]
    row = lax.broadcasted_iota(jnp.int32, x.shape, 0)
    y = pltpu.roll(x, n - k, 0)
    for r in range(k):
        y = jnp.where(row == n - k + r, halo[r:r + 1, :], y)
    return y


def _conv_vals(u, halo, w, b):
    s1 = _shift_down(u, halo, 1)
    s2 = _shift_down(u, halo, 2)
    return b + w[0:1, :] * s2 + w[1:2, :] * s1 + w[2:3, :] * u, s1, s2


def _conv_in_specs(order):
    rc = (lambda i, j: (i, j)) if order == "rc" else (lambda j, i: (i, j))
    per = CONV_BR // HALO
    main = lambda off: pl.BlockSpec((CONV_BR, CONV_BC), lambda *g: (rc(*g)[0], off + rc(*g)[1]))
    halo = lambda off: pl.BlockSpec((HALO, CONV_BC), lambda *g: (jnp.maximum(rc(*g)[0] * per - 1, 0), off + rc(*g)[1]))
    wspec = lambda off: pl.BlockSpec((3, CONV_BC), lambda *g: (0, off + rc(*g)[1]))
    bspec = lambda off: pl.BlockSpec((1, CONV_BC), lambda *g: (0, off + rc(*g)[1]))
    return [main(0), halo(0), main(CONV_NJ), halo(CONV_NJ), wspec(0), wspec(CONV_NJ), bspec(0), bspec(CONV_NJ)]


def _conv_fwd(u, cw, cb, *, name):
    s = u.shape[0]

    def body(uv_ref, hv_ref, ug_ref, hg_ref, wv_ref, wg_ref, bv_ref, bg_ref, o_ref):
        first = pl.program_id(0) == 0
        hv = jnp.where(first, 0.0, hv_ref[...])
        hg = jnp.where(first, 0.0, hg_ref[...])
        val, _, _ = _conv_vals(uv_ref[...], hv, wv_ref[...], bv_ref[...])
        gate, _, _ = _conv_vals(ug_ref[...], hg, wg_ref[...], bg_ref[...])
        o_ref[...] = (gate / (1.0 + jnp.exp(-gate)) * val).astype(BF16)

    return pl.pallas_call(body, out_shape=jax.ShapeDtypeStruct((s, D_FF), BF16), grid=(s // CONV_BR, CONV_NJ),
                          in_specs=_conv_in_specs("rc"), out_specs=pl.BlockSpec((CONV_BR, CONV_BC), lambda i, j: (i, j)),
                          compiler_params=_params("parallel", "parallel"), name=name)(u, u, u, u, cw, cw, cb, cb)


def _conv_bwd_dc(u, cw, cb, da, *, name):
    s = u.shape[0]

    def body(uv_ref, hv_ref, ug_ref, hg_ref, wv_ref, wg_ref, bv_ref, bg_ref, da_ref, dc_ref, dwb_ref):
        first = pl.program_id(1) == 0
        hv = jnp.where(first, 0.0, hv_ref[...])
        hg = jnp.where(first, 0.0, hg_ref[...])
        uv, ug = uv_ref[...], ug_ref[...]
        val, v1, v2 = _conv_vals(uv, hv, wv_ref[...], bv_ref[...])
        gate, g1, g2 = _conv_vals(ug, hg, wg_ref[...], bg_ref[...])
        da = da_ref[...]
        sg = 1.0 / (1.0 + jnp.exp(-gate))
        dval = da * (gate * sg)
        dgate = da * val * (sg * (1.0 + gate * (1.0 - sg)))
        dc_ref[0] = dval
        dc_ref[1] = dgate

        @pl.when(first)
        def _():
            dwb_ref[...] = jnp.zeros_like(dwb_ref)

        cs = lambda t: jnp.sum(t, axis=0, keepdims=True)
        r8 = lax.broadcasted_iota(jnp.int32, (8, CONV_BC), 0)
        rows4 = lambda a, b, c, d: jnp.where(r8 == 0, a, jnp.where(r8 == 1, b, jnp.where(r8 == 2, c, jnp.where(r8 == 3, d, 0.0))))
        dwb_ref[0] += rows4(cs(dval * v2), cs(dval * v1), cs(dval * uv), cs(dval))
        dwb_ref[1] += rows4(cs(dgate * g2), cs(dgate * g1), cs(dgate * ug), cs(dgate))

    specs = _conv_in_specs("cr") + [pl.BlockSpec((CONV_BR, CONV_BC), lambda j, i: (i, j))]
    return pl.pallas_call(
        body, out_shape=(jax.ShapeDtypeStruct((2, s, D_FF), F32), jax.ShapeDtypeStruct((2, 8, D_FF), F32)),
        grid=(CONV_NJ, s // CONV_BR), in_specs=specs,
        out_specs=(pl.BlockSpec((2, CONV_BR, CONV_BC), lambda j, i: (0, i, j)), pl.BlockSpec((2, 8, CONV_BC), lambda j, i: (0, 0, j))),
        compiler_params=_params("parallel", "arbitrary"), name=name)(u, u, u, u, cw, cw, cb, cb, da)


def _conv_bwd_du(dc, cw, *, name):
    s = dc.shape[1]
    nrow = s // CONV_BR
    per = CONV_BR // HALO

    def body(dc_ref, h_ref, w_ref, o_ref):
        last = pl.program_id(1) == nrow - 1
        h = jnp.where(last, 0.0, h_ref[0])
        x = dc_ref[0]
        w = w_ref[...]
        o_ref[...] = (w[2:3, :] * x + w[1:2, :] * _shift_up(x, h, 1) + w[0:1, :] * _shift_up(x, h, 2)).astype(BF16)

    return pl.pallas_call(
        body, out_shape=jax.ShapeDtypeStruct((s, 2 * D_FF), BF16), grid=(2, nrow, CONV_NJ),
        in_specs=[pl.BlockSpec((1, CONV_BR, CONV_BC), lambda p, i, j: (p, i, j)),
                  pl.BlockSpec((1, HALO, CONV_BC), lambda p, i, j: (p, jnp.minimum((i + 1) * per, nrow * per - 1), j)),
                  pl.BlockSpec((3, CONV_BC), lambda p, i, j: (0, p * CONV_NJ + j))],
        out_specs=pl.BlockSpec((CONV_BR, CONV_BC), lambda p, i, j: (i, p * CONV_NJ + j)),
        compiler_params=_params("parallel", "parallel", "parallel"), name=name)(dc, dc, cw)


def _all_gather(x, *, name):
    def body(x_ref, out_ref, send_sems, recv_sems, local_sem):
        mx, my, mc = lax.axis_index("x"), lax.axis_index("y"), lax.axis_index("c")
        me, sibling = (mx, my, mc), (mx, my, 1 - mc)
        chips = [(1 - mx, my), (mx, 1 - my), (1 - mx, 1 - my)]

        def slot(px, py, pc):
            return out_ref.at[4 * px + 2 * py + pc]

        def copy(k, block, to, src=None):
            return pltpu.make_async_remote_copy(src_ref=slot(*block) if src is None else src, dst_ref=slot(*block),
                                                send_sem=send_sems.at[k], recv_sem=recv_sems.at[k], device_id=to, device_id_type=MESH)

        mine = pltpu.make_async_copy(x_ref, slot(*me), local_sem)
        mine.start()
        first = [copy(0, me, sibling, src=x_ref)] + [copy(1 + j, me, (*chip, mc), src=x_ref) for j, chip in enumerate(chips)]
        for cp in first:
            cp.start()
        passed = [copy(4 + j, (*chip, mc), sibling) for j, chip in enumerate(chips)]
        for j, chip in enumerate(chips):
            copy(1 + j, (*chip, mc), me).wait_recv()
            passed[j].start()
        copy(0, sibling, me).wait_recv()
        for j, chip in enumerate(chips):
            copy(4 + j, (*chip, 1 - mc), me).wait_recv()
        for cp in first + passed:
            cp.wait_send()
        mine.wait()

    return pl.pallas_call(
        body, out_shape=jax.ShapeDtypeStruct((N_DEV,) + x.shape, x.dtype),
        in_specs=[pl.BlockSpec(memory_space=pl.ANY)], out_specs=pl.BlockSpec(memory_space=pl.ANY),
        scratch_shapes=[pltpu.SemaphoreType.DMA((7,)), pltpu.SemaphoreType.DMA((7,)), pltpu.SemaphoreType.DMA],
        name=name)(x)


def _all_to_all(g, *, name):
    def body(g_ref, out_ref, send_sems, recv_sems, local_sem):
        mx, my, mc = lax.axis_index("x"), lax.axis_index("y"), lax.axis_index("c")
        me = 4 * mx + 2 * my + mc
        mine = pltpu.make_async_copy(g_ref.at[me], out_ref.at[me], local_sem)
        mine.start()
        flips = [(fx, fy, fc) for fx in (0, 1) for fy in (0, 1) for fc in (0, 1)][1:]
        copies = []
        for k, (fx, fy, fc) in enumerate(flips):
            px, py, pc = mx ^ fx, my ^ fy, mc ^ fc
            cp = pltpu.make_async_remote_copy(src_ref=g_ref.at[4 * px + 2 * py + pc], dst_ref=out_ref.at[me],
                                              send_sem=send_sems.at[k], recv_sem=recv_sems.at[k], device_id=(px, py, pc), device_id_type=MESH)
            cp.start()
            copies.append(cp)
        for k, (fx, fy, fc) in enumerate(flips):
            src = 4 * (mx ^ fx) + 2 * (my ^ fy) + (mc ^ fc)
            pltpu.make_async_remote_copy(src_ref=g_ref.at[me], dst_ref=out_ref.at[src], send_sem=send_sems.at[k], recv_sem=recv_sems.at[k],
                                         device_id=(mx ^ fx, my ^ fy, mc ^ fc), device_id_type=MESH).wait_recv()
        for cp in copies:
            cp.wait_send()
        mine.wait()

    return pl.pallas_call(
        body, out_shape=jax.ShapeDtypeStruct(g.shape, g.dtype),
        in_specs=[pl.BlockSpec(memory_space=pl.ANY)], out_specs=pl.BlockSpec(memory_space=pl.ANY),
        scratch_shapes=[pltpu.SemaphoreType.DMA((7,)), pltpu.SemaphoreType.DMA((7,)), pltpu.SemaphoreType.DMA],
        name=name)(g)


def _adamw(parts, w, m, v, *, br, name):
    r, wd = w.shape

    def body(p_ref, w_ref, m_ref, v_ref, g_ref, d_ref, nm_ref, nv_ref):
        g = p_ref[0].astype(F32)
        for k in range(1, N_DEV):
            g = g + p_ref[k].astype(F32)
        mm = ADAM_B1 * m_ref[...] + (1.0 - ADAM_B1) * g
        vv = ADAM_B2 * v_ref[...] + (1.0 - ADAM_B2) * (g * g)
        m_hat = mm / (1.0 - ADAM_B1 ** ADAM_STEP)
        v_hat = vv / (1.0 - ADAM_B2 ** ADAM_STEP)
        g_ref[...] = g
        d_ref[...] = -ADAM_LR * (m_hat / (jnp.sqrt(v_hat) + ADAM_EPS) + ADAM_WD * w_ref[...])
        nm_ref[...] = mm
        nv_ref[...] = vv

    blk = pl.BlockSpec((br, wd), lambda i: (i, 0))
    shp = jax.ShapeDtypeStruct((r, wd), F32)
    return pl.pallas_call(body, out_shape=(shp, shp, shp, shp), grid=(r // br,),
                          in_specs=[pl.BlockSpec((N_DEV, br, wd), lambda i: (0, i, 0)), blk, blk, blk], out_specs=(blk, blk, blk, blk),
                          compiler_params=_params("parallel"), name=name)(parts, w, m, v)


BIG = ("w_in_fox", "w_in_dil", "w_mem_kv", "w_out", "w_up", "w_down")
SMALL = ("norm_mix", "norm_mem", "norm_ffn", "b_forget", "conv_b", "norm_final")


def _pack(tensors, dtype):
    flat = jnp.concatenate([t.reshape(-1).astype(dtype) for t in tensors])
    rows = -(-flat.shape[0] // (PACK_W * PACK_ROW_ALIGN)) * PACK_ROW_ALIGN
    flat = jnp.pad(flat, (0, rows * PACK_W - flat.shape[0]))
    return flat.reshape(rows, PACK_W)


def _unpack(buf, shapes):
    lead = buf.shape[:-2]
    flat = buf.reshape(lead + (-1,))
    out, off = [], 0
    for shp in shapes:
        n = math.prod(shp)
        out.append(flat[..., off:off + n].reshape(lead + tuple(shp)))
        off += n
    return out


def _rows_full(t):
    n_dev, layers, r, n = t.shape
    return t.transpose(1, 0, 2, 3).reshape(layers, n_dev * r, n)


def _cols_full(t):
    n_dev, layers, k, c = t.shape
    return t.transpose(1, 2, 0, 3).reshape(layers, k, n_dev * c)


def _rows_chunks(t):
    layers, r, n = t.shape
    return t.reshape(layers, N_DEV, r // N_DEV, n).transpose(1, 0, 2, 3)


def _cols_chunks(t):
    layers, k, c = t.shape
    return t.reshape(layers, k, N_DEV, c // N_DEV).transpose(2, 0, 1, 3)


def _fox_permute(w):
    pad = jnp.zeros(w.shape[:-1] + (FOX_P - FOX_IN,), w.dtype)
    return jnp.concatenate([w[..., :3 * D_MIX], w[..., 3 * D_MIX + N_MIX_HEADS:], w[..., 3 * D_MIX:3 * D_MIX + N_MIX_HEADS], pad], axis=-1)


def _fox_unpermute(w):
    return jnp.concatenate([w[..., :3 * D_MIX], w[..., DIL_IN:DIL_IN + N_MIX_HEADS], w[..., 3 * D_MIX:DIL_IN]], axis=-1)


def _bias_layouts(c):
    s = c.shape[0]
    ct = c[:, :N_MIX_HEADS].T.reshape(N_MIX_HP, 2, s)
    cq6 = jnp.repeat(ct, HEAD_DIM, axis=1).transpose(0, 2, 1)
    ck6 = jnp.pad(ct, ((0, 0), (0, 6), (0, 0)))
    return cq6, ck6


def _bias_grads(dcq6, dck6):
    s = dcq6.shape[1]
    dq = dcq6[:, :, ::HEAD_DIM].transpose(1, 0, 2).reshape(s, N_MIX_HEADS)
    dk = dck6[:, :2, :].reshape(N_MIX_HEADS, s).T
    pad = lambda t: jnp.pad(t, ((0, 0), (0, LANES - N_MIX_HEADS)))
    return pad(dq), pad(dk)


def _device_step(x, mem, target, wts, small):
    s = x.shape[0]
    mt = mem.shape[0]
    bq = 512
    cos_t, sin_t = _rope_tables(s)
    row = lambda t, l: t[l][None, :]
    saved = []
    h = x
    for l in range(DEPTH):
        kind, slot = l % 2, l // 2
        w_in = wts["w_in_fox"][slot] if kind == 0 else wts["w_in_dil"][slot]
        npj = w_in.shape[1]
        xn = _rmsnorm_fwd(h, row(small["norm_mix"], l), br=512, name=f"norm_mix_fwd{l}")
        mn = _rmsnorm_fwd(mem, row(small["norm_mem"], l), br=mt, name=f"norm_mem_fwd{l}")
        proj = _mm(xn, w_in, "nn", tm=1024, tn=384 if kind == 0 else 512, name=f"in_proj{l}")
        kvm = _mm(mn, wts["w_mem_kv"][l], "nn", tm=mt, tn=512, name=f"mem_kv{l}")
        st = dict(h=h, xn=xn, mn=mn, proj=proj, kvm=kvm)
        if kind == 0:
            b_pad = jnp.pad(small["b_forget"][slot], (0, LANES - N_MIX_HEADS))[None, :]
            c = _forget_cumsum(proj, b_pad, name=f"forget_cumsum{l}")
            cq6, ck6 = _bias_layouts(c)
            mix, lse = _attn_fwd(proj, proj, cq6, ck6, q_col=0, k_col=N_MIX_HP, v_col=2 * N_MIX_HP, n_hp=N_MIX_HP,
                                 causal=True, bq=bq, bk=bq, name=f"fox_fwd{l}")
            st.update(b_pad=b_pad, cq6=cq6, ck6=ck6, mix=mix, lse=lse)
        else:
            qk_r = _rope(proj, cos_t, sin_t, n_cols=2 * N_MIX_HP, out_dtype=BF16, br=512, name=f"rope_fwd{l}")
            outs, lses = [], []
            for _, dil in DILATED_BRANCHES:
                o, ls = _dil_branch_fwd(qk_r, proj, dil, name=f"dil_fwd{l}_d{dil}")
                outs.append(o)
                lses.append(ls)
            mix, lse_tot = _dil_merge(outs, lses, br=512, name=f"dil_merge{l}")
            st.update(qk_r=qk_r, lses=lses, mix=mix, lse_tot=lse_tot)
        mo, lse_m = _attn_fwd(proj, kvm, None, None, q_col=QM_COL, k_col=0, v_col=N_MEM_HP, n_hp=N_MEM_HP,
                              causal=False, bq=bq, bk=mt, name=f"mem_fwd{l}")
        heads = jnp.concatenate([mix.astype(BF16), mo.astype(BF16)], axis=1)
        h1 = _mm(heads, wts["w_out"][l], "nn", tm=1024, tn=512, res=h, name=f"out_proj{l}")
        xf = _rmsnorm_fwd(h1, row(small["norm_ffn"], l), br=512, name=f"norm_ffn_fwd{l}")
        u = _mm(xf, wts["w_up"][l], "nn", tm=1024, tn=512, name=f"up_proj{l}")
        cw, cb = wts["conv_w"][l], small["conv_b"][l][None, :]
        a = _conv_fwd(u, cw, cb, name=f"conv_fwd{l}")
        h = _mm(a, wts["w_down"][l], "nn", tm=1024, tn=512, res=h1, name=f"down_proj{l}")
        st.update(mo=mo, lse_m=lse_m, heads=heads, h1=h1, xf=xf, u=u, a=a, npj=npj)
        saved.append(st)

    dh, dhb, dg_final, loss = _loss_head(h, target, small["norm_final"][None, :], br=512, name="loss_head")
    gw = {k: [None] * wts[k].shape[0] for k in ("w_in_fox", "w_in_dil", "w_mem_kv", "w_out", "w_up", "w_down", "conv_w")}
    gs = {k: [None] * DEPTH for k in ("norm_mix", "norm_mem", "norm_ffn", "conv_b")}
    gs["b_forget"] = [None] * 2
    for l in reversed(range(DEPTH)):
        st = saved[l]
        kind, slot = l % 2, l // 2
        cw, cb = wts["conv_w"][l], small["conv_b"][l][None, :]
        da = _mm(dhb, wts["w_down"][l], "nt", tm=1024, tn=256, name=f"down_dx{l}")
        gw["w_down"][l] = _mm(st["a"], dhb, "tn", tm=256, tn=512, name=f"down_dw{l}")
        dc, dwb = _conv_bwd_dc(st["u"], cw, cb, da, name=f"conv_bwd_dc{l}")
        gw["conv_w"][l] = jnp.concatenate([dwb[0, :3], dwb[1, :3]], axis=1)
        gs["conv_b"][l] = jnp.concatenate([dwb[0, 3], dwb[1, 3]], axis=0)
        du = _conv_bwd_du(dc, cw, name=f"conv_bwd_du{l}")
        dxf = _mm(du, wts["w_up"][l], "nt", tm=512, tn=512, name=f"up_dx{l}")
        gw["w_up"][l] = _mm(st["xf"], du, "tn", tm=512, tn=512, name=f"up_dw{l}")
        dh1, dh1b, dgf = _rmsnorm_bwd(st["h1"], dxf, row(small["norm_ffn"], l), dh, br=512, name=f"norm_ffn_bwd{l}")
        gs["norm_ffn"][l] = dgf[0]
        dheads = _mm(dh1b, wts["w_out"][l], "nt", tm=1024, tn=512, name=f"out_dx{l}")
        gw["w_out"][l] = _mm(st["heads"], dh1b, "tn", tm=512, tn=512, name=f"out_dw{l}")
        dqm, dkm, dvm = _attn_bwd(st["proj"], st["kvm"], st["mo"], dheads, st["lse_m"], None, None, q_col=QM_COL, k_col=0,
                                  v_col=N_MEM_HP, o_col=N_MIX_HP, n_hp=N_MEM_HP, causal=False, bq=bq, bk=mt, name=f"mem_bwd{l}")
        dkvm = jnp.concatenate([dkm, dvm], axis=1).astype(BF16)
        gw["w_mem_kv"][l] = _mm(st["mn"], dkvm, "tn", tm=512, tn=512, name=f"mem_kv_dw{l}")
        dmn = _mm(dkvm, wts["w_mem_kv"][l], "nt", tm=mt, tn=512, name=f"mem_kv_dx{l}")
        _, _, dgm = _rmsnorm_bwd(mem, dmn, row(small["norm_mem"], l), None, br=mt, name=f"norm_mem_bwd{l}")
        gs["norm_mem"][l] = dgm[0]
        if kind == 0:
            dq, dk, dv, dcq6, dck6 = _attn_bwd(st["proj"], st["proj"], st["mix"], dheads, st["lse"], st["cq6"], st["ck6"], q_col=0,
                                               k_col=N_MIX_HP, v_col=2 * N_MIX_HP, o_col=0, n_hp=N_MIX_HP, causal=True,
                                               bq=bq, bk=bq, name=f"fox_bwd{l}")
            dcq, dck = _bias_grads(dcq6, dck6)
            dz, db = _forget_cumsum_bwd(st["proj"], st["b_pad"], dcq, dck, name=f"forget_cumsum_bwd{l}")
            gs["b_forget"][slot] = db[0, :N_MIX_HEADS]
            dproj = jnp.concatenate([dq.astype(BF16), dk.astype(BF16), dv.astype(BF16), dqm.astype(BF16), dz], axis=1)
            w_in = wts["w_in_fox"][slot]
        else:
            acc = None
            for (_, dil), lse_g in zip(DILATED_BRANCHES, st["lses"]):
                parts = _dil_branch_bwd(st["qk_r"], st["proj"], st["mix"], dheads, lse_g, st["lse_tot"], dil, name=f"dil_bwd{l}_d{dil}")
                acc = [parts] if acc is None else acc + [parts]
            dq_r = _add3(acc[0][0], acc[1][0], acc[2][0], br=512, out_dtype=F32, name=f"dil_sum_q{l}")
            dk_r = _add3(acc[0][1], acc[1][1], acc[2][1], br=512, out_dtype=F32, name=f"dil_sum_k{l}")
            dv = _add3(acc[0][2], acc[1][2], acc[2][2], br=512, out_dtype=BF16, name=f"dil_sum_v{l}")
            dqk = _rope(jnp.concatenate([dq_r, dk_r], axis=1), cos_t, -sin_t, n_cols=2 * N_MIX_HP, out_dtype=BF16, br=512, name=f"rope_bwd{l}")
            dproj = jnp.concatenate([dqk, dv, dqm.astype(BF16)], axis=1)
            w_in = wts["w_in_dil"][slot]
        dxn = _mm(dproj, w_in, "nt", tm=1024, tn=512, name=f"in_dx{l}")
        gw["w_in_fox" if kind == 0 else "w_in_dil"][slot] = _mm(st["xn"], dproj, "tn", tm=512, tn=384 if kind == 0 else 512, name=f"in_dw{l}")
        dh, dhb, dgx = _rmsnorm_bwd(st["h"], dxn, row(small["norm_mix"], l), dh1, br=512, name=f"norm_mix_bwd{l}")
        gs["norm_mix"][l] = dgx[0]

    grads_w = {k: jnp.stack(v) for k, v in gw.items()}
    grads_s = {k: jnp.stack(v) for k, v in gs.items()}
    grads_s["norm_final"] = dg_final[0]
    return loss[0, 0], dh, grads_w, grads_s


def kernel(x, mem, norm_mix, norm_mem, norm_ffn, w_in_fox, b_forget, w_in_dil, w_mem_kv, w_out, w_up, conv_w, conv_b, w_down, norm_final, loss_target, m_norm_mix, m_norm_mem, m_norm_ffn, m_w_in_fox, m_b_forget, m_w_in_dil, m_w_mem_kv, m_w_out, m_w_up, m_conv_w, m_conv_b, m_w_down, m_norm_final, v_norm_mix, v_norm_mem, v_norm_ffn, v_w_in_fox, v_b_forget, v_w_in_dil, v_w_mem_kv, v_w_out, v_w_up, v_conv_w, v_conv_b, v_w_down, v_norm_final):
    names = ["norm_mix", "norm_mem", "norm_ffn", "w_in_fox", "b_forget", "w_in_dil", "w_mem_kv", "w_out", "w_up", "conv_w", "conv_b",
             "w_down", "norm_final"]
    w = dict(zip(names, (norm_mix, norm_mem, norm_ffn, w_in_fox, b_forget, w_in_dil, w_mem_kv, w_out, w_up, conv_w, conv_b, w_down, norm_final)))
    m = dict(zip(names, (m_norm_mix, m_norm_mem, m_norm_ffn, m_w_in_fox, m_b_forget, m_w_in_dil, m_w_mem_kv, m_w_out, m_w_up, m_conv_w,
                         m_conv_b, m_w_down, m_norm_final)))
    v = dict(zip(names, (v_norm_mix, v_norm_mem, v_norm_ffn, v_w_in_fox, v_b_forget, v_w_in_dil, v_w_mem_kv, v_w_out, v_w_up, v_conv_w,
                         v_conv_b, v_w_down, v_norm_final)))
    big = BIG + ("conv_w",)
    big_shapes = [w[k].shape for k in big]
    small_shapes = [w[k].shape for k in SMALL]

    w_pack = _pack([w[k] for k in big], F32)
    ag_pack = _pack([w[k].astype(BF16) for k in BIG] + [lax.bitcast_convert_type(conv_w, BF16)], BF16)
    gathered = _all_gather(ag_pack, name="weights_all_gather")
    parts = dict(zip(big, _unpack(gathered, big_shapes[:-1] + [conv_w.shape + (2,)])))
    full = {
        "w_in_fox": _fox_permute(_rows_full(parts["w_in_fox"])),
        "w_in_dil": _cols_full(parts["w_in_dil"]),
        "w_mem_kv": _rows_full(parts["w_mem_kv"]),
        "w_out": _rows_full(parts["w_out"]),
        "w_up": _cols_full(parts["w_up"]),
        "w_down": _rows_full(parts["w_down"]),
        "conv_w": _cols_full(lax.bitcast_convert_type(parts["conv_w"], F32)),
    }
    small = {k: w[k] for k in SMALL}

    loss, grad_x, gw, gs = _device_step(x[0], mem[0], loss_target[0], full, small)

    chunks = {
        "w_in_fox": _rows_chunks(_fox_unpermute(gw["w_in_fox"])),
        "w_in_dil": _cols_chunks(gw["w_in_dil"]),
        "w_mem_kv": _rows_chunks(gw["w_mem_kv"]),
        "w_out": _rows_chunks(gw["w_out"]),
        "w_up": _cols_chunks(gw["w_up"]),
        "w_down": _rows_chunks(gw["w_down"]),
        "conv_w": _cols_chunks(gw["conv_w"]),
    }
    g_pack = jnp.stack([_pack([chunks[k][j] for k in big], BF16) for j in range(N_DEV)])
    g_recv = _all_to_all(g_pack, name="grads_all_to_all")
    s_pack = _pack([gs[k] for k in SMALL], F32)
    s_recv = _all_gather(s_pack, name="small_grads_all_gather")

    outs_big = _adamw(g_recv, w_pack, _pack([m[k] for k in big], F32), _pack([v[k] for k in big], F32), br=256, name="adamw_big")
    outs_small = _adamw(s_recv, _pack([w[k] for k in SMALL], F32), _pack([m[k] for k in SMALL], F32), _pack([v[k] for k in SMALL], F32),
                        br=256, name="adamw_small")
    res = []
    for ob, os_ in zip(outs_big, outs_small):
        d = dict(zip(big, _unpack(ob, big_shapes)))
        d.update(zip(SMALL, _unpack(os_, small_shapes)))
        res.append([d[k] for k in names])
    loss = lax.psum(loss, ("x", "y", "c"))
    return (loss, grad_x[None], *res[0], *res[1], *res[2], *res[3])
```

```python
import functools
import math

import jax
import jax.numpy as jnp
from jax import lax
from jax.experimental import pallas as pl
from jax.experimental.pallas import tpu as pltpu

F32 = jnp.float32
BF16 = jnp.bfloat16

D_MODEL = 1024
HEAD_DIM = 64
N_MIX_HEADS = 12
N_MEM_HEADS = 4
D_MIX = N_MIX_HEADS * HEAD_DIM
D_MEMQ = N_MEM_HEADS * HEAD_DIM
D_FF = 2816
DEPTH = 4
FOX_IN = 3 * D_MIX + N_MIX_HEADS + D_MEMQ
DIL_IN = 3 * D_MIX + D_MEMQ
LANES = 128
FOX_P = DIL_IN + LANES
N_MIX_HP = D_MIX // LANES
N_MEM_HP = D_MEMQ // LANES
QM_COL = 3 * N_MIX_HP
F_COL = DIL_IN // LANES
DILATED_BRANCHES = ((128, 1), (512, 4), (2048, 16))
DIL_L = 128
ROPE_THETA = 10000.0
NORM_EPS = 1e-6
NEG = -1e30
SCALE = HEAD_DIM ** -0.5
N_DEV = 8

ADAM_LR = 0.001
ADAM_B1 = 0.9
ADAM_B2 = 0.999
ADAM_EPS = 1e-08
ADAM_WD = 0.01
ADAM_STEP = 10

VMEM_LIMIT = 56 * 1024 * 1024
PACK_W = 1024
PACK_ROW_ALIGN = 8

MESH = pl.DeviceIdType.MESH
NT = (((1,), (1,)), ((), ()))
NN = (((1,), (0,)), ((), ()))
TN = (((0,), (0,)), ((), ()))


def _params(*sem):
    return pltpu.CompilerParams(dimension_semantics=sem, vmem_limit_bytes=VMEM_LIMIT)


def _lane_lo(shape):
    return lax.broadcasted_iota(jnp.int32, shape, len(shape) - 1) < HEAD_DIM


def _pair(lo, a, b):
    return jnp.where(lo, a, b)


def _mm(a, b, mode, *, tm, tn, name, out_dtype=F32, res=None, layer=None, chunk=None):
    lead = () if layer is None else (layer,)
    nl = (None,) * len(lead)
    bs = b.shape[len(lead):]
    dims = {"nn": NN, "nt": NT, "tn": TN}[mode]
    reduce_n = 0
    if chunk is None:
        (m, k) = a.shape[::-1] if mode == "tn" else a.shape
        n = bs[0] if mode == "nt" else bs[1]
        grid = (m // tm, n // tn)
        a_spec = pl.BlockSpec((k, tm), lambda i, j: (0, i)) if mode == "tn" else pl.BlockSpec((tm, k), lambda i, j: (i, 0))
        b_spec = pl.BlockSpec(nl + ((tn, k) if mode == "nt" else (k, tn)), lambda i, j: lead + ((j, 0) if mode == "nt" else (0, j)))
        o_spec = pl.BlockSpec((tm, tn), lambda i, j: (i, j))
        out_shape = (m, n)
    elif chunk == "b":
        (m, k) = a.shape[::-1] if mode == "tn" else a.shape
        c, nc = bs[0], (bs[1] if mode == "nt" else bs[2])
        grid = (m // tm, c)
        a_spec = pl.BlockSpec((k, tm), lambda i, j: (0, i)) if mode == "tn" else pl.BlockSpec((tm, k), lambda i, j: (i, 0))
        b_spec = pl.BlockSpec(nl + (None,) + tuple(bs[1:]), lambda i, j: lead + (j, 0, 0))
        o_spec = pl.BlockSpec((None, tm, nc), lambda i, j: (j, i, 0))
        out_shape = (c, m, nc)
    elif chunk == "a":
        assert mode == "tn"
        c, k, mc = a.shape
        n = bs[1]
        grid = (c, n // tn)
        a_spec = pl.BlockSpec((None, k, mc), lambda i, j: (i, 0, 0))
        b_spec = pl.BlockSpec(nl + (k, tn), lambda i, j: lead + (0, j))
        o_spec = pl.BlockSpec((None, mc, tn), lambda i, j: (i, 0, j))
        out_shape = (c, mc, n)
    else:
        reduce_n, m, kc = a.shape
        n = bs[1] if mode == "nt" else bs[2]
        grid = (m // tm, n // tn)
        a_spec = pl.BlockSpec((reduce_n, tm, kc), lambda i, j: (0, i, 0))
        b_spec = pl.BlockSpec(nl + ((reduce_n, tn, kc) if mode == "nt" else (reduce_n, kc, tn)),
                              lambda i, j: lead + ((0, j, 0) if mode == "nt" else (0, 0, j)))
        o_spec = pl.BlockSpec((tm, tn), lambda i, j: (i, j))
        out_shape = (m, n)

    def body(*refs):
        a_ref, b_ref = refs[0], refs[1]
        o_ref = refs[-1]
        dot = lambda x, y: lax.dot_general(x.astype(BF16), y.astype(BF16), dims, preferred_element_type=F32)
        if reduce_n:
            acc = dot(a_ref[0], b_ref[0])
            for r in range(1, reduce_n):
                acc = acc + dot(a_ref[r], b_ref[r])
        else:
            acc = dot(a_ref[...], b_ref[...])
        if res is not None:
            acc = acc + refs[2][...]
        o_ref[...] = acc.astype(o_ref.dtype)

    ins = [a, b] + ([res] if res is not None else [])
    specs = [a_spec, b_spec] + ([o_spec] if res is not None else [])
    return pl.pallas_call(body, out_shape=jax.ShapeDtypeStruct(out_shape, out_dtype), grid=grid,
                          in_specs=specs, out_specs=o_spec, compiler_params=_params("parallel", "parallel"), name=name)(*ins)


def _rmsnorm_fwd(x, g, *, br, name):
    r, d = x.shape

    def body(x_ref, g_ref, o_ref):
        xf = x_ref[...]
        rs = lax.rsqrt(jnp.mean(xf * xf, axis=-1, keepdims=True) + NORM_EPS)
        o_ref[...] = (xf * rs * g_ref[...]).astype(BF16)

    return pl.pallas_call(body, out_shape=jax.ShapeDtypeStruct((r, d), BF16), grid=(r // br,),
                          in_specs=[pl.BlockSpec((br, d), lambda i: (i, 0)), pl.BlockSpec((1, d), lambda i: (0, 0))],
                          out_specs=pl.BlockSpec((br, d), lambda i: (i, 0)), compiler_params=_params("parallel"), name=name)(x, g)


def _rms_bwd_math(x, dy, g):
    d = x.shape[-1]
    rs = lax.rsqrt(jnp.mean(x * x, axis=-1, keepdims=True) + NORM_EPS)
    gy = dy * g
    proj = jnp.sum(x * gy, axis=-1, keepdims=True) * (1.0 / d)
    dx = rs * gy - x * (rs * rs * rs) * proj
    dg = jnp.sum(dy * (x * rs), axis=0, keepdims=True)
    return dx, dg


def _rmsnorm_bwd(x, dy, g, res, *, br, name):
    r, d = x.shape
    has_res = res is not None

    def body(*refs):
        x_ref, dy_ref, g_ref = refs[:3]
        dx_ref, dxb_ref, dg_ref = refs[-3:]
        dx, dg = _rms_bwd_math(x_ref[...], dy_ref[...], g_ref[...])
        if has_res:
            dx = dx + refs[3][...]
        dx_ref[...] = dx
        dxb_ref[...] = dx.astype(BF16)

        @pl.when(pl.program_id(0) == 0)
        def _():
            dg_ref[...] = jnp.zeros_like(dg_ref)

        dg_ref[0:1, :] += dg

    row = pl.BlockSpec((br, d), lambda i: (i, 0))
    ins = [x, dy, g] + ([res] if has_res else [])
    specs = [row, row, pl.BlockSpec((1, d), lambda i: (0, 0))] + ([row] if has_res else [])
    return pl.pallas_call(
        body, out_shape=(jax.ShapeDtypeStruct((r, d), F32), jax.ShapeDtypeStruct((r, d), BF16), jax.ShapeDtypeStruct((8, d), F32)),
        grid=(r // br,), in_specs=specs, out_specs=(row, row, pl.BlockSpec((8, d), lambda i: (0, 0))),
        compiler_params=_params("arbitrary"), name=name)(*ins)


def _loss_head(h, target, g, *, br, name):
    r, d = h.shape

    def body(x_ref, t_ref, g_ref, dx_ref, dxb_ref, dg_ref, loss_ref):
        x = x_ref[...]
        gg = g_ref[...]
        rs = lax.rsqrt(jnp.mean(x * x, axis=-1, keepdims=True) + NORM_EPS)
        err = x * rs * gg - t_ref[...]
        part = jnp.sum(jnp.sum(err * err, axis=1, keepdims=True), axis=0, keepdims=True) * (0.5 / d)
        dx, dg = _rms_bwd_math(x, err * (1.0 / d), gg)
        dx_ref[...] = dx
        dxb_ref[...] = dx.astype(BF16)

        @pl.when(pl.program_id(0) == 0)
        def _():
            dg_ref[...] = jnp.zeros_like(dg_ref)
            loss_ref[...] = jnp.zeros_like(loss_ref)

        dg_ref[0:1, :] += dg
        loss_ref[...] += jnp.broadcast_to(part, loss_ref.shape)

    row = pl.BlockSpec((br, d), lambda i: (i, 0))
    return pl.pallas_call(
        body, out_shape=(jax.ShapeDtypeStruct((r, d), F32), jax.ShapeDtypeStruct((r, d), BF16),
                         jax.ShapeDtypeStruct((8, d), F32), jax.ShapeDtypeStruct((8, LANES), F32)),
        grid=(r // br,), in_specs=[row, row, pl.BlockSpec((1, d), lambda i: (0, 0))],
        out_specs=(row, row, pl.BlockSpec((8, d), lambda i: (0, 0)), pl.BlockSpec((8, LANES), lambda i: (0, 0))),
        compiler_params=_params("arbitrary"), name=name)(h, target, g)


def _split3(x):
    hi = x.astype(BF16)
    r1 = x - hi.astype(F32)
    mid = r1.astype(BF16)
    lo = (r1 - mid.astype(F32)).astype(BF16)
    return hi, mid, lo


def _tri_sum(tri, x):
    hi, mid, lo = _split3(x)
    dot = lambda t: jnp.dot(tri, t, preferred_element_type=F32)
    return dot(hi) + dot(mid) + dot(lo)


def _forget_cumsum(proj, b_pad, *, name):
    s = proj.shape[0]
    blk = LANES

    def body(f_ref, b_ref, c_ref):
        ri = lax.broadcasted_iota(jnp.int32, (blk, blk), 0)
        ci = lax.broadcasted_iota(jnp.int32, (blk, blk), 1)
        tri = (ci <= ri).astype(BF16)
        bias = b_ref[...]

        def step(t, carry):
            rows = pl.ds(pl.multiple_of(t * blk, blk), blk)
            z = f_ref[rows, :] + bias
            lf = jnp.minimum(z, 0.0) - jnp.log(1.0 + jnp.exp(-jnp.abs(z)))
            cs = _tri_sum(tri, lf) + carry
            c_ref[rows, :] = cs
            return cs[blk - 1:blk, :]

        lax.fori_loop(0, s // blk, step, jnp.zeros((1, blk), F32))

    return pl.pallas_call(body, out_shape=jax.ShapeDtypeStruct((s, LANES), F32), grid=(1,),
                          in_specs=[pl.BlockSpec((s, LANES), lambda i: (0, F_COL)), pl.BlockSpec((1, LANES), lambda i: (0, 0))],
                          out_specs=pl.BlockSpec((s, LANES), lambda i: (0, 0)), compiler_params=_params("arbitrary"), name=name)(proj, b_pad)


def _forget_cumsum_bwd(proj, b_pad, dcq, dck, *, name):
    s = proj.shape[0]
    blk = LANES
    nblk = s // blk

    def body(f_ref, b_ref, dcq_ref, dck_ref, dz_ref, db_ref):
        ri = lax.broadcasted_iota(jnp.int32, (blk, blk), 0)
        ci = lax.broadcasted_iota(jnp.int32, (blk, blk), 1)
        triu = (ci >= ri).astype(BF16)
        bias = b_ref[...]

        def step(t, carry):
            tail, dbs = carry
            rows = pl.ds(pl.multiple_of((nblk - 1 - t) * blk, blk), blk)
            dc = dcq_ref[rows, :] - dck_ref[rows, :]
            dlf = _tri_sum(triu, dc) + tail
            z = f_ref[rows, :] + bias
            e = jnp.exp(-jnp.abs(z))
            sig_neg = jnp.where(z >= 0.0, e, 1.0) / (1.0 + e)
            dz = dlf * sig_neg
            dz_ref[rows, :] = dz.astype(BF16)
            return dlf[0:1, :], dbs + jnp.sum(dz, axis=0, keepdims=True)

        _, dbs = lax.fori_loop(0, nblk, step, (jnp.zeros((1, blk), F32), jnp.zeros((1, blk), F32)))
        db_ref[...] = jnp.broadcast_to(dbs, db_ref.shape)

    full = pl.BlockSpec((s, LANES), lambda i: (0, 0))
    return pl.pallas_call(body, out_shape=(jax.ShapeDtypeStruct((s, LANES), BF16), jax.ShapeDtypeStruct((8, LANES), F32)), grid=(1,),
                          in_specs=[pl.BlockSpec((s, LANES), lambda i: (0, F_COL)), pl.BlockSpec((1, LANES), lambda i: (0, 0)), full, full],
                          out_specs=(full, pl.BlockSpec((8, LANES), lambda i: (0, 0))), compiler_params=_params("arbitrary"), name=name)(proj, b_pad, dcq, dck)


def _attn_fwd(q_arr, kv_arr, cq6, ck6, *, q_col, k_col, v_col, n_hp, causal, bq, bk, name):
    s = q_arr.shape[0]
    skv = kv_arr.shape[0]
    bias = cq6 is not None
    nq = s // bq

    def body(*refs):
        q_ref, k_ref, v_ref = refs[:3]
        o_ref, lse_ref = refs[-2:]
        i = pl.program_id(1)
        lo = _lane_lo((bq, LANES))
        q = q_ref[...]
        q0 = jnp.where(lo, q, 0.0).astype(BF16)
        q1 = jnp.where(lo, 0.0, q).astype(BF16)
        if bias:
            cq = refs[3][0]
            cq0, cq1 = cq[:, 0:1], cq[:, HEAD_DIM:HEAD_DIM + 1]
            ck_ref = refs[4]
        if causal:
            rowpos = i * bq + lax.broadcasted_iota(jnp.int32, (bq, bk), 0)
            colid = lax.broadcasted_iota(jnp.int32, (bq, bk), 1)

        def step(j, carry):
            m0, l0, a0, m1, l1, a1 = carry
            ks = pl.ds(pl.multiple_of(j * bk, bk), bk)
            k = k_ref[ks, :].astype(BF16)
            v = v_ref[ks, :].astype(BF16)
            s0 = lax.dot_general(q0, k, NT, preferred_element_type=F32) * SCALE
            s1 = lax.dot_general(q1, k, NT, preferred_element_type=F32) * SCALE
            if bias:
                s0 = s0 + cq0 - ck_ref[0, 0:1, ks]
                s1 = s1 + cq1 - ck_ref[0, 1:2, ks]
            if causal:
                ok = (j * bk + colid) <= rowpos
                s0 = jnp.where(ok, s0, NEG)
                s1 = jnp.where(ok, s1, NEG)

            def upd(sc, m, l, a):
                mn = jnp.maximum(m, jnp.max(sc, axis=1, keepdims=True))
                p = jnp.exp(sc - mn)
                al = jnp.exp(m - mn)
                return mn, al * l + jnp.sum(p, axis=1, keepdims=True), al * a + jnp.dot(p.astype(BF16), v, preferred_element_type=F32)

            m0, l0, a0 = upd(s0, m0, l0, a0)
            m1, l1, a1 = upd(s1, m1, l1, a1)
            return m0, l0, a0, m1, l1, a1

        nk = ((i + 1) * bq + bk - 1) // bk if causal else skv // bk
        col = lambda v_: jnp.full((bq, 1), v_, F32)
        init = (col(NEG), col(0.0), jnp.zeros((bq, LANES), F32), col(NEG), col(0.0), jnp.zeros((bq, LANES), F32))
        m0, l0, a0, m1, l1, a1 = lax.fori_loop(0, nk, step, init)
        o_ref[...] = _pair(lo, a0 / l0, a1 / l1)
        lse_ref[0] = _pair(lo, m0 + jnp.log(l0), m1 + jnp.log(l1))

    specs = [pl.BlockSpec((bq, LANES), lambda h, i: (i, q_col + h)),
             pl.BlockSpec((skv, LANES), lambda h, i: (0, k_col + h)),
             pl.BlockSpec((skv, LANES), lambda h, i: (0, v_col + h))]
    ins = [q_arr, kv_arr, kv_arr]
    if bias:
        specs += [pl.BlockSpec((1, bq, LANES), lambda h, i: (h, i, 0)), pl.BlockSpec((1, 8, skv), lambda h, i: (h, 0, 0))]
        ins += [cq6, ck6]
    return pl.pallas_call(
        body, out_shape=(jax.ShapeDtypeStruct((s, n_hp * LANES), F32), jax.ShapeDtypeStruct((n_hp, s, LANES), F32)),
        grid=(n_hp, nq), in_specs=specs,
        out_specs=(pl.BlockSpec((bq, LANES), lambda h, i: (i, h)), pl.BlockSpec((1, bq, LANES), lambda h, i: (h, i, 0))),
        compiler_params=_params("parallel", "parallel"), name=name)(*ins)


def _attn_bwd(q_arr, kv_arr, o_arr, do_arr, lse, cq6, ck6, *, q_col, k_col, v_col, o_col, n_hp, causal, bq, bk, name):
    s = q_arr.shape[0]
    skv = kv_arr.shape[0]
    bias = cq6 is not None
    nq = s // bq

    def body(*refs):
        q_ref, k_ref, v_ref, o_ref, do_ref, lse_ref = refs[:6]
        if bias:
            cq_ref, ck_ref = refs[6:8]
            dq_ref, dk_ref, dv_ref, dcq_ref, dck_ref = refs[-5:]
        else:
            dq_ref, dk_ref, dv_ref = refs[-3:]
        j = pl.program_id(1)
        lo_q = _lane_lo((bq, LANES))
        lo_k = _lane_lo((bk, LANES))
        k = k_ref[...]
        v = v_ref[...].astype(BF16)
        kb = k.astype(BF16)
        k0 = jnp.where(lo_k, k, 0.0).astype(BF16)
        k1 = jnp.where(lo_k, 0.0, k).astype(BF16)
        if bias:
            ck0 = ck_ref[0, 0:1, :]
            ck1 = ck_ref[0, 1:2, :]
        if causal:
            colpos = j * bk + lax.broadcasted_iota(jnp.int32, (bq, bk), 1)
            rowid = lax.broadcasted_iota(jnp.int32, (bq, bk), 0)

        @pl.when(j == 0)
        def _():
            dq_ref[...] = jnp.zeros_like(dq_ref)
            if bias:
                dcq_ref[...] = jnp.zeros_like(dcq_ref)

        def step(i, carry):
            dk_acc, dv_acc, cs0, cs1 = carry
            qs = pl.ds(pl.multiple_of(i * bq, bq), bq)
            q = q_ref[qs, :]
            do = do_ref[qs, :]
            dd = do * o_ref[qs, :]
            lse_i = lse_ref[0, qs, :]
            q0 = jnp.where(lo_q, q, 0.0).astype(BF16)
            q1 = jnp.where(lo_q, 0.0, q).astype(BF16)
            do0 = jnp.where(lo_q, do, 0.0).astype(BF16)
            do1 = jnp.where(lo_q, 0.0, do).astype(BF16)
            d0 = jnp.sum(jnp.where(lo_q, dd, 0.0), axis=1, keepdims=True)
            d1 = jnp.sum(jnp.where(lo_q, 0.0, dd), axis=1, keepdims=True)
            s0 = lax.dot_general(q0, kb, NT, preferred_element_type=F32) * SCALE
            s1 = lax.dot_general(q1, kb, NT, preferred_element_type=F32) * SCALE
            if bias:
                cq = cq_ref[0, qs, :]
                s0 = s0 + cq[:, 0:1] - ck0
                s1 = s1 + cq[:, HEAD_DIM:HEAD_DIM + 1] - ck1
            if causal:
                ok = colpos <= (i * bq + rowid)
                s0 = jnp.where(ok, s0, NEG)
                s1 = jnp.where(ok, s1, NEG)
            p0 = jnp.exp(s0 - lse_i[:, 0:1])
            p1 = jnp.exp(s1 - lse_i[:, HEAD_DIM:HEAD_DIM + 1])
            dp0 = lax.dot_general(do0, v, NT, preferred_element_type=F32)
            dp1 = lax.dot_general(do1, v, NT, preferred_element_type=F32)
            ds0 = p0 * (dp0 - d0)
            ds1 = p1 * (dp1 - d1)
            ds0b, ds1b = ds0.astype(BF16), ds1.astype(BF16)
            dv_acc = dv_acc + lax.dot_general(p0.astype(BF16), do0, TN, preferred_element_type=F32) \
                + lax.dot_general(p1.astype(BF16), do1, TN, preferred_element_type=F32)
            dk_acc = dk_acc + lax.dot_general(ds0b, q0, TN, preferred_element_type=F32) \
                + lax.dot_general(ds1b, q1, TN, preferred_element_type=F32)
            dq_ref[qs, :] += (jnp.dot(ds0b, k0, preferred_element_type=F32) + jnp.dot(ds1b, k1, preferred_element_type=F32)) * SCALE
            if bias:
                dcq_ref[0, qs, :] += _pair(lo_q, jnp.sum(ds0, axis=1, keepdims=True), jnp.sum(ds1, axis=1, keepdims=True))
                cs0 = cs0 + jnp.sum(ds0, axis=0, keepdims=True)
                cs1 = cs1 + jnp.sum(ds1, axis=0, keepdims=True)
            return dk_acc, dv_acc, cs0, cs1

        i0 = (j * bk) // bq if causal else 0
        init = (jnp.zeros((bk, LANES), F32), jnp.zeros((bk, LANES), F32), jnp.zeros((1, bk), F32), jnp.zeros((1, bk), F32))
        dk_acc, dv_acc, cs0, cs1 = lax.fori_loop(i0, nq, step, init)
        dk_ref[...] = dk_acc * SCALE
        dv_ref[...] = dv_acc
        if bias:
            r8 = lax.broadcasted_iota(jnp.int32, (8, bk), 0)
            dck_ref[0] = jnp.where(r8 == 0, cs0, jnp.where(r8 == 1, cs1, 0.0))

    full_q = lambda c: pl.BlockSpec((s, LANES), lambda h, j: (0, c + h))
    specs = [full_q(q_col),
             pl.BlockSpec((bk, LANES), lambda h, j: (j, k_col + h)),
             pl.BlockSpec((bk, LANES), lambda h, j: (j, v_col + h)),
             full_q(0), full_q(o_col),
             pl.BlockSpec((1, s, LANES), lambda h, j: (h, 0, 0))]
    ins = [q_arr, kv_arr, kv_arr, o_arr, do_arr, lse]
    out_shape = [jax.ShapeDtypeStruct((s, n_hp * LANES), F32), jax.ShapeDtypeStruct((skv, n_hp * LANES), F32),
                 jax.ShapeDtypeStruct((skv, n_hp * LANES), F32)]
    out_specs = [full_q(0), pl.BlockSpec((bk, LANES), lambda h, j: (j, h)), pl.BlockSpec((bk, LANES), lambda h, j: (j, h))]
    if bias:
        specs += [pl.BlockSpec((1, s, LANES), lambda h, j: (h, 0, 0)), pl.BlockSpec((1, 8, bk), lambda h, j: (h, 0, j))]
        ins += [cq6, ck6]
        out_shape += [jax.ShapeDtypeStruct((n_hp, s, LANES), F32), jax.ShapeDtypeStruct((n_hp, 8, skv), F32)]
        out_specs += [pl.BlockSpec((1, s, LANES), lambda h, j: (h, 0, 0)), pl.BlockSpec((1, 8, bk), lambda h, j: (h, 0, j))]
    return pl.pallas_call(body, out_shape=tuple(out_shape), grid=(n_hp, skv // bk), in_specs=specs, out_specs=tuple(out_specs),
                          compiler_params=_params("parallel", "arbitrary"), name=name)(*ins)


def _rope_tables(s):
    inv = 1.0 / (ROPE_THETA ** (jnp.arange(0, HEAD_DIM, 2, dtype=F32) / HEAD_DIM))
    ang = jnp.arange(s, dtype=F32)[:, None] * inv[None, :]
    cos, sin = jnp.cos(ang), jnp.sin(ang)
    return jnp.tile(cos, (1, 4)), jnp.concatenate([-sin, sin, -sin, sin], axis=1)


def _rope(x_arr, cos_t, sin_t, *, n_cols, out_dtype, br, name):
    s = x_arr.shape[0]

    def body(x_ref, c_ref, s_ref, o_ref):
        x = x_ref[...].astype(F32)
        first = (lax.broadcasted_iota(jnp.int32, x.shape, 1) % HEAD_DIM) < (HEAD_DIM // 2)
        swapped = jnp.where(first, pltpu.roll(x, LANES - HEAD_DIM // 2, 1), pltpu.roll(x, HEAD_DIM // 2, 1))
        o_ref[...] = (x * c_ref[...] + swapped * s_ref[...]).astype(o_ref.dtype)

    tab = pl.BlockSpec((br, LANES), lambda i, j: (i, 0))
    blk = pl.BlockSpec((br, LANES), lambda i, j: (i, j))
    return pl.pallas_call(body, out_shape=jax.ShapeDtypeStruct((s, n_cols * LANES), out_dtype), grid=(s // br, n_cols),
                          in_specs=[blk, tab, tab], out_specs=blk, compiler_params=_params("parallel", "parallel"), name=name)(x_arr, cos_t, sin_t)


def _dil_scores(q0, q1, kp, kc, has_prev):
    a = lax.broadcasted_iota(jnp.int32, (DIL_L, DIL_L), 0)
    c = lax.broadcasted_iota(jnp.int32, (DIL_L, DIL_L), 1)
    ok_p, ok_c = (c >= a) & has_prev, c <= a
    sc = lambda q, k, ok: jnp.where(ok, lax.dot_general(q, k, NT, preferred_element_type=F32) * SCALE, NEG)
    return sc(q0, kp, ok_p), sc(q0, kc, ok_c), sc(q1, kp, ok_p), sc(q1, kc, ok_c)


def _dil_rows(t, dil):
    r, m = t % dil, t // dil
    start = m * (DIL_L * dil) + r
    prev = jnp.maximum(start - DIL_L * dil, 0)
    return pl.ds(start, DIL_L, stride=dil), pl.ds(prev, DIL_L, stride=dil), m > 0


def _softmax3(a, b, c):
    m = jnp.maximum(jnp.maximum(a, b), c)
    ea, eb, ec = jnp.exp(a - m), jnp.exp(b - m), jnp.exp(c - m)
    den = ea + eb + ec
    inv = 1.0 / den
    return ea * inv, eb * inv, ec * inv, m + jnp.log(den)


def _dil_fwd(qk_r, proj, *, name):
    s = qk_r.shape[0]
    nsub = s // DIL_L
    mb = 512

    def body(q_ref, k_ref, v_ref, mix_ref, l1_ref, l2_ref, l3_ref, o1_scr, o2_scr, o3_scr):
        lo = _lane_lo((DIL_L, LANES))
        for (_, dil), o_scr, l_ref in zip(DILATED_BRANCHES, (o1_scr, o2_scr, o3_scr), (l1_ref, l2_ref, l3_ref)):
            def step(t, carry, dil=dil, o_scr=o_scr, l_ref=l_ref):
                cur, prev, has_prev = _dil_rows(t, dil)
                q = q_ref[cur, :]
                q0 = jnp.where(lo, q, 0.0).astype(BF16)
                q1 = jnp.where(lo, 0.0, q).astype(BF16)
                s0p, s0c, s1p, s1c = _dil_scores(q0, q1, k_ref[prev, :].astype(BF16), k_ref[cur, :].astype(BF16), has_prev)
                vp = v_ref[prev, :].astype(BF16)
                vc = v_ref[cur, :].astype(BF16)

                def head(sp, sc):
                    m = jnp.maximum(jnp.max(sp, axis=1, keepdims=True), jnp.max(sc, axis=1, keepdims=True))
                    ep, ec = jnp.exp(sp - m), jnp.exp(sc - m)
                    den = jnp.sum(ep, axis=1, keepdims=True) + jnp.sum(ec, axis=1, keepdims=True)
                    inv = 1.0 / den
                    o = jnp.dot((ep * inv).astype(BF16), vp, preferred_element_type=F32) \
                        + jnp.dot((ec * inv).astype(BF16), vc, preferred_element_type=F32)
                    return o, m + jnp.log(den)

                o0, l0 = head(s0p, s0c)
                o1, l1 = head(s1p, s1c)
                o_scr[cur, :] = _pair(lo, o0, o1)
                l_ref[cur, :] = _pair(lo, l0, l1)
                return carry

            lax.fori_loop(0, nsub, step, 0)

        def merge(i, carry):
            rows = pl.ds(pl.multiple_of(i * mb, mb), mb)
            wa, wb, wc, _ = _softmax3(l1_ref[rows, :], l2_ref[rows, :], l3_ref[rows, :])
            mix_ref[rows, :] = wa * o1_scr[rows, :] + wb * o2_scr[rows, :] + wc * o3_scr[rows, :]
            return carry

        lax.fori_loop(0, s // mb, merge, 0)

    col = lambda arr_col: pl.BlockSpec((s, LANES), lambda h: (0, arr_col + h))
    shp = jax.ShapeDtypeStruct((s, D_MIX), F32)
    mix, l1, l2, l3 = pl.pallas_call(
        body, out_shape=(shp, shp, shp, shp), grid=(N_MIX_HP,), in_specs=[col(0), col(N_MIX_HP), col(2 * N_MIX_HP)],
        out_specs=(col(0),) * 4, scratch_shapes=[pltpu.VMEM((s, LANES), F32)] * 3,
        compiler_params=_params("parallel"), name=name)(qk_r, qk_r, proj)
    return mix, (l1, l2, l3)


def _dil_bwd(qk_r, proj, mix, dheads, lses, *, name):
    s = qk_r.shape[0]
    nsub = s // DIL_L
    mb = 512

    def body(q_ref, k_ref, v_ref, mix_ref, dm_ref, l1_ref, l2_ref, l3_ref, dq_ref, dk_ref, dv_ref, lt_scr, dd_scr):
        lo = _lane_lo((DIL_L, LANES))
        lo_m = _lane_lo((mb, LANES))

        def prep(i, carry):
            rows = pl.ds(pl.multiple_of(i * mb, mb), mb)
            _, _, _, lt = _softmax3(l1_ref[rows, :], l2_ref[rows, :], l3_ref[rows, :])
            lt_scr[rows, :] = lt
            dd = dm_ref[rows, :] * mix_ref[rows, :]
            dd_scr[rows, :] = _pair(lo_m, jnp.sum(jnp.where(lo_m, dd, 0.0), axis=1, keepdims=True),
                                    jnp.sum(jnp.where(lo_m, 0.0, dd), axis=1, keepdims=True))
            zero = jnp.zeros((mb, LANES), F32)
            dq_ref[rows, :] = zero
            dk_ref[rows, :] = zero
            dv_ref[rows, :] = zero
            return carry

        lax.fori_loop(0, s // mb, prep, 0)

        for (_, dil), l_ref in zip(DILATED_BRANCHES, (l1_ref, l2_ref, l3_ref)):
            def step(t, carry, dil=dil, l_ref=l_ref):
                cur, prev, has_prev = _dil_rows(t, dil)
                q = q_ref[cur, :]
                q0 = jnp.where(lo, q, 0.0).astype(BF16)
                q1 = jnp.where(lo, 0.0, q).astype(BF16)
                kp, kc = k_ref[prev, :], k_ref[cur, :]
                s0p, s0c, s1p, s1c = _dil_scores(q0, q1, kp.astype(BF16), kc.astype(BF16), has_prev)
                vp = v_ref[prev, :].astype(BF16)
                vc = v_ref[cur, :].astype(BF16)
                lg = l_ref[cur, :]
                w = jnp.exp(lg - lt_scr[cur, :])
                wd = w * dd_scr[cur, :]
                dog = w * dm_ref[cur, :]
                do0 = jnp.where(lo, dog, 0.0).astype(BF16)
                do1 = jnp.where(lo, 0.0, dog).astype(BF16)
                kp0, kp1 = jnp.where(lo, kp, 0.0).astype(BF16), jnp.where(lo, 0.0, kp).astype(BF16)
                kc0, kc1 = jnp.where(lo, kc, 0.0).astype(BF16), jnp.where(lo, 0.0, kc).astype(BF16)

                def head(sp, sc, lcol, do_h):
                    lse_h = lg[:, lcol:lcol + 1]
                    wd_h = wd[:, lcol:lcol + 1]
                    pp, pc = jnp.exp(sp - lse_h), jnp.exp(sc - lse_h)
                    dsp = pp * (lax.dot_general(do_h, vp, NT, preferred_element_type=F32) - wd_h)
                    dsc = pc * (lax.dot_general(do_h, vc, NT, preferred_element_type=F32) - wd_h)
                    return pp.astype(BF16), pc.astype(BF16), dsp.astype(BF16), dsc.astype(BF16)

                p0p, p0c, ds0p, ds0c = head(s0p, s0c, 0, do0)
                p1p, p1c, ds1p, ds1c = head(s1p, s1c, HEAD_DIM, do1)
                dot = lambda a, b: jnp.dot(a, b, preferred_element_type=F32)
                dott = lambda a, b: lax.dot_general(a, b, TN, preferred_element_type=F32)
                dq_ref[cur, :] += (dot(ds0p, kp0) + dot(ds0c, kc0) + dot(ds1p, kp1) + dot(ds1c, kc1)) * SCALE
                dk_ref[cur, :] += (dott(ds0c, q0) + dott(ds1c, q1)) * SCALE
                dv_ref[cur, :] += dott(p0c, do0) + dott(p1c, do1)
                dk_ref[prev, :] += (dott(ds0p, q0) + dott(ds1p, q1)) * SCALE
                dv_ref[prev, :] += dott(p0p, do0) + dott(p1p, do1)
                return carry

            lax.fori_loop(0, nsub, step, 0)

    col = lambda arr_col: pl.BlockSpec((s, LANES), lambda h: (0, arr_col + h))
    shp = jax.ShapeDtypeStruct((s, D_MIX), F32)
    return pl.pallas_call(
        body, out_shape=(shp, shp, shp), grid=(N_MIX_HP,),
        in_specs=[col(0), col(N_MIX_HP), col(2 * N_MIX_HP), col(0), col(0), col(0), col(0), col(0)], out_specs=(col(0),) * 3,
        scratch_shapes=[pltpu.VMEM((s, LANES), F32)] * 2,
        compiler_params=_params("parallel"), name=name)(qk_r, qk_r, proj, mix, dheads, *lses)


CONV_BR = 512
FF_CHUNK = 2 * D_FF // N_DEV
FF_HALF = N_DEV // 2
HALO = 8


def _shift_down(x, halo, k):
    row = lax.broadcasted_iota(jnp.int32, x.shape, 0)
    y = pltpu.roll(x, k, 0)
    for r in range(k):
        y = jnp.where(row == r, halo[HALO - k + r:HALO - k + r + 1, :], y)
    return y


def _shift_up(x, halo, k):
    n = x.shape[0]
    row = lax.broadcasted_iota(jnp.int32, x.shape, 0)
    y = pltpu.roll(x, n - k, 0)
    for r in range(k):
        y = jnp.where(row == n - k + r, halo[r:r + 1, :], y)
    return y


def _conv_vals(u, halo, w, b):
    s1 = _shift_down(u, halo, 1)
    s2 = _shift_down(u, halo, 2)
    return b + w[0:1, :] * s2 + w[1:2, :] * s1 + w[2:3, :] * u, s1, s2


def _conv_in_specs(order, layer):
    rc = (lambda i, j: (i, j)) if order == "rc" else (lambda j, i: (i, j))
    per = CONV_BR // HALO
    main = lambda off: pl.BlockSpec((None, CONV_BR, FF_CHUNK), lambda *g: (off + rc(*g)[1], rc(*g)[0], 0))
    halo = lambda off: pl.BlockSpec((None, HALO, FF_CHUNK), lambda *g: (off + rc(*g)[1], jnp.maximum(rc(*g)[0] * per - 1, 0), 0))
    wspec = lambda off: pl.BlockSpec((None, None, 3, FF_CHUNK), lambda *g: (layer, off + rc(*g)[1], 0, 0))
    bspec = lambda off: pl.BlockSpec((None, 1, FF_CHUNK), lambda *g: (off + rc(*g)[1], 0, 0))
    return [main(0), halo(0), main(FF_HALF), halo(FF_HALF), wspec(0), wspec(FF_HALF), bspec(0), bspec(FF_HALF)]


def _conv_fwd(u, cw, cb, layer, *, name):
    s = u.shape[1]

    def body(uv_ref, hv_ref, ug_ref, hg_ref, wv_ref, wg_ref, bv_ref, bg_ref, o_ref):
        first = pl.program_id(0) == 0
        hv = jnp.where(first, 0.0, hv_ref[...])
        hg = jnp.where(first, 0.0, hg_ref[...])
        val, _, _ = _conv_vals(uv_ref[...], hv, wv_ref[...], bv_ref[...])
        gate, _, _ = _conv_vals(ug_ref[...], hg, wg_ref[...], bg_ref[...])
        o_ref[...] = (gate / (1.0 + jnp.exp(-gate)) * val).astype(BF16)

    return pl.pallas_call(body, out_shape=jax.ShapeDtypeStruct((FF_HALF, s, FF_CHUNK), BF16), grid=(s // CONV_BR, FF_HALF),
                          in_specs=_conv_in_specs("rc", layer), out_specs=pl.BlockSpec((None, CONV_BR, FF_CHUNK), lambda i, j: (j, i, 0)),
                          compiler_params=_params("parallel", "parallel"), name=name)(u, u, u, u, cw, cw, cb, cb)


def _conv_bwd_dc(u, cw, cb, da, layer, *, name):
    s = u.shape[1]

    def body(uv_ref, hv_ref, ug_ref, hg_ref, wv_ref, wg_ref, bv_ref, bg_ref, da_ref, dc_ref, dwb_ref):
        first = pl.program_id(1) == 0
        hv = jnp.where(first, 0.0, hv_ref[...])
        hg = jnp.where(first, 0.0, hg_ref[...])
        uv, ug = uv_ref[...], ug_ref[...]
        val, v1, v2 = _conv_vals(uv, hv, wv_ref[...], bv_ref[...])
        gate, g1, g2 = _conv_vals(ug, hg, wg_ref[...], bg_ref[...])
        da = da_ref[...]
        sg = 1.0 / (1.0 + jnp.exp(-gate))
        dval = da * (gate * sg)
        dgate = da * val * (sg * (1.0 + gate * (1.0 - sg)))
        dc_ref[0] = dval
        dc_ref[1] = dgate

        @pl.when(first)
        def _():
            dwb_ref[...] = jnp.zeros_like(dwb_ref)

        cs = lambda t: jnp.sum(t, axis=0, keepdims=True)
        r8 = lax.broadcasted_iota(jnp.int32, (8, FF_CHUNK), 0)
        rows4 = lambda a, b, c, d: jnp.where(r8 == 0, a, jnp.where(r8 == 1, b, jnp.where(r8 == 2, c, jnp.where(r8 == 3, d, 0.0))))
        dwb_ref[0] += rows4(cs(dval * v2), cs(dval * v1), cs(dval * uv), cs(dval))
        dwb_ref[1] += rows4(cs(dgate * g2), cs(dgate * g1), cs(dgate * ug), cs(dgate))

    specs = _conv_in_specs("cr", layer) + [pl.BlockSpec((None, CONV_BR, FF_CHUNK), lambda j, i: (j, i, 0))]
    return pl.pallas_call(
        body, out_shape=(jax.ShapeDtypeStruct((2, FF_HALF, s, FF_CHUNK), F32), jax.ShapeDtypeStruct((2, FF_HALF, 8, FF_CHUNK), F32)),
        grid=(FF_HALF, s // CONV_BR), in_specs=specs,
        out_specs=(pl.BlockSpec((2, None, CONV_BR, FF_CHUNK), lambda j, i: (0, j, i, 0)),
                   pl.BlockSpec((2, None, 8, FF_CHUNK), lambda j, i: (0, j, 0, 0))),
        compiler_params=_params("parallel", "arbitrary"), name=name)(u, u, u, u, cw, cw, cb, cb, da)


def _conv_bwd_du(dc, cw, layer, *, name):
    n_chunk, s, _ = dc.shape
    nrow = s // CONV_BR
    per = CONV_BR // HALO

    def body(dc_ref, h_ref, w_ref, o_ref):
        last = pl.program_id(1) == nrow - 1
        h = jnp.where(last, 0.0, h_ref[...])
        x = dc_ref[...]
        w = w_ref[...]
        o_ref[...] = (w[2:3, :] * x + w[1:2, :] * _shift_up(x, h, 1) + w[0:1, :] * _shift_up(x, h, 2)).astype(BF16)

    return pl.pallas_call(
        body, out_shape=jax.ShapeDtypeStruct((n_chunk, s, FF_CHUNK), BF16), grid=(n_chunk, nrow),
        in_specs=[pl.BlockSpec((None, CONV_BR, FF_CHUNK), lambda c, i: (c, i, 0)),
                  pl.BlockSpec((None, HALO, FF_CHUNK), lambda c, i: (c, jnp.minimum((i + 1) * per, nrow * per - 1), 0)),
                  pl.BlockSpec((None, None, 3, FF_CHUNK), lambda c, i: (layer, c, 0, 0))],
        out_specs=pl.BlockSpec((None, CONV_BR, FF_CHUNK), lambda c, i: (c, i, 0)),
        compiler_params=_params("parallel", "parallel"), name=name)(dc, dc, cw)


def _rows_of(r):
    return lambda ref, idx: ref.at[:, pl.ds(idx * r, r), :]


def _slot1(ref, idx):
    return ref.at[:, idx]


def _slot0(ref, idx):
    return ref.at[idx]


def _all_gather(shards, full_shapes, places, *, name):
    n = len(shards)

    def body(*refs):
        ins, outs = refs[:n], refs[n:2 * n]
        send_sems, recv_sems, local_sems = refs[2 * n:]
        mx, my, mc = lax.axis_index("x"), lax.axis_index("y"), lax.axis_index("c")
        me, sibling = (mx, my, mc), (mx, my, 1 - mc)
        chips = [(1 - mx, my), (mx, 1 - my), (1 - mx, 1 - my)]

        def win(t, px, py, pc):
            return places[t](outs[t], 4 * px + 2 * py + pc)

        def copy(t, k, block, to, src=None):
            return pltpu.make_async_remote_copy(src_ref=win(t, *block) if src is None else src, dst_ref=win(t, *block),
                                                send_sem=send_sems.at[t, k], recv_sem=recv_sems.at[t, k], device_id=to, device_id_type=MESH)

        mine = [pltpu.make_async_copy(ins[t], win(t, *me), local_sems.at[t]) for t in range(n)]
        for cp in mine:
            cp.start()
        first = []
        for t in range(n):
            first += [copy(t, 0, me, sibling, src=ins[t])] + [copy(t, 1 + j, me, (*chip, mc), src=ins[t]) for j, chip in enumerate(chips)]
        for cp in first:
            cp.start()
        passed = []
        for j, chip in enumerate(chips):
            for t in range(n):
                copy(t, 1 + j, (*chip, mc), me).wait_recv()
                fwd = copy(t, 4 + j, (*chip, mc), sibling)
                fwd.start()
                passed.append(fwd)
        for t in range(n):
            copy(t, 0, sibling, me).wait_recv()
            for j, chip in enumerate(chips):
                copy(t, 4 + j, (*chip, 1 - mc), me).wait_recv()
        for cp in first + passed:
            cp.wait_send()
        for cp in mine:
            cp.wait()

    hbm = pl.BlockSpec(memory_space=pl.ANY)
    return pl.pallas_call(
        body, out_shape=tuple(jax.ShapeDtypeStruct(s, x.dtype) for s, x in zip(full_shapes, shards)),
        in_specs=[hbm] * n, out_specs=(hbm,) * n,
        scratch_shapes=[pltpu.SemaphoreType.DMA((n, 7)), pltpu.SemaphoreType.DMA((n, 7)), pltpu.SemaphoreType.DMA((n,))],
        name=name)(*shards)


def _all_to_all(fulls, shard_shapes, places, *, name):
    n = len(fulls)

    def body(*refs):
        ins, outs = refs[:n], refs[n:2 * n]
        send_sems, recv_sems, local_sems = refs[2 * n:]
        mx, my, mc = lax.axis_index("x"), lax.axis_index("y"), lax.axis_index("c")
        me = 4 * mx + 2 * my + mc
        mine = [pltpu.make_async_copy(places[t](ins[t], me), outs[t].at[me], local_sems.at[t]) for t in range(n)]
        for cp in mine:
            cp.start()
        flips = [(fx, fy, fc) for fx in (0, 1) for fy in (0, 1) for fc in (0, 1)][1:]
        copies = []
        for k, (fx, fy, fc) in enumerate(flips):
            px, py, pc = mx ^ fx, my ^ fy, mc ^ fc
            for t in range(n):
                cp = pltpu.make_async_remote_copy(src_ref=places[t](ins[t], 4 * px + 2 * py + pc), dst_ref=outs[t].at[me],
                                                  send_sem=send_sems.at[t, k], recv_sem=recv_sems.at[t, k], device_id=(px, py, pc),
                                                  device_id_type=MESH)
                cp.start()
                copies.append(cp)
        for k, (fx, fy, fc) in enumerate(flips):
            px, py, pc = mx ^ fx, my ^ fy, mc ^ fc
            for t in range(n):
                pltpu.make_async_remote_copy(src_ref=places[t](ins[t], me), dst_ref=outs[t].at[4 * px + 2 * py + pc],
                                             send_sem=send_sems.at[t, k], recv_sem=recv_sems.at[t, k], device_id=(px, py, pc),
                                             device_id_type=MESH).wait_recv()
        for cp in copies:
            cp.wait_send()
        for cp in mine:
            cp.wait()

    hbm = pl.BlockSpec(memory_space=pl.ANY)
    return pl.pallas_call(
        body, out_shape=tuple(jax.ShapeDtypeStruct((N_DEV,) + tuple(s), x.dtype) for s, x in zip(shard_shapes, fulls)),
        in_specs=[hbm] * n, out_specs=(hbm,) * n,
        scratch_shapes=[pltpu.SemaphoreType.DMA((n, 7)), pltpu.SemaphoreType.DMA((n, 7)), pltpu.SemaphoreType.DMA((n,))],
        name=name)(*fulls)


def _adamw(parts, w, m, v, *, br, name):
    layers, r, wd = w.shape

    def body(p_ref, w_ref, m_ref, v_ref, g_ref, d_ref, nm_ref, nv_ref):
        g = p_ref[0].astype(F32)
        for k in range(1, N_DEV):
            g = g + p_ref[k].astype(F32)
        mm = ADAM_B1 * m_ref[...] + (1.0 - ADAM_B1) * g
        vv = ADAM_B2 * v_ref[...] + (1.0 - ADAM_B2) * (g * g)
        m_hat = mm / (1.0 - ADAM_B1 ** ADAM_STEP)
        v_hat = vv / (1.0 - ADAM_B2 ** ADAM_STEP)
        g_ref[...] = g
        d_ref[...] = -ADAM_LR * (m_hat / (jnp.sqrt(v_hat) + ADAM_EPS) + ADAM_WD * w_ref[...])
        nm_ref[...] = mm
        nv_ref[...] = vv

    blk = pl.BlockSpec((None, br, wd), lambda l, i: (l, i, 0))
    shp = jax.ShapeDtypeStruct((layers, r, wd), F32)
    return pl.pallas_call(body, out_shape=(shp, shp, shp, shp), grid=(layers, r // br),
                          in_specs=[pl.BlockSpec((N_DEV, None, br, wd), lambda l, i: (0, l, i, 0)), blk, blk, blk], out_specs=(blk, blk, blk, blk),
                          compiler_params=_params("parallel", "parallel"), name=name)(parts, w, m, v)


SMALL = ("norm_mix", "norm_mem", "norm_ffn", "b_forget", "conv_b", "norm_final")


def _pack(tensors):
    flat = jnp.concatenate([t.reshape(-1) for t in tensors])
    rows = -(-flat.shape[0] // (PACK_W * PACK_ROW_ALIGN)) * PACK_ROW_ALIGN
    flat = jnp.pad(flat, (0, rows * PACK_W - flat.shape[0]))
    return flat.reshape(1, rows, PACK_W)


def _unpack(buf, shapes):
    flat = buf.reshape(-1)
    out, off = [], 0
    for shp in shapes:
        n = math.prod(shp)
        out.append(flat[off:off + n].reshape(tuple(shp)))
        off += n
    return out


def _fox_permute(w):
    pad = jnp.zeros(w.shape[:-1] + (FOX_P - FOX_IN,), w.dtype)
    return jnp.concatenate([w[..., :3 * D_MIX], w[..., 3 * D_MIX + N_MIX_HEADS:], w[..., 3 * D_MIX:3 * D_MIX + N_MIX_HEADS], pad], axis=-1)


def _fox_unpermute(w):
    return jnp.concatenate([w[..., :3 * D_MIX], w[..., DIL_IN:DIL_IN + N_MIX_HEADS], w[..., 3 * D_MIX:DIL_IN]], axis=-1)


def _bias_layouts(c):
    s = c.shape[0]
    ct = c[:, :N_MIX_HEADS].T.reshape(N_MIX_HP, 2, s)
    cq6 = jnp.repeat(ct, HEAD_DIM, axis=1).transpose(0, 2, 1)
    ck6 = jnp.pad(ct, ((0, 0), (0, 6), (0, 0)))
    return cq6, ck6


def _bias_grads(dcq6, dck6):
    s = dcq6.shape[1]
    dq = dcq6[:, :, ::HEAD_DIM].transpose(1, 0, 2).reshape(s, N_MIX_HEADS)
    dk = dck6[:, :2, :].reshape(N_MIX_HEADS, s).T
    pad = lambda t: jnp.pad(t, ((0, 0), (0, LANES - N_MIX_HEADS)))
    return pad(dq), pad(dk)


def _device_step(x, mem, target, wts, small):
    s = x.shape[0]
    mt = mem.shape[0]
    bq = 512
    cos_t, sin_t = _rope_tables(s)
    row = lambda t, l: t[l][None, :]
    saved = []
    h = x
    cb8 = small["conv_b"].reshape(DEPTH, N_DEV, 1, FF_CHUNK)
    for l in range(DEPTH):
        kind, slot = l % 2, l // 2
        w_in = wts["w_in_fox"] if kind == 0 else wts["w_in_dil"]
        xn = _rmsnorm_fwd(h, row(small["norm_mix"], l), br=512, name=f"norm_mix_fwd{l}")
        mn = _rmsnorm_fwd(mem, row(small["norm_mem"], l), br=mt, name=f"norm_mem_fwd{l}")
        proj = _mm(xn, w_in, "nn", tm=1024, tn=384 if kind == 0 else 512, layer=slot, name=f"in_proj{l}")
        kvm = _mm(mn, wts["w_mem_kv"], "nn", tm=mt, tn=512, layer=l, name=f"mem_kv{l}")
        st = dict(h=h, xn=xn, mn=mn, proj=proj, kvm=kvm)
        if kind == 0:
            b_pad = jnp.pad(small["b_forget"][slot], (0, LANES - N_MIX_HEADS))[None, :]
            c = _forget_cumsum(proj, b_pad, name=f"forget_cumsum{l}")
            cq6, ck6 = _bias_layouts(c)
            mix, lse = _attn_fwd(proj, proj, cq6, ck6, q_col=0, k_col=N_MIX_HP, v_col=2 * N_MIX_HP, n_hp=N_MIX_HP,
                                 causal=True, bq=bq, bk=bq, name=f"fox_fwd{l}")
            st.update(b_pad=b_pad, cq6=cq6, ck6=ck6, mix=mix, lse=lse)
        else:
            qk_r = _rope(proj, cos_t, sin_t, n_cols=2 * N_MIX_HP, out_dtype=F32, br=512, name=f"rope_fwd{l}")
            mix, lses = _dil_fwd(qk_r, proj, name=f"dil_fwd{l}")
            st.update(qk_r=qk_r, lses=lses, mix=mix)
        mo, lse_m = _attn_fwd(proj, kvm, None, None, q_col=QM_COL, k_col=0, v_col=N_MEM_HP, n_hp=N_MEM_HP,
                              causal=False, bq=bq, bk=mt, name=f"mem_fwd{l}")
        heads = jnp.concatenate([mix.astype(BF16), mo.astype(BF16)], axis=1)
        h1 = _mm(heads, wts["w_out"], "nn", tm=1024, tn=512, res=h, layer=l, name=f"out_proj{l}")
        xf = _rmsnorm_fwd(h1, row(small["norm_ffn"], l), br=512, name=f"norm_ffn_fwd{l}")
        u = _mm(xf, wts["w_up"], "nn", tm=1024, tn=FF_CHUNK, layer=l, chunk="b", name=f"up_proj{l}")
        a = _conv_fwd(u, wts["conv_w"], cb8[l], l, name=f"conv_fwd{l}")
        h = _mm(a, wts["w_down"], "nn", tm=1024, tn=512, res=h1, layer=l, chunk="reduce", name=f"down_proj{l}")
        st.update(mo=mo, lse_m=lse_m, heads=heads, h1=h1, xf=xf, u=u, a=a)
        saved.append(st)

    dh, dhb, dg_final, loss = _loss_head(h, target, small["norm_final"][None, :], br=512, name="loss_head")
    gw = {k: [None] * wts[k].shape[0] for k in ("w_in_fox", "w_in_dil", "w_mem_kv", "w_out", "w_up", "w_down", "conv_w")}
    gs = {k: [None] * DEPTH for k in ("norm_mix", "norm_mem", "norm_ffn", "conv_b")}
    gs["b_forget"] = [None] * 2
    for l in reversed(range(DEPTH)):
        st = saved[l]
        kind, slot = l % 2, l // 2
        da = _mm(dhb, wts["w_down"], "nt", tm=1024, tn=FF_CHUNK, layer=l, chunk="b", name=f"down_dx{l}")
        gw["w_down"][l] = _mm(st["a"], dhb, "tn", tm=FF_CHUNK, tn=512, out_dtype=BF16, chunk="a", name=f"down_dw{l}")
        dc, dwb = _conv_bwd_dc(st["u"], wts["conv_w"], cb8[l], da, l, name=f"conv_bwd_dc{l}")
        dwb = dwb.reshape(N_DEV, 8, FF_CHUNK)
        gw["conv_w"][l] = dwb
        gs["conv_b"][l] = dwb[:, 3, :].reshape(-1)
        du = _conv_bwd_du(dc.reshape(N_DEV, s, FF_CHUNK), wts["conv_w"], l, name=f"conv_bwd_du{l}")
        dxf = _mm(du, wts["w_up"], "nt", tm=512, tn=512, layer=l, chunk="reduce", name=f"up_dx{l}")
        gw["w_up"][l] = _mm(st["xf"], du, "tn", tm=512, tn=FF_CHUNK, out_dtype=BF16, chunk="b", name=f"up_dw{l}")
        dh1, dh1b, dgf = _rmsnorm_bwd(st["h1"], dxf, row(small["norm_ffn"], l), dh, br=512, name=f"norm_ffn_bwd{l}")
        gs["norm_ffn"][l] = dgf[0]
        dheads = _mm(dh1b, wts["w_out"], "nt", tm=1024, tn=512, layer=l, name=f"out_dx{l}")
        gw["w_out"][l] = _mm(st["heads"], dh1b, "tn", tm=512, tn=512, out_dtype=BF16, name=f"out_dw{l}")
        dqm, dkm, dvm = _attn_bwd(st["proj"], st["kvm"], st["mo"], dheads, st["lse_m"], None, None, q_col=QM_COL, k_col=0,
                                  v_col=N_MEM_HP, o_col=N_MIX_HP, n_hp=N_MEM_HP, causal=False, bq=bq, bk=mt, name=f"mem_bwd{l}")
        dkvm = jnp.concatenate([dkm, dvm], axis=1).astype(BF16)
        gw["w_mem_kv"][l] = _mm(st["mn"], dkvm, "tn", tm=512, tn=512, out_dtype=BF16, name=f"mem_kv_dw{l}")
        dmn = _mm(dkvm, wts["w_mem_kv"], "nt", tm=mt, tn=512, layer=l, name=f"mem_kv_dx{l}")
        _, _, dgm = _rmsnorm_bwd(mem, dmn, row(small["norm_mem"], l), None, br=mt, name=f"norm_mem_bwd{l}")
        gs["norm_mem"][l] = dgm[0]
        if kind == 0:
            dq, dk, dv, dcq6, dck6 = _attn_bwd(st["proj"], st["proj"], st["mix"], dheads, st["lse"], st["cq6"], st["ck6"], q_col=0,
                                               k_col=N_MIX_HP, v_col=2 * N_MIX_HP, o_col=0, n_hp=N_MIX_HP, causal=True,
                                               bq=bq, bk=bq, name=f"fox_bwd{l}")
            dcq, dck = _bias_grads(dcq6, dck6)
            dz, db = _forget_cumsum_bwd(st["proj"], st["b_pad"], dcq, dck, name=f"forget_cumsum_bwd{l}")
            gs["b_forget"][slot] = db[0, :N_MIX_HEADS]
            dproj = jnp.concatenate([dq.astype(BF16), dk.astype(BF16), dv.astype(BF16), dqm.astype(BF16), dz], axis=1)
            w_in = wts["w_in_fox"]
        else:
            dq_r, dk_r, dv = _dil_bwd(st["qk_r"], st["proj"], st["mix"], dheads, st["lses"], name=f"dil_bwd{l}")
            dq = _rope(dq_r, cos_t, -sin_t, n_cols=N_MIX_HP, out_dtype=BF16, br=512, name=f"rope_bwd_q{l}")
            dk = _rope(dk_r, cos_t, -sin_t, n_cols=N_MIX_HP, out_dtype=BF16, br=512, name=f"rope_bwd_k{l}")
            dproj = jnp.concatenate([dq, dk, dv.astype(BF16), dqm.astype(BF16)], axis=1)
            w_in = wts["w_in_dil"]
        dxn = _mm(dproj, w_in, "nt", tm=1024, tn=512, layer=slot, name=f"in_dx{l}")
        gw["w_in_fox" if kind == 0 else "w_in_dil"][slot] = _mm(st["xn"], dproj, "tn", tm=512, tn=384 if kind == 0 else 512,
                                                                 out_dtype=BF16, name=f"in_dw{l}")
        dh, dhb, dgx = _rmsnorm_bwd(st["h"], dxn, row(small["norm_mix"], l), dh1, br=512, name=f"norm_mix_bwd{l}")
        gs["norm_mix"][l] = dgx[0]

    grads_w = {k: jnp.stack(v) for k, v in gw.items()}
    grads_s = {k: jnp.stack(v) for k, v in gs.items()}
    grads_s["norm_final"] = dg_final[0]
    return loss[0, 0], dh, grads_w, grads_s


def kernel(x, mem, norm_mix, norm_mem, norm_ffn, w_in_fox, b_forget, w_in_dil, w_mem_kv, w_out, w_up, conv_w, conv_b, w_down, norm_final, loss_target, m_norm_mix, m_norm_mem, m_norm_ffn, m_w_in_fox, m_b_forget, m_w_in_dil, m_w_mem_kv, m_w_out, m_w_up, m_conv_w, m_conv_b, m_w_down, m_norm_final, v_norm_mix, v_norm_mem, v_norm_ffn, v_w_in_fox, v_b_forget, v_w_in_dil, v_w_mem_kv, v_w_out, v_w_up, v_conv_w, v_conv_b, v_w_down, v_norm_final):
    names = ["norm_mix", "norm_mem", "norm_ffn", "w_in_fox", "b_forget", "w_in_dil", "w_mem_kv", "w_out", "w_up", "conv_w", "conv_b",
             "w_down", "norm_final"]
    w = dict(zip(names, (norm_mix, norm_mem, norm_ffn, w_in_fox, b_forget, w_in_dil, w_mem_kv, w_out, w_up, conv_w, conv_b, w_down, norm_final)))
    m = dict(zip(names, (m_norm_mix, m_norm_mem, m_norm_ffn, m_w_in_fox, m_b_forget, m_w_in_dil, m_w_mem_kv, m_w_out, m_w_up, m_conv_w,
                         m_conv_b, m_w_down, m_norm_final)))
    v = dict(zip(names, (v_norm_mix, v_norm_mem, v_norm_ffn, v_w_in_fox, v_b_forget, v_w_in_dil, v_w_mem_kv, v_w_out, v_w_up, v_conv_w,
                         v_conv_b, v_w_down, v_norm_final)))
    big = ("w_in_fox", "w_in_dil", "w_mem_kv", "w_out", "w_up", "w_down", "conv_w")
    small_shapes = [w[k].shape for k in SMALL]
    n_slot, n_layer = w_in_fox.shape[0], w_out.shape[0]
    dil_c = w_in_dil.shape[2]
    rows = {k: w[k].shape[1] for k in ("w_in_fox", "w_mem_kv", "w_out", "w_down")}
    places = [_rows_of(rows["w_in_fox"]), _slot1, _rows_of(rows["w_mem_kv"]), _rows_of(rows["w_out"]), _slot1, _rows_of(rows["w_down"]), _slot1]

    shards = [_fox_permute(w_in_fox).astype(BF16), w_in_dil.astype(BF16), w_mem_kv.astype(BF16), w_out.astype(BF16), w_up.astype(BF16),
              w_down.astype(BF16), conv_w]
    shard_shapes = [t.shape for t in shards]
    full_shapes = [(n_slot, D_MODEL, FOX_P), (n_slot, N_DEV, D_MODEL, dil_c), (n_layer, D_MODEL, 2 * D_MEMQ), (n_layer, D_MODEL, D_MODEL),
                   (n_layer, N_DEV, D_MODEL, FF_CHUNK), (n_layer, D_FF, D_MODEL), (n_layer, N_DEV, 3, FF_CHUNK)]
    wfox, wdil8, wmem, wout, wup, wdown, cw = _all_gather(shards, full_shapes, places, name="weights_all_gather")
    full = {
        "w_in_fox": wfox,
        "w_in_dil": jnp.concatenate([wdil8[:, j] for j in range(N_DEV)], axis=-1),
        "w_mem_kv": wmem,
        "w_out": wout,
        "w_up": wup,
        "w_down": wdown.reshape(n_layer, FF_HALF, FF_CHUNK, D_MODEL),
        "conv_w": cw,
    }
    small = {k: w[k] for k in SMALL}

    loss, grad_x, gw, gs = _device_step(x[0], mem[0], loss_target[0], full, small)

    gdil8 = jnp.stack([gw["w_in_dil"][:, :, j * dil_c:(j + 1) * dil_c] for j in range(N_DEV)], axis=1)
    fulls = [gw["w_in_fox"], gdil8, gw["w_mem_kv"], gw["w_out"], gw["w_up"], gw["w_down"].reshape(n_layer, D_FF, D_MODEL), gw["conv_w"]]
    recv = list(_all_to_all(fulls, shard_shapes[:-1] + [(n_layer, 8, FF_CHUNK)], places, name="grads_all_to_all"))
    recv[-1] = recv[-1][:, :, :3, :]
    s_pack = _pack([gs[k] for k in SMALL])
    (s_recv,) = _all_gather([s_pack], [(N_DEV,) + s_pack.shape], [_slot0], name="small_grads_all_gather")

    to_local = {k: (lambda t: t) for k in big}
    to_local["w_in_fox"] = _fox_permute
    from_local = {k: (lambda t: t) for k in big}
    from_local["w_in_fox"] = _fox_unpermute
    blocks = {"w_in_fox": rows["w_in_fox"], "w_in_dil": 512, "w_mem_kv": rows["w_mem_kv"], "w_out": rows["w_out"], "w_up": 256,
              "w_down": rows["w_down"], "conv_w": 3}
    outs = {}
    for k, parts in zip(big, recv):
        f = to_local[k]
        outs[k] = [from_local[k](t) for t in _adamw(parts, f(w[k]), f(m[k]), f(v[k]), br=blocks[k], name=f"adamw_{k}")]
    small_outs = _adamw(s_recv, _pack([w[k] for k in SMALL]), _pack([m[k] for k in SMALL]), _pack([v[k] for k in SMALL]),
                        br=s_pack.shape[1], name="adamw_small")
    res = []
    for i, os_ in enumerate(small_outs):
        d = {k: outs[k][i] for k in big}
        d.update(zip(SMALL, _unpack(os_, small_shapes)))
        res.append([d[k] for k in names])
    loss = lax.psum(loss, ("x", "y", "c"))
    return (loss, grad_x[None], *res[0], *res[1], *res[2], *res[3])
```

```python
import functools
import math

import jax
import jax.numpy as jnp
from jax import lax
from jax.experimental import pallas as pl
from jax.experimental.pallas import tpu as pltpu

F32 = jnp.float32
BF16 = jnp.bfloat16

D_MODEL = 1024
HEAD_DIM = 64
N_MIX_HEADS = 12
N_MEM_HEADS = 4
D_MIX = N_MIX_HEADS * HEAD_DIM
D_MEMQ = N_MEM_HEADS * HEAD_DIM
D_FF = 2816
DEPTH = 4
FOX_IN = 3 * D_MIX + N_MIX_HEADS + D_MEMQ
DIL_IN = 3 * D_MIX + D_MEMQ
LANES = 128
FOX_P = DIL_IN + LANES
N_MIX_HP = D_MIX // LANES
N_MEM_HP = D_MEMQ // LANES
QM_COL = 3 * N_MIX_HP
F_COL = DIL_IN // LANES
DILATED_BRANCHES = ((128, 1), (512, 4), (2048, 16))
DIL_L = 128
ROPE_THETA = 10000.0
NORM_EPS = 1e-6
NEG = -1e30
SCALE = HEAD_DIM ** -0.5
N_DEV = 8

ADAM_LR = 0.001
ADAM_B1 = 0.9
ADAM_B2 = 0.999
ADAM_EPS = 1e-08
ADAM_WD = 0.01
ADAM_STEP = 10

VMEM_LIMIT = 56 * 1024 * 1024
PACK_W = 1024
PACK_ROW_ALIGN = 8

MESH = pl.DeviceIdType.MESH
NT = (((1,), (1,)), ((), ()))
NN = (((1,), (0,)), ((), ()))
TN = (((0,), (0,)), ((), ()))


def _params(*sem):
    return pltpu.CompilerParams(dimension_semantics=sem, vmem_limit_bytes=VMEM_LIMIT)


def _lane_lo(shape):
    return lax.broadcasted_iota(jnp.int32, shape, len(shape) - 1) < HEAD_DIM


def _pair(lo, a, b):
    return jnp.where(lo, a, b)


def _mm(a, b, mode, *, tm, tn, name, out_dtype=F32, res=None, layer=None, chunk=None):
    lead = () if layer is None else (layer,)
    nl = (None,) * len(lead)
    bs = b.shape[len(lead):]
    dims = {"nn": NN, "nt": NT, "tn": TN}[mode]
    reduce_n = 0
    if chunk is None:
        (m, k) = a.shape[::-1] if mode == "tn" else a.shape
        n = bs[0] if mode == "nt" else bs[1]
        grid = (m // tm, n // tn)
        a_spec = pl.BlockSpec((k, tm), lambda i, j: (0, i)) if mode == "tn" else pl.BlockSpec((tm, k), lambda i, j: (i, 0))
        b_spec = pl.BlockSpec(nl + ((tn, k) if mode == "nt" else (k, tn)), lambda i, j: lead + ((j, 0) if mode == "nt" else (0, j)))
        o_spec = pl.BlockSpec((tm, tn), lambda i, j: (i, j))
        out_shape = (m, n)
    elif chunk == "b":
        (m, k) = a.shape[::-1] if mode == "tn" else a.shape
        c, nc = bs[0], (bs[1] if mode == "nt" else bs[2])
        grid = (m // tm, c)
        a_spec = pl.BlockSpec((k, tm), lambda i, j: (0, i)) if mode == "tn" else pl.BlockSpec((tm, k), lambda i, j: (i, 0))
        b_spec = pl.BlockSpec(nl + (None,) + tuple(bs[1:]), lambda i, j: lead + (j, 0, 0))
        o_spec = pl.BlockSpec((None, tm, nc), lambda i, j: (j, i, 0))
        out_shape = (c, m, nc)
    elif chunk == "a":
        assert mode == "tn"
        c, k, mc = a.shape
        n = bs[1]
        grid = (c, n // tn)
        a_spec = pl.BlockSpec((None, k, mc), lambda i, j: (i, 0, 0))
        b_spec = pl.BlockSpec(nl + (k, tn), lambda i, j: lead + (0, j))
        o_spec = pl.BlockSpec((None, mc, tn), lambda i, j: (i, 0, j))
        out_shape = (c, mc, n)
    else:
        reduce_n, m, kc = a.shape
        n = bs[1] if mode == "nt" else bs[2]
        grid = (m // tm, n // tn)
        a_spec = pl.BlockSpec((reduce_n, tm, kc), lambda i, j: (0, i, 0))
        b_spec = pl.BlockSpec(nl + ((reduce_n, tn, kc) if mode == "nt" else (reduce_n, kc, tn)),
                              lambda i, j: lead + ((0, j, 0) if mode == "nt" else (0, 0, j)))
        o_spec = pl.BlockSpec((tm, tn), lambda i, j: (i, j))
        out_shape = (m, n)

    def body(*refs):
        a_ref, b_ref = refs[0], refs[1]
        o_ref = refs[-1]
        dot = lambda x, y: lax.dot_general(x.astype(BF16), y.astype(BF16), dims, preferred_element_type=F32)
        if reduce_n:
            acc = dot(a_ref[0], b_ref[0])
            for r in range(1, reduce_n):
                acc = acc + dot(a_ref[r], b_ref[r])
        else:
            acc = dot(a_ref[...], b_ref[...])
        if res is not None:
            acc = acc + refs[2][...]
        o_ref[...] = acc.astype(o_ref.dtype)

    ins = [a, b] + ([res] if res is not None else [])
    specs = [a_spec, b_spec] + ([o_spec] if res is not None else [])
    return pl.pallas_call(body, out_shape=jax.ShapeDtypeStruct(out_shape, out_dtype), grid=grid,
                          in_specs=specs, out_specs=o_spec, compiler_params=_params("parallel", "parallel"), name=name)(*ins)


def _rmsnorm_fwd(x, g, *, br, name):
    r, d = x.shape

    def body(x_ref, g_ref, o_ref):
        xf = x_ref[...]
        rs = lax.rsqrt(jnp.mean(xf * xf, axis=-1, keepdims=True) + NORM_EPS)
        o_ref[...] = (xf * rs * g_ref[...]).astype(BF16)

    return pl.pallas_call(body, out_shape=jax.ShapeDtypeStruct((r, d), BF16), grid=(r // br,),
                          in_specs=[pl.BlockSpec((br, d), lambda i: (i, 0)), pl.BlockSpec((1, d), lambda i: (0, 0))],
                          out_specs=pl.BlockSpec((br, d), lambda i: (i, 0)), compiler_params=_params("parallel"), name=name)(x, g)


def _rms_bwd_math(x, dy, g):
    d = x.shape[-1]
    rs = lax.rsqrt(jnp.mean(x * x, axis=-1, keepdims=True) + NORM_EPS)
    gy = dy * g
    proj = jnp.sum(x * gy, axis=-1, keepdims=True) * (1.0 / d)
    dx = rs * gy - x * (rs * rs * rs) * proj
    dg = jnp.sum(dy * (x * rs), axis=0, keepdims=True)
    return dx, dg


def _rmsnorm_bwd(x, dy, g, res, *, br, name):
    r, d = x.shape
    has_res = res is not None

    def body(*refs):
        x_ref, dy_ref, g_ref = refs[:3]
        dx_ref, dxb_ref, dg_ref = refs[-3:]
        dx, dg = _rms_bwd_math(x_ref[...], dy_ref[...], g_ref[...])
        if has_res:
            dx = dx + refs[3][...]
        dx_ref[...] = dx
        dxb_ref[...] = dx.astype(BF16)

        @pl.when(pl.program_id(0) == 0)
        def _():
            dg_ref[...] = jnp.zeros_like(dg_ref)

        dg_ref[0:1, :] += dg

    row = pl.BlockSpec((br, d), lambda i: (i, 0))
    ins = [x, dy, g] + ([res] if has_res else [])
    specs = [row, row, pl.BlockSpec((1, d), lambda i: (0, 0))] + ([row] if has_res else [])
    return pl.pallas_call(
        body, out_shape=(jax.ShapeDtypeStruct((r, d), F32), jax.ShapeDtypeStruct((r, d), BF16), jax.ShapeDtypeStruct((8, d), F32)),
        grid=(r // br,), in_specs=specs, out_specs=(row, row, pl.BlockSpec((8, d), lambda i: (0, 0))),
        compiler_params=_params("arbitrary"), name=name)(*ins)


def _loss_head(h, target, g, *, br, name):
    r, d = h.shape

    def body(x_ref, t_ref, g_ref, dx_ref, dxb_ref, dg_ref, loss_ref):
        x = x_ref[...]
        gg = g_ref[...]
        rs = lax.rsqrt(jnp.mean(x * x, axis=-1, keepdims=True) + NORM_EPS)
        err = x * rs * gg - t_ref[...]
        part = jnp.sum(jnp.sum(err * err, axis=1, keepdims=True), axis=0, keepdims=True) * (0.5 / d)
        dx, dg = _rms_bwd_math(x, err * (1.0 / d), gg)
        dx_ref[...] = dx
        dxb_ref[...] = dx.astype(BF16)

        @pl.when(pl.program_id(0) == 0)
        def _():
            dg_ref[...] = jnp.zeros_like(dg_ref)
            loss_ref[...] = jnp.zeros_like(loss_ref)

        dg_ref[0:1, :] += dg
        loss_ref[...] += jnp.broadcast_to(part, loss_ref.shape)

    row = pl.BlockSpec((br, d), lambda i: (i, 0))
    return pl.pallas_call(
        body, out_shape=(jax.ShapeDtypeStruct((r, d), F32), jax.ShapeDtypeStruct((r, d), BF16),
                         jax.ShapeDtypeStruct((8, d), F32), jax.ShapeDtypeStruct((8, LANES), F32)),
        grid=(r // br,), in_specs=[row, row, pl.BlockSpec((1, d), lambda i: (0, 0))],
        out_specs=(row, row, pl.BlockSpec((8, d), lambda i: (0, 0)), pl.BlockSpec((8, LANES), lambda i: (0, 0))),
        compiler_params=_params("arbitrary"), name=name)(h, target, g)


def _split3(x):
    hi = x.astype(BF16)
    r1 = x - hi.astype(F32)
    mid = r1.astype(BF16)
    lo = (r1 - mid.astype(F32)).astype(BF16)
    return hi, mid, lo


def _tri_sum(tri, x):
    hi, mid, lo = _split3(x)
    dot = lambda t: jnp.dot(tri, t, preferred_element_type=F32)
    return dot(hi) + dot(mid) + dot(lo)


def _forget_cumsum(proj, b_pad, *, name):
    s = proj.shape[0]
    blk = LANES

    def body(f_ref, b_ref, c_ref):
        ri = lax.broadcasted_iota(jnp.int32, (blk, blk), 0)
        ci = lax.broadcasted_iota(jnp.int32, (blk, blk), 1)
        tri = (ci <= ri).astype(BF16)
        bias = b_ref[...]

        def step(t, carry):
            rows = pl.ds(pl.multiple_of(t * blk, blk), blk)
            z = f_ref[rows, :] + bias
            lf = jnp.minimum(z, 0.0) - jnp.log(1.0 + jnp.exp(-jnp.abs(z)))
            cs = _tri_sum(tri, lf) + carry
            c_ref[rows, :] = cs
            return cs[blk - 1:blk, :]

        lax.fori_loop(0, s // blk, step, jnp.zeros((1, blk), F32))

    return pl.pallas_call(body, out_shape=jax.ShapeDtypeStruct((s, LANES), F32), grid=(1,),
                          in_specs=[pl.BlockSpec((s, LANES), lambda i: (0, F_COL)), pl.BlockSpec((1, LANES), lambda i: (0, 0))],
                          out_specs=pl.BlockSpec((s, LANES), lambda i: (0, 0)), compiler_params=_params("arbitrary"), name=name)(proj, b_pad)


def _forget_cumsum_bwd(proj, b_pad, dcq, dck, *, name):
    s = proj.shape[0]
    blk = LANES
    nblk = s // blk

    def body(f_ref, b_ref, dcq_ref, dck_ref, dz_ref, db_ref):
        ri = lax.broadcasted_iota(jnp.int32, (blk, blk), 0)
        ci = lax.broadcasted_iota(jnp.int32, (blk, blk), 1)
        triu = (ci >= ri).astype(BF16)
        bias = b_ref[...]

        def step(t, carry):
            tail, dbs = carry
            rows = pl.ds(pl.multiple_of((nblk - 1 - t) * blk, blk), blk)
            dc = dcq_ref[rows, :] - dck_ref[rows, :]
            dlf = _tri_sum(triu, dc) + tail
            z = f_ref[rows, :] + bias
            e = jnp.exp(-jnp.abs(z))
            sig_neg = jnp.where(z >= 0.0, e, 1.0) / (1.0 + e)
            dz = dlf * sig_neg
            dz_ref[rows, :] = dz.astype(BF16)
            return dlf[0:1, :], dbs + jnp.sum(dz, axis=0, keepdims=True)

        _, dbs = lax.fori_loop(0, nblk, step, (jnp.zeros((1, blk), F32), jnp.zeros((1, blk), F32)))
        db_ref[...] = jnp.broadcast_to(dbs, db_ref.shape)

    full = pl.BlockSpec((s, LANES), lambda i: (0, 0))
    return pl.pallas_call(body, out_shape=(jax.ShapeDtypeStruct((s, LANES), BF16), jax.ShapeDtypeStruct((8, LANES), F32)), grid=(1,),
                          in_specs=[pl.BlockSpec((s, LANES), lambda i: (0, F_COL)), pl.BlockSpec((1, LANES), lambda i: (0, 0)), full, full],
                          out_specs=(full, pl.BlockSpec((8, LANES), lambda i: (0, 0))), compiler_params=_params("arbitrary"), name=name)(proj, b_pad, dcq, dck)


def _attn_fwd(q_arr, kv_arr, cq6, ck6, *, q_col, k_col, v_col, n_hp, causal, bq, bk, name):
    s = q_arr.shape[0]
    skv = kv_arr.shape[0]
    bias = cq6 is not None
    nq = s // bq

    def body(*refs):
        q_ref, k_ref, v_ref = refs[:3]
        o_ref, lse_ref = refs[-2:]
        i = pl.program_id(1)
        lo = _lane_lo((bq, LANES))
        q = q_ref[...]
        q0 = jnp.where(lo, q, 0.0).astype(BF16)
        q1 = jnp.where(lo, 0.0, q).astype(BF16)
        if bias:
            cq = refs[3][0]
            cq0, cq1 = cq[:, 0:1], cq[:, HEAD_DIM:HEAD_DIM + 1]
            ck_ref = refs[4]
        if causal:
            rowpos = i * bq + lax.broadcasted_iota(jnp.int32, (bq, bk), 0)
            colid = lax.broadcasted_iota(jnp.int32, (bq, bk), 1)

        def step(j, carry):
            m0, l0, a0, m1, l1, a1 = carry
            ks = pl.ds(pl.multiple_of(j * bk, bk), bk)
            k = k_ref[ks, :].astype(BF16)
            v = v_ref[ks, :].astype(BF16)
            s0 = lax.dot_general(q0, k, NT, preferred_element_type=F32) * SCALE
            s1 = lax.dot_general(q1, k, NT, preferred_element_type=F32) * SCALE
            if bias:
                s0 = s0 + cq0 - ck_ref[0, 0:1, ks]
                s1 = s1 + cq1 - ck_ref[0, 1:2, ks]
            if causal:
                ok = (j * bk + colid) <= rowpos
                s0 = jnp.where(ok, s0, NEG)
                s1 = jnp.where(ok, s1, NEG)

            def upd(sc, m, l, a):
                mn = jnp.maximum(m, jnp.max(sc, axis=1, keepdims=True))
                p = jnp.exp(sc - mn)
                al = jnp.exp(m - mn)
                return mn, al * l + jnp.sum(p, axis=1, keepdims=True), al * a + jnp.dot(p.astype(BF16), v, preferred_element_type=F32)

            m0, l0, a0 = upd(s0, m0, l0, a0)
            m1, l1, a1 = upd(s1, m1, l1, a1)
            return m0, l0, a0, m1, l1, a1

        nk = ((i + 1) * bq + bk - 1) // bk if causal else skv // bk
        col = lambda v_: jnp.full((bq, 1), v_, F32)
        init = (col(NEG), col(0.0), jnp.zeros((bq, LANES), F32), col(NEG), col(0.0), jnp.zeros((bq, LANES), F32))
        m0, l0, a0, m1, l1, a1 = lax.fori_loop(0, nk, step, init)
        o_ref[...] = _pair(lo, a0 / l0, a1 / l1)
        lse_ref[0] = _pair(lo, m0 + jnp.log(l0), m1 + jnp.log(l1))

    specs = [pl.BlockSpec((bq, LANES), lambda h, i: (i, q_col + h)),
             pl.BlockSpec((skv, LANES), lambda h, i: (0, k_col + h)),
             pl.BlockSpec((skv, LANES), lambda h, i: (0, v_col + h))]
    ins = [q_arr, kv_arr, kv_arr]
    if bias:
        specs += [pl.BlockSpec((1, bq, LANES), lambda h, i: (h, i, 0)), pl.BlockSpec((1, 8, skv), lambda h, i: (h, 0, 0))]
        ins += [cq6, ck6]
    return pl.pallas_call(
        body, out_shape=(jax.ShapeDtypeStruct((s, n_hp * LANES), F32), jax.ShapeDtypeStruct((n_hp, s, LANES), F32)),
        grid=(n_hp, nq), in_specs=specs,
        out_specs=(pl.BlockSpec((bq, LANES), lambda h, i: (i, h)), pl.BlockSpec((1, bq, LANES), lambda h, i: (h, i, 0))),
        compiler_params=_params("parallel", "parallel"), name=name)(*ins)


def _attn_bwd(q_arr, kv_arr, o_arr, do_arr, lse, cq6, ck6, *, q_col, k_col, v_col, o_col, n_hp, causal, bq, bk, name):
    s = q_arr.shape[0]
    skv = kv_arr.shape[0]
    bias = cq6 is not None
    nq = s // bq

    def body(*refs):
        q_ref, k_ref, v_ref, o_ref, do_ref, lse_ref = refs[:6]
        if bias:
            cq_ref, ck_ref = refs[6:8]
            dq_ref, dk_ref, dv_ref, dcq_ref, dck_ref = refs[-5:]
        else:
            dq_ref, dk_ref, dv_ref = refs[-3:]
        j = pl.program_id(1)
        lo_q = _lane_lo((bq, LANES))
        lo_k = _lane_lo((bk, LANES))
        k = k_ref[...]
        v = v_ref[...].astype(BF16)
        kb = k.astype(BF16)
        k0 = jnp.where(lo_k, k, 0.0).astype(BF16)
        k1 = jnp.where(lo_k, 0.0, k).astype(BF16)
        if bias:
            ck0 = ck_ref[0, 0:1, :]
            ck1 = ck_ref[0, 1:2, :]
        if causal:
            colpos = j * bk + lax.broadcasted_iota(jnp.int32, (bq, bk), 1)
            rowid = lax.broadcasted_iota(jnp.int32, (bq, bk), 0)

        @pl.when(j == 0)
        def _():
            dq_ref[...] = jnp.zeros_like(dq_ref)
            if bias:
                dcq_ref[...] = jnp.zeros_like(dcq_ref)

        def step(i, carry):
            dk_acc, dv_acc, cs0, cs1 = carry
            qs = pl.ds(pl.multiple_of(i * bq, bq), bq)
            q = q_ref[qs, :]
            do = do_ref[qs, :]
            dd = do * o_ref[qs, :]
            lse_i = lse_ref[0, qs, :]
            q0 = jnp.where(lo_q, q, 0.0).astype(BF16)
            q1 = jnp.where(lo_q, 0.0, q).astype(BF16)
            do0 = jnp.where(lo_q, do, 0.0).astype(BF16)
            do1 = jnp.where(lo_q, 0.0, do).astype(BF16)
            d0 = jnp.sum(jnp.where(lo_q, dd, 0.0), axis=1, keepdims=True)
            d1 = jnp.sum(jnp.where(lo_q, 0.0, dd), axis=1, keepdims=True)
            s0 = lax.dot_general(q0, kb, NT, preferred_element_type=F32) * SCALE
            s1 = lax.dot_general(q1, kb, NT, preferred_element_type=F32) * SCALE
            if bias:
                cq = cq_ref[0, qs, :]
                s0 = s0 + cq[:, 0:1] - ck0
                s1 = s1 + cq[:, HEAD_DIM:HEAD_DIM + 1] - ck1
            if causal:
                ok = colpos <= (i * bq + rowid)
                s0 = jnp.where(ok, s0, NEG)
                s1 = jnp.where(ok, s1, NEG)
            p0 = jnp.exp(s0 - lse_i[:, 0:1])
            p1 = jnp.exp(s1 - lse_i[:, HEAD_DIM:HEAD_DIM + 1])
            dp0 = lax.dot_general(do0, v, NT, preferred_element_type=F32)
            dp1 = lax.dot_general(do1, v, NT, preferred_element_type=F32)
            ds0 = p0 * (dp0 - d0)
            ds1 = p1 * (dp1 - d1)
            ds0b, ds1b = ds0.astype(BF16), ds1.astype(BF16)
            dv_acc = dv_acc + lax.dot_general(p0.astype(BF16), do0, TN, preferred_element_type=F32) \
                + lax.dot_general(p1.astype(BF16), do1, TN, preferred_element_type=F32)
            dk_acc = dk_acc + lax.dot_general(ds0b, q0, TN, preferred_element_type=F32) \
                + lax.dot_general(ds1b, q1, TN, preferred_element_type=F32)
            dq_ref[qs, :] += (jnp.dot(ds0b, k0, preferred_element_type=F32) + jnp.dot(ds1b, k1, preferred_element_type=F32)) * SCALE
            if bias:
                dcq_ref[0, qs, :] += _pair(lo_q, jnp.sum(ds0, axis=1, keepdims=True), jnp.sum(ds1, axis=1, keepdims=True))
                cs0 = cs0 + jnp.sum(ds0, axis=0, keepdims=True)
                cs1 = cs1 + jnp.sum(ds1, axis=0, keepdims=True)
            return dk_acc, dv_acc, cs0, cs1

        i0 = (j * bk) // bq if causal else 0
        init = (jnp.zeros((bk, LANES), F32), jnp.zeros((bk, LANES), F32), jnp.zeros((1, bk), F32), jnp.zeros((1, bk), F32))
        dk_acc, dv_acc, cs0, cs1 = lax.fori_loop(i0, nq, step, init)
        dk_ref[...] = dk_acc * SCALE
        dv_ref[...] = dv_acc
        if bias:
            r8 = lax.broadcasted_iota(jnp.int32, (8, bk), 0)
            dck_ref[0] = jnp.where(r8 == 0, cs0, jnp.where(r8 == 1, cs1, 0.0))

    full_q = lambda c: pl.BlockSpec((s, LANES), lambda h, j: (0, c + h))
    specs = [full_q(q_col),
             pl.BlockSpec((bk, LANES), lambda h, j: (j, k_col + h)),
             pl.BlockSpec((bk, LANES), lambda h, j: (j, v_col + h)),
             full_q(0), full_q(o_col),
             pl.BlockSpec((1, s, LANES), lambda h, j: (h, 0, 0))]
    ins = [q_arr, kv_arr, kv_arr, o_arr, do_arr, lse]
    out_shape = [jax.ShapeDtypeStruct((s, n_hp * LANES), F32), jax.ShapeDtypeStruct((skv, n_hp * LANES), F32),
                 jax.ShapeDtypeStruct((skv, n_hp * LANES), F32)]
    out_specs = [full_q(0), pl.BlockSpec((bk, LANES), lambda h, j: (j, h)), pl.BlockSpec((bk, LANES), lambda h, j: (j, h))]
    if bias:
        specs += [pl.BlockSpec((1, s, LANES), lambda h, j: (h, 0, 0)), pl.BlockSpec((1, 8, bk), lambda h, j: (h, 0, j))]
        ins += [cq6, ck6]
        out_shape += [jax.ShapeDtypeStruct((n_hp, s, LANES), F32), jax.ShapeDtypeStruct((n_hp, 8, skv), F32)]
        out_specs += [pl.BlockSpec((1, s, LANES), lambda h, j: (h, 0, 0)), pl.BlockSpec((1, 8, bk), lambda h, j: (h, 0, j))]
    return pl.pallas_call(body, out_shape=tuple(out_shape), grid=(n_hp, skv // bk), in_specs=specs, out_specs=tuple(out_specs),
                          compiler_params=_params("parallel", "arbitrary"), name=name)(*ins)


def _rope_tables(s):
    inv = 1.0 / (ROPE_THETA ** (jnp.arange(0, HEAD_DIM, 2, dtype=F32) / HEAD_DIM))
    ang = jnp.arange(s, dtype=F32)[:, None] * inv[None, :]
    cos, sin = jnp.cos(ang), jnp.sin(ang)
    return jnp.tile(cos, (1, 4)), jnp.concatenate([-sin, sin, -sin, sin], axis=1)


def _rope(x_arr, cos_t, sin_t, *, n_cols, out_dtype, br, name):
    s = x_arr.shape[0]

    def body(x_ref, c_ref, s_ref, o_ref):
        x = x_ref[...].astype(F32)
        first = (lax.broadcasted_iota(jnp.int32, x.shape, 1) % HEAD_DIM) < (HEAD_DIM // 2)
        swapped = jnp.where(first, pltpu.roll(x, LANES - HEAD_DIM // 2, 1), pltpu.roll(x, HEAD_DIM // 2, 1))
        o_ref[...] = (x * c_ref[...] + swapped * s_ref[...]).astype(o_ref.dtype)

    tab = pl.BlockSpec((br, LANES), lambda i, j: (i, 0))
    blk = pl.BlockSpec((br, LANES), lambda i, j: (i, j))
    return pl.pallas_call(body, out_shape=jax.ShapeDtypeStruct((s, n_cols * LANES), out_dtype), grid=(s // br, n_cols),
                          in_specs=[blk, tab, tab], out_specs=blk, compiler_params=_params("parallel", "parallel"), name=name)(x_arr, cos_t, sin_t)


def _dil_scores(q0, q1, kp, kc, has_prev):
    a = lax.broadcasted_iota(jnp.int32, (DIL_L, DIL_L), 0)
    c = lax.broadcasted_iota(jnp.int32, (DIL_L, DIL_L), 1)
    ok_p, ok_c = (c >= a) & has_prev, c <= a
    sc = lambda q, k, ok: jnp.where(ok, lax.dot_general(q, k, NT, preferred_element_type=F32) * SCALE, NEG)
    return sc(q0, kp, ok_p), sc(q0, kc, ok_c), sc(q1, kp, ok_p), sc(q1, kc, ok_c)


def _dil_rows(t, dil):
    r, m = t % dil, t // dil
    start = m * (DIL_L * dil) + r
    prev = jnp.maximum(start - DIL_L * dil, 0)
    return pl.ds(start, DIL_L, stride=dil), pl.ds(prev, DIL_L, stride=dil), m > 0


def _softmax3(a, b, c):
    m = jnp.maximum(jnp.maximum(a, b), c)
    ea, eb, ec = jnp.exp(a - m), jnp.exp(b - m), jnp.exp(c - m)
    den = ea + eb + ec
    inv = 1.0 / den
    return ea * inv, eb * inv, ec * inv, m + jnp.log(den)


def _dil_fwd(qk_r, proj, *, name):
    s = qk_r.shape[0]
    nsub = s // DIL_L
    mb = 512

    def body(q_ref, k_ref, v_ref, mix_ref, l1_ref, l2_ref, l3_ref, o1_scr, o2_scr, o3_scr):
        lo = _lane_lo((DIL_L, LANES))
        for (_, dil), o_scr, l_ref in zip(DILATED_BRANCHES, (o1_scr, o2_scr, o3_scr), (l1_ref, l2_ref, l3_ref)):
            def step(t, carry, dil=dil, o_scr=o_scr, l_ref=l_ref):
                cur, prev, has_prev = _dil_rows(t, dil)
                q = q_ref[cur, :]
                q0 = jnp.where(lo, q, 0.0).astype(BF16)
                q1 = jnp.where(lo, 0.0, q).astype(BF16)
                s0p, s0c, s1p, s1c = _dil_scores(q0, q1, k_ref[prev, :].astype(BF16), k_ref[cur, :].astype(BF16), has_prev)
                vp = v_ref[prev, :].astype(BF16)
                vc = v_ref[cur, :].astype(BF16)

                def head(sp, sc):
                    m = jnp.maximum(jnp.max(sp, axis=1, keepdims=True), jnp.max(sc, axis=1, keepdims=True))
                    ep, ec = jnp.exp(sp - m), jnp.exp(sc - m)
                    den = jnp.sum(ep, axis=1, keepdims=True) + jnp.sum(ec, axis=1, keepdims=True)
                    inv = 1.0 / den
                    o = jnp.dot((ep * inv).astype(BF16), vp, preferred_element_type=F32) \
                        + jnp.dot((ec * inv).astype(BF16), vc, preferred_element_type=F32)
                    return o, m + jnp.log(den)

                o0, l0 = head(s0p, s0c)
                o1, l1 = head(s1p, s1c)
                o_scr[cur, :] = _pair(lo, o0, o1)
                l_ref[cur, :] = _pair(lo, l0, l1)
                return carry

            lax.fori_loop(0, nsub, step, 0)

        def merge(i, carry):
            rows = pl.ds(pl.multiple_of(i * mb, mb), mb)
            wa, wb, wc, _ = _softmax3(l1_ref[rows, :], l2_ref[rows, :], l3_ref[rows, :])
            mix_ref[rows, :] = wa * o1_scr[rows, :] + wb * o2_scr[rows, :] + wc * o3_scr[rows, :]
            return carry

        lax.fori_loop(0, s // mb, merge, 0)

    col = lambda arr_col: pl.BlockSpec((s, LANES), lambda h: (0, arr_col + h))
    shp = jax.ShapeDtypeStruct((s, D_MIX), F32)
    mix, l1, l2, l3 = pl.pallas_call(
        body, out_shape=(shp, shp, shp, shp), grid=(N_MIX_HP,), in_specs=[col(0), col(N_MIX_HP), col(2 * N_MIX_HP)],
        out_specs=(col(0),) * 4, scratch_shapes=[pltpu.VMEM((s, LANES), F32)] * 3,
        compiler_params=_params("parallel"), name=name)(qk_r, qk_r, proj)
    return mix, (l1, l2, l3)


def _dil_bwd(qk_r, proj, mix, dheads, lses, *, name):
    s = qk_r.shape[0]
    nsub = s // DIL_L
    mb = 512

    def body(q_ref, k_ref, v_ref, mix_ref, dm_ref, l1_ref, l2_ref, l3_ref, dq_ref, dk_ref, dv_ref, lt_scr, dd_scr):
        lo = _lane_lo((DIL_L, LANES))
        lo_m = _lane_lo((mb, LANES))

        def prep(i, carry):
            rows = pl.ds(pl.multiple_of(i * mb, mb), mb)
            _, _, _, lt = _softmax3(l1_ref[rows, :], l2_ref[rows, :], l3_ref[rows, :])
            lt_scr[rows, :] = lt
            dd = dm_ref[rows, :] * mix_ref[rows, :]
            dd_scr[rows, :] = _pair(lo_m, jnp.sum(jnp.where(lo_m, dd, 0.0), axis=1, keepdims=True),
                                    jnp.sum(jnp.where(lo_m, 0.0, dd), axis=1, keepdims=True))
            zero = jnp.zeros((mb, LANES), F32)
            dq_ref[rows, :] = zero
            dk_ref[rows, :] = zero
            dv_ref[rows, :] = zero
            return carry

        lax.fori_loop(0, s // mb, prep, 0)

        for (_, dil), l_ref in zip(DILATED_BRANCHES, (l1_ref, l2_ref, l3_ref)):
            def step(t, carry, dil=dil, l_ref=l_ref):
                cur, prev, has_prev = _dil_rows(t, dil)
                q = q_ref[cur, :]
                q0 = jnp.where(lo, q, 0.0).astype(BF16)
                q1 = jnp.where(lo, 0.0, q).astype(BF16)
                kp, kc = k_ref[prev, :], k_ref[cur, :]
                s0p, s0c, s1p, s1c = _dil_scores(q0, q1, kp.astype(BF16), kc.astype(BF16), has_prev)
                vp = v_ref[prev, :].astype(BF16)
                vc = v_ref[cur, :].astype(BF16)
                lg = l_ref[cur, :]
                w = jnp.exp(lg - lt_scr[cur, :])
                wd = w * dd_scr[cur, :]
                dog = w * dm_ref[cur, :]
                do0 = jnp.where(lo, dog, 0.0).astype(BF16)
                do1 = jnp.where(lo, 0.0, dog).astype(BF16)
                kp0, kp1 = jnp.where(lo, kp, 0.0).astype(BF16), jnp.where(lo, 0.0, kp).astype(BF16)
                kc0, kc1 = jnp.where(lo, kc, 0.0).astype(BF16), jnp.where(lo, 0.0, kc).astype(BF16)

                def head(sp, sc, lcol, do_h):
                    lse_h = lg[:, lcol:lcol + 1]
                    wd_h = wd[:, lcol:lcol + 1]
                    pp, pc = jnp.exp(sp - lse_h), jnp.exp(sc - lse_h)
                    dsp = pp * (lax.dot_general(do_h, vp, NT, preferred_element_type=F32) - wd_h)
                    dsc = pc * (lax.dot_general(do_h, vc, NT, preferred_element_type=F32) - wd_h)
                    return pp.astype(BF16), pc.astype(BF16), dsp.astype(BF16), dsc.astype(BF16)

                p0p, p0c, ds0p, ds0c = head(s0p, s0c, 0, do0)
                p1p, p1c, ds1p, ds1c = head(s1p, s1c, HEAD_DIM, do1)
                dot = lambda a, b: jnp.dot(a, b, preferred_element_type=F32)
                dott = lambda a, b: lax.dot_general(a, b, TN, preferred_element_type=F32)
                dq_ref[cur, :] += (dot(ds0p, kp0) + dot(ds0c, kc0) + dot(ds1p, kp1) + dot(ds1c, kc1)) * SCALE
                dk_ref[cur, :] += (dott(ds0c, q0) + dott(ds1c, q1)) * SCALE
                dv_ref[cur, :] += dott(p0c, do0) + dott(p1c, do1)
                dk_ref[prev, :] += (dott(ds0p, q0) + dott(ds1p, q1)) * SCALE
                dv_ref[prev, :] += dott(p0p, do0) + dott(p1p, do1)
                return carry

            lax.fori_loop(0, nsub, step, 0)

    col = lambda arr_col: pl.BlockSpec((s, LANES), lambda h: (0, arr_col + h))
    shp = jax.ShapeDtypeStruct((s, D_MIX), F32)
    return pl.pallas_call(
        body, out_shape=(shp, shp, shp), grid=(N_MIX_HP,),
        in_specs=[col(0), col(N_MIX_HP), col(2 * N_MIX_HP), col(0), col(0), col(0), col(0), col(0)], out_specs=(col(0),) * 3,
        scratch_shapes=[pltpu.VMEM((s, LANES), F32)] * 2,
        compiler_params=_params("parallel"), name=name)(qk_r, qk_r, proj, mix, dheads, *lses)


CONV_BR = 512
FF_CHUNK = 2 * D_FF // N_DEV
FF_HALF = N_DEV // 2
HALO = 8


def _shift_down(x, halo, k):
    row = lax.broadcasted_iota(jnp.int32, x.shape, 0)
    y = pltpu.roll(x, k, 0)
    for r in range(k):
        y = jnp.where(row == r, halo[HALO - k + r:HALO - k + r + 1, :], y)
    return y


def _shift_up(x, halo, k):
    n = x.shape[0]
    row = lax.broadcasted_iota(jnp.int32, x.shape, 0)
    y = pltpu.roll(x, n - k, 0)
    for r in range(k):
        y = jnp.where(row == n - k + r, halo[r:r + 1, :], y)
    return y


def _conv_vals(u, halo, w, b):
    s1 = _shift_down(u, halo, 1)
    s2 = _shift_down(u, halo, 2)
    return b + w[0:1, :] * s2 + w[1:2, :] * s1 + w[2:3, :] * u, s1, s2


def _conv_in_specs(order, layer):
    rc = (lambda i, j: (i, j)) if order == "rc" else (lambda j, i: (i, j))
    per = CONV_BR // HALO
    main = lambda off: pl.BlockSpec((None, CONV_BR, FF_CHUNK), lambda *g: (off + rc(*g)[1], rc(*g)[0], 0))
    halo = lambda off: pl.BlockSpec((None, HALO, FF_CHUNK), lambda *g: (off + rc(*g)[1], jnp.maximum(rc(*g)[0] * per - 1, 0), 0))
    wspec = lambda off: pl.BlockSpec((None, None, 3, FF_CHUNK), lambda *g: (layer, off + rc(*g)[1], 0, 0))
    bspec = lambda off: pl.BlockSpec((None, 1, FF_CHUNK), lambda *g: (off + rc(*g)[1], 0, 0))
    return [main(0), halo(0), main(FF_HALF), halo(FF_HALF), wspec(0), wspec(FF_HALF), bspec(0), bspec(FF_HALF)]


def _conv_fwd(u, cw, cb, layer, *, name):
    s = u.shape[1]

    def body(uv_ref, hv_ref, ug_ref, hg_ref, wv_ref, wg_ref, bv_ref, bg_ref, o_ref):
        first = pl.program_id(0) == 0
        hv = jnp.where(first, 0.0, hv_ref[...])
        hg = jnp.where(first, 0.0, hg_ref[...])
        val, _, _ = _conv_vals(uv_ref[...], hv, wv_ref[...], bv_ref[...])
        gate, _, _ = _conv_vals(ug_ref[...], hg, wg_ref[...], bg_ref[...])
        o_ref[...] = (gate / (1.0 + jnp.exp(-gate)) * val).astype(BF16)

    return pl.pallas_call(body, out_shape=jax.ShapeDtypeStruct((FF_HALF, s, FF_CHUNK), BF16), grid=(s // CONV_BR, FF_HALF),
                          in_specs=_conv_in_specs("rc", layer), out_specs=pl.BlockSpec((None, CONV_BR, FF_CHUNK), lambda i, j: (j, i, 0)),
                          compiler_params=_params("parallel", "parallel"), name=name)(u, u, u, u, cw, cw, cb, cb)


def _conv_bwd_dc(u, cw, cb, da, layer, *, name):
    s = u.shape[1]

    def body(uv_ref, hv_ref, ug_ref, hg_ref, wv_ref, wg_ref, bv_ref, bg_ref, da_ref, dc_ref, dwb_ref):
        first = pl.program_id(1) == 0
        hv = jnp.where(first, 0.0, hv_ref[...])
        hg = jnp.where(first, 0.0, hg_ref[...])
        uv, ug = uv_ref[...], ug_ref[...]
        val, v1, v2 = _conv_vals(uv, hv, wv_ref[...], bv_ref[...])
        gate, g1, g2 = _conv_vals(ug, hg, wg_ref[...], bg_ref[...])
        da = da_ref[...]
        sg = 1.0 / (1.0 + jnp.exp(-gate))
        dval = da * (gate * sg)
        dgate = da * val * (sg * (1.0 + gate * (1.0 - sg)))
        dc_ref[0] = dval
        dc_ref[1] = dgate

        @pl.when(first)
        def _():
            dwb_ref[...] = jnp.zeros_like(dwb_ref)

        cs = lambda t: jnp.sum(t, axis=0, keepdims=True)
        r8 = lax.broadcasted_iota(jnp.int32, (8, FF_CHUNK), 0)
        rows4 = lambda a, b, c, d: jnp.where(r8 == 0, a, jnp.where(r8 == 1, b, jnp.where(r8 == 2, c, jnp.where(r8 == 3, d, 0.0))))
        dwb_ref[0] += rows4(cs(dval * v2), cs(dval * v1), cs(dval * uv), cs(dval))
        dwb_ref[1] += rows4(cs(dgate * g2), cs(dgate * g1), cs(dgate * ug), cs(dgate))

    specs = _conv_in_specs("cr", layer) + [pl.BlockSpec((None, CONV_BR, FF_CHUNK), lambda j, i: (j, i, 0))]
    return pl.pallas_call(
        body, out_shape=(jax.ShapeDtypeStruct((2, FF_HALF, s, FF_CHUNK), F32), jax.ShapeDtypeStruct((2, FF_HALF, 8, FF_CHUNK), F32)),
        grid=(FF_HALF, s // CONV_BR), in_specs=specs,
        out_specs=(pl.BlockSpec((2, None, CONV_BR, FF_CHUNK), lambda j, i: (0, j, i, 0)),
                   pl.BlockSpec((2, None, 8, FF_CHUNK), lambda j, i: (0, j, 0, 0))),
        compiler_params=_params("parallel", "arbitrary"), name=name)(u, u, u, u, cw, cw, cb, cb, da)


def _conv_bwd_du(dc, cw, layer, *, name):
    n_chunk, s, _ = dc.shape
    nrow = s // CONV_BR
    per = CONV_BR // HALO

    def body(dc_ref, h_ref, w_ref, o_ref):
        last = pl.program_id(1) == nrow - 1
        h = jnp.where(last, 0.0, h_ref[...])
        x = dc_ref[...]
        w = w_ref[...]
        o_ref[...] = (w[2:3, :] * x + w[1:2, :] * _shift_up(x, h, 1) + w[0:1, :] * _shift_up(x, h, 2)).astype(BF16)

    return pl.pallas_call(
        body, out_shape=jax.ShapeDtypeStruct((n_chunk, s, FF_CHUNK), BF16), grid=(n_chunk, nrow),
        in_specs=[pl.BlockSpec((None, CONV_BR, FF_CHUNK), lambda c, i: (c, i, 0)),
                  pl.BlockSpec((None, HALO, FF_CHUNK), lambda c, i: (c, jnp.minimum((i + 1) * per, nrow * per - 1), 0)),
                  pl.BlockSpec((None, None, 3, FF_CHUNK), lambda c, i: (layer, c, 0, 0))],
        out_specs=pl.BlockSpec((None, CONV_BR, FF_CHUNK), lambda c, i: (c, i, 0)),
        compiler_params=_params("parallel", "parallel"), name=name)(dc, dc, cw)


def _rows_of(r):
    return lambda ref, idx: ref.at[:, pl.ds(idx * r, r), :]


def _slot1(ref, idx):
    return ref.at[:, idx]


def _slot0(ref, idx):
    return ref.at[idx]


def _all_gather(shards, full_shapes, places, *, name):
    n = len(shards)

    def body(*refs):
        ins, outs = refs[:n], refs[n:2 * n]
        send_sems, recv_sems, local_sems = refs[2 * n:]
        mx, my, mc = lax.axis_index("x"), lax.axis_index("y"), lax.axis_index("c")
        me, sibling = (mx, my, mc), (mx, my, 1 - mc)
        chips = [(1 - mx, my), (mx, 1 - my), (1 - mx, 1 - my)]

        def win(t, px, py, pc):
            return places[t](outs[t], 4 * px + 2 * py + pc)

        def copy(t, k, block, to, src=None):
            return pltpu.make_async_remote_copy(src_ref=win(t, *block) if src is None else src, dst_ref=win(t, *block),
                                                send_sem=send_sems.at[t, k], recv_sem=recv_sems.at[t, k], device_id=to, device_id_type=MESH)

        mine = [pltpu.make_async_copy(ins[t], win(t, *me), local_sems.at[t]) for t in range(n)]
        for cp in mine:
            cp.start()
        first = []
        for t in range(n):
            first += [copy(t, 0, me, sibling, src=ins[t])] + [copy(t, 1 + j, me, (*chip, mc), src=ins[t]) for j, chip in enumerate(chips)]
        for cp in first:
            cp.start()
        passed = []
        for j, chip in enumerate(chips):
            for t in range(n):
                copy(t, 1 + j, (*chip, mc), me).wait_recv()
                fwd = copy(t, 4 + j, (*chip, mc), sibling)
                fwd.start()
                passed.append(fwd)
        for t in range(n):
            copy(t, 0, sibling, me).wait_recv()
            for j, chip in enumerate(chips):
                copy(t, 4 + j, (*chip, 1 - mc), me).wait_recv()
        for cp in first + passed:
            cp.wait_send()
        for cp in mine:
            cp.wait()

    hbm = pl.BlockSpec(memory_space=pl.ANY)
    return pl.pallas_call(
        body, out_shape=tuple(jax.ShapeDtypeStruct(s, x.dtype) for s, x in zip(full_shapes, shards)),
        in_specs=[hbm] * n, out_specs=(hbm,) * n,
        scratch_shapes=[pltpu.SemaphoreType.DMA((n, 7)), pltpu.SemaphoreType.DMA((n, 7)), pltpu.SemaphoreType.DMA((n,))],
        name=name)(*shards)


FLIPS = [(fx, fy, fc) for fx in (0, 1) for fy in (0, 1) for fc in (0, 1)][1:]


def _exchange_copies(kind, places, src, land, send_sems, recv_sems, local_sems):
    mx, my, mc = lax.axis_index("x"), lax.axis_index("y"), lax.axis_index("c")
    me = 4 * mx + 2 * my + mc
    n = len(src)
    local, remote = [], []
    for t in range(n):
        if kind == "gather":
            local.append(pltpu.make_async_copy(src[t], places[t](land[t], me), local_sems.at[t]))
        else:
            local.append(pltpu.make_async_copy(places[t](src[t], me), land[t].at[me], local_sems.at[t]))
    for k, (fx, fy, fc) in enumerate(FLIPS):
        px, py, pc = mx ^ fx, my ^ fy, mc ^ fc
        peer = 4 * px + 2 * py + pc
        for t in range(n):
            sems = dict(send_sem=send_sems.at[7 * t + k], recv_sem=recv_sems.at[7 * t + k], device_id=(px, py, pc), device_id_type=MESH)
            if kind == "gather":
                pair = [(src[t], places[t](land[t], me)), (src[t], places[t](land[t], peer))]
            else:
                pair = [(places[t](src[t], peer), land[t].at[me]), (places[t](src[t], peer), land[t].at[peer])]
            remote.append([functools.partial(pltpu.make_async_remote_copy, src_ref=s_, dst_ref=d_, **sems) for s_, d_ in pair])
    return local, remote


HBM_SPEC = pl.BlockSpec(memory_space=pltpu.HBM)
SEM_SPEC = pl.BlockSpec(memory_space=pltpu.SEMAPHORE)
SIDE_EFFECT = pltpu.SideEffectType.DATAFLOW_SIDE_EFFECTING


def _exchange_start(kind, srcs, land_shapes, places, after, *, name):
    n = len(srcs)

    def body(*refs):
        src, land = refs[:n], refs[n:2 * n]
        send_sems, recv_sems, local_sems = refs[2 * n + 1:2 * n + 4]
        token = refs[-1]
        local, remote = _exchange_copies(kind, places, src, land, send_sems, recv_sems, local_sems)
        for cp in local:
            cp.start()
        for send, _ in remote:
            send().start()
        token[...] = jnp.zeros_like(token)

    hbm = lambda t: pltpu.with_memory_space_constraint(t, pltpu.HBM)
    lands = [hbm(lax.empty(tuple(s), x.dtype)) for s, x in zip(land_shapes, srcs)]
    out_shape = (pltpu.SemaphoreType.DMA((7 * n,)), pltpu.SemaphoreType.DMA((7 * n,)), pltpu.SemaphoreType.DMA((n,)),
                 *[pltpu.HBM(x.shape, x.dtype) for x in srcs], *[pltpu.HBM(tuple(s), x.dtype) for s, x in zip(land_shapes, srcs)],
                 jax.ShapeDtypeStruct((8, LANES), F32))
    outs = pl.pallas_call(
        body, name=name, out_shape=out_shape, in_specs=[HBM_SPEC] * (2 * n) + [pl.BlockSpec(memory_space=pl.ANY)],
        out_specs=(SEM_SPEC, SEM_SPEC, SEM_SPEC) + (HBM_SPEC,) * (2 * n) + (pl.BlockSpec(memory_space=pltpu.VMEM),),
        input_output_aliases={i: 3 + i for i in range(2 * n)},
        compiler_params=pltpu.CompilerParams(has_side_effects=SIDE_EFFECT))(*[hbm(x) for x in srcs], *lands, after)
    return dict(sems=outs[:3], src=outs[3:3 + n], land=outs[3 + n:3 + 2 * n], token=outs[-1])


def _exchange_wait(kind, started, places, after, *, name):
    n = len(started["src"])

    def body(*refs):
        src, land = refs[:n], refs[n:2 * n]
        send_sems, recv_sems, local_sems = refs[2 * n:2 * n + 3]
        local, remote = _exchange_copies(kind, places, src, land, send_sems, recv_sems, local_sems)
        for cp in local:
            cp.wait()
        for send, arrival in remote:
            send().wait_send()
            arrival().wait_recv()

    out_shape = tuple(pltpu.HBM(x.shape, x.dtype) for x in started["src"]) + tuple(pltpu.HBM(x.shape, x.dtype) for x in started["land"])
    outs = pl.pallas_call(
        body, name=name, out_shape=out_shape,
        in_specs=[HBM_SPEC] * (2 * n) + [SEM_SPEC] * 3 + [pl.BlockSpec(memory_space=pl.ANY)], out_specs=(HBM_SPEC,) * (2 * n),
        input_output_aliases={i: i for i in range(2 * n)},
        compiler_params=pltpu.CompilerParams(has_side_effects=SIDE_EFFECT))(*started["src"], *started["land"], *started["sems"], after)
    return list(outs[n:])


def _adamw(parts, w, m, v, *, br, name):
    layers, r, wd = w.shape
    assert len(parts) == layers

    def body(*refs):
        p_refs = refs[:layers]
        w_ref, m_ref, v_ref, g_ref, d_ref, nm_ref, nv_ref = refs[layers:]
        for k in range(layers):
            @pl.when(pl.program_id(0) == k)
            def _(p_ref=p_refs[k]):
                g = p_ref[0].astype(F32)
                for dev in range(1, N_DEV):
                    g = g + p_ref[dev].astype(F32)
                mm = ADAM_B1 * m_ref[...] + (1.0 - ADAM_B1) * g
                vv = ADAM_B2 * v_ref[...] + (1.0 - ADAM_B2) * (g * g)
                m_hat = mm / (1.0 - ADAM_B1 ** ADAM_STEP)
                v_hat = vv / (1.0 - ADAM_B2 ** ADAM_STEP)
                g_ref[...] = g
                d_ref[...] = -ADAM_LR * (m_hat / (jnp.sqrt(v_hat) + ADAM_EPS) + ADAM_WD * w_ref[...])
                nm_ref[...] = mm
                nv_ref[...] = vv

    p_spec = lambda k: pl.BlockSpec((N_DEV, None, br, wd), lambda l, i: (0, 0, jnp.where(l == k, i, 0), 0))
    blk = pl.BlockSpec((None, br, wd), lambda l, i: (l, i, 0))
    shp = jax.ShapeDtypeStruct((layers, r, wd), F32)
    return pl.pallas_call(body, out_shape=(shp, shp, shp, shp), grid=(layers, r // br),
                          in_specs=[p_spec(k) for k in range(layers)] + [blk, blk, blk], out_specs=(blk, blk, blk, blk),
                          compiler_params=_params("arbitrary", "arbitrary"), name=name)(*parts, w, m, v)


SMALL = ("norm_mix", "norm_mem", "norm_ffn", "b_forget", "conv_b", "norm_final")


def _pack(tensors):
    flat = jnp.concatenate([t.reshape(-1) for t in tensors])
    rows = -(-flat.shape[0] // (PACK_W * PACK_ROW_ALIGN)) * PACK_ROW_ALIGN
    flat = jnp.pad(flat, (0, rows * PACK_W - flat.shape[0]))
    return flat.reshape(1, rows, PACK_W)


def _unpack(buf, shapes):
    flat = buf.reshape(-1)
    out, off = [], 0
    for shp in shapes:
        n = math.prod(shp)
        out.append(flat[off:off + n].reshape(tuple(shp)))
        off += n
    return out


def _fox_permute(w):
    pad = jnp.zeros(w.shape[:-1] + (FOX_P - FOX_IN,), w.dtype)
    return jnp.concatenate([w[..., :3 * D_MIX], w[..., 3 * D_MIX + N_MIX_HEADS:], w[..., 3 * D_MIX:3 * D_MIX + N_MIX_HEADS], pad], axis=-1)


def _fox_unpermute(w):
    return jnp.concatenate([w[..., :3 * D_MIX], w[..., DIL_IN:DIL_IN + N_MIX_HEADS], w[..., 3 * D_MIX:DIL_IN]], axis=-1)


def _bias_layouts(c):
    s = c.shape[0]
    ct = c[:, :N_MIX_HEADS].T.reshape(N_MIX_HP, 2, s)
    cq6 = jnp.repeat(ct, HEAD_DIM, axis=1).transpose(0, 2, 1)
    ck6 = jnp.pad(ct, ((0, 0), (0, 6), (0, 0)))
    return cq6, ck6


def _bias_grads(dcq6, dck6):
    s = dcq6.shape[1]
    dq = dcq6[:, :, ::HEAD_DIM].transpose(1, 0, 2).reshape(s, N_MIX_HEADS)
    dk = dck6[:, :2, :].reshape(N_MIX_HEADS, s).T
    pad = lambda t: jnp.pad(t, ((0, 0), (0, LANES - N_MIX_HEADS)))
    return pad(dq), pad(dk)


def _device_step(x, mem, target, small, get_weights, put_grads):
    s = x.shape[0]
    mt = mem.shape[0]
    bq = 512
    cos_t, sin_t = _rope_tables(s)
    row = lambda t, l: t[l][None, :]
    saved = []
    h = x
    cb8 = small["conv_b"].reshape(DEPTH, N_DEV, 1, FF_CHUNK)
    for l in range(DEPTH):
        kind, slot = l % 2, l // 2
        wl = get_weights(l, h)
        xn = _rmsnorm_fwd(h, row(small["norm_mix"], l), br=512, name=f"norm_mix_fwd{l}")
        mn = _rmsnorm_fwd(mem, row(small["norm_mem"], l), br=mt, name=f"norm_mem_fwd{l}")
        proj = _mm(xn, wl["w_in"], "nn", tm=1024, tn=384 if kind == 0 else 512, layer=0, name=f"in_proj{l}")
        kvm = _mm(mn, wl["w_mem_kv"], "nn", tm=mt, tn=512, layer=0, name=f"mem_kv{l}")
        st = dict(h=h, xn=xn, mn=mn, proj=proj, kvm=kvm, w=wl)
        if kind == 0:
            b_pad = jnp.pad(small["b_forget"][slot], (0, LANES - N_MIX_HEADS))[None, :]
            c = _forget_cumsum(proj, b_pad, name=f"forget_cumsum{l}")
            cq6, ck6 = _bias_layouts(c)
            mix, lse = _attn_fwd(proj, proj, cq6, ck6, q_col=0, k_col=N_MIX_HP, v_col=2 * N_MIX_HP, n_hp=N_MIX_HP,
                                 causal=True, bq=bq, bk=bq, name=f"fox_fwd{l}")
            st.update(b_pad=b_pad, cq6=cq6, ck6=ck6, mix=mix, lse=lse)
        else:
            qk_r = _rope(proj, cos_t, sin_t, n_cols=2 * N_MIX_HP, out_dtype=F32, br=512, name=f"rope_fwd{l}")
            mix, lses = _dil_fwd(qk_r, proj, name=f"dil_fwd{l}")
            st.update(qk_r=qk_r, lses=lses, mix=mix)
        mo, lse_m = _attn_fwd(proj, kvm, None, None, q_col=QM_COL, k_col=0, v_col=N_MEM_HP, n_hp=N_MEM_HP,
                              causal=False, bq=bq, bk=mt, name=f"mem_fwd{l}")
        heads = jnp.concatenate([mix.astype(BF16), mo.astype(BF16)], axis=1)
        h1 = _mm(heads, wl["w_out"], "nn", tm=1024, tn=512, res=h, layer=0, name=f"out_proj{l}")
        xf = _rmsnorm_fwd(h1, row(small["norm_ffn"], l), br=512, name=f"norm_ffn_fwd{l}")
        u = _mm(xf, wl["w_up"], "nn", tm=1024, tn=FF_CHUNK, layer=0, chunk="b", name=f"up_proj{l}")
        a = _conv_fwd(u, wl["conv_w"], cb8[l], 0, name=f"conv_fwd{l}")
        h = _mm(a, wl["w_down"], "nn", tm=1024, tn=512, res=h1, layer=0, chunk="reduce", name=f"down_proj{l}")
        st.update(mo=mo, lse_m=lse_m, heads=heads, h1=h1, xf=xf, u=u, a=a)
        saved.append(st)

    dh, dhb, dg_final, loss = _loss_head(h, target, small["norm_final"][None, :], br=512, name="loss_head")
    gs = {k: [None] * DEPTH for k in ("norm_mix", "norm_mem", "norm_ffn", "conv_b")}
    gs["b_forget"] = [None] * 2
    dep = 0.0
    for l in reversed(range(DEPTH)):
        st = saved[l]
        wl = st["w"]
        gw = {}
        kind, slot = l % 2, l // 2
        da = _mm(dhb, wl["w_down"], "nt", tm=1024, tn=FF_CHUNK, layer=0, chunk="b", name=f"down_dx{l}")
        gw["w_down"] = _mm(st["a"], dhb, "tn", tm=FF_CHUNK, tn=512, out_dtype=BF16, chunk="a", name=f"down_dw{l}")
        dc, dwb = _conv_bwd_dc(st["u"], wl["conv_w"], cb8[l] + dep, da, 0, name=f"conv_bwd_dc{l}")
        dwb = dwb.reshape(N_DEV, 8, FF_CHUNK)
        gw["conv_w"] = dwb
        gs["conv_b"][l] = dwb[:, 3, :].reshape(-1)
        du = _conv_bwd_du(dc.reshape(N_DEV, s, FF_CHUNK), wl["conv_w"], 0, name=f"conv_bwd_du{l}")
        dxf = _mm(du, wl["w_up"], "nt", tm=512, tn=512, layer=0, chunk="reduce", name=f"up_dx{l}")
        gw["w_up"] = _mm(st["xf"], du, "tn", tm=512, tn=FF_CHUNK, out_dtype=BF16, chunk="b", name=f"up_dw{l}")
        dh1, dh1b, dgf = _rmsnorm_bwd(st["h1"], dxf, row(small["norm_ffn"], l), dh, br=512, name=f"norm_ffn_bwd{l}")
        gs["norm_ffn"][l] = dgf[0]
        dheads = _mm(dh1b, wl["w_out"], "nt", tm=1024, tn=512, layer=0, name=f"out_dx{l}")
        gw["w_out"] = _mm(st["heads"], dh1b, "tn", tm=512, tn=512, out_dtype=BF16, name=f"out_dw{l}")
        dqm, dkm, dvm = _attn_bwd(st["proj"], st["kvm"], st["mo"], dheads, st["lse_m"], None, None, q_col=QM_COL, k_col=0,
                                  v_col=N_MEM_HP, o_col=N_MIX_HP, n_hp=N_MEM_HP, causal=False, bq=bq, bk=mt, name=f"mem_bwd{l}")
        dkvm = jnp.concatenate([dkm, dvm], axis=1).astype(BF16)
        gw["w_mem_kv"] = _mm(st["mn"], dkvm, "tn", tm=512, tn=512, out_dtype=BF16, name=f"mem_kv_dw{l}")
        dmn = _mm(dkvm, wl["w_mem_kv"], "nt", tm=mt, tn=512, layer=0, name=f"mem_kv_dx{l}")
        _, _, dgm = _rmsnorm_bwd(mem, dmn, row(small["norm_mem"], l), None, br=mt, name=f"norm_mem_bwd{l}")
        gs["norm_mem"][l] = dgm[0]
        if kind == 0:
            dq, dk, dv, dcq6, dck6 = _attn_bwd(st["proj"], st["proj"], st["mix"], dheads, st["lse"], st["cq6"], st["ck6"], q_col=0,
                                               k_col=N_MIX_HP, v_col=2 * N_MIX_HP, o_col=0, n_hp=N_MIX_HP, causal=True,
                                               bq=bq, bk=bq, name=f"fox_bwd{l}")
            dcq, dck = _bias_grads(dcq6, dck6)
            dz, db = _forget_cumsum_bwd(st["proj"], st["b_pad"], dcq, dck, name=f"forget_cumsum_bwd{l}")
            gs["b_forget"][slot] = db[0, :N_MIX_HEADS]
            dproj = jnp.concatenate([dq.astype(BF16), dk.astype(BF16), dv.astype(BF16), dqm.astype(BF16), dz], axis=1)
        else:
            dq_r, dk_r, dv = _dil_bwd(st["qk_r"], st["proj"], st["mix"], dheads, st["lses"], name=f"dil_bwd{l}")
            dq = _rope(dq_r, cos_t, -sin_t, n_cols=N_MIX_HP, out_dtype=BF16, br=512, name=f"rope_bwd_q{l}")
            dk = _rope(dk_r, cos_t, -sin_t, n_cols=N_MIX_HP, out_dtype=BF16, br=512, name=f"rope_bwd_k{l}")
            dproj = jnp.concatenate([dq, dk, dv.astype(BF16), dqm.astype(BF16)], axis=1)
        dxn = _mm(dproj, wl["w_in"], "nt", tm=1024, tn=512, layer=0, name=f"in_dx{l}")
        gw["w_in"] = _mm(st["xn"], dproj, "tn", tm=512, tn=384 if kind == 0 else 512, out_dtype=BF16, name=f"in_dw{l}")
        dh, dhb, dgx = _rmsnorm_bwd(st["h"], dxn, row(small["norm_mix"], l), dh1, br=512, name=f"norm_mix_bwd{l}")
        gs["norm_mix"][l] = dgx[0]
        dep = put_grads(l, gw)

    grads_s = {k: jnp.stack(v) for k, v in gs.items()}
    grads_s["norm_final"] = dg_final[0]
    return loss[0, 0], dh, grads_s


def kernel(x, mem, norm_mix, norm_mem, norm_ffn, w_in_fox, b_forget, w_in_dil, w_mem_kv, w_out, w_up, conv_w, conv_b, w_down, norm_final, loss_target, m_norm_mix, m_norm_mem, m_norm_ffn, m_w_in_fox, m_b_forget, m_w_in_dil, m_w_mem_kv, m_w_out, m_w_up, m_conv_w, m_conv_b, m_w_down, m_norm_final, v_norm_mix, v_norm_mem, v_norm_ffn, v_w_in_fox, v_b_forget, v_w_in_dil, v_w_mem_kv, v_w_out, v_w_up, v_conv_w, v_conv_b, v_w_down, v_norm_final):
    names = ["norm_mix", "norm_mem", "norm_ffn", "w_in_fox", "b_forget", "w_in_dil", "w_mem_kv", "w_out", "w_up", "conv_w", "conv_b",
             "w_down", "norm_final"]
    w = dict(zip(names, (norm_mix, norm_mem, norm_ffn, w_in_fox, b_forget, w_in_dil, w_mem_kv, w_out, w_up, conv_w, conv_b, w_down, norm_final)))
    m = dict(zip(names, (m_norm_mix, m_norm_mem, m_norm_ffn, m_w_in_fox, m_b_forget, m_w_in_dil, m_w_mem_kv, m_w_out, m_w_up, m_conv_w,
                         m_conv_b, m_w_down, m_norm_final)))
    v = dict(zip(names, (v_norm_mix, v_norm_mem, v_norm_ffn, v_w_in_fox, v_b_forget, v_w_in_dil, v_w_mem_kv, v_w_out, v_w_up, v_conv_w,
                         v_conv_b, v_w_down, v_norm_final)))
    big = ("w_in_fox", "w_in_dil", "w_mem_kv", "w_out", "w_up", "w_down", "conv_w")
    small_shapes = [w[k].shape for k in SMALL]
    dil_c = w_in_dil.shape[2]
    rows = {k: w[k].shape[1] for k in ("w_in_fox", "w_mem_kv", "w_out", "w_down")}

    def places(l):
        w_in_place = _rows_of(rows["w_in_fox"]) if l % 2 == 0 else _slot1
        return [w_in_place, _rows_of(rows["w_mem_kv"]), _rows_of(rows["w_out"]), _slot1, _rows_of(rows["w_down"]), _slot1]

    def full_shapes(l):
        w_in_shape = (1, D_MODEL, FOX_P) if l % 2 == 0 else (1, N_DEV, D_MODEL, dil_c)
        return [w_in_shape, (1, D_MODEL, 2 * D_MEMQ), (1, D_MODEL, D_MODEL), (1, N_DEV, D_MODEL, FF_CHUNK), (1, D_FF, D_MODEL),
                (1, N_DEV, 3, FF_CHUNK)]

    cast = {"w_in_fox": _fox_permute(w_in_fox).astype(BF16), "w_in_dil": w_in_dil.astype(BF16), "w_mem_kv": w_mem_kv.astype(BF16),
            "w_out": w_out.astype(BF16), "w_up": w_up.astype(BF16), "w_down": w_down.astype(BF16), "conv_w": conv_w}
    gathers, after = [], norm_final
    for l in range(DEPTH):
        w_in_shard = cast["w_in_fox" if l % 2 == 0 else "w_in_dil"][l // 2][None]
        shards = [w_in_shard] + [cast[k][l][None] for k in ("w_mem_kv", "w_out", "w_up", "w_down", "conv_w")]
        gathers.append(_exchange_start("gather", shards, full_shapes(l), places(l), after, name=f"weights_gather_start{l}"))
        after = gathers[-1]["token"]
    started = gathers[0]["token"][0, 0] + gathers[1]["token"][0, 0] + gathers[2]["token"][0, 0] + gathers[3]["token"][0, 0]
    small = {k: w[k] for k in SMALL}
    small["norm_mix"] = norm_mix + started

    def get_weights(l, h):
        w_in, wmem, wout, wup, wdown, cw = _exchange_wait("gather", gathers[l], places(l), h, name=f"weights_gather_wait{l}")
        if l % 2 == 1:
            w_in = jnp.concatenate([w_in[:, j] for j in range(N_DEV)], axis=-1)
        return dict(w_in=w_in, w_mem_kv=wmem, w_out=wout, w_up=wup, w_down=wdown.reshape(1, FF_HALF, FF_CHUNK, D_MODEL), conv_w=cw)

    scatters = {}

    def put_grads(l, g):
        if l % 2 == 0:
            g_in = g["w_in"][None]
        else:
            g_in = jnp.stack([g["w_in"][:, j * dil_c:(j + 1) * dil_c] for j in range(N_DEV)], axis=0)[None]
        srcs = [g_in, g["w_mem_kv"][None], g["w_out"][None], g["w_up"][None], g["w_down"].reshape(1, D_FF, D_MODEL), g["conv_w"][None]]
        shard_shapes = [cast["w_in_fox" if l % 2 == 0 else "w_in_dil"][l // 2].shape] + \
            [cast[k][l].shape for k in ("w_mem_kv", "w_out", "w_up", "w_down")] + [(8, FF_CHUNK)]
        scatters[l] = _exchange_start("scatter", srcs, [(N_DEV, 1) + tuple(s) for s in shard_shapes], places(l), g["w_out"],
                                      name=f"grads_scatter_start{l}")
        return scatters[l]["token"][0, 0]

    loss, grad_x, gs = _device_step(x[0], mem[0], loss_target[0], small, get_weights, put_grads)

    recv = {l: _exchange_wait("scatter", scatters[l], places(l), grad_x, name=f"grads_scatter_wait{l}") for l in reversed(range(DEPTH))}
    s_pack = _pack([gs[k] for k in SMALL])
    (s_recv,) = _all_gather([s_pack], [(N_DEV,) + s_pack.shape], [_slot0], name="small_grads_all_gather")

    layer_tensors = ("w_in", "w_mem_kv", "w_out", "w_up", "w_down", "conv_w")
    parts = {k: [recv[l][layer_tensors.index(k)] for l in range(DEPTH)] for k in layer_tensors[1:]}
    parts["conv_w"] = [p[:, :, :3, :] for p in parts["conv_w"]]
    parts["w_in_fox"] = [recv[l][0] for l in range(0, DEPTH, 2)]
    parts["w_in_dil"] = [recv[l][0] for l in range(1, DEPTH, 2)]
    to_local = {k: (lambda t: t) for k in big}
    to_local["w_in_fox"] = _fox_permute
    from_local = {k: (lambda t: t) for k in big}
    from_local["w_in_fox"] = _fox_unpermute
    blocks = {"w_in_fox": rows["w_in_fox"], "w_in_dil": 512, "w_mem_kv": rows["w_mem_kv"], "w_out": rows["w_out"], "w_up": 256,
              "w_down": rows["w_down"] // 2, "conv_w": 3}
    outs = {}
    for k in big:
        f = to_local[k]
        outs[k] = [from_local[k](t) for t in _adamw(parts[k], f(w[k]), f(m[k]), f(v[k]), br=blocks[k], name=f"adamw_{k}")]
    small_outs = _adamw([s_recv], _pack([w[k] for k in SMALL]), _pack([m[k] for k in SMALL]), _pack([v[k] for k in SMALL]),
                        br=s_pack.shape[1], name="adamw_small")
    res = []
    for i, os_ in enumerate(small_outs):
        d = {k: outs[k][i] for k in big}
        d.update(zip(SMALL, _unpack(os_, small_shapes)))
        res.append([d[k] for k in names])
    loss = lax.psum(loss, ("x", "y", "c"))
    return (loss, grad_x[None], *res[0], *res[1], *res[2], *res[3])
```

```python
import functools
import math

import jax
import jax.numpy as jnp
from jax import lax
from jax.experimental import pallas as pl
from jax.experimental.pallas import tpu as pltpu

F32 = jnp.float32
BF16 = jnp.bfloat16

D_MODEL = 1024
HEAD_DIM = 64
N_MIX_HEADS = 12
N_MEM_HEADS = 4
D_MIX = N_MIX_HEADS * HEAD_DIM
D_MEMQ = N_MEM_HEADS * HEAD_DIM
D_FF = 2816
DEPTH = 4
FOX_IN = 3 * D_MIX + N_MIX_HEADS + D_MEMQ
DIL_IN = 3 * D_MIX + D_MEMQ
LANES = 128
FOX_P = DIL_IN + LANES
N_MIX_HP = D_MIX // LANES
N_MEM_HP = D_MEMQ // LANES
QM_COL = 3 * N_MIX_HP
F_COL = DIL_IN // LANES
DILATED_BRANCHES = ((128, 1), (512, 4), (2048, 16))
DIL_L = 128
ROPE_THETA = 10000.0
NORM_EPS = 1e-6
NEG = -1e30
SCALE = HEAD_DIM ** -0.5
N_DEV = 8

ADAM_LR = 0.001
ADAM_B1 = 0.9
ADAM_B2 = 0.999
ADAM_EPS = 1e-08
ADAM_WD = 0.01
ADAM_STEP = 10

VMEM_LIMIT = 56 * 1024 * 1024
PACK_W = 1024
PACK_ROW_ALIGN = 8

MESH = pl.DeviceIdType.MESH
NT = (((1,), (1,)), ((), ()))
NN = (((1,), (0,)), ((), ()))
TN = (((0,), (0,)), ((), ()))


def _params(*sem):
    return pltpu.CompilerParams(dimension_semantics=sem, vmem_limit_bytes=VMEM_LIMIT)


def _lane_lo(shape):
    return lax.broadcasted_iota(jnp.int32, shape, len(shape) - 1) < HEAD_DIM


def _pair(lo, a, b):
    return jnp.where(lo, a, b)


def _mm(a, b, mode, *, tm, tn, name, out_dtype=F32, res=None, layer=None, chunk=None):
    lead = () if layer is None else (layer,)
    nl = (None,) * len(lead)
    bs = b.shape[len(lead):]
    dims = {"nn": NN, "nt": NT, "tn": TN}[mode]
    reduce_n = 0
    if chunk is None:
        (m, k) = a.shape[::-1] if mode == "tn" else a.shape
        n = bs[0] if mode == "nt" else bs[1]
        grid = (m // tm, n // tn)
        a_spec = pl.BlockSpec((k, tm), lambda i, j: (0, i)) if mode == "tn" else pl.BlockSpec((tm, k), lambda i, j: (i, 0))
        b_spec = pl.BlockSpec(nl + ((tn, k) if mode == "nt" else (k, tn)), lambda i, j: lead + ((j, 0) if mode == "nt" else (0, j)))
        o_spec = pl.BlockSpec((tm, tn), lambda i, j: (i, j))
        out_shape = (m, n)
    elif chunk == "b":
        (m, k) = a.shape[::-1] if mode == "tn" else a.shape
        c, nc = bs[0], (bs[1] if mode == "nt" else bs[2])
        grid = (m // tm, c)
        a_spec = pl.BlockSpec((k, tm), lambda i, j: (0, i)) if mode == "tn" else pl.BlockSpec((tm, k), lambda i, j: (i, 0))
        b_spec = pl.BlockSpec(nl + (None,) + tuple(bs[1:]), lambda i, j: lead + (j, 0, 0))
        o_spec = pl.BlockSpec((None, tm, nc), lambda i, j: (j, i, 0))
        out_shape = (c, m, nc)
    elif chunk == "a":
        assert mode == "tn"
        c, k, mc = a.shape
        n = bs[1]
        grid = (c, n // tn)
        a_spec = pl.BlockSpec((None, k, mc), lambda i, j: (i, 0, 0))
        b_spec = pl.BlockSpec(nl + (k, tn), lambda i, j: lead + (0, j))
        o_spec = pl.BlockSpec((None, mc, tn), lambda i, j: (i, 0, j))
        out_shape = (c, mc, n)
    else:
        reduce_n, m, kc = a.shape
        n = bs[1] if mode == "nt" else bs[2]
        grid = (m // tm, n // tn)
        a_spec = pl.BlockSpec((reduce_n, tm, kc), lambda i, j: (0, i, 0))
        b_spec = pl.BlockSpec(nl + ((reduce_n, tn, kc) if mode == "nt" else (reduce_n, kc, tn)),
                              lambda i, j: lead + ((0, j, 0) if mode == "nt" else (0, 0, j)))
        o_spec = pl.BlockSpec((tm, tn), lambda i, j: (i, j))
        out_shape = (m, n)

    def body(*refs):
        a_ref, b_ref = refs[0], refs[1]
        o_ref = refs[-1]
        dot = lambda x, y: lax.dot_general(x.astype(BF16), y.astype(BF16), dims, preferred_element_type=F32)
        if reduce_n:
            acc = dot(a_ref[0], b_ref[0])
            for r in range(1, reduce_n):
                acc = acc + dot(a_ref[r], b_ref[r])
        else:
            acc = dot(a_ref[...], b_ref[...])
        if res is not None:
            acc = acc + refs[2][...]
        o_ref[...] = acc.astype(o_ref.dtype)

    ins = [a, b] + ([res] if res is not None else [])
    specs = [a_spec, b_spec] + ([o_spec] if res is not None else [])
    return pl.pallas_call(body, out_shape=jax.ShapeDtypeStruct(out_shape, out_dtype), grid=grid,
                          in_specs=specs, out_specs=o_spec, compiler_params=_params("parallel", "parallel"), name=name)(*ins)


def _rmsnorm_fwd(x, g, *, br, name):
    r, d = x.shape

    def body(x_ref, g_ref, o_ref):
        xf = x_ref[...]
        rs = lax.rsqrt(jnp.mean(xf * xf, axis=-1, keepdims=True) + NORM_EPS)
        o_ref[...] = (xf * rs * g_ref[...]).astype(BF16)

    return pl.pallas_call(body, out_shape=jax.ShapeDtypeStruct((r, d), BF16), grid=(r // br,),
                          in_specs=[pl.BlockSpec((br, d), lambda i: (i, 0)), pl.BlockSpec((1, d), lambda i: (0, 0))],
                          out_specs=pl.BlockSpec((br, d), lambda i: (i, 0)), compiler_params=_params("parallel"), name=name)(x, g)


def _rms_bwd_math(x, dy, g):
    d = x.shape[-1]
    rs = lax.rsqrt(jnp.mean(x * x, axis=-1, keepdims=True) + NORM_EPS)
    gy = dy * g
    proj = jnp.sum(x * gy, axis=-1, keepdims=True) * (1.0 / d)
    dx = rs * gy - x * (rs * rs * rs) * proj
    dg = jnp.sum(dy * (x * rs), axis=0, keepdims=True)
    return dx, dg


def _rmsnorm_bwd(x, dy, g, res, *, br, name):
    r, d = x.shape
    has_res = res is not None

    def body(*refs):
        x_ref, dy_ref, g_ref = refs[:3]
        dx_ref, dxb_ref, dg_ref = refs[-3:]
        dx, dg = _rms_bwd_math(x_ref[...], dy_ref[...], g_ref[...])
        if has_res:
            dx = dx + refs[3][...]
        dx_ref[...] = dx
        dxb_ref[...] = dx.astype(BF16)

        @pl.when(pl.program_id(0) == 0)
        def _():
            dg_ref[...] = jnp.zeros_like(dg_ref)

        dg_ref[0:1, :] += dg

    row = pl.BlockSpec((br, d), lambda i: (i, 0))
    ins = [x, dy, g] + ([res] if has_res else [])
    specs = [row, row, pl.BlockSpec((1, d), lambda i: (0, 0))] + ([row] if has_res else [])
    return pl.pallas_call(
        body, out_shape=(jax.ShapeDtypeStruct((r, d), F32), jax.ShapeDtypeStruct((r, d), BF16), jax.ShapeDtypeStruct((8, d), F32)),
        grid=(r // br,), in_specs=specs, out_specs=(row, row, pl.BlockSpec((8, d), lambda i: (0, 0))),
        compiler_params=_params("arbitrary"), name=name)(*ins)


def _loss_head(h, target, g, *, br, name):
    r, d = h.shape

    def body(x_ref, t_ref, g_ref, dx_ref, dxb_ref, dg_ref, loss_ref):
        x = x_ref[...]
        gg = g_ref[...]
        rs = lax.rsqrt(jnp.mean(x * x, axis=-1, keepdims=True) + NORM_EPS)
        err = x * rs * gg - t_ref[...]
        part = jnp.sum(jnp.sum(err * err, axis=1, keepdims=True), axis=0, keepdims=True) * (0.5 / d)
        dx, dg = _rms_bwd_math(x, err * (1.0 / d), gg)
        dx_ref[...] = dx
        dxb_ref[...] = dx.astype(BF16)

        @pl.when(pl.program_id(0) == 0)
        def _():
            dg_ref[...] = jnp.zeros_like(dg_ref)
            loss_ref[...] = jnp.zeros_like(loss_ref)

        dg_ref[0:1, :] += dg
        loss_ref[...] += jnp.broadcast_to(part, loss_ref.shape)

    row = pl.BlockSpec((br, d), lambda i: (i, 0))
    return pl.pallas_call(
        body, out_shape=(jax.ShapeDtypeStruct((r, d), F32), jax.ShapeDtypeStruct((r, d), BF16),
                         jax.ShapeDtypeStruct((8, d), F32), jax.ShapeDtypeStruct((8, LANES), F32)),
        grid=(r // br,), in_specs=[row, row, pl.BlockSpec((1, d), lambda i: (0, 0))],
        out_specs=(row, row, pl.BlockSpec((8, d), lambda i: (0, 0)), pl.BlockSpec((8, LANES), lambda i: (0, 0))),
        compiler_params=_params("arbitrary"), name=name)(h, target, g)


def _split3(x):
    hi = x.astype(BF16)
    r1 = x - hi.astype(F32)
    mid = r1.astype(BF16)
    lo = (r1 - mid.astype(F32)).astype(BF16)
    return hi, mid, lo


def _tri_sum(tri, x):
    hi, mid, lo = _split3(x)
    dot = lambda t: jnp.dot(tri, t, preferred_element_type=F32)
    return dot(hi) + dot(mid) + dot(lo)


def _forget_cumsum(proj, b_pad, *, name):
    s = proj.shape[0]
    blk = LANES

    def body(f_ref, b_ref, c_ref):
        ri = lax.broadcasted_iota(jnp.int32, (blk, blk), 0)
        ci = lax.broadcasted_iota(jnp.int32, (blk, blk), 1)
        tri = (ci <= ri).astype(BF16)
        bias = b_ref[...]

        def step(t, carry):
            rows = pl.ds(pl.multiple_of(t * blk, blk), blk)
            z = f_ref[rows, :] + bias
            lf = jnp.minimum(z, 0.0) - jnp.log(1.0 + jnp.exp(-jnp.abs(z)))
            cs = _tri_sum(tri, lf) + carry
            c_ref[rows, :] = cs
            return cs[blk - 1:blk, :]

        lax.fori_loop(0, s // blk, step, jnp.zeros((1, blk), F32))

    return pl.pallas_call(body, out_shape=jax.ShapeDtypeStruct((s, LANES), F32), grid=(1,),
                          in_specs=[pl.BlockSpec((s, LANES), lambda i: (0, F_COL)), pl.BlockSpec((1, LANES), lambda i: (0, 0))],
                          out_specs=pl.BlockSpec((s, LANES), lambda i: (0, 0)), compiler_params=_params("arbitrary"), name=name)(proj, b_pad)


def _forget_cumsum_bwd(proj, b_pad, dcq, dck, *, name):
    s = proj.shape[0]
    blk = LANES
    nblk = s // blk

    def body(f_ref, b_ref, dcq_ref, dck_ref, dz_ref, db_ref):
        ri = lax.broadcasted_iota(jnp.int32, (blk, blk), 0)
        ci = lax.broadcasted_iota(jnp.int32, (blk, blk), 1)
        triu = (ci >= ri).astype(BF16)
        bias = b_ref[...]

        def step(t, carry):
            tail, dbs = carry
            rows = pl.ds(pl.multiple_of((nblk - 1 - t) * blk, blk), blk)
            dc = dcq_ref[rows, :] - dck_ref[rows, :]
            dlf = _tri_sum(triu, dc) + tail
            z = f_ref[rows, :] + bias
            e = jnp.exp(-jnp.abs(z))
            sig_neg = jnp.where(z >= 0.0, e, 1.0) / (1.0 + e)
            dz = dlf * sig_neg
            dz_ref[rows, :] = dz.astype(BF16)
            return dlf[0:1, :], dbs + jnp.sum(dz, axis=0, keepdims=True)

        _, dbs = lax.fori_loop(0, nblk, step, (jnp.zeros((1, blk), F32), jnp.zeros((1, blk), F32)))
        db_ref[...] = jnp.broadcast_to(dbs, db_ref.shape)

    full = pl.BlockSpec((s, LANES), lambda i: (0, 0))
    return pl.pallas_call(body, out_shape=(jax.ShapeDtypeStruct((s, LANES), BF16), jax.ShapeDtypeStruct((8, LANES), F32)), grid=(1,),
                          in_specs=[pl.BlockSpec((s, LANES), lambda i: (0, F_COL)), pl.BlockSpec((1, LANES), lambda i: (0, 0)), full, full],
                          out_specs=(full, pl.BlockSpec((8, LANES), lambda i: (0, 0))), compiler_params=_params("arbitrary"), name=name)(proj, b_pad, dcq, dck)


def _attn_fwd(q_arr, kv_arr, ck6, *, q_col, k_col, v_col, n_hp, causal, bq, bk, name):
    s = q_arr.shape[0]
    skv = kv_arr.shape[0]
    bias = ck6 is not None
    nq = s // bq
    assert not causal or bq == bk

    def body(*refs):
        q_ref, k_ref, v_ref = refs[:3]
        ck_ref = refs[3] if bias else None
        o_ref, lse_ref = refs[-2:]
        i = pl.program_id(1)
        lo = _lane_lo((bq, LANES))
        q = q_ref[...] * SCALE
        qh = (jnp.where(lo, q, 0.0).astype(BF16), jnp.where(lo, 0.0, q).astype(BF16))

        def block(j, carry, diagonal):
            ks = pl.ds(pl.multiple_of(j * bk, bk), bk)
            k = k_ref[ks, :].astype(BF16)
            v = v_ref[ks, :].astype(BF16)
            if diagonal:
                ok = lax.broadcasted_iota(jnp.int32, (bq, bk), 1) <= lax.broadcasted_iota(jnp.int32, (bq, bk), 0)
            out = []
            for h in range(2):
                m, l, acc = carry[3 * h:3 * h + 3]
                sc = lax.dot_general(qh[h], k, NT, preferred_element_type=F32)
                if bias:
                    sc = sc - ck_ref[0, h:h + 1, ks]
                if diagonal:
                    sc = jnp.where(ok, sc, NEG)
                mn = jnp.maximum(m, jnp.max(sc, axis=1, keepdims=True))
                p = jnp.exp(sc - mn)
                al = jnp.exp(m - mn)
                out += [mn, al * l + jnp.sum(p, axis=1, keepdims=True), al * acc + jnp.dot(p.astype(BF16), v, preferred_element_type=F32)]
            return tuple(out)

        col = lambda v_: jnp.full((bq, 1), v_, F32)
        init = (col(NEG), col(0.0), jnp.zeros((bq, LANES), F32)) * 2
        n_full = i if causal else skv // bk
        carry = lax.fori_loop(0, n_full, functools.partial(block, diagonal=False), init)
        if causal:
            carry = block(i, carry, True)
        m0, l0, a0, m1, l1, a1 = carry
        o_ref[...] = _pair(lo, a0 / l0, a1 / l1)
        lse_ref[0] = _pair(lo, m0 + jnp.log(l0), m1 + jnp.log(l1))

    specs = [pl.BlockSpec((bq, LANES), lambda h, i: (i, q_col + h)),
             pl.BlockSpec((skv, LANES), lambda h, i: (0, k_col + h)),
             pl.BlockSpec((skv, LANES), lambda h, i: (0, v_col + h))]
    ins = [q_arr, kv_arr, kv_arr]
    if bias:
        specs += [pl.BlockSpec((1, 8, skv), lambda h, i: (h, 0, 0))]
        ins += [ck6]
    return pl.pallas_call(
        body, out_shape=(jax.ShapeDtypeStruct((s, n_hp * LANES), F32), jax.ShapeDtypeStruct((n_hp, s, LANES), F32)),
        grid=(n_hp, nq), in_specs=specs,
        out_specs=(pl.BlockSpec((bq, LANES), lambda h, i: (i, h)), pl.BlockSpec((1, bq, LANES), lambda h, i: (h, i, 0))),
        compiler_params=_params("parallel", "parallel"), name=name)(*ins)


def _attn_bwd(q_arr, kv_arr, o_arr, do_arr, lse, ck6, *, q_col, k_col, v_col, o_col, n_hp, causal, bq, bk, name):
    s = q_arr.shape[0]
    skv = kv_arr.shape[0]
    bias = ck6 is not None
    nq = s // bq
    assert not causal or bq == bk

    def body(*refs):
        q_ref, k_ref, v_ref, o_ref, do_ref, lse_ref = refs[:6]
        if bias:
            ck_ref = refs[6]
            dq_ref, dk_ref, dv_ref, dcq_ref, dck_ref = refs[-5:]
        else:
            dq_ref, dk_ref, dv_ref = refs[-3:]
        j = pl.program_id(1)
        lo_q = _lane_lo((bq, LANES))
        lo_k = _lane_lo((bk, LANES))
        k = k_ref[...]
        v = v_ref[...].astype(BF16)
        kb = k.astype(BF16)
        kh = (jnp.where(lo_k, k, 0.0).astype(BF16), jnp.where(lo_k, 0.0, k).astype(BF16))
        if bias:
            pick_k = [(lax.broadcasted_iota(jnp.int32, (8, bk), 0) == h).astype(BF16) for h in range(2)]
            pick_q = [(lax.broadcasted_iota(jnp.int32, (8, bq), 0) == h).astype(BF16) for h in range(2)]

        @pl.when(j == 0)
        def _():
            dq_ref[...] = jnp.zeros_like(dq_ref)
            if bias:
                dcq_ref[...] = jnp.zeros_like(dcq_ref)

        def block(i, carry, diagonal):
            dk_acc, dv_acc, cs = carry
            qs = pl.ds(pl.multiple_of(i * bq, bq), bq)
            q = q_ref[qs, :] * SCALE
            do = do_ref[qs, :]
            dd = do * o_ref[qs, :]
            lse_i = lse_ref[0, qs, :]
            qh = (jnp.where(lo_q, q, 0.0).astype(BF16), jnp.where(lo_q, 0.0, q).astype(BF16))
            doh = (jnp.where(lo_q, do, 0.0).astype(BF16), jnp.where(lo_q, 0.0, do).astype(BF16))
            dh = (jnp.sum(jnp.where(lo_q, dd, 0.0), axis=1, keepdims=True), jnp.sum(jnp.where(lo_q, 0.0, dd), axis=1, keepdims=True))
            if diagonal:
                ok = lax.broadcasted_iota(jnp.int32, (bq, bk), 1) <= lax.broadcasted_iota(jnp.int32, (bq, bk), 0)
            dq_blk = None
            rs = None
            for h in range(2):
                sc = lax.dot_general(qh[h], kb, NT, preferred_element_type=F32)
                if bias:
                    sc = sc - ck_ref[0, h:h + 1, :]
                if diagonal:
                    sc = jnp.where(ok, sc, NEG)
                p = jnp.exp(sc - lse_i[:, h * HEAD_DIM:h * HEAD_DIM + 1])
                ds = p * (lax.dot_general(doh[h], v, NT, preferred_element_type=F32) - dh[h])
                dsb = ds.astype(BF16)
                dv_acc = dv_acc + lax.dot_general(p.astype(BF16), doh[h], TN, preferred_element_type=F32)
                dk_acc = dk_acc + lax.dot_general(dsb, qh[h], TN, preferred_element_type=F32)
                part = jnp.dot(dsb, kh[h], preferred_element_type=F32)
                dq_blk = part if dq_blk is None else dq_blk + part
                if bias:
                    cs = cs + jnp.dot(pick_q[h], dsb, preferred_element_type=F32)
                    row_sums = lax.dot_general(pick_k[h], dsb, NT, preferred_element_type=F32)
                    rs = row_sums if rs is None else rs + row_sums
            dq_ref[qs, :] += dq_blk * SCALE
            if bias:
                dcq_ref[0, :, qs] += rs
            return dk_acc, dv_acc, cs

        carry = (jnp.zeros((bk, LANES), F32), jnp.zeros((bk, LANES), F32), jnp.zeros((8, bk), F32))
        if causal:
            carry = block(j, carry, True)
        dk_acc, dv_acc, cs = lax.fori_loop(j + 1 if causal else 0, nq, functools.partial(block, diagonal=False), carry)
        dk_ref[...] = dk_acc
        dv_ref[...] = dv_acc
        if bias:
            dck_ref[0] = cs

    full_q = lambda c: pl.BlockSpec((s, LANES), lambda h, j: (0, c + h))
    specs = [full_q(q_col),
             pl.BlockSpec((bk, LANES), lambda h, j: (j, k_col + h)),
             pl.BlockSpec((bk, LANES), lambda h, j: (j, v_col + h)),
             full_q(0), full_q(o_col),
             pl.BlockSpec((1, s, LANES), lambda h, j: (h, 0, 0))]
    ins = [q_arr, kv_arr, kv_arr, o_arr, do_arr, lse]
    out_shape = [jax.ShapeDtypeStruct((s, n_hp * LANES), F32), jax.ShapeDtypeStruct((skv, n_hp * LANES), F32),
                 jax.ShapeDtypeStruct((skv, n_hp * LANES), F32)]
    out_specs = [full_q(0), pl.BlockSpec((bk, LANES), lambda h, j: (j, h)), pl.BlockSpec((bk, LANES), lambda h, j: (j, h))]
    if bias:
        specs += [pl.BlockSpec((1, 8, bk), lambda h, j: (h, 0, j))]
        ins += [ck6]
        out_shape += [jax.ShapeDtypeStruct((n_hp, 8, s), F32), jax.ShapeDtypeStruct((n_hp, 8, skv), F32)]
        out_specs += [pl.BlockSpec((1, 8, s), lambda h, j: (h, 0, 0)), pl.BlockSpec((1, 8, bk), lambda h, j: (h, 0, j))]
    return pl.pallas_call(body, out_shape=tuple(out_shape), grid=(n_hp, skv // bk), in_specs=specs, out_specs=tuple(out_specs),
                          compiler_params=_params("parallel", "arbitrary"), name=name)(*ins)


def _rope_tables(s):
    inv = 1.0 / (ROPE_THETA ** (jnp.arange(0, HEAD_DIM, 2, dtype=F32) / HEAD_DIM))
    ang = jnp.arange(s, dtype=F32)[:, None] * inv[None, :]
    cos, sin = jnp.cos(ang), jnp.sin(ang)
    return jnp.tile(cos, (1, 4)), jnp.concatenate([-sin, sin, -sin, sin], axis=1)


def _rope(x_arr, cos_t, sin_t, *, n_cols, out_dtype, br, name):
    s = x_arr.shape[0]

    def body(x_ref, c_ref, s_ref, o_ref):
        x = x_ref[...].astype(F32)
        first = (lax.broadcasted_iota(jnp.int32, x.shape, 1) % HEAD_DIM) < (HEAD_DIM // 2)
        swapped = jnp.where(first, pltpu.roll(x, LANES - HEAD_DIM // 2, 1), pltpu.roll(x, HEAD_DIM // 2, 1))
        o_ref[...] = (x * c_ref[...] + swapped * s_ref[...]).astype(o_ref.dtype)

    tab = pl.BlockSpec((br, LANES), lambda i, j: (i, 0))
    blk = pl.BlockSpec((br, LANES), lambda i, j: (i, j))
    return pl.pallas_call(body, out_shape=jax.ShapeDtypeStruct((s, n_cols * LANES), out_dtype), grid=(s // br, n_cols),
                          in_specs=[blk, tab, tab], out_specs=blk, compiler_params=_params("parallel", "parallel"), name=name)(x_arr, cos_t, sin_t)


def _dil_scores(q0, q1, kp, kc, has_prev):
    a = lax.broadcasted_iota(jnp.int32, (DIL_L, DIL_L), 0)
    c = lax.broadcasted_iota(jnp.int32, (DIL_L, DIL_L), 1)
    ok_p, ok_c = (c >= a) & has_prev, c <= a
    sc = lambda q, k, ok: jnp.where(ok, lax.dot_general(q, k, NT, preferred_element_type=F32) * SCALE, NEG)
    return sc(q0, kp, ok_p), sc(q0, kc, ok_c), sc(q1, kp, ok_p), sc(q1, kc, ok_c)


def _dil_rows(t, dil):
    r, m = t % dil, t // dil
    start = m * (DIL_L * dil) + r
    prev = jnp.maximum(start - DIL_L * dil, 0)
    return pl.ds(start, DIL_L, stride=dil), pl.ds(prev, DIL_L, stride=dil), m > 0


def _softmax3(a, b, c):
    m = jnp.maximum(jnp.maximum(a, b), c)
    ea, eb, ec = jnp.exp(a - m), jnp.exp(b - m), jnp.exp(c - m)
    den = ea + eb + ec
    inv = 1.0 / den
    return ea * inv, eb * inv, ec * inv, m + jnp.log(den)


def _dil_fwd(qk_r, proj, *, name):
    s = qk_r.shape[0]
    nsub = s // DIL_L
    mb = 512

    def body(q_ref, k_ref, v_ref, mix_ref, l1_ref, l2_ref, l3_ref, o1_scr, o2_scr, o3_scr):
        lo = _lane_lo((DIL_L, LANES))
        for (_, dil), o_scr, l_ref in zip(DILATED_BRANCHES, (o1_scr, o2_scr, o3_scr), (l1_ref, l2_ref, l3_ref)):
            def step(t, carry, dil=dil, o_scr=o_scr, l_ref=l_ref):
                cur, prev, has_prev = _dil_rows(t, dil)
                q = q_ref[cur, :]
                q0 = jnp.where(lo, q, 0.0).astype(BF16)
                q1 = jnp.where(lo, 0.0, q).astype(BF16)
                s0p, s0c, s1p, s1c = _dil_scores(q0, q1, k_ref[prev, :].astype(BF16), k_ref[cur, :].astype(BF16), has_prev)
                vp = v_ref[prev, :].astype(BF16)
                vc = v_ref[cur, :].astype(BF16)

                def head(sp, sc):
                    m = jnp.maximum(jnp.max(sp, axis=1, keepdims=True), jnp.max(sc, axis=1, keepdims=True))
                    ep, ec = jnp.exp(sp - m), jnp.exp(sc - m)
                    den = jnp.sum(ep, axis=1, keepdims=True) + jnp.sum(ec, axis=1, keepdims=True)
                    inv = 1.0 / den
                    o = jnp.dot((ep * inv).astype(BF16), vp, preferred_element_type=F32) \
                        + jnp.dot((ec * inv).astype(BF16), vc, preferred_element_type=F32)
                    return o, m + jnp.log(den)

                o0, l0 = head(s0p, s0c)
                o1, l1 = head(s1p, s1c)
                o_scr[cur, :] = _pair(lo, o0, o1)
                l_ref[cur, :] = _pair(lo, l0, l1)
                return carry

            lax.fori_loop(0, nsub, step, 0)

        def merge(i, carry):
            rows = pl.ds(pl.multiple_of(i * mb, mb), mb)
            wa, wb, wc, _ = _softmax3(l1_ref[rows, :], l2_ref[rows, :], l3_ref[rows, :])
            mix_ref[rows, :] = wa * o1_scr[rows, :] + wb * o2_scr[rows, :] + wc * o3_scr[rows, :]
            return carry

        lax.fori_loop(0, s // mb, merge, 0)

    col = lambda arr_col: pl.BlockSpec((s, LANES), lambda h: (0, arr_col + h))
    shp = jax.ShapeDtypeStruct((s, D_MIX), F32)
    mix, l1, l2, l3 = pl.pallas_call(
        body, out_shape=(shp, shp, shp, shp), grid=(N_MIX_HP,), in_specs=[col(0), col(N_MIX_HP), col(2 * N_MIX_HP)],
        out_specs=(col(0),) * 4, scratch_shapes=[pltpu.VMEM((s, LANES), F32)] * 3,
        compiler_params=_params("parallel"), name=name)(qk_r, qk_r, proj)
    return mix, (l1, l2, l3)


def _dil_bwd(qk_r, proj, mix, dheads, lses, *, name):
    s = qk_r.shape[0]
    nsub = s // DIL_L
    mb = 512

    def body(q_ref, k_ref, v_ref, mix_ref, dm_ref, l1_ref, l2_ref, l3_ref, dq_ref, dk_ref, dv_ref, lt_scr, dd_scr):
        lo = _lane_lo((DIL_L, LANES))
        lo_m = _lane_lo((mb, LANES))

        def prep(i, carry):
            rows = pl.ds(pl.multiple_of(i * mb, mb), mb)
            _, _, _, lt = _softmax3(l1_ref[rows, :], l2_ref[rows, :], l3_ref[rows, :])
            lt_scr[rows, :] = lt
            dd = dm_ref[rows, :] * mix_ref[rows, :]
            dd_scr[rows, :] = _pair(lo_m, jnp.sum(jnp.where(lo_m, dd, 0.0), axis=1, keepdims=True),
                                    jnp.sum(jnp.where(lo_m, 0.0, dd), axis=1, keepdims=True))
            zero = jnp.zeros((mb, LANES), F32)
            dq_ref[rows, :] = zero
            dk_ref[rows, :] = zero
            dv_ref[rows, :] = zero
            return carry

        lax.fori_loop(0, s // mb, prep, 0)

        for (_, dil), l_ref in zip(DILATED_BRANCHES, (l1_ref, l2_ref, l3_ref)):
            def step(t, carry, dil=dil, l_ref=l_ref):
                cur, prev, has_prev = _dil_rows(t, dil)
                q = q_ref[cur, :]
                q0 = jnp.where(lo, q, 0.0).astype(BF16)
                q1 = jnp.where(lo, 0.0, q).astype(BF16)
                kp, kc = k_ref[prev, :], k_ref[cur, :]
                s0p, s0c, s1p, s1c = _dil_scores(q0, q1, kp.astype(BF16), kc.astype(BF16), has_prev)
                vp = v_ref[prev, :].astype(BF16)
                vc = v_ref[cur, :].astype(BF16)
                lg = l_ref[cur, :]
                w = jnp.exp(lg - lt_scr[cur, :])
                wd = w * dd_scr[cur, :]
                dog = w * dm_ref[cur, :]
                do0 = jnp.where(lo, dog, 0.0).astype(BF16)
                do1 = jnp.where(lo, 0.0, dog).astype(BF16)
                kp0, kp1 = jnp.where(lo, kp, 0.0).astype(BF16), jnp.where(lo, 0.0, kp).astype(BF16)
                kc0, kc1 = jnp.where(lo, kc, 0.0).astype(BF16), jnp.where(lo, 0.0, kc).astype(BF16)

                def head(sp, sc, lcol, do_h):
                    lse_h = lg[:, lcol:lcol + 1]
                    wd_h = wd[:, lcol:lcol + 1]
                    pp, pc = jnp.exp(sp - lse_h), jnp.exp(sc - lse_h)
                    dsp = pp * (lax.dot_general(do_h, vp, NT, preferred_element_type=F32) - wd_h)
                    dsc = pc * (lax.dot_general(do_h, vc, NT, preferred_element_type=F32) - wd_h)
                    return pp.astype(BF16), pc.astype(BF16), dsp.astype(BF16), dsc.astype(BF16)

                p0p, p0c, ds0p, ds0c = head(s0p, s0c, 0, do0)
                p1p, p1c, ds1p, ds1c = head(s1p, s1c, HEAD_DIM, do1)
                dot = lambda a, b: jnp.dot(a, b, preferred_element_type=F32)
                dott = lambda a, b: lax.dot_general(a, b, TN, preferred_element_type=F32)
                dq_ref[cur, :] += (dot(ds0p, kp0) + dot(ds0c, kc0) + dot(ds1p, kp1) + dot(ds1c, kc1)) * SCALE
                dk_ref[cur, :] += (dott(ds0c, q0) + dott(ds1c, q1)) * SCALE
                dv_ref[cur, :] += dott(p0c, do0) + dott(p1c, do1)
                dk_ref[prev, :] += (dott(ds0p, q0) + dott(ds1p, q1)) * SCALE
                dv_ref[prev, :] += dott(p0p, do0) + dott(p1p, do1)
                return carry

            lax.fori_loop(0, nsub, step, 0)

    col = lambda arr_col: pl.BlockSpec((s, LANES), lambda h: (0, arr_col + h))
    shp = jax.ShapeDtypeStruct((s, D_MIX), F32)
    return pl.pallas_call(
        body, out_shape=(shp, shp, shp), grid=(N_MIX_HP,),
        in_specs=[col(0), col(N_MIX_HP), col(2 * N_MIX_HP), col(0), col(0), col(0), col(0), col(0)], out_specs=(col(0),) * 3,
        scratch_shapes=[pltpu.VMEM((s, LANES), F32)] * 2,
        compiler_params=_params("parallel"), name=name)(qk_r, qk_r, proj, mix, dheads, *lses)


CONV_BR = 512
FF_CHUNK = 2 * D_FF // N_DEV
FF_HALF = N_DEV // 2
HALO = 8


def _shift_down(x, halo, k):
    row = lax.broadcasted_iota(jnp.int32, x.shape, 0)
    y = pltpu.roll(x, k, 0)
    for r in range(k):
        y = jnp.where(row == r, halo[HALO - k + r:HALO - k + r + 1, :], y)
    return y


def _shift_up(x, halo, k):
    n = x.shape[0]
    row = lax.broadcasted_iota(jnp.int32, x.shape, 0)
    y = pltpu.roll(x, n - k, 0)
    for r in range(k):
        y = jnp.where(row == n - k + r, halo[r:r + 1, :], y)
    return y


def _conv_vals(u, halo, w, b):
    s1 = _shift_down(u, halo, 1)
    s2 = _shift_down(u, halo, 2)
    return b + w[0:1, :] * s2 + w[1:2, :] * s1 + w[2:3, :] * u, s1, s2


def _conv_in_specs(order, layer):
    rc = (lambda i, j: (i, j)) if order == "rc" else (lambda j, i: (i, j))
    per = CONV_BR // HALO
    main = lambda off: pl.BlockSpec((None, CONV_BR, FF_CHUNK), lambda *g: (off + rc(*g)[1], rc(*g)[0], 0))
    halo = lambda off: pl.BlockSpec((None, HALO, FF_CHUNK), lambda *g: (off + rc(*g)[1], jnp.maximum(rc(*g)[0] * per - 1, 0), 0))
    wspec = lambda off: pl.BlockSpec((None, None, 3, FF_CHUNK), lambda *g: (layer, off + rc(*g)[1], 0, 0))
    bspec = lambda off: pl.BlockSpec((None, 1, FF_CHUNK), lambda *g: (off + rc(*g)[1], 0, 0))
    return [main(0), halo(0), main(FF_HALF), halo(FF_HALF), wspec(0), wspec(FF_HALF), bspec(0), bspec(FF_HALF)]


def _conv_fwd(u, cw, cb, layer, *, name):
    s = u.shape[1]

    def body(uv_ref, hv_ref, ug_ref, hg_ref, wv_ref, wg_ref, bv_ref, bg_ref, o_ref):
        first = pl.program_id(0) == 0
        hv = jnp.where(first, 0.0, hv_ref[...])
        hg = jnp.where(first, 0.0, hg_ref[...])
        val, _, _ = _conv_vals(uv_ref[...], hv, wv_ref[...], bv_ref[...])
        gate, _, _ = _conv_vals(ug_ref[...], hg, wg_ref[...], bg_ref[...])
        o_ref[...] = (gate / (1.0 + jnp.exp(-gate)) * val).astype(BF16)

    return pl.pallas_call(body, out_shape=jax.ShapeDtypeStruct((FF_HALF, s, FF_CHUNK), BF16), grid=(s // CONV_BR, FF_HALF),
                          in_specs=_conv_in_specs("rc", layer), out_specs=pl.BlockSpec((None, CONV_BR, FF_CHUNK), lambda i, j: (j, i, 0)),
                          compiler_params=_params("parallel", "parallel"), name=name)(u, u, u, u, cw, cw, cb, cb)


def _conv_bwd_dc(u, cw, cb, da, layer, *, name):
    s = u.shape[1]

    def body(uv_ref, hv_ref, ug_ref, hg_ref, wv_ref, wg_ref, bv_ref, bg_ref, da_ref, dc_ref, dwb_ref):
        first = pl.program_id(1) == 0
        hv = jnp.where(first, 0.0, hv_ref[...])
        hg = jnp.where(first, 0.0, hg_ref[...])
        uv, ug = uv_ref[...], ug_ref[...]
        val, v1, v2 = _conv_vals(uv, hv, wv_ref[...], bv_ref[...])
        gate, g1, g2 = _conv_vals(ug, hg, wg_ref[...], bg_ref[...])
        da = da_ref[...]
        sg = 1.0 / (1.0 + jnp.exp(-gate))
        dval = da * (gate * sg)
        dgate = da * val * (sg * (1.0 + gate * (1.0 - sg)))
        dc_ref[0] = dval
        dc_ref[1] = dgate

        @pl.when(first)
        def _():
            dwb_ref[...] = jnp.zeros_like(dwb_ref)

        cs = lambda t: jnp.sum(t, axis=0, keepdims=True)
        r8 = lax.broadcasted_iota(jnp.int32, (8, FF_CHUNK), 0)
        rows4 = lambda a, b, c, d: jnp.where(r8 == 0, a, jnp.where(r8 == 1, b, jnp.where(r8 == 2, c, jnp.where(r8 == 3, d, 0.0))))
        dwb_ref[0] += rows4(cs(dval * v2), cs(dval * v1), cs(dval * uv), cs(dval))
        dwb_ref[1] += rows4(cs(dgate * g2), cs(dgate * g1), cs(dgate * ug), cs(dgate))

    specs = _conv_in_specs("cr", layer) + [pl.BlockSpec((None, CONV_BR, FF_CHUNK), lambda j, i: (j, i, 0))]
    return pl.pallas_call(
        body, out_shape=(jax.ShapeDtypeStruct((2, FF_HALF, s, FF_CHUNK), F32), jax.ShapeDtypeStruct((2, FF_HALF, 8, FF_CHUNK), F32)),
        grid=(FF_HALF, s // CONV_BR), in_specs=specs,
        out_specs=(pl.BlockSpec((2, None, CONV_BR, FF_CHUNK), lambda j, i: (0, j, i, 0)),
                   pl.BlockSpec((2, None, 8, FF_CHUNK), lambda j, i: (0, j, 0, 0))),
        compiler_params=_params("parallel", "arbitrary"), name=name)(u, u, u, u, cw, cw, cb, cb, da)


def _conv_bwd_du(dc, cw, layer, *, name):
    n_chunk, s, _ = dc.shape
    nrow = s // CONV_BR
    per = CONV_BR // HALO

    def body(dc_ref, h_ref, w_ref, o_ref):
        last = pl.program_id(1) == nrow - 1
        h = jnp.where(last, 0.0, h_ref[...])
        x = dc_ref[...]
        w = w_ref[...]
        o_ref[...] = (w[2:3, :] * x + w[1:2, :] * _shift_up(x, h, 1) + w[0:1, :] * _shift_up(x, h, 2)).astype(BF16)

    return pl.pallas_call(
        body, out_shape=jax.ShapeDtypeStruct((n_chunk, s, FF_CHUNK), BF16), grid=(n_chunk, nrow),
        in_specs=[pl.BlockSpec((None, CONV_BR, FF_CHUNK), lambda c, i: (c, i, 0)),
                  pl.BlockSpec((None, HALO, FF_CHUNK), lambda c, i: (c, jnp.minimum((i + 1) * per, nrow * per - 1), 0)),
                  pl.BlockSpec((None, None, 3, FF_CHUNK), lambda c, i: (layer, c, 0, 0))],
        out_specs=pl.BlockSpec((None, CONV_BR, FF_CHUNK), lambda c, i: (c, i, 0)),
        compiler_params=_params("parallel", "parallel"), name=name)(dc, dc, cw)


def _rows_of(r):
    return lambda ref, idx: ref.at[:, pl.ds(idx * r, r), :]


def _slot1(ref, idx):
    return ref.at[:, idx]


def _slot0(ref, idx):
    return ref.at[idx]


def _all_gather(shards, full_shapes, places, *, name):
    n = len(shards)

    def body(*refs):
        ins, outs = refs[:n], refs[n:2 * n]
        send_sems, recv_sems, local_sems = refs[2 * n:]
        mx, my, mc = lax.axis_index("x"), lax.axis_index("y"), lax.axis_index("c")
        me, sibling = (mx, my, mc), (mx, my, 1 - mc)
        chips = [(1 - mx, my), (mx, 1 - my), (1 - mx, 1 - my)]

        def win(t, px, py, pc):
            return places[t](outs[t], 4 * px + 2 * py + pc)

        def copy(t, k, block, to, src=None):
            return pltpu.make_async_remote_copy(src_ref=win(t, *block) if src is None else src, dst_ref=win(t, *block),
                                                send_sem=send_sems.at[t, k], recv_sem=recv_sems.at[t, k], device_id=to, device_id_type=MESH)

        mine = [pltpu.make_async_copy(ins[t], win(t, *me), local_sems.at[t]) for t in range(n)]
        for cp in mine:
            cp.start()
        first = []
        for t in range(n):
            first += [copy(t, 0, me, sibling, src=ins[t])] + [copy(t, 1 + j, me, (*chip, mc), src=ins[t]) for j, chip in enumerate(chips)]
        for cp in first:
            cp.start()
        passed = []
        for j, chip in enumerate(chips):
            for t in range(n):
                copy(t, 1 + j, (*chip, mc), me).wait_recv()
                fwd = copy(t, 4 + j, (*chip, mc), sibling)
                fwd.start()
                passed.append(fwd)
        for t in range(n):
            copy(t, 0, sibling, me).wait_recv()
            for j, chip in enumerate(chips):
                copy(t, 4 + j, (*chip, 1 - mc), me).wait_recv()
        for cp in first + passed:
            cp.wait_send()
        for cp in mine:
            cp.wait()

    hbm = pl.BlockSpec(memory_space=pl.ANY)
    return pl.pallas_call(
        body, out_shape=tuple(jax.ShapeDtypeStruct(s, x.dtype) for s, x in zip(full_shapes, shards)),
        in_specs=[hbm] * n, out_specs=(hbm,) * n,
        scratch_shapes=[pltpu.SemaphoreType.DMA((n, 7)), pltpu.SemaphoreType.DMA((n, 7)), pltpu.SemaphoreType.DMA((n,))],
        name=name)(*shards)


FLIPS = [(fx, fy, fc) for fx in (0, 1) for fy in (0, 1) for fc in (0, 1)][1:]


def _exchange_copies(kind, places, src, land, send_sems, recv_sems, local_sems):
    mx, my, mc = lax.axis_index("x"), lax.axis_index("y"), lax.axis_index("c")
    me = 4 * mx + 2 * my + mc
    n = len(src)
    local, remote = [], []
    for t in range(n):
        if kind == "gather":
            local.append(pltpu.make_async_copy(src[t], places[t](land[t], me), local_sems.at[t]))
        else:
            local.append(pltpu.make_async_copy(places[t](src[t], me), land[t].at[me], local_sems.at[t]))
    for k, (fx, fy, fc) in enumerate(FLIPS):
        px, py, pc = mx ^ fx, my ^ fy, mc ^ fc
        peer = 4 * px + 2 * py + pc
        for t in range(n):
            sems = dict(send_sem=send_sems.at[7 * t + k], recv_sem=recv_sems.at[7 * t + k], device_id=(px, py, pc), device_id_type=MESH)
            if kind == "gather":
                pair = [(src[t], places[t](land[t], me)), (src[t], places[t](land[t], peer))]
            else:
                pair = [(places[t](src[t], peer), land[t].at[me]), (places[t](src[t], peer), land[t].at[peer])]
            remote.append([functools.partial(pltpu.make_async_remote_copy, src_ref=s_, dst_ref=d_, **sems) for s_, d_ in pair])
    return local, remote


HBM_SPEC = pl.BlockSpec(memory_space=pltpu.HBM)
SEM_SPEC = pl.BlockSpec(memory_space=pltpu.SEMAPHORE)
SIDE_EFFECT = pltpu.SideEffectType.DATAFLOW_SIDE_EFFECTING


def _exchange_start(kind, srcs, land_shapes, places, after, *, name):
    n = len(srcs)

    def body(*refs):
        src, land = refs[:n], refs[n:2 * n]
        send_sems, recv_sems, local_sems = refs[2 * n + 1:2 * n + 4]
        token = refs[-1]
        local, remote = _exchange_copies(kind, places, src, land, send_sems, recv_sems, local_sems)
        for cp in local:
            cp.start()
        for send, _ in remote:
            send().start()
        token[...] = jnp.zeros_like(token)

    hbm = lambda t: pltpu.with_memory_space_constraint(t, pltpu.HBM)
    lands = [hbm(lax.empty(tuple(s), x.dtype)) for s, x in zip(land_shapes, srcs)]
    out_shape = (pltpu.SemaphoreType.DMA((7 * n,)), pltpu.SemaphoreType.DMA((7 * n,)), pltpu.SemaphoreType.DMA((n,)),
                 *[pltpu.HBM(x.shape, x.dtype) for x in srcs], *[pltpu.HBM(tuple(s), x.dtype) for s, x in zip(land_shapes, srcs)],
                 jax.ShapeDtypeStruct((8, LANES), F32))
    outs = pl.pallas_call(
        body, name=name, out_shape=out_shape, in_specs=[HBM_SPEC] * (2 * n) + [pl.BlockSpec(memory_space=pl.ANY)],
        out_specs=(SEM_SPEC, SEM_SPEC, SEM_SPEC) + (HBM_SPEC,) * (2 * n) + (pl.BlockSpec(memory_space=pltpu.VMEM),),
        input_output_aliases={i: 3 + i for i in range(2 * n)},
        compiler_params=pltpu.CompilerParams(has_side_effects=SIDE_EFFECT))(*[hbm(x) for x in srcs], *lands, after)
    return dict(sems=outs[:3], src=outs[3:3 + n], land=outs[3 + n:3 + 2 * n], token=outs[-1])


def _exchange_wait(kind, started, places, after, *, name):
    n = len(started["src"])

    def body(*refs):
        src, land = refs[:n], refs[n:2 * n]
        send_sems, recv_sems, local_sems = refs[2 * n:2 * n + 3]
        local, remote = _exchange_copies(kind, places, src, land, send_sems, recv_sems, local_sems)
        for cp in local:
            cp.wait()
        for send, arrival in remote:
            send().wait_send()
            arrival().wait_recv()

    out_shape = tuple(pltpu.HBM(x.shape, x.dtype) for x in started["src"]) + tuple(pltpu.HBM(x.shape, x.dtype) for x in started["land"])
    outs = pl.pallas_call(
        body, name=name, out_shape=out_shape,
        in_specs=[HBM_SPEC] * (2 * n) + [SEM_SPEC] * 3 + [pl.BlockSpec(memory_space=pl.ANY)], out_specs=(HBM_SPEC,) * (2 * n),
        input_output_aliases={i: i for i in range(2 * n)},
        compiler_params=pltpu.CompilerParams(has_side_effects=SIDE_EFFECT))(*started["src"], *started["land"], *started["sems"], after)
    return list(outs[n:])


def _adamw(parts, w, m, v, *, br, name):
    layers, r, wd = w.shape
    assert len(parts) == layers

    def body(*refs):
        p_refs = refs[:layers]
        w_ref, m_ref, v_ref, g_ref, d_ref, nm_ref, nv_ref = refs[layers:]
        for k in range(layers):
            @pl.when(pl.program_id(0) == k)
            def _(p_ref=p_refs[k]):
                g = p_ref[0].astype(F32)
                for dev in range(1, N_DEV):
                    g = g + p_ref[dev].astype(F32)
                mm = ADAM_B1 * m_ref[...] + (1.0 - ADAM_B1) * g
                vv = ADAM_B2 * v_ref[...] + (1.0 - ADAM_B2) * (g * g)
                m_hat = mm / (1.0 - ADAM_B1 ** ADAM_STEP)
                v_hat = vv / (1.0 - ADAM_B2 ** ADAM_STEP)
                g_ref[...] = g
                d_ref[...] = -ADAM_LR * (m_hat / (jnp.sqrt(v_hat) + ADAM_EPS) + ADAM_WD * w_ref[...])
                nm_ref[...] = mm
                nv_ref[...] = vv

    p_spec = lambda k: pl.BlockSpec((N_DEV, None, br, wd), lambda l, i: (0, 0, jnp.where(l == k, i, 0), 0))
    blk = pl.BlockSpec((None, br, wd), lambda l, i: (l, i, 0))
    shp = jax.ShapeDtypeStruct((layers, r, wd), F32)
    return pl.pallas_call(body, out_shape=(shp, shp, shp, shp), grid=(layers, r // br),
                          in_specs=[p_spec(k) for k in range(layers)] + [blk, blk, blk], out_specs=(blk, blk, blk, blk),
                          compiler_params=_params("arbitrary", "arbitrary"), name=name)(*parts, w, m, v)


SMALL = ("norm_mix", "norm_mem", "norm_ffn", "b_forget", "conv_b", "norm_final")


def _pack(tensors):
    flat = jnp.concatenate([t.reshape(-1) for t in tensors])
    rows = -(-flat.shape[0] // (PACK_W * PACK_ROW_ALIGN)) * PACK_ROW_ALIGN
    flat = jnp.pad(flat, (0, rows * PACK_W - flat.shape[0]))
    return flat.reshape(1, rows, PACK_W)


def _unpack(buf, shapes):
    flat = buf.reshape(-1)
    out, off = [], 0
    for shp in shapes:
        n = math.prod(shp)
        out.append(flat[off:off + n].reshape(tuple(shp)))
        off += n
    return out


def _fox_permute(w):
    pad = jnp.zeros(w.shape[:-1] + (FOX_P - FOX_IN,), w.dtype)
    return jnp.concatenate([w[..., :3 * D_MIX], w[..., 3 * D_MIX + N_MIX_HEADS:], w[..., 3 * D_MIX:3 * D_MIX + N_MIX_HEADS], pad], axis=-1)


def _fox_unpermute(w):
    return jnp.concatenate([w[..., :3 * D_MIX], w[..., DIL_IN:DIL_IN + N_MIX_HEADS], w[..., 3 * D_MIX:DIL_IN]], axis=-1)


def _bias_layout(c):
    s = c.shape[0]
    ct = c[:, :N_MIX_HEADS].T.reshape(N_MIX_HP, 2, s)
    return jnp.pad(ct, ((0, 0), (0, 6), (0, 0)))


def _bias_grad(dck6):
    s = dck6.shape[2]
    dk = dck6[:, :2, :].reshape(N_MIX_HEADS, s).T
    return jnp.pad(dk, ((0, 0), (0, LANES - N_MIX_HEADS)))


def _device_step(x, mem, target, small, get_weights, put_grads):
    s = x.shape[0]
    mt = mem.shape[0]
    bq = 512
    cos_t, sin_t = _rope_tables(s)
    row = lambda t, l: t[l][None, :]
    saved = []
    h = x
    cb8 = small["conv_b"].reshape(DEPTH, N_DEV, 1, FF_CHUNK)
    for l in range(DEPTH):
        kind, slot = l % 2, l // 2
        wl = dict(get_weights(l, "attn", h))
        xn = _rmsnorm_fwd(h, row(small["norm_mix"], l), br=512, name=f"norm_mix_fwd{l}")
        mn = _rmsnorm_fwd(mem, row(small["norm_mem"], l), br=mt, name=f"norm_mem_fwd{l}")
        proj = _mm(xn, wl["w_in"], "nn", tm=1024, tn=384 if kind == 0 else 512, layer=0, name=f"in_proj{l}")
        kvm = _mm(mn, wl["w_mem_kv"], "nn", tm=mt, tn=512, layer=0, name=f"mem_kv{l}")
        st = dict(h=h, xn=xn, mn=mn, proj=proj, kvm=kvm, w=wl)
        if kind == 0:
            b_pad = jnp.pad(small["b_forget"][slot], (0, LANES - N_MIX_HEADS))[None, :]
            c = _forget_cumsum(proj, b_pad, name=f"forget_cumsum{l}")
            ck6 = _bias_layout(c)
            mix, lse = _attn_fwd(proj, proj, ck6, q_col=0, k_col=N_MIX_HP, v_col=2 * N_MIX_HP, n_hp=N_MIX_HP,
                                 causal=True, bq=min(s, 1024), bk=min(s, 1024), name=f"fox_fwd{l}")
            st.update(b_pad=b_pad, ck6=ck6, mix=mix, lse=lse)
        else:
            qk_r = _rope(proj, cos_t, sin_t, n_cols=2 * N_MIX_HP, out_dtype=F32, br=512, name=f"rope_fwd{l}")
            mix, lses = _dil_fwd(qk_r, proj, name=f"dil_fwd{l}")
            st.update(qk_r=qk_r, lses=lses, mix=mix)
        mo, lse_m = _attn_fwd(proj, kvm, None, q_col=QM_COL, k_col=0, v_col=N_MEM_HP, n_hp=N_MEM_HP,
                              causal=False, bq=bq, bk=mt, name=f"mem_fwd{l}")
        heads = jnp.concatenate([mix.astype(BF16), mo.astype(BF16)], axis=1)
        h1 = _mm(heads, wl["w_out"], "nn", tm=1024, tn=512, res=h, layer=0, name=f"out_proj{l}")
        xf = _rmsnorm_fwd(h1, row(small["norm_ffn"], l), br=512, name=f"norm_ffn_fwd{l}")
        wl.update(get_weights(l, "ffn", xf))
        u = _mm(xf, wl["w_up"], "nn", tm=1024, tn=FF_CHUNK, layer=0, chunk="b", name=f"up_proj{l}")
        a = _conv_fwd(u, wl["conv_w"], cb8[l], 0, name=f"conv_fwd{l}")
        h = _mm(a, wl["w_down"], "nn", tm=1024, tn=512, res=h1, layer=0, chunk="reduce", name=f"down_proj{l}")
        st.update(mo=mo, lse_m=lse_m, heads=heads, h1=h1, xf=xf, u=u, a=a)
        saved.append(st)

    dh, dhb, dg_final, loss = _loss_head(h, target, small["norm_final"][None, :], br=512, name="loss_head")
    gs = {k: [None] * DEPTH for k in ("norm_mix", "norm_mem", "norm_ffn", "conv_b")}
    gs["b_forget"] = [None] * 2
    dep = 0.0
    for l in reversed(range(DEPTH)):
        st = saved[l]
        wl = st["w"]
        gw = {}
        kind, slot = l % 2, l // 2
        da = _mm(dhb, wl["w_down"], "nt", tm=1024, tn=FF_CHUNK, layer=0, chunk="b", name=f"down_dx{l}")
        gw["w_down"] = _mm(st["a"], dhb, "tn", tm=FF_CHUNK, tn=512, out_dtype=BF16, chunk="a", name=f"down_dw{l}")
        dc, dwb = _conv_bwd_dc(st["u"], wl["conv_w"], cb8[l] + dep, da, 0, name=f"conv_bwd_dc{l}")
        dwb = dwb.reshape(N_DEV, 8, FF_CHUNK)
        gw["conv_w"] = dwb
        gs["conv_b"][l] = dwb[:, 3, :].reshape(-1)
        du = _conv_bwd_du(dc.reshape(N_DEV, s, FF_CHUNK), wl["conv_w"], 0, name=f"conv_bwd_du{l}")
        dxf = _mm(du, wl["w_up"], "nt", tm=512, tn=512, layer=0, chunk="reduce", name=f"up_dx{l}")
        gw["w_up"] = _mm(st["xf"], du, "tn", tm=512, tn=FF_CHUNK, out_dtype=BF16, chunk="b", name=f"up_dw{l}")
        dep_ffn = put_grads(l, "ffn", gw)
        dh1, dh1b, dgf = _rmsnorm_bwd(st["h1"], dxf, row(small["norm_ffn"], l) + dep_ffn, dh, br=512, name=f"norm_ffn_bwd{l}")
        gs["norm_ffn"][l] = dgf[0]
        dheads = _mm(dh1b, wl["w_out"], "nt", tm=1024, tn=512, layer=0, name=f"out_dx{l}")
        gw["w_out"] = _mm(st["heads"], dh1b, "tn", tm=512, tn=512, out_dtype=BF16, name=f"out_dw{l}")
        dqm, dkm, dvm = _attn_bwd(st["proj"], st["kvm"], st["mo"], dheads, st["lse_m"], None, q_col=QM_COL, k_col=0,
                                  v_col=N_MEM_HP, o_col=N_MIX_HP, n_hp=N_MEM_HP, causal=False, bq=bq, bk=mt, name=f"mem_bwd{l}")
        dkvm = jnp.concatenate([dkm, dvm], axis=1).astype(BF16)
        gw["w_mem_kv"] = _mm(st["mn"], dkvm, "tn", tm=512, tn=512, out_dtype=BF16, name=f"mem_kv_dw{l}")
        dmn = _mm(dkvm, wl["w_mem_kv"], "nt", tm=mt, tn=512, layer=0, name=f"mem_kv_dx{l}")
        _, _, dgm = _rmsnorm_bwd(mem, dmn, row(small["norm_mem"], l), None, br=mt, name=f"norm_mem_bwd{l}")
        gs["norm_mem"][l] = dgm[0]
        if kind == 0:
            dq, dk, dv, dcq6, dck6 = _attn_bwd(st["proj"], st["proj"], st["mix"], dheads, st["lse"], st["ck6"], q_col=0,
                                               k_col=N_MIX_HP, v_col=2 * N_MIX_HP, o_col=0, n_hp=N_MIX_HP, causal=True,
                                               bq=bq, bk=bq, name=f"fox_bwd{l}")
            dz, db = _forget_cumsum_bwd(st["proj"], st["b_pad"], _bias_grad(dcq6), _bias_grad(dck6), name=f"forget_cumsum_bwd{l}")
            gs["b_forget"][slot] = db[0, :N_MIX_HEADS]
            dproj = jnp.concatenate([dq.astype(BF16), dk.astype(BF16), dv.astype(BF16), dqm.astype(BF16), dz], axis=1)
        else:
            dq_r, dk_r, dv = _dil_bwd(st["qk_r"], st["proj"], st["mix"], dheads, st["lses"], name=f"dil_bwd{l}")
            dq = _rope(dq_r, cos_t, -sin_t, n_cols=N_MIX_HP, out_dtype=BF16, br=512, name=f"rope_bwd_q{l}")
            dk = _rope(dk_r, cos_t, -sin_t, n_cols=N_MIX_HP, out_dtype=BF16, br=512, name=f"rope_bwd_k{l}")
            dproj = jnp.concatenate([dq, dk, dv.astype(BF16), dqm.astype(BF16)], axis=1)
        dxn = _mm(dproj, wl["w_in"], "nt", tm=1024, tn=512, layer=0, name=f"in_dx{l}")
        gw["w_in"] = _mm(st["xn"], dproj, "tn", tm=512, tn=384 if kind == 0 else 512, out_dtype=BF16, name=f"in_dw{l}")
        dh, dhb, dgx = _rmsnorm_bwd(st["h"], dxn, row(small["norm_mix"], l), dh1, br=512, name=f"norm_mix_bwd{l}")
        gs["norm_mix"][l] = dgx[0]
        dep = put_grads(l, "attn", gw)

    grads_s = {k: jnp.stack(v) for k, v in gs.items()}
    grads_s["norm_final"] = dg_final[0]
    return loss[0, 0], dh, grads_s


def kernel(x, mem, norm_mix, norm_mem, norm_ffn, w_in_fox, b_forget, w_in_dil, w_mem_kv, w_out, w_up, conv_w, conv_b, w_down, norm_final, loss_target, m_norm_mix, m_norm_mem, m_norm_ffn, m_w_in_fox, m_b_forget, m_w_in_dil, m_w_mem_kv, m_w_out, m_w_up, m_conv_w, m_conv_b, m_w_down, m_norm_final, v_norm_mix, v_norm_mem, v_norm_ffn, v_w_in_fox, v_b_forget, v_w_in_dil, v_w_mem_kv, v_w_out, v_w_up, v_conv_w, v_conv_b, v_w_down, v_norm_final):
    names = ["norm_mix", "norm_mem", "norm_ffn", "w_in_fox", "b_forget", "w_in_dil", "w_mem_kv", "w_out", "w_up", "conv_w", "conv_b",
             "w_down", "norm_final"]
    w = dict(zip(names, (norm_mix, norm_mem, norm_ffn, w_in_fox, b_forget, w_in_dil, w_mem_kv, w_out, w_up, conv_w, conv_b, w_down, norm_final)))
    m = dict(zip(names, (m_norm_mix, m_norm_mem, m_norm_ffn, m_w_in_fox, m_b_forget, m_w_in_dil, m_w_mem_kv, m_w_out, m_w_up, m_conv_w,
                         m_conv_b, m_w_down, m_norm_final)))
    v = dict(zip(names, (v_norm_mix, v_norm_mem, v_norm_ffn, v_w_in_fox, v_b_forget, v_w_in_dil, v_w_mem_kv, v_w_out, v_w_up, v_conv_w,
                         v_conv_b, v_w_down, v_norm_final)))
    big = ("w_in_fox", "w_in_dil", "w_mem_kv", "w_out", "w_up", "w_down", "conv_w")
    small_shapes = [w[k].shape for k in SMALL]
    dil_c = w_in_dil.shape[2]
    rows = {k: w[k].shape[1] for k in ("w_in_fox", "w_mem_kv", "w_out", "w_down")}

    def places(l):
        w_in_place = _rows_of(rows["w_in_fox"]) if l % 2 == 0 else _slot1
        return [w_in_place, _rows_of(rows["w_mem_kv"]), _rows_of(rows["w_out"]), _slot1, _rows_of(rows["w_down"]), _slot1]

    def full_shapes(l):
        w_in_shape = (1, D_MODEL, FOX_P) if l % 2 == 0 else (1, N_DEV, D_MODEL, dil_c)
        return [w_in_shape, (1, D_MODEL, 2 * D_MEMQ), (1, D_MODEL, D_MODEL), (1, N_DEV, D_MODEL, FF_CHUNK), (1, D_FF, D_MODEL),
                (1, N_DEV, 3, FF_CHUNK)]

    cast = {"w_in_fox": _fox_permute(w_in_fox).astype(BF16), "w_in_dil": w_in_dil.astype(BF16), "w_mem_kv": w_mem_kv.astype(BF16),
            "w_out": w_out.astype(BF16), "w_up": w_up.astype(BF16), "w_down": w_down.astype(BF16), "conv_w": conv_w}
    part_of = {"attn": (0, 1, 2), "ffn": (3, 4, 5)}
    groups = {l: (("attn",), ("ffn",)) if l == 0 else (("attn", "ffn"),) for l in range(DEPTH)}
    members = lambda group: [i for p in group for i in part_of[p]]
    pick = lambda seq, group: [seq[i] for i in members(group)]
    tag = lambda l, group: f"{l}" + ("" if len(group) == 2 else group[0])

    gathers, after = {}, norm_final
    for l in range(DEPTH):
        w_in_shard = cast["w_in_fox" if l % 2 == 0 else "w_in_dil"][l // 2][None]
        shards = [w_in_shard] + [cast[k][l][None] for k in ("w_mem_kv", "w_out", "w_up", "w_down", "conv_w")]
        for group in groups[l]:
            gathers[l, group] = _exchange_start("gather", pick(shards, group), pick(full_shapes(l), group), pick(places(l), group), after,
                                                name=f"weights_gather_start{tag(l, group)}")
            after = gathers[l, group]["token"]
    started = sum(g["token"][0, 0] for g in gathers.values())
    small = {k: w[k] for k in SMALL}
    small["norm_mix"] = norm_mix + started
    landed = {}

    def get_weights(l, part, h):
        group = [g for g in groups[l] if part in g][0]
        if (l, group) not in landed:
            lands = _exchange_wait("gather", gathers[l, group], pick(places(l), group), h, name=f"weights_gather_wait{tag(l, group)}")
            landed[l, group] = dict(zip(members(group), lands))
        got = landed[l, group]
        if part == "ffn":
            return dict(w_up=got[3], w_down=got[4].reshape(1, FF_HALF, FF_CHUNK, D_MODEL), conv_w=got[5])
        w_in = got[0] if l % 2 == 0 else jnp.concatenate([got[0][:, j] for j in range(N_DEV)], axis=-1)
        return dict(w_in=w_in, w_mem_kv=got[1], w_out=got[2])

    scatters, pending = {}, {}

    def put_grads(l, part, g):
        pending.setdefault(l, {}).update(g)
        group = [g_ for g_ in groups[l] if part in g_][0]
        if part == "ffn" and "attn" in group:
            return 0.0
        have = pending[l]
        srcs = {3: lambda: have["w_up"][None], 4: lambda: have["w_down"].reshape(1, D_FF, D_MODEL), 5: lambda: have["conv_w"][None],
                1: lambda: have["w_mem_kv"][None], 2: lambda: have["w_out"][None]}
        if l % 2 == 0:
            srcs[0] = lambda: have["w_in"][None]
        else:
            srcs[0] = lambda: jnp.stack([have["w_in"][:, j * dil_c:(j + 1) * dil_c] for j in range(N_DEV)], axis=0)[None]
        shard_shapes = [cast["w_in_fox" if l % 2 == 0 else "w_in_dil"][l // 2].shape] + \
            [cast[k][l].shape for k in ("w_mem_kv", "w_out", "w_up", "w_down")] + [(8, FF_CHUNK)]
        sources = [srcs[i]() for i in members(group)]
        scatters[l, group] = _exchange_start("scatter", sources, [(N_DEV, 1) + tuple(s) for s in pick(shard_shapes, group)],
                                             pick(places(l), group), sources[0], name=f"grads_scatter_start{tag(l, group)}")
        return scatters[l, group]["token"][0, 0]

    loss, grad_x, gs = _device_step(x[0], mem[0], loss_target[0], small, get_weights, put_grads)

    recv = {}

    def wait_scatter(l, group, after_):
        lands = _exchange_wait("scatter", scatters[l, group], pick(places(l), group), after_, name=f"grads_scatter_wait{tag(l, group)}")
        recv.setdefault(l, {}).update(zip(members(group), lands))

    for l in reversed(range(1, DEPTH)):
        wait_scatter(l, groups[l][0], grad_x)
    wait_scatter(0, ("ffn",), grad_x)
    s_pack = _pack([gs[k] for k in SMALL])
    (s_recv,) = _all_gather([s_pack], [(N_DEV,) + s_pack.shape], [_slot0], name="small_grads_all_gather")

    layer_tensors = ("w_in", "w_mem_kv", "w_out", "w_up", "w_down", "conv_w")
    layer_parts = lambda k: [recv[l][layer_tensors.index(k)] for l in range(DEPTH)]
    to_local = {k: (lambda t: t) for k in big}
    to_local["w_in_fox"] = _fox_permute
    from_local = {k: (lambda t: t) for k in big}
    from_local["w_in_fox"] = _fox_unpermute
    blocks = {"w_in_fox": rows["w_in_fox"], "w_in_dil": 512, "w_mem_kv": rows["w_mem_kv"], "w_out": rows["w_out"], "w_up": 256,
              "w_down": rows["w_down"] // 2, "conv_w": 3}
    outs = {}

    def update(k, parts):
        f = to_local[k]
        outs[k] = [from_local[k](t) for t in _adamw(parts, f(w[k]), f(m[k]), f(v[k]), br=blocks[k], name=f"adamw_{k}")]

    update("w_in_dil", [recv[l][0] for l in range(1, DEPTH, 2)])
    update("w_up", layer_parts("w_up"))
    update("w_down", layer_parts("w_down"))
    update("conv_w", [p[:, :, :3, :] for p in layer_parts("conv_w")])
    wait_scatter(0, ("attn",), outs["w_down"][1])
    update("w_in_fox", [recv[l][0] for l in range(0, DEPTH, 2)])
    update("w_mem_kv", layer_parts("w_mem_kv"))
    update("w_out", layer_parts("w_out"))
    small_outs = _adamw([s_recv], _pack([w[k] for k in SMALL]), _pack([m[k] for k in SMALL]), _pack([v[k] for k in SMALL]),
                        br=s_pack.shape[1], name="adamw_small")
    res = []
    for i, os_ in enumerate(small_outs):
        d = {k: outs[k][i] for k in big}
        d.update(zip(SMALL, _unpack(os_, small_shapes)))
        res.append([d[k] for k in names])
    loss = lax.psum(loss, ("x", "y", "c"))
    return (loss, grad_x[None], *res[0], *res[1], *res[2], *res[3])
```

```python
import functools
import math

import jax
import jax.numpy as jnp
from jax import lax
from jax.experimental import pallas as pl
from jax.experimental.pallas import tpu as pltpu

F32 = jnp.float32
BF16 = jnp.bfloat16

D_MODEL = 1024
HEAD_DIM = 64
N_MIX_HEADS = 12
N_MEM_HEADS = 4
D_MIX = N_MIX_HEADS * HEAD_DIM
D_MEMQ = N_MEM_HEADS * HEAD_DIM
D_FF = 2816
DEPTH = 4
FOX_IN = 3 * D_MIX + N_MIX_HEADS + D_MEMQ
DIL_IN = 3 * D_MIX + D_MEMQ
LANES = 128
FOX_P = DIL_IN + LANES
N_MIX_HP = D_MIX // LANES
N_MEM_HP = D_MEMQ // LANES
QM_COL = 3 * N_MIX_HP
F_COL = DIL_IN // LANES
DILATED_BRANCHES = ((128, 1), (512, 4), (2048, 16))
DIL_L = 128
DIL_UNROLL_FWD = 8
DIL_UNROLL_BWD = 4
ROPE_THETA = 10000.0
NORM_EPS = 1e-6
NEG = -1e30
SCALE = HEAD_DIM ** -0.5
N_DEV = 8

ADAM_LR = 0.001
ADAM_B1 = 0.9
ADAM_B2 = 0.999
ADAM_EPS = 1e-08
ADAM_WD = 0.01
ADAM_STEP = 10

VMEM_LIMIT = 56 * 1024 * 1024
PACK_W = 1024
PACK_ROW_ALIGN = 8

MESH = pl.DeviceIdType.MESH
NT = (((1,), (1,)), ((), ()))
NN = (((1,), (0,)), ((), ()))
TN = (((0,), (0,)), ((), ()))


def _params(*sem):
    return pltpu.CompilerParams(dimension_semantics=sem, vmem_limit_bytes=VMEM_LIMIT)


def _lane_lo(shape):
    return lax.broadcasted_iota(jnp.int32, shape, len(shape) - 1) < HEAD_DIM


def _pair(lo, a, b):
    return jnp.where(lo, a, b)


def _mm(a, b, mode, *, tm, tn, name, out_dtype=F32, res=None, layer=None, chunk=None):
    lead = () if layer is None else (layer,)
    nl = (None,) * len(lead)
    bs = b.shape[len(lead):]
    dims = {"nn": NN, "nt": NT, "tn": TN}[mode]
    reduce_n = 0
    if chunk is None:
        (m, k) = a.shape[::-1] if mode == "tn" else a.shape
        n = bs[0] if mode == "nt" else bs[1]
        grid = (m // tm, n // tn)
        a_spec = pl.BlockSpec((k, tm), lambda i, j: (0, i)) if mode == "tn" else pl.BlockSpec((tm, k), lambda i, j: (i, 0))
        b_spec = pl.BlockSpec(nl + ((tn, k) if mode == "nt" else (k, tn)), lambda i, j: lead + ((j, 0) if mode == "nt" else (0, j)))
        o_spec = pl.BlockSpec((tm, tn), lambda i, j: (i, j))
        out_shape = (m, n)
    elif chunk == "b":
        (m, k) = a.shape[::-1] if mode == "tn" else a.shape
        c, nc = bs[0], (bs[1] if mode == "nt" else bs[2])
        grid = (m // tm, c)
        a_spec = pl.BlockSpec((k, tm), lambda i, j: (0, i)) if mode == "tn" else pl.BlockSpec((tm, k), lambda i, j: (i, 0))
        b_spec = pl.BlockSpec(nl + (None,) + tuple(bs[1:]), lambda i, j: lead + (j, 0, 0))
        o_spec = pl.BlockSpec((None, tm, nc), lambda i, j: (j, i, 0))
        out_shape = (c, m, nc)
    elif chunk == "a":
        assert mode == "tn"
        c, k, mc = a.shape
        n = bs[1]
        grid = (c, n // tn)
        a_spec = pl.BlockSpec((None, k, mc), lambda i, j: (i, 0, 0))
        b_spec = pl.BlockSpec(nl + (k, tn), lambda i, j: lead + (0, j))
        o_spec = pl.BlockSpec((None, mc, tn), lambda i, j: (i, 0, j))
        out_shape = (c, mc, n)
    else:
        reduce_n, m, kc = a.shape
        n = bs[1] if mode == "nt" else bs[2]
        grid = (m // tm, n // tn)
        a_spec = pl.BlockSpec((reduce_n, tm, kc), lambda i, j: (0, i, 0))
        b_spec = pl.BlockSpec(nl + ((reduce_n, tn, kc) if mode == "nt" else (reduce_n, kc, tn)),
                              lambda i, j: lead + ((0, j, 0) if mode == "nt" else (0, 0, j)))
        o_spec = pl.BlockSpec((tm, tn), lambda i, j: (i, j))
        out_shape = (m, n)

    def body(*refs):
        a_ref, b_ref = refs[0], refs[1]
        o_ref = refs[-1]
        dot = lambda x, y: lax.dot_general(x.astype(BF16), y.astype(BF16), dims, preferred_element_type=F32)
        if reduce_n:
            acc = dot(a_ref[0], b_ref[0])
            for r in range(1, reduce_n):
                acc = acc + dot(a_ref[r], b_ref[r])
        else:
            acc = dot(a_ref[...], b_ref[...])
        if res is not None:
            acc = acc + refs[2][...]
        o_ref[...] = acc.astype(o_ref.dtype)

    ins = [a, b] + ([res] if res is not None else [])
    specs = [a_spec, b_spec] + ([o_spec] if res is not None else [])
    return pl.pallas_call(body, out_shape=jax.ShapeDtypeStruct(out_shape, out_dtype), grid=grid,
                          in_specs=specs, out_specs=o_spec, compiler_params=_params("parallel", "parallel"), name=name)(*ins)


def _rmsnorm_fwd(x, g, *, br, name):
    r, d = x.shape

    def body(x_ref, g_ref, o_ref):
        xf = x_ref[...]
        rs = lax.rsqrt(jnp.mean(xf * xf, axis=-1, keepdims=True) + NORM_EPS)
        o_ref[...] = (xf * rs * g_ref[...]).astype(BF16)

    return pl.pallas_call(body, out_shape=jax.ShapeDtypeStruct((r, d), BF16), grid=(r // br,),
                          in_specs=[pl.BlockSpec((br, d), lambda i: (i, 0)), pl.BlockSpec((1, d), lambda i: (0, 0))],
                          out_specs=pl.BlockSpec((br, d), lambda i: (i, 0)), compiler_params=_params("parallel"), name=name)(x, g)


def _rms_bwd_math(x, dy, g):
    d = x.shape[-1]
    rs = lax.rsqrt(jnp.mean(x * x, axis=-1, keepdims=True) + NORM_EPS)
    gy = dy * g
    proj = jnp.sum(x * gy, axis=-1, keepdims=True) * (1.0 / d)
    dx = rs * gy - x * (rs * rs * rs) * proj
    dg = jnp.sum(dy * (x * rs), axis=0, keepdims=True)
    return dx, dg


def _rmsnorm_bwd(x, dy, g, res, *, br, name):
    r, d = x.shape
    has_res = res is not None

    def body(*refs):
        x_ref, dy_ref, g_ref = refs[:3]
        dx_ref, dxb_ref, dg_ref = refs[-3:]
        dx, dg = _rms_bwd_math(x_ref[...], dy_ref[...], g_ref[...])
        if has_res:
            dx = dx + refs[3][...]
        dx_ref[...] = dx
        dxb_ref[...] = dx.astype(BF16)

        @pl.when(pl.program_id(0) == 0)
        def _():
            dg_ref[...] = jnp.zeros_like(dg_ref)

        dg_ref[0:1, :] += dg

    row = pl.BlockSpec((br, d), lambda i: (i, 0))
    ins = [x, dy, g] + ([res] if has_res else [])
    specs = [row, row, pl.BlockSpec((1, d), lambda i: (0, 0))] + ([row] if has_res else [])
    return pl.pallas_call(
        body, out_shape=(jax.ShapeDtypeStruct((r, d), F32), jax.ShapeDtypeStruct((r, d), BF16), jax.ShapeDtypeStruct((8, d), F32)),
        grid=(r // br,), in_specs=specs, out_specs=(row, row, pl.BlockSpec((8, d), lambda i: (0, 0))),
        compiler_params=_params("arbitrary"), name=name)(*ins)


def _loss_head(h, target, g, *, br, name):
    r, d = h.shape

    def body(x_ref, t_ref, g_ref, dx_ref, dxb_ref, dg_ref, loss_ref):
        x = x_ref[...]
        gg = g_ref[...]
        rs = lax.rsqrt(jnp.mean(x * x, axis=-1, keepdims=True) + NORM_EPS)
        err = x * rs * gg - t_ref[...]
        part = jnp.sum(jnp.sum(err * err, axis=1, keepdims=True), axis=0, keepdims=True) * (0.5 / d)
        dx, dg = _rms_bwd_math(x, err * (1.0 / d), gg)
        dx_ref[...] = dx
        dxb_ref[...] = dx.astype(BF16)

        @pl.when(pl.program_id(0) == 0)
        def _():
            dg_ref[...] = jnp.zeros_like(dg_ref)
            loss_ref[...] = jnp.zeros_like(loss_ref)

        dg_ref[0:1, :] += dg
        loss_ref[...] += jnp.broadcast_to(part, loss_ref.shape)

    row = pl.BlockSpec((br, d), lambda i: (i, 0))
    return pl.pallas_call(
        body, out_shape=(jax.ShapeDtypeStruct((r, d), F32), jax.ShapeDtypeStruct((r, d), BF16),
                         jax.ShapeDtypeStruct((8, d), F32), jax.ShapeDtypeStruct((8, LANES), F32)),
        grid=(r // br,), in_specs=[row, row, pl.BlockSpec((1, d), lambda i: (0, 0))],
        out_specs=(row, row, pl.BlockSpec((8, d), lambda i: (0, 0)), pl.BlockSpec((8, LANES), lambda i: (0, 0))),
        compiler_params=_params("arbitrary"), name=name)(h, target, g)


def _split3(x):
    hi = x.astype(BF16)
    r1 = x - hi.astype(F32)
    mid = r1.astype(BF16)
    lo = (r1 - mid.astype(F32)).astype(BF16)
    return hi, mid, lo


def _tri_sum(tri, x):
    hi, mid, lo = _split3(x)
    dot = lambda t: jnp.dot(tri, t, preferred_element_type=F32)
    return dot(hi) + dot(mid) + dot(lo)


def _forget_cumsum(proj, b_pad, *, name):
    s = proj.shape[0]
    blk = LANES

    def body(f_ref, b_ref, c_ref):
        ri = lax.broadcasted_iota(jnp.int32, (blk, blk), 0)
        ci = lax.broadcasted_iota(jnp.int32, (blk, blk), 1)
        tri = (ci <= ri).astype(BF16)
        bias = b_ref[...]

        def step(t, carry):
            rows = pl.ds(pl.multiple_of(t * blk, blk), blk)
            z = f_ref[rows, :] + bias
            lf = jnp.minimum(z, 0.0) - jnp.log(1.0 + jnp.exp(-jnp.abs(z)))
            cs = _tri_sum(tri, lf) + carry
            c_ref[rows, :] = cs
            return cs[blk - 1:blk, :]

        lax.fori_loop(0, s // blk, step, jnp.zeros((1, blk), F32))

    return pl.pallas_call(body, out_shape=jax.ShapeDtypeStruct((s, LANES), F32), grid=(1,),
                          in_specs=[pl.BlockSpec((s, LANES), lambda i: (0, F_COL)), pl.BlockSpec((1, LANES), lambda i: (0, 0))],
                          out_specs=pl.BlockSpec((s, LANES), lambda i: (0, 0)), compiler_params=_params("arbitrary"), name=name)(proj, b_pad)


def _forget_cumsum_bwd(proj, b_pad, dcq, dck, *, name):
    s = proj.shape[0]
    blk = LANES
    nblk = s // blk

    def body(f_ref, b_ref, dcq_ref, dck_ref, dz_ref, db_ref):
        ri = lax.broadcasted_iota(jnp.int32, (blk, blk), 0)
        ci = lax.broadcasted_iota(jnp.int32, (blk, blk), 1)
        triu = (ci >= ri).astype(BF16)
        bias = b_ref[...]

        def step(t, carry):
            tail, dbs = carry
            rows = pl.ds(pl.multiple_of((nblk - 1 - t) * blk, blk), blk)
            dc = dcq_ref[rows, :] - dck_ref[rows, :]
            dlf = _tri_sum(triu, dc) + tail
            z = f_ref[rows, :] + bias
            e = jnp.exp(-jnp.abs(z))
            sig_neg = jnp.where(z >= 0.0, e, 1.0) / (1.0 + e)
            dz = dlf * sig_neg
            dz_ref[rows, :] = dz.astype(BF16)
            return dlf[0:1, :], dbs + jnp.sum(dz, axis=0, keepdims=True)

        _, dbs = lax.fori_loop(0, nblk, step, (jnp.zeros((1, blk), F32), jnp.zeros((1, blk), F32)))
        db_ref[...] = jnp.broadcast_to(dbs, db_ref.shape)

    full = pl.BlockSpec((s, LANES), lambda i: (0, 0))
    return pl.pallas_call(body, out_shape=(jax.ShapeDtypeStruct((s, LANES), BF16), jax.ShapeDtypeStruct((8, LANES), F32)), grid=(1,),
                          in_specs=[pl.BlockSpec((s, LANES), lambda i: (0, F_COL)), pl.BlockSpec((1, LANES), lambda i: (0, 0)), full, full],
                          out_specs=(full, pl.BlockSpec((8, LANES), lambda i: (0, 0))), compiler_params=_params("arbitrary"), name=name)(proj, b_pad, dcq, dck)


def _attn_fwd(q_arr, kv_arr, ck6, *, q_col, k_col, v_col, n_hp, causal, bq, bk, name):
    s = q_arr.shape[0]
    skv = kv_arr.shape[0]
    bias = ck6 is not None
    nq = s // bq
    assert not causal or bq == bk

    def body(*refs):
        q_ref, k_ref, v_ref = refs[:3]
        ck_ref = refs[3] if bias else None
        o_ref, lse_ref = refs[-2:]
        i = pl.program_id(1)
        lo = _lane_lo((bq, LANES))
        q = q_ref[...] * SCALE
        qh = (jnp.where(lo, q, 0.0).astype(BF16), jnp.where(lo, 0.0, q).astype(BF16))

        def block(j, carry, diagonal):
            ks = pl.ds(pl.multiple_of(j * bk, bk), bk)
            k = k_ref[ks, :].astype(BF16)
            v = v_ref[ks, :].astype(BF16)
            if diagonal:
                ok = lax.broadcasted_iota(jnp.int32, (bq, bk), 1) <= lax.broadcasted_iota(jnp.int32, (bq, bk), 0)
            out = []
            for h in range(2):
                m, l, acc = carry[3 * h:3 * h + 3]
                sc = lax.dot_general(qh[h], k, NT, preferred_element_type=F32)
                if bias:
                    sc = sc - ck_ref[0, h:h + 1, ks]
                if diagonal:
                    sc = jnp.where(ok, sc, NEG)
                mn = jnp.maximum(m, jnp.max(sc, axis=1, keepdims=True))
                p = jnp.exp(sc - mn)
                al = jnp.exp(m - mn)
                out += [mn, al * l + jnp.sum(p, axis=1, keepdims=True), al * acc + jnp.dot(p.astype(BF16), v, preferred_element_type=F32)]
            return tuple(out)

        col = lambda v_: jnp.full((bq, 1), v_, F32)
        init = (col(NEG), col(0.0), jnp.zeros((bq, LANES), F32)) * 2
        n_full = i if causal else skv // bk
        carry = lax.fori_loop(0, n_full, functools.partial(block, diagonal=False), init)
        if causal:
            carry = block(i, carry, True)
        m0, l0, a0, m1, l1, a1 = carry
        o_ref[...] = _pair(lo, a0 / l0, a1 / l1)
        lse_ref[0] = _pair(lo, m0 + jnp.log(l0), m1 + jnp.log(l1))

    specs = [pl.BlockSpec((bq, LANES), lambda h, i: (i, q_col + h)),
             pl.BlockSpec((skv, LANES), lambda h, i: (0, k_col + h)),
             pl.BlockSpec((skv, LANES), lambda h, i: (0, v_col + h))]
    ins = [q_arr, kv_arr, kv_arr]
    if bias:
        specs += [pl.BlockSpec((1, 8, skv), lambda h, i: (h, 0, 0))]
        ins += [ck6]
    return pl.pallas_call(
        body, out_shape=(jax.ShapeDtypeStruct((s, n_hp * LANES), F32), jax.ShapeDtypeStruct((n_hp, s, LANES), F32)),
        grid=(n_hp, nq), in_specs=specs,
        out_specs=(pl.BlockSpec((bq, LANES), lambda h, i: (i, h)), pl.BlockSpec((1, bq, LANES), lambda h, i: (h, i, 0))),
        compiler_params=_params("parallel", "parallel"), name=name)(*ins)


def _attn_bwd(q_arr, kv_arr, o_arr, do_arr, lse, ck6, *, q_col, k_col, v_col, o_col, n_hp, causal, bq, bk, name):
    s = q_arr.shape[0]
    skv = kv_arr.shape[0]
    bias = ck6 is not None
    nq = s // bq
    assert not causal or bq == bk

    def body(*refs):
        q_ref, k_ref, v_ref, o_ref, do_ref, lse_ref = refs[:6]
        if bias:
            ck_ref = refs[6]
            dq_ref, dk_ref, dv_ref, dcq_ref, dck_ref = refs[-5:]
        else:
            dq_ref, dk_ref, dv_ref = refs[-3:]
        j = pl.program_id(1)
        lo_q = _lane_lo((bq, LANES))
        lo_k = _lane_lo((bk, LANES))
        k = k_ref[...]
        v = v_ref[...].astype(BF16)
        kb = k.astype(BF16)
        kh = (jnp.where(lo_k, k, 0.0).astype(BF16), jnp.where(lo_k, 0.0, k).astype(BF16))
        if bias:
            pick_k = [(lax.broadcasted_iota(jnp.int32, (8, bk), 0) == h).astype(BF16) for h in range(2)]
            pick_q = [(lax.broadcasted_iota(jnp.int32, (8, bq), 0) == h).astype(BF16) for h in range(2)]

        @pl.when(j == 0)
        def _():
            dq_ref[...] = jnp.zeros_like(dq_ref)
            if bias:
                dcq_ref[...] = jnp.zeros_like(dcq_ref)

        def block(i, carry, diagonal):
            dk_acc, dv_acc, cs = carry
            qs = pl.ds(pl.multiple_of(i * bq, bq), bq)
            q = q_ref[qs, :] * SCALE
            do = do_ref[qs, :]
            dd = do * o_ref[qs, :]
            lse_i = lse_ref[0, qs, :]
            qh = (jnp.where(lo_q, q, 0.0).astype(BF16), jnp.where(lo_q, 0.0, q).astype(BF16))
            doh = (jnp.where(lo_q, do, 0.0).astype(BF16), jnp.where(lo_q, 0.0, do).astype(BF16))
            dh = (jnp.sum(jnp.where(lo_q, dd, 0.0), axis=1, keepdims=True), jnp.sum(jnp.where(lo_q, 0.0, dd), axis=1, keepdims=True))
            if diagonal:
                ok = lax.broadcasted_iota(jnp.int32, (bq, bk), 1) <= lax.broadcasted_iota(jnp.int32, (bq, bk), 0)
            dq_blk = None
            rs = None
            for h in range(2):
                sc = lax.dot_general(qh[h], kb, NT, preferred_element_type=F32)
                if bias:
                    sc = sc - ck_ref[0, h:h + 1, :]
                if diagonal:
                    sc = jnp.where(ok, sc, NEG)
                p = jnp.exp(sc - lse_i[:, h * HEAD_DIM:h * HEAD_DIM + 1])
                ds = p * (lax.dot_general(doh[h], v, NT, preferred_element_type=F32) - dh[h])
                dsb = ds.astype(BF16)
                dv_acc = dv_acc + lax.dot_general(p.astype(BF16), doh[h], TN, preferred_element_type=F32)
                dk_acc = dk_acc + lax.dot_general(dsb, qh[h], TN, preferred_element_type=F32)
                part = jnp.dot(dsb, kh[h], preferred_element_type=F32)
                dq_blk = part if dq_blk is None else dq_blk + part
                if bias:
                    cs = cs + jnp.dot(pick_q[h], dsb, preferred_element_type=F32)
                    row_sums = lax.dot_general(pick_k[h], dsb, NT, preferred_element_type=F32)
                    rs = row_sums if rs is None else rs + row_sums
            dq_ref[qs, :] += dq_blk * SCALE
            if bias:
                dcq_ref[0, :, qs] += rs
            return dk_acc, dv_acc, cs

        carry = (jnp.zeros((bk, LANES), F32), jnp.zeros((bk, LANES), F32), jnp.zeros((8, bk), F32))
        if causal:
            carry = block(j, carry, True)
        dk_acc, dv_acc, cs = lax.fori_loop(j + 1 if causal else 0, nq, functools.partial(block, diagonal=False), carry)
        dk_ref[...] = dk_acc
        dv_ref[...] = dv_acc
        if bias:
            dck_ref[0] = cs

    full_q = lambda c: pl.BlockSpec((s, LANES), lambda h, j: (0, c + h))
    specs = [full_q(q_col),
             pl.BlockSpec((bk, LANES), lambda h, j: (j, k_col + h)),
             pl.BlockSpec((bk, LANES), lambda h, j: (j, v_col + h)),
             full_q(0), full_q(o_col),
             pl.BlockSpec((1, s, LANES), lambda h, j: (h, 0, 0))]
    ins = [q_arr, kv_arr, kv_arr, o_arr, do_arr, lse]
    out_shape = [jax.ShapeDtypeStruct((s, n_hp * LANES), F32), jax.ShapeDtypeStruct((skv, n_hp * LANES), F32),
                 jax.ShapeDtypeStruct((skv, n_hp * LANES), F32)]
    out_specs = [full_q(0), pl.BlockSpec((bk, LANES), lambda h, j: (j, h)), pl.BlockSpec((bk, LANES), lambda h, j: (j, h))]
    if bias:
        specs += [pl.BlockSpec((1, 8, bk), lambda h, j: (h, 0, j))]
        ins += [ck6]
        out_shape += [jax.ShapeDtypeStruct((n_hp, 8, s), F32), jax.ShapeDtypeStruct((n_hp, 8, skv), F32)]
        out_specs += [pl.BlockSpec((1, 8, s), lambda h, j: (h, 0, 0)), pl.BlockSpec((1, 8, bk), lambda h, j: (h, 0, j))]
    return pl.pallas_call(body, out_shape=tuple(out_shape), grid=(n_hp, skv // bk), in_specs=specs, out_specs=tuple(out_specs),
                          compiler_params=_params("parallel", "arbitrary"), name=name)(*ins)


def _rope_tables(s):
    inv = 1.0 / (ROPE_THETA ** (jnp.arange(0, HEAD_DIM, 2, dtype=F32) / HEAD_DIM))
    ang = jnp.arange(s, dtype=F32)[:, None] * inv[None, :]
    cos, sin = jnp.cos(ang), jnp.sin(ang)
    return jnp.tile(cos, (1, 4)), jnp.concatenate([-sin, sin, -sin, sin], axis=1)


def _rope(x_arr, cos_t, sin_t, *, n_cols, out_dtype, br, name):
    s = x_arr.shape[0]

    def body(x_ref, c_ref, s_ref, o_ref):
        cos, sin = c_ref[...], s_ref[...]
        first = (lax.broadcasted_iota(jnp.int32, (br, LANES), 1) % HEAD_DIM) < (HEAD_DIM // 2)
        for j in range(n_cols):
            lanes = slice(j * LANES, (j + 1) * LANES)
            x = x_ref[:, lanes].astype(F32)
            swapped = jnp.where(first, pltpu.roll(x, LANES - HEAD_DIM // 2, 1), pltpu.roll(x, HEAD_DIM // 2, 1))
            o_ref[:, lanes] = (x * cos + swapped * sin).astype(o_ref.dtype)

    tab = pl.BlockSpec((br, LANES), lambda i: (i, 0))
    blk = pl.BlockSpec((br, n_cols * LANES), lambda i: (i, 0))
    return pl.pallas_call(body, out_shape=jax.ShapeDtypeStruct((s, n_cols * LANES), out_dtype), grid=(s // br,),
                          in_specs=[blk, tab, tab], out_specs=blk, compiler_params=_params("parallel"), name=name)(x_arr, cos_t, sin_t)


def _stack_heads(x):
    lo = _lane_lo(x.shape)
    return jnp.concatenate([jnp.where(lo, x, 0.0), jnp.where(lo, 0.0, x)], axis=0).astype(BF16)


def _unstack_heads(x):
    return jnp.where(_lane_lo((DIL_L, LANES)), x[:DIL_L], x[DIL_L:])


def _dil_scores(q_ref, k_ref, cur, prev, has_prev):
    qs = _stack_heads(q_ref[cur, :] * SCALE)
    kk = jnp.concatenate([k_ref[prev, :], k_ref[cur, :]], axis=0).astype(BF16)
    a = lax.broadcasted_iota(jnp.int32, (2 * DIL_L, 2 * DIL_L), 0) & (DIL_L - 1)
    c = lax.broadcasted_iota(jnp.int32, (2 * DIL_L, 2 * DIL_L), 1)
    ok = ((c < DIL_L) & (c >= a) & has_prev) | ((c >= DIL_L) & (c - DIL_L <= a))
    return qs, kk, jnp.where(ok, lax.dot_general(qs, kk, NT, preferred_element_type=F32), NEG)


def _dil_rows(t, dil):
    r, m = t % dil, t // dil
    start = m * (DIL_L * dil) + r
    prev = jnp.maximum(start - DIL_L * dil, 0)
    return pl.ds(start, DIL_L, stride=dil), pl.ds(prev, DIL_L, stride=dil), m > 0


def _softmax3(a, b, c):
    m = jnp.maximum(jnp.maximum(a, b), c)
    ea, eb, ec = jnp.exp(a - m), jnp.exp(b - m), jnp.exp(c - m)
    den = ea + eb + ec
    inv = 1.0 / den
    return ea * inv, eb * inv, ec * inv, m + jnp.log(den)


def _dil_fwd(qk_r, proj, *, name):
    s = qk_r.shape[0]
    nsub = s // DIL_L
    mb = 512

    def body(q_ref, k_ref, v_ref, mix_ref, l1_ref, l2_ref, l3_ref, o1_scr, o2_scr, o3_scr):
        lo = _lane_lo((DIL_L, LANES))
        for (_, dil), o_scr, l_ref in zip(DILATED_BRANCHES, (o1_scr, o2_scr, o3_scr), (l1_ref, l2_ref, l3_ref)):
            def step(t, carry, dil=dil, o_scr=o_scr, l_ref=l_ref):
                cur, prev, has_prev = _dil_rows(t, dil)
                _, _, sc = _dil_scores(q_ref, k_ref, cur, prev, has_prev)
                vv = jnp.concatenate([v_ref[prev, :], v_ref[cur, :]], axis=0).astype(BF16)
                m = jnp.max(sc, axis=1, keepdims=True)
                e = jnp.exp(sc - m)
                den = jnp.sum(e, axis=1, keepdims=True)
                o = jnp.dot((e * (1.0 / den)).astype(BF16), vv, preferred_element_type=F32)
                o_scr[cur, :] = _unstack_heads(o)
                l_ref[cur, :] = _unstack_heads(jnp.broadcast_to(m + jnp.log(den), (2 * DIL_L, LANES)))
                return carry

            lax.fori_loop(0, nsub, step, 0, unroll=DIL_UNROLL_FWD)

        def merge(i, carry):
            rows = pl.ds(pl.multiple_of(i * mb, mb), mb)
            wa, wb, wc, _ = _softmax3(l1_ref[rows, :], l2_ref[rows, :], l3_ref[rows, :])
            mix_ref[rows, :] = wa * o1_scr[rows, :] + wb * o2_scr[rows, :] + wc * o3_scr[rows, :]
            return carry

        lax.fori_loop(0, s // mb, merge, 0)

    col = lambda arr_col: pl.BlockSpec((s, LANES), lambda h: (0, arr_col + h))
    shp = jax.ShapeDtypeStruct((s, D_MIX), F32)
    mix, l1, l2, l3 = pl.pallas_call(
        body, out_shape=(shp, shp, shp, shp), grid=(N_MIX_HP,), in_specs=[col(0), col(N_MIX_HP), col(2 * N_MIX_HP)],
        out_specs=(col(0),) * 4, scratch_shapes=[pltpu.VMEM((s, LANES), F32)] * 3,
        compiler_params=_params("parallel"), name=name)(qk_r, qk_r, proj)
    return mix, (l1, l2, l3)


def _dil_bwd(qk_r, proj, mix, dheads, lses, *, name):
    s = qk_r.shape[0]
    nsub = s // DIL_L
    mb = 512

    def body(q_ref, k_ref, v_ref, mix_ref, dm_ref, l1_ref, l2_ref, l3_ref, dq_ref, dk_ref, dv_ref, lt_scr, dd_scr):
        lo = _lane_lo((DIL_L, LANES))
        lo_m = _lane_lo((mb, LANES))

        def prep(i, carry):
            rows = pl.ds(pl.multiple_of(i * mb, mb), mb)
            _, _, _, lt = _softmax3(l1_ref[rows, :], l2_ref[rows, :], l3_ref[rows, :])
            lt_scr[rows, :] = lt
            dd = dm_ref[rows, :] * mix_ref[rows, :]
            dd_scr[rows, :] = _pair(lo_m, jnp.sum(jnp.where(lo_m, dd, 0.0), axis=1, keepdims=True),
                                    jnp.sum(jnp.where(lo_m, 0.0, dd), axis=1, keepdims=True))
            zero = jnp.zeros((mb, LANES), F32)
            dq_ref[rows, :] = zero
            dk_ref[rows, :] = zero
            dv_ref[rows, :] = zero
            return carry

        lax.fori_loop(0, s // mb, prep, 0)

        for (_, dil), l_ref in zip(DILATED_BRANCHES, (l1_ref, l2_ref, l3_ref)):
            def step(t, carry, dil=dil, l_ref=l_ref):
                cur, prev, has_prev = _dil_rows(t, dil)
                qs, kk, sc = _dil_scores(q_ref, k_ref, cur, prev, has_prev)
                vv = jnp.concatenate([v_ref[prev, :], v_ref[cur, :]], axis=0).astype(BF16)
                lg = l_ref[cur, :]
                w = jnp.exp(lg - lt_scr[cur, :])
                wd = w * dd_scr[cur, :]
                column = lambda x: jnp.concatenate([x[:, 0:1], x[:, HEAD_DIM:HEAD_DIM + 1]], axis=0)
                dos = _stack_heads(w * dm_ref[cur, :])
                p = jnp.exp(sc - column(lg))
                ds = (p * (lax.dot_general(dos, vv, NT, preferred_element_type=F32) - column(wd))).astype(BF16)
                dq_ref[cur, :] += _unstack_heads(jnp.dot(ds, kk, preferred_element_type=F32)) * SCALE
                dkk = lax.dot_general(ds, qs, TN, preferred_element_type=F32)
                dvv = lax.dot_general(p.astype(BF16), dos, TN, preferred_element_type=F32)
                dk_ref[cur, :] += dkk[DIL_L:]
                dv_ref[cur, :] += dvv[DIL_L:]
                dk_ref[prev, :] += dkk[:DIL_L]
                dv_ref[prev, :] += dvv[:DIL_L]
                return carry

            lax.fori_loop(0, nsub, step, 0, unroll=DIL_UNROLL_BWD)

    col = lambda arr_col: pl.BlockSpec((s, LANES), lambda h: (0, arr_col + h))
    shp = jax.ShapeDtypeStruct((s, D_MIX), F32)
    return pl.pallas_call(
        body, out_shape=(shp, shp, shp), grid=(N_MIX_HP,),
        in_specs=[col(0), col(N_MIX_HP), col(2 * N_MIX_HP), col(0), col(0), col(0), col(0), col(0)], out_specs=(col(0),) * 3,
        scratch_shapes=[pltpu.VMEM((s, LANES), F32)] * 2,
        compiler_params=_params("parallel"), name=name)(qk_r, qk_r, proj, mix, dheads, *lses)


CONV_BR = 512
FF_CHUNK = 2 * D_FF // N_DEV
FF_HALF = N_DEV // 2
HALO = 8


def _shift_down(x, halo, k):
    row = lax.broadcasted_iota(jnp.int32, x.shape, 0)
    y = pltpu.roll(x, k, 0)
    for r in range(k):
        y = jnp.where(row == r, halo[HALO - k + r:HALO - k + r + 1, :], y)
    return y


def _shift_up(x, halo, k):
    n = x.shape[0]
    row = lax.broadcasted_iota(jnp.int32, x.shape, 0)
    y = pltpu.roll(x, n - k, 0)
    for r in range(k):
        y = jnp.where(row == n - k + r, halo[r:r + 1, :], y)
    return y


def _conv_vals(u, halo, w, b):
    s1 = _shift_down(u, halo, 1)
    s2 = _shift_down(u, halo, 2)
    return b + w[0:1, :] * s2 + w[1:2, :] * s1 + w[2:3, :] * u, s1, s2


def _conv_in_specs(order, layer):
    rc = (lambda i, j: (i, j)) if order == "rc" else (lambda j, i: (i, j))
    per = CONV_BR // HALO
    main = lambda off: pl.BlockSpec((None, CONV_BR, FF_CHUNK), lambda *g: (off + rc(*g)[1], rc(*g)[0], 0))
    halo = lambda off: pl.BlockSpec((None, HALO, FF_CHUNK), lambda *g: (off + rc(*g)[1], jnp.maximum(rc(*g)[0] * per - 1, 0), 0))
    wspec = lambda off: pl.BlockSpec((None, None, 3, FF_CHUNK), lambda *g: (layer, off + rc(*g)[1], 0, 0))
    bspec = lambda off: pl.BlockSpec((None, 1, FF_CHUNK), lambda *g: (off + rc(*g)[1], 0, 0))
    return [main(0), halo(0), main(FF_HALF), halo(FF_HALF), wspec(0), wspec(FF_HALF), bspec(0), bspec(FF_HALF)]


def _conv_fwd(u, cw, cb, layer, *, name):
    s = u.shape[1]

    def body(uv_ref, hv_ref, ug_ref, hg_ref, wv_ref, wg_ref, bv_ref, bg_ref, o_ref):
        first = pl.program_id(0) == 0
        hv = jnp.where(first, 0.0, hv_ref[...])
        hg = jnp.where(first, 0.0, hg_ref[...])
        val, _, _ = _conv_vals(uv_ref[...], hv, wv_ref[...], bv_ref[...])
        gate, _, _ = _conv_vals(ug_ref[...], hg, wg_ref[...], bg_ref[...])
        o_ref[...] = (gate / (1.0 + jnp.exp(-gate)) * val).astype(BF16)

    return pl.pallas_call(body, out_shape=jax.ShapeDtypeStruct((FF_HALF, s, FF_CHUNK), BF16), grid=(s // CONV_BR, FF_HALF),
                          in_specs=_conv_in_specs("rc", layer), out_specs=pl.BlockSpec((None, CONV_BR, FF_CHUNK), lambda i, j: (j, i, 0)),
                          compiler_params=_params("parallel", "parallel"), name=name)(u, u, u, u, cw, cw, cb, cb)


def _swiglu_bwd(val, gate, da):
    sg = 1.0 / (1.0 + jnp.exp(-gate))
    return da * (gate * sg), da * val * (sg * (1.0 + gate * (1.0 - sg)))


def _conv_bwd(u, cw, cb, da, layer, *, name):
    s = u.shape[1]
    nrow = s // CONV_BR
    per = CONV_BR // HALO

    def body(uv_ref, hv_ref, ug_ref, hg_ref, wv_ref, wg_ref, bv_ref, bg_ref, da_ref, nv_ref, ng_ref, nda_ref, du_ref, dwb_ref):
        i = pl.program_id(1)
        first, last = i == 0, i == nrow - 1
        hv = jnp.where(first, 0.0, hv_ref[...])
        hg = jnp.where(first, 0.0, hg_ref[...])
        uv, ug = uv_ref[...], ug_ref[...]
        wv, wg, bv, bg = wv_ref[...], wg_ref[...], bv_ref[...], bg_ref[...]
        val, v1, v2 = _conv_vals(uv, hv, wv, bv)
        gate, g1, g2 = _conv_vals(ug, hg, wg, bg)
        dval, dgate = _swiglu_bwd(val, gate, da_ref[...])
        val_n, _, _ = _conv_vals(nv_ref[...], uv[CONV_BR - HALO:, :], wv, bv)
        gate_n, _, _ = _conv_vals(ng_ref[...], ug[CONV_BR - HALO:, :], wg, bg)
        dval_n, dgate_n = _swiglu_bwd(val_n, gate_n, nda_ref[...])
        dval_n = jnp.where(last, 0.0, dval_n)
        dgate_n = jnp.where(last, 0.0, dgate_n)
        back = lambda dc, dc_n, w: w[2:3, :] * dc + w[1:2, :] * _shift_up(dc, dc_n, 1) + w[0:1, :] * _shift_up(dc, dc_n, 2)
        du_ref[0] = back(dval, dval_n, wv).astype(BF16)
        du_ref[1] = back(dgate, dgate_n, wg).astype(BF16)

        @pl.when(first)
        def _():
            dwb_ref[...] = jnp.zeros_like(dwb_ref)

        cs = lambda t: jnp.sum(t, axis=0, keepdims=True)
        r8 = lax.broadcasted_iota(jnp.int32, (8, FF_CHUNK), 0)
        rows4 = lambda a, b, c, d: jnp.where(r8 == 0, a, jnp.where(r8 == 1, b, jnp.where(r8 == 2, c, jnp.where(r8 == 3, d, 0.0))))
        dwb_ref[0] += rows4(cs(dval * v2), cs(dval * v1), cs(dval * uv), cs(dval))
        dwb_ref[1] += rows4(cs(dgate * g2), cs(dgate * g1), cs(dgate * ug), cs(dgate))

    nxt = lambda off: pl.BlockSpec((None, HALO, FF_CHUNK), lambda j, i: (off + j, jnp.minimum((i + 1) * per, nrow * per - 1), 0))
    specs = _conv_in_specs("cr", layer) + [pl.BlockSpec((None, CONV_BR, FF_CHUNK), lambda j, i: (j, i, 0)), nxt(0), nxt(FF_HALF), nxt(0)]
    return pl.pallas_call(
        body, out_shape=(jax.ShapeDtypeStruct((2, FF_HALF, s, FF_CHUNK), BF16), jax.ShapeDtypeStruct((2, FF_HALF, 8, FF_CHUNK), F32)),
        grid=(FF_HALF, nrow), in_specs=specs,
        out_specs=(pl.BlockSpec((2, None, CONV_BR, FF_CHUNK), lambda j, i: (0, j, i, 0)),
                   pl.BlockSpec((2, None, 8, FF_CHUNK), lambda j, i: (0, j, 0, 0))),
        compiler_params=_params("parallel", "arbitrary"), name=name)(u, u, u, u, cw, cw, cb, cb, da, u, u, da)


def _rows_of(r):
    return lambda ref, idx: ref.at[:, pl.ds(idx * r, r), :]


def _slot1(ref, idx):
    return ref.at[:, idx]


def _slot0(ref, idx):
    return ref.at[idx]


def _all_gather(shards, full_shapes, places, *, name):
    n = len(shards)

    def body(*refs):
        ins, outs = refs[:n], refs[n:2 * n]
        send_sems, recv_sems, local_sems = refs[2 * n:]
        mx, my, mc = lax.axis_index("x"), lax.axis_index("y"), lax.axis_index("c")
        me, sibling = (mx, my, mc), (mx, my, 1 - mc)
        chips = [(1 - mx, my), (mx, 1 - my), (1 - mx, 1 - my)]

        def win(t, px, py, pc):
            return places[t](outs[t], 4 * px + 2 * py + pc)

        def copy(t, k, block, to, src=None):
            return pltpu.make_async_remote_copy(src_ref=win(t, *block) if src is None else src, dst_ref=win(t, *block),
                                                send_sem=send_sems.at[t, k], recv_sem=recv_sems.at[t, k], device_id=to, device_id_type=MESH)

        mine = [pltpu.make_async_copy(ins[t], win(t, *me), local_sems.at[t]) for t in range(n)]
        for cp in mine:
            cp.start()
        first = []
        for t in range(n):
            first += [copy(t, 0, me, sibling, src=ins[t])] + [copy(t, 1 + j, me, (*chip, mc), src=ins[t]) for j, chip in enumerate(chips)]
        for cp in first:
            cp.start()
        passed = []
        for j, chip in enumerate(chips):
            for t in range(n):
                copy(t, 1 + j, (*chip, mc), me).wait_recv()
                fwd = copy(t, 4 + j, (*chip, mc), sibling)
                fwd.start()
                passed.append(fwd)
        for t in range(n):
            copy(t, 0, sibling, me).wait_recv()
            for j, chip in enumerate(chips):
                copy(t, 4 + j, (*chip, 1 - mc), me).wait_recv()
        for cp in first + passed:
            cp.wait_send()
        for cp in mine:
            cp.wait()

    hbm = pl.BlockSpec(memory_space=pl.ANY)
    return pl.pallas_call(
        body, out_shape=tuple(jax.ShapeDtypeStruct(s, x.dtype) for s, x in zip(full_shapes, shards)),
        in_specs=[hbm] * n, out_specs=(hbm,) * n,
        scratch_shapes=[pltpu.SemaphoreType.DMA((n, 7)), pltpu.SemaphoreType.DMA((n, 7)), pltpu.SemaphoreType.DMA((n,))],
        name=name)(*shards)


FLIPS = [(fx, fy, fc) for fx in (0, 1) for fy in (0, 1) for fc in (0, 1)][1:]


def _exchange_copies(kind, places, src, land, send_sems, recv_sems, local_sems):
    mx, my, mc = lax.axis_index("x"), lax.axis_index("y"), lax.axis_index("c")
    me = 4 * mx + 2 * my + mc
    n = len(src)
    local, remote = [], []
    for t in range(n):
        if kind == "gather":
            local.append(pltpu.make_async_copy(src[t], places[t](land[t], me), local_sems.at[t]))
        else:
            local.append(pltpu.make_async_copy(places[t](src[t], me), land[t].at[me], local_sems.at[t]))
    for k, (fx, fy, fc) in enumerate(FLIPS):
        px, py, pc = mx ^ fx, my ^ fy, mc ^ fc
        peer = 4 * px + 2 * py + pc
        for t in range(n):
            sems = dict(send_sem=send_sems.at[7 * t + k], recv_sem=recv_sems.at[7 * t + k], device_id=(px, py, pc), device_id_type=MESH)
            if kind == "gather":
                pair = [(src[t], places[t](land[t], me)), (src[t], places[t](land[t], peer))]
            else:
                pair = [(places[t](src[t], peer), land[t].at[me]), (places[t](src[t], peer), land[t].at[peer])]
            remote.append([functools.partial(pltpu.make_async_remote_copy, src_ref=s_, dst_ref=d_, **sems) for s_, d_ in pair])
    return local, remote


HBM_SPEC = pl.BlockSpec(memory_space=pltpu.HBM)
SEM_SPEC = pl.BlockSpec(memory_space=pltpu.SEMAPHORE)
SIDE_EFFECT = pltpu.SideEffectType.DATAFLOW_SIDE_EFFECTING


def _exchange_start(kind, srcs, land_shapes, places, after, *, name):
    n = len(srcs)

    def body(*refs):
        src, land = refs[:n], refs[n:2 * n]
        send_sems, recv_sems, local_sems = refs[2 * n + 1:2 * n + 4]
        token = refs[-1]
        local, remote = _exchange_copies(kind, places, src, land, send_sems, recv_sems, local_sems)
        for cp in local:
            cp.start()
        for send, _ in remote:
            send().start()
        token[...] = jnp.zeros_like(token)

    hbm = lambda t: pltpu.with_memory_space_constraint(t, pltpu.HBM)
    lands = [hbm(lax.empty(tuple(s), x.dtype)) for s, x in zip(land_shapes, srcs)]
    out_shape = (pltpu.SemaphoreType.DMA((7 * n,)), pltpu.SemaphoreType.DMA((7 * n,)), pltpu.SemaphoreType.DMA((n,)),
                 *[pltpu.HBM(x.shape, x.dtype) for x in srcs], *[pltpu.HBM(tuple(s), x.dtype) for s, x in zip(land_shapes, srcs)],
                 jax.ShapeDtypeStruct((8, LANES), F32))
    outs = pl.pallas_call(
        body, name=name, out_shape=out_shape, in_specs=[HBM_SPEC] * (2 * n) + [pl.BlockSpec(memory_space=pl.ANY)],
        out_specs=(SEM_SPEC, SEM_SPEC, SEM_SPEC) + (HBM_SPEC,) * (2 * n) + (pl.BlockSpec(memory_space=pltpu.VMEM),),
        input_output_aliases={i: 3 + i for i in range(2 * n)},
        compiler_params=pltpu.CompilerParams(has_side_effects=SIDE_EFFECT))(*[hbm(x) for x in srcs], *lands, after)
    return dict(sems=outs[:3], src=outs[3:3 + n], land=outs[3 + n:3 + 2 * n], token=outs[-1])


def _exchange_wait(kind, started, places, after, *, name):
    n = len(started["src"])

    def body(*refs):
        src, land = refs[:n], refs[n:2 * n]
        send_sems, recv_sems, local_sems = refs[2 * n:2 * n + 3]
        local, remote = _exchange_copies(kind, places, src, land, send_sems, recv_sems, local_sems)
        for cp in local:
            cp.wait()
        for send, arrival in remote:
            send().wait_send()
            arrival().wait_recv()

    out_shape = tuple(pltpu.HBM(x.shape, x.dtype) for x in started["src"]) + tuple(pltpu.HBM(x.shape, x.dtype) for x in started["land"])
    outs = pl.pallas_call(
        body, name=name, out_shape=out_shape,
        in_specs=[HBM_SPEC] * (2 * n) + [SEM_SPEC] * 3 + [pl.BlockSpec(memory_space=pl.ANY)], out_specs=(HBM_SPEC,) * (2 * n),
        input_output_aliases={i: i for i in range(2 * n)},
        compiler_params=pltpu.CompilerParams(has_side_effects=SIDE_EFFECT))(*started["src"], *started["land"], *started["sems"], after)
    return list(outs[n:])


def _adamw(parts, w, m, v, *, br, name):
    layers, r, wd = w.shape
    assert len(parts) == layers

    def body(*refs):
        p_refs = refs[:layers]
        w_ref, m_ref, v_ref, g_ref, d_ref, nm_ref, nv_ref = refs[layers:]
        for k in range(layers):
            @pl.when(pl.program_id(0) == k)
            def _(p_ref=p_refs[k]):
                g = p_ref[0].astype(F32)
                for dev in range(1, N_DEV):
                    g = g + p_ref[dev].astype(F32)
                mm = ADAM_B1 * m_ref[...] + (1.0 - ADAM_B1) * g
                vv = ADAM_B2 * v_ref[...] + (1.0 - ADAM_B2) * (g * g)
                m_hat = mm / (1.0 - ADAM_B1 ** ADAM_STEP)
                v_hat = vv / (1.0 - ADAM_B2 ** ADAM_STEP)
                g_ref[...] = g
                d_ref[...] = -ADAM_LR * (m_hat / (jnp.sqrt(v_hat) + ADAM_EPS) + ADAM_WD * w_ref[...])
                nm_ref[...] = mm
                nv_ref[...] = vv

    p_spec = lambda k: pl.BlockSpec((N_DEV, None, br, wd), lambda l, i: (0, 0, jnp.where(l == k, i, 0), 0))
    blk = pl.BlockSpec((None, br, wd), lambda l, i: (l, i, 0))
    shp = jax.ShapeDtypeStruct((layers, r, wd), F32)
    return pl.pallas_call(body, out_shape=(shp, shp, shp, shp), grid=(layers, r // br),
                          in_specs=[p_spec(k) for k in range(layers)] + [blk, blk, blk], out_specs=(blk, blk, blk, blk),
                          compiler_params=_params("arbitrary", "arbitrary"), name=name)(*parts, w, m, v)


SMALL = ("norm_mix", "norm_mem", "norm_ffn", "b_forget", "conv_b", "norm_final")


def _pack(tensors):
    flat = jnp.concatenate([t.reshape(-1) for t in tensors])
    rows = -(-flat.shape[0] // (PACK_W * PACK_ROW_ALIGN)) * PACK_ROW_ALIGN
    flat = jnp.pad(flat, (0, rows * PACK_W - flat.shape[0]))
    return flat.reshape(1, rows, PACK_W)


def _unpack(buf, shapes):
    flat = buf.reshape(-1)
    out, off = [], 0
    for shp in shapes:
        n = math.prod(shp)
        out.append(flat[off:off + n].reshape(tuple(shp)))
        off += n
    return out


def _fox_permute(w):
    pad = jnp.zeros(w.shape[:-1] + (FOX_P - FOX_IN,), w.dtype)
    return jnp.concatenate([w[..., :3 * D_MIX], w[..., 3 * D_MIX + N_MIX_HEADS:], w[..., 3 * D_MIX:3 * D_MIX + N_MIX_HEADS], pad], axis=-1)


def _fox_unpermute(w):
    return jnp.concatenate([w[..., :3 * D_MIX], w[..., DIL_IN:DIL_IN + N_MIX_HEADS], w[..., 3 * D_MIX:DIL_IN]], axis=-1)


def _bias_layout(c):
    s = c.shape[0]
    ct = c[:, :N_MIX_HEADS].T.reshape(N_MIX_HP, 2, s)
    return jnp.pad(ct, ((0, 0), (0, 6), (0, 0)))


def _bias_grad(dck6):
    s = dck6.shape[2]
    dk = dck6[:, :2, :].reshape(N_MIX_HEADS, s).T
    return jnp.pad(dk, ((0, 0), (0, LANES - N_MIX_HEADS)))


def _device_step(x, mem, target, small, get_weights, put_grads):
    s = x.shape[0]
    mt = mem.shape[0]
    bq = 512
    cos_t, sin_t = _rope_tables(s)
    row = lambda t, l: t[l][None, :]
    saved = []
    h = x
    cb8 = small["conv_b"].reshape(DEPTH, N_DEV, 1, FF_CHUNK)
    for l in range(DEPTH):
        kind, slot = l % 2, l // 2
        wl = dict(get_weights(l, "attn", h))
        xn = _rmsnorm_fwd(h, row(small["norm_mix"], l), br=512, name=f"norm_mix_fwd{l}")
        mn = _rmsnorm_fwd(mem, row(small["norm_mem"], l), br=mt, name=f"norm_mem_fwd{l}")
        proj = _mm(xn, wl["w_in"], "nn", tm=1024, tn=384 if kind == 0 else 512, layer=0, name=f"in_proj{l}")
        kvm = _mm(mn, wl["w_mem_kv"], "nn", tm=mt, tn=512, layer=0, name=f"mem_kv{l}")
        st = dict(h=h, xn=xn, mn=mn, proj=proj, kvm=kvm, w=wl)
        if kind == 0:
            b_pad = jnp.pad(small["b_forget"][slot], (0, LANES - N_MIX_HEADS))[None, :]
            c = _forget_cumsum(proj, b_pad, name=f"forget_cumsum{l}")
            ck6 = _bias_layout(c)
            mix, lse = _attn_fwd(proj, proj, ck6, q_col=0, k_col=N_MIX_HP, v_col=2 * N_MIX_HP, n_hp=N_MIX_HP,
                                 causal=True, bq=min(s, 1024), bk=min(s, 1024), name=f"fox_fwd{l}")
            st.update(b_pad=b_pad, ck6=ck6, mix=mix, lse=lse)
        else:
            qk_r = _rope(proj, cos_t, sin_t, n_cols=2 * N_MIX_HP, out_dtype=F32, br=512, name=f"rope_fwd{l}")
            mix, lses = _dil_fwd(qk_r, proj, name=f"dil_fwd{l}")
            st.update(qk_r=qk_r, lses=lses, mix=mix)
        mo, lse_m = _attn_fwd(proj, kvm, None, q_col=QM_COL, k_col=0, v_col=N_MEM_HP, n_hp=N_MEM_HP,
                              causal=False, bq=bq, bk=mt, name=f"mem_fwd{l}")
        heads = jnp.concatenate([mix.astype(BF16), mo.astype(BF16)], axis=1)
        h1 = _mm(heads, wl["w_out"], "nn", tm=1024, tn=512, res=h, layer=0, name=f"out_proj{l}")
        xf = _rmsnorm_fwd(h1, row(small["norm_ffn"], l), br=512, name=f"norm_ffn_fwd{l}")
        wl.update(get_weights(l, "ffn", xf))
        u = _mm(xf, wl["w_up"], "nn", tm=1024, tn=FF_CHUNK, layer=0, chunk="b", name=f"up_proj{l}")
        a = _conv_fwd(u, wl["conv_w"], cb8[l], 0, name=f"conv_fwd{l}")
        h = _mm(a, wl["w_down"], "nn", tm=1024, tn=512, res=h1, layer=0, chunk="reduce", name=f"down_proj{l}")
        st.update(mo=mo, lse_m=lse_m, heads=heads, h1=h1, xf=xf, u=u, a=a)
        saved.append(st)

    dh, dhb, dg_final, loss = _loss_head(h, target, small["norm_final"][None, :], br=512, name="loss_head")
    gs = {k: [None] * DEPTH for k in ("norm_mix", "norm_mem", "norm_ffn", "conv_b")}
    gs["b_forget"] = [None] * 2
    dep = 0.0
    for l in reversed(range(DEPTH)):
        st = saved[l]
        wl = st["w"]
        gw = {}
        kind, slot = l % 2, l // 2
        da = _mm(dhb, wl["w_down"], "nt", tm=1024, tn=FF_CHUNK, layer=0, chunk="b", name=f"down_dx{l}")
        gw["w_down"] = _mm(st["a"], dhb, "tn", tm=FF_CHUNK, tn=512, out_dtype=BF16, chunk="a", name=f"down_dw{l}")
        du, dwb = _conv_bwd(st["u"], wl["conv_w"], cb8[l] + dep, da, 0, name=f"conv_bwd{l}")
        du = du.reshape(N_DEV, s, FF_CHUNK)
        dwb = dwb.reshape(N_DEV, 8, FF_CHUNK)
        gw["conv_w"] = dwb
        gs["conv_b"][l] = dwb[:, 3, :].reshape(-1)
        dxf = _mm(du, wl["w_up"], "nt", tm=512, tn=512, layer=0, chunk="reduce", name=f"up_dx{l}")
        gw["w_up"] = _mm(st["xf"], du, "tn", tm=512, tn=FF_CHUNK, out_dtype=BF16, chunk="b", name=f"up_dw{l}")
        dep_ffn = put_grads(l, "ffn", gw)
        dh1, dh1b, dgf = _rmsnorm_bwd(st["h1"], dxf, row(small["norm_ffn"], l) + dep_ffn, dh, br=512, name=f"norm_ffn_bwd{l}")
        gs["norm_ffn"][l] = dgf[0]
        dheads = _mm(dh1b, wl["w_out"], "nt", tm=1024, tn=512, layer=0, name=f"out_dx{l}")
        gw["w_out"] = _mm(st["heads"], dh1b, "tn", tm=512, tn=512, out_dtype=BF16, name=f"out_dw{l}")
        dqm, dkm, dvm = _attn_bwd(st["proj"], st["kvm"], st["mo"], dheads, st["lse_m"], None, q_col=QM_COL, k_col=0,
                                  v_col=N_MEM_HP, o_col=N_MIX_HP, n_hp=N_MEM_HP, causal=False, bq=bq, bk=mt, name=f"mem_bwd{l}")
        dkvm = jnp.concatenate([dkm, dvm], axis=1).astype(BF16)
        gw["w_mem_kv"] = _mm(st["mn"], dkvm, "tn", tm=512, tn=512, out_dtype=BF16, name=f"mem_kv_dw{l}")
        dmn = _mm(dkvm, wl["w_mem_kv"], "nt", tm=mt, tn=512, layer=0, name=f"mem_kv_dx{l}")
        _, _, dgm = _rmsnorm_bwd(mem, dmn, row(small["norm_mem"], l), None, br=mt, name=f"norm_mem_bwd{l}")
        gs["norm_mem"][l] = dgm[0]
        if kind == 0:
            dq, dk, dv, dcq6, dck6 = _attn_bwd(st["proj"], st["proj"], st["mix"], dheads, st["lse"], st["ck6"], q_col=0,
                                               k_col=N_MIX_HP, v_col=2 * N_MIX_HP, o_col=0, n_hp=N_MIX_HP, causal=True,
                                               bq=bq, bk=bq, name=f"fox_bwd{l}")
            dz, db = _forget_cumsum_bwd(st["proj"], st["b_pad"], _bias_grad(dcq6), _bias_grad(dck6), name=f"forget_cumsum_bwd{l}")
            gs["b_forget"][slot] = db[0, :N_MIX_HEADS]
            dproj = jnp.concatenate([dq.astype(BF16), dk.astype(BF16), dv.astype(BF16), dqm.astype(BF16), dz], axis=1)
        else:
            dq_r, dk_r, dv = _dil_bwd(st["qk_r"], st["proj"], st["mix"], dheads, st["lses"], name=f"dil_bwd{l}")
            dq = _rope(dq_r, cos_t, -sin_t, n_cols=N_MIX_HP, out_dtype=BF16, br=512, name=f"rope_bwd_q{l}")
            dk = _rope(dk_r, cos_t, -sin_t, n_cols=N_MIX_HP, out_dtype=BF16, br=512, name=f"rope_bwd_k{l}")
            dproj = jnp.concatenate([dq, dk, dv.astype(BF16), dqm.astype(BF16)], axis=1)
        dxn = _mm(dproj, wl["w_in"], "nt", tm=1024, tn=512, layer=0, name=f"in_dx{l}")
        gw["w_in"] = _mm(st["xn"], dproj, "tn", tm=512, tn=384 if kind == 0 else 512, out_dtype=BF16, name=f"in_dw{l}")
        dh, dhb, dgx = _rmsnorm_bwd(st["h"], dxn, row(small["norm_mix"], l), dh1, br=512, name=f"norm_mix_bwd{l}")
        gs["norm_mix"][l] = dgx[0]
        dep = put_grads(l, "attn", gw)

    grads_s = {k: jnp.stack(v) for k, v in gs.items()}
    grads_s["norm_final"] = dg_final[0]
    return loss[0, 0], dh, grads_s


def kernel(x, mem, norm_mix, norm_mem, norm_ffn, w_in_fox, b_forget, w_in_dil, w_mem_kv, w_out, w_up, conv_w, conv_b, w_down, norm_final, loss_target, m_norm_mix, m_norm_mem, m_norm_ffn, m_w_in_fox, m_b_forget, m_w_in_dil, m_w_mem_kv, m_w_out, m_w_up, m_conv_w, m_conv_b, m_w_down, m_norm_final, v_norm_mix, v_norm_mem, v_norm_ffn, v_w_in_fox, v_b_forget, v_w_in_dil, v_w_mem_kv, v_w_out, v_w_up, v_conv_w, v_conv_b, v_w_down, v_norm_final):
    names = ["norm_mix", "norm_mem", "norm_ffn", "w_in_fox", "b_forget", "w_in_dil", "w_mem_kv", "w_out", "w_up", "conv_w", "conv_b",
             "w_down", "norm_final"]
    w = dict(zip(names, (norm_mix, norm_mem, norm_ffn, w_in_fox, b_forget, w_in_dil, w_mem_kv, w_out, w_up, conv_w, conv_b, w_down, norm_final)))
    m = dict(zip(names, (m_norm_mix, m_norm_mem, m_norm_ffn, m_w_in_fox, m_b_forget, m_w_in_dil, m_w_mem_kv, m_w_out, m_w_up, m_conv_w,
                         m_conv_b, m_w_down, m_norm_final)))
    v = dict(zip(names, (v_norm_mix, v_norm_mem, v_norm_ffn, v_w_in_fox, v_b_forget, v_w_in_dil, v_w_mem_kv, v_w_out, v_w_up, v_conv_w,
                         v_conv_b, v_w_down, v_norm_final)))
    big = ("w_in_fox", "w_in_dil", "w_mem_kv", "w_out", "w_up", "w_down", "conv_w")
    small_shapes = [w[k].shape for k in SMALL]
    dil_c = w_in_dil.shape[2]
    rows = {k: w[k].shape[1] for k in ("w_in_fox", "w_mem_kv", "w_out", "w_down")}

    def places(l):
        w_in_place = _rows_of(rows["w_in_fox"]) if l % 2 == 0 else _slot1
        return [w_in_place, _rows_of(rows["w_mem_kv"]), _rows_of(rows["w_out"]), _slot1, _rows_of(rows["w_down"]), _slot1]

    def full_shapes(l):
        w_in_shape = (1, D_MODEL, FOX_P) if l % 2 == 0 else (1, N_DEV, D_MODEL, dil_c)
        return [w_in_shape, (1, D_MODEL, 2 * D_MEMQ), (1, D_MODEL, D_MODEL), (1, N_DEV, D_MODEL, FF_CHUNK), (1, D_FF, D_MODEL),
                (1, N_DEV, 3, FF_CHUNK)]

    cast = {"w_in_fox": _fox_permute(w_in_fox).astype(BF16), "w_in_dil": w_in_dil.astype(BF16), "w_mem_kv": w_mem_kv.astype(BF16),
            "w_out": w_out.astype(BF16), "w_up": w_up.astype(BF16), "w_down": w_down.astype(BF16), "conv_w": conv_w}
    part_of = {"attn": (0, 1, 2), "ffn": (3, 4, 5)}
    groups = {l: (("attn",), ("ffn",)) if l == 0 else (("attn", "ffn"),) for l in range(DEPTH)}
    members = lambda group: [i for p in group for i in part_of[p]]
    pick = lambda seq, group: [seq[i] for i in members(group)]
    tag = lambda l, group: f"{l}" + ("" if len(group) == 2 else group[0])

    gathers, after = {}, norm_final
    for l in range(DEPTH):
        w_in_shard = cast["w_in_fox" if l % 2 == 0 else "w_in_dil"][l // 2][None]
        shards = [w_in_shard] + [cast[k][l][None] for k in ("w_mem_kv", "w_out", "w_up", "w_down", "conv_w")]
        for group in groups[l]:
            gathers[l, group] = _exchange_start("gather", pick(shards, group), pick(full_shapes(l), group), pick(places(l), group), after,
                                                name=f"weights_gather_start{tag(l, group)}")
            after = gathers[l, group]["token"]
    started = sum(g["token"][0, 0] for g in gathers.values())
    small = {k: w[k] for k in SMALL}
    small["norm_mix"] = norm_mix + started
    landed = {}

    def get_weights(l, part, h):
        group = [g for g in groups[l] if part in g][0]
        if (l, group) not in landed:
            lands = _exchange_wait("gather", gathers[l, group], pick(places(l), group), h, name=f"weights_gather_wait{tag(l, group)}")
            landed[l, group] = dict(zip(members(group), lands))
        got = landed[l, group]
        if part == "ffn":
            return dict(w_up=got[3], w_down=got[4].reshape(1, FF_HALF, FF_CHUNK, D_MODEL), conv_w=got[5])
        w_in = got[0] if l % 2 == 0 else jnp.concatenate([got[0][:, j] for j in range(N_DEV)], axis=-1)
        return dict(w_in=w_in, w_mem_kv=got[1], w_out=got[2])

    scatters, pending = {}, {}

    def put_grads(l, part, g):
        pending.setdefault(l, {}).update(g)
        group = [g_ for g_ in groups[l] if part in g_][0]
        if part == "ffn" and "attn" in group:
            return 0.0
        have = pending[l]
        srcs = {3: lambda: have["w_up"][None], 4: lambda: have["w_down"].reshape(1, D_FF, D_MODEL), 5: lambda: have["conv_w"][None],
                1: lambda: have["w_mem_kv"][None], 2: lambda: have["w_out"][None]}
        if l % 2 == 0:
            srcs[0] = lambda: have["w_in"][None]
        else:
            srcs[0] = lambda: jnp.stack([have["w_in"][:, j * dil_c:(j + 1) * dil_c] for j in range(N_DEV)], axis=0)[None]
        shard_shapes = [cast["w_in_fox" if l % 2 == 0 else "w_in_dil"][l // 2].shape] + \
            [cast[k][l].shape for k in ("w_mem_kv", "w_out", "w_up", "w_down")] + [(8, FF_CHUNK)]
        sources = [srcs[i]() for i in members(group)]
        scatters[l, group] = _exchange_start("scatter", sources, [(N_DEV, 1) + tuple(s) for s in pick(shard_shapes, group)],
                                             pick(places(l), group), sources[0], name=f"grads_scatter_start{tag(l, group)}")
        return scatters[l, group]["token"][0, 0]

    loss, grad_x, gs = _device_step(x[0], mem[0], loss_target[0], small, get_weights, put_grads)

    recv = {}

    def wait_scatter(l, group, after_):
        lands = _exchange_wait("scatter", scatters[l, group], pick(places(l), group), after_, name=f"grads_scatter_wait{tag(l, group)}")
        recv.setdefault(l, {}).update(zip(members(group), lands))

    for l in reversed(range(1, DEPTH)):
        wait_scatter(l, groups[l][0], grad_x)
    wait_scatter(0, ("ffn",), grad_x)
    s_pack = _pack([gs[k] for k in SMALL])
    (s_recv,) = _all_gather([s_pack], [(N_DEV,) + s_pack.shape], [_slot0], name="small_grads_all_gather")

    layer_tensors = ("w_in", "w_mem_kv", "w_out", "w_up", "w_down", "conv_w")
    layer_parts = lambda k: [recv[l][layer_tensors.index(k)] for l in range(DEPTH)]
    to_local = {k: (lambda t: t) for k in big}
    to_local["w_in_fox"] = _fox_permute
    from_local = {k: (lambda t: t) for k in big}
    from_local["w_in_fox"] = _fox_unpermute
    blocks = {"w_in_fox": rows["w_in_fox"], "w_in_dil": 512, "w_mem_kv": rows["w_mem_kv"], "w_out": rows["w_out"], "w_up": 256,
              "w_down": rows["w_down"] // 2, "conv_w": 3}
    outs = {}

    def update(k, parts):
        f = to_local[k]
        outs[k] = [from_local[k](t) for t in _adamw(parts, f(w[k]), f(m[k]), f(v[k]), br=blocks[k], name=f"adamw_{k}")]

    update("w_in_dil", [recv[l][0] for l in range(1, DEPTH, 2)])
    update("w_up", layer_parts("w_up"))
    update("w_down", layer_parts("w_down"))
    update("conv_w", [p[:, :, :3, :] for p in layer_parts("conv_w")])
    wait_scatter(0, ("attn",), outs["w_down"][1])
    update("w_in_fox", [recv[l][0] for l in range(0, DEPTH, 2)])
    update("w_mem_kv", layer_parts("w_mem_kv"))
    update("w_out", layer_parts("w_out"))
    small_outs = _adamw([s_recv], _pack([w[k] for k in SMALL]), _pack([m[k] for k in SMALL]), _pack([v[k] for k in SMALL]),
                        br=s_pack.shape[1], name="adamw_small")
    res = []
    for i, os_ in enumerate(small_outs):
        d = {k: outs[k][i] for k in big}
        d.update(zip(SMALL, _unpack(os_, small_shapes)))
        res.append([d[k] for k in names])
    loss = lax.psum(loss, ("x", "y", "c"))
    return (loss, grad_x[None], *res[0], *res[1], *res[2], *res[3])
```

```python
import functools
import math

import jax
import jax.numpy as jnp
from jax import lax
from jax.experimental import pallas as pl
from jax.experimental.pallas import tpu as pltpu

F32 = jnp.float32
BF16 = jnp.bfloat16

D_MODEL = 1024
HEAD_DIM = 64
N_MIX_HEADS = 12
N_MEM_HEADS = 4
D_MIX = N_MIX_HEADS * HEAD_DIM
D_MEMQ = N_MEM_HEADS * HEAD_DIM
D_FF = 2816
DEPTH = 4
FOX_IN = 3 * D_MIX + N_MIX_HEADS + D_MEMQ
DIL_IN = 3 * D_MIX + D_MEMQ
LANES = 128
FOX_P = DIL_IN + LANES
N_MIX_HP = D_MIX // LANES
N_MEM_HP = D_MEMQ // LANES
QM_COL = 3 * N_MIX_HP
F_COL = DIL_IN // LANES
DILATED_BRANCHES = ((128, 1), (512, 4), (2048, 16))
DIL_L = 128
DIL_UNROLL_FWD = 8
DIL_UNROLL_BWD = 4
ROPE_THETA = 10000.0
NORM_EPS = 1e-6
NEG = -1e30
SCALE = HEAD_DIM ** -0.5
N_DEV = 8

ADAM_LR = 0.001
ADAM_B1 = 0.9
ADAM_B2 = 0.999
ADAM_EPS = 1e-08
ADAM_WD = 0.01
ADAM_STEP = 10

VMEM_LIMIT = 56 * 1024 * 1024
PACK_W = 1024
PACK_ROW_ALIGN = 8

MESH = pl.DeviceIdType.MESH
NT = (((1,), (1,)), ((), ()))
NN = (((1,), (0,)), ((), ()))
TN = (((0,), (0,)), ((), ()))


def _params(*sem):
    return pltpu.CompilerParams(dimension_semantics=sem, vmem_limit_bytes=VMEM_LIMIT)


def _lane_lo(shape):
    return lax.broadcasted_iota(jnp.int32, shape, len(shape) - 1) < HEAD_DIM


def _pair(lo, a, b):
    return jnp.where(lo, a, b)


def _mm(a, b, mode, *, tm, tn, name, out_dtype=F32, res=None, layer=None, chunk=None):
    lead = () if layer is None else (layer,)
    nl = (None,) * len(lead)
    bs = b.shape[len(lead):]
    dims = {"nn": NN, "nt": NT, "tn": TN}[mode]
    reduce_n = 0
    if chunk is None:
        (m, k) = a.shape[::-1] if mode == "tn" else a.shape
        n = bs[0] if mode == "nt" else bs[1]
        grid = (m // tm, n // tn)
        a_spec = pl.BlockSpec((k, tm), lambda i, j: (0, i)) if mode == "tn" else pl.BlockSpec((tm, k), lambda i, j: (i, 0))
        b_spec = pl.BlockSpec(nl + ((tn, k) if mode == "nt" else (k, tn)), lambda i, j: lead + ((j, 0) if mode == "nt" else (0, j)))
        o_spec = pl.BlockSpec((tm, tn), lambda i, j: (i, j))
        out_shape = (m, n)
    elif chunk == "b":
        (m, k) = a.shape[::-1] if mode == "tn" else a.shape
        c, nc = bs[0], (bs[1] if mode == "nt" else bs[2])
        grid = (m // tm, c)
        a_spec = pl.BlockSpec((k, tm), lambda i, j: (0, i)) if mode == "tn" else pl.BlockSpec((tm, k), lambda i, j: (i, 0))
        b_spec = pl.BlockSpec(nl + (None,) + tuple(bs[1:]), lambda i, j: lead + (j, 0, 0))
        o_spec = pl.BlockSpec((None, tm, nc), lambda i, j: (j, i, 0))
        out_shape = (c, m, nc)
    elif chunk == "a":
        assert mode == "tn"
        c, k, mc = a.shape
        n = bs[1]
        grid = (c, n // tn)
        a_spec = pl.BlockSpec((None, k, mc), lambda i, j: (i, 0, 0))
        b_spec = pl.BlockSpec(nl + (k, tn), lambda i, j: lead + (0, j))
        o_spec = pl.BlockSpec((None, mc, tn), lambda i, j: (i, 0, j))
        out_shape = (c, mc, n)
    else:
        reduce_n, m, kc = a.shape
        n = bs[1] if mode == "nt" else bs[2]
        grid = (m // tm, n // tn)
        a_spec = pl.BlockSpec((reduce_n, tm, kc), lambda i, j: (0, i, 0))
        b_spec = pl.BlockSpec(nl + ((reduce_n, tn, kc) if mode == "nt" else (reduce_n, kc, tn)),
                              lambda i, j: lead + ((0, j, 0) if mode == "nt" else (0, 0, j)))
        o_spec = pl.BlockSpec((tm, tn), lambda i, j: (i, j))
        out_shape = (m, n)

    def body(*refs):
        a_ref, b_ref = refs[0], refs[1]
        o_ref = refs[-1]
        dot = lambda x, y: lax.dot_general(x.astype(BF16), y.astype(BF16), dims, preferred_element_type=F32)
        if reduce_n:
            acc = dot(a_ref[0], b_ref[0])
            for r in range(1, reduce_n):
                acc = acc + dot(a_ref[r], b_ref[r])
        else:
            acc = dot(a_ref[...], b_ref[...])
        if res is not None:
            acc = acc + refs[2][...]
        o_ref[...] = acc.astype(o_ref.dtype)

    ins = [a, b] + ([res] if res is not None else [])
    specs = [a_spec, b_spec] + ([o_spec] if res is not None else [])
    return pl.pallas_call(body, out_shape=jax.ShapeDtypeStruct(out_shape, out_dtype), grid=grid,
                          in_specs=specs, out_specs=o_spec, compiler_params=_params("parallel", "parallel"), name=name)(*ins)


def _rmsnorm_fwd(x, g, *, br, name):
    r, d = x.shape

    def body(x_ref, g_ref, o_ref):
        xf = x_ref[...]
        rs = lax.rsqrt(jnp.mean(xf * xf, axis=-1, keepdims=True) + NORM_EPS)
        o_ref[...] = (xf * rs * g_ref[...]).astype(BF16)

    return pl.pallas_call(body, out_shape=jax.ShapeDtypeStruct((r, d), BF16), grid=(r // br,),
                          in_specs=[pl.BlockSpec((br, d), lambda i: (i, 0)), pl.BlockSpec((1, d), lambda i: (0, 0))],
                          out_specs=pl.BlockSpec((br, d), lambda i: (i, 0)), compiler_params=_params("parallel"), name=name)(x, g)


def _rms_bwd_math(x, dy, g):
    d = x.shape[-1]
    rs = lax.rsqrt(jnp.mean(x * x, axis=-1, keepdims=True) + NORM_EPS)
    gy = dy * g
    proj = jnp.sum(x * gy, axis=-1, keepdims=True) * (1.0 / d)
    dx = rs * gy - x * (rs * rs * rs) * proj
    dg = jnp.sum(dy * (x * rs), axis=0, keepdims=True)
    return dx, dg


def _rmsnorm_bwd(x, dy, g, res, *, br, name):
    r, d = x.shape
    has_res = res is not None

    def body(*refs):
        x_ref, dy_ref, g_ref = refs[:3]
        dx_ref, dxb_ref, dg_ref = refs[-3:]
        dx, dg = _rms_bwd_math(x_ref[...], dy_ref[...], g_ref[...])
        if has_res:
            dx = dx + refs[3][...]
        dx_ref[...] = dx
        dxb_ref[...] = dx.astype(BF16)

        @pl.when(pl.program_id(0) == 0)
        def _():
            dg_ref[...] = jnp.zeros_like(dg_ref)

        dg_ref[0:1, :] += dg

    row = pl.BlockSpec((br, d), lambda i: (i, 0))
    ins = [x, dy, g] + ([res] if has_res else [])
    specs = [row, row, pl.BlockSpec((1, d), lambda i: (0, 0))] + ([row] if has_res else [])
    return pl.pallas_call(
        body, out_shape=(jax.ShapeDtypeStruct((r, d), F32), jax.ShapeDtypeStruct((r, d), BF16), jax.ShapeDtypeStruct((8, d), F32)),
        grid=(r // br,), in_specs=specs, out_specs=(row, row, pl.BlockSpec((8, d), lambda i: (0, 0))),
        compiler_params=_params("arbitrary"), name=name)(*ins)


def _loss_head(h, target, g, *, br, name):
    r, d = h.shape

    def body(x_ref, t_ref, g_ref, dx_ref, dxb_ref, dg_ref, loss_ref):
        x = x_ref[...]
        gg = g_ref[...]
        rs = lax.rsqrt(jnp.mean(x * x, axis=-1, keepdims=True) + NORM_EPS)
        err = x * rs * gg - t_ref[...]
        part = jnp.sum(jnp.sum(err * err, axis=1, keepdims=True), axis=0, keepdims=True) * (0.5 / d)
        dx, dg = _rms_bwd_math(x, err * (1.0 / d), gg)
        dx_ref[...] = dx
        dxb_ref[...] = dx.astype(BF16)

        @pl.when(pl.program_id(0) == 0)
        def _():
            dg_ref[...] = jnp.zeros_like(dg_ref)
            loss_ref[...] = jnp.zeros_like(loss_ref)

        dg_ref[0:1, :] += dg
        loss_ref[...] += jnp.broadcast_to(part, loss_ref.shape)

    row = pl.BlockSpec((br, d), lambda i: (i, 0))
    return pl.pallas_call(
        body, out_shape=(jax.ShapeDtypeStruct((r, d), F32), jax.ShapeDtypeStruct((r, d), BF16),
                         jax.ShapeDtypeStruct((8, d), F32), jax.ShapeDtypeStruct((8, LANES), F32)),
        grid=(r // br,), in_specs=[row, row, pl.BlockSpec((1, d), lambda i: (0, 0))],
        out_specs=(row, row, pl.BlockSpec((8, d), lambda i: (0, 0)), pl.BlockSpec((8, LANES), lambda i: (0, 0))),
        compiler_params=_params("arbitrary"), name=name)(h, target, g)


def _split3(x):
    hi = x.astype(BF16)
    r1 = x - hi.astype(F32)
    mid = r1.astype(BF16)
    lo = (r1 - mid.astype(F32)).astype(BF16)
    return hi, mid, lo


def _tri_sum(tri, x):
    hi, mid, lo = _split3(x)
    dot = lambda t: jnp.dot(tri, t, preferred_element_type=F32)
    return dot(hi) + dot(mid) + dot(lo)


def _forget_cumsum(proj, b_pad, *, name):
    s = proj.shape[0]
    blk = LANES

    def body(f_ref, b_ref, c_ref):
        ri = lax.broadcasted_iota(jnp.int32, (blk, blk), 0)
        ci = lax.broadcasted_iota(jnp.int32, (blk, blk), 1)
        tri = (ci <= ri).astype(BF16)
        bias = b_ref[...]

        def step(t, carry):
            rows = pl.ds(pl.multiple_of(t * blk, blk), blk)
            z = f_ref[rows, :] + bias
            lf = jnp.minimum(z, 0.0) - jnp.log(1.0 + jnp.exp(-jnp.abs(z)))
            cs = _tri_sum(tri, lf) + carry
            c_ref[rows, :] = cs
            return cs[blk - 1:blk, :]

        lax.fori_loop(0, s // blk, step, jnp.zeros((1, blk), F32))

    return pl.pallas_call(body, out_shape=jax.ShapeDtypeStruct((s, LANES), F32), grid=(1,),
                          in_specs=[pl.BlockSpec((s, LANES), lambda i: (0, F_COL)), pl.BlockSpec((1, LANES), lambda i: (0, 0))],
                          out_specs=pl.BlockSpec((s, LANES), lambda i: (0, 0)), compiler_params=_params("arbitrary"), name=name)(proj, b_pad)


def _forget_cumsum_bwd(proj, b_pad, dcq, dck, *, name):
    s = proj.shape[0]
    blk = LANES
    nblk = s // blk

    def body(f_ref, b_ref, dcq_ref, dck_ref, dz_ref, db_ref):
        ri = lax.broadcasted_iota(jnp.int32, (blk, blk), 0)
        ci = lax.broadcasted_iota(jnp.int32, (blk, blk), 1)
        triu = (ci >= ri).astype(BF16)
        bias = b_ref[...]

        def step(t, carry):
            tail, dbs = carry
            rows = pl.ds(pl.multiple_of((nblk - 1 - t) * blk, blk), blk)
            dc = dcq_ref[rows, :] - dck_ref[rows, :]
            dlf = _tri_sum(triu, dc) + tail
            z = f_ref[rows, :] + bias
            e = jnp.exp(-jnp.abs(z))
            sig_neg = jnp.where(z >= 0.0, e, 1.0) / (1.0 + e)
            dz = dlf * sig_neg
            dz_ref[rows, :] = dz.astype(BF16)
            return dlf[0:1, :], dbs + jnp.sum(dz, axis=0, keepdims=True)

        _, dbs = lax.fori_loop(0, nblk, step, (jnp.zeros((1, blk), F32), jnp.zeros((1, blk), F32)))
        db_ref[...] = jnp.broadcast_to(dbs, db_ref.shape)

    full = pl.BlockSpec((s, LANES), lambda i: (0, 0))
    return pl.pallas_call(body, out_shape=(jax.ShapeDtypeStruct((s, LANES), BF16), jax.ShapeDtypeStruct((8, LANES), F32)), grid=(1,),
                          in_specs=[pl.BlockSpec((s, LANES), lambda i: (0, F_COL)), pl.BlockSpec((1, LANES), lambda i: (0, 0)), full, full],
                          out_specs=(full, pl.BlockSpec((8, LANES), lambda i: (0, 0))), compiler_params=_params("arbitrary"), name=name)(proj, b_pad, dcq, dck)


def _attn_fwd(q_arr, kv_arr, ck6, *, q_col, k_col, v_col, n_hp, causal, bq, bk, name):
    s = q_arr.shape[0]
    skv = kv_arr.shape[0]
    bias = ck6 is not None
    nq = s // bq
    assert not causal or bq == bk

    def body(*refs):
        q_ref, k_ref, v_ref = refs[:3]
        ck_ref = refs[3] if bias else None
        o_ref, lse_ref = refs[-2:]
        i = pl.program_id(1)
        lo = _lane_lo((bq, LANES))
        q = q_ref[...] * SCALE
        qh = (jnp.where(lo, q, 0.0).astype(BF16), jnp.where(lo, 0.0, q).astype(BF16))

        def block(j, carry, diagonal):
            ks = pl.ds(pl.multiple_of(j * bk, bk), bk)
            k = k_ref[ks, :].astype(BF16)
            v = v_ref[ks, :].astype(BF16)
            if diagonal:
                ok = lax.broadcasted_iota(jnp.int32, (bq, bk), 1) <= lax.broadcasted_iota(jnp.int32, (bq, bk), 0)
            out = []
            for h in range(2):
                m, l, acc = carry[3 * h:3 * h + 3]
                sc = lax.dot_general(qh[h], k, NT, preferred_element_type=F32)
                if bias:
                    sc = sc - ck_ref[0, h:h + 1, ks]
                if diagonal:
                    sc = jnp.where(ok, sc, NEG)
                mn = jnp.maximum(m, jnp.max(sc, axis=1, keepdims=True))
                p = jnp.exp(sc - mn)
                al = jnp.exp(m - mn)
                out += [mn, al * l + jnp.sum(p, axis=1, keepdims=True), al * acc + jnp.dot(p.astype(BF16), v, preferred_element_type=F32)]
            return tuple(out)

        col = lambda v_: jnp.full((bq, 1), v_, F32)
        init = (col(NEG), col(0.0), jnp.zeros((bq, LANES), F32)) * 2
        n_full = i if causal else skv // bk
        carry = lax.fori_loop(0, n_full, functools.partial(block, diagonal=False), init)
        if causal:
            carry = block(i, carry, True)
        m0, l0, a0, m1, l1, a1 = carry
        o_ref[...] = _pair(lo, a0 / l0, a1 / l1)
        lse_ref[0] = _pair(lo, m0 + jnp.log(l0), m1 + jnp.log(l1))

    specs = [pl.BlockSpec((bq, LANES), lambda h, i: (i, q_col + h)),
             pl.BlockSpec((skv, LANES), lambda h, i: (0, k_col + h)),
             pl.BlockSpec((skv, LANES), lambda h, i: (0, v_col + h))]
    ins = [q_arr, kv_arr, kv_arr]
    if bias:
        specs += [pl.BlockSpec((1, 8, skv), lambda h, i: (h, 0, 0))]
        ins += [ck6]
    return pl.pallas_call(
        body, out_shape=(jax.ShapeDtypeStruct((s, n_hp * LANES), F32), jax.ShapeDtypeStruct((n_hp, s, LANES), F32)),
        grid=(n_hp, nq), in_specs=specs,
        out_specs=(pl.BlockSpec((bq, LANES), lambda h, i: (i, h)), pl.BlockSpec((1, bq, LANES), lambda h, i: (h, i, 0))),
        compiler_params=_params("parallel", "parallel"), name=name)(*ins)


def _attn_bwd(q_arr, kv_arr, o_arr, do_arr, lse, ck6, *, q_col, k_col, v_col, o_col, n_hp, causal, bq, bk, name):
    s = q_arr.shape[0]
    skv = kv_arr.shape[0]
    bias = ck6 is not None
    nq = s // bq
    assert not causal or bq == bk

    def body(*refs):
        q_ref, k_ref, v_ref, o_ref, do_ref, lse_ref = refs[:6]
        if bias:
            ck_ref = refs[6]
            dq_ref, dk_ref, dv_ref, dcq_ref, dck_ref = refs[-5:]
        else:
            dq_ref, dk_ref, dv_ref = refs[-3:]
        j = pl.program_id(1)
        lo_q = _lane_lo((bq, LANES))
        lo_k = _lane_lo((bk, LANES))
        k = k_ref[...]
        v = v_ref[...].astype(BF16)
        kb = k.astype(BF16)
        kh = (jnp.where(lo_k, k, 0.0).astype(BF16), jnp.where(lo_k, 0.0, k).astype(BF16))
        if bias:
            pick_k = [(lax.broadcasted_iota(jnp.int32, (8, bk), 0) == h).astype(BF16) for h in range(2)]
            pick_q = [(lax.broadcasted_iota(jnp.int32, (8, bq), 0) == h).astype(BF16) for h in range(2)]

        @pl.when(j == 0)
        def _():
            dq_ref[...] = jnp.zeros_like(dq_ref)
            if bias:
                dcq_ref[...] = jnp.zeros_like(dcq_ref)

        def block(i, carry, diagonal):
            dk_acc, dv_acc, cs = carry
            qs = pl.ds(pl.multiple_of(i * bq, bq), bq)
            q = q_ref[qs, :] * SCALE
            do = do_ref[qs, :]
            dd = do * o_ref[qs, :]
            lse_i = lse_ref[0, qs, :]
            qh = (jnp.where(lo_q, q, 0.0).astype(BF16), jnp.where(lo_q, 0.0, q).astype(BF16))
            doh = (jnp.where(lo_q, do, 0.0).astype(BF16), jnp.where(lo_q, 0.0, do).astype(BF16))
            dh = (jnp.sum(jnp.where(lo_q, dd, 0.0), axis=1, keepdims=True), jnp.sum(jnp.where(lo_q, 0.0, dd), axis=1, keepdims=True))
            if diagonal:
                ok = lax.broadcasted_iota(jnp.int32, (bq, bk), 1) <= lax.broadcasted_iota(jnp.int32, (bq, bk), 0)
            dq_blk = None
            rs = None
            for h in range(2):
                sc = lax.dot_general(qh[h], kb, NT, preferred_element_type=F32)
                if bias:
                    sc = sc - ck_ref[0, h:h + 1, :]
                if diagonal:
                    sc = jnp.where(ok, sc, NEG)
                p = jnp.exp(sc - lse_i[:, h * HEAD_DIM:h * HEAD_DIM + 1])
                ds = p * (lax.dot_general(doh[h], v, NT, preferred_element_type=F32) - dh[h])
                dsb = ds.astype(BF16)
                dv_acc = dv_acc + lax.dot_general(p.astype(BF16), doh[h], TN, preferred_element_type=F32)
                dk_acc = dk_acc + lax.dot_general(dsb, qh[h], TN, preferred_element_type=F32)
                part = jnp.dot(dsb, kh[h], preferred_element_type=F32)
                dq_blk = part if dq_blk is None else dq_blk + part
                if bias:
                    cs = cs + jnp.dot(pick_q[h], dsb, preferred_element_type=F32)
                    row_sums = lax.dot_general(pick_k[h], dsb, NT, preferred_element_type=F32)
                    rs = row_sums if rs is None else rs + row_sums
            dq_ref[qs, :] += dq_blk * SCALE
            if bias:
                dcq_ref[0, :, qs] += rs
            return dk_acc, dv_acc, cs

        carry = (jnp.zeros((bk, LANES), F32), jnp.zeros((bk, LANES), F32), jnp.zeros((8, bk), F32))
        if causal:
            carry = block(j, carry, True)
        dk_acc, dv_acc, cs = lax.fori_loop(j + 1 if causal else 0, nq, functools.partial(block, diagonal=False), carry)
        dk_ref[...] = dk_acc.astype(BF16)
        dv_ref[...] = dv_acc.astype(BF16)
        if bias:
            dck_ref[0] = cs

    full_q = lambda c: pl.BlockSpec((s, LANES), lambda h, j: (0, c + h))
    specs = [full_q(q_col),
             pl.BlockSpec((bk, LANES), lambda h, j: (j, k_col + h)),
             pl.BlockSpec((bk, LANES), lambda h, j: (j, v_col + h)),
             full_q(0), full_q(o_col),
             pl.BlockSpec((1, s, LANES), lambda h, j: (h, 0, 0))]
    ins = [q_arr, kv_arr, kv_arr, o_arr, do_arr, lse]
    out_shape = [jax.ShapeDtypeStruct((s, n_hp * LANES), F32), jax.ShapeDtypeStruct((skv, n_hp * LANES), BF16),
                 jax.ShapeDtypeStruct((skv, n_hp * LANES), BF16)]
    out_specs = [full_q(0), pl.BlockSpec((bk, LANES), lambda h, j: (j, h)), pl.BlockSpec((bk, LANES), lambda h, j: (j, h))]
    if bias:
        specs += [pl.BlockSpec((1, 8, bk), lambda h, j: (h, 0, j))]
        ins += [ck6]
        out_shape += [jax.ShapeDtypeStruct((n_hp, 8, s), F32), jax.ShapeDtypeStruct((n_hp, 8, skv), F32)]
        out_specs += [pl.BlockSpec((1, 8, s), lambda h, j: (h, 0, 0)), pl.BlockSpec((1, 8, bk), lambda h, j: (h, 0, j))]
    return pl.pallas_call(body, out_shape=tuple(out_shape), grid=(n_hp, skv // bk), in_specs=specs, out_specs=tuple(out_specs),
                          compiler_params=_params("parallel", "arbitrary"), name=name)(*ins)


def _rope_tables(s):
    inv = 1.0 / (ROPE_THETA ** (jnp.arange(0, HEAD_DIM, 2, dtype=F32) / HEAD_DIM))
    ang = jnp.arange(s, dtype=F32)[:, None] * inv[None, :]
    cos, sin = jnp.cos(ang), jnp.sin(ang)
    return jnp.tile(cos, (1, 4)), jnp.concatenate([-sin, sin, -sin, sin], axis=1)


def _rope(x_arr, cos_t, sin_t, *, n_cols, out_dtype, br, name):
    s = x_arr.shape[0]

    def body(x_ref, c_ref, s_ref, o_ref):
        cos, sin = c_ref[...], s_ref[...]
        first = (lax.broadcasted_iota(jnp.int32, (br, LANES), 1) % HEAD_DIM) < (HEAD_DIM // 2)
        for j in range(n_cols):
            lanes = slice(j * LANES, (j + 1) * LANES)
            x = x_ref[:, lanes].astype(F32)
            swapped = jnp.where(first, pltpu.roll(x, LANES - HEAD_DIM // 2, 1), pltpu.roll(x, HEAD_DIM // 2, 1))
            o_ref[:, lanes] = (x * cos + swapped * sin).astype(o_ref.dtype)

    tab = pl.BlockSpec((br, LANES), lambda i: (i, 0))
    blk = pl.BlockSpec((br, n_cols * LANES), lambda i: (i, 0))
    return pl.pallas_call(body, out_shape=jax.ShapeDtypeStruct((s, n_cols * LANES), out_dtype), grid=(s // br,),
                          in_specs=[blk, tab, tab], out_specs=blk, compiler_params=_params("parallel"), name=name)(x_arr, cos_t, sin_t)


def _stack_heads(x):
    lo = _lane_lo(x.shape)
    return jnp.concatenate([jnp.where(lo, x, 0.0), jnp.where(lo, 0.0, x)], axis=0).astype(BF16)


def _unstack_heads(x):
    return jnp.where(_lane_lo((DIL_L, LANES)), x[:DIL_L], x[DIL_L:])


def _dil_scores(q_ref, k_ref, cur, prev, has_prev):
    qs = _stack_heads(q_ref[cur, :] * SCALE)
    kk = jnp.concatenate([k_ref[prev, :], k_ref[cur, :]], axis=0).astype(BF16)
    a = lax.broadcasted_iota(jnp.int32, (2 * DIL_L, 2 * DIL_L), 0) & (DIL_L - 1)
    c = lax.broadcasted_iota(jnp.int32, (2 * DIL_L, 2 * DIL_L), 1)
    ok = ((c < DIL_L) & (c >= a) & has_prev) | ((c >= DIL_L) & (c - DIL_L <= a))
    return qs, kk, jnp.where(ok, lax.dot_general(qs, kk, NT, preferred_element_type=F32), NEG)


def _dil_rows(t, dil):
    r, m = t % dil, t // dil
    start = m * (DIL_L * dil) + r
    prev = jnp.maximum(start - DIL_L * dil, 0)
    return pl.ds(start, DIL_L, stride=dil), pl.ds(prev, DIL_L, stride=dil), m > 0


def _softmax3(a, b, c):
    m = jnp.maximum(jnp.maximum(a, b), c)
    ea, eb, ec = jnp.exp(a - m), jnp.exp(b - m), jnp.exp(c - m)
    den = ea + eb + ec
    inv = 1.0 / den
    return ea * inv, eb * inv, ec * inv, m + jnp.log(den)


def _dil_fwd(qk_r, proj, *, name):
    s = qk_r.shape[0]
    nsub = s // DIL_L
    mb = 512

    def body(q_ref, k_ref, v_ref, mix_ref, l1_ref, l2_ref, l3_ref, o1_scr, o2_scr, o3_scr):
        lo = _lane_lo((DIL_L, LANES))
        for (_, dil), o_scr, l_ref in zip(DILATED_BRANCHES, (o1_scr, o2_scr, o3_scr), (l1_ref, l2_ref, l3_ref)):
            def step(t, carry, dil=dil, o_scr=o_scr, l_ref=l_ref):
                cur, prev, has_prev = _dil_rows(t, dil)
                _, _, sc = _dil_scores(q_ref, k_ref, cur, prev, has_prev)
                vv = jnp.concatenate([v_ref[prev, :], v_ref[cur, :]], axis=0).astype(BF16)
                m = jnp.max(sc, axis=1, keepdims=True)
                e = jnp.exp(sc - m)
                den = jnp.sum(e, axis=1, keepdims=True)
                o = jnp.dot((e * (1.0 / den)).astype(BF16), vv, preferred_element_type=F32)
                o_scr[cur, :] = _unstack_heads(o)
                l_ref[cur, :] = _unstack_heads(jnp.broadcast_to(m + jnp.log(den), (2 * DIL_L, LANES)))
                return carry

            lax.fori_loop(0, nsub, step, 0, unroll=DIL_UNROLL_FWD)

        def merge(i, carry):
            rows = pl.ds(pl.multiple_of(i * mb, mb), mb)
            wa, wb, wc, _ = _softmax3(l1_ref[rows, :], l2_ref[rows, :], l3_ref[rows, :])
            mix_ref[rows, :] = wa * o1_scr[rows, :] + wb * o2_scr[rows, :] + wc * o3_scr[rows, :]
            return carry

        lax.fori_loop(0, s // mb, merge, 0)

    col = lambda arr_col: pl.BlockSpec((s, LANES), lambda h: (0, arr_col + h))
    shp = jax.ShapeDtypeStruct((s, D_MIX), F32)
    mix, l1, l2, l3 = pl.pallas_call(
        body, out_shape=(shp, shp, shp, shp), grid=(N_MIX_HP,), in_specs=[col(0), col(N_MIX_HP), col(2 * N_MIX_HP)],
        out_specs=(col(0),) * 4, scratch_shapes=[pltpu.VMEM((s, LANES), F32)] * 3,
        compiler_params=_params("parallel"), name=name)(qk_r, qk_r, proj)
    return mix, (l1, l2, l3)


def _dil_bwd(qk_r, proj, mix, dheads, lses, *, name):
    s = qk_r.shape[0]
    nsub = s // DIL_L
    mb = 512

    def body(q_ref, k_ref, v_ref, mix_ref, dm_ref, l1_ref, l2_ref, l3_ref, dq_ref, dk_ref, dv_ref, lt_scr, dd_scr):
        lo = _lane_lo((DIL_L, LANES))
        lo_m = _lane_lo((mb, LANES))

        def prep(i, carry):
            rows = pl.ds(pl.multiple_of(i * mb, mb), mb)
            _, _, _, lt = _softmax3(l1_ref[rows, :], l2_ref[rows, :], l3_ref[rows, :])
            lt_scr[rows, :] = lt
            dd = dm_ref[rows, :] * mix_ref[rows, :]
            dd_scr[rows, :] = _pair(lo_m, jnp.sum(jnp.where(lo_m, dd, 0.0), axis=1, keepdims=True),
                                    jnp.sum(jnp.where(lo_m, 0.0, dd), axis=1, keepdims=True))
            zero = jnp.zeros((mb, LANES), F32)
            dq_ref[rows, :] = zero
            dk_ref[rows, :] = zero
            dv_ref[rows, :] = zero
            return carry

        lax.fori_loop(0, s // mb, prep, 0)

        for (_, dil), l_ref in zip(DILATED_BRANCHES, (l1_ref, l2_ref, l3_ref)):
            def step(t, carry, dil=dil, l_ref=l_ref):
                cur, prev, has_prev = _dil_rows(t, dil)
                qs, kk, sc = _dil_scores(q_ref, k_ref, cur, prev, has_prev)
                vv = jnp.concatenate([v_ref[prev, :], v_ref[cur, :]], axis=0).astype(BF16)
                lg = l_ref[cur, :]
                w = jnp.exp(lg - lt_scr[cur, :])
                wd = w * dd_scr[cur, :]
                column = lambda x: jnp.concatenate([x[:, 0:1], x[:, HEAD_DIM:HEAD_DIM + 1]], axis=0)
                dos = _stack_heads(w * dm_ref[cur, :])
                p = jnp.exp(sc - column(lg))
                ds = (p * (lax.dot_general(dos, vv, NT, preferred_element_type=F32) - column(wd))).astype(BF16)
                dq_ref[cur, :] += _unstack_heads(jnp.dot(ds, kk, preferred_element_type=F32)) * SCALE
                dkk = lax.dot_general(ds, qs, TN, preferred_element_type=F32)
                dvv = lax.dot_general(p.astype(BF16), dos, TN, preferred_element_type=F32)
                dk_ref[cur, :] += dkk[DIL_L:]
                dv_ref[cur, :] += dvv[DIL_L:]
                dk_ref[prev, :] += dkk[:DIL_L]
                dv_ref[prev, :] += dvv[:DIL_L]
                return carry

            lax.fori_loop(0, nsub, step, 0, unroll=DIL_UNROLL_BWD)

    col = lambda arr_col: pl.BlockSpec((s, LANES), lambda h: (0, arr_col + h))
    shp = jax.ShapeDtypeStruct((s, D_MIX), F32)
    return pl.pallas_call(
        body, out_shape=(shp, shp, shp), grid=(N_MIX_HP,),
        in_specs=[col(0), col(N_MIX_HP), col(2 * N_MIX_HP), col(0), col(0), col(0), col(0), col(0)], out_specs=(col(0),) * 3,
        scratch_shapes=[pltpu.VMEM((s, LANES), F32)] * 2,
        compiler_params=_params("parallel"), name=name)(qk_r, qk_r, proj, mix, dheads, *lses)


CONV_BR = 512
FF_CHUNK = 2 * D_FF // N_DEV
FF_HALF = N_DEV // 2
HALO = 8


def _shift_down(x, halo, k):
    row = lax.broadcasted_iota(jnp.int32, x.shape, 0)
    y = pltpu.roll(x, k, 0)
    for r in range(k):
        y = jnp.where(row == r, halo[HALO - k + r:HALO - k + r + 1, :], y)
    return y


def _shift_up(x, halo, k):
    n = x.shape[0]
    row = lax.broadcasted_iota(jnp.int32, x.shape, 0)
    y = pltpu.roll(x, n - k, 0)
    for r in range(k):
        y = jnp.where(row == n - k + r, halo[r:r + 1, :], y)
    return y


def _conv_vals(u, halo, w, b):
    s1 = _shift_down(u, halo, 1)
    s2 = _shift_down(u, halo, 2)
    return b + w[0:1, :] * s2 + w[1:2, :] * s1 + w[2:3, :] * u, s1, s2


def _conv_in_specs(order, layer):
    rc = (lambda i, j: (i, j)) if order == "rc" else (lambda j, i: (i, j))
    per = CONV_BR // HALO
    main = lambda off: pl.BlockSpec((None, CONV_BR, FF_CHUNK), lambda *g: (off + rc(*g)[1], rc(*g)[0], 0))
    halo = lambda off: pl.BlockSpec((None, HALO, FF_CHUNK), lambda *g: (off + rc(*g)[1], jnp.maximum(rc(*g)[0] * per - 1, 0), 0))
    wspec = lambda off: pl.BlockSpec((None, None, 3, FF_CHUNK), lambda *g: (layer, off + rc(*g)[1], 0, 0))
    bspec = lambda off: pl.BlockSpec((None, 1, FF_CHUNK), lambda *g: (off + rc(*g)[1], 0, 0))
    return [main(0), halo(0), main(FF_HALF), halo(FF_HALF), wspec(0), wspec(FF_HALF), bspec(0), bspec(FF_HALF)]


def _conv_fwd(u, cw, cb, layer, *, name):
    s = u.shape[1]

    def body(uv_ref, hv_ref, ug_ref, hg_ref, wv_ref, wg_ref, bv_ref, bg_ref, o_ref):
        first = pl.program_id(0) == 0
        hv = jnp.where(first, 0.0, hv_ref[...])
        hg = jnp.where(first, 0.0, hg_ref[...])
        val, _, _ = _conv_vals(uv_ref[...], hv, wv_ref[...], bv_ref[...])
        gate, _, _ = _conv_vals(ug_ref[...], hg, wg_ref[...], bg_ref[...])
        o_ref[...] = (gate / (1.0 + jnp.exp(-gate)) * val).astype(BF16)

    return pl.pallas_call(body, out_shape=jax.ShapeDtypeStruct((FF_HALF, s, FF_CHUNK), BF16), grid=(s // CONV_BR, FF_HALF),
                          in_specs=_conv_in_specs("rc", layer), out_specs=pl.BlockSpec((None, CONV_BR, FF_CHUNK), lambda i, j: (j, i, 0)),
                          compiler_params=_params("parallel", "parallel"), name=name)(u, u, u, u, cw, cw, cb, cb)


def _swiglu_bwd(val, gate, da):
    sg = 1.0 / (1.0 + jnp.exp(-gate))
    return da * (gate * sg), da * val * (sg * (1.0 + gate * (1.0 - sg)))


def _conv_bwd(u, cw, cb, da, layer, *, name):
    s = u.shape[1]
    nrow = s // CONV_BR
    per = CONV_BR // HALO

    def body(uv_ref, hv_ref, ug_ref, hg_ref, wv_ref, wg_ref, bv_ref, bg_ref, da_ref, nv_ref, ng_ref, nda_ref, du_ref, dwb_ref):
        i = pl.program_id(1)
        first, last = i == 0, i == nrow - 1
        hv = jnp.where(first, 0.0, hv_ref[...])
        hg = jnp.where(first, 0.0, hg_ref[...])
        uv, ug = uv_ref[...], ug_ref[...]
        wv, wg, bv, bg = wv_ref[...], wg_ref[...], bv_ref[...], bg_ref[...]
        val, v1, v2 = _conv_vals(uv, hv, wv, bv)
        gate, g1, g2 = _conv_vals(ug, hg, wg, bg)
        dval, dgate = _swiglu_bwd(val, gate, da_ref[...])
        val_n, _, _ = _conv_vals(nv_ref[...], uv[CONV_BR - HALO:, :], wv, bv)
        gate_n, _, _ = _conv_vals(ng_ref[...], ug[CONV_BR - HALO:, :], wg, bg)
        dval_n, dgate_n = _swiglu_bwd(val_n, gate_n, nda_ref[...])
        dval_n = jnp.where(last, 0.0, dval_n)
        dgate_n = jnp.where(last, 0.0, dgate_n)
        back = lambda dc, dc_n, w: w[2:3, :] * dc + w[1:2, :] * _shift_up(dc, dc_n, 1) + w[0:1, :] * _shift_up(dc, dc_n, 2)
        du_ref[0] = back(dval, dval_n, wv).astype(BF16)
        du_ref[1] = back(dgate, dgate_n, wg).astype(BF16)

        @pl.when(first)
        def _():
            dwb_ref[...] = jnp.zeros_like(dwb_ref)

        cs = lambda t: jnp.sum(t, axis=0, keepdims=True)
        r8 = lax.broadcasted_iota(jnp.int32, (8, FF_CHUNK), 0)
        rows4 = lambda a, b, c, d: jnp.where(r8 == 0, a, jnp.where(r8 == 1, b, jnp.where(r8 == 2, c, jnp.where(r8 == 3, d, 0.0))))
        dwb_ref[0] += rows4(cs(dval * v2), cs(dval * v1), cs(dval * uv), cs(dval))
        dwb_ref[1] += rows4(cs(dgate * g2), cs(dgate * g1), cs(dgate * ug), cs(dgate))

    nxt = lambda off: pl.BlockSpec((None, HALO, FF_CHUNK), lambda j, i: (off + j, jnp.minimum((i + 1) * per, nrow * per - 1), 0))
    specs = _conv_in_specs("cr", layer) + [pl.BlockSpec((None, CONV_BR, FF_CHUNK), lambda j, i: (j, i, 0)), nxt(0), nxt(FF_HALF), nxt(0)]
    return pl.pallas_call(
        body, out_shape=(jax.ShapeDtypeStruct((2, FF_HALF, s, FF_CHUNK), BF16), jax.ShapeDtypeStruct((2, FF_HALF, 8, FF_CHUNK), F32)),
        grid=(FF_HALF, nrow), in_specs=specs,
        out_specs=(pl.BlockSpec((2, None, CONV_BR, FF_CHUNK), lambda j, i: (0, j, i, 0)),
                   pl.BlockSpec((2, None, 8, FF_CHUNK), lambda j, i: (0, j, 0, 0))),
        compiler_params=_params("parallel", "arbitrary"), name=name)(u, u, u, u, cw, cw, cb, cb, da, u, u, da)


def _rows_of(r):
    return lambda ref, idx: ref.at[:, pl.ds(idx * r, r), :]


def _slot1(ref, idx):
    return ref.at[:, idx]


def _slot0(ref, idx):
    return ref.at[idx]


def _all_gather(shards, full_shapes, places, *, name):
    n = len(shards)

    def body(*refs):
        ins, outs = refs[:n], refs[n:2 * n]
        send_sems, recv_sems, local_sems = refs[2 * n:]
        mx, my, mc = lax.axis_index("x"), lax.axis_index("y"), lax.axis_index("c")
        me, sibling = (mx, my, mc), (mx, my, 1 - mc)
        chips = [(1 - mx, my), (mx, 1 - my), (1 - mx, 1 - my)]

        def win(t, px, py, pc):
            return places[t](outs[t], 4 * px + 2 * py + pc)

        def copy(t, k, block, to, src=None):
            return pltpu.make_async_remote_copy(src_ref=win(t, *block) if src is None else src, dst_ref=win(t, *block),
                                                send_sem=send_sems.at[t, k], recv_sem=recv_sems.at[t, k], device_id=to, device_id_type=MESH)

        mine = [pltpu.make_async_copy(ins[t], win(t, *me), local_sems.at[t]) for t in range(n)]
        for cp in mine:
            cp.start()
        first = []
        for t in range(n):
            first += [copy(t, 0, me, sibling, src=ins[t])] + [copy(t, 1 + j, me, (*chip, mc), src=ins[t]) for j, chip in enumerate(chips)]
        for cp in first:
            cp.start()
        passed = []
        for j, chip in enumerate(chips):
            for t in range(n):
                copy(t, 1 + j, (*chip, mc), me).wait_recv()
                fwd = copy(t, 4 + j, (*chip, mc), sibling)
                fwd.start()
                passed.append(fwd)
        for t in range(n):
            copy(t, 0, sibling, me).wait_recv()
            for j, chip in enumerate(chips):
                copy(t, 4 + j, (*chip, 1 - mc), me).wait_recv()
        for cp in first + passed:
            cp.wait_send()
        for cp in mine:
            cp.wait()

    hbm = pl.BlockSpec(memory_space=pl.ANY)
    return pl.pallas_call(
        body, out_shape=tuple(jax.ShapeDtypeStruct(s, x.dtype) for s, x in zip(full_shapes, shards)),
        in_specs=[hbm] * n, out_specs=(hbm,) * n,
        scratch_shapes=[pltpu.SemaphoreType.DMA((n, 7)), pltpu.SemaphoreType.DMA((n, 7)), pltpu.SemaphoreType.DMA((n,))],
        name=name)(*shards)


FLIPS = [(fx, fy, fc) for fx in (0, 1) for fy in (0, 1) for fc in (0, 1)][1:]


def _exchange_copies(kind, places, src, land, send_sems, recv_sems, local_sems):
    mx, my, mc = lax.axis_index("x"), lax.axis_index("y"), lax.axis_index("c")
    me = 4 * mx + 2 * my + mc
    n = len(src)
    local, remote = [], []
    for t in range(n):
        if kind == "gather":
            local.append(pltpu.make_async_copy(src[t], places[t](land[t], me), local_sems.at[t]))
        else:
            local.append(pltpu.make_async_copy(places[t](src[t], me), land[t].at[me], local_sems.at[t]))
    for k, (fx, fy, fc) in enumerate(FLIPS):
        px, py, pc = mx ^ fx, my ^ fy, mc ^ fc
        peer = 4 * px + 2 * py + pc
        for t in range(n):
            sems = dict(send_sem=send_sems.at[7 * t + k], recv_sem=recv_sems.at[7 * t + k], device_id=(px, py, pc), device_id_type=MESH)
            if kind == "gather":
                pair = [(src[t], places[t](land[t], me)), (src[t], places[t](land[t], peer))]
            else:
                pair = [(places[t](src[t], peer), land[t].at[me]), (places[t](src[t], peer), land[t].at[peer])]
            remote.append([functools.partial(pltpu.make_async_remote_copy, src_ref=s_, dst_ref=d_, **sems) for s_, d_ in pair])
    return local, remote


HBM_SPEC = pl.BlockSpec(memory_space=pltpu.HBM)
SEM_SPEC = pl.BlockSpec(memory_space=pltpu.SEMAPHORE)
SIDE_EFFECT = pltpu.SideEffectType.DATAFLOW_SIDE_EFFECTING


def _exchange_start(kind, srcs, land_shapes, places, after, *, name):
    n = len(srcs)

    def body(*refs):
        src, land = refs[:n], refs[n:2 * n]
        send_sems, recv_sems, local_sems = refs[2 * n + 1:2 * n + 4]
        token = refs[-1]
        local, remote = _exchange_copies(kind, places, src, land, send_sems, recv_sems, local_sems)
        for cp in local:
            cp.start()
        for send, _ in remote:
            send().start()
        token[...] = jnp.zeros_like(token)

    hbm = lambda t: pltpu.with_memory_space_constraint(t, pltpu.HBM)
    lands = [hbm(lax.empty(tuple(s), x.dtype)) for s, x in zip(land_shapes, srcs)]
    out_shape = (pltpu.SemaphoreType.DMA((7 * n,)), pltpu.SemaphoreType.DMA((7 * n,)), pltpu.SemaphoreType.DMA((n,)),
                 *[pltpu.HBM(x.shape, x.dtype) for x in srcs], *[pltpu.HBM(tuple(s), x.dtype) for s, x in zip(land_shapes, srcs)],
                 jax.ShapeDtypeStruct((8, LANES), F32))
    outs = pl.pallas_call(
        body, name=name, out_shape=out_shape, in_specs=[HBM_SPEC] * (2 * n) + [pl.BlockSpec(memory_space=pl.ANY)],
        out_specs=(SEM_SPEC, SEM_SPEC, SEM_SPEC) + (HBM_SPEC,) * (2 * n) + (pl.BlockSpec(memory_space=pltpu.VMEM),),
        input_output_aliases={i: 3 + i for i in range(2 * n)},
        compiler_params=pltpu.CompilerParams(has_side_effects=SIDE_EFFECT))(*[hbm(x) for x in srcs], *lands, after)
    return dict(sems=outs[:3], src=outs[3:3 + n], land=outs[3 + n:3 + 2 * n], token=outs[-1])


def _exchange_wait(kind, started, places, after, *, name):
    n = len(started["src"])

    def body(*refs):
        src, land = refs[:n], refs[n:2 * n]
        send_sems, recv_sems, local_sems = refs[2 * n:2 * n + 3]
        local, remote = _exchange_copies(kind, places, src, land, send_sems, recv_sems, local_sems)
        for cp in local:
            cp.wait()
        for send, arrival in remote:
            send().wait_send()
            arrival().wait_recv()

    out_shape = tuple(pltpu.HBM(x.shape, x.dtype) for x in started["src"]) + tuple(pltpu.HBM(x.shape, x.dtype) for x in started["land"])
    outs = pl.pallas_call(
        body, name=name, out_shape=out_shape,
        in_specs=[HBM_SPEC] * (2 * n) + [SEM_SPEC] * 3 + [pl.BlockSpec(memory_space=pl.ANY)], out_specs=(HBM_SPEC,) * (2 * n),
        input_output_aliases={i: i for i in range(2 * n)},
        compiler_params=pltpu.CompilerParams(has_side_effects=SIDE_EFFECT))(*started["src"], *started["land"], *started["sems"], after)
    return list(outs[n:])


def _adamw(parts, w, m, v, *, br, name):
    layers, r, wd = w.shape
    assert len(parts) == layers

    def body(*refs):
        p_refs = refs[:layers]
        w_ref, m_ref, v_ref, g_ref, d_ref, nm_ref, nv_ref = refs[layers:]
        for k in range(layers):
            @pl.when(pl.program_id(0) == k)
            def _(p_ref=p_refs[k]):
                g = p_ref[0].astype(F32)
                for dev in range(1, N_DEV):
                    g = g + p_ref[dev].astype(F32)
                mm = ADAM_B1 * m_ref[...] + (1.0 - ADAM_B1) * g
                vv = ADAM_B2 * v_ref[...] + (1.0 - ADAM_B2) * (g * g)
                m_hat = mm / (1.0 - ADAM_B1 ** ADAM_STEP)
                v_hat = vv / (1.0 - ADAM_B2 ** ADAM_STEP)
                g_ref[...] = g
                d_ref[...] = -ADAM_LR * (m_hat / (jnp.sqrt(v_hat) + ADAM_EPS) + ADAM_WD * w_ref[...])
                nm_ref[...] = mm
                nv_ref[...] = vv

    p_spec = lambda k: pl.BlockSpec((N_DEV, None, br, wd), lambda l, i: (0, 0, jnp.where(l == k, i, 0), 0))
    blk = pl.BlockSpec((None, br, wd), lambda l, i: (l, i, 0))
    shp = jax.ShapeDtypeStruct((layers, r, wd), F32)
    return pl.pallas_call(body, out_shape=(shp, shp, shp, shp), grid=(layers, r // br),
                          in_specs=[p_spec(k) for k in range(layers)] + [blk, blk, blk], out_specs=(blk, blk, blk, blk),
                          compiler_params=_params("arbitrary", "arbitrary"), name=name)(*parts, w, m, v)


SMALL = ("norm_mix", "norm_mem", "norm_ffn", "b_forget", "conv_b", "norm_final")


def _pack(tensors):
    flat = jnp.concatenate([t.reshape(-1) for t in tensors])
    rows = -(-flat.shape[0] // (PACK_W * PACK_ROW_ALIGN)) * PACK_ROW_ALIGN
    flat = jnp.pad(flat, (0, rows * PACK_W - flat.shape[0]))
    return flat.reshape(1, rows, PACK_W)


def _unpack(buf, shapes):
    flat = buf.reshape(-1)
    out, off = [], 0
    for shp in shapes:
        n = math.prod(shp)
        out.append(flat[off:off + n].reshape(tuple(shp)))
        off += n
    return out


def _fox_permute(w):
    pad = jnp.zeros(w.shape[:-1] + (FOX_P - FOX_IN,), w.dtype)
    return jnp.concatenate([w[..., :3 * D_MIX], w[..., 3 * D_MIX + N_MIX_HEADS:], w[..., 3 * D_MIX:3 * D_MIX + N_MIX_HEADS], pad], axis=-1)


def _fox_unpermute(w):
    return jnp.concatenate([w[..., :3 * D_MIX], w[..., DIL_IN:DIL_IN + N_MIX_HEADS], w[..., 3 * D_MIX:DIL_IN]], axis=-1)


def _bias_layout(c):
    s = c.shape[0]
    ct = c[:, :N_MIX_HEADS].T.reshape(N_MIX_HP, 2, s)
    return jnp.pad(ct, ((0, 0), (0, 6), (0, 0)))


def _bias_grad(dck6):
    s = dck6.shape[2]
    dk = dck6[:, :2, :].reshape(N_MIX_HEADS, s).T
    return jnp.pad(dk, ((0, 0), (0, LANES - N_MIX_HEADS)))


def _device_step(x, mem, target, small, get_weights, put_grads):
    s = x.shape[0]
    mt = mem.shape[0]
    bq = 512
    cos_t, sin_t = _rope_tables(s)
    row = lambda t, l: t[l][None, :]
    saved = []
    h = x
    cb8 = small["conv_b"].reshape(DEPTH, N_DEV, 1, FF_CHUNK)
    for l in range(DEPTH):
        kind, slot = l % 2, l // 2
        wl = dict(get_weights(l, "attn", h))
        xn = _rmsnorm_fwd(h, row(small["norm_mix"], l), br=512, name=f"norm_mix_fwd{l}")
        mn = _rmsnorm_fwd(mem, row(small["norm_mem"], l), br=mt, name=f"norm_mem_fwd{l}")
        proj = _mm(xn, wl["w_in"], "nn" if kind == 0 else "nt", tm=1024, tn=384 if kind == 0 else 512, layer=0, name=f"in_proj{l}")
        kvm = _mm(mn, wl["w_mem_kv"], "nn", tm=mt, tn=512, layer=0, name=f"mem_kv{l}")
        st = dict(h=h, xn=xn, mn=mn, proj=proj, kvm=kvm, w=wl)
        if kind == 0:
            b_pad = jnp.pad(small["b_forget"][slot], (0, LANES - N_MIX_HEADS))[None, :]
            c = _forget_cumsum(proj, b_pad, name=f"forget_cumsum{l}")
            ck6 = _bias_layout(c)
            mix, lse = _attn_fwd(proj, proj, ck6, q_col=0, k_col=N_MIX_HP, v_col=2 * N_MIX_HP, n_hp=N_MIX_HP,
                                 causal=True, bq=min(s, 1024), bk=min(s, 1024), name=f"fox_fwd{l}")
            st.update(b_pad=b_pad, ck6=ck6, mix=mix, lse=lse)
        else:
            qk_r = _rope(proj, cos_t, sin_t, n_cols=2 * N_MIX_HP, out_dtype=F32, br=512, name=f"rope_fwd{l}")
            mix, lses = _dil_fwd(qk_r, proj, name=f"dil_fwd{l}")
            st.update(qk_r=qk_r, lses=lses, mix=mix)
        mo, lse_m = _attn_fwd(proj, kvm, None, q_col=QM_COL, k_col=0, v_col=N_MEM_HP, n_hp=N_MEM_HP,
                              causal=False, bq=bq, bk=mt, name=f"mem_fwd{l}")
        heads = jnp.concatenate([mix.astype(BF16), mo.astype(BF16)], axis=1)
        h1 = _mm(heads, wl["w_out"], "nn", tm=1024, tn=512, res=h, layer=0, name=f"out_proj{l}")
        xf = _rmsnorm_fwd(h1, row(small["norm_ffn"], l), br=512, name=f"norm_ffn_fwd{l}")
        wl.update(get_weights(l, "ffn", xf))
        u = _mm(xf, wl["w_up"], "nt", tm=1024, tn=FF_CHUNK, layer=0, chunk="b", name=f"up_proj{l}")
        a = _conv_fwd(u, wl["conv_w"], cb8[l], 0, name=f"conv_fwd{l}")
        h = _mm(a, wl["w_down"], "nn", tm=1024, tn=512, res=h1, layer=0, chunk="reduce", name=f"down_proj{l}")
        st.update(mo=mo, lse_m=lse_m, heads=heads, h1=h1, xf=xf, u=u, a=a)
        saved.append(st)

    dh, dhb, dg_final, loss = _loss_head(h, target, small["norm_final"][None, :], br=512, name="loss_head")
    gs = {k: [None] * DEPTH for k in ("norm_mix", "norm_mem", "norm_ffn", "conv_b")}
    gs["b_forget"] = [None] * 2
    dep = 0.0
    for l in reversed(range(DEPTH)):
        st = saved[l]
        wl = st["w"]
        gw = {}
        kind, slot = l % 2, l // 2
        da = _mm(dhb, wl["w_down"], "nt", tm=1024, tn=FF_CHUNK, layer=0, chunk="b", name=f"down_dx{l}")
        gw["w_down"] = _mm(st["a"], dhb, "tn", tm=FF_CHUNK, tn=512, out_dtype=BF16, chunk="a", name=f"down_dw{l}")
        du, dwb = _conv_bwd(st["u"], wl["conv_w"], cb8[l] + dep, da, 0, name=f"conv_bwd{l}")
        du = du.reshape(N_DEV, s, FF_CHUNK)
        dwb = dwb.reshape(N_DEV, 8, FF_CHUNK)
        gw["conv_w"] = dwb
        gs["conv_b"][l] = dwb[:, 3, :].reshape(-1)
        dxf = _mm(du, wl["w_up"], "nn", tm=512, tn=512, layer=0, chunk="reduce", name=f"up_dx{l}")
        gw["w_up"] = _mm(du, st["xf"], "tn", tm=FF_CHUNK, tn=512, out_dtype=BF16, chunk="a", name=f"up_dw{l}")
        dep_ffn = put_grads(l, "ffn", gw)
        dh1, dh1b, dgf = _rmsnorm_bwd(st["h1"], dxf, row(small["norm_ffn"], l) + dep_ffn, dh, br=512, name=f"norm_ffn_bwd{l}")
        gs["norm_ffn"][l] = dgf[0]
        dheads = _mm(dh1b, wl["w_out"], "nt", tm=1024, tn=512, layer=0, name=f"out_dx{l}")
        gw["w_out"] = _mm(st["heads"], dh1b, "tn", tm=512, tn=512, out_dtype=BF16, name=f"out_dw{l}")
        dqm, dkm, dvm = _attn_bwd(st["proj"], st["kvm"], st["mo"], dheads, st["lse_m"], None, q_col=QM_COL, k_col=0,
                                  v_col=N_MEM_HP, o_col=N_MIX_HP, n_hp=N_MEM_HP, causal=False, bq=bq, bk=mt, name=f"mem_bwd{l}")
        dkvm = jnp.concatenate([dkm, dvm], axis=1)
        gw["w_mem_kv"] = _mm(st["mn"], dkvm, "tn", tm=512, tn=512, out_dtype=BF16, name=f"mem_kv_dw{l}")
        dmn = _mm(dkvm, wl["w_mem_kv"], "nt", tm=mt, tn=512, layer=0, name=f"mem_kv_dx{l}")
        _, _, dgm = _rmsnorm_bwd(mem, dmn, row(small["norm_mem"], l), None, br=mt, name=f"norm_mem_bwd{l}")
        gs["norm_mem"][l] = dgm[0]
        if kind == 0:
            dq, dk, dv, dcq6, dck6 = _attn_bwd(st["proj"], st["proj"], st["mix"], dheads, st["lse"], st["ck6"], q_col=0,
                                               k_col=N_MIX_HP, v_col=2 * N_MIX_HP, o_col=0, n_hp=N_MIX_HP, causal=True,
                                               bq=bq, bk=bq, name=f"fox_bwd{l}")
            dz, db = _forget_cumsum_bwd(st["proj"], st["b_pad"], _bias_grad(dcq6), _bias_grad(dck6), name=f"forget_cumsum_bwd{l}")
            gs["b_forget"][slot] = db[0, :N_MIX_HEADS]
            dproj = jnp.concatenate([dq.astype(BF16), dk, dv, dqm.astype(BF16), dz], axis=1)
        else:
            dq_r, dk_r, dv = _dil_bwd(st["qk_r"], st["proj"], st["mix"], dheads, st["lses"], name=f"dil_bwd{l}")
            dq = _rope(dq_r, cos_t, -sin_t, n_cols=N_MIX_HP, out_dtype=BF16, br=512, name=f"rope_bwd_q{l}")
            dk = _rope(dk_r, cos_t, -sin_t, n_cols=N_MIX_HP, out_dtype=BF16, br=512, name=f"rope_bwd_k{l}")
            dproj = jnp.concatenate([dq, dk, dv.astype(BF16), dqm.astype(BF16)], axis=1)
        dxn = _mm(dproj, wl["w_in"], "nt" if kind == 0 else "nn", tm=1024, tn=512, layer=0, name=f"in_dx{l}")
        if kind == 0:
            gw["w_in"] = _mm(st["xn"], dproj, "tn", tm=512, tn=384, out_dtype=BF16, name=f"in_dw{l}")
        else:
            gw["w_in"] = _mm(dproj, st["xn"], "tn", tm=512, tn=512, out_dtype=BF16, name=f"in_dw{l}")
        dh, dhb, dgx = _rmsnorm_bwd(st["h"], dxn, row(small["norm_mix"], l), dh1, br=512, name=f"norm_mix_bwd{l}")
        gs["norm_mix"][l] = dgx[0]
        dep = put_grads(l, "attn", gw)

    grads_s = {k: jnp.stack(v) for k, v in gs.items()}
    grads_s["norm_final"] = dg_final[0]
    return loss[0, 0], dh, grads_s


def kernel(x, mem, norm_mix, norm_mem, norm_ffn, w_in_fox, b_forget, w_in_dil, w_mem_kv, w_out, w_up, conv_w, conv_b, w_down, norm_final, loss_target, m_norm_mix, m_norm_mem, m_norm_ffn, m_w_in_fox, m_b_forget, m_w_in_dil, m_w_mem_kv, m_w_out, m_w_up, m_conv_w, m_conv_b, m_w_down, m_norm_final, v_norm_mix, v_norm_mem, v_norm_ffn, v_w_in_fox, v_b_forget, v_w_in_dil, v_w_mem_kv, v_w_out, v_w_up, v_conv_w, v_conv_b, v_w_down, v_norm_final):
    names = ["norm_mix", "norm_mem", "norm_ffn", "w_in_fox", "b_forget", "w_in_dil", "w_mem_kv", "w_out", "w_up", "conv_w", "conv_b",
             "w_down", "norm_final"]
    w = dict(zip(names, (norm_mix, norm_mem, norm_ffn, w_in_fox, b_forget, w_in_dil, w_mem_kv, w_out, w_up, conv_w, conv_b, w_down, norm_final)))
    m = dict(zip(names, (m_norm_mix, m_norm_mem, m_norm_ffn, m_w_in_fox, m_b_forget, m_w_in_dil, m_w_mem_kv, m_w_out, m_w_up, m_conv_w,
                         m_conv_b, m_w_down, m_norm_final)))
    v = dict(zip(names, (v_norm_mix, v_norm_mem, v_norm_ffn, v_w_in_fox, v_b_forget, v_w_in_dil, v_w_mem_kv, v_w_out, v_w_up, v_conv_w,
                         v_conv_b, v_w_down, v_norm_final)))
    big = ("w_in_fox", "w_in_dil", "w_mem_kv", "w_out", "w_up", "w_down", "conv_w")
    small_shapes = [w[k].shape for k in SMALL]
    dil_c = w_in_dil.shape[2]
    rows = {k: w[k].shape[1] for k in ("w_in_fox", "w_mem_kv", "w_out", "w_down")}

    def places(l):
        w_in_place = _rows_of(rows["w_in_fox"]) if l % 2 == 0 else _slot1
        return [w_in_place, _rows_of(rows["w_mem_kv"]), _rows_of(rows["w_out"]), _slot1, _rows_of(rows["w_down"]), _slot1]

    def full_shapes(l):
        w_in_shape = (1, D_MODEL, FOX_P) if l % 2 == 0 else (1, N_DEV, dil_c, D_MODEL)
        return [w_in_shape, (1, D_MODEL, 2 * D_MEMQ), (1, D_MODEL, D_MODEL), (1, N_DEV, FF_CHUNK, D_MODEL), (1, D_FF, D_MODEL),
                (1, N_DEV, 3, FF_CHUNK)]

    transposed = lambda t: jnp.swapaxes(t, 1, 2)
    cast = {"w_in_fox": _fox_permute(w_in_fox).astype(BF16), "w_in_dil": transposed(w_in_dil).astype(BF16),
            "w_mem_kv": w_mem_kv.astype(BF16), "w_out": w_out.astype(BF16), "w_up": transposed(w_up).astype(BF16),
            "w_down": w_down.astype(BF16), "conv_w": conv_w}
    part_of = {"attn": (0, 1, 2), "ffn": (3, 4, 5)}
    groups = {l: (("attn",), ("ffn",)) if l == 0 else (("attn", "ffn"),) for l in range(DEPTH)}
    members = lambda group: [i for p in group for i in part_of[p]]
    pick = lambda seq, group: [seq[i] for i in members(group)]
    tag = lambda l, group: f"{l}" + ("" if len(group) == 2 else group[0])

    gathers, after = {}, norm_final
    for l in range(DEPTH):
        w_in_shard = cast["w_in_fox" if l % 2 == 0 else "w_in_dil"][l // 2][None]
        shards = [w_in_shard] + [cast[k][l][None] for k in ("w_mem_kv", "w_out", "w_up", "w_down", "conv_w")]
        for group in groups[l]:
            gathers[l, group] = _exchange_start("gather", pick(shards, group), pick(full_shapes(l), group), pick(places(l), group), after,
                                                name=f"weights_gather_start{tag(l, group)}")
            after = gathers[l, group]["token"]
    started = sum(g["token"][0, 0] for g in gathers.values())
    small = {k: w[k] for k in SMALL}
    small["norm_mix"] = norm_mix + started
    landed = {}

    def get_weights(l, part, h):
        group = [g for g in groups[l] if part in g][0]
        if (l, group) not in landed:
            lands = _exchange_wait("gather", gathers[l, group], pick(places(l), group), h, name=f"weights_gather_wait{tag(l, group)}")
            landed[l, group] = dict(zip(members(group), lands))
        got = landed[l, group]
        if part == "ffn":
            return dict(w_up=got[3], w_down=got[4].reshape(1, FF_HALF, FF_CHUNK, D_MODEL), conv_w=got[5])
        w_in = got[0] if l % 2 == 0 else got[0].reshape(1, N_DEV * dil_c, D_MODEL)
        return dict(w_in=w_in, w_mem_kv=got[1], w_out=got[2])

    scatters, pending = {}, {}

    def put_grads(l, part, g):
        pending.setdefault(l, {}).update(g)
        group = [g_ for g_ in groups[l] if part in g_][0]
        if part == "ffn" and "attn" in group:
            return 0.0
        have = pending[l]
        srcs = {3: lambda: have["w_up"][None], 4: lambda: have["w_down"].reshape(1, D_FF, D_MODEL), 5: lambda: have["conv_w"][None],
                1: lambda: have["w_mem_kv"][None], 2: lambda: have["w_out"][None]}
        if l % 2 == 0:
            srcs[0] = lambda: have["w_in"][None]
        else:
            srcs[0] = lambda: have["w_in"].reshape(1, N_DEV, dil_c, D_MODEL)
        shard_shapes = [cast["w_in_fox" if l % 2 == 0 else "w_in_dil"][l // 2].shape] + \
            [cast[k][l].shape for k in ("w_mem_kv", "w_out", "w_up", "w_down")] + [(8, FF_CHUNK)]
        sources = [srcs[i]() for i in members(group)]
        scatters[l, group] = _exchange_start("scatter", sources, [(N_DEV, 1) + tuple(s) for s in pick(shard_shapes, group)],
                                             pick(places(l), group), sources[0], name=f"grads_scatter_start{tag(l, group)}")
        return scatters[l, group]["token"][0, 0]

    loss, grad_x, gs = _device_step(x[0], mem[0], loss_target[0], small, get_weights, put_grads)

    recv = {}

    def wait_scatter(l, group, after_):
        lands = _exchange_wait("scatter", scatters[l, group], pick(places(l), group), after_, name=f"grads_scatter_wait{tag(l, group)}")
        recv.setdefault(l, {}).update(zip(members(group), lands))

    for l in reversed(range(1, DEPTH)):
        wait_scatter(l, groups[l][0], grad_x)
    wait_scatter(0, ("ffn",), grad_x)
    s_pack = _pack([gs[k] for k in SMALL])
    (s_recv,) = _all_gather([s_pack], [(N_DEV,) + s_pack.shape], [_slot0], name="small_grads_all_gather")

    layer_tensors = ("w_in", "w_mem_kv", "w_out", "w_up", "w_down", "conv_w")
    layer_parts = lambda k: [recv[l][layer_tensors.index(k)] for l in range(DEPTH)]
    to_local = {k: (lambda t: t) for k in big}
    to_local["w_in_fox"] = _fox_permute
    to_local["w_in_dil"] = to_local["w_up"] = transposed
    from_local = {k: (lambda t: t) for k in big}
    from_local["w_in_fox"] = _fox_unpermute
    from_local["w_in_dil"] = from_local["w_up"] = transposed
    blocks = {"w_in_fox": rows["w_in_fox"], "w_in_dil": dil_c, "w_mem_kv": rows["w_mem_kv"], "w_out": rows["w_out"], "w_up": FF_CHUNK // 4,
              "w_down": rows["w_down"] // 2, "conv_w": 3}
    outs = {}

    def update(k, parts):
        f = to_local[k]
        outs[k] = [from_local[k](t) for t in _adamw(parts, f(w[k]), f(m[k]), f(v[k]), br=blocks[k], name=f"adamw_{k}")]

    update("w_in_dil", [recv[l][0] for l in range(1, DEPTH, 2)])
    update("w_up", layer_parts("w_up"))
    update("w_down", layer_parts("w_down"))
    update("conv_w", [p[:, :, :3, :] for p in layer_parts("conv_w")])
    wait_scatter(0, ("attn",), outs["w_down"][1])
    update("w_in_fox", [recv[l][0] for l in range(0, DEPTH, 2)])
    update("w_mem_kv", layer_parts("w_mem_kv"))
    update("w_out", layer_parts("w_out"))
    small_outs = _adamw([s_recv], _pack([w[k] for k in SMALL]), _pack([m[k] for k in SMALL]), _pack([v[k] for k in SMALL]),
                        br=s_pack.shape[1], name="adamw_small")
    res = []
    for i, os_ in enumerate(small_outs):
        d = {k: outs[k][i] for k in big}
        d.update(zip(SMALL, _unpack(os_, small_shapes)))
        res.append([d[k] for k in names])
    loss = lax.psum(loss, ("x", "y", "c"))
    return (loss, grad_x[None], *res[0], *res[1], *res[2], *res[3])
```

```python
import functools
import math

import jax
import jax.numpy as jnp
from jax import lax
from jax.experimental import pallas as pl
from jax.experimental.pallas import tpu as pltpu

F32 = jnp.float32
BF16 = jnp.bfloat16

D_MODEL = 1024
HEAD_DIM = 64
N_MIX_HEADS = 12
N_MEM_HEADS = 4
D_MIX = N_MIX_HEADS * HEAD_DIM
D_MEMQ = N_MEM_HEADS * HEAD_DIM
D_FF = 2816
DEPTH = 4
FOX_IN = 3 * D_MIX + N_MIX_HEADS + D_MEMQ
DIL_IN = 3 * D_MIX + D_MEMQ
LANES = 128
FOX_P = DIL_IN + LANES
N_MIX_HP = D_MIX // LANES
N_MEM_HP = D_MEMQ // LANES
QM_COL = 3 * N_MIX_HP
F_COL = DIL_IN // LANES
DILATED_BRANCHES = ((128, 1), (512, 4), (2048, 16))
DIL_L = 128
DIL_UNROLL_FWD = 8
DIL_UNROLL_BWD = 4
ROPE_THETA = 10000.0
NORM_EPS = 1e-6
NEG = -1e30
SCALE = HEAD_DIM ** -0.5
N_DEV = 8

ADAM_LR = 0.001
ADAM_B1 = 0.9
ADAM_B2 = 0.999
ADAM_EPS = 1e-08
ADAM_WD = 0.01
ADAM_STEP = 10

VMEM_LIMIT = 56 * 1024 * 1024
PACK_W = 1024
PACK_ROW_ALIGN = 8

MESH = pl.DeviceIdType.MESH
NT = (((1,), (1,)), ((), ()))
NN = (((1,), (0,)), ((), ()))
TN = (((0,), (0,)), ((), ()))


def _params(*sem):
    return pltpu.CompilerParams(dimension_semantics=sem, vmem_limit_bytes=VMEM_LIMIT)


def _lane_lo(shape):
    return lax.broadcasted_iota(jnp.int32, shape, len(shape) - 1) < HEAD_DIM


def _pair(lo, a, b):
    return jnp.where(lo, a, b)


def _mm(a, b, mode, *, tm, tn, name, out_dtype=F32, res=None, layer=None, chunk=None, norm_gain=None):
    lead = () if layer is None else (layer,)
    nl = (None,) * len(lead)
    bs = b.shape[len(lead):]
    dims = {"nn": NN, "nt": NT, "tn": TN}[mode]
    reduce_n = 0
    if chunk is None:
        (m, k) = a.shape[::-1] if mode == "tn" else a.shape
        n = bs[0] if mode == "nt" else bs[1]
        grid = (m // tm, n // tn)
        a_spec = pl.BlockSpec((k, tm), lambda i, j: (0, i)) if mode == "tn" else pl.BlockSpec((tm, k), lambda i, j: (i, 0))
        b_spec = pl.BlockSpec(nl + ((tn, k) if mode == "nt" else (k, tn)), lambda i, j: lead + ((j, 0) if mode == "nt" else (0, j)))
        o_spec = pl.BlockSpec((tm, tn), lambda i, j: (i, j))
        out_shape = (m, n)
    elif chunk == "b":
        (m, k) = a.shape[::-1] if mode == "tn" else a.shape
        c, nc = bs[0], (bs[1] if mode == "nt" else bs[2])
        grid = (m // tm, c)
        a_spec = pl.BlockSpec((k, tm), lambda i, j: (0, i)) if mode == "tn" else pl.BlockSpec((tm, k), lambda i, j: (i, 0))
        b_spec = pl.BlockSpec(nl + (None,) + tuple(bs[1:]), lambda i, j: lead + (j, 0, 0))
        o_spec = pl.BlockSpec((None, tm, nc), lambda i, j: (j, i, 0))
        out_shape = (c, m, nc)
    elif chunk == "a":
        assert mode == "tn"
        c, k, mc = a.shape
        n = bs[1]
        grid = (c, n // tn)
        a_spec = pl.BlockSpec((None, k, mc), lambda i, j: (i, 0, 0))
        b_spec = pl.BlockSpec(nl + (k, tn), lambda i, j: lead + (0, j))
        o_spec = pl.BlockSpec((None, mc, tn), lambda i, j: (i, 0, j))
        out_shape = (c, mc, n)
    else:
        reduce_n, m, kc = a.shape
        n = bs[1] if mode == "nt" else bs[2]
        grid = (m // tm, n // tn)
        a_spec = pl.BlockSpec((reduce_n, tm, kc), lambda i, j: (0, i, 0))
        b_spec = pl.BlockSpec(nl + ((reduce_n, tn, kc) if mode == "nt" else (reduce_n, kc, tn)),
                              lambda i, j: lead + ((0, j, 0) if mode == "nt" else (0, 0, j)))
        o_spec = pl.BlockSpec((tm, tn), lambda i, j: (i, j))
        out_shape = (m, n)

    if norm_gain is not None:
        assert chunk in (None, "reduce") and tn == n, "the RMSNorm of the result needs whole rows in a block"

    def body(*refs):
        a_ref, b_ref = refs[0], refs[1]
        o_ref = refs[-2] if norm_gain is not None else refs[-1]
        dot = lambda x, y: lax.dot_general(x.astype(BF16), y.astype(BF16), dims, preferred_element_type=F32)
        if reduce_n:
            acc = dot(a_ref[0], b_ref[0])
            for r in range(1, reduce_n):
                acc = acc + dot(a_ref[r], b_ref[r])
        else:
            acc = dot(a_ref[...], b_ref[...])
        if res is not None:
            acc = acc + refs[2][...]
        o_ref[...] = acc.astype(o_ref.dtype)
        if norm_gain is not None:
            rs = lax.rsqrt(jnp.mean(acc * acc, axis=-1, keepdims=True) + NORM_EPS)
            refs[-1][...] = (acc * rs * refs[3][...]).astype(BF16)

    ins = [a, b] + ([res] if res is not None else [])
    specs = [a_spec, b_spec] + ([o_spec] if res is not None else [])
    out_shapes, out_specs = jax.ShapeDtypeStruct(out_shape, out_dtype), o_spec
    if norm_gain is not None:
        assert res is not None
        ins.append(norm_gain)
        specs.append(pl.BlockSpec((1, n), lambda i, j: (0, 0)))
        out_shapes, out_specs = (out_shapes, jax.ShapeDtypeStruct(out_shape, BF16)), (o_spec, o_spec)
    return pl.pallas_call(body, out_shape=out_shapes, grid=grid, in_specs=specs, out_specs=out_specs,
                          compiler_params=_params("parallel", "parallel"), name=name)(*ins)


def _rmsnorm_fwd(x, g, *, br, name):
    r, d = x.shape

    def body(x_ref, g_ref, o_ref):
        xf = x_ref[...]
        rs = lax.rsqrt(jnp.mean(xf * xf, axis=-1, keepdims=True) + NORM_EPS)
        o_ref[...] = (xf * rs * g_ref[...]).astype(BF16)

    return pl.pallas_call(body, out_shape=jax.ShapeDtypeStruct((r, d), BF16), grid=(r // br,),
                          in_specs=[pl.BlockSpec((br, d), lambda i: (i, 0)), pl.BlockSpec((1, d), lambda i: (0, 0))],
                          out_specs=pl.BlockSpec((br, d), lambda i: (i, 0)), compiler_params=_params("parallel"), name=name)(x, g)


def _rms_bwd_math(x, dy, g):
    d = x.shape[-1]
    rs = lax.rsqrt(jnp.mean(x * x, axis=-1, keepdims=True) + NORM_EPS)
    gy = dy * g
    proj = jnp.sum(x * gy, axis=-1, keepdims=True) * (1.0 / d)
    dx = rs * gy - x * (rs * rs * rs) * proj
    dg = jnp.sum(dy * (x * rs), axis=0, keepdims=True)
    return dx, dg


def _rmsnorm_bwd(x, dy, g, res, *, br, name):
    r, d = x.shape
    has_res = res is not None

    def body(*refs):
        x_ref, dy_ref, g_ref = refs[:3]
        dx_ref, dxb_ref, dg_ref = refs[-3:]
        dx, dg = _rms_bwd_math(x_ref[...], dy_ref[...], g_ref[...])
        if has_res:
            dx = dx + refs[3][...]
        dx_ref[...] = dx
        dxb_ref[...] = dx.astype(BF16)

        @pl.when(pl.program_id(0) == 0)
        def _():
            dg_ref[...] = jnp.zeros_like(dg_ref)

        dg_ref[0:1, :] += dg

    row = pl.BlockSpec((br, d), lambda i: (i, 0))
    ins = [x, dy, g] + ([res] if has_res else [])
    specs = [row, row, pl.BlockSpec((1, d), lambda i: (0, 0))] + ([row] if has_res else [])
    return pl.pallas_call(
        body, out_shape=(jax.ShapeDtypeStruct((r, d), F32), jax.ShapeDtypeStruct((r, d), BF16), jax.ShapeDtypeStruct((8, d), F32)),
        grid=(r // br,), in_specs=specs, out_specs=(row, row, pl.BlockSpec((8, d), lambda i: (0, 0))),
        compiler_params=_params("arbitrary"), name=name)(*ins)


def _loss_head(h, target, g, *, br, name):
    r, d = h.shape

    def body(x_ref, t_ref, g_ref, dx_ref, dxb_ref, dg_ref, loss_ref):
        x = x_ref[...]
        gg = g_ref[...]
        rs = lax.rsqrt(jnp.mean(x * x, axis=-1, keepdims=True) + NORM_EPS)
        err = x * rs * gg - t_ref[...]
        part = jnp.sum(jnp.sum(err * err, axis=1, keepdims=True), axis=0, keepdims=True) * (0.5 / d)
        dx, dg = _rms_bwd_math(x, err * (1.0 / d), gg)
        dx_ref[...] = dx
        dxb_ref[...] = dx.astype(BF16)

        @pl.when(pl.program_id(0) == 0)
        def _():
            dg_ref[...] = jnp.zeros_like(dg_ref)
            loss_ref[...] = jnp.zeros_like(loss_ref)

        dg_ref[0:1, :] += dg
        loss_ref[...] += jnp.broadcast_to(part, loss_ref.shape)

    row = pl.BlockSpec((br, d), lambda i: (i, 0))
    return pl.pallas_call(
        body, out_shape=(jax.ShapeDtypeStruct((r, d), F32), jax.ShapeDtypeStruct((r, d), BF16),
                         jax.ShapeDtypeStruct((8, d), F32), jax.ShapeDtypeStruct((8, LANES), F32)),
        grid=(r // br,), in_specs=[row, row, pl.BlockSpec((1, d), lambda i: (0, 0))],
        out_specs=(row, row, pl.BlockSpec((8, d), lambda i: (0, 0)), pl.BlockSpec((8, LANES), lambda i: (0, 0))),
        compiler_params=_params("arbitrary"), name=name)(h, target, g)


def _split3(x):
    hi = x.astype(BF16)
    r1 = x - hi.astype(F32)
    mid = r1.astype(BF16)
    lo = (r1 - mid.astype(F32)).astype(BF16)
    return hi, mid, lo


def _tri_sum(tri, x):
    hi, mid, lo = _split3(x)
    dot = lambda t: jnp.dot(tri, t, preferred_element_type=F32)
    return dot(hi) + dot(mid) + dot(lo)


def _forget_cumsum(proj, b_pad, *, name):
    s = proj.shape[0]
    blk = LANES

    def body(f_ref, b_ref, c_ref):
        ri = lax.broadcasted_iota(jnp.int32, (blk, blk), 0)
        ci = lax.broadcasted_iota(jnp.int32, (blk, blk), 1)
        tri = (ci <= ri).astype(BF16)
        bias = b_ref[...]

        def step(t, carry):
            rows = pl.ds(pl.multiple_of(t * blk, blk), blk)
            z = f_ref[rows, :] + bias
            lf = jnp.minimum(z, 0.0) - jnp.log(1.0 + jnp.exp(-jnp.abs(z)))
            cs = _tri_sum(tri, lf) + carry
            c_ref[rows, :] = cs
            return cs[blk - 1:blk, :]

        lax.fori_loop(0, s // blk, step, jnp.zeros((1, blk), F32))

    return pl.pallas_call(body, out_shape=jax.ShapeDtypeStruct((s, LANES), F32), grid=(1,),
                          in_specs=[pl.BlockSpec((s, LANES), lambda i: (0, F_COL)), pl.BlockSpec((1, LANES), lambda i: (0, 0))],
                          out_specs=pl.BlockSpec((s, LANES), lambda i: (0, 0)), compiler_params=_params("arbitrary"), name=name)(proj, b_pad)


def _forget_cumsum_bwd(proj, b_pad, dcq, dck, *, name):
    s = proj.shape[0]
    blk = LANES
    nblk = s // blk

    def body(f_ref, b_ref, dcq_ref, dck_ref, dz_ref, db_ref):
        ri = lax.broadcasted_iota(jnp.int32, (blk, blk), 0)
        ci = lax.broadcasted_iota(jnp.int32, (blk, blk), 1)
        triu = (ci >= ri).astype(BF16)
        bias = b_ref[...]

        def step(t, carry):
            tail, dbs = carry
            rows = pl.ds(pl.multiple_of((nblk - 1 - t) * blk, blk), blk)
            dc = dcq_ref[rows, :] - dck_ref[rows, :]
            dlf = _tri_sum(triu, dc) + tail
            z = f_ref[rows, :] + bias
            e = jnp.exp(-jnp.abs(z))
            sig_neg = jnp.where(z >= 0.0, e, 1.0) / (1.0 + e)
            dz = dlf * sig_neg
            dz_ref[rows, :] = dz.astype(BF16)
            return dlf[0:1, :], dbs + jnp.sum(dz, axis=0, keepdims=True)

        _, dbs = lax.fori_loop(0, nblk, step, (jnp.zeros((1, blk), F32), jnp.zeros((1, blk), F32)))
        db_ref[...] = jnp.broadcast_to(dbs, db_ref.shape)

    full = pl.BlockSpec((s, LANES), lambda i: (0, 0))
    return pl.pallas_call(body, out_shape=(jax.ShapeDtypeStruct((s, LANES), BF16), jax.ShapeDtypeStruct((8, LANES), F32)), grid=(1,),
                          in_specs=[pl.BlockSpec((s, LANES), lambda i: (0, F_COL)), pl.BlockSpec((1, LANES), lambda i: (0, 0)), full, full],
                          out_specs=(full, pl.BlockSpec((8, LANES), lambda i: (0, 0))), compiler_params=_params("arbitrary"), name=name)(proj, b_pad, dcq, dck)


def _attn_fwd(q_arr, kv_arr, ck6, *, q_col, k_col, v_col, n_hp, causal, bq, bk, name):
    s = q_arr.shape[0]
    skv = kv_arr.shape[0]
    bias = ck6 is not None
    nq = s // bq
    assert not causal or bq == bk

    def body(*refs):
        q_ref, k_ref, v_ref = refs[:3]
        ck_ref = refs[3] if bias else None
        o_ref, lse_ref = refs[-2:]
        i = pl.program_id(1)
        lo = _lane_lo((bq, LANES))
        q = q_ref[...] * SCALE
        qh = (jnp.where(lo, q, 0.0).astype(BF16), jnp.where(lo, 0.0, q).astype(BF16))

        def block(j, carry, diagonal):
            ks = pl.ds(pl.multiple_of(j * bk, bk), bk)
            k = k_ref[ks, :].astype(BF16)
            v = v_ref[ks, :].astype(BF16)
            if diagonal:
                ok = lax.broadcasted_iota(jnp.int32, (bq, bk), 1) <= lax.broadcasted_iota(jnp.int32, (bq, bk), 0)
            out = []
            for h in range(2):
                m, l, acc = carry[3 * h:3 * h + 3]
                sc = lax.dot_general(qh[h], k, NT, preferred_element_type=F32)
                if bias:
                    sc = sc - ck_ref[0, h:h + 1, ks]
                if diagonal:
                    sc = jnp.where(ok, sc, NEG)
                mn = jnp.maximum(m, jnp.max(sc, axis=1, keepdims=True))
                p = jnp.exp(sc - mn)
                al = jnp.exp(m - mn)
                out += [mn, al * l + jnp.sum(p, axis=1, keepdims=True), al * acc + jnp.dot(p.astype(BF16), v, preferred_element_type=F32)]
            return tuple(out)

        col = lambda v_: jnp.full((bq, 1), v_, F32)
        init = (col(NEG), col(0.0), jnp.zeros((bq, LANES), F32)) * 2
        n_full = i if causal else skv // bk
        carry = lax.fori_loop(0, n_full, functools.partial(block, diagonal=False), init)
        if causal:
            carry = block(i, carry, True)
        m0, l0, a0, m1, l1, a1 = carry
        o_ref[...] = _pair(lo, a0 / l0, a1 / l1)
        lse_ref[0] = _pair(lo, m0 + jnp.log(l0), m1 + jnp.log(l1))

    specs = [pl.BlockSpec((bq, LANES), lambda h, i: (i, q_col + h)),
             pl.BlockSpec((skv, LANES), lambda h, i: (0, k_col + h)),
             pl.BlockSpec((skv, LANES), lambda h, i: (0, v_col + h))]
    ins = [q_arr, kv_arr, kv_arr]
    if bias:
        specs += [pl.BlockSpec((1, 8, skv), lambda h, i: (h, 0, 0))]
        ins += [ck6]
    return pl.pallas_call(
        body, out_shape=(jax.ShapeDtypeStruct((s, n_hp * LANES), F32), jax.ShapeDtypeStruct((n_hp, s, LANES), F32)),
        grid=(n_hp, nq), in_specs=specs,
        out_specs=(pl.BlockSpec((bq, LANES), lambda h, i: (i, h)), pl.BlockSpec((1, bq, LANES), lambda h, i: (h, i, 0))),
        compiler_params=_params("parallel", "parallel"), name=name)(*ins)


def _attn_bwd(q_arr, kv_arr, o_arr, do_arr, lse, ck6, *, q_col, k_col, v_col, o_col, n_hp, causal, bq, bk, name):
    s = q_arr.shape[0]
    skv = kv_arr.shape[0]
    bias = ck6 is not None
    nq = s // bq
    assert not causal or bq == bk

    def body(*refs):
        q_ref, k_ref, v_ref, o_ref, do_ref, lse_ref = refs[:6]
        if bias:
            ck_ref = refs[6]
            dq_ref, dk_ref, dv_ref, dcq_ref, dck_ref = refs[-5:]
        else:
            dq_ref, dk_ref, dv_ref = refs[-3:]
        j = pl.program_id(1)
        lo_q = _lane_lo((bq, LANES))
        lo_k = _lane_lo((bk, LANES))
        k = k_ref[...]
        v = v_ref[...].astype(BF16)
        kb = k.astype(BF16)
        kh = (jnp.where(lo_k, k, 0.0).astype(BF16), jnp.where(lo_k, 0.0, k).astype(BF16))
        if bias:
            pick_k = [(lax.broadcasted_iota(jnp.int32, (8, bk), 0) == h).astype(BF16) for h in range(2)]
            pick_q = [(lax.broadcasted_iota(jnp.int32, (8, bq), 0) == h).astype(BF16) for h in range(2)]

        @pl.when(j == 0)
        def _():
            dq_ref[...] = jnp.zeros_like(dq_ref)
            if bias:
                dcq_ref[...] = jnp.zeros_like(dcq_ref)

        def block(i, carry, diagonal):
            dk_acc, dv_acc, cs = carry
            qs = pl.ds(pl.multiple_of(i * bq, bq), bq)
            q = q_ref[qs, :] * SCALE
            do = do_ref[qs, :]
            dd = do * o_ref[qs, :]
            lse_i = lse_ref[0, qs, :]
            qh = (jnp.where(lo_q, q, 0.0).astype(BF16), jnp.where(lo_q, 0.0, q).astype(BF16))
            doh = (jnp.where(lo_q, do, 0.0).astype(BF16), jnp.where(lo_q, 0.0, do).astype(BF16))
            dh = (jnp.sum(jnp.where(lo_q, dd, 0.0), axis=1, keepdims=True), jnp.sum(jnp.where(lo_q, 0.0, dd), axis=1, keepdims=True))
            if diagonal:
                ok = lax.broadcasted_iota(jnp.int32, (bq, bk), 1) <= lax.broadcasted_iota(jnp.int32, (bq, bk), 0)
            dq_blk = None
            rs = None
            for h in range(2):
                sc = lax.dot_general(qh[h], kb, NT, preferred_element_type=F32)
                if bias:
                    sc = sc - ck_ref[0, h:h + 1, :]
                if diagonal:
                    sc = jnp.where(ok, sc, NEG)
                p = jnp.exp(sc - lse_i[:, h * HEAD_DIM:h * HEAD_DIM + 1])
                ds = p * (lax.dot_general(doh[h], v, NT, preferred_element_type=F32) - dh[h])
                dsb = ds.astype(BF16)
                dv_acc = dv_acc + lax.dot_general(p.astype(BF16), doh[h], TN, preferred_element_type=F32)
                dk_acc = dk_acc + lax.dot_general(dsb, qh[h], TN, preferred_element_type=F32)
                part = jnp.dot(dsb, kh[h], preferred_element_type=F32)
                dq_blk = part if dq_blk is None else dq_blk + part
                if bias:
                    cs = cs + jnp.dot(pick_q[h], dsb, preferred_element_type=F32)
                    row_sums = lax.dot_general(pick_k[h], dsb, NT, preferred_element_type=F32)
                    rs = row_sums if rs is None else rs + row_sums
            dq_ref[qs, :] += dq_blk * SCALE
            if bias:
                dcq_ref[0, :, qs] += rs
            return dk_acc, dv_acc, cs

        carry = (jnp.zeros((bk, LANES), F32), jnp.zeros((bk, LANES), F32), jnp.zeros((8, bk), F32))
        if causal:
            carry = block(j, carry, True)
        dk_acc, dv_acc, cs = lax.fori_loop(j + 1 if causal else 0, nq, functools.partial(block, diagonal=False), carry)
        dk_ref[...] = dk_acc.astype(BF16)
        dv_ref[...] = dv_acc.astype(BF16)
        if bias:
            dck_ref[0] = cs

    full_q = lambda c: pl.BlockSpec((s, LANES), lambda h, j: (0, c + h))
    specs = [full_q(q_col),
             pl.BlockSpec((bk, LANES), lambda h, j: (j, k_col + h)),
             pl.BlockSpec((bk, LANES), lambda h, j: (j, v_col + h)),
             full_q(0), full_q(o_col),
             pl.BlockSpec((1, s, LANES), lambda h, j: (h, 0, 0))]
    ins = [q_arr, kv_arr, kv_arr, o_arr, do_arr, lse]
    out_shape = [jax.ShapeDtypeStruct((s, n_hp * LANES), F32), jax.ShapeDtypeStruct((skv, n_hp * LANES), BF16),
                 jax.ShapeDtypeStruct((skv, n_hp * LANES), BF16)]
    out_specs = [full_q(0), pl.BlockSpec((bk, LANES), lambda h, j: (j, h)), pl.BlockSpec((bk, LANES), lambda h, j: (j, h))]
    if bias:
        specs += [pl.BlockSpec((1, 8, bk), lambda h, j: (h, 0, j))]
        ins += [ck6]
        out_shape += [jax.ShapeDtypeStruct((n_hp, 8, s), F32), jax.ShapeDtypeStruct((n_hp, 8, skv), F32)]
        out_specs += [pl.BlockSpec((1, 8, s), lambda h, j: (h, 0, 0)), pl.BlockSpec((1, 8, bk), lambda h, j: (h, 0, j))]
    return pl.pallas_call(body, out_shape=tuple(out_shape), grid=(n_hp, skv // bk), in_specs=specs, out_specs=tuple(out_specs),
                          compiler_params=_params("parallel", "arbitrary"), name=name)(*ins)


def _rope_tables(s):
    inv = 1.0 / (ROPE_THETA ** (jnp.arange(0, HEAD_DIM, 2, dtype=F32) / HEAD_DIM))
    ang = jnp.arange(s, dtype=F32)[:, None] * inv[None, :]
    cos, sin = jnp.cos(ang), jnp.sin(ang)
    return jnp.tile(cos, (1, 4)), jnp.concatenate([-sin, sin, -sin, sin], axis=1)


def _rope(x_arr, cos_t, sin_t, *, n_cols, out_dtype, br, name):
    s = x_arr.shape[0]

    def body(x_ref, c_ref, s_ref, o_ref):
        cos, sin = c_ref[...], s_ref[...]
        first = (lax.broadcasted_iota(jnp.int32, (br, LANES), 1) % HEAD_DIM) < (HEAD_DIM // 2)
        for j in range(n_cols):
            lanes = slice(j * LANES, (j + 1) * LANES)
            x = x_ref[:, lanes].astype(F32)
            swapped = jnp.where(first, pltpu.roll(x, LANES - HEAD_DIM // 2, 1), pltpu.roll(x, HEAD_DIM // 2, 1))
            o_ref[:, lanes] = (x * cos + swapped * sin).astype(o_ref.dtype)

    tab = pl.BlockSpec((br, LANES), lambda i: (i, 0))
    blk = pl.BlockSpec((br, n_cols * LANES), lambda i: (i, 0))
    return pl.pallas_call(body, out_shape=jax.ShapeDtypeStruct((s, n_cols * LANES), out_dtype), grid=(s // br,),
                          in_specs=[blk, tab, tab], out_specs=blk, compiler_params=_params("parallel"), name=name)(x_arr, cos_t, sin_t)


def _stack_heads(x):
    lo = _lane_lo(x.shape)
    return jnp.concatenate([jnp.where(lo, x, 0.0), jnp.where(lo, 0.0, x)], axis=0).astype(BF16)


def _unstack_heads(x):
    return jnp.where(_lane_lo((DIL_L, LANES)), x[:DIL_L], x[DIL_L:])


def _dil_scores(q_ref, k_ref, cur, prev, has_prev):
    qs = _stack_heads(q_ref[cur, :] * SCALE)
    kk = jnp.concatenate([k_ref[prev, :], k_ref[cur, :]], axis=0).astype(BF16)
    a = lax.broadcasted_iota(jnp.int32, (2 * DIL_L, 2 * DIL_L), 0) & (DIL_L - 1)
    c = lax.broadcasted_iota(jnp.int32, (2 * DIL_L, 2 * DIL_L), 1)
    ok = ((c < DIL_L) & (c >= a) & has_prev) | ((c >= DIL_L) & (c - DIL_L <= a))
    return qs, kk, jnp.where(ok, lax.dot_general(qs, kk, NT, preferred_element_type=F32), NEG)


def _dil_rows(t, dil):
    r, m = t % dil, t // dil
    start = m * (DIL_L * dil) + r
    prev = jnp.maximum(start - DIL_L * dil, 0)
    return pl.ds(start, DIL_L, stride=dil), pl.ds(prev, DIL_L, stride=dil), m > 0


def _softmax3(a, b, c):
    m = jnp.maximum(jnp.maximum(a, b), c)
    ea, eb, ec = jnp.exp(a - m), jnp.exp(b - m), jnp.exp(c - m)
    den = ea + eb + ec
    inv = 1.0 / den
    return ea * inv, eb * inv, ec * inv, m + jnp.log(den)


def _dil_fwd(qk_r, proj, *, name):
    s = qk_r.shape[0]
    nsub = s // DIL_L
    mb = 512

    def body(q_ref, k_ref, v_ref, mix_ref, l1_ref, l2_ref, l3_ref, o1_scr, o2_scr, o3_scr):
        lo = _lane_lo((DIL_L, LANES))
        for (_, dil), o_scr, l_ref in zip(DILATED_BRANCHES, (o1_scr, o2_scr, o3_scr), (l1_ref, l2_ref, l3_ref)):
            def step(t, carry, dil=dil, o_scr=o_scr, l_ref=l_ref):
                cur, prev, has_prev = _dil_rows(t, dil)
                _, _, sc = _dil_scores(q_ref, k_ref, cur, prev, has_prev)
                vv = jnp.concatenate([v_ref[prev, :], v_ref[cur, :]], axis=0).astype(BF16)
                m = jnp.max(sc, axis=1, keepdims=True)
                e = jnp.exp(sc - m)
                den = jnp.sum(e, axis=1, keepdims=True)
                o = jnp.dot((e * (1.0 / den)).astype(BF16), vv, preferred_element_type=F32)
                o_scr[cur, :] = _unstack_heads(o)
                l_ref[cur, :] = _unstack_heads(jnp.broadcast_to(m + jnp.log(den), (2 * DIL_L, LANES)))
                return carry

            lax.fori_loop(0, nsub, step, 0, unroll=DIL_UNROLL_FWD)

        def merge(i, carry):
            rows = pl.ds(pl.multiple_of(i * mb, mb), mb)
            wa, wb, wc, _ = _softmax3(l1_ref[rows, :], l2_ref[rows, :], l3_ref[rows, :])
            mix_ref[rows, :] = wa * o1_scr[rows, :] + wb * o2_scr[rows, :] + wc * o3_scr[rows, :]
            return carry

        lax.fori_loop(0, s // mb, merge, 0)

    col = lambda arr_col: pl.BlockSpec((s, LANES), lambda h: (0, arr_col + h))
    shp = jax.ShapeDtypeStruct((s, D_MIX), F32)
    mix, l1, l2, l3 = pl.pallas_call(
        body, out_shape=(shp, shp, shp, shp), grid=(N_MIX_HP,), in_specs=[col(0), col(N_MIX_HP), col(2 * N_MIX_HP)],
        out_specs=(col(0),) * 4, scratch_shapes=[pltpu.VMEM((s, LANES), F32)] * 3,
        compiler_params=_params("parallel"), name=name)(qk_r, qk_r, proj)
    return mix, (l1, l2, l3)


def _dil_bwd(qk_r, proj, mix, dheads, lses, *, name):
    s = qk_r.shape[0]
    nsub = s // DIL_L
    mb = 512

    def body(q_ref, k_ref, v_ref, mix_ref, dm_ref, l1_ref, l2_ref, l3_ref, dq_ref, dk_ref, dv_ref, lt_scr, dd_scr):
        lo = _lane_lo((DIL_L, LANES))
        lo_m = _lane_lo((mb, LANES))

        def prep(i, carry):
            rows = pl.ds(pl.multiple_of(i * mb, mb), mb)
            _, _, _, lt = _softmax3(l1_ref[rows, :], l2_ref[rows, :], l3_ref[rows, :])
            lt_scr[rows, :] = lt
            dd = dm_ref[rows, :] * mix_ref[rows, :]
            dd_scr[rows, :] = _pair(lo_m, jnp.sum(jnp.where(lo_m, dd, 0.0), axis=1, keepdims=True),
                                    jnp.sum(jnp.where(lo_m, 0.0, dd), axis=1, keepdims=True))
            zero = jnp.zeros((mb, LANES), F32)
            dq_ref[rows, :] = zero
            dk_ref[rows, :] = zero
            dv_ref[rows, :] = zero
            return carry

        lax.fori_loop(0, s // mb, prep, 0)

        for (_, dil), l_ref in zip(DILATED_BRANCHES, (l1_ref, l2_ref, l3_ref)):
            def step(t, carry, dil=dil, l_ref=l_ref):
                cur, prev, has_prev = _dil_rows(t, dil)
                qs, kk, sc = _dil_scores(q_ref, k_ref, cur, prev, has_prev)
                vv = jnp.concatenate([v_ref[prev, :], v_ref[cur, :]], axis=0).astype(BF16)
                lg = l_ref[cur, :]
                w = jnp.exp(lg - lt_scr[cur, :])
                wd = w * dd_scr[cur, :]
                column = lambda x: jnp.concatenate([x[:, 0:1], x[:, HEAD_DIM:HEAD_DIM + 1]], axis=0)
                dos = _stack_heads(w * dm_ref[cur, :])
                p = jnp.exp(sc - column(lg))
                ds = (p * (lax.dot_general(dos, vv, NT, preferred_element_type=F32) - column(wd))).astype(BF16)
                dq_ref[cur, :] += _unstack_heads(jnp.dot(ds, kk, preferred_element_type=F32)) * SCALE
                dkk = lax.dot_general(ds, qs, TN, preferred_element_type=F32)
                dvv = lax.dot_general(p.astype(BF16), dos, TN, preferred_element_type=F32)
                dk_ref[cur, :] += dkk[DIL_L:]
                dv_ref[cur, :] += dvv[DIL_L:]
                dk_ref[prev, :] += dkk[:DIL_L]
                dv_ref[prev, :] += dvv[:DIL_L]
                return carry

            lax.fori_loop(0, nsub, step, 0, unroll=DIL_UNROLL_BWD)

    col = lambda arr_col: pl.BlockSpec((s, LANES), lambda h: (0, arr_col + h))
    shp = jax.ShapeDtypeStruct((s, D_MIX), F32)
    return pl.pallas_call(
        body, out_shape=(shp, shp, shp), grid=(N_MIX_HP,),
        in_specs=[col(0), col(N_MIX_HP), col(2 * N_MIX_HP), col(0), col(0), col(0), col(0), col(0)], out_specs=(col(0),) * 3,
        scratch_shapes=[pltpu.VMEM((s, LANES), F32)] * 2,
        compiler_params=_params("parallel"), name=name)(qk_r, qk_r, proj, mix, dheads, *lses)


CONV_BR = 512
FF_CHUNK = 2 * D_FF // N_DEV
FF_HALF = N_DEV // 2
HALO = 8


def _shift_down(x, halo, k):
    row = lax.broadcasted_iota(jnp.int32, x.shape, 0)
    y = pltpu.roll(x, k, 0)
    for r in range(k):
        y = jnp.where(row == r, halo[HALO - k + r:HALO - k + r + 1, :], y)
    return y


def _shift_up(x, halo, k):
    n = x.shape[0]
    row = lax.broadcasted_iota(jnp.int32, x.shape, 0)
    y = pltpu.roll(x, n - k, 0)
    for r in range(k):
        y = jnp.where(row == n - k + r, halo[r:r + 1, :], y)
    return y


def _conv_vals(u, halo, w, b):
    s1 = _shift_down(u, halo, 1)
    s2 = _shift_down(u, halo, 2)
    return b + w[0:1, :] * s2 + w[1:2, :] * s1 + w[2:3, :] * u, s1, s2


def _conv_in_specs(order, layer):
    rc = (lambda i, j: (i, j)) if order == "rc" else (lambda j, i: (i, j))
    per = CONV_BR // HALO
    main = lambda off: pl.BlockSpec((None, CONV_BR, FF_CHUNK), lambda *g: (off + rc(*g)[1], rc(*g)[0], 0))
    halo = lambda off: pl.BlockSpec((None, HALO, FF_CHUNK), lambda *g: (off + rc(*g)[1], jnp.maximum(rc(*g)[0] * per - 1, 0), 0))
    wspec = lambda off: pl.BlockSpec((None, None, 3, FF_CHUNK), lambda *g: (layer, off + rc(*g)[1], 0, 0))
    bspec = lambda off: pl.BlockSpec((None, 1, FF_CHUNK), lambda *g: (off + rc(*g)[1], 0, 0))
    return [main(0), halo(0), main(FF_HALF), halo(FF_HALF), wspec(0), wspec(FF_HALF), bspec(0), bspec(FF_HALF)]


def _conv_fwd(u, cw, cb, layer, *, name):
    s = u.shape[1]

    def body(uv_ref, hv_ref, ug_ref, hg_ref, wv_ref, wg_ref, bv_ref, bg_ref, o_ref):
        first = pl.program_id(0) == 0
        hv = jnp.where(first, 0.0, hv_ref[...])
        hg = jnp.where(first, 0.0, hg_ref[...])
        val, _, _ = _conv_vals(uv_ref[...], hv, wv_ref[...], bv_ref[...])
        gate, _, _ = _conv_vals(ug_ref[...], hg, wg_ref[...], bg_ref[...])
        o_ref[...] = (gate / (1.0 + jnp.exp(-gate)) * val).astype(BF16)

    return pl.pallas_call(body, out_shape=jax.ShapeDtypeStruct((FF_HALF, s, FF_CHUNK), BF16), grid=(s // CONV_BR, FF_HALF),
                          in_specs=_conv_in_specs("rc", layer), out_specs=pl.BlockSpec((None, CONV_BR, FF_CHUNK), lambda i, j: (j, i, 0)),
                          compiler_params=_params("parallel", "parallel"), name=name)(u, u, u, u, cw, cw, cb, cb)


def _swiglu_bwd(val, gate, da):
    sg = 1.0 / (1.0 + jnp.exp(-gate))
    return da * (gate * sg), da * val * (sg * (1.0 + gate * (1.0 - sg)))


def _conv_bwd(u, cw, cb, da, layer, *, name):
    s = u.shape[1]
    nrow = s // CONV_BR
    per = CONV_BR // HALO

    def body(uv_ref, hv_ref, ug_ref, hg_ref, wv_ref, wg_ref, bv_ref, bg_ref, da_ref, nv_ref, ng_ref, nda_ref, du_ref, dwb_ref):
        i = pl.program_id(1)
        first, last = i == 0, i == nrow - 1
        hv = jnp.where(first, 0.0, hv_ref[...])
        hg = jnp.where(first, 0.0, hg_ref[...])
        uv, ug = uv_ref[...], ug_ref[...]
        wv, wg, bv, bg = wv_ref[...], wg_ref[...], bv_ref[...], bg_ref[...]
        val, v1, v2 = _conv_vals(uv, hv, wv, bv)
        gate, g1, g2 = _conv_vals(ug, hg, wg, bg)
        dval, dgate = _swiglu_bwd(val, gate, da_ref[...])
        val_n, _, _ = _conv_vals(nv_ref[...], uv[CONV_BR - HALO:, :], wv, bv)
        gate_n, _, _ = _conv_vals(ng_ref[...], ug[CONV_BR - HALO:, :], wg, bg)
        dval_n, dgate_n = _swiglu_bwd(val_n, gate_n, nda_ref[...])
        dval_n = jnp.where(last, 0.0, dval_n)
        dgate_n = jnp.where(last, 0.0, dgate_n)
        back = lambda dc, dc_n, w: w[2:3, :] * dc + w[1:2, :] * _shift_up(dc, dc_n, 1) + w[0:1, :] * _shift_up(dc, dc_n, 2)
        du_ref[0] = back(dval, dval_n, wv).astype(BF16)
        du_ref[1] = back(dgate, dgate_n, wg).astype(BF16)

        @pl.when(first)
        def _():
            dwb_ref[...] = jnp.zeros_like(dwb_ref)

        cs = lambda t: jnp.sum(t, axis=0, keepdims=True)
        r8 = lax.broadcasted_iota(jnp.int32, (8, FF_CHUNK), 0)
        rows4 = lambda a, b, c, d: jnp.where(r8 == 0, a, jnp.where(r8 == 1, b, jnp.where(r8 == 2, c, jnp.where(r8 == 3, d, 0.0))))
        dwb_ref[0] += rows4(cs(dval * v2), cs(dval * v1), cs(dval * uv), cs(dval))
        dwb_ref[1] += rows4(cs(dgate * g2), cs(dgate * g1), cs(dgate * ug), cs(dgate))

    nxt = lambda off: pl.BlockSpec((None, HALO, FF_CHUNK), lambda j, i: (off + j, jnp.minimum((i + 1) * per, nrow * per - 1), 0))
    specs = _conv_in_specs("cr", layer) + [pl.BlockSpec((None, CONV_BR, FF_CHUNK), lambda j, i: (j, i, 0)), nxt(0), nxt(FF_HALF), nxt(0)]
    return pl.pallas_call(
        body, out_shape=(jax.ShapeDtypeStruct((2, FF_HALF, s, FF_CHUNK), BF16), jax.ShapeDtypeStruct((2, FF_HALF, 8, FF_CHUNK), F32)),
        grid=(FF_HALF, nrow), in_specs=specs,
        out_specs=(pl.BlockSpec((2, None, CONV_BR, FF_CHUNK), lambda j, i: (0, j, i, 0)),
                   pl.BlockSpec((2, None, 8, FF_CHUNK), lambda j, i: (0, j, 0, 0))),
        compiler_params=_params("parallel", "arbitrary"), name=name)(u, u, u, u, cw, cw, cb, cb, da, u, u, da)


def _rows_of(r):
    return lambda ref, idx: ref.at[:, pl.ds(idx * r, r), :]


def _slot1(ref, idx):
    return ref.at[:, idx]


def _slot0(ref, idx):
    return ref.at[idx]


def _all_gather(shards, full_shapes, places, *, name):
    n = len(shards)

    def body(*refs):
        ins, outs = refs[:n], refs[n:2 * n]
        send_sems, recv_sems, local_sems = refs[2 * n:]
        mx, my, mc = lax.axis_index("x"), lax.axis_index("y"), lax.axis_index("c")
        me, sibling = (mx, my, mc), (mx, my, 1 - mc)
        chips = [(1 - mx, my), (mx, 1 - my), (1 - mx, 1 - my)]

        def win(t, px, py, pc):
            return places[t](outs[t], 4 * px + 2 * py + pc)

        def copy(t, k, block, to, src=None):
            return pltpu.make_async_remote_copy(src_ref=win(t, *block) if src is None else src, dst_ref=win(t, *block),
                                                send_sem=send_sems.at[t, k], recv_sem=recv_sems.at[t, k], device_id=to, device_id_type=MESH)

        mine = [pltpu.make_async_copy(ins[t], win(t, *me), local_sems.at[t]) for t in range(n)]
        for cp in mine:
            cp.start()
        first = []
        for t in range(n):
            first += [copy(t, 0, me, sibling, src=ins[t])] + [copy(t, 1 + j, me, (*chip, mc), src=ins[t]) for j, chip in enumerate(chips)]
        for cp in first:
            cp.start()
        passed = []
        for j, chip in enumerate(chips):
            for t in range(n):
                copy(t, 1 + j, (*chip, mc), me).wait_recv()
                fwd = copy(t, 4 + j, (*chip, mc), sibling)
                fwd.start()
                passed.append(fwd)
        for t in range(n):
            copy(t, 0, sibling, me).wait_recv()
            for j, chip in enumerate(chips):
                copy(t, 4 + j, (*chip, 1 - mc), me).wait_recv()
        for cp in first + passed:
            cp.wait_send()
        for cp in mine:
            cp.wait()

    hbm = pl.BlockSpec(memory_space=pl.ANY)
    return pl.pallas_call(
        body, out_shape=tuple(jax.ShapeDtypeStruct(s, x.dtype) for s, x in zip(full_shapes, shards)),
        in_specs=[hbm] * n, out_specs=(hbm,) * n,
        scratch_shapes=[pltpu.SemaphoreType.DMA((n, 7)), pltpu.SemaphoreType.DMA((n, 7)), pltpu.SemaphoreType.DMA((n,))],
        name=name)(*shards)


FLIPS = [(fx, fy, fc) for fx in (0, 1) for fy in (0, 1) for fc in (0, 1)][1:]


def _exchange_copies(kind, places, src, land, send_sems, recv_sems, local_sems):
    mx, my, mc = lax.axis_index("x"), lax.axis_index("y"), lax.axis_index("c")
    me = 4 * mx + 2 * my + mc
    n = len(src)
    local, remote = [], []
    for t in range(n):
        if kind == "gather":
            local.append(pltpu.make_async_copy(src[t], places[t](land[t], me), local_sems.at[t]))
        else:
            local.append(pltpu.make_async_copy(places[t](src[t], me), land[t].at[me], local_sems.at[t]))
    for k, (fx, fy, fc) in enumerate(FLIPS):
        px, py, pc = mx ^ fx, my ^ fy, mc ^ fc
        peer = 4 * px + 2 * py + pc
        for t in range(n):
            sems = dict(send_sem=send_sems.at[7 * t + k], recv_sem=recv_sems.at[7 * t + k], device_id=(px, py, pc), device_id_type=MESH)
            if kind == "gather":
                pair = [(src[t], places[t](land[t], me)), (src[t], places[t](land[t], peer))]
            else:
                pair = [(places[t](src[t], peer), land[t].at[me]), (places[t](src[t], peer), land[t].at[peer])]
            remote.append([functools.partial(pltpu.make_async_remote_copy, src_ref=s_, dst_ref=d_, **sems) for s_, d_ in pair])
    return local, remote


HBM_SPEC = pl.BlockSpec(memory_space=pltpu.HBM)
SEM_SPEC = pl.BlockSpec(memory_space=pltpu.SEMAPHORE)
SIDE_EFFECT = pltpu.SideEffectType.DATAFLOW_SIDE_EFFECTING


def _exchange_start(kind, srcs, land_shapes, places, after, *, name):
    n = len(srcs)

    def body(*refs):
        src, land = refs[:n], refs[n:2 * n]
        send_sems, recv_sems, local_sems = refs[2 * n + 1:2 * n + 4]
        token = refs[-1]
        local, remote = _exchange_copies(kind, places, src, land, send_sems, recv_sems, local_sems)
        for cp in local:
            cp.start()
        for send, _ in remote:
            send().start()
        token[...] = jnp.zeros_like(token)

    hbm = lambda t: pltpu.with_memory_space_constraint(t, pltpu.HBM)
    lands = [hbm(lax.empty(tuple(s), x.dtype)) for s, x in zip(land_shapes, srcs)]
    out_shape = (pltpu.SemaphoreType.DMA((7 * n,)), pltpu.SemaphoreType.DMA((7 * n,)), pltpu.SemaphoreType.DMA((n,)),
                 *[pltpu.HBM(x.shape, x.dtype) for x in srcs], *[pltpu.HBM(tuple(s), x.dtype) for s, x in zip(land_shapes, srcs)],
                 jax.ShapeDtypeStruct((8, LANES), F32))
    outs = pl.pallas_call(
        body, name=name, out_shape=out_shape, in_specs=[HBM_SPEC] * (2 * n) + [pl.BlockSpec(memory_space=pl.ANY)],
        out_specs=(SEM_SPEC, SEM_SPEC, SEM_SPEC) + (HBM_SPEC,) * (2 * n) + (pl.BlockSpec(memory_space=pltpu.VMEM),),
        input_output_aliases={i: 3 + i for i in range(2 * n)},
        compiler_params=pltpu.CompilerParams(has_side_effects=SIDE_EFFECT))(*[hbm(x) for x in srcs], *lands, after)
    return dict(sems=outs[:3], src=outs[3:3 + n], land=outs[3 + n:3 + 2 * n], token=outs[-1])


def _exchange_wait(kind, started, places, after, *, name):
    n = len(started["src"])

    def body(*refs):
        src, land = refs[:n], refs[n:2 * n]
        send_sems, recv_sems, local_sems = refs[2 * n:2 * n + 3]
        local, remote = _exchange_copies(kind, places, src, land, send_sems, recv_sems, local_sems)
        for cp in local:
            cp.wait()
        for send, arrival in remote:
            send().wait_send()
            arrival().wait_recv()

    out_shape = tuple(pltpu.HBM(x.shape, x.dtype) for x in started["src"]) + tuple(pltpu.HBM(x.shape, x.dtype) for x in started["land"])
    outs = pl.pallas_call(
        body, name=name, out_shape=out_shape,
        in_specs=[HBM_SPEC] * (2 * n) + [SEM_SPEC] * 3 + [pl.BlockSpec(memory_space=pl.ANY)], out_specs=(HBM_SPEC,) * (2 * n),
        input_output_aliases={i: i for i in range(2 * n)},
        compiler_params=pltpu.CompilerParams(has_side_effects=SIDE_EFFECT))(*started["src"], *started["land"], *started["sems"], after)
    return list(outs[n:])


def _adamw(parts, w, m, v, *, br, name):
    layers, r, wd = w.shape
    assert len(parts) == layers

    def body(*refs):
        p_refs = refs[:layers]
        w_ref, m_ref, v_ref, g_ref, d_ref, nm_ref, nv_ref = refs[layers:]
        for k in range(layers):
            @pl.when(pl.program_id(0) == k)
            def _(p_ref=p_refs[k]):
                g = p_ref[0].astype(F32)
                for dev in range(1, N_DEV):
                    g = g + p_ref[dev].astype(F32)
                mm = ADAM_B1 * m_ref[...] + (1.0 - ADAM_B1) * g
                vv = ADAM_B2 * v_ref[...] + (1.0 - ADAM_B2) * (g * g)
                m_hat = mm / (1.0 - ADAM_B1 ** ADAM_STEP)
                v_hat = vv / (1.0 - ADAM_B2 ** ADAM_STEP)
                g_ref[...] = g
                d_ref[...] = -ADAM_LR * (m_hat / (jnp.sqrt(v_hat) + ADAM_EPS) + ADAM_WD * w_ref[...])
                nm_ref[...] = mm
                nv_ref[...] = vv

    p_spec = lambda k: pl.BlockSpec((N_DEV, None, br, wd), lambda l, i: (0, 0, jnp.where(l == k, i, 0), 0))
    blk = pl.BlockSpec((None, br, wd), lambda l, i: (l, i, 0))
    shp = jax.ShapeDtypeStruct((layers, r, wd), F32)
    return pl.pallas_call(body, out_shape=(shp, shp, shp, shp), grid=(layers, r // br),
                          in_specs=[p_spec(k) for k in range(layers)] + [blk, blk, blk], out_specs=(blk, blk, blk, blk),
                          compiler_params=_params("arbitrary", "arbitrary"), name=name)(*parts, w, m, v)


SMALL = ("norm_mix", "norm_mem", "norm_ffn", "b_forget", "conv_b", "norm_final")


def _pack(tensors):
    flat = jnp.concatenate([t.reshape(-1) for t in tensors])
    rows = -(-flat.shape[0] // (PACK_W * PACK_ROW_ALIGN)) * PACK_ROW_ALIGN
    flat = jnp.pad(flat, (0, rows * PACK_W - flat.shape[0]))
    return flat.reshape(1, rows, PACK_W)


def _unpack(buf, shapes):
    flat = buf.reshape(-1)
    out, off = [], 0
    for shp in shapes:
        n = math.prod(shp)
        out.append(flat[off:off + n].reshape(tuple(shp)))
        off += n
    return out


def _fox_permute(w):
    pad = jnp.zeros(w.shape[:-1] + (FOX_P - FOX_IN,), w.dtype)
    return jnp.concatenate([w[..., :3 * D_MIX], w[..., 3 * D_MIX + N_MIX_HEADS:], w[..., 3 * D_MIX:3 * D_MIX + N_MIX_HEADS], pad], axis=-1)


def _fox_unpermute(w):
    return jnp.concatenate([w[..., :3 * D_MIX], w[..., DIL_IN:DIL_IN + N_MIX_HEADS], w[..., 3 * D_MIX:DIL_IN]], axis=-1)


def _bias_layout(c):
    s = c.shape[0]
    ct = c[:, :N_MIX_HEADS].T.reshape(N_MIX_HP, 2, s)
    return jnp.pad(ct, ((0, 0), (0, 6), (0, 0)))


def _bias_grad(dck6):
    s = dck6.shape[2]
    dk = dck6[:, :2, :].reshape(N_MIX_HEADS, s).T
    return jnp.pad(dk, ((0, 0), (0, LANES - N_MIX_HEADS)))


def _device_step(x, mem, target, small, get_weights, put_grads):
    s = x.shape[0]
    mt = mem.shape[0]
    bq = 512
    cos_t, sin_t = _rope_tables(s)
    row = lambda t, l: t[l][None, :]
    saved = []
    h = x
    cb8 = small["conv_b"].reshape(DEPTH, N_DEV, 1, FF_CHUNK)
    for l in range(DEPTH):
        kind, slot = l % 2, l // 2
        wl = dict(get_weights(l, "attn", h))
        if l == 0:
            xn = _rmsnorm_fwd(h, row(small["norm_mix"], l), br=512, name=f"norm_mix_fwd{l}")
        mn = _rmsnorm_fwd(mem, row(small["norm_mem"], l), br=mt, name=f"norm_mem_fwd{l}")
        proj = _mm(xn, wl["w_in"], "nn" if kind == 0 else "nt", tm=1024, tn=384 if kind == 0 else 512, layer=0, name=f"in_proj{l}")
        kvm = _mm(mn, wl["w_mem_kv"], "nn", tm=mt, tn=512, layer=0, name=f"mem_kv{l}")
        st = dict(h=h, xn=xn, mn=mn, proj=proj, kvm=kvm, w=wl)
        if kind == 0:
            b_pad = jnp.pad(small["b_forget"][slot], (0, LANES - N_MIX_HEADS))[None, :]
            c = _forget_cumsum(proj, b_pad, name=f"forget_cumsum{l}")
            ck6 = _bias_layout(c)
            mix, lse = _attn_fwd(proj, proj, ck6, q_col=0, k_col=N_MIX_HP, v_col=2 * N_MIX_HP, n_hp=N_MIX_HP,
                                 causal=True, bq=min(s, 1024), bk=min(s, 1024), name=f"fox_fwd{l}")
            st.update(b_pad=b_pad, ck6=ck6, mix=mix, lse=lse)
        else:
            qk_r = _rope(proj, cos_t, sin_t, n_cols=2 * N_MIX_HP, out_dtype=F32, br=512, name=f"rope_fwd{l}")
            mix, lses = _dil_fwd(qk_r, proj, name=f"dil_fwd{l}")
            st.update(qk_r=qk_r, lses=lses, mix=mix)
        mo, lse_m = _attn_fwd(proj, kvm, None, q_col=QM_COL, k_col=0, v_col=N_MEM_HP, n_hp=N_MEM_HP,
                              causal=False, bq=bq, bk=mt, name=f"mem_fwd{l}")
        heads = jnp.concatenate([mix.astype(BF16), mo.astype(BF16)], axis=1)
        h1, xf = _mm(heads, wl["w_out"], "nn", tm=1024, tn=D_MODEL, res=h, layer=0, norm_gain=row(small["norm_ffn"], l),
                     name=f"out_proj{l}")
        wl.update(get_weights(l, "ffn", xf))
        u = _mm(xf, wl["w_up"], "nt", tm=1024, tn=FF_CHUNK, layer=0, chunk="b", name=f"up_proj{l}")
        a = _conv_fwd(u, wl["conv_w"], cb8[l], 0, name=f"conv_fwd{l}")
        st.update(mo=mo, lse_m=lse_m, heads=heads, h1=h1, xf=xf, u=u, a=a)
        saved.append(st)
        if l + 1 < DEPTH:
            h, xn = _mm(a, wl["w_down"], "nn", tm=512, tn=D_MODEL, res=h1, layer=0, chunk="reduce",
                        norm_gain=row(small["norm_mix"], l + 1), name=f"down_proj{l}")
        else:
            h = _mm(a, wl["w_down"], "nn", tm=1024, tn=512, res=h1, layer=0, chunk="reduce", name=f"down_proj{l}")

    dh, dhb, dg_final, loss = _loss_head(h, target, small["norm_final"][None, :], br=512, name="loss_head")
    gs = {k: [None] * DEPTH for k in ("norm_mix", "norm_mem", "norm_ffn", "conv_b")}
    gs["b_forget"] = [None] * 2
    dep = 0.0
    for l in reversed(range(DEPTH)):
        st = saved[l]
        wl = st["w"]
        gw = {}
        kind, slot = l % 2, l // 2
        da = _mm(dhb, wl["w_down"], "nt", tm=1024, tn=FF_CHUNK, layer=0, chunk="b", name=f"down_dx{l}")
        gw["w_down"] = _mm(st["a"], dhb, "tn", tm=FF_CHUNK, tn=512, out_dtype=BF16, chunk="a", name=f"down_dw{l}")
        du, dwb = _conv_bwd(st["u"], wl["conv_w"], cb8[l] + dep, da, 0, name=f"conv_bwd{l}")
        du = du.reshape(N_DEV, s, FF_CHUNK)
        dwb = dwb.reshape(N_DEV, 8, FF_CHUNK)
        gw["conv_w"] = dwb
        gs["conv_b"][l] = dwb[:, 3, :].reshape(-1)
        dxf = _mm(du, wl["w_up"], "nn", tm=512, tn=512, layer=0, chunk="reduce", name=f"up_dx{l}")
        gw["w_up"] = _mm(du, st["xf"], "tn", tm=FF_CHUNK, tn=512, out_dtype=BF16, chunk="a", name=f"up_dw{l}")
        dh1, dh1b, dgf = _rmsnorm_bwd(st["h1"], dxf, row(small["norm_ffn"], l), dh, br=512, name=f"norm_ffn_bwd{l}")
        gs["norm_ffn"][l] = dgf[0]
        dheads = _mm(dh1b, wl["w_out"], "nt", tm=1024, tn=512, layer=0, name=f"out_dx{l}")
        gw["w_out"] = _mm(st["heads"], dh1b, "tn", tm=512, tn=512, out_dtype=BF16, name=f"out_dw{l}")
        dqm, dkm, dvm = _attn_bwd(st["proj"], st["kvm"], st["mo"], dheads, st["lse_m"], None, q_col=QM_COL, k_col=0,
                                  v_col=N_MEM_HP, o_col=N_MIX_HP, n_hp=N_MEM_HP, causal=False, bq=bq, bk=mt, name=f"mem_bwd{l}")
        dkvm = jnp.concatenate([dkm, dvm], axis=1)
        gw["w_mem_kv"] = _mm(st["mn"], dkvm, "tn", tm=512, tn=512, out_dtype=BF16, name=f"mem_kv_dw{l}")
        dmn = _mm(dkvm, wl["w_mem_kv"], "nt", tm=mt, tn=512, layer=0, name=f"mem_kv_dx{l}")
        dep_early = put_grads(l, "ffn", gw)
        _, _, dgm = _rmsnorm_bwd(mem, dmn, row(small["norm_mem"], l), None, br=mt, name=f"norm_mem_bwd{l}")
        gs["norm_mem"][l] = dgm[0]
        if kind == 0:
            dq, dk, dv, dcq6, dck6 = _attn_bwd(st["proj"], st["proj"], st["mix"], dheads, st["lse"], st["ck6"] + dep_early, q_col=0,
                                               k_col=N_MIX_HP, v_col=2 * N_MIX_HP, o_col=0, n_hp=N_MIX_HP, causal=True,
                                               bq=bq, bk=bq, name=f"fox_bwd{l}")
            dz, db = _forget_cumsum_bwd(st["proj"], st["b_pad"], _bias_grad(dcq6), _bias_grad(dck6), name=f"forget_cumsum_bwd{l}")
            gs["b_forget"][slot] = db[0, :N_MIX_HEADS]
            dproj = jnp.concatenate([dq.astype(BF16), dk, dv, dqm.astype(BF16), dz], axis=1)
        else:
            dq_r, dk_r, dv = _dil_bwd(st["qk_r"], st["proj"], st["mix"], dheads, st["lses"], name=f"dil_bwd{l}")
            dq = _rope(dq_r, cos_t + dep_early, -sin_t, n_cols=N_MIX_HP, out_dtype=BF16, br=512, name=f"rope_bwd_q{l}")
            dk = _rope(dk_r, cos_t, -sin_t, n_cols=N_MIX_HP, out_dtype=BF16, br=512, name=f"rope_bwd_k{l}")
            dproj = jnp.concatenate([dq, dk, dv.astype(BF16), dqm.astype(BF16)], axis=1)
        dxn = _mm(dproj, wl["w_in"], "nt" if kind == 0 else "nn", tm=1024, tn=512, layer=0, name=f"in_dx{l}")
        if kind == 0:
            gw["w_in"] = _mm(st["xn"], dproj, "tn", tm=512, tn=384, out_dtype=BF16, name=f"in_dw{l}")
        else:
            gw["w_in"] = _mm(dproj, st["xn"], "tn", tm=512, tn=512, out_dtype=BF16, name=f"in_dw{l}")
        dh, dhb, dgx = _rmsnorm_bwd(st["h"], dxn, row(small["norm_mix"], l), dh1, br=512, name=f"norm_mix_bwd{l}")
        gs["norm_mix"][l] = dgx[0]
        dep = put_grads(l, "attn", gw)

    grads_s = {k: jnp.stack(v) for k, v in gs.items()}
    grads_s["norm_final"] = dg_final[0]
    return loss[0, 0], dh, grads_s


def kernel(x, mem, norm_mix, norm_mem, norm_ffn, w_in_fox, b_forget, w_in_dil, w_mem_kv, w_out, w_up, conv_w, conv_b, w_down, norm_final, loss_target, m_norm_mix, m_norm_mem, m_norm_ffn, m_w_in_fox, m_b_forget, m_w_in_dil, m_w_mem_kv, m_w_out, m_w_up, m_conv_w, m_conv_b, m_w_down, m_norm_final, v_norm_mix, v_norm_mem, v_norm_ffn, v_w_in_fox, v_b_forget, v_w_in_dil, v_w_mem_kv, v_w_out, v_w_up, v_conv_w, v_conv_b, v_w_down, v_norm_final):
    names = ["norm_mix", "norm_mem", "norm_ffn", "w_in_fox", "b_forget", "w_in_dil", "w_mem_kv", "w_out", "w_up", "conv_w", "conv_b",
             "w_down", "norm_final"]
    w = dict(zip(names, (norm_mix, norm_mem, norm_ffn, w_in_fox, b_forget, w_in_dil, w_mem_kv, w_out, w_up, conv_w, conv_b, w_down, norm_final)))
    m = dict(zip(names, (m_norm_mix, m_norm_mem, m_norm_ffn, m_w_in_fox, m_b_forget, m_w_in_dil, m_w_mem_kv, m_w_out, m_w_up, m_conv_w,
                         m_conv_b, m_w_down, m_norm_final)))
    v = dict(zip(names, (v_norm_mix, v_norm_mem, v_norm_ffn, v_w_in_fox, v_b_forget, v_w_in_dil, v_w_mem_kv, v_w_out, v_w_up, v_conv_w,
                         v_conv_b, v_w_down, v_norm_final)))
    big = ("w_in_fox", "w_in_dil", "w_mem_kv", "w_out", "w_up", "w_down", "conv_w")
    small_shapes = [w[k].shape for k in SMALL]
    dil_c = w_in_dil.shape[2]
    rows = {k: w[k].shape[1] for k in ("w_in_fox", "w_mem_kv", "w_out", "w_down")}

    def places(l):
        w_in_place = _rows_of(rows["w_in_fox"]) if l % 2 == 0 else _slot1
        return [w_in_place, _rows_of(rows["w_mem_kv"]), _rows_of(rows["w_out"]), _slot1, _rows_of(rows["w_down"]), _slot1]

    def full_shapes(l):
        w_in_shape = (1, D_MODEL, FOX_P) if l % 2 == 0 else (1, N_DEV, dil_c, D_MODEL)
        return [w_in_shape, (1, D_MODEL, 2 * D_MEMQ), (1, D_MODEL, D_MODEL), (1, N_DEV, FF_CHUNK, D_MODEL), (1, D_FF, D_MODEL),
                (1, N_DEV, 3, FF_CHUNK)]

    transposed = lambda t: jnp.swapaxes(t, 1, 2)
    cast = {"w_in_fox": _fox_permute(w_in_fox).astype(BF16), "w_in_dil": transposed(w_in_dil).astype(BF16),
            "w_mem_kv": w_mem_kv.astype(BF16), "w_out": w_out.astype(BF16), "w_up": transposed(w_up).astype(BF16),
            "w_down": w_down.astype(BF16), "conv_w": conv_w}
    everything = (0, 1, 2, 3, 4, 5)
    gather_groups = {l: ((0, 1, 2), (3, 4, 5)) if l == 0 else (everything,) for l in range(DEPTH)}
    scatter_groups = {l: ((1, 2, 3, 4, 5), (0,)) if l == 0 else (everything,) for l in range(DEPTH)}
    pick = lambda seq, group: [seq[i] for i in group]
    tag = lambda l, group: f"{l}" + ("" if group == everything else "_" + "".join(str(i) for i in group))

    gathers, after = {}, norm_final
    for l in range(DEPTH):
        w_in_shard = cast["w_in_fox" if l % 2 == 0 else "w_in_dil"][l // 2][None]
        shards = [w_in_shard] + [cast[k][l][None] for k in ("w_mem_kv", "w_out", "w_up", "w_down", "conv_w")]
        for group in gather_groups[l]:
            gathers[l, group] = _exchange_start("gather", pick(shards, group), pick(full_shapes(l), group), pick(places(l), group), after,
                                                name=f"weights_gather_start{tag(l, group)}")
            after = gathers[l, group]["token"]
    started = sum(g["token"][0, 0] for g in gathers.values())
    small = {k: w[k] for k in SMALL}
    small["norm_mix"] = norm_mix + started
    landed = {}

    def get_weights(l, part, h):
        group = [g for g in gather_groups[l] if (0 if part == "attn" else 3) in g][0]
        if (l, group) not in landed:
            lands = _exchange_wait("gather", gathers[l, group], pick(places(l), group), h, name=f"weights_gather_wait{tag(l, group)}")
            landed[l, group] = dict(zip(group, lands))
        got = landed[l, group]
        if part == "ffn":
            return dict(w_up=got[3], w_down=got[4].reshape(1, FF_HALF, FF_CHUNK, D_MODEL), conv_w=got[5])
        w_in = got[0] if l % 2 == 0 else got[0].reshape(1, N_DEV * dil_c, D_MODEL)
        return dict(w_in=w_in, w_mem_kv=got[1], w_out=got[2])

    scatters, pending = {}, {}

    def put_grads(l, part, g):
        pending.setdefault(l, {}).update(g)
        if part == "ffn" and len(scatter_groups[l]) == 1:
            return 0.0
        group = scatter_groups[l][0 if part == "ffn" else -1]
        have = pending[l]
        srcs = {3: lambda: have["w_up"][None], 4: lambda: have["w_down"].reshape(1, D_FF, D_MODEL), 5: lambda: have["conv_w"][None],
                1: lambda: have["w_mem_kv"][None], 2: lambda: have["w_out"][None]}
        if l % 2 == 0:
            srcs[0] = lambda: have["w_in"][None]
        else:
            srcs[0] = lambda: have["w_in"].reshape(1, N_DEV, dil_c, D_MODEL)
        shard_shapes = [cast["w_in_fox" if l % 2 == 0 else "w_in_dil"][l // 2].shape] + \
            [cast[k][l].shape for k in ("w_mem_kv", "w_out", "w_up", "w_down")] + [(8, FF_CHUNK)]
        sources = [srcs[i]() for i in group]
        scatters[l, group] = _exchange_start("scatter", sources, [(N_DEV, 1) + tuple(s) for s in pick(shard_shapes, group)],
                                             pick(places(l), group), sources[0], name=f"grads_scatter_start{tag(l, group)}")
        return scatters[l, group]["token"][0, 0]

    loss, grad_x, gs = _device_step(x[0], mem[0], loss_target[0], small, get_weights, put_grads)

    recv = {}

    def wait_scatter(l, group, after_):
        lands = _exchange_wait("scatter", scatters[l, group], pick(places(l), group), after_, name=f"grads_scatter_wait{tag(l, group)}")
        recv.setdefault(l, {}).update(zip(group, lands))

    for l in reversed(range(1, DEPTH)):
        wait_scatter(l, everything, grad_x)
    wait_scatter(0, scatter_groups[0][0], grad_x)
    s_pack = _pack([gs[k] for k in SMALL])
    (s_recv,) = _all_gather([s_pack], [(N_DEV,) + s_pack.shape], [_slot0], name="small_grads_all_gather")

    layer_tensors = ("w_in", "w_mem_kv", "w_out", "w_up", "w_down", "conv_w")
    layer_parts = lambda k: [recv[l][layer_tensors.index(k)] for l in range(DEPTH)]
    to_local = {k: (lambda t: t) for k in big}
    to_local["w_in_fox"] = _fox_permute
    to_local["w_in_dil"] = to_local["w_up"] = transposed
    from_local = {k: (lambda t: t) for k in big}
    from_local["w_in_fox"] = _fox_unpermute
    from_local["w_in_dil"] = from_local["w_up"] = transposed
    blocks = {"w_in_fox": rows["w_in_fox"], "w_in_dil": dil_c, "w_mem_kv": rows["w_mem_kv"], "w_out": rows["w_out"], "w_up": FF_CHUNK // 4,
              "w_down": rows["w_down"] // 2, "conv_w": 3}
    outs = {}

    def update(k, parts):
        f = to_local[k]
        outs[k] = [from_local[k](t) for t in _adamw(parts, f(w[k]), f(m[k]), f(v[k]), br=blocks[k], name=f"adamw_{k}")]

    update("w_in_dil", [recv[l][0] for l in range(1, DEPTH, 2)])
    update("w_up", layer_parts("w_up"))
    update("w_down", layer_parts("w_down"))
    update("conv_w", [p[:, :, :3, :] for p in layer_parts("conv_w")])
    update("w_mem_kv", layer_parts("w_mem_kv"))
    update("w_out", layer_parts("w_out"))
    wait_scatter(0, scatter_groups[0][-1], outs["w_out"][1])
    update("w_in_fox", [recv[l][0] for l in range(0, DEPTH, 2)])
    small_outs = _adamw([s_recv], _pack([w[k] for k in SMALL]), _pack([m[k] for k in SMALL]), _pack([v[k] for k in SMALL]),
                        br=s_pack.shape[1], name="adamw_small")
    res = []
    for i, os_ in enumerate(small_outs):
        d = {k: outs[k][i] for k in big}
        d.update(zip(SMALL, _unpack(os_, small_shapes)))
        res.append([d[k] for k in names])
    loss = lax.psum(loss, ("x", "y", "c"))
    return (loss, grad_x[None], *res[0], *res[1], *res[2], *res[3])
```

```python
import functools
import math

import jax
import jax.numpy as jnp
from jax import lax
from jax.experimental import pallas as pl
from jax.experimental.pallas import tpu as pltpu

F32 = jnp.float32
BF16 = jnp.bfloat16

D_MODEL = 1024
HEAD_DIM = 64
N_MIX_HEADS = 12
N_MEM_HEADS = 4
D_MIX = N_MIX_HEADS * HEAD_DIM
D_MEMQ = N_MEM_HEADS * HEAD_DIM
D_FF = 2816
DEPTH = 4
FOX_IN = 3 * D_MIX + N_MIX_HEADS + D_MEMQ
DIL_IN = 3 * D_MIX + D_MEMQ
LANES = 128
FOX_P = DIL_IN + LANES
N_MIX_HP = D_MIX // LANES
N_MEM_HP = D_MEMQ // LANES
QM_COL = 3 * N_MIX_HP
F_COL = DIL_IN // LANES
DILATED_BRANCHES = ((128, 1), (512, 4), (2048, 16))
DIL_L = 128
DIL_UNROLL_FWD = 8
DIL_UNROLL_BWD = 4
ROPE_THETA = 10000.0
NORM_EPS = 1e-6
NEG = -1e30
SCALE = HEAD_DIM ** -0.5
N_DEV = 8

ADAM_LR = 0.001
ADAM_B1 = 0.9
ADAM_B2 = 0.999
ADAM_EPS = 1e-08
ADAM_WD = 0.01
ADAM_STEP = 10

VMEM_LIMIT = 56 * 1024 * 1024
PACK_W = 1024
PACK_ROW_ALIGN = 8

MESH = pl.DeviceIdType.MESH
NT = (((1,), (1,)), ((), ()))
NN = (((1,), (0,)), ((), ()))
TN = (((0,), (0,)), ((), ()))


def _params(*sem):
    return pltpu.CompilerParams(dimension_semantics=sem, vmem_limit_bytes=VMEM_LIMIT)


def _lane_lo(shape):
    return lax.broadcasted_iota(jnp.int32, shape, len(shape) - 1) < HEAD_DIM


def _pair(lo, a, b):
    return jnp.where(lo, a, b)


def _mm(a, b, mode, *, tm, tn, name, out_dtype=F32, res=None, layer=None, chunk=None, norm_gain=None):
    lead = () if layer is None else (layer,)
    nl = (None,) * len(lead)
    bs = b.shape[len(lead):]
    dims = {"nn": NN, "nt": NT, "tn": TN}[mode]
    reduce_n = 0
    if chunk is None:
        (m, k) = a.shape[::-1] if mode == "tn" else a.shape
        n = bs[0] if mode == "nt" else bs[1]
        grid = (m // tm, n // tn)
        a_spec = pl.BlockSpec((k, tm), lambda i, j: (0, i)) if mode == "tn" else pl.BlockSpec((tm, k), lambda i, j: (i, 0))
        b_spec = pl.BlockSpec(nl + ((tn, k) if mode == "nt" else (k, tn)), lambda i, j: lead + ((j, 0) if mode == "nt" else (0, j)))
        o_spec = pl.BlockSpec((tm, tn), lambda i, j: (i, j))
        out_shape = (m, n)
    elif chunk == "b":
        (m, k) = a.shape[::-1] if mode == "tn" else a.shape
        c, nc = bs[0], (bs[1] if mode == "nt" else bs[2])
        grid = (m // tm, c)
        a_spec = pl.BlockSpec((k, tm), lambda i, j: (0, i)) if mode == "tn" else pl.BlockSpec((tm, k), lambda i, j: (i, 0))
        b_spec = pl.BlockSpec(nl + (None,) + tuple(bs[1:]), lambda i, j: lead + (j, 0, 0))
        o_spec = pl.BlockSpec((None, tm, nc), lambda i, j: (j, i, 0))
        out_shape = (c, m, nc)
    elif chunk == "a":
        assert mode == "tn"
        c, k, mc = a.shape
        n = bs[1]
        grid = (c, n // tn)
        a_spec = pl.BlockSpec((None, k, mc), lambda i, j: (i, 0, 0))
        b_spec = pl.BlockSpec(nl + (k, tn), lambda i, j: lead + (0, j))
        o_spec = pl.BlockSpec((None, mc, tn), lambda i, j: (i, 0, j))
        out_shape = (c, mc, n)
    else:
        reduce_n, m, kc = a.shape
        n = bs[1] if mode == "nt" else bs[2]
        grid = (m // tm, n // tn)
        a_spec = pl.BlockSpec((reduce_n, tm, kc), lambda i, j: (0, i, 0))
        b_spec = pl.BlockSpec(nl + ((reduce_n, tn, kc) if mode == "nt" else (reduce_n, kc, tn)),
                              lambda i, j: lead + ((0, j, 0) if mode == "nt" else (0, 0, j)))
        o_spec = pl.BlockSpec((tm, tn), lambda i, j: (i, j))
        out_shape = (m, n)

    if norm_gain is not None:
        assert chunk in (None, "reduce") and tn == n, "the RMSNorm of the result needs whole rows in a block"

    def body(*refs):
        a_ref, b_ref = refs[0], refs[1]
        o_ref = refs[-2] if norm_gain is not None else refs[-1]
        dot = lambda x, y: lax.dot_general(x.astype(BF16), y.astype(BF16), dims, preferred_element_type=F32)
        if reduce_n:
            acc = dot(a_ref[0], b_ref[0])
            for r in range(1, reduce_n):
                acc = acc + dot(a_ref[r], b_ref[r])
        else:
            acc = dot(a_ref[...], b_ref[...])
        if res is not None:
            acc = acc + refs[2][...]
        o_ref[...] = acc.astype(o_ref.dtype)
        if norm_gain is not None:
            rs = lax.rsqrt(jnp.mean(acc * acc, axis=-1, keepdims=True) + NORM_EPS)
            refs[-1][...] = (acc * rs * refs[3][...]).astype(BF16)

    ins = [a, b] + ([res] if res is not None else [])
    specs = [a_spec, b_spec] + ([o_spec] if res is not None else [])
    out_shapes, out_specs = jax.ShapeDtypeStruct(out_shape, out_dtype), o_spec
    if norm_gain is not None:
        assert res is not None
        ins.append(norm_gain)
        specs.append(pl.BlockSpec((1, n), lambda i, j: (0, 0)))
        out_shapes, out_specs = (out_shapes, jax.ShapeDtypeStruct(out_shape, BF16)), (o_spec, o_spec)
    return pl.pallas_call(body, out_shape=out_shapes, grid=grid, in_specs=specs, out_specs=out_specs,
                          compiler_params=_params("parallel", "parallel"), name=name)(*ins)


def _rmsnorm_fwd(x, g, *, br, name):
    r, d = x.shape

    def body(x_ref, g_ref, o_ref):
        xf = x_ref[...]
        rs = lax.rsqrt(jnp.mean(xf * xf, axis=-1, keepdims=True) + NORM_EPS)
        o_ref[...] = (xf * rs * g_ref[...]).astype(BF16)

    return pl.pallas_call(body, out_shape=jax.ShapeDtypeStruct((r, d), BF16), grid=(r // br,),
                          in_specs=[pl.BlockSpec((br, d), lambda i: (i, 0)), pl.BlockSpec((1, d), lambda i: (0, 0))],
                          out_specs=pl.BlockSpec((br, d), lambda i: (i, 0)), compiler_params=_params("parallel"), name=name)(x, g)


def _rms_bwd_math(x, dy, g):
    d = x.shape[-1]
    rs = lax.rsqrt(jnp.mean(x * x, axis=-1, keepdims=True) + NORM_EPS)
    gy = dy * g
    proj = jnp.sum(x * gy, axis=-1, keepdims=True) * (1.0 / d)
    dx = rs * gy - x * (rs * rs * rs) * proj
    dg = jnp.sum(dy * (x * rs), axis=0, keepdims=True)
    return dx, dg


def _rmsnorm_bwd(x, dy, g, res, *, br, name):
    r, d = x.shape
    has_res = res is not None

    def body(*refs):
        x_ref, dy_ref, g_ref = refs[:3]
        dx_ref, dxb_ref, dg_ref = refs[-3:]
        dx, dg = _rms_bwd_math(x_ref[...], dy_ref[...], g_ref[...])
        if has_res:
            dx = dx + refs[3][...]
        dx_ref[...] = dx
        dxb_ref[...] = dx.astype(BF16)

        @pl.when(pl.program_id(0) == 0)
        def _():
            dg_ref[...] = jnp.zeros_like(dg_ref)

        dg_ref[0:1, :] += dg

    row = pl.BlockSpec((br, d), lambda i: (i, 0))
    ins = [x, dy, g] + ([res] if has_res else [])
    specs = [row, row, pl.BlockSpec((1, d), lambda i: (0, 0))] + ([row] if has_res else [])
    return pl.pallas_call(
        body, out_shape=(jax.ShapeDtypeStruct((r, d), F32), jax.ShapeDtypeStruct((r, d), BF16), jax.ShapeDtypeStruct((8, d), F32)),
        grid=(r // br,), in_specs=specs, out_specs=(row, row, pl.BlockSpec((8, d), lambda i: (0, 0))),
        compiler_params=_params("arbitrary"), name=name)(*ins)


def _loss_head(h, target, g, *, br, name):
    r, d = h.shape

    def body(x_ref, t_ref, g_ref, dx_ref, dxb_ref, dg_ref, loss_ref):
        x = x_ref[...]
        gg = g_ref[...]
        rs = lax.rsqrt(jnp.mean(x * x, axis=-1, keepdims=True) + NORM_EPS)
        err = x * rs * gg - t_ref[...]
        part = jnp.sum(jnp.sum(err * err, axis=1, keepdims=True), axis=0, keepdims=True) * (0.5 / d)
        dx, dg = _rms_bwd_math(x, err * (1.0 / d), gg)
        dx_ref[...] = dx
        dxb_ref[...] = dx.astype(BF16)

        @pl.when(pl.program_id(0) == 0)
        def _():
            dg_ref[...] = jnp.zeros_like(dg_ref)
            loss_ref[...] = jnp.zeros_like(loss_ref)

        dg_ref[0:1, :] += dg
        loss_ref[...] += jnp.broadcast_to(part, loss_ref.shape)

    row = pl.BlockSpec((br, d), lambda i: (i, 0))
    return pl.pallas_call(
        body, out_shape=(jax.ShapeDtypeStruct((r, d), F32), jax.ShapeDtypeStruct((r, d), BF16),
                         jax.ShapeDtypeStruct((8, d), F32), jax.ShapeDtypeStruct((8, LANES), F32)),
        grid=(r // br,), in_specs=[row, row, pl.BlockSpec((1, d), lambda i: (0, 0))],
        out_specs=(row, row, pl.BlockSpec((8, d), lambda i: (0, 0)), pl.BlockSpec((8, LANES), lambda i: (0, 0))),
        compiler_params=_params("arbitrary"), name=name)(h, target, g)


def _split3(x):
    hi = x.astype(BF16)
    r1 = x - hi.astype(F32)
    mid = r1.astype(BF16)
    lo = (r1 - mid.astype(F32)).astype(BF16)
    return hi, mid, lo


def _tri_sum(tri, x):
    hi, mid, lo = _split3(x)
    dot = lambda t: jnp.dot(tri, t, preferred_element_type=F32)
    return dot(hi) + dot(mid) + dot(lo)


def _forget_cumsum(proj, b_pad, *, name):
    s = proj.shape[0]
    blk = LANES

    def body(f_ref, b_ref, c_ref):
        ri = lax.broadcasted_iota(jnp.int32, (blk, blk), 0)
        ci = lax.broadcasted_iota(jnp.int32, (blk, blk), 1)
        tri = (ci <= ri).astype(BF16)
        bias = b_ref[...]

        def step(t, carry):
            rows = pl.ds(pl.multiple_of(t * blk, blk), blk)
            z = f_ref[rows, :] + bias
            lf = jnp.minimum(z, 0.0) - jnp.log(1.0 + jnp.exp(-jnp.abs(z)))
            cs = _tri_sum(tri, lf) + carry
            c_ref[rows, :] = cs
            return cs[blk - 1:blk, :]

        lax.fori_loop(0, s // blk, step, jnp.zeros((1, blk), F32))

    return pl.pallas_call(body, out_shape=jax.ShapeDtypeStruct((s, LANES), F32), grid=(1,),
                          in_specs=[pl.BlockSpec((s, LANES), lambda i: (0, F_COL)), pl.BlockSpec((1, LANES), lambda i: (0, 0))],
                          out_specs=pl.BlockSpec((s, LANES), lambda i: (0, 0)), compiler_params=_params("arbitrary"), name=name)(proj, b_pad)


def _forget_cumsum_bwd(proj, b_pad, dcq, dck, *, name):
    s = proj.shape[0]
    blk = LANES
    nblk = s // blk

    def body(f_ref, b_ref, dcq_ref, dck_ref, dz_ref, db_ref):
        ri = lax.broadcasted_iota(jnp.int32, (blk, blk), 0)
        ci = lax.broadcasted_iota(jnp.int32, (blk, blk), 1)
        triu = (ci >= ri).astype(BF16)
        bias = b_ref[...]

        def step(t, carry):
            tail, dbs = carry
            rows = pl.ds(pl.multiple_of((nblk - 1 - t) * blk, blk), blk)
            dc = dcq_ref[rows, :] - dck_ref[rows, :]
            dlf = _tri_sum(triu, dc) + tail
            z = f_ref[rows, :] + bias
            e = jnp.exp(-jnp.abs(z))
            sig_neg = jnp.where(z >= 0.0, e, 1.0) / (1.0 + e)
            dz = dlf * sig_neg
            dz_ref[rows, :] = dz.astype(BF16)
            return dlf[0:1, :], dbs + jnp.sum(dz, axis=0, keepdims=True)

        _, dbs = lax.fori_loop(0, nblk, step, (jnp.zeros((1, blk), F32), jnp.zeros((1, blk), F32)))
        db_ref[...] = jnp.broadcast_to(dbs, db_ref.shape)

    full = pl.BlockSpec((s, LANES), lambda i: (0, 0))
    return pl.pallas_call(body, out_shape=(jax.ShapeDtypeStruct((s, LANES), BF16), jax.ShapeDtypeStruct((8, LANES), F32)), grid=(1,),
                          in_specs=[pl.BlockSpec((s, LANES), lambda i: (0, F_COL)), pl.BlockSpec((1, LANES), lambda i: (0, 0)), full, full],
                          out_specs=(full, pl.BlockSpec((8, LANES), lambda i: (0, 0))), compiler_params=_params("arbitrary"), name=name)(proj, b_pad, dcq, dck)


def _attn_fwd(q_arr, kv_arr, ck6, *, q_col, k_col, v_col, n_hp, causal, bq, bk, name, heads=None, heads_col=0):
    s = q_arr.shape[0]
    skv = kv_arr.shape[0]
    bias = ck6 is not None
    nq = s // bq
    assert not causal or bq == bk

    def body(*refs):
        q_ref, k_ref, v_ref = refs[:3]
        ck_ref = refs[3] if bias else None
        o_ref, lse_ref, heads_ref = refs[-3:]
        i = pl.program_id(1)
        lo = _lane_lo((bq, LANES))
        q = q_ref[...] * SCALE
        qh = (jnp.where(lo, q, 0.0).astype(BF16), jnp.where(lo, 0.0, q).astype(BF16))

        def block(j, carry, diagonal):
            ks = pl.ds(pl.multiple_of(j * bk, bk), bk)
            k = k_ref[ks, :].astype(BF16)
            v = v_ref[ks, :].astype(BF16)
            if diagonal:
                ok = lax.broadcasted_iota(jnp.int32, (bq, bk), 1) <= lax.broadcasted_iota(jnp.int32, (bq, bk), 0)
            out = []
            for h in range(2):
                m, l, acc = carry[3 * h:3 * h + 3]
                sc = lax.dot_general(qh[h], k, NT, preferred_element_type=F32)
                if bias:
                    sc = sc - ck_ref[0, h:h + 1, ks]
                if diagonal:
                    sc = jnp.where(ok, sc, NEG)
                mn = jnp.maximum(m, jnp.max(sc, axis=1, keepdims=True))
                p = jnp.exp(sc - mn)
                al = jnp.exp(m - mn)
                out += [mn, al * l + jnp.sum(p, axis=1, keepdims=True), al * acc + jnp.dot(p.astype(BF16), v, preferred_element_type=F32)]
            return tuple(out)

        col = lambda v_: jnp.full((bq, 1), v_, F32)
        init = (col(NEG), col(0.0), jnp.zeros((bq, LANES), F32)) * 2
        n_full = i if causal else skv // bk
        carry = lax.fori_loop(0, n_full, functools.partial(block, diagonal=False), init)
        if causal:
            carry = block(i, carry, True)
        m0, l0, a0, m1, l1, a1 = carry
        out = _pair(lo, a0 / l0, a1 / l1)
        o_ref[...] = out
        lse_ref[0] = _pair(lo, m0 + jnp.log(l0), m1 + jnp.log(l1))
        heads_ref[...] = out.astype(BF16)

    specs = [pl.BlockSpec((bq, LANES), lambda h, i: (i, q_col + h)),
             pl.BlockSpec((skv, LANES), lambda h, i: (0, k_col + h)),
             pl.BlockSpec((skv, LANES), lambda h, i: (0, v_col + h))]
    ins = [q_arr, kv_arr, kv_arr]
    if bias:
        specs += [pl.BlockSpec((1, 8, skv), lambda h, i: (h, 0, 0))]
        ins += [ck6]
    aliases = {}
    if heads is not None:
        aliases = {len(ins): 2}
        specs += [pl.BlockSpec(memory_space=pl.ANY)]
        ins += [heads]
    return pl.pallas_call(
        body, out_shape=(jax.ShapeDtypeStruct((s, n_hp * LANES), F32), jax.ShapeDtypeStruct((n_hp, s, LANES), F32),
                         jax.ShapeDtypeStruct((s, D_MODEL), BF16)),
        grid=(n_hp, nq), in_specs=specs,
        out_specs=(pl.BlockSpec((bq, LANES), lambda h, i: (i, h)), pl.BlockSpec((1, bq, LANES), lambda h, i: (h, i, 0)),
                   pl.BlockSpec((bq, LANES), lambda h, i: (i, heads_col + h))),
        input_output_aliases=aliases, compiler_params=_params("parallel", "parallel"), name=name)(*ins)


def _attn_bwd(q_arr, kv_arr, o_arr, do_arr, lse, ck6, *, q_col, k_col, v_col, o_col, n_hp, causal, bq, bk, name):
    s = q_arr.shape[0]
    skv = kv_arr.shape[0]
    bias = ck6 is not None
    nq = s // bq
    assert not causal or bq == bk

    def body(*refs):
        q_ref, k_ref, v_ref, o_ref, do_ref, lse_ref = refs[:6]
        if bias:
            ck_ref = refs[6]
            dq_ref, dk_ref, dv_ref, dcq_ref, dck_ref = refs[-5:]
        else:
            dq_ref, dk_ref, dv_ref = refs[-3:]
        j = pl.program_id(1)
        lo_q = _lane_lo((bq, LANES))
        lo_k = _lane_lo((bk, LANES))
        k = k_ref[...]
        v = v_ref[...].astype(BF16)
        kb = k.astype(BF16)
        kh = (jnp.where(lo_k, k, 0.0).astype(BF16), jnp.where(lo_k, 0.0, k).astype(BF16))
        if bias:
            pick_k = [(lax.broadcasted_iota(jnp.int32, (8, bk), 0) == h).astype(BF16) for h in range(2)]
            pick_q = [(lax.broadcasted_iota(jnp.int32, (8, bq), 0) == h).astype(BF16) for h in range(2)]

        @pl.when(j == 0)
        def _():
            dq_ref[...] = jnp.zeros_like(dq_ref)
            if bias:
                dcq_ref[...] = jnp.zeros_like(dcq_ref)

        def block(i, carry, diagonal):
            dk_acc, dv_acc, cs = carry
            qs = pl.ds(pl.multiple_of(i * bq, bq), bq)
            q = q_ref[qs, :] * SCALE
            do = do_ref[qs, :]
            dd = do * o_ref[qs, :]
            lse_i = lse_ref[0, qs, :]
            qh = (jnp.where(lo_q, q, 0.0).astype(BF16), jnp.where(lo_q, 0.0, q).astype(BF16))
            doh = (jnp.where(lo_q, do, 0.0).astype(BF16), jnp.where(lo_q, 0.0, do).astype(BF16))
            dh = (jnp.sum(jnp.where(lo_q, dd, 0.0), axis=1, keepdims=True), jnp.sum(jnp.where(lo_q, 0.0, dd), axis=1, keepdims=True))
            if diagonal:
                ok = lax.broadcasted_iota(jnp.int32, (bq, bk), 1) <= lax.broadcasted_iota(jnp.int32, (bq, bk), 0)
            dq_blk = None
            rs = None
            for h in range(2):
                sc = lax.dot_general(qh[h], kb, NT, preferred_element_type=F32)
                if bias:
                    sc = sc - ck_ref[0, h:h + 1, :]
                if diagonal:
                    sc = jnp.where(ok, sc, NEG)
                p = jnp.exp(sc - lse_i[:, h * HEAD_DIM:h * HEAD_DIM + 1])
                ds = p * (lax.dot_general(doh[h], v, NT, preferred_element_type=F32) - dh[h])
                dsb = ds.astype(BF16)
                dv_acc = dv_acc + lax.dot_general(p.astype(BF16), doh[h], TN, preferred_element_type=F32)
                dk_acc = dk_acc + lax.dot_general(dsb, qh[h], TN, preferred_element_type=F32)
                part = jnp.dot(dsb, kh[h], preferred_element_type=F32)
                dq_blk = part if dq_blk is None else dq_blk + part
                if bias:
                    cs = cs + jnp.dot(pick_q[h], dsb, preferred_element_type=F32)
                    row_sums = lax.dot_general(pick_k[h], dsb, NT, preferred_element_type=F32)
                    rs = row_sums if rs is None else rs + row_sums
            dq_ref[qs, :] += dq_blk * SCALE
            if bias:
                dcq_ref[0, :, qs] += rs
            return dk_acc, dv_acc, cs

        carry = (jnp.zeros((bk, LANES), F32), jnp.zeros((bk, LANES), F32), jnp.zeros((8, bk), F32))
        if causal:
            carry = block(j, carry, True)
        dk_acc, dv_acc, cs = lax.fori_loop(j + 1 if causal else 0, nq, functools.partial(block, diagonal=False), carry)
        dk_ref[...] = dk_acc.astype(BF16)
        dv_ref[...] = dv_acc.astype(BF16)
        if bias:
            dck_ref[0] = cs

    full_q = lambda c: pl.BlockSpec((s, LANES), lambda h, j: (0, c + h))
    specs = [full_q(q_col),
             pl.BlockSpec((bk, LANES), lambda h, j: (j, k_col + h)),
             pl.BlockSpec((bk, LANES), lambda h, j: (j, v_col + h)),
             full_q(0), full_q(o_col),
             pl.BlockSpec((1, s, LANES), lambda h, j: (h, 0, 0))]
    ins = [q_arr, kv_arr, kv_arr, o_arr, do_arr, lse]
    out_shape = [jax.ShapeDtypeStruct((s, n_hp * LANES), F32), jax.ShapeDtypeStruct((skv, n_hp * LANES), BF16),
                 jax.ShapeDtypeStruct((skv, n_hp * LANES), BF16)]
    out_specs = [full_q(0), pl.BlockSpec((bk, LANES), lambda h, j: (j, h)), pl.BlockSpec((bk, LANES), lambda h, j: (j, h))]
    if bias:
        specs += [pl.BlockSpec((1, 8, bk), lambda h, j: (h, 0, j))]
        ins += [ck6]
        out_shape += [jax.ShapeDtypeStruct((n_hp, 8, s), F32), jax.ShapeDtypeStruct((n_hp, 8, skv), F32)]
        out_specs += [pl.BlockSpec((1, 8, s), lambda h, j: (h, 0, 0)), pl.BlockSpec((1, 8, bk), lambda h, j: (h, 0, j))]
    return pl.pallas_call(body, out_shape=tuple(out_shape), grid=(n_hp, skv // bk), in_specs=specs, out_specs=tuple(out_specs),
                          compiler_params=_params("parallel", "arbitrary"), name=name)(*ins)


def _rope_tables(s):
    inv = 1.0 / (ROPE_THETA ** (jnp.arange(0, HEAD_DIM, 2, dtype=F32) / HEAD_DIM))
    ang = jnp.arange(s, dtype=F32)[:, None] * inv[None, :]
    cos, sin = jnp.cos(ang), jnp.sin(ang)
    return jnp.tile(cos, (1, 4)), jnp.concatenate([-sin, sin, -sin, sin], axis=1)


def _rope(x_arr, cos_t, sin_t, *, n_cols, out_dtype, br, name):
    s = x_arr.shape[0]

    def body(x_ref, c_ref, s_ref, o_ref):
        cos, sin = c_ref[...], s_ref[...]
        first = (lax.broadcasted_iota(jnp.int32, (br, LANES), 1) % HEAD_DIM) < (HEAD_DIM // 2)
        for j in range(n_cols):
            lanes = slice(j * LANES, (j + 1) * LANES)
            x = x_ref[:, lanes].astype(F32)
            swapped = jnp.where(first, pltpu.roll(x, LANES - HEAD_DIM // 2, 1), pltpu.roll(x, HEAD_DIM // 2, 1))
            o_ref[:, lanes] = (x * cos + swapped * sin).astype(o_ref.dtype)

    tab = pl.BlockSpec((br, LANES), lambda i: (i, 0))
    blk = pl.BlockSpec((br, n_cols * LANES), lambda i: (i, 0))
    return pl.pallas_call(body, out_shape=jax.ShapeDtypeStruct((s, n_cols * LANES), out_dtype), grid=(s // br,),
                          in_specs=[blk, tab, tab], out_specs=blk, compiler_params=_params("parallel"), name=name)(x_arr, cos_t, sin_t)


def _stack_heads(x):
    lo = _lane_lo(x.shape)
    return jnp.concatenate([jnp.where(lo, x, 0.0), jnp.where(lo, 0.0, x)], axis=0).astype(BF16)


def _unstack_heads(x):
    return jnp.where(_lane_lo((DIL_L, LANES)), x[:DIL_L], x[DIL_L:])


def _dil_scores(q_ref, k_ref, cur, prev, has_prev):
    qs = _stack_heads(q_ref[cur, :] * SCALE)
    kk = jnp.concatenate([k_ref[prev, :], k_ref[cur, :]], axis=0).astype(BF16)
    a = lax.broadcasted_iota(jnp.int32, (2 * DIL_L, 2 * DIL_L), 0) & (DIL_L - 1)
    c = lax.broadcasted_iota(jnp.int32, (2 * DIL_L, 2 * DIL_L), 1)
    ok = ((c < DIL_L) & (c >= a) & has_prev) | ((c >= DIL_L) & (c - DIL_L <= a))
    return qs, kk, jnp.where(ok, lax.dot_general(qs, kk, NT, preferred_element_type=F32), NEG)


def _dil_rows(t, dil):
    r, m = t % dil, t // dil
    start = m * (DIL_L * dil) + r
    prev = jnp.maximum(start - DIL_L * dil, 0)
    return pl.ds(start, DIL_L, stride=dil), pl.ds(prev, DIL_L, stride=dil), m > 0


def _softmax3(a, b, c):
    m = jnp.maximum(jnp.maximum(a, b), c)
    ea, eb, ec = jnp.exp(a - m), jnp.exp(b - m), jnp.exp(c - m)
    den = ea + eb + ec
    inv = 1.0 / den
    return ea * inv, eb * inv, ec * inv, m + jnp.log(den)


def _dil_fwd(qk_r, proj, *, name):
    s = qk_r.shape[0]
    nsub = s // DIL_L
    mb = 512

    def body(q_ref, k_ref, v_ref, mix_ref, l1_ref, l2_ref, l3_ref, heads_ref, o1_scr, o2_scr, o3_scr):
        for (_, dil), o_scr, l_ref in zip(DILATED_BRANCHES, (o1_scr, o2_scr, o3_scr), (l1_ref, l2_ref, l3_ref)):
            def step(t, carry, dil=dil, o_scr=o_scr, l_ref=l_ref):
                cur, prev, has_prev = _dil_rows(t, dil)
                _, _, sc = _dil_scores(q_ref, k_ref, cur, prev, has_prev)
                vv = jnp.concatenate([v_ref[prev, :], v_ref[cur, :]], axis=0).astype(BF16)
                m = jnp.max(sc, axis=1, keepdims=True)
                e = jnp.exp(sc - m)
                den = jnp.sum(e, axis=1, keepdims=True)
                o = jnp.dot((e * (1.0 / den)).astype(BF16), vv, preferred_element_type=F32)
                o_scr[cur, :] = _unstack_heads(o)
                l_ref[cur, :] = _unstack_heads(jnp.broadcast_to(m + jnp.log(den), (2 * DIL_L, LANES)))
                return carry

            lax.fori_loop(0, nsub, step, 0, unroll=DIL_UNROLL_FWD)

        def merge(i, carry):
            rows = pl.ds(pl.multiple_of(i * mb, mb), mb)
            wa, wb, wc, _ = _softmax3(l1_ref[rows, :], l2_ref[rows, :], l3_ref[rows, :])
            mix = wa * o1_scr[rows, :] + wb * o2_scr[rows, :] + wc * o3_scr[rows, :]
            mix_ref[rows, :] = mix
            heads_ref[rows, :] = mix.astype(BF16)
            return carry

        lax.fori_loop(0, s // mb, merge, 0)

    col = lambda arr_col: pl.BlockSpec((s, LANES), lambda h: (0, arr_col + h))
    shp = jax.ShapeDtypeStruct((s, D_MIX), F32)
    mix, l1, l2, l3, heads = pl.pallas_call(
        body, out_shape=(shp, shp, shp, shp, jax.ShapeDtypeStruct((s, D_MODEL), BF16)), grid=(N_MIX_HP,),
        in_specs=[col(0), col(N_MIX_HP), col(2 * N_MIX_HP)],
        out_specs=(col(0),) * 5, scratch_shapes=[pltpu.VMEM((s, LANES), F32)] * 3,
        compiler_params=_params("parallel"), name=name)(qk_r, qk_r, proj)
    return mix, (l1, l2, l3), heads


def _dil_bwd(qk_r, proj, mix, dheads, lses, *, name):
    s = qk_r.shape[0]
    nsub = s // DIL_L
    mb = 512

    def body(q_ref, k_ref, v_ref, mix_ref, dm_ref, l1_ref, l2_ref, l3_ref, dq_ref, dk_ref, dv_ref, lt_scr, dd_scr):
        lo = _lane_lo((DIL_L, LANES))
        lo_m = _lane_lo((mb, LANES))

        def prep(i, carry):
            rows = pl.ds(pl.multiple_of(i * mb, mb), mb)
            _, _, _, lt = _softmax3(l1_ref[rows, :], l2_ref[rows, :], l3_ref[rows, :])
            lt_scr[rows, :] = lt
            dd = dm_ref[rows, :] * mix_ref[rows, :]
            dd_scr[rows, :] = _pair(lo_m, jnp.sum(jnp.where(lo_m, dd, 0.0), axis=1, keepdims=True),
                                    jnp.sum(jnp.where(lo_m, 0.0, dd), axis=1, keepdims=True))
            zero = jnp.zeros((mb, LANES), F32)
            dq_ref[rows, :] = zero
            dk_ref[rows, :] = zero
            dv_ref[rows, :] = zero
            return carry

        lax.fori_loop(0, s // mb, prep, 0)

        for (_, dil), l_ref in zip(DILATED_BRANCHES, (l1_ref, l2_ref, l3_ref)):
            def step(t, carry, dil=dil, l_ref=l_ref):
                cur, prev, has_prev = _dil_rows(t, dil)
                qs, kk, sc = _dil_scores(q_ref, k_ref, cur, prev, has_prev)
                vv = jnp.concatenate([v_ref[prev, :], v_ref[cur, :]], axis=0).astype(BF16)
                lg = l_ref[cur, :]
                w = jnp.exp(lg - lt_scr[cur, :])
                wd = w * dd_scr[cur, :]
                column = lambda x: jnp.concatenate([x[:, 0:1], x[:, HEAD_DIM:HEAD_DIM + 1]], axis=0)
                dos = _stack_heads(w * dm_ref[cur, :])
                p = jnp.exp(sc - column(lg))
                ds = (p * (lax.dot_general(dos, vv, NT, preferred_element_type=F32) - column(wd))).astype(BF16)
                dq_ref[cur, :] += _unstack_heads(jnp.dot(ds, kk, preferred_element_type=F32)) * SCALE
                dkk = lax.dot_general(ds, qs, TN, preferred_element_type=F32)
                dvv = lax.dot_general(p.astype(BF16), dos, TN, preferred_element_type=F32)
                dk_ref[cur, :] += dkk[DIL_L:]
                dv_ref[cur, :] += dvv[DIL_L:]
                dk_ref[prev, :] += dkk[:DIL_L]
                dv_ref[prev, :] += dvv[:DIL_L]
                return carry

            lax.fori_loop(0, nsub, step, 0, unroll=DIL_UNROLL_BWD)

    col = lambda arr_col: pl.BlockSpec((s, LANES), lambda h: (0, arr_col + h))
    shp = jax.ShapeDtypeStruct((s, D_MIX), F32)
    return pl.pallas_call(
        body, out_shape=(shp, shp, shp), grid=(N_MIX_HP,),
        in_specs=[col(0), col(N_MIX_HP), col(2 * N_MIX_HP), col(0), col(0), col(0), col(0), col(0)], out_specs=(col(0),) * 3,
        scratch_shapes=[pltpu.VMEM((s, LANES), F32)] * 2,
        compiler_params=_params("parallel"), name=name)(qk_r, qk_r, proj, mix, dheads, *lses)


CONV_BR = 512
FF_CHUNK = 2 * D_FF // N_DEV
FF_HALF = N_DEV // 2
HALO = 8


def _shift_down(x, halo, k):
    row = lax.broadcasted_iota(jnp.int32, x.shape, 0)
    y = pltpu.roll(x, k, 0)
    for r in range(k):
        y = jnp.where(row == r, halo[HALO - k + r:HALO - k + r + 1, :], y)
    return y


def _shift_up(x, halo, k):
    n = x.shape[0]
    row = lax.broadcasted_iota(jnp.int32, x.shape, 0)
    y = pltpu.roll(x, n - k, 0)
    for r in range(k):
        y = jnp.where(row == n - k + r, halo[r:r + 1, :], y)
    return y


def _conv_vals(u, halo, w, b):
    s1 = _shift_down(u, halo, 1)
    s2 = _shift_down(u, halo, 2)
    return b + w[0:1, :] * s2 + w[1:2, :] * s1 + w[2:3, :] * u, s1, s2


def _conv_in_specs(order, layer):
    rc = (lambda i, j: (i, j)) if order == "rc" else (lambda j, i: (i, j))
    per = CONV_BR // HALO
    main = lambda off: pl.BlockSpec((None, CONV_BR, FF_CHUNK), lambda *g: (off + rc(*g)[1], rc(*g)[0], 0))
    halo = lambda off: pl.BlockSpec((None, HALO, FF_CHUNK), lambda *g: (off + rc(*g)[1], jnp.maximum(rc(*g)[0] * per - 1, 0), 0))
    wspec = lambda off: pl.BlockSpec((None, None, 3, FF_CHUNK), lambda *g: (layer, off + rc(*g)[1], 0, 0))
    bspec = lambda off: pl.BlockSpec((None, 1, FF_CHUNK), lambda *g: (off + rc(*g)[1], 0, 0))
    return [main(0), halo(0), main(FF_HALF), halo(FF_HALF), wspec(0), wspec(FF_HALF), bspec(0), bspec(FF_HALF)]


def _conv_fwd(u, cw, cb, layer, *, name):
    s = u.shape[1]

    def body(uv_ref, hv_ref, ug_ref, hg_ref, wv_ref, wg_ref, bv_ref, bg_ref, o_ref):
        first = pl.program_id(0) == 0
        hv = jnp.where(first, 0.0, hv_ref[...])
        hg = jnp.where(first, 0.0, hg_ref[...])
        val, _, _ = _conv_vals(uv_ref[...], hv, wv_ref[...], bv_ref[...])
        gate, _, _ = _conv_vals(ug_ref[...], hg, wg_ref[...], bg_ref[...])
        o_ref[...] = (gate / (1.0 + jnp.exp(-gate)) * val).astype(BF16)

    return pl.pallas_call(body, out_shape=jax.ShapeDtypeStruct((FF_HALF, s, FF_CHUNK), BF16), grid=(s // CONV_BR, FF_HALF),
                          in_specs=_conv_in_specs("rc", layer), out_specs=pl.BlockSpec((None, CONV_BR, FF_CHUNK), lambda i, j: (j, i, 0)),
                          compiler_params=_params("parallel", "parallel"), name=name)(u, u, u, u, cw, cw, cb, cb)


def _swiglu_bwd(val, gate, da):
    sg = 1.0 / (1.0 + jnp.exp(-gate))
    return da * (gate * sg), da * val * (sg * (1.0 + gate * (1.0 - sg)))


def _conv_bwd(u, cw, cb, da, layer, *, name):
    s = u.shape[1]
    nrow = s // CONV_BR
    per = CONV_BR // HALO

    def body(uv_ref, hv_ref, ug_ref, hg_ref, wv_ref, wg_ref, bv_ref, bg_ref, da_ref, nv_ref, ng_ref, nda_ref, du_ref, dwb_ref):
        i = pl.program_id(1)
        first, last = i == 0, i == nrow - 1
        hv = jnp.where(first, 0.0, hv_ref[...])
        hg = jnp.where(first, 0.0, hg_ref[...])
        uv, ug = uv_ref[...], ug_ref[...]
        wv, wg, bv, bg = wv_ref[...], wg_ref[...], bv_ref[...], bg_ref[...]
        val, v1, v2 = _conv_vals(uv, hv, wv, bv)
        gate, g1, g2 = _conv_vals(ug, hg, wg, bg)
        dval, dgate = _swiglu_bwd(val, gate, da_ref[...])
        val_n, _, _ = _conv_vals(nv_ref[...], uv[CONV_BR - HALO:, :], wv, bv)
        gate_n, _, _ = _conv_vals(ng_ref[...], ug[CONV_BR - HALO:, :], wg, bg)
        dval_n, dgate_n = _swiglu_bwd(val_n, gate_n, nda_ref[...])
        dval_n = jnp.where(last, 0.0, dval_n)
        dgate_n = jnp.where(last, 0.0, dgate_n)
        back = lambda dc, dc_n, w: w[2:3, :] * dc + w[1:2, :] * _shift_up(dc, dc_n, 1) + w[0:1, :] * _shift_up(dc, dc_n, 2)
        du_ref[0] = back(dval, dval_n, wv).astype(BF16)
        du_ref[1] = back(dgate, dgate_n, wg).astype(BF16)

        @pl.when(first)
        def _():
            dwb_ref[...] = jnp.zeros_like(dwb_ref)

        cs = lambda t: jnp.sum(t, axis=0, keepdims=True)
        r8 = lax.broadcasted_iota(jnp.int32, (8, FF_CHUNK), 0)
        rows4 = lambda a, b, c, d: jnp.where(r8 == 0, a, jnp.where(r8 == 1, b, jnp.where(r8 == 2, c, jnp.where(r8 == 3, d, 0.0))))
        dwb_ref[0] += rows4(cs(dval * v2), cs(dval * v1), cs(dval * uv), cs(dval))
        dwb_ref[1] += rows4(cs(dgate * g2), cs(dgate * g1), cs(dgate * ug), cs(dgate))

    nxt = lambda off: pl.BlockSpec((None, HALO, FF_CHUNK), lambda j, i: (off + j, jnp.minimum((i + 1) * per, nrow * per - 1), 0))
    specs = _conv_in_specs("cr", layer) + [pl.BlockSpec((None, CONV_BR, FF_CHUNK), lambda j, i: (j, i, 0)), nxt(0), nxt(FF_HALF), nxt(0)]
    return pl.pallas_call(
        body, out_shape=(jax.ShapeDtypeStruct((2, FF_HALF, s, FF_CHUNK), BF16), jax.ShapeDtypeStruct((2, FF_HALF, 8, FF_CHUNK), F32)),
        grid=(FF_HALF, nrow), in_specs=specs,
        out_specs=(pl.BlockSpec((2, None, CONV_BR, FF_CHUNK), lambda j, i: (0, j, i, 0)),
                   pl.BlockSpec((2, None, 8, FF_CHUNK), lambda j, i: (0, j, 0, 0))),
        compiler_params=_params("parallel", "arbitrary"), name=name)(u, u, u, u, cw, cw, cb, cb, da, u, u, da)


def _rows_of(r):
    return lambda ref, idx: ref.at[:, pl.ds(idx * r, r), :]


def _slot1(ref, idx):
    return ref.at[:, idx]


def _slot0(ref, idx):
    return ref.at[idx]


def _all_gather(shards, full_shapes, places, *, name):
    n = len(shards)

    def body(*refs):
        ins, outs = refs[:n], refs[n:2 * n]
        send_sems, recv_sems, local_sems = refs[2 * n:]
        mx, my, mc = lax.axis_index("x"), lax.axis_index("y"), lax.axis_index("c")
        me, sibling = (mx, my, mc), (mx, my, 1 - mc)
        chips = [(1 - mx, my), (mx, 1 - my), (1 - mx, 1 - my)]

        def win(t, px, py, pc):
            return places[t](outs[t], 4 * px + 2 * py + pc)

        def copy(t, k, block, to, src=None):
            return pltpu.make_async_remote_copy(src_ref=win(t, *block) if src is None else src, dst_ref=win(t, *block),
                                                send_sem=send_sems.at[t, k], recv_sem=recv_sems.at[t, k], device_id=to, device_id_type=MESH)

        mine = [pltpu.make_async_copy(ins[t], win(t, *me), local_sems.at[t]) for t in range(n)]
        for cp in mine:
            cp.start()
        first = []
        for t in range(n):
            first += [copy(t, 0, me, sibling, src=ins[t])] + [copy(t, 1 + j, me, (*chip, mc), src=ins[t]) for j, chip in enumerate(chips)]
        for cp in first:
            cp.start()
        passed = []
        for j, chip in enumerate(chips):
            for t in range(n):
                copy(t, 1 + j, (*chip, mc), me).wait_recv()
                fwd = copy(t, 4 + j, (*chip, mc), sibling)
                fwd.start()
                passed.append(fwd)
        for t in range(n):
            copy(t, 0, sibling, me).wait_recv()
            for j, chip in enumerate(chips):
                copy(t, 4 + j, (*chip, 1 - mc), me).wait_recv()
        for cp in first + passed:
            cp.wait_send()
        for cp in mine:
            cp.wait()

    hbm = pl.BlockSpec(memory_space=pl.ANY)
    return pl.pallas_call(
        body, out_shape=tuple(jax.ShapeDtypeStruct(s, x.dtype) for s, x in zip(full_shapes, shards)),
        in_specs=[hbm] * n, out_specs=(hbm,) * n,
        scratch_shapes=[pltpu.SemaphoreType.DMA((n, 7)), pltpu.SemaphoreType.DMA((n, 7)), pltpu.SemaphoreType.DMA((n,))],
        name=name)(*shards)


FLIPS = [(fx, fy, fc) for fx in (0, 1) for fy in (0, 1) for fc in (0, 1)][1:]


def _exchange_copies(kind, places, src, land, send_sems, recv_sems, local_sems):
    mx, my, mc = lax.axis_index("x"), lax.axis_index("y"), lax.axis_index("c")
    me = 4 * mx + 2 * my + mc
    n = len(src)
    local, remote = [], []
    for t in range(n):
        if kind == "gather":
            local.append(pltpu.make_async_copy(src[t], places[t](land[t], me), local_sems.at[t]))
        else:
            local.append(pltpu.make_async_copy(places[t](src[t], me), land[t].at[me], local_sems.at[t]))
    for k, (fx, fy, fc) in enumerate(FLIPS):
        px, py, pc = mx ^ fx, my ^ fy, mc ^ fc
        peer = 4 * px + 2 * py + pc
        for t in range(n):
            sems = dict(send_sem=send_sems.at[7 * t + k], recv_sem=recv_sems.at[7 * t + k], device_id=(px, py, pc), device_id_type=MESH)
            if kind == "gather":
                pair = [(src[t], places[t](land[t], me)), (src[t], places[t](land[t], peer))]
            else:
                pair = [(places[t](src[t], peer), land[t].at[me]), (places[t](src[t], peer), land[t].at[peer])]
            remote.append([functools.partial(pltpu.make_async_remote_copy, src_ref=s_, dst_ref=d_, **sems) for s_, d_ in pair])
    return local, remote


HBM_SPEC = pl.BlockSpec(memory_space=pltpu.HBM)
SEM_SPEC = pl.BlockSpec(memory_space=pltpu.SEMAPHORE)
SIDE_EFFECT = pltpu.SideEffectType.DATAFLOW_SIDE_EFFECTING


def _exchange_start(kind, srcs, land_shapes, places, after, *, name):
    n = len(srcs)

    def body(*refs):
        src, land = refs[:n], refs[n:2 * n]
        send_sems, recv_sems, local_sems = refs[2 * n + 1:2 * n + 4]
        token = refs[-1]
        local, remote = _exchange_copies(kind, places, src, land, send_sems, recv_sems, local_sems)
        for cp in local:
            cp.start()
        for send, _ in remote:
            send().start()
        token[...] = jnp.zeros_like(token)

    hbm = lambda t: pltpu.with_memory_space_constraint(t, pltpu.HBM)
    lands = [hbm(lax.empty(tuple(s), x.dtype)) for s, x in zip(land_shapes, srcs)]
    out_shape = (pltpu.SemaphoreType.DMA((7 * n,)), pltpu.SemaphoreType.DMA((7 * n,)), pltpu.SemaphoreType.DMA((n,)),
                 *[pltpu.HBM(x.shape, x.dtype) for x in srcs], *[pltpu.HBM(tuple(s), x.dtype) for s, x in zip(land_shapes, srcs)],
                 jax.ShapeDtypeStruct((8, LANES), F32))
    outs = pl.pallas_call(
        body, name=name, out_shape=out_shape, in_specs=[HBM_SPEC] * (2 * n) + [pl.BlockSpec(memory_space=pl.ANY)],
        out_specs=(SEM_SPEC, SEM_SPEC, SEM_SPEC) + (HBM_SPEC,) * (2 * n) + (pl.BlockSpec(memory_space=pltpu.VMEM),),
        input_output_aliases={i: 3 + i for i in range(2 * n)},
        compiler_params=pltpu.CompilerParams(has_side_effects=SIDE_EFFECT))(*[hbm(x) for x in srcs], *lands, after)
    return dict(sems=outs[:3], src=outs[3:3 + n], land=outs[3 + n:3 + 2 * n], token=outs[-1])


def _exchange_wait(kind, started, places, after, *, name):
    n = len(started["src"])

    def body(*refs):
        src, land = refs[:n], refs[n:2 * n]
        send_sems, recv_sems, local_sems = refs[2 * n:2 * n + 3]
        local, remote = _exchange_copies(kind, places, src, land, send_sems, recv_sems, local_sems)
        for cp in local:
            cp.wait()
        for send, arrival in remote:
            send().wait_send()
            arrival().wait_recv()

    out_shape = tuple(pltpu.HBM(x.shape, x.dtype) for x in started["src"]) + tuple(pltpu.HBM(x.shape, x.dtype) for x in started["land"])
    outs = pl.pallas_call(
        body, name=name, out_shape=out_shape,
        in_specs=[HBM_SPEC] * (2 * n) + [SEM_SPEC] * 3 + [pl.BlockSpec(memory_space=pl.ANY)], out_specs=(HBM_SPEC,) * (2 * n),
        input_output_aliases={i: i for i in range(2 * n)},
        compiler_params=pltpu.CompilerParams(has_side_effects=SIDE_EFFECT))(*started["src"], *started["land"], *started["sems"], after)
    return list(outs[n:])


def _adamw(parts, w, m, v, *, br, name):
    layers, r, wd = w.shape
    assert len(parts) == layers

    def body(*refs):
        p_refs = refs[:layers]
        w_ref, m_ref, v_ref, g_ref, d_ref, nm_ref, nv_ref = refs[layers:]
        for k in range(layers):
            @pl.when(pl.program_id(0) == k)
            def _(p_ref=p_refs[k]):
                g = p_ref[0].astype(F32)
                for dev in range(1, N_DEV):
                    g = g + p_ref[dev].astype(F32)
                mm = ADAM_B1 * m_ref[...] + (1.0 - ADAM_B1) * g
                vv = ADAM_B2 * v_ref[...] + (1.0 - ADAM_B2) * (g * g)
                m_hat = mm / (1.0 - ADAM_B1 ** ADAM_STEP)
                v_hat = vv / (1.0 - ADAM_B2 ** ADAM_STEP)
                g_ref[...] = g
                d_ref[...] = -ADAM_LR * (m_hat / (jnp.sqrt(v_hat) + ADAM_EPS) + ADAM_WD * w_ref[...])
                nm_ref[...] = mm
                nv_ref[...] = vv

    p_spec = lambda k: pl.BlockSpec((N_DEV, None, br, wd), lambda l, i: (0, 0, jnp.where(l == k, i, 0), 0))
    blk = pl.BlockSpec((None, br, wd), lambda l, i: (l, i, 0))
    shp = jax.ShapeDtypeStruct((layers, r, wd), F32)
    return pl.pallas_call(body, out_shape=(shp, shp, shp, shp), grid=(layers, r // br),
                          in_specs=[p_spec(k) for k in range(layers)] + [blk, blk, blk], out_specs=(blk, blk, blk, blk),
                          compiler_params=_params("arbitrary", "arbitrary"), name=name)(*parts, w, m, v)


SMALL = ("norm_mix", "norm_mem", "norm_ffn", "b_forget", "conv_b", "norm_final")


def _pack(tensors):
    flat = jnp.concatenate([t.reshape(-1) for t in tensors])
    rows = -(-flat.shape[0] // (PACK_W * PACK_ROW_ALIGN)) * PACK_ROW_ALIGN
    flat = jnp.pad(flat, (0, rows * PACK_W - flat.shape[0]))
    return flat.reshape(1, rows, PACK_W)


def _unpack(buf, shapes):
    flat = buf.reshape(-1)
    out, off = [], 0
    for shp in shapes:
        n = math.prod(shp)
        out.append(flat[off:off + n].reshape(tuple(shp)))
        off += n
    return out


def _fox_permute(w):
    pad = jnp.zeros(w.shape[:-1] + (FOX_P - FOX_IN,), w.dtype)
    return jnp.concatenate([w[..., :3 * D_MIX], w[..., 3 * D_MIX + N_MIX_HEADS:], w[..., 3 * D_MIX:3 * D_MIX + N_MIX_HEADS], pad], axis=-1)


def _fox_unpermute(w):
    return jnp.concatenate([w[..., :3 * D_MIX], w[..., DIL_IN:DIL_IN + N_MIX_HEADS], w[..., 3 * D_MIX:DIL_IN]], axis=-1)


def _bias_layout(c):
    s = c.shape[0]
    ct = c[:, :N_MIX_HEADS].T.reshape(N_MIX_HP, 2, s)
    return jnp.pad(ct, ((0, 0), (0, 6), (0, 0)))


def _bias_grad(dck6):
    s = dck6.shape[2]
    dk = dck6[:, :2, :].reshape(N_MIX_HEADS, s).T
    return jnp.pad(dk, ((0, 0), (0, LANES - N_MIX_HEADS)))


def _device_step(x, mem, target, small, get_weights, put_grads):
    s = x.shape[0]
    mt = mem.shape[0]
    bq = 512
    cos_t, sin_t = _rope_tables(s)
    row = lambda t, l: t[l][None, :]
    saved = []
    h = x
    cb8 = small["conv_b"].reshape(DEPTH, N_DEV, 1, FF_CHUNK)
    for l in range(DEPTH):
        kind, slot = l % 2, l // 2
        wl = dict(get_weights(l, "attn", h))
        if l == 0:
            xn = _rmsnorm_fwd(h, row(small["norm_mix"], l), br=512, name=f"norm_mix_fwd{l}")
        mn = _rmsnorm_fwd(mem, row(small["norm_mem"], l), br=mt, name=f"norm_mem_fwd{l}")
        proj = _mm(xn, wl["w_in"], "nn" if kind == 0 else "nt", tm=1024, tn=384 if kind == 0 else 512, layer=0, name=f"in_proj{l}")
        kvm = _mm(mn, wl["w_mem_kv"], "nn", tm=mt, tn=512, layer=0, name=f"mem_kv{l}")
        st = dict(h=h, xn=xn, mn=mn, proj=proj, kvm=kvm, w=wl)
        if kind == 0:
            b_pad = jnp.pad(small["b_forget"][slot], (0, LANES - N_MIX_HEADS))[None, :]
            c = _forget_cumsum(proj, b_pad, name=f"forget_cumsum{l}")
            ck6 = _bias_layout(c)
            mix, lse, heads = _attn_fwd(proj, proj, ck6, q_col=0, k_col=N_MIX_HP, v_col=2 * N_MIX_HP, n_hp=N_MIX_HP,
                                        causal=True, bq=min(s, 1024), bk=min(s, 1024), name=f"fox_fwd{l}")
            st.update(b_pad=b_pad, ck6=ck6, mix=mix, lse=lse)
        else:
            qk_r = _rope(proj, cos_t, sin_t, n_cols=2 * N_MIX_HP, out_dtype=F32, br=512, name=f"rope_fwd{l}")
            mix, lses, heads = _dil_fwd(qk_r, proj, name=f"dil_fwd{l}")
            st.update(qk_r=qk_r, lses=lses, mix=mix)
        mo, lse_m, heads = _attn_fwd(proj, kvm, None, q_col=QM_COL, k_col=0, v_col=N_MEM_HP, n_hp=N_MEM_HP, causal=False,
                                     bq=min(s, 2048), bk=mt, heads=heads, heads_col=N_MIX_HP, name=f"mem_fwd{l}")
        h1, xf = _mm(heads, wl["w_out"], "nn", tm=1024, tn=D_MODEL, res=h, layer=0, norm_gain=row(small["norm_ffn"], l),
                     name=f"out_proj{l}")
        wl.update(get_weights(l, "ffn", xf))
        u = _mm(xf, wl["w_up"], "nt", tm=1024, tn=FF_CHUNK, layer=0, chunk="b", name=f"up_proj{l}")
        a = _conv_fwd(u, wl["conv_w"], cb8[l], 0, name=f"conv_fwd{l}")
        st.update(mo=mo, lse_m=lse_m, heads=heads, h1=h1, xf=xf, u=u, a=a)
        saved.append(st)
        if l + 1 < DEPTH:
            h, xn = _mm(a, wl["w_down"], "nn", tm=512, tn=D_MODEL, res=h1, layer=0, chunk="reduce",
                        norm_gain=row(small["norm_mix"], l + 1), name=f"down_proj{l}")
        else:
            h = _mm(a, wl["w_down"], "nn", tm=1024, tn=512, res=h1, layer=0, chunk="reduce", name=f"down_proj{l}")

    dh, dhb, dg_final, loss = _loss_head(h, target, small["norm_final"][None, :], br=512, name="loss_head")
    gs = {k: [None] * DEPTH for k in ("norm_mix", "norm_mem", "norm_ffn", "conv_b")}
    gs["b_forget"] = [None] * 2
    dep = 0.0
    for l in reversed(range(DEPTH)):
        st = saved[l]
        wl = st["w"]
        gw = {}
        kind, slot = l % 2, l // 2
        da = _mm(dhb, wl["w_down"], "nt", tm=1024, tn=FF_CHUNK, layer=0, chunk="b", name=f"down_dx{l}")
        gw["w_down"] = _mm(st["a"], dhb, "tn", tm=FF_CHUNK, tn=512, out_dtype=BF16, chunk="a", name=f"down_dw{l}")
        du, dwb = _conv_bwd(st["u"], wl["conv_w"], cb8[l] + dep, da, 0, name=f"conv_bwd{l}")
        du = du.reshape(N_DEV, s, FF_CHUNK)
        dwb = dwb.reshape(N_DEV, 8, FF_CHUNK)
        gw["conv_w"] = dwb
        gs["conv_b"][l] = dwb[:, 3, :].reshape(-1)
        dxf = _mm(du, wl["w_up"], "nn", tm=512, tn=512, layer=0, chunk="reduce", name=f"up_dx{l}")
        gw["w_up"] = _mm(du, st["xf"], "tn", tm=FF_CHUNK, tn=512, out_dtype=BF16, chunk="a", name=f"up_dw{l}")
        dh1, dh1b, dgf = _rmsnorm_bwd(st["h1"], dxf, row(small["norm_ffn"], l), dh, br=512, name=f"norm_ffn_bwd{l}")
        gs["norm_ffn"][l] = dgf[0]
        dheads = _mm(dh1b, wl["w_out"], "nt", tm=1024, tn=512, layer=0, name=f"out_dx{l}")
        gw["w_out"] = _mm(st["heads"], dh1b, "tn", tm=512, tn=512, out_dtype=BF16, name=f"out_dw{l}")
        dqm, dkm, dvm = _attn_bwd(st["proj"], st["kvm"], st["mo"], dheads, st["lse_m"], None, q_col=QM_COL, k_col=0,
                                  v_col=N_MEM_HP, o_col=N_MIX_HP, n_hp=N_MEM_HP, causal=False, bq=min(s, 1024), bk=mt,
                                  name=f"mem_bwd{l}")
        dkvm = jnp.concatenate([dkm, dvm], axis=1)
        gw["w_mem_kv"] = _mm(st["mn"], dkvm, "tn", tm=512, tn=512, out_dtype=BF16, name=f"mem_kv_dw{l}")
        dmn = _mm(dkvm, wl["w_mem_kv"], "nt", tm=mt, tn=512, layer=0, name=f"mem_kv_dx{l}")
        dep_early = put_grads(l, "ffn", gw)
        _, _, dgm = _rmsnorm_bwd(mem, dmn, row(small["norm_mem"], l), None, br=mt, name=f"norm_mem_bwd{l}")
        gs["norm_mem"][l] = dgm[0]
        if kind == 0:
            dq, dk, dv, dcq6, dck6 = _attn_bwd(st["proj"], st["proj"], st["mix"], dheads, st["lse"], st["ck6"] + dep_early, q_col=0,
                                               k_col=N_MIX_HP, v_col=2 * N_MIX_HP, o_col=0, n_hp=N_MIX_HP, causal=True,
                                               bq=bq, bk=bq, name=f"fox_bwd{l}")
            dz, db = _forget_cumsum_bwd(st["proj"], st["b_pad"], _bias_grad(dcq6), _bias_grad(dck6), name=f"forget_cumsum_bwd{l}")
            gs["b_forget"][slot] = db[0, :N_MIX_HEADS]
            dproj = jnp.concatenate([dq.astype(BF16), dk, dv, dqm.astype(BF16), dz], axis=1)
        else:
            dq_r, dk_r, dv = _dil_bwd(st["qk_r"], st["proj"], st["mix"], dheads, st["lses"], name=f"dil_bwd{l}")
            dq = _rope(dq_r, cos_t + dep_early, -sin_t, n_cols=N_MIX_HP, out_dtype=BF16, br=512, name=f"rope_bwd_q{l}")
            dk = _rope(dk_r, cos_t, -sin_t, n_cols=N_MIX_HP, out_dtype=BF16, br=512, name=f"rope_bwd_k{l}")
            dproj = jnp.concatenate([dq, dk, dv.astype(BF16), dqm.astype(BF16)], axis=1)
        dxn = _mm(dproj, wl["w_in"], "nt" if kind == 0 else "nn", tm=1024, tn=512, layer=0, name=f"in_dx{l}")
        if kind == 0:
            gw["w_in"] = _mm(st["xn"], dproj, "tn", tm=512, tn=384, out_dtype=BF16, name=f"in_dw{l}")
        else:
            gw["w_in"] = _mm(dproj, st["xn"], "tn", tm=512, tn=512, out_dtype=BF16, name=f"in_dw{l}")
        dh, dhb, dgx = _rmsnorm_bwd(st["h"], dxn, row(small["norm_mix"], l), dh1, br=512, name=f"norm_mix_bwd{l}")
        gs["norm_mix"][l] = dgx[0]
        dep = put_grads(l, "attn", gw)

    grads_s = {k: jnp.stack(v) for k, v in gs.items()}
    grads_s["norm_final"] = dg_final[0]
    return loss[0, 0], dh, grads_s


def kernel(x, mem, norm_mix, norm_mem, norm_ffn, w_in_fox, b_forget, w_in_dil, w_mem_kv, w_out, w_up, conv_w, conv_b, w_down, norm_final, loss_target, m_norm_mix, m_norm_mem, m_norm_ffn, m_w_in_fox, m_b_forget, m_w_in_dil, m_w_mem_kv, m_w_out, m_w_up, m_conv_w, m_conv_b, m_w_down, m_norm_final, v_norm_mix, v_norm_mem, v_norm_ffn, v_w_in_fox, v_b_forget, v_w_in_dil, v_w_mem_kv, v_w_out, v_w_up, v_conv_w, v_conv_b, v_w_down, v_norm_final):
    names = ["norm_mix", "norm_mem", "norm_ffn", "w_in_fox", "b_forget", "w_in_dil", "w_mem_kv", "w_out", "w_up", "conv_w", "conv_b",
             "w_down", "norm_final"]
    w = dict(zip(names, (norm_mix, norm_mem, norm_ffn, w_in_fox, b_forget, w_in_dil, w_mem_kv, w_out, w_up, conv_w, conv_b, w_down, norm_final)))
    m = dict(zip(names, (m_norm_mix, m_norm_mem, m_norm_ffn, m_w_in_fox, m_b_forget, m_w_in_dil, m_w_mem_kv, m_w_out, m_w_up, m_conv_w,
                         m_conv_b, m_w_down, m_norm_final)))
    v = dict(zip(names, (v_norm_mix, v_norm_mem, v_norm_ffn, v_w_in_fox, v_b_forget, v_w_in_dil, v_w_mem_kv, v_w_out, v_w_up, v_conv_w,
                         v_conv_b, v_w_down, v_norm_final)))
    big = ("w_in_fox", "w_in_dil", "w_mem_kv", "w_out", "w_up", "w_down", "conv_w")
    small_shapes = [w[k].shape for k in SMALL]
    dil_c = w_in_dil.shape[2]
    rows = {k: w[k].shape[1] for k in ("w_in_fox", "w_mem_kv", "w_out", "w_down")}

    def places(l):
        w_in_place = _rows_of(rows["w_in_fox"]) if l % 2 == 0 else _slot1
        return [w_in_place, _rows_of(rows["w_mem_kv"]), _rows_of(rows["w_out"]), _slot1, _rows_of(rows["w_down"]), _slot1]

    def full_shapes(l):
        w_in_shape = (1, D_MODEL, FOX_P) if l % 2 == 0 else (1, N_DEV, dil_c, D_MODEL)
        return [w_in_shape, (1, D_MODEL, 2 * D_MEMQ), (1, D_MODEL, D_MODEL), (1, N_DEV, FF_CHUNK, D_MODEL), (1, D_FF, D_MODEL),
                (1, N_DEV, 3, FF_CHUNK)]

    transposed = lambda t: jnp.swapaxes(t, 1, 2)
    cast = {"w_in_fox": _fox_permute(w_in_fox).astype(BF16), "w_in_dil": transposed(w_in_dil).astype(BF16),
            "w_mem_kv": w_mem_kv.astype(BF16), "w_out": w_out.astype(BF16), "w_up": transposed(w_up).astype(BF16),
            "w_down": w_down.astype(BF16), "conv_w": conv_w}
    everything = (0, 1, 2, 3, 4, 5)
    gather_groups = {l: ((0, 1, 2), (3, 4, 5)) if l == 0 else (everything,) for l in range(DEPTH)}
    scatter_groups = {l: ((1, 2, 3, 4, 5), (0,)) if l == 0 else (everything,) for l in range(DEPTH)}
    pick = lambda seq, group: [seq[i] for i in group]
    tag = lambda l, group: f"{l}" + ("" if group == everything else "_" + "".join(str(i) for i in group))

    gathers, after = {}, norm_final
    for l in range(DEPTH):
        w_in_shard = cast["w_in_fox" if l % 2 == 0 else "w_in_dil"][l // 2][None]
        shards = [w_in_shard] + [cast[k][l][None] for k in ("w_mem_kv", "w_out", "w_up", "w_down", "conv_w")]
        for group in gather_groups[l]:
            gathers[l, group] = _exchange_start("gather", pick(shards, group), pick(full_shapes(l), group), pick(places(l), group), after,
                                                name=f"weights_gather_start{tag(l, group)}")
            after = gathers[l, group]["token"]
    started = sum(g["token"][0, 0] for g in gathers.values())
    small = {k: w[k] for k in SMALL}
    small["norm_mix"] = norm_mix + started
    landed = {}

    def get_weights(l, part, h):
        group = [g for g in gather_groups[l] if (0 if part == "attn" else 3) in g][0]
        if (l, group) not in landed:
            lands = _exchange_wait("gather", gathers[l, group], pick(places(l), group), h, name=f"weights_gather_wait{tag(l, group)}")
            landed[l, group] = dict(zip(group, lands))
        got = landed[l, group]
        if part == "ffn":
            return dict(w_up=got[3], w_down=got[4].reshape(1, FF_HALF, FF_CHUNK, D_MODEL), conv_w=got[5])
        w_in = got[0] if l % 2 == 0 else got[0].reshape(1, N_DEV * dil_c, D_MODEL)
        return dict(w_in=w_in, w_mem_kv=got[1], w_out=got[2])

    scatters, pending = {}, {}

    def put_grads(l, part, g):
        pending.setdefault(l, {}).update(g)
        if part == "ffn" and len(scatter_groups[l]) == 1:
            return 0.0
        group = scatter_groups[l][0 if part == "ffn" else -1]
        have = pending[l]
        srcs = {3: lambda: have["w_up"][None], 4: lambda: have["w_down"].reshape(1, D_FF, D_MODEL), 5: lambda: have["conv_w"][None],
                1: lambda: have["w_mem_kv"][None], 2: lambda: have["w_out"][None]}
        if l % 2 == 0:
            srcs[0] = lambda: have["w_in"][None]
        else:
            srcs[0] = lambda: have["w_in"].reshape(1, N_DEV, dil_c, D_MODEL)
        shard_shapes = [cast["w_in_fox" if l % 2 == 0 else "w_in_dil"][l // 2].shape] + \
            [cast[k][l].shape for k in ("w_mem_kv", "w_out", "w_up", "w_down")] + [(8, FF_CHUNK)]
        sources = [srcs[i]() for i in group]
        scatters[l, group] = _exchange_start("scatter", sources, [(N_DEV, 1) + tuple(s) for s in pick(shard_shapes, group)],
                                             pick(places(l), group), sources[0], name=f"grads_scatter_start{tag(l, group)}")
        return scatters[l, group]["token"][0, 0]

    loss, grad_x, gs = _device_step(x[0], mem[0], loss_target[0], small, get_weights, put_grads)

    recv = {}

    def wait_scatter(l, group, after_):
        lands = _exchange_wait("scatter", scatters[l, group], pick(places(l), group), after_, name=f"grads_scatter_wait{tag(l, group)}")
        recv.setdefault(l, {}).update(zip(group, lands))

    for l in reversed(range(1, DEPTH)):
        wait_scatter(l, everything, grad_x)
    wait_scatter(0, scatter_groups[0][0], grad_x)
    s_pack = _pack([gs[k] for k in SMALL])
    (s_recv,) = _all_gather([s_pack], [(N_DEV,) + s_pack.shape], [_slot0], name="small_grads_all_gather")

    layer_tensors = ("w_in", "w_mem_kv", "w_out", "w_up", "w_down", "conv_w")
    layer_parts = lambda k: [recv[l][layer_tensors.index(k)] for l in range(DEPTH)]
    to_local = {k: (lambda t: t) for k in big}
    to_local["w_in_fox"] = _fox_permute
    to_local["w_in_dil"] = to_local["w_up"] = transposed
    from_local = {k: (lambda t: t) for k in big}
    from_local["w_in_fox"] = _fox_unpermute
    from_local["w_in_dil"] = from_local["w_up"] = transposed
    blocks = {"w_in_fox": rows["w_in_fox"], "w_in_dil": dil_c, "w_mem_kv": rows["w_mem_kv"], "w_out": rows["w_out"], "w_up": FF_CHUNK // 4,
              "w_down": rows["w_down"] // 2, "conv_w": 3}
    outs = {}

    def update(k, parts):
        f = to_local[k]
        outs[k] = [from_local[k](t) for t in _adamw(parts, f(w[k]), f(m[k]), f(v[k]), br=blocks[k], name=f"adamw_{k}")]

    update("w_in_dil", [recv[l][0] for l in range(1, DEPTH, 2)])
    update("w_up", layer_parts("w_up"))
    update("w_down", layer_parts("w_down"))
    update("conv_w", [p[:, :, :3, :] for p in layer_parts("conv_w")])
    update("w_mem_kv", layer_parts("w_mem_kv"))
    update("w_out", layer_parts("w_out"))
    wait_scatter(0, scatter_groups[0][-1], outs["w_out"][1])
    update("w_in_fox", [recv[l][0] for l in range(0, DEPTH, 2)])
    small_outs = _adamw([s_recv], _pack([w[k] for k in SMALL]), _pack([m[k] for k in SMALL]), _pack([v[k] for k in SMALL]),
                        br=s_pack.shape[1], name="adamw_small")
    res = []
    for i, os_ in enumerate(small_outs):
        d = {k: outs[k][i] for k in big}
        d.update(zip(SMALL, _unpack(os_, small_shapes)))
        res.append([d[k] for k in names])
    loss = lax.psum(loss, ("x", "y", "c"))
    return (loss, grad_x[None], *res[0], *res[1], *res[2], *res[3])
```

```python
import functools
import math

import jax
import jax.numpy as jnp
from jax import lax
from jax.experimental import pallas as pl
from jax.experimental.pallas import tpu as pltpu

F32 = jnp.float32
BF16 = jnp.bfloat16

D_MODEL = 1024
HEAD_DIM = 64
N_MIX_HEADS = 12
N_MEM_HEADS = 4
D_MIX = N_MIX_HEADS * HEAD_DIM
D_MEMQ = N_MEM_HEADS * HEAD_DIM
D_FF = 2816
DEPTH = 4
FOX_IN = 3 * D_MIX + N_MIX_HEADS + D_MEMQ
DIL_IN = 3 * D_MIX + D_MEMQ
LANES = 128
FOX_P = DIL_IN + LANES
N_MIX_HP = D_MIX // LANES
N_MEM_HP = D_MEMQ // LANES
QM_COL = 3 * N_MIX_HP
F_COL = DIL_IN // LANES
DILATED_BRANCHES = ((128, 1), (512, 4), (2048, 16))
DIL_L = 128
DIL_UNROLL_FWD = 8
DIL_UNROLL_BWD = 4
ROPE_THETA = 10000.0
NORM_EPS = 1e-6
NEG = -1e30
SCALE = HEAD_DIM ** -0.5
N_DEV = 8

ADAM_LR = 0.001
ADAM_B1 = 0.9
ADAM_B2 = 0.999
ADAM_EPS = 1e-08
ADAM_WD = 0.01
ADAM_STEP = 10

VMEM_LIMIT = 56 * 1024 * 1024
PACK_W = 1024
PACK_ROW_ALIGN = 8

MESH = pl.DeviceIdType.MESH
NT = (((1,), (1,)), ((), ()))
NN = (((1,), (0,)), ((), ()))
TN = (((0,), (0,)), ((), ()))


def _params(*sem):
    return pltpu.CompilerParams(dimension_semantics=sem, vmem_limit_bytes=VMEM_LIMIT)


def _lane_lo(shape):
    return lax.broadcasted_iota(jnp.int32, shape, len(shape) - 1) < HEAD_DIM


def _pair(lo, a, b):
    return jnp.where(lo, a, b)


def _mm(a, b, mode, *, tm, tn, name, out_dtype=F32, res=None, layer=None, chunk=None, norm_gain=None, norm_bwd=None):
    lead = () if layer is None else (layer,)
    nl = (None,) * len(lead)
    bs = b.shape[len(lead):]
    dims = {"nn": NN, "nt": NT, "tn": TN}[mode]
    reduce_n = 0
    if chunk is None:
        (m, k) = a.shape[::-1] if mode == "tn" else a.shape
        n = bs[0] if mode == "nt" else bs[1]
        grid = (m // tm, n // tn)
        a_spec = pl.BlockSpec((k, tm), lambda i, j: (0, i)) if mode == "tn" else pl.BlockSpec((tm, k), lambda i, j: (i, 0))
        b_spec = pl.BlockSpec(nl + ((tn, k) if mode == "nt" else (k, tn)), lambda i, j: lead + ((j, 0) if mode == "nt" else (0, j)))
        o_spec = pl.BlockSpec((tm, tn), lambda i, j: (i, j))
        out_shape = (m, n)
    elif chunk == "b":
        (m, k) = a.shape[::-1] if mode == "tn" else a.shape
        c, nc = bs[0], (bs[1] if mode == "nt" else bs[2])
        grid = (m // tm, c)
        a_spec = pl.BlockSpec((k, tm), lambda i, j: (0, i)) if mode == "tn" else pl.BlockSpec((tm, k), lambda i, j: (i, 0))
        b_spec = pl.BlockSpec(nl + (None,) + tuple(bs[1:]), lambda i, j: lead + (j, 0, 0))
        o_spec = pl.BlockSpec((None, tm, nc), lambda i, j: (j, i, 0))
        out_shape = (c, m, nc)
    elif chunk == "a":
        assert mode == "tn"
        c, k, mc = a.shape
        n = bs[1]
        grid = (c, n // tn)
        a_spec = pl.BlockSpec((None, k, mc), lambda i, j: (i, 0, 0))
        b_spec = pl.BlockSpec(nl + (k, tn), lambda i, j: lead + (0, j))
        o_spec = pl.BlockSpec((None, mc, tn), lambda i, j: (i, 0, j))
        out_shape = (c, mc, n)
    else:
        reduce_n, m, kc = a.shape
        n = bs[1] if mode == "nt" else bs[2]
        grid = (m // tm, n // tn)
        a_spec = pl.BlockSpec((reduce_n, tm, kc), lambda i, j: (0, i, 0))
        b_spec = pl.BlockSpec(nl + ((reduce_n, tn, kc) if mode == "nt" else (reduce_n, kc, tn)),
                              lambda i, j: lead + ((0, j, 0) if mode == "nt" else (0, 0, j)))
        o_spec = pl.BlockSpec((tm, tn), lambda i, j: (i, j))
        out_shape = (m, n)

    if norm_gain is not None or norm_bwd is not None:
        assert chunk in (None, "reduce") and tn == n, "the RMSNorm of the result needs whole rows in a block"

    def body(*refs):
        a_ref, b_ref = refs[0], refs[1]
        dot = lambda x, y: lax.dot_general(x.astype(BF16), y.astype(BF16), dims, preferred_element_type=F32)
        if reduce_n:
            acc = dot(a_ref[0], b_ref[0])
            for r in range(1, reduce_n):
                acc = acc + dot(a_ref[r], b_ref[r])
        else:
            acc = dot(a_ref[...], b_ref[...])
        if norm_bwd is not None:
            x_ref, g_ref, r_ref = refs[2:5]
            dx_ref, dxb_ref, dg_ref = refs[-3:]
            dx, dg = _rms_bwd_math(x_ref[...], acc, g_ref[...])
            dx = dx + r_ref[...]
            dx_ref[...] = dx
            dxb_ref[...] = dx.astype(BF16)

            @pl.when(pl.program_id(0) == 0)
            def _():
                dg_ref[...] = jnp.zeros_like(dg_ref)

            dg_ref[0:1, :] += dg
            return
        o_ref = refs[-2] if norm_gain is not None else refs[-1]
        if res is not None:
            acc = acc + refs[2][...]
        o_ref[...] = acc.astype(o_ref.dtype)
        if norm_gain is not None:
            rs = lax.rsqrt(jnp.mean(acc * acc, axis=-1, keepdims=True) + NORM_EPS)
            refs[-1][...] = (acc * rs * refs[3][...]).astype(BF16)

    ins = [a, b] + ([res] if res is not None else [])
    specs = [a_spec, b_spec] + ([o_spec] if res is not None else [])
    out_shapes, out_specs = jax.ShapeDtypeStruct(out_shape, out_dtype), o_spec
    sem = ("parallel", "parallel")
    if norm_gain is not None:
        assert res is not None
        ins.append(norm_gain)
        specs.append(pl.BlockSpec((1, n), lambda i, j: (0, 0)))
        out_shapes, out_specs = (out_shapes, jax.ShapeDtypeStruct(out_shape, BF16)), (o_spec, o_spec)
    if norm_bwd is not None:
        assert res is None and norm_gain is None
        x_in, gain, resid = norm_bwd
        ins += [x_in, gain, resid]
        specs += [o_spec, pl.BlockSpec((1, n), lambda i, j: (0, 0)), o_spec]
        out_shapes = (jax.ShapeDtypeStruct(out_shape, F32), jax.ShapeDtypeStruct(out_shape, BF16), jax.ShapeDtypeStruct((8, n), F32))
        out_specs = (o_spec, o_spec, pl.BlockSpec((8, n), lambda i, j: (0, 0)))
        sem = ("arbitrary", "arbitrary")
    return pl.pallas_call(body, out_shape=out_shapes, grid=grid, in_specs=specs, out_specs=out_specs,
                          compiler_params=_params(*sem), name=name)(*ins)


def _rmsnorm_fwd(x, g, *, br, name):
    r, d = x.shape

    def body(x_ref, g_ref, o_ref):
        xf = x_ref[...]
        rs = lax.rsqrt(jnp.mean(xf * xf, axis=-1, keepdims=True) + NORM_EPS)
        o_ref[...] = (xf * rs * g_ref[...]).astype(BF16)

    return pl.pallas_call(body, out_shape=jax.ShapeDtypeStruct((r, d), BF16), grid=(r // br,),
                          in_specs=[pl.BlockSpec((br, d), lambda i: (i, 0)), pl.BlockSpec((1, d), lambda i: (0, 0))],
                          out_specs=pl.BlockSpec((br, d), lambda i: (i, 0)), compiler_params=_params("parallel"), name=name)(x, g)


def _rms_bwd_math(x, dy, g):
    d = x.shape[-1]
    rs = lax.rsqrt(jnp.mean(x * x, axis=-1, keepdims=True) + NORM_EPS)
    gy = dy * g
    proj = jnp.sum(x * gy, axis=-1, keepdims=True) * (1.0 / d)
    dx = rs * gy - x * (rs * rs * rs) * proj
    dg = jnp.sum(dy * (x * rs), axis=0, keepdims=True)
    return dx, dg


def _rmsnorm_bwd(x, dy, g, res, *, br, name):
    r, d = x.shape
    has_res = res is not None

    def body(*refs):
        x_ref, dy_ref, g_ref = refs[:3]
        dx_ref, dxb_ref, dg_ref = refs[-3:]
        dx, dg = _rms_bwd_math(x_ref[...], dy_ref[...], g_ref[...])
        if has_res:
            dx = dx + refs[3][...]
        dx_ref[...] = dx
        dxb_ref[...] = dx.astype(BF16)

        @pl.when(pl.program_id(0) == 0)
        def _():
            dg_ref[...] = jnp.zeros_like(dg_ref)

        dg_ref[0:1, :] += dg

    row = pl.BlockSpec((br, d), lambda i: (i, 0))
    ins = [x, dy, g] + ([res] if has_res else [])
    specs = [row, row, pl.BlockSpec((1, d), lambda i: (0, 0))] + ([row] if has_res else [])
    return pl.pallas_call(
        body, out_shape=(jax.ShapeDtypeStruct((r, d), F32), jax.ShapeDtypeStruct((r, d), BF16), jax.ShapeDtypeStruct((8, d), F32)),
        grid=(r // br,), in_specs=specs, out_specs=(row, row, pl.BlockSpec((8, d), lambda i: (0, 0))),
        compiler_params=_params("arbitrary"), name=name)(*ins)


def _loss_head(h, target, g, *, br, name):
    r, d = h.shape

    def body(x_ref, t_ref, g_ref, dx_ref, dxb_ref, dg_ref, loss_ref):
        x = x_ref[...]
        gg = g_ref[...]
        rs = lax.rsqrt(jnp.mean(x * x, axis=-1, keepdims=True) + NORM_EPS)
        err = x * rs * gg - t_ref[...]
        part = jnp.sum(jnp.sum(err * err, axis=1, keepdims=True), axis=0, keepdims=True) * (0.5 / d)
        dx, dg = _rms_bwd_math(x, err * (1.0 / d), gg)
        dx_ref[...] = dx
        dxb_ref[...] = dx.astype(BF16)

        @pl.when(pl.program_id(0) == 0)
        def _():
            dg_ref[...] = jnp.zeros_like(dg_ref)
            loss_ref[...] = jnp.zeros_like(loss_ref)

        dg_ref[0:1, :] += dg
        loss_ref[...] += jnp.broadcast_to(part, loss_ref.shape)

    row = pl.BlockSpec((br, d), lambda i: (i, 0))
    return pl.pallas_call(
        body, out_shape=(jax.ShapeDtypeStruct((r, d), F32), jax.ShapeDtypeStruct((r, d), BF16),
                         jax.ShapeDtypeStruct((8, d), F32), jax.ShapeDtypeStruct((8, LANES), F32)),
        grid=(r // br,), in_specs=[row, row, pl.BlockSpec((1, d), lambda i: (0, 0))],
        out_specs=(row, row, pl.BlockSpec((8, d), lambda i: (0, 0)), pl.BlockSpec((8, LANES), lambda i: (0, 0))),
        compiler_params=_params("arbitrary"), name=name)(h, target, g)


def _split3(x):
    hi = x.astype(BF16)
    r1 = x - hi.astype(F32)
    mid = r1.astype(BF16)
    lo = (r1 - mid.astype(F32)).astype(BF16)
    return hi, mid, lo


def _tri_sum(tri, x):
    hi, mid, lo = _split3(x)
    dot = lambda t: jnp.dot(tri, t, preferred_element_type=F32)
    return dot(hi) + dot(mid) + dot(lo)


def _forget_cumsum(proj, b_pad, *, name):
    s = proj.shape[0]
    blk = LANES

    def body(f_ref, b_ref, c_ref):
        ri = lax.broadcasted_iota(jnp.int32, (blk, blk), 0)
        ci = lax.broadcasted_iota(jnp.int32, (blk, blk), 1)
        tri = (ci <= ri).astype(BF16)
        bias = b_ref[...]

        def step(t, carry):
            rows = pl.ds(pl.multiple_of(t * blk, blk), blk)
            z = f_ref[rows, :] + bias
            lf = jnp.minimum(z, 0.0) - jnp.log(1.0 + jnp.exp(-jnp.abs(z)))
            cs = _tri_sum(tri, lf) + carry
            c_ref[rows, :] = cs
            return cs[blk - 1:blk, :]

        lax.fori_loop(0, s // blk, step, jnp.zeros((1, blk), F32))

    return pl.pallas_call(body, out_shape=jax.ShapeDtypeStruct((s, LANES), F32), grid=(1,),
                          in_specs=[pl.BlockSpec((s, LANES), lambda i: (0, F_COL)), pl.BlockSpec((1, LANES), lambda i: (0, 0))],
                          out_specs=pl.BlockSpec((s, LANES), lambda i: (0, 0)), compiler_params=_params("arbitrary"), name=name)(proj, b_pad)


def _forget_cumsum_bwd(proj, b_pad, dcq, dck, *, name):
    s = proj.shape[0]
    blk = LANES
    nblk = s // blk

    def body(f_ref, b_ref, dcq_ref, dck_ref, dz_ref, db_ref):
        ri = lax.broadcasted_iota(jnp.int32, (blk, blk), 0)
        ci = lax.broadcasted_iota(jnp.int32, (blk, blk), 1)
        triu = (ci >= ri).astype(BF16)
        bias = b_ref[...]

        def step(t, carry):
            tail, dbs = carry
            rows = pl.ds(pl.multiple_of((nblk - 1 - t) * blk, blk), blk)
            dc = dcq_ref[rows, :] - dck_ref[rows, :]
            dlf = _tri_sum(triu, dc) + tail
            z = f_ref[rows, :] + bias
            e = jnp.exp(-jnp.abs(z))
            sig_neg = jnp.where(z >= 0.0, e, 1.0) / (1.0 + e)
            dz = dlf * sig_neg
            dz_ref[rows, :] = dz.astype(BF16)
            return dlf[0:1, :], dbs + jnp.sum(dz, axis=0, keepdims=True)

        _, dbs = lax.fori_loop(0, nblk, step, (jnp.zeros((1, blk), F32), jnp.zeros((1, blk), F32)))
        db_ref[...] = jnp.broadcast_to(dbs, db_ref.shape)

    full = pl.BlockSpec((s, LANES), lambda i: (0, 0))
    return pl.pallas_call(body, out_shape=(jax.ShapeDtypeStruct((s, LANES), BF16), jax.ShapeDtypeStruct((8, LANES), F32)), grid=(1,),
                          in_specs=[pl.BlockSpec((s, LANES), lambda i: (0, F_COL)), pl.BlockSpec((1, LANES), lambda i: (0, 0)), full, full],
                          out_specs=(full, pl.BlockSpec((8, LANES), lambda i: (0, 0))), compiler_params=_params("arbitrary"), name=name)(proj, b_pad, dcq, dck)


def _attn_fwd(q_arr, kv_arr, ck6, *, q_col, k_col, v_col, n_hp, causal, bq, bk, name, heads=None, heads_col=0):
    s = q_arr.shape[0]
    skv = kv_arr.shape[0]
    bias = ck6 is not None
    nq = s // bq
    assert not causal or bq == bk

    def body(*refs):
        q_ref, k_ref, v_ref = refs[:3]
        ck_ref = refs[3] if bias else None
        o_ref, lse_ref, heads_ref = refs[-3:]
        i = pl.program_id(1)
        lo = _lane_lo((bq, LANES))
        q = q_ref[...] * SCALE
        qh = (jnp.where(lo, q, 0.0).astype(BF16), jnp.where(lo, 0.0, q).astype(BF16))

        def block(j, carry, diagonal):
            ks = pl.ds(pl.multiple_of(j * bk, bk), bk)
            k = k_ref[ks, :].astype(BF16)
            v = v_ref[ks, :].astype(BF16)
            if diagonal:
                ok = lax.broadcasted_iota(jnp.int32, (bq, bk), 1) <= lax.broadcasted_iota(jnp.int32, (bq, bk), 0)
            out = []
            for h in range(2):
                m, l, acc = carry[3 * h:3 * h + 3]
                sc = lax.dot_general(qh[h], k, NT, preferred_element_type=F32)
                if bias:
                    sc = sc - ck_ref[0, h:h + 1, ks]
                if diagonal:
                    sc = jnp.where(ok, sc, NEG)
                mn = jnp.maximum(m, jnp.max(sc, axis=1, keepdims=True))
                p = jnp.exp(sc - mn)
                al = jnp.exp(m - mn)
                out += [mn, al * l + jnp.sum(p, axis=1, keepdims=True), al * acc + jnp.dot(p.astype(BF16), v, preferred_element_type=F32)]
            return tuple(out)

        col = lambda v_: jnp.full((bq, 1), v_, F32)
        init = (col(NEG), col(0.0), jnp.zeros((bq, LANES), F32)) * 2
        n_full = i if causal else skv // bk
        carry = lax.fori_loop(0, n_full, functools.partial(block, diagonal=False), init)
        if causal:
            carry = block(i, carry, True)
        m0, l0, a0, m1, l1, a1 = carry
        out = _pair(lo, a0 / l0, a1 / l1)
        o_ref[...] = out
        lse_ref[0] = _pair(lo, m0 + jnp.log(l0), m1 + jnp.log(l1))
        heads_ref[...] = out.astype(BF16)

    specs = [pl.BlockSpec((bq, LANES), lambda h, i: (i, q_col + h)),
             pl.BlockSpec((skv, LANES), lambda h, i: (0, k_col + h)),
             pl.BlockSpec((skv, LANES), lambda h, i: (0, v_col + h))]
    ins = [q_arr, kv_arr, kv_arr]
    if bias:
        specs += [pl.BlockSpec((1, 8, skv), lambda h, i: (h, 0, 0))]
        ins += [ck6]
    aliases = {}
    if heads is not None:
        aliases = {len(ins): 2}
        specs += [pl.BlockSpec(memory_space=pl.ANY)]
        ins += [heads]
    return pl.pallas_call(
        body, out_shape=(jax.ShapeDtypeStruct((s, n_hp * LANES), F32), jax.ShapeDtypeStruct((n_hp, s, LANES), F32),
                         jax.ShapeDtypeStruct((s, D_MODEL), BF16)),
        grid=(n_hp, nq), in_specs=specs,
        out_specs=(pl.BlockSpec((bq, LANES), lambda h, i: (i, h)), pl.BlockSpec((1, bq, LANES), lambda h, i: (h, i, 0)),
                   pl.BlockSpec((bq, LANES), lambda h, i: (i, heads_col + h))),
        input_output_aliases=aliases, compiler_params=_params("parallel", "parallel"), name=name)(*ins)


def _attn_bwd(q_arr, kv_arr, o_arr, do_arr, lse, ck6, *, q_col, k_col, v_col, o_col, n_hp, causal, bq, bk, name):
    s = q_arr.shape[0]
    skv = kv_arr.shape[0]
    bias = ck6 is not None
    nq = s // bq
    assert not causal or bq == bk

    def body(*refs):
        q_ref, k_ref, v_ref, o_ref, do_ref, lse_ref = refs[:6]
        if bias:
            ck_ref = refs[6]
            dq_ref, dk_ref, dv_ref, dcq_ref, dck_ref = refs[-5:]
        else:
            dq_ref, dk_ref, dv_ref = refs[-3:]
        j = pl.program_id(1)
        lo_q = _lane_lo((bq, LANES))
        lo_k = _lane_lo((bk, LANES))
        k = k_ref[...]
        v = v_ref[...].astype(BF16)
        kb = k.astype(BF16)
        kh = (jnp.where(lo_k, k, 0.0).astype(BF16), jnp.where(lo_k, 0.0, k).astype(BF16))
        if bias:
            pick_k = [(lax.broadcasted_iota(jnp.int32, (8, bk), 0) == h).astype(BF16) for h in range(2)]
            pick_q = [(lax.broadcasted_iota(jnp.int32, (8, bq), 0) == h).astype(BF16) for h in range(2)]

        @pl.when(j == 0)
        def _():
            dq_ref[...] = jnp.zeros_like(dq_ref)
            if bias:
                dcq_ref[...] = jnp.zeros_like(dcq_ref)

        def block(i, carry, diagonal):
            dk_acc, dv_acc, cs = carry
            qs = pl.ds(pl.multiple_of(i * bq, bq), bq)
            q = q_ref[qs, :] * SCALE
            do = do_ref[qs, :]
            dd = do * o_ref[qs, :]
            lse_i = lse_ref[0, qs, :]
            qh = (jnp.where(lo_q, q, 0.0).astype(BF16), jnp.where(lo_q, 0.0, q).astype(BF16))
            doh = (jnp.where(lo_q, do, 0.0).astype(BF16), jnp.where(lo_q, 0.0, do).astype(BF16))
            dh = (jnp.sum(jnp.where(lo_q, dd, 0.0), axis=1, keepdims=True), jnp.sum(jnp.where(lo_q, 0.0, dd), axis=1, keepdims=True))
            if diagonal:
                ok = lax.broadcasted_iota(jnp.int32, (bq, bk), 1) <= lax.broadcasted_iota(jnp.int32, (bq, bk), 0)
            dq_blk = None
            rs = None
            for h in range(2):
                sc = lax.dot_general(qh[h], kb, NT, preferred_element_type=F32)
                if bias:
                    sc = sc - ck_ref[0, h:h + 1, :]
                if diagonal:
                    sc = jnp.where(ok, sc, NEG)
                p = jnp.exp(sc - lse_i[:, h * HEAD_DIM:h * HEAD_DIM + 1])
                ds = p * (lax.dot_general(doh[h], v, NT, preferred_element_type=F32) - dh[h])
                dsb = ds.astype(BF16)
                dv_acc = dv_acc + lax.dot_general(p.astype(BF16), doh[h], TN, preferred_element_type=F32)
                dk_acc = dk_acc + lax.dot_general(dsb, qh[h], TN, preferred_element_type=F32)
                part = jnp.dot(dsb, kh[h], preferred_element_type=F32)
                dq_blk = part if dq_blk is None else dq_blk + part
                if bias:
                    cs = cs + jnp.dot(pick_q[h], dsb, preferred_element_type=F32)
                    row_sums = lax.dot_general(pick_k[h], dsb, NT, preferred_element_type=F32)
                    rs = row_sums if rs is None else rs + row_sums
            dq_ref[qs, :] += dq_blk * SCALE
            if bias:
                dcq_ref[0, :, qs] += rs
            return dk_acc, dv_acc, cs

        carry = (jnp.zeros((bk, LANES), F32), jnp.zeros((bk, LANES), F32), jnp.zeros((8, bk), F32))
        if causal:
            carry = block(j, carry, True)
        dk_acc, dv_acc, cs = lax.fori_loop(j + 1 if causal else 0, nq, functools.partial(block, diagonal=False), carry)
        dk_ref[...] = dk_acc.astype(BF16)
        dv_ref[...] = dv_acc.astype(BF16)
        if bias:
            dck_ref[0] = cs

    full_q = lambda c: pl.BlockSpec((s, LANES), lambda h, j: (0, c + h))
    specs = [full_q(q_col),
             pl.BlockSpec((bk, LANES), lambda h, j: (j, k_col + h)),
             pl.BlockSpec((bk, LANES), lambda h, j: (j, v_col + h)),
             full_q(0), full_q(o_col),
             pl.BlockSpec((1, s, LANES), lambda h, j: (h, 0, 0))]
    ins = [q_arr, kv_arr, kv_arr, o_arr, do_arr, lse]
    out_shape = [jax.ShapeDtypeStruct((s, n_hp * LANES), F32), jax.ShapeDtypeStruct((skv, n_hp * LANES), BF16),
                 jax.ShapeDtypeStruct((skv, n_hp * LANES), BF16)]
    out_specs = [full_q(0), pl.BlockSpec((bk, LANES), lambda h, j: (j, h)), pl.BlockSpec((bk, LANES), lambda h, j: (j, h))]
    if bias:
        specs += [pl.BlockSpec((1, 8, bk), lambda h, j: (h, 0, j))]
        ins += [ck6]
        out_shape += [jax.ShapeDtypeStruct((n_hp, 8, s), F32), jax.ShapeDtypeStruct((n_hp, 8, skv), F32)]
        out_specs += [pl.BlockSpec((1, 8, s), lambda h, j: (h, 0, 0)), pl.BlockSpec((1, 8, bk), lambda h, j: (h, 0, j))]
    return pl.pallas_call(body, out_shape=tuple(out_shape), grid=(n_hp, skv // bk), in_specs=specs, out_specs=tuple(out_specs),
                          compiler_params=_params("parallel", "arbitrary"), name=name)(*ins)


def _rope_tables(s):
    inv = 1.0 / (ROPE_THETA ** (jnp.arange(0, HEAD_DIM, 2, dtype=F32) / HEAD_DIM))
    ang = jnp.arange(s, dtype=F32)[:, None] * inv[None, :]
    cos, sin = jnp.cos(ang), jnp.sin(ang)
    return jnp.tile(cos, (1, 4)), jnp.concatenate([-sin, sin, -sin, sin], axis=1)


def _rope(x_arr, cos_t, sin_t, *, n_cols, out_dtype, br, name):
    s = x_arr.shape[0]

    def body(x_ref, c_ref, s_ref, o_ref):
        cos, sin = c_ref[...], s_ref[...]
        first = (lax.broadcasted_iota(jnp.int32, (br, LANES), 1) % HEAD_DIM) < (HEAD_DIM // 2)
        for j in range(n_cols):
            lanes = slice(j * LANES, (j + 1) * LANES)
            x = x_ref[:, lanes].astype(F32)
            swapped = jnp.where(first, pltpu.roll(x, LANES - HEAD_DIM // 2, 1), pltpu.roll(x, HEAD_DIM // 2, 1))
            o_ref[:, lanes] = (x * cos + swapped * sin).astype(o_ref.dtype)

    tab = pl.BlockSpec((br, LANES), lambda i: (i, 0))
    blk = pl.BlockSpec((br, n_cols * LANES), lambda i: (i, 0))
    return pl.pallas_call(body, out_shape=jax.ShapeDtypeStruct((s, n_cols * LANES), out_dtype), grid=(s // br,),
                          in_specs=[blk, tab, tab], out_specs=blk, compiler_params=_params("parallel"), name=name)(x_arr, cos_t, sin_t)


def _stack_heads(x):
    lo = _lane_lo(x.shape)
    return jnp.concatenate([jnp.where(lo, x, 0.0), jnp.where(lo, 0.0, x)], axis=0).astype(BF16)


def _unstack_heads(x):
    return jnp.where(_lane_lo((DIL_L, LANES)), x[:DIL_L], x[DIL_L:])


def _dil_scores(q_ref, k_ref, cur, prev, has_prev):
    qs = _stack_heads(q_ref[cur, :] * SCALE)
    kk = jnp.concatenate([k_ref[prev, :], k_ref[cur, :]], axis=0).astype(BF16)
    a = lax.broadcasted_iota(jnp.int32, (2 * DIL_L, 2 * DIL_L), 0) & (DIL_L - 1)
    c = lax.broadcasted_iota(jnp.int32, (2 * DIL_L, 2 * DIL_L), 1)
    ok = ((c < DIL_L) & (c >= a) & has_prev) | ((c >= DIL_L) & (c - DIL_L <= a))
    return qs, kk, jnp.where(ok, lax.dot_general(qs, kk, NT, preferred_element_type=F32), NEG)


def _dil_rows(t, dil):
    r, m = t % dil, t // dil
    start = m * (DIL_L * dil) + r
    prev = jnp.maximum(start - DIL_L * dil, 0)
    return pl.ds(start, DIL_L, stride=dil), pl.ds(prev, DIL_L, stride=dil), m > 0


def _softmax3(a, b, c):
    m = jnp.maximum(jnp.maximum(a, b), c)
    ea, eb, ec = jnp.exp(a - m), jnp.exp(b - m), jnp.exp(c - m)
    den = ea + eb + ec
    inv = 1.0 / den
    return ea * inv, eb * inv, ec * inv, m + jnp.log(den)


def _dil_fwd(qk_r, proj, *, name):
    s = qk_r.shape[0]
    nsub = s // DIL_L
    mb = 512

    def body(q_ref, k_ref, v_ref, mix_ref, l1_ref, l2_ref, l3_ref, heads_ref, o1_scr, o2_scr, o3_scr):
        for (_, dil), o_scr, l_ref in zip(DILATED_BRANCHES, (o1_scr, o2_scr, o3_scr), (l1_ref, l2_ref, l3_ref)):
            def step(t, carry, dil=dil, o_scr=o_scr, l_ref=l_ref):
                cur, prev, has_prev = _dil_rows(t, dil)
                _, _, sc = _dil_scores(q_ref, k_ref, cur, prev, has_prev)
                vv = jnp.concatenate([v_ref[prev, :], v_ref[cur, :]], axis=0).astype(BF16)
                m = jnp.max(sc, axis=1, keepdims=True)
                e = jnp.exp(sc - m)
                den = jnp.sum(e, axis=1, keepdims=True)
                o = jnp.dot((e * (1.0 / den)).astype(BF16), vv, preferred_element_type=F32)
                o_scr[cur, :] = _unstack_heads(o)
                l_ref[cur, :] = _unstack_heads(jnp.broadcast_to(m + jnp.log(den), (2 * DIL_L, LANES)))
                return carry

            lax.fori_loop(0, nsub, step, 0, unroll=DIL_UNROLL_FWD)

        def merge(i, carry):
            rows = pl.ds(pl.multiple_of(i * mb, mb), mb)
            wa, wb, wc, _ = _softmax3(l1_ref[rows, :], l2_ref[rows, :], l3_ref[rows, :])
            mix = wa * o1_scr[rows, :] + wb * o2_scr[rows, :] + wc * o3_scr[rows, :]
            mix_ref[rows, :] = mix
            heads_ref[rows, :] = mix.astype(BF16)
            return carry

        lax.fori_loop(0, s // mb, merge, 0)

    col = lambda arr_col: pl.BlockSpec((s, LANES), lambda h: (0, arr_col + h))
    shp = jax.ShapeDtypeStruct((s, D_MIX), F32)
    mix, l1, l2, l3, heads = pl.pallas_call(
        body, out_shape=(shp, shp, shp, shp, jax.ShapeDtypeStruct((s, D_MODEL), BF16)), grid=(N_MIX_HP,),
        in_specs=[col(0), col(N_MIX_HP), col(2 * N_MIX_HP)],
        out_specs=(col(0),) * 5, scratch_shapes=[pltpu.VMEM((s, LANES), F32)] * 3,
        compiler_params=_params("parallel"), name=name)(qk_r, qk_r, proj)
    return mix, (l1, l2, l3), heads


def _dil_bwd(qk_r, proj, mix, dheads, lses, *, name):
    s = qk_r.shape[0]
    nsub = s // DIL_L
    mb = 512

    def body(q_ref, k_ref, v_ref, mix_ref, dm_ref, l1_ref, l2_ref, l3_ref, dq_ref, dk_ref, dv_ref, lt_scr, dd_scr):
        lo = _lane_lo((DIL_L, LANES))
        lo_m = _lane_lo((mb, LANES))

        def prep(i, carry):
            rows = pl.ds(pl.multiple_of(i * mb, mb), mb)
            _, _, _, lt = _softmax3(l1_ref[rows, :], l2_ref[rows, :], l3_ref[rows, :])
            lt_scr[rows, :] = lt
            dd = dm_ref[rows, :] * mix_ref[rows, :]
            dd_scr[rows, :] = _pair(lo_m, jnp.sum(jnp.where(lo_m, dd, 0.0), axis=1, keepdims=True),
                                    jnp.sum(jnp.where(lo_m, 0.0, dd), axis=1, keepdims=True))
            zero = jnp.zeros((mb, LANES), F32)
            dq_ref[rows, :] = zero
            dk_ref[rows, :] = zero
            dv_ref[rows, :] = zero
            return carry

        lax.fori_loop(0, s // mb, prep, 0)

        for (_, dil), l_ref in zip(DILATED_BRANCHES, (l1_ref, l2_ref, l3_ref)):
            def step(t, carry, dil=dil, l_ref=l_ref):
                cur, prev, has_prev = _dil_rows(t, dil)
                qs, kk, sc = _dil_scores(q_ref, k_ref, cur, prev, has_prev)
                vv = jnp.concatenate([v_ref[prev, :], v_ref[cur, :]], axis=0).astype(BF16)
                lg = l_ref[cur, :]
                w = jnp.exp(lg - lt_scr[cur, :])
                wd = w * dd_scr[cur, :]
                column = lambda x: jnp.concatenate([x[:, 0:1], x[:, HEAD_DIM:HEAD_DIM + 1]], axis=0)
                dos = _stack_heads(w * dm_ref[cur, :])
                p = jnp.exp(sc - column(lg))
                ds = (p * (lax.dot_general(dos, vv, NT, preferred_element_type=F32) - column(wd))).astype(BF16)
                dq_ref[cur, :] += _unstack_heads(jnp.dot(ds, kk, preferred_element_type=F32)) * SCALE
                dkk = lax.dot_general(ds, qs, TN, preferred_element_type=F32)
                dvv = lax.dot_general(p.astype(BF16), dos, TN, preferred_element_type=F32)
                dk_ref[cur, :] += dkk[DIL_L:]
                dv_ref[cur, :] += dvv[DIL_L:]
                dk_ref[prev, :] += dkk[:DIL_L]
                dv_ref[prev, :] += dvv[:DIL_L]
                return carry

            lax.fori_loop(0, nsub, step, 0, unroll=DIL_UNROLL_BWD)

    col = lambda arr_col: pl.BlockSpec((s, LANES), lambda h: (0, arr_col + h))
    shp = jax.ShapeDtypeStruct((s, D_MIX), F32)
    return pl.pallas_call(
        body, out_shape=(shp, shp, shp), grid=(N_MIX_HP,),
        in_specs=[col(0), col(N_MIX_HP), col(2 * N_MIX_HP), col(0), col(0), col(0), col(0), col(0)], out_specs=(col(0),) * 3,
        scratch_shapes=[pltpu.VMEM((s, LANES), F32)] * 2,
        compiler_params=_params("parallel"), name=name)(qk_r, qk_r, proj, mix, dheads, *lses)


CONV_BR = 512
FF_CHUNK = 2 * D_FF // N_DEV
FF_HALF = N_DEV // 2
HALO = 8


def _shift_down(x, halo, k):
    row = lax.broadcasted_iota(jnp.int32, x.shape, 0)
    y = pltpu.roll(x, k, 0)
    for r in range(k):
        y = jnp.where(row == r, halo[HALO - k + r:HALO - k + r + 1, :], y)
    return y


def _shift_up(x, halo, k):
    n = x.shape[0]
    row = lax.broadcasted_iota(jnp.int32, x.shape, 0)
    y = pltpu.roll(x, n - k, 0)
    for r in range(k):
        y = jnp.where(row == n - k + r, halo[r:r + 1, :], y)
    return y


def _conv_vals(u, halo, w, b):
    s1 = _shift_down(u, halo, 1)
    s2 = _shift_down(u, halo, 2)
    return b + w[0:1, :] * s2 + w[1:2, :] * s1 + w[2:3, :] * u, s1, s2


def _conv_in_specs(order, layer):
    rc = (lambda i, j: (i, j)) if order == "rc" else (lambda j, i: (i, j))
    per = CONV_BR // HALO
    main = lambda off: pl.BlockSpec((None, CONV_BR, FF_CHUNK), lambda *g: (off + rc(*g)[1], rc(*g)[0], 0))
    halo = lambda off: pl.BlockSpec((None, HALO, FF_CHUNK), lambda *g: (off + rc(*g)[1], jnp.maximum(rc(*g)[0] * per - 1, 0), 0))
    wspec = lambda off: pl.BlockSpec((None, None, 3, FF_CHUNK), lambda *g: (layer, off + rc(*g)[1], 0, 0))
    bspec = lambda off: pl.BlockSpec((None, 1, FF_CHUNK), lambda *g: (off + rc(*g)[1], 0, 0))
    return [main(0), halo(0), main(FF_HALF), halo(FF_HALF), wspec(0), wspec(FF_HALF), bspec(0), bspec(FF_HALF)]


def _conv_fwd(u, cw, cb, layer, *, name):
    s = u.shape[1]

    def body(uv_ref, hv_ref, ug_ref, hg_ref, wv_ref, wg_ref, bv_ref, bg_ref, o_ref):
        first = pl.program_id(0) == 0
        hv = jnp.where(first, 0.0, hv_ref[...])
        hg = jnp.where(first, 0.0, hg_ref[...])
        val, _, _ = _conv_vals(uv_ref[...], hv, wv_ref[...], bv_ref[...])
        gate, _, _ = _conv_vals(ug_ref[...], hg, wg_ref[...], bg_ref[...])
        o_ref[...] = (gate / (1.0 + jnp.exp(-gate)) * val).astype(BF16)

    return pl.pallas_call(body, out_shape=jax.ShapeDtypeStruct((FF_HALF, s, FF_CHUNK), BF16), grid=(s // CONV_BR, FF_HALF),
                          in_specs=_conv_in_specs("rc", layer), out_specs=pl.BlockSpec((None, CONV_BR, FF_CHUNK), lambda i, j: (j, i, 0)),
                          compiler_params=_params("parallel", "parallel"), name=name)(u, u, u, u, cw, cw, cb, cb)


def _swiglu_bwd(val, gate, da):
    sg = 1.0 / (1.0 + jnp.exp(-gate))
    return da * (gate * sg), da * val * (sg * (1.0 + gate * (1.0 - sg)))


def _conv_bwd(u, cw, cb, da, layer, *, name):
    s = u.shape[1]
    nrow = s // CONV_BR
    per = CONV_BR // HALO

    def body(uv_ref, hv_ref, ug_ref, hg_ref, wv_ref, wg_ref, bv_ref, bg_ref, da_ref, nv_ref, ng_ref, nda_ref, du_ref, dwb_ref):
        i = pl.program_id(1)
        first, last = i == 0, i == nrow - 1
        hv = jnp.where(first, 0.0, hv_ref[...])
        hg = jnp.where(first, 0.0, hg_ref[...])
        uv, ug = uv_ref[...], ug_ref[...]
        wv, wg, bv, bg = wv_ref[...], wg_ref[...], bv_ref[...], bg_ref[...]
        val, v1, v2 = _conv_vals(uv, hv, wv, bv)
        gate, g1, g2 = _conv_vals(ug, hg, wg, bg)
        dval, dgate = _swiglu_bwd(val, gate, da_ref[...])
        val_n, _, _ = _conv_vals(nv_ref[...], uv[CONV_BR - HALO:, :], wv, bv)
        gate_n, _, _ = _conv_vals(ng_ref[...], ug[CONV_BR - HALO:, :], wg, bg)
        dval_n, dgate_n = _swiglu_bwd(val_n, gate_n, nda_ref[...])
        dval_n = jnp.where(last, 0.0, dval_n)
        dgate_n = jnp.where(last, 0.0, dgate_n)
        back = lambda dc, dc_n, w: w[2:3, :] * dc + w[1:2, :] * _shift_up(dc, dc_n, 1) + w[0:1, :] * _shift_up(dc, dc_n, 2)
        du_ref[0] = back(dval, dval_n, wv).astype(BF16)
        du_ref[1] = back(dgate, dgate_n, wg).astype(BF16)

        @pl.when(first)
        def _():
            dwb_ref[...] = jnp.zeros_like(dwb_ref)

        cs = lambda t: jnp.sum(t, axis=0, keepdims=True)
        r8 = lax.broadcasted_iota(jnp.int32, (8, FF_CHUNK), 0)
        rows4 = lambda a, b, c, d: jnp.where(r8 == 0, a, jnp.where(r8 == 1, b, jnp.where(r8 == 2, c, jnp.where(r8 == 3, d, 0.0))))
        dwb_ref[0] += rows4(cs(dval * v2), cs(dval * v1), cs(dval * uv), cs(dval))
        dwb_ref[1] += rows4(cs(dgate * g2), cs(dgate * g1), cs(dgate * ug), cs(dgate))

    nxt = lambda off: pl.BlockSpec((None, HALO, FF_CHUNK), lambda j, i: (off + j, jnp.minimum((i + 1) * per, nrow * per - 1), 0))
    specs = _conv_in_specs("cr", layer) + [pl.BlockSpec((None, CONV_BR, FF_CHUNK), lambda j, i: (j, i, 0)), nxt(0), nxt(FF_HALF), nxt(0)]
    return pl.pallas_call(
        body, out_shape=(jax.ShapeDtypeStruct((2, FF_HALF, s, FF_CHUNK), BF16), jax.ShapeDtypeStruct((2, FF_HALF, 8, FF_CHUNK), F32)),
        grid=(FF_HALF, nrow), in_specs=specs,
        out_specs=(pl.BlockSpec((2, None, CONV_BR, FF_CHUNK), lambda j, i: (0, j, i, 0)),
                   pl.BlockSpec((2, None, 8, FF_CHUNK), lambda j, i: (0, j, 0, 0))),
        compiler_params=_params("parallel", "arbitrary"), name=name)(u, u, u, u, cw, cw, cb, cb, da, u, u, da)


def _rows_of(r):
    return lambda ref, idx: ref.at[:, pl.ds(idx * r, r), :]


def _slot1(ref, idx):
    return ref.at[:, idx]


def _slot0(ref, idx):
    return ref.at[idx]


def _all_gather(shards, full_shapes, places, *, name):
    n = len(shards)

    def body(*refs):
        ins, outs = refs[:n], refs[n:2 * n]
        send_sems, recv_sems, local_sems = refs[2 * n:]
        mx, my, mc = lax.axis_index("x"), lax.axis_index("y"), lax.axis_index("c")
        me, sibling = (mx, my, mc), (mx, my, 1 - mc)
        chips = [(1 - mx, my), (mx, 1 - my), (1 - mx, 1 - my)]

        def win(t, px, py, pc):
            return places[t](outs[t], 4 * px + 2 * py + pc)

        def copy(t, k, block, to, src=None):
            return pltpu.make_async_remote_copy(src_ref=win(t, *block) if src is None else src, dst_ref=win(t, *block),
                                                send_sem=send_sems.at[t, k], recv_sem=recv_sems.at[t, k], device_id=to, device_id_type=MESH)

        mine = [pltpu.make_async_copy(ins[t], win(t, *me), local_sems.at[t]) for t in range(n)]
        for cp in mine:
            cp.start()
        first = []
        for t in range(n):
            first += [copy(t, 0, me, sibling, src=ins[t])] + [copy(t, 1 + j, me, (*chip, mc), src=ins[t]) for j, chip in enumerate(chips)]
        for cp in first:
            cp.start()
        passed = []
        for j, chip in enumerate(chips):
            for t in range(n):
                copy(t, 1 + j, (*chip, mc), me).wait_recv()
                fwd = copy(t, 4 + j, (*chip, mc), sibling)
                fwd.start()
                passed.append(fwd)
        for t in range(n):
            copy(t, 0, sibling, me).wait_recv()
            for j, chip in enumerate(chips):
                copy(t, 4 + j, (*chip, 1 - mc), me).wait_recv()
        for cp in first + passed:
            cp.wait_send()
        for cp in mine:
            cp.wait()

    hbm = pl.BlockSpec(memory_space=pl.ANY)
    return pl.pallas_call(
        body, out_shape=tuple(jax.ShapeDtypeStruct(s, x.dtype) for s, x in zip(full_shapes, shards)),
        in_specs=[hbm] * n, out_specs=(hbm,) * n,
        scratch_shapes=[pltpu.SemaphoreType.DMA((n, 7)), pltpu.SemaphoreType.DMA((n, 7)), pltpu.SemaphoreType.DMA((n,))],
        name=name)(*shards)


FLIPS = [(fx, fy, fc) for fx in (0, 1) for fy in (0, 1) for fc in (0, 1)][1:]


def _exchange_copies(kind, places, src, land, send_sems, recv_sems, local_sems):
    mx, my, mc = lax.axis_index("x"), lax.axis_index("y"), lax.axis_index("c")
    me = 4 * mx + 2 * my + mc
    n = len(src)
    local, remote = [], []
    for t in range(n):
        if kind == "gather":
            local.append(pltpu.make_async_copy(src[t], places[t](land[t], me), local_sems.at[t]))
        else:
            local.append(pltpu.make_async_copy(places[t](src[t], me), land[t].at[me], local_sems.at[t]))
    for k, (fx, fy, fc) in enumerate(FLIPS):
        px, py, pc = mx ^ fx, my ^ fy, mc ^ fc
        peer = 4 * px + 2 * py + pc
        for t in range(n):
            sems = dict(send_sem=send_sems.at[7 * t + k], recv_sem=recv_sems.at[7 * t + k], device_id=(px, py, pc), device_id_type=MESH)
            if kind == "gather":
                pair = [(src[t], places[t](land[t], me)), (src[t], places[t](land[t], peer))]
            else:
                pair = [(places[t](src[t], peer), land[t].at[me]), (places[t](src[t], peer), land[t].at[peer])]
            remote.append([functools.partial(pltpu.make_async_remote_copy, src_ref=s_, dst_ref=d_, **sems) for s_, d_ in pair])
    return local, remote


HBM_SPEC = pl.BlockSpec(memory_space=pltpu.HBM)
SEM_SPEC = pl.BlockSpec(memory_space=pltpu.SEMAPHORE)
SIDE_EFFECT = pltpu.SideEffectType.DATAFLOW_SIDE_EFFECTING


def _exchange_start(kind, srcs, land_shapes, places, after, *, name):
    n = len(srcs)

    def body(*refs):
        src, land = refs[:n], refs[n:2 * n]
        send_sems, recv_sems, local_sems = refs[2 * n + 1:2 * n + 4]
        token = refs[-1]
        local, remote = _exchange_copies(kind, places, src, land, send_sems, recv_sems, local_sems)
        for cp in local:
            cp.start()
        for send, _ in remote:
            send().start()
        token[...] = jnp.zeros_like(token)

    hbm = lambda t: pltpu.with_memory_space_constraint(t, pltpu.HBM)
    lands = [hbm(lax.empty(tuple(s), x.dtype)) for s, x in zip(land_shapes, srcs)]
    out_shape = (pltpu.SemaphoreType.DMA((7 * n,)), pltpu.SemaphoreType.DMA((7 * n,)), pltpu.SemaphoreType.DMA((n,)),
                 *[pltpu.HBM(x.shape, x.dtype) for x in srcs], *[pltpu.HBM(tuple(s), x.dtype) for s, x in zip(land_shapes, srcs)],
                 jax.ShapeDtypeStruct((8, LANES), F32))
    outs = pl.pallas_call(
        body, name=name, out_shape=out_shape, in_specs=[HBM_SPEC] * (2 * n) + [pl.BlockSpec(memory_space=pl.ANY)],
        out_specs=(SEM_SPEC, SEM_SPEC, SEM_SPEC) + (HBM_SPEC,) * (2 * n) + (pl.BlockSpec(memory_space=pltpu.VMEM),),
        input_output_aliases={i: 3 + i for i in range(2 * n)},
        compiler_params=pltpu.CompilerParams(has_side_effects=SIDE_EFFECT))(*[hbm(x) for x in srcs], *lands, after)
    return dict(sems=outs[:3], src=outs[3:3 + n], land=outs[3 + n:3 + 2 * n], token=outs[-1])


def _exchange_wait(kind, started, places, after, *, name):
    n = len(started["src"])

    def body(*refs):
        src, land = refs[:n], refs[n:2 * n]
        send_sems, recv_sems, local_sems = refs[2 * n:2 * n + 3]
        local, remote = _exchange_copies(kind, places, src, land, send_sems, recv_sems, local_sems)
        for cp in local:
            cp.wait()
        for send, arrival in remote:
            send().wait_send()
            arrival().wait_recv()

    out_shape = tuple(pltpu.HBM(x.shape, x.dtype) for x in started["src"]) + tuple(pltpu.HBM(x.shape, x.dtype) for x in started["land"])
    outs = pl.pallas_call(
        body, name=name, out_shape=out_shape,
        in_specs=[HBM_SPEC] * (2 * n) + [SEM_SPEC] * 3 + [pl.BlockSpec(memory_space=pl.ANY)], out_specs=(HBM_SPEC,) * (2 * n),
        input_output_aliases={i: i for i in range(2 * n)},
        compiler_params=pltpu.CompilerParams(has_side_effects=SIDE_EFFECT))(*started["src"], *started["land"], *started["sems"], after)
    return list(outs[n:])


def _adamw(parts, w, m, v, *, br, name):
    layers, r, wd = w.shape
    assert len(parts) == layers

    def body(*refs):
        p_refs = refs[:layers]
        w_ref, m_ref, v_ref, g_ref, d_ref, nm_ref, nv_ref = refs[layers:]
        for k in range(layers):
            @pl.when(pl.program_id(0) == k)
            def _(p_ref=p_refs[k]):
                g = p_ref[0].astype(F32)
                for dev in range(1, N_DEV):
                    g = g + p_ref[dev].astype(F32)
                mm = ADAM_B1 * m_ref[...] + (1.0 - ADAM_B1) * g
                vv = ADAM_B2 * v_ref[...] + (1.0 - ADAM_B2) * (g * g)
                m_hat = mm / (1.0 - ADAM_B1 ** ADAM_STEP)
                v_hat = vv / (1.0 - ADAM_B2 ** ADAM_STEP)
                g_ref[...] = g
                d_ref[...] = -ADAM_LR * (m_hat / (jnp.sqrt(v_hat) + ADAM_EPS) + ADAM_WD * w_ref[...])
                nm_ref[...] = mm
                nv_ref[...] = vv

    p_spec = lambda k: pl.BlockSpec((N_DEV, None, br, wd), lambda l, i: (0, 0, jnp.where(l == k, i, 0), 0))
    blk = pl.BlockSpec((None, br, wd), lambda l, i: (l, i, 0))
    shp = jax.ShapeDtypeStruct((layers, r, wd), F32)
    return pl.pallas_call(body, out_shape=(shp, shp, shp, shp), grid=(layers, r // br),
                          in_specs=[p_spec(k) for k in range(layers)] + [blk, blk, blk], out_specs=(blk, blk, blk, blk),
                          compiler_params=_params("arbitrary", "arbitrary"), name=name)(*parts, w, m, v)


SMALL = ("norm_mix", "norm_mem", "norm_ffn", "b_forget", "conv_b", "norm_final")


def _pack(tensors):
    flat = jnp.concatenate([t.reshape(-1) for t in tensors])
    rows = -(-flat.shape[0] // (PACK_W * PACK_ROW_ALIGN)) * PACK_ROW_ALIGN
    flat = jnp.pad(flat, (0, rows * PACK_W - flat.shape[0]))
    return flat.reshape(1, rows, PACK_W)


def _unpack(buf, shapes):
    flat = buf.reshape(-1)
    out, off = [], 0
    for shp in shapes:
        n = math.prod(shp)
        out.append(flat[off:off + n].reshape(tuple(shp)))
        off += n
    return out


def _fox_permute(w):
    pad = jnp.zeros(w.shape[:-1] + (FOX_P - FOX_IN,), w.dtype)
    return jnp.concatenate([w[..., :3 * D_MIX], w[..., 3 * D_MIX + N_MIX_HEADS:], w[..., 3 * D_MIX:3 * D_MIX + N_MIX_HEADS], pad], axis=-1)


def _fox_unpermute(w):
    return jnp.concatenate([w[..., :3 * D_MIX], w[..., DIL_IN:DIL_IN + N_MIX_HEADS], w[..., 3 * D_MIX:DIL_IN]], axis=-1)


def _bias_layout(c):
    s = c.shape[0]
    ct = c[:, :N_MIX_HEADS].T.reshape(N_MIX_HP, 2, s)
    return jnp.pad(ct, ((0, 0), (0, 6), (0, 0)))


def _bias_grad(dck6):
    s = dck6.shape[2]
    dk = dck6[:, :2, :].reshape(N_MIX_HEADS, s).T
    return jnp.pad(dk, ((0, 0), (0, LANES - N_MIX_HEADS)))


def _device_step(x, mem, target, small, get_weights, put_grads):
    s = x.shape[0]
    mt = mem.shape[0]
    bq = 512
    cos_t, sin_t = _rope_tables(s)
    row = lambda t, l: t[l][None, :]
    saved = []
    h = x
    cb8 = small["conv_b"].reshape(DEPTH, N_DEV, 1, FF_CHUNK)
    for l in range(DEPTH):
        kind, slot = l % 2, l // 2
        wl = dict(get_weights(l, "attn", h))
        if l == 0:
            xn = _rmsnorm_fwd(h, row(small["norm_mix"], l), br=512, name=f"norm_mix_fwd{l}")
        mn = _rmsnorm_fwd(mem, row(small["norm_mem"], l), br=mt, name=f"norm_mem_fwd{l}")
        proj = _mm(xn, wl["w_in"], "nn" if kind == 0 else "nt", tm=1024, tn=384 if kind == 0 else 512, layer=0, name=f"in_proj{l}")
        kvm = _mm(mn, wl["w_mem_kv"], "nn", tm=mt, tn=512, layer=0, name=f"mem_kv{l}")
        st = dict(h=h, xn=xn, mn=mn, proj=proj, kvm=kvm, w=wl)
        if kind == 0:
            b_pad = jnp.pad(small["b_forget"][slot], (0, LANES - N_MIX_HEADS))[None, :]
            c = _forget_cumsum(proj, b_pad, name=f"forget_cumsum{l}")
            ck6 = _bias_layout(c)
            mix, lse, heads = _attn_fwd(proj, proj, ck6, q_col=0, k_col=N_MIX_HP, v_col=2 * N_MIX_HP, n_hp=N_MIX_HP,
                                        causal=True, bq=min(s, 1024), bk=min(s, 1024), name=f"fox_fwd{l}")
            st.update(b_pad=b_pad, ck6=ck6, mix=mix, lse=lse)
        else:
            qk_r = _rope(proj, cos_t, sin_t, n_cols=2 * N_MIX_HP, out_dtype=F32, br=512, name=f"rope_fwd{l}")
            mix, lses, heads = _dil_fwd(qk_r, proj, name=f"dil_fwd{l}")
            st.update(qk_r=qk_r, lses=lses, mix=mix)
        mo, lse_m, heads = _attn_fwd(proj, kvm, None, q_col=QM_COL, k_col=0, v_col=N_MEM_HP, n_hp=N_MEM_HP, causal=False,
                                     bq=min(s, 2048), bk=mt, heads=heads, heads_col=N_MIX_HP, name=f"mem_fwd{l}")
        h1, xf = _mm(heads, wl["w_out"], "nn", tm=1024, tn=D_MODEL, res=h, layer=0, norm_gain=row(small["norm_ffn"], l),
                     name=f"out_proj{l}")
        wl.update(get_weights(l, "ffn", xf))
        u = _mm(xf, wl["w_up"], "nt", tm=1024, tn=FF_CHUNK, layer=0, chunk="b", name=f"up_proj{l}")
        a = _conv_fwd(u, wl["conv_w"], cb8[l], 0, name=f"conv_fwd{l}")
        st.update(mo=mo, lse_m=lse_m, heads=heads, h1=h1, xf=xf, u=u, a=a)
        saved.append(st)
        if l + 1 < DEPTH:
            h, xn = _mm(a, wl["w_down"], "nn", tm=512, tn=D_MODEL, res=h1, layer=0, chunk="reduce",
                        norm_gain=row(small["norm_mix"], l + 1), name=f"down_proj{l}")
        else:
            h = _mm(a, wl["w_down"], "nn", tm=1024, tn=512, res=h1, layer=0, chunk="reduce", name=f"down_proj{l}")

    dh, dhb, dg_final, loss = _loss_head(h, target, small["norm_final"][None, :], br=512, name="loss_head")
    gs = {k: [None] * DEPTH for k in ("norm_mix", "norm_mem", "norm_ffn", "conv_b")}
    gs["b_forget"] = [None] * 2
    dep = 0.0
    for l in reversed(range(DEPTH)):
        st = saved[l]
        wl = st["w"]
        gw = {}
        kind, slot = l % 2, l // 2
        da = _mm(dhb, wl["w_down"], "nt", tm=1024, tn=FF_CHUNK, layer=0, chunk="b", name=f"down_dx{l}")
        gw["w_down"] = _mm(st["a"], dhb, "tn", tm=FF_CHUNK, tn=512, out_dtype=BF16, chunk="a", name=f"down_dw{l}")
        du, dwb = _conv_bwd(st["u"], wl["conv_w"], cb8[l] + dep, da, 0, name=f"conv_bwd{l}")
        du = du.reshape(N_DEV, s, FF_CHUNK)
        dwb = dwb.reshape(N_DEV, 8, FF_CHUNK)
        gw["conv_w"] = dwb
        gs["conv_b"][l] = dwb[:, 3, :].reshape(-1)
        dh1, dh1b, dgf = _mm(du, wl["w_up"], "nn", tm=256, tn=D_MODEL, layer=0, chunk="reduce",
                             norm_bwd=(st["h1"], row(small["norm_ffn"], l), dh), name=f"up_dx{l}")
        gw["w_up"] = _mm(du, st["xf"], "tn", tm=FF_CHUNK, tn=512, out_dtype=BF16, chunk="a", name=f"up_dw{l}")
        gs["norm_ffn"][l] = dgf[0]
        dheads = _mm(dh1b, wl["w_out"], "nt", tm=1024, tn=512, layer=0, name=f"out_dx{l}")
        gw["w_out"] = _mm(st["heads"], dh1b, "tn", tm=512, tn=512, out_dtype=BF16, name=f"out_dw{l}")
        dqm, dkm, dvm = _attn_bwd(st["proj"], st["kvm"], st["mo"], dheads, st["lse_m"], None, q_col=QM_COL, k_col=0,
                                  v_col=N_MEM_HP, o_col=N_MIX_HP, n_hp=N_MEM_HP, causal=False, bq=min(s, 1024), bk=mt,
                                  name=f"mem_bwd{l}")
        dkvm = jnp.concatenate([dkm, dvm], axis=1)
        gw["w_mem_kv"] = _mm(st["mn"], dkvm, "tn", tm=512, tn=512, out_dtype=BF16, name=f"mem_kv_dw{l}")
        dmn = _mm(dkvm, wl["w_mem_kv"], "nt", tm=mt, tn=512, layer=0, name=f"mem_kv_dx{l}")
        dep_early = put_grads(l, "ffn", gw)
        _, _, dgm = _rmsnorm_bwd(mem, dmn, row(small["norm_mem"], l), None, br=mt, name=f"norm_mem_bwd{l}")
        gs["norm_mem"][l] = dgm[0]
        if kind == 0:
            dq, dk, dv, dcq6, dck6 = _attn_bwd(st["proj"], st["proj"], st["mix"], dheads, st["lse"], st["ck6"] + dep_early, q_col=0,
                                               k_col=N_MIX_HP, v_col=2 * N_MIX_HP, o_col=0, n_hp=N_MIX_HP, causal=True,
                                               bq=bq, bk=bq, name=f"fox_bwd{l}")
            dz, db = _forget_cumsum_bwd(st["proj"], st["b_pad"], _bias_grad(dcq6), _bias_grad(dck6), name=f"forget_cumsum_bwd{l}")
            gs["b_forget"][slot] = db[0, :N_MIX_HEADS]
            dproj = jnp.concatenate([dq.astype(BF16), dk, dv, dqm.astype(BF16), dz], axis=1)
        else:
            dq_r, dk_r, dv = _dil_bwd(st["qk_r"], st["proj"], st["mix"], dheads, st["lses"], name=f"dil_bwd{l}")
            dq = _rope(dq_r, cos_t + dep_early, -sin_t, n_cols=N_MIX_HP, out_dtype=BF16, br=512, name=f"rope_bwd_q{l}")
            dk = _rope(dk_r, cos_t, -sin_t, n_cols=N_MIX_HP, out_dtype=BF16, br=512, name=f"rope_bwd_k{l}")
            dproj = jnp.concatenate([dq, dk, dv.astype(BF16), dqm.astype(BF16)], axis=1)
        if kind == 0:
            gw["w_in"] = _mm(st["xn"], dproj, "tn", tm=512, tn=384, out_dtype=BF16, name=f"in_dw{l}")
        else:
            gw["w_in"] = _mm(dproj, st["xn"], "tn", tm=512, tn=512, out_dtype=BF16, name=f"in_dw{l}")
        dh, dhb, dgx = _mm(dproj, wl["w_in"], "nt" if kind == 0 else "nn", tm=512, tn=D_MODEL, layer=0,
                           norm_bwd=(st["h"], row(small["norm_mix"], l), dh1), name=f"in_dx{l}")
        gs["norm_mix"][l] = dgx[0]
        dep = put_grads(l, "attn", gw)

    grads_s = {k: jnp.stack(v) for k, v in gs.items()}
    grads_s["norm_final"] = dg_final[0]
    return loss[0, 0], dh, grads_s


def kernel(x, mem, norm_mix, norm_mem, norm_ffn, w_in_fox, b_forget, w_in_dil, w_mem_kv, w_out, w_up, conv_w, conv_b, w_down, norm_final, loss_target, m_norm_mix, m_norm_mem, m_norm_ffn, m_w_in_fox, m_b_forget, m_w_in_dil, m_w_mem_kv, m_w_out, m_w_up, m_conv_w, m_conv_b, m_w_down, m_norm_final, v_norm_mix, v_norm_mem, v_norm_ffn, v_w_in_fox, v_b_forget, v_w_in_dil, v_w_mem_kv, v_w_out, v_w_up, v_conv_w, v_conv_b, v_w_down, v_norm_final):
    names = ["norm_mix", "norm_mem", "norm_ffn", "w_in_fox", "b_forget", "w_in_dil", "w_mem_kv", "w_out", "w_up", "conv_w", "conv_b",
             "w_down", "norm_final"]
    w = dict(zip(names, (norm_mix, norm_mem, norm_ffn, w_in_fox, b_forget, w_in_dil, w_mem_kv, w_out, w_up, conv_w, conv_b, w_down, norm_final)))
    m = dict(zip(names, (m_norm_mix, m_norm_mem, m_norm_ffn, m_w_in_fox, m_b_forget, m_w_in_dil, m_w_mem_kv, m_w_out, m_w_up, m_conv_w,
                         m_conv_b, m_w_down, m_norm_final)))
    v = dict(zip(names, (v_norm_mix, v_norm_mem, v_norm_ffn, v_w_in_fox, v_b_forget, v_w_in_dil, v_w_mem_kv, v_w_out, v_w_up, v_conv_w,
                         v_conv_b, v_w_down, v_norm_final)))
    big = ("w_in_fox", "w_in_dil", "w_mem_kv", "w_out", "w_up", "w_down", "conv_w")
    small_shapes = [w[k].shape for k in SMALL]
    dil_c = w_in_dil.shape[2]
    rows = {k: w[k].shape[1] for k in ("w_in_fox", "w_mem_kv", "w_out", "w_down")}

    def places(l):
        w_in_place = _rows_of(rows["w_in_fox"]) if l % 2 == 0 else _slot1
        return [w_in_place, _rows_of(rows["w_mem_kv"]), _rows_of(rows["w_out"]), _slot1, _rows_of(rows["w_down"]), _slot1]

    def full_shapes(l):
        w_in_shape = (1, D_MODEL, FOX_P) if l % 2 == 0 else (1, N_DEV, dil_c, D_MODEL)
        return [w_in_shape, (1, D_MODEL, 2 * D_MEMQ), (1, D_MODEL, D_MODEL), (1, N_DEV, FF_CHUNK, D_MODEL), (1, D_FF, D_MODEL),
                (1, N_DEV, 3, FF_CHUNK)]

    transposed = lambda t: jnp.swapaxes(t, 1, 2)
    cast = {"w_in_fox": _fox_permute(w_in_fox).astype(BF16), "w_in_dil": transposed(w_in_dil).astype(BF16),
            "w_mem_kv": w_mem_kv.astype(BF16), "w_out": w_out.astype(BF16), "w_up": transposed(w_up).astype(BF16),
            "w_down": w_down.astype(BF16), "conv_w": conv_w}
    everything = (0, 1, 2, 3, 4, 5)
    gather_groups = {l: ((0, 1, 2), (3, 4, 5)) if l == 0 else (everything,) for l in range(DEPTH)}
    scatter_groups = {l: ((1, 2, 3, 4, 5), (0,)) if l == 0 else (everything,) for l in range(DEPTH)}
    pick = lambda seq, group: [seq[i] for i in group]
    tag = lambda l, group: f"{l}" + ("" if group == everything else "_" + "".join(str(i) for i in group))

    gathers, after = {}, norm_final
    for l in range(DEPTH):
        w_in_shard = cast["w_in_fox" if l % 2 == 0 else "w_in_dil"][l // 2][None]
        shards = [w_in_shard] + [cast[k][l][None] for k in ("w_mem_kv", "w_out", "w_up", "w_down", "conv_w")]
        for group in gather_groups[l]:
            gathers[l, group] = _exchange_start("gather", pick(shards, group), pick(full_shapes(l), group), pick(places(l), group), after,
                                                name=f"weights_gather_start{tag(l, group)}")
            after = gathers[l, group]["token"]
    started = sum(g["token"][0, 0] for g in gathers.values())
    small = {k: w[k] for k in SMALL}
    small["norm_mix"] = norm_mix + started
    landed = {}

    def get_weights(l, part, h):
        group = [g for g in gather_groups[l] if (0 if part == "attn" else 3) in g][0]
        if (l, group) not in landed:
            lands = _exchange_wait("gather", gathers[l, group], pick(places(l), group), h, name=f"weights_gather_wait{tag(l, group)}")
            landed[l, group] = dict(zip(group, lands))
        got = landed[l, group]
        if part == "ffn":
            return dict(w_up=got[3], w_down=got[4].reshape(1, FF_HALF, FF_CHUNK, D_MODEL), conv_w=got[5])
        w_in = got[0] if l % 2 == 0 else got[0].reshape(1, N_DEV * dil_c, D_MODEL)
        return dict(w_in=w_in, w_mem_kv=got[1], w_out=got[2])

    scatters, pending = {}, {}

    def put_grads(l, part, g):
        pending.setdefault(l, {}).update(g)
        if part == "ffn" and len(scatter_groups[l]) == 1:
            return 0.0
        group = scatter_groups[l][0 if part == "ffn" else -1]
        have = pending[l]
        srcs = {3: lambda: have["w_up"][None], 4: lambda: have["w_down"].reshape(1, D_FF, D_MODEL), 5: lambda: have["conv_w"][None],
                1: lambda: have["w_mem_kv"][None], 2: lambda: have["w_out"][None]}
        if l % 2 == 0:
            srcs[0] = lambda: have["w_in"][None]
        else:
            srcs[0] = lambda: have["w_in"].reshape(1, N_DEV, dil_c, D_MODEL)
        shard_shapes = [cast["w_in_fox" if l % 2 == 0 else "w_in_dil"][l // 2].shape] + \
            [cast[k][l].shape for k in ("w_mem_kv", "w_out", "w_up", "w_down")] + [(8, FF_CHUNK)]
        sources = [srcs[i]() for i in group]
        scatters[l, group] = _exchange_start("scatter", sources, [(N_DEV, 1) + tuple(s) for s in pick(shard_shapes, group)],
                                             pick(places(l), group), sources[0], name=f"grads_scatter_start{tag(l, group)}")
        return scatters[l, group]["token"][0, 0]

    loss, grad_x, gs = _device_step(x[0], mem[0], loss_target[0], small, get_weights, put_grads)

    recv = {}

    def wait_scatter(l, group, after_):
        lands = _exchange_wait("scatter", scatters[l, group], pick(places(l), group), after_, name=f"grads_scatter_wait{tag(l, group)}")
        recv.setdefault(l, {}).update(zip(group, lands))

    for l in reversed(range(1, DEPTH)):
        wait_scatter(l, everything, grad_x)
    wait_scatter(0, scatter_groups[0][0], grad_x)
    s_pack = _pack([gs[k] for k in SMALL])
    (s_recv,) = _all_gather([s_pack], [(N_DEV,) + s_pack.shape], [_slot0], name="small_grads_all_gather")

    layer_tensors = ("w_in", "w_mem_kv", "w_out", "w_up", "w_down", "conv_w")
    layer_parts = lambda k: [recv[l][layer_tensors.index(k)] for l in range(DEPTH)]
    to_local = {k: (lambda t: t) for k in big}
    to_local["w_in_fox"] = _fox_permute
    to_local["w_in_dil"] = to_local["w_up"] = transposed
    from_local = {k: (lambda t: t) for k in big}
    from_local["w_in_fox"] = _fox_unpermute
    from_local["w_in_dil"] = from_local["w_up"] = transposed
    blocks = {"w_in_fox": rows["w_in_fox"], "w_in_dil": dil_c, "w_mem_kv": rows["w_mem_kv"], "w_out": rows["w_out"], "w_up": FF_CHUNK // 4,
              "w_down": rows["w_down"] // 2, "conv_w": 3}
    outs = {}

    def update(k, parts):
        f = to_local[k]
        outs[k] = [from_local[k](t) for t in _adamw(parts, f(w[k]), f(m[k]), f(v[k]), br=blocks[k], name=f"adamw_{k}")]

    update("w_in_dil", [recv[l][0] for l in range(1, DEPTH, 2)])
    update("w_up", layer_parts("w_up"))
    update("w_down", layer_parts("w_down"))
    update("conv_w", [p[:, :, :3, :] for p in layer_parts("conv_w")])
    update("w_mem_kv", layer_parts("w_mem_kv"))
    update("w_out", layer_parts("w_out"))
    wait_scatter(0, scatter_groups[0][-1], outs["w_out"][1])
    update("w_in_fox", [recv[l][0] for l in range(0, DEPTH, 2)])
    small_outs = _adamw([s_recv], _pack([w[k] for k in SMALL]), _pack([m[k] for k in SMALL]), _pack([v[k] for k in SMALL]),
                        br=s_pack.shape[1], name="adamw_small")
    res = []
    for i, os_ in enumerate(small_outs):
        d = {k: outs[k][i] for k in big}
        d.update(zip(SMALL, _unpack(os_, small_shapes)))
        res.append([d[k] for k in names])
    loss = lax.psum(loss, ("x", "y", "c"))
    return (loss, grad_x[None], *res[0], *res[1], *res[2], *res[3])
```

```python
import functools
import math

import jax
import jax.numpy as jnp
from jax import lax
from jax.experimental import pallas as pl
from jax.experimental.pallas import tpu as pltpu

F32 = jnp.float32
BF16 = jnp.bfloat16

D_MODEL = 1024
HEAD_DIM = 64
N_MIX_HEADS = 12
N_MEM_HEADS = 4
D_MIX = N_MIX_HEADS * HEAD_DIM
D_MEMQ = N_MEM_HEADS * HEAD_DIM
D_FF = 2816
DEPTH = 4
FOX_IN = 3 * D_MIX + N_MIX_HEADS + D_MEMQ
DIL_IN = 3 * D_MIX + D_MEMQ
LANES = 128
FOX_P = DIL_IN + LANES
N_MIX_HP = D_MIX // LANES
N_MEM_HP = D_MEMQ // LANES
QM_COL = 3 * N_MIX_HP
F_COL = DIL_IN // LANES
DILATED_BRANCHES = ((128, 1), (512, 4), (2048, 16))
DIL_L = 128
DIL_UNROLL_FWD = 8
DIL_UNROLL_BWD = 4
ROPE_THETA = 10000.0
NORM_EPS = 1e-6
NEG = -1e30
SCALE = HEAD_DIM ** -0.5
N_DEV = 8

ADAM_LR = 0.001
ADAM_B1 = 0.9
ADAM_B2 = 0.999
ADAM_EPS = 1e-08
ADAM_WD = 0.01
ADAM_STEP = 10

VMEM_LIMIT = 56 * 1024 * 1024
PACK_W = 1024
PACK_ROW_ALIGN = 8

MESH = pl.DeviceIdType.MESH
NT = (((1,), (1,)), ((), ()))
NN = (((1,), (0,)), ((), ()))
TN = (((0,), (0,)), ((), ()))


def _params(*sem):
    return pltpu.CompilerParams(dimension_semantics=sem, vmem_limit_bytes=VMEM_LIMIT)


def _lane_lo(shape):
    return lax.broadcasted_iota(jnp.int32, shape, len(shape) - 1) < HEAD_DIM


def _pair(lo, a, b):
    return jnp.where(lo, a, b)


def _mm(a, b, mode, *, tm, tn, name, out_dtype=F32, res=None, layer=None, chunk=None, norm_gain=None, norm_bwd=None):
    lead = () if layer is None else (layer,)
    nl = (None,) * len(lead)
    bs = b.shape[len(lead):]
    dims = {"nn": NN, "nt": NT, "tn": TN}[mode]
    reduce_n = 0
    if chunk is None:
        (m, k) = a.shape[::-1] if mode == "tn" else a.shape
        n = bs[0] if mode == "nt" else bs[1]
        grid = (m // tm, n // tn)
        a_spec = pl.BlockSpec((k, tm), lambda i, j: (0, i)) if mode == "tn" else pl.BlockSpec((tm, k), lambda i, j: (i, 0))
        b_spec = pl.BlockSpec(nl + ((tn, k) if mode == "nt" else (k, tn)), lambda i, j: lead + ((j, 0) if mode == "nt" else (0, j)))
        o_spec = pl.BlockSpec((tm, tn), lambda i, j: (i, j))
        out_shape = (m, n)
    elif chunk == "b":
        (m, k) = a.shape[::-1] if mode == "tn" else a.shape
        c, nc = bs[0], (bs[1] if mode == "nt" else bs[2])
        grid = (m // tm, c)
        a_spec = pl.BlockSpec((k, tm), lambda i, j: (0, i)) if mode == "tn" else pl.BlockSpec((tm, k), lambda i, j: (i, 0))
        b_spec = pl.BlockSpec(nl + (None,) + tuple(bs[1:]), lambda i, j: lead + (j, 0, 0))
        o_spec = pl.BlockSpec((None, tm, nc), lambda i, j: (j, i, 0))
        out_shape = (c, m, nc)
    elif chunk == "a":
        assert mode == "tn"
        c, k, mc = a.shape
        n = bs[1]
        grid = (c, n // tn)
        a_spec = pl.BlockSpec((None, k, mc), lambda i, j: (i, 0, 0))
        b_spec = pl.BlockSpec(nl + (k, tn), lambda i, j: lead + (0, j))
        o_spec = pl.BlockSpec((None, mc, tn), lambda i, j: (i, 0, j))
        out_shape = (c, mc, n)
    else:
        reduce_n, m, kc = a.shape
        n = bs[1] if mode == "nt" else bs[2]
        grid = (m // tm, n // tn)
        a_spec = pl.BlockSpec((reduce_n, tm, kc), lambda i, j: (0, i, 0))
        b_spec = pl.BlockSpec(nl + ((reduce_n, tn, kc) if mode == "nt" else (reduce_n, kc, tn)),
                              lambda i, j: lead + ((0, j, 0) if mode == "nt" else (0, 0, j)))
        o_spec = pl.BlockSpec((tm, tn), lambda i, j: (i, j))
        out_shape = (m, n)

    if norm_gain is not None or norm_bwd is not None:
        assert chunk in (None, "reduce") and tn == n, "the RMSNorm of the result needs whole rows in a block"

    def body(*refs):
        a_ref, b_ref = refs[0], refs[1]
        dot = lambda x, y: lax.dot_general(x.astype(BF16), y.astype(BF16), dims, preferred_element_type=F32)
        if reduce_n:
            acc = dot(a_ref[0], b_ref[0])
            for r in range(1, reduce_n):
                acc = acc + dot(a_ref[r], b_ref[r])
        else:
            acc = dot(a_ref[...], b_ref[...])
        if norm_bwd is not None:
            x_ref, g_ref, r_ref = refs[2:5]
            dx_ref, dxb_ref, dg_ref = refs[-3:]
            dx, dg = _rms_bwd_math(x_ref[...], acc, g_ref[...])
            dx = dx + r_ref[...]
            dx_ref[...] = dx
            dxb_ref[...] = dx.astype(BF16)

            @pl.when(pl.program_id(0) == 0)
            def _():
                dg_ref[...] = jnp.zeros_like(dg_ref)

            dg_ref[0:1, :] += dg
            return
        o_ref = refs[-2] if norm_gain is not None else refs[-1]
        if res is not None:
            acc = acc + refs[2][...]
        o_ref[...] = acc.astype(o_ref.dtype)
        if norm_gain is not None:
            rs = lax.rsqrt(jnp.mean(acc * acc, axis=-1, keepdims=True) + NORM_EPS)
            refs[-1][...] = (acc * rs * refs[3][...]).astype(BF16)

    ins = [a, b] + ([res] if res is not None else [])
    specs = [a_spec, b_spec] + ([o_spec] if res is not None else [])
    out_shapes, out_specs = jax.ShapeDtypeStruct(out_shape, out_dtype), o_spec
    sem = ("parallel", "parallel")
    if norm_gain is not None:
        assert res is not None
        ins.append(norm_gain)
        specs.append(pl.BlockSpec((1, n), lambda i, j: (0, 0)))
        out_shapes, out_specs = (out_shapes, jax.ShapeDtypeStruct(out_shape, BF16)), (o_spec, o_spec)
    if norm_bwd is not None:
        assert res is None and norm_gain is None
        x_in, gain, resid = norm_bwd
        ins += [x_in, gain, resid]
        specs += [o_spec, pl.BlockSpec((1, n), lambda i, j: (0, 0)), o_spec]
        out_shapes = (jax.ShapeDtypeStruct(out_shape, F32), jax.ShapeDtypeStruct(out_shape, BF16), jax.ShapeDtypeStruct((8, n), F32))
        out_specs = (o_spec, o_spec, pl.BlockSpec((8, n), lambda i, j: (0, 0)))
        sem = ("arbitrary", "arbitrary")
    return pl.pallas_call(body, out_shape=out_shapes, grid=grid, in_specs=specs, out_specs=out_specs,
                          compiler_params=_params(*sem), name=name)(*ins)


def _rmsnorm_fwd(x, g, *, br, name):
    r, d = x.shape

    def body(x_ref, g_ref, o_ref):
        xf = x_ref[...]
        rs = lax.rsqrt(jnp.mean(xf * xf, axis=-1, keepdims=True) + NORM_EPS)
        o_ref[...] = (xf * rs * g_ref[...]).astype(BF16)

    return pl.pallas_call(body, out_shape=jax.ShapeDtypeStruct((r, d), BF16), grid=(r // br,),
                          in_specs=[pl.BlockSpec((br, d), lambda i: (i, 0)), pl.BlockSpec((1, d), lambda i: (0, 0))],
                          out_specs=pl.BlockSpec((br, d), lambda i: (i, 0)), compiler_params=_params("parallel"), name=name)(x, g)


def _rms_bwd_math(x, dy, g):
    d = x.shape[-1]
    rs = lax.rsqrt(jnp.mean(x * x, axis=-1, keepdims=True) + NORM_EPS)
    gy = dy * g
    proj = jnp.sum(x * gy, axis=-1, keepdims=True) * (1.0 / d)
    dx = rs * gy - x * (rs * rs * rs) * proj
    dg = jnp.sum(dy * (x * rs), axis=0, keepdims=True)
    return dx, dg


def _rmsnorm_bwd(x, dy, g, res, *, br, name):
    r, d = x.shape
    has_res = res is not None

    def body(*refs):
        x_ref, dy_ref, g_ref = refs[:3]
        dx_ref, dxb_ref, dg_ref = refs[-3:]
        dx, dg = _rms_bwd_math(x_ref[...], dy_ref[...], g_ref[...])
        if has_res:
            dx = dx + refs[3][...]
        dx_ref[...] = dx
        dxb_ref[...] = dx.astype(BF16)

        @pl.when(pl.program_id(0) == 0)
        def _():
            dg_ref[...] = jnp.zeros_like(dg_ref)

        dg_ref[0:1, :] += dg

    row = pl.BlockSpec((br, d), lambda i: (i, 0))
    ins = [x, dy, g] + ([res] if has_res else [])
    specs = [row, row, pl.BlockSpec((1, d), lambda i: (0, 0))] + ([row] if has_res else [])
    return pl.pallas_call(
        body, out_shape=(jax.ShapeDtypeStruct((r, d), F32), jax.ShapeDtypeStruct((r, d), BF16), jax.ShapeDtypeStruct((8, d), F32)),
        grid=(r // br,), in_specs=specs, out_specs=(row, row, pl.BlockSpec((8, d), lambda i: (0, 0))),
        compiler_params=_params("arbitrary"), name=name)(*ins)


def _loss_head(h, target, g, *, br, name):
    r, d = h.shape

    def body(x_ref, t_ref, g_ref, dx_ref, dxb_ref, dg_ref, loss_ref):
        x = x_ref[...]
        gg = g_ref[...]
        rs = lax.rsqrt(jnp.mean(x * x, axis=-1, keepdims=True) + NORM_EPS)
        err = x * rs * gg - t_ref[...]
        part = jnp.sum(jnp.sum(err * err, axis=1, keepdims=True), axis=0, keepdims=True) * (0.5 / d)
        dx, dg = _rms_bwd_math(x, err * (1.0 / d), gg)
        dx_ref[...] = dx
        dxb_ref[...] = dx.astype(BF16)

        @pl.when(pl.program_id(0) == 0)
        def _():
            dg_ref[...] = jnp.zeros_like(dg_ref)
            loss_ref[...] = jnp.zeros_like(loss_ref)

        dg_ref[0:1, :] += dg
        loss_ref[...] += jnp.broadcast_to(part, loss_ref.shape)

    row = pl.BlockSpec((br, d), lambda i: (i, 0))
    return pl.pallas_call(
        body, out_shape=(jax.ShapeDtypeStruct((r, d), F32), jax.ShapeDtypeStruct((r, d), BF16),
                         jax.ShapeDtypeStruct((8, d), F32), jax.ShapeDtypeStruct((8, LANES), F32)),
        grid=(r // br,), in_specs=[row, row, pl.BlockSpec((1, d), lambda i: (0, 0))],
        out_specs=(row, row, pl.BlockSpec((8, d), lambda i: (0, 0)), pl.BlockSpec((8, LANES), lambda i: (0, 0))),
        compiler_params=_params("arbitrary"), name=name)(h, target, g)


def _split3(x):
    hi = x.astype(BF16)
    r1 = x - hi.astype(F32)
    mid = r1.astype(BF16)
    lo = (r1 - mid.astype(F32)).astype(BF16)
    return hi, mid, lo


def _tri_sum(tri, x):
    hi, mid, lo = _split3(x)
    dot = lambda t: jnp.dot(tri, t, preferred_element_type=F32)
    return dot(hi) + dot(mid) + dot(lo)


def _forget_cumsum(proj, b_pad, *, name):
    s = proj.shape[0]
    blk = LANES

    def body(f_ref, b_ref, c_ref):
        ri = lax.broadcasted_iota(jnp.int32, (blk, blk), 0)
        ci = lax.broadcasted_iota(jnp.int32, (blk, blk), 1)
        tri = (ci <= ri).astype(BF16)
        bias = b_ref[...]

        def step(t, carry):
            rows = pl.ds(pl.multiple_of(t * blk, blk), blk)
            z = f_ref[rows, :] + bias
            lf = jnp.minimum(z, 0.0) - jnp.log(1.0 + jnp.exp(-jnp.abs(z)))
            cs = _tri_sum(tri, lf) + carry
            c_ref[rows, :] = cs
            return cs[blk - 1:blk, :]

        lax.fori_loop(0, s // blk, step, jnp.zeros((1, blk), F32))

    return pl.pallas_call(body, out_shape=jax.ShapeDtypeStruct((s, LANES), F32), grid=(1,),
                          in_specs=[pl.BlockSpec((s, LANES), lambda i: (0, F_COL)), pl.BlockSpec((1, LANES), lambda i: (0, 0))],
                          out_specs=pl.BlockSpec((s, LANES), lambda i: (0, 0)), compiler_params=_params("arbitrary"), name=name)(proj, b_pad)


def _forget_cumsum_bwd(proj, b_pad, dcq, dck, *, name):
    s = proj.shape[0]
    blk = LANES
    nblk = s // blk

    def body(f_ref, b_ref, dcq_ref, dck_ref, dz_ref, db_ref):
        ri = lax.broadcasted_iota(jnp.int32, (blk, blk), 0)
        ci = lax.broadcasted_iota(jnp.int32, (blk, blk), 1)
        triu = (ci >= ri).astype(BF16)
        bias = b_ref[...]

        def step(t, carry):
            tail, dbs = carry
            rows = pl.ds(pl.multiple_of((nblk - 1 - t) * blk, blk), blk)
            dc = dcq_ref[rows, :] - dck_ref[rows, :]
            dlf = _tri_sum(triu, dc) + tail
            z = f_ref[rows, :] + bias
            e = jnp.exp(-jnp.abs(z))
            sig_neg = jnp.where(z >= 0.0, e, 1.0) / (1.0 + e)
            dz = dlf * sig_neg
            dz_ref[rows, :] = dz.astype(BF16)
            return dlf[0:1, :], dbs + jnp.sum(dz, axis=0, keepdims=True)

        _, dbs = lax.fori_loop(0, nblk, step, (jnp.zeros((1, blk), F32), jnp.zeros((1, blk), F32)))
        db_ref[...] = jnp.broadcast_to(dbs, db_ref.shape)

    full = pl.BlockSpec((s, LANES), lambda i: (0, 0))
    return pl.pallas_call(body, out_shape=(jax.ShapeDtypeStruct((s, LANES), BF16), jax.ShapeDtypeStruct((8, LANES), F32)), grid=(1,),
                          in_specs=[pl.BlockSpec((s, LANES), lambda i: (0, F_COL)), pl.BlockSpec((1, LANES), lambda i: (0, 0)), full, full],
                          out_specs=(full, pl.BlockSpec((8, LANES), lambda i: (0, 0))), compiler_params=_params("arbitrary"), name=name)(proj, b_pad, dcq, dck)


def _attn_fwd(q_arr, kv_arr, ck6, *, q_col, k_col, v_col, n_hp, causal, bq, bk, name, heads=None, heads_col=0):
    s = q_arr.shape[0]
    skv = kv_arr.shape[0]
    bias = ck6 is not None
    nq = s // bq
    assert not causal or bq == bk

    def body(*refs):
        q_ref, k_ref, v_ref = refs[:3]
        ck_ref = refs[3] if bias else None
        o_ref, lse_ref, heads_ref = refs[-3:]
        i = pl.program_id(1)
        lo = _lane_lo((bq, LANES))
        q = q_ref[...] * SCALE
        qh = (jnp.where(lo, q, 0.0).astype(BF16), jnp.where(lo, 0.0, q).astype(BF16))

        def block(j, carry, diagonal):
            ks = pl.ds(pl.multiple_of(j * bk, bk), bk)
            k = k_ref[ks, :].astype(BF16)
            v = v_ref[ks, :].astype(BF16)
            if diagonal:
                ok = lax.broadcasted_iota(jnp.int32, (bq, bk), 1) <= lax.broadcasted_iota(jnp.int32, (bq, bk), 0)
            out = []
            for h in range(2):
                m, l, acc = carry[3 * h:3 * h + 3]
                sc = lax.dot_general(qh[h], k, NT, preferred_element_type=F32)
                if bias:
                    sc = sc - ck_ref[0, h:h + 1, ks]
                if diagonal:
                    sc = jnp.where(ok, sc, NEG)
                mn = jnp.maximum(m, jnp.max(sc, axis=1, keepdims=True))
                p = jnp.exp(sc - mn)
                al = jnp.exp(m - mn)
                out += [mn, al * l + jnp.sum(p, axis=1, keepdims=True), al * acc + jnp.dot(p.astype(BF16), v, preferred_element_type=F32)]
            return tuple(out)

        col = lambda v_: jnp.full((bq, 1), v_, F32)
        init = (col(NEG), col(0.0), jnp.zeros((bq, LANES), F32)) * 2
        n_full = i if causal else skv // bk
        carry = lax.fori_loop(0, n_full, functools.partial(block, diagonal=False), init)
        if causal:
            carry = block(i, carry, True)
        m0, l0, a0, m1, l1, a1 = carry
        out = _pair(lo, a0 / l0, a1 / l1)
        o_ref[...] = out
        lse_ref[0] = _pair(lo, m0 + jnp.log(l0), m1 + jnp.log(l1))
        heads_ref[...] = out.astype(BF16)

    specs = [pl.BlockSpec((bq, LANES), lambda h, i: (i, q_col + h)),
             pl.BlockSpec((skv, LANES), lambda h, i: (0, k_col + h)),
             pl.BlockSpec((skv, LANES), lambda h, i: (0, v_col + h))]
    ins = [q_arr, kv_arr, kv_arr]
    if bias:
        specs += [pl.BlockSpec((1, 8, skv), lambda h, i: (h, 0, 0))]
        ins += [ck6]
    aliases = {}
    if heads is not None:
        aliases = {len(ins): 2}
        specs += [pl.BlockSpec(memory_space=pl.ANY)]
        ins += [heads]
    return pl.pallas_call(
        body, out_shape=(jax.ShapeDtypeStruct((s, n_hp * LANES), F32), jax.ShapeDtypeStruct((n_hp, s, LANES), F32),
                         jax.ShapeDtypeStruct((s, D_MODEL), BF16)),
        grid=(n_hp, nq), in_specs=specs,
        out_specs=(pl.BlockSpec((bq, LANES), lambda h, i: (i, h)), pl.BlockSpec((1, bq, LANES), lambda h, i: (h, i, 0)),
                   pl.BlockSpec((bq, LANES), lambda h, i: (i, heads_col + h))),
        input_output_aliases=aliases, compiler_params=_params("parallel", "parallel"), name=name)(*ins)


def _attn_bwd(q_arr, kv_arr, o_arr, do_arr, lse, ck6, *, q_col, k_col, v_col, o_col, n_hp, causal, bq, bk, name):
    s = q_arr.shape[0]
    skv = kv_arr.shape[0]
    bias = ck6 is not None
    nq = s // bq
    assert not causal or bq == bk

    def body(*refs):
        q_ref, k_ref, v_ref, o_ref, do_ref, lse_ref = refs[:6]
        if bias:
            ck_ref = refs[6]
            dq_ref, dk_ref, dv_ref, dcq_ref, dck_ref = refs[-5:]
        else:
            dq_ref, dk_ref, dv_ref = refs[-3:]
        j = pl.program_id(1)
        lo_q = _lane_lo((bq, LANES))
        lo_k = _lane_lo((bk, LANES))
        k = k_ref[...]
        v = v_ref[...].astype(BF16)
        kb = k.astype(BF16)
        kh = (jnp.where(lo_k, k, 0.0).astype(BF16), jnp.where(lo_k, 0.0, k).astype(BF16))
        if bias:
            pick_k = [(lax.broadcasted_iota(jnp.int32, (8, bk), 0) == h).astype(BF16) for h in range(2)]
            pick_q = [(lax.broadcasted_iota(jnp.int32, (8, bq), 0) == h).astype(BF16) for h in range(2)]

        @pl.when(j == 0)
        def _():
            dq_ref[...] = jnp.zeros_like(dq_ref)
            if bias:
                dcq_ref[...] = jnp.zeros_like(dcq_ref)

        def block(i, carry, diagonal):
            dk_acc, dv_acc, cs = carry
            qs = pl.ds(pl.multiple_of(i * bq, bq), bq)
            q = q_ref[qs, :] * SCALE
            do = do_ref[qs, :]
            dd = do * o_ref[qs, :]
            lse_i = lse_ref[0, qs, :]
            qh = (jnp.where(lo_q, q, 0.0).astype(BF16), jnp.where(lo_q, 0.0, q).astype(BF16))
            doh = (jnp.where(lo_q, do, 0.0).astype(BF16), jnp.where(lo_q, 0.0, do).astype(BF16))
            dh = (jnp.sum(jnp.where(lo_q, dd, 0.0), axis=1, keepdims=True), jnp.sum(jnp.where(lo_q, 0.0, dd), axis=1, keepdims=True))
            if diagonal:
                ok = lax.broadcasted_iota(jnp.int32, (bq, bk), 1) <= lax.broadcasted_iota(jnp.int32, (bq, bk), 0)
            dq_blk = None
            rs = None
            for h in range(2):
                sc = lax.dot_general(qh[h], kb, NT, preferred_element_type=F32)
                if bias:
                    sc = sc - ck_ref[0, h:h + 1, :]
                if diagonal:
                    sc = jnp.where(ok, sc, NEG)
                p = jnp.exp(sc - lse_i[:, h * HEAD_DIM:h * HEAD_DIM + 1])
                ds = p * (lax.dot_general(doh[h], v, NT, preferred_element_type=F32) - dh[h])
                dsb = ds.astype(BF16)
                dv_acc = dv_acc + lax.dot_general(p.astype(BF16), doh[h], TN, preferred_element_type=F32)
                dk_acc = dk_acc + lax.dot_general(dsb, qh[h], TN, preferred_element_type=F32)
                part = jnp.dot(dsb, kh[h], preferred_element_type=F32)
                dq_blk = part if dq_blk is None else dq_blk + part
                if bias:
                    cs = cs + jnp.dot(pick_q[h], dsb, preferred_element_type=F32)
                    row_sums = lax.dot_general(pick_k[h], dsb, NT, preferred_element_type=F32)
                    rs = row_sums if rs is None else rs + row_sums
            dq_ref[qs, :] += dq_blk * SCALE
            if bias:
                dcq_ref[0, :, qs] += rs
            return dk_acc, dv_acc, cs

        carry = (jnp.zeros((bk, LANES), F32), jnp.zeros((bk, LANES), F32), jnp.zeros((8, bk), F32))
        if causal:
            carry = block(j, carry, True)
        dk_acc, dv_acc, cs = lax.fori_loop(j + 1 if causal else 0, nq, functools.partial(block, diagonal=False), carry)
        dk_ref[...] = dk_acc.astype(BF16)
        dv_ref[...] = dv_acc.astype(BF16)
        if bias:
            dck_ref[0] = cs

    full_q = lambda c: pl.BlockSpec((s, LANES), lambda h, j: (0, c + h))
    specs = [full_q(q_col),
             pl.BlockSpec((bk, LANES), lambda h, j: (j, k_col + h)),
             pl.BlockSpec((bk, LANES), lambda h, j: (j, v_col + h)),
             full_q(0), full_q(o_col),
             pl.BlockSpec((1, s, LANES), lambda h, j: (h, 0, 0))]
    ins = [q_arr, kv_arr, kv_arr, o_arr, do_arr, lse]
    out_shape = [jax.ShapeDtypeStruct((s, n_hp * LANES), F32), jax.ShapeDtypeStruct((skv, n_hp * LANES), BF16),
                 jax.ShapeDtypeStruct((skv, n_hp * LANES), BF16)]
    out_specs = [full_q(0), pl.BlockSpec((bk, LANES), lambda h, j: (j, h)), pl.BlockSpec((bk, LANES), lambda h, j: (j, h))]
    if bias:
        specs += [pl.BlockSpec((1, 8, bk), lambda h, j: (h, 0, j))]
        ins += [ck6]
        out_shape += [jax.ShapeDtypeStruct((n_hp, 8, s), F32), jax.ShapeDtypeStruct((n_hp, 8, skv), F32)]
        out_specs += [pl.BlockSpec((1, 8, s), lambda h, j: (h, 0, 0)), pl.BlockSpec((1, 8, bk), lambda h, j: (h, 0, j))]
    return pl.pallas_call(body, out_shape=tuple(out_shape), grid=(n_hp, skv // bk), in_specs=specs, out_specs=tuple(out_specs),
                          compiler_params=_params("parallel", "arbitrary"), name=name)(*ins)


def _rope_tables(s):
    inv = 1.0 / (ROPE_THETA ** (jnp.arange(0, HEAD_DIM, 2, dtype=F32) / HEAD_DIM))
    ang = jnp.arange(s, dtype=F32)[:, None] * inv[None, :]
    cos, sin = jnp.cos(ang), jnp.sin(ang)
    return jnp.tile(cos, (1, 4)), jnp.concatenate([-sin, sin, -sin, sin], axis=1)


def _rope(x_arr, cos_t, sin_t, *, n_cols, out_dtype, br, name):
    s = x_arr.shape[0]

    def body(x_ref, c_ref, s_ref, o_ref):
        cos, sin = c_ref[...], s_ref[...]
        first = (lax.broadcasted_iota(jnp.int32, (br, LANES), 1) % HEAD_DIM) < (HEAD_DIM // 2)
        for j in range(n_cols):
            lanes = slice(j * LANES, (j + 1) * LANES)
            x = x_ref[:, lanes].astype(F32)
            swapped = jnp.where(first, pltpu.roll(x, LANES - HEAD_DIM // 2, 1), pltpu.roll(x, HEAD_DIM // 2, 1))
            o_ref[:, lanes] = (x * cos + swapped * sin).astype(o_ref.dtype)

    tab = pl.BlockSpec((br, LANES), lambda i: (i, 0))
    blk = pl.BlockSpec((br, n_cols * LANES), lambda i: (i, 0))
    return pl.pallas_call(body, out_shape=jax.ShapeDtypeStruct((s, n_cols * LANES), out_dtype), grid=(s // br,),
                          in_specs=[blk, tab, tab], out_specs=blk, compiler_params=_params("parallel"), name=name)(x_arr, cos_t, sin_t)


def _stack_heads(x):
    lo = _lane_lo(x.shape)
    return jnp.concatenate([jnp.where(lo, x, 0.0), jnp.where(lo, 0.0, x)], axis=0).astype(BF16)


def _unstack_heads(x):
    return jnp.where(_lane_lo((DIL_L, LANES)), x[:DIL_L], x[DIL_L:])


def _dil_scores(q_ref, k_ref, cur, prev, has_prev):
    qs = _stack_heads(q_ref[cur, :] * SCALE)
    kk = jnp.concatenate([k_ref[prev, :], k_ref[cur, :]], axis=0).astype(BF16)
    a = lax.broadcasted_iota(jnp.int32, (2 * DIL_L, 2 * DIL_L), 0) & (DIL_L - 1)
    c = lax.broadcasted_iota(jnp.int32, (2 * DIL_L, 2 * DIL_L), 1)
    ok = ((c < DIL_L) & (c >= a) & has_prev) | ((c >= DIL_L) & (c - DIL_L <= a))
    return qs, kk, jnp.where(ok, lax.dot_general(qs, kk, NT, preferred_element_type=F32), NEG)


def _dil_rows(t, dil):
    r, m = t % dil, t // dil
    start = m * (DIL_L * dil) + r
    prev = jnp.maximum(start - DIL_L * dil, 0)
    return pl.ds(start, DIL_L, stride=dil), pl.ds(prev, DIL_L, stride=dil), m > 0


def _softmax3(a, b, c):
    m = jnp.maximum(jnp.maximum(a, b), c)
    ea, eb, ec = jnp.exp(a - m), jnp.exp(b - m), jnp.exp(c - m)
    den = ea + eb + ec
    inv = 1.0 / den
    return ea * inv, eb * inv, ec * inv, m + jnp.log(den)


def _dil_fwd(qk_r, proj, *, name):
    s = qk_r.shape[0]
    nsub = s // DIL_L
    mb = 512

    def body(q_ref, k_ref, v_ref, mix_ref, l1_ref, l2_ref, l3_ref, heads_ref, o1_scr, o2_scr, o3_scr):
        for (_, dil), o_scr, l_ref in zip(DILATED_BRANCHES, (o1_scr, o2_scr, o3_scr), (l1_ref, l2_ref, l3_ref)):
            def step(t, carry, dil=dil, o_scr=o_scr, l_ref=l_ref):
                cur, prev, has_prev = _dil_rows(t, dil)
                _, _, sc = _dil_scores(q_ref, k_ref, cur, prev, has_prev)
                vv = jnp.concatenate([v_ref[prev, :], v_ref[cur, :]], axis=0).astype(BF16)
                m = jnp.max(sc, axis=1, keepdims=True)
                e = jnp.exp(sc - m)
                den = jnp.sum(e, axis=1, keepdims=True)
                o = jnp.dot((e * (1.0 / den)).astype(BF16), vv, preferred_element_type=F32)
                o_scr[cur, :] = _unstack_heads(o)
                l_ref[cur, :] = _unstack_heads(jnp.broadcast_to(m + jnp.log(den), (2 * DIL_L, LANES)))
                return carry

            lax.fori_loop(0, nsub, step, 0, unroll=DIL_UNROLL_FWD)

        def merge(i, carry):
            rows = pl.ds(pl.multiple_of(i * mb, mb), mb)
            wa, wb, wc, _ = _softmax3(l1_ref[rows, :], l2_ref[rows, :], l3_ref[rows, :])
            mix = wa * o1_scr[rows, :] + wb * o2_scr[rows, :] + wc * o3_scr[rows, :]
            mix_ref[rows, :] = mix
            heads_ref[rows, :] = mix.astype(BF16)
            return carry

        lax.fori_loop(0, s // mb, merge, 0)

    col = lambda arr_col: pl.BlockSpec((s, LANES), lambda h: (0, arr_col + h))
    shp = jax.ShapeDtypeStruct((s, D_MIX), F32)
    mix, l1, l2, l3, heads = pl.pallas_call(
        body, out_shape=(shp, shp, shp, shp, jax.ShapeDtypeStruct((s, D_MODEL), BF16)), grid=(N_MIX_HP,),
        in_specs=[col(0), col(N_MIX_HP), col(2 * N_MIX_HP)],
        out_specs=(col(0),) * 5, scratch_shapes=[pltpu.VMEM((s, LANES), F32)] * 3,
        compiler_params=_params("parallel"), name=name)(qk_r, qk_r, proj)
    return mix, (l1, l2, l3), heads


def _dil_bwd(qk_r, proj, mix, dheads, lses, *, name):
    s = qk_r.shape[0]
    nsub = s // DIL_L
    mb = 512

    def body(q_ref, k_ref, v_ref, mix_ref, dm_ref, l1_ref, l2_ref, l3_ref, dq_ref, dk_ref, dv_ref, lt_scr, dd_scr):
        lo = _lane_lo((DIL_L, LANES))
        lo_m = _lane_lo((mb, LANES))

        def prep(i, carry):
            rows = pl.ds(pl.multiple_of(i * mb, mb), mb)
            _, _, _, lt = _softmax3(l1_ref[rows, :], l2_ref[rows, :], l3_ref[rows, :])
            lt_scr[rows, :] = lt
            dd = dm_ref[rows, :] * mix_ref[rows, :]
            dd_scr[rows, :] = _pair(lo_m, jnp.sum(jnp.where(lo_m, dd, 0.0), axis=1, keepdims=True),
                                    jnp.sum(jnp.where(lo_m, 0.0, dd), axis=1, keepdims=True))
            zero = jnp.zeros((mb, LANES), F32)
            dq_ref[rows, :] = zero
            dk_ref[rows, :] = zero
            dv_ref[rows, :] = zero
            return carry

        lax.fori_loop(0, s // mb, prep, 0)

        for (_, dil), l_ref in zip(DILATED_BRANCHES, (l1_ref, l2_ref, l3_ref)):
            def step(t, carry, dil=dil, l_ref=l_ref):
                cur, prev, has_prev = _dil_rows(t, dil)
                qs, kk, sc = _dil_scores(q_ref, k_ref, cur, prev, has_prev)
                vv = jnp.concatenate([v_ref[prev, :], v_ref[cur, :]], axis=0).astype(BF16)
                lg = l_ref[cur, :]
                w = jnp.exp(lg - lt_scr[cur, :])
                wd = w * dd_scr[cur, :]
                column = lambda x: jnp.concatenate([x[:, 0:1], x[:, HEAD_DIM:HEAD_DIM + 1]], axis=0)
                dos = _stack_heads(w * dm_ref[cur, :])
                p = jnp.exp(sc - column(lg))
                ds = (p * (lax.dot_general(dos, vv, NT, preferred_element_type=F32) - column(wd))).astype(BF16)
                dq_ref[cur, :] += _unstack_heads(jnp.dot(ds, kk, preferred_element_type=F32)) * SCALE
                dkk = lax.dot_general(ds, qs, TN, preferred_element_type=F32)
                dvv = lax.dot_general(p.astype(BF16), dos, TN, preferred_element_type=F32)
                dk_ref[cur, :] += dkk[DIL_L:]
                dv_ref[cur, :] += dvv[DIL_L:]
                dk_ref[prev, :] += dkk[:DIL_L]
                dv_ref[prev, :] += dvv[:DIL_L]
                return carry

            lax.fori_loop(0, nsub, step, 0, unroll=DIL_UNROLL_BWD)

    col = lambda arr_col: pl.BlockSpec((s, LANES), lambda h: (0, arr_col + h))
    shp = jax.ShapeDtypeStruct((s, D_MIX), F32)
    return pl.pallas_call(
        body, out_shape=(shp, shp, shp), grid=(N_MIX_HP,),
        in_specs=[col(0), col(N_MIX_HP), col(2 * N_MIX_HP), col(0), col(0), col(0), col(0), col(0)], out_specs=(col(0),) * 3,
        scratch_shapes=[pltpu.VMEM((s, LANES), F32)] * 2,
        compiler_params=_params("parallel"), name=name)(qk_r, qk_r, proj, mix, dheads, *lses)


CONV_BR = 512
FF_CHUNK = 2 * D_FF // N_DEV
FF_HALF = N_DEV // 2
HALO = 8


def _shift_down(x, halo, k):
    row = lax.broadcasted_iota(jnp.int32, x.shape, 0)
    y = pltpu.roll(x, k, 0)
    for r in range(k):
        y = jnp.where(row == r, halo[HALO - k + r:HALO - k + r + 1, :], y)
    return y


def _shift_up(x, halo, k):
    n = x.shape[0]
    row = lax.broadcasted_iota(jnp.int32, x.shape, 0)
    y = pltpu.roll(x, n - k, 0)
    for r in range(k):
        y = jnp.where(row == n - k + r, halo[r:r + 1, :], y)
    return y


def _conv_vals(u, halo, w, b):
    s1 = _shift_down(u, halo, 1)
    s2 = _shift_down(u, halo, 2)
    return b + w[0:1, :] * s2 + w[1:2, :] * s1 + w[2:3, :] * u, s1, s2


def _conv_in_specs(order, layer):
    rc = (lambda i, j: (i, j)) if order == "rc" else (lambda j, i: (i, j))
    per = CONV_BR // HALO
    main = lambda off: pl.BlockSpec((None, CONV_BR, FF_CHUNK), lambda *g: (off + rc(*g)[1], rc(*g)[0], 0))
    halo = lambda off: pl.BlockSpec((None, HALO, FF_CHUNK), lambda *g: (off + rc(*g)[1], jnp.maximum(rc(*g)[0] * per - 1, 0), 0))
    wspec = lambda off: pl.BlockSpec((None, None, 3, FF_CHUNK), lambda *g: (layer, off + rc(*g)[1], 0, 0))
    bspec = lambda off: pl.BlockSpec((None, 1, FF_CHUNK), lambda *g: (off + rc(*g)[1], 0, 0))
    return [main(0), halo(0), main(FF_HALF), halo(FF_HALF), wspec(0), wspec(FF_HALF), bspec(0), bspec(FF_HALF)]


def _conv_fwd(u, cw, cb, layer, *, name):
    s = u.shape[1]

    def body(uv_ref, hv_ref, ug_ref, hg_ref, wv_ref, wg_ref, bv_ref, bg_ref, o_ref):
        first = pl.program_id(0) == 0
        hv = jnp.where(first, 0.0, hv_ref[...])
        hg = jnp.where(first, 0.0, hg_ref[...])
        val, _, _ = _conv_vals(uv_ref[...], hv, wv_ref[...], bv_ref[...])
        gate, _, _ = _conv_vals(ug_ref[...], hg, wg_ref[...], bg_ref[...])
        o_ref[...] = (gate / (1.0 + jnp.exp(-gate)) * val).astype(BF16)

    return pl.pallas_call(body, out_shape=jax.ShapeDtypeStruct((FF_HALF, s, FF_CHUNK), BF16), grid=(s // CONV_BR, FF_HALF),
                          in_specs=_conv_in_specs("rc", layer), out_specs=pl.BlockSpec((None, CONV_BR, FF_CHUNK), lambda i, j: (j, i, 0)),
                          compiler_params=_params("parallel", "parallel"), name=name)(u, u, u, u, cw, cw, cb, cb)


def _swiglu_bwd(val, gate, da):
    sg = 1.0 / (1.0 + jnp.exp(-gate))
    return da * (gate * sg), da * val * (sg * (1.0 + gate * (1.0 - sg)))


def _conv_bwd(u, cw, cb, da, layer, *, name):
    s = u.shape[1]
    nrow = s // CONV_BR
    per = CONV_BR // HALO

    def body(uv_ref, hv_ref, ug_ref, hg_ref, wv_ref, wg_ref, bv_ref, bg_ref, da_ref, nv_ref, ng_ref, nda_ref, du_ref, dwb_ref):
        i = pl.program_id(1)
        first, last = i == 0, i == nrow - 1
        hv = jnp.where(first, 0.0, hv_ref[...])
        hg = jnp.where(first, 0.0, hg_ref[...])
        uv, ug = uv_ref[...], ug_ref[...]
        wv, wg, bv, bg = wv_ref[...], wg_ref[...], bv_ref[...], bg_ref[...]
        val, v1, v2 = _conv_vals(uv, hv, wv, bv)
        gate, g1, g2 = _conv_vals(ug, hg, wg, bg)
        dval, dgate = _swiglu_bwd(val, gate, da_ref[...])
        val_n, _, _ = _conv_vals(nv_ref[...], uv[CONV_BR - HALO:, :], wv, bv)
        gate_n, _, _ = _conv_vals(ng_ref[...], ug[CONV_BR - HALO:, :], wg, bg)
        dval_n, dgate_n = _swiglu_bwd(val_n, gate_n, nda_ref[...])
        dval_n = jnp.where(last, 0.0, dval_n)
        dgate_n = jnp.where(last, 0.0, dgate_n)
        back = lambda dc, dc_n, w: w[2:3, :] * dc + w[1:2, :] * _shift_up(dc, dc_n, 1) + w[0:1, :] * _shift_up(dc, dc_n, 2)
        du_ref[0] = back(dval, dval_n, wv).astype(BF16)
        du_ref[1] = back(dgate, dgate_n, wg).astype(BF16)

        @pl.when(first)
        def _():
            dwb_ref[...] = jnp.zeros_like(dwb_ref)

        cs = lambda t: jnp.sum(t, axis=0, keepdims=True)
        r8 = lax.broadcasted_iota(jnp.int32, (8, FF_CHUNK), 0)
        rows4 = lambda a, b, c, d: jnp.where(r8 == 0, a, jnp.where(r8 == 1, b, jnp.where(r8 == 2, c, jnp.where(r8 == 3, d, 0.0))))
        dwb_ref[0] += rows4(cs(dval * v2), cs(dval * v1), cs(dval * uv), cs(dval))
        dwb_ref[1] += rows4(cs(dgate * g2), cs(dgate * g1), cs(dgate * ug), cs(dgate))

    nxt = lambda off: pl.BlockSpec((None, HALO, FF_CHUNK), lambda j, i: (off + j, jnp.minimum((i + 1) * per, nrow * per - 1), 0))
    specs = _conv_in_specs("cr", layer) + [pl.BlockSpec((None, CONV_BR, FF_CHUNK), lambda j, i: (j, i, 0)), nxt(0), nxt(FF_HALF), nxt(0)]
    return pl.pallas_call(
        body, out_shape=(jax.ShapeDtypeStruct((2, FF_HALF, s, FF_CHUNK), BF16), jax.ShapeDtypeStruct((2, FF_HALF, 8, FF_CHUNK), F32)),
        grid=(FF_HALF, nrow), in_specs=specs,
        out_specs=(pl.BlockSpec((2, None, CONV_BR, FF_CHUNK), lambda j, i: (0, j, i, 0)),
                   pl.BlockSpec((2, None, 8, FF_CHUNK), lambda j, i: (0, j, 0, 0))),
        compiler_params=_params("parallel", "arbitrary"), name=name)(u, u, u, u, cw, cw, cb, cb, da, u, u, da)


def _rows_of(r):
    return lambda ref, idx: ref.at[:, pl.ds(idx * r, r), :]


def _slot1(ref, idx):
    return ref.at[:, idx]


def _slot0(ref, idx):
    return ref.at[idx]


def _all_gather(shards, full_shapes, places, *, name):
    n = len(shards)

    def body(*refs):
        ins, outs = refs[:n], refs[n:2 * n]
        send_sems, recv_sems, local_sems = refs[2 * n:]
        mx, my, mc = lax.axis_index("x"), lax.axis_index("y"), lax.axis_index("c")
        me, sibling = (mx, my, mc), (mx, my, 1 - mc)
        chips = [(1 - mx, my), (mx, 1 - my), (1 - mx, 1 - my)]

        def win(t, px, py, pc):
            return places[t](outs[t], 4 * px + 2 * py + pc)

        def copy(t, k, block, to, src=None):
            return pltpu.make_async_remote_copy(src_ref=win(t, *block) if src is None else src, dst_ref=win(t, *block),
                                                send_sem=send_sems.at[t, k], recv_sem=recv_sems.at[t, k], device_id=to, device_id_type=MESH)

        mine = [pltpu.make_async_copy(ins[t], win(t, *me), local_sems.at[t]) for t in range(n)]
        for cp in mine:
            cp.start()
        first = []
        for t in range(n):
            first += [copy(t, 0, me, sibling, src=ins[t])] + [copy(t, 1 + j, me, (*chip, mc), src=ins[t]) for j, chip in enumerate(chips)]
        for cp in first:
            cp.start()
        passed = []
        for j, chip in enumerate(chips):
            for t in range(n):
                copy(t, 1 + j, (*chip, mc), me).wait_recv()
                fwd = copy(t, 4 + j, (*chip, mc), sibling)
                fwd.start()
                passed.append(fwd)
        for t in range(n):
            copy(t, 0, sibling, me).wait_recv()
            for j, chip in enumerate(chips):
                copy(t, 4 + j, (*chip, 1 - mc), me).wait_recv()
        for cp in first + passed:
            cp.wait_send()
        for cp in mine:
            cp.wait()

    hbm = pl.BlockSpec(memory_space=pl.ANY)
    return pl.pallas_call(
        body, out_shape=tuple(jax.ShapeDtypeStruct(s, x.dtype) for s, x in zip(full_shapes, shards)),
        in_specs=[hbm] * n, out_specs=(hbm,) * n,
        scratch_shapes=[pltpu.SemaphoreType.DMA((n, 7)), pltpu.SemaphoreType.DMA((n, 7)), pltpu.SemaphoreType.DMA((n,))],
        name=name)(*shards)


FLIPS = [(fx, fy, fc) for fx in (0, 1) for fy in (0, 1) for fc in (0, 1)][1:]


def _exchange_copies(kind, places, src, land, send_sems, recv_sems, local_sems):
    mx, my, mc = lax.axis_index("x"), lax.axis_index("y"), lax.axis_index("c")
    me = 4 * mx + 2 * my + mc
    n = len(src)
    local, remote = [], []
    for t in range(n):
        if kind == "gather":
            local.append(pltpu.make_async_copy(src[t], places[t](land[t], me), local_sems.at[t]))
        else:
            local.append(pltpu.make_async_copy(places[t](src[t], me), land[t].at[me], local_sems.at[t]))
    for k, (fx, fy, fc) in enumerate(FLIPS):
        px, py, pc = mx ^ fx, my ^ fy, mc ^ fc
        peer = 4 * px + 2 * py + pc
        for t in range(n):
            sems = dict(send_sem=send_sems.at[7 * t + k], recv_sem=recv_sems.at[7 * t + k], device_id=(px, py, pc), device_id_type=MESH)
            if kind == "gather":
                pair = [(src[t], places[t](land[t], me)), (src[t], places[t](land[t], peer))]
            else:
                pair = [(places[t](src[t], peer), land[t].at[me]), (places[t](src[t], peer), land[t].at[peer])]
            remote.append([functools.partial(pltpu.make_async_remote_copy, src_ref=s_, dst_ref=d_, **sems) for s_, d_ in pair])
    return local, remote


HBM_SPEC = pl.BlockSpec(memory_space=pltpu.HBM)
SEM_SPEC = pl.BlockSpec(memory_space=pltpu.SEMAPHORE)
SIDE_EFFECT = pltpu.SideEffectType.DATAFLOW_SIDE_EFFECTING


def _exchange_start(kind, srcs, land_shapes, places, after, *, name):
    n = len(srcs)

    def body(*refs):
        src, land = refs[:n], refs[n:2 * n]
        send_sems, recv_sems, local_sems = refs[2 * n + 1:2 * n + 4]
        token = refs[-1]
        local, remote = _exchange_copies(kind, places, src, land, send_sems, recv_sems, local_sems)
        for cp in local:
            cp.start()
        for send, _ in remote:
            send().start()
        token[...] = jnp.zeros_like(token)

    hbm = lambda t: pltpu.with_memory_space_constraint(t, pltpu.HBM)
    lands = [hbm(lax.empty(tuple(s), x.dtype)) for s, x in zip(land_shapes, srcs)]
    out_shape = (pltpu.SemaphoreType.DMA((7 * n,)), pltpu.SemaphoreType.DMA((7 * n,)), pltpu.SemaphoreType.DMA((n,)),
                 *[pltpu.HBM(x.shape, x.dtype) for x in srcs], *[pltpu.HBM(tuple(s), x.dtype) for s, x in zip(land_shapes, srcs)],
                 jax.ShapeDtypeStruct((8, LANES), F32))
    outs = pl.pallas_call(
        body, name=name, out_shape=out_shape, in_specs=[HBM_SPEC] * (2 * n) + [pl.BlockSpec(memory_space=pl.ANY)],
        out_specs=(SEM_SPEC, SEM_SPEC, SEM_SPEC) + (HBM_SPEC,) * (2 * n) + (pl.BlockSpec(memory_space=pltpu.VMEM),),
        input_output_aliases={i: 3 + i for i in range(2 * n)},
        compiler_params=pltpu.CompilerParams(has_side_effects=SIDE_EFFECT))(*[hbm(x) for x in srcs], *lands, after)
    return dict(sems=outs[:3], src=outs[3:3 + n], land=outs[3 + n:3 + 2 * n], token=outs[-1])


def _exchange_wait(kind, started, places, after, *, name):
    n = len(started["src"])

    def body(*refs):
        src, land = refs[:n], refs[n:2 * n]
        send_sems, recv_sems, local_sems = refs[2 * n:2 * n + 3]
        local, remote = _exchange_copies(kind, places, src, land, send_sems, recv_sems, local_sems)
        for cp in local:
            cp.wait()
        for send, arrival in remote:
            send().wait_send()
            arrival().wait_recv()

    out_shape = tuple(pltpu.HBM(x.shape, x.dtype) for x in started["src"]) + tuple(pltpu.HBM(x.shape, x.dtype) for x in started["land"])
    outs = pl.pallas_call(
        body, name=name, out_shape=out_shape,
        in_specs=[HBM_SPEC] * (2 * n) + [SEM_SPEC] * 3 + [pl.BlockSpec(memory_space=pl.ANY)], out_specs=(HBM_SPEC,) * (2 * n),
        input_output_aliases={i: i for i in range(2 * n)},
        compiler_params=pltpu.CompilerParams(has_side_effects=SIDE_EFFECT))(*started["src"], *started["land"], *started["sems"], after)
    return list(outs[n:])


def _adamw(parts, w, m, v, *, br, name):
    layers, r, wd = w.shape
    assert len(parts) == layers

    def body(*refs):
        p_refs = refs[:layers]
        w_ref, m_ref, v_ref, g_ref, d_ref, nm_ref, nv_ref = refs[layers:]
        for k in range(layers):
            @pl.when(pl.program_id(0) == k)
            def _(p_ref=p_refs[k]):
                g = p_ref[0].astype(F32)
                for dev in range(1, N_DEV):
                    g = g + p_ref[dev].astype(F32)
                mm = ADAM_B1 * m_ref[...] + (1.0 - ADAM_B1) * g
                vv = ADAM_B2 * v_ref[...] + (1.0 - ADAM_B2) * (g * g)
                m_hat = mm / (1.0 - ADAM_B1 ** ADAM_STEP)
                v_hat = vv / (1.0 - ADAM_B2 ** ADAM_STEP)
                g_ref[...] = g
                d_ref[...] = -ADAM_LR * (m_hat / (jnp.sqrt(v_hat) + ADAM_EPS) + ADAM_WD * w_ref[...])
                nm_ref[...] = mm
                nv_ref[...] = vv

    p_spec = lambda k: pl.BlockSpec((N_DEV, None, br, wd), lambda l, i: (0, 0, jnp.where(l == k, i, 0), 0))
    blk = pl.BlockSpec((None, br, wd), lambda l, i: (l, i, 0))
    shp = jax.ShapeDtypeStruct((layers, r, wd), F32)
    return pl.pallas_call(body, out_shape=(shp, shp, shp, shp), grid=(layers, r // br),
                          in_specs=[p_spec(k) for k in range(layers)] + [blk, blk, blk], out_specs=(blk, blk, blk, blk),
                          compiler_params=_params("arbitrary", "arbitrary"), name=name)(*parts, w, m, v)


SMALL = ("norm_mix", "norm_mem", "norm_ffn", "b_forget", "conv_b", "norm_final")


def _pack(tensors):
    flat = jnp.concatenate([t.reshape(-1) for t in tensors])
    rows = -(-flat.shape[0] // (PACK_W * PACK_ROW_ALIGN)) * PACK_ROW_ALIGN
    flat = jnp.pad(flat, (0, rows * PACK_W - flat.shape[0]))
    return flat.reshape(1, rows, PACK_W)


def _unpack(buf, shapes):
    flat = buf.reshape(-1)
    out, off = [], 0
    for shp in shapes:
        n = math.prod(shp)
        out.append(flat[off:off + n].reshape(tuple(shp)))
        off += n
    return out


def _fox_permute(w):
    pad = jnp.zeros(w.shape[:-1] + (FOX_P - FOX_IN,), w.dtype)
    return jnp.concatenate([w[..., :3 * D_MIX], w[..., 3 * D_MIX + N_MIX_HEADS:], w[..., 3 * D_MIX:3 * D_MIX + N_MIX_HEADS], pad], axis=-1)


def _fox_unpermute(w):
    return jnp.concatenate([w[..., :3 * D_MIX], w[..., DIL_IN:DIL_IN + N_MIX_HEADS], w[..., 3 * D_MIX:DIL_IN]], axis=-1)


def _bias_layout(c):
    s = c.shape[0]
    ct = c[:, :N_MIX_HEADS].T.reshape(N_MIX_HP, 2, s)
    return jnp.pad(ct, ((0, 0), (0, 6), (0, 0)))


def _bias_grad(dck6):
    s = dck6.shape[2]
    dk = dck6[:, :2, :].reshape(N_MIX_HEADS, s).T
    return jnp.pad(dk, ((0, 0), (0, LANES - N_MIX_HEADS)))


def _device_step(x, mem, target, small, get_weights, put_grads):
    s = x.shape[0]
    mt = mem.shape[0]
    bq = 512
    cos_t, sin_t = _rope_tables(s)
    row = lambda t, l: t[l][None, :]
    saved = []
    h = x
    cb8 = small["conv_b"].reshape(DEPTH, N_DEV, 1, FF_CHUNK)
    for l in range(DEPTH):
        kind, slot = l % 2, l // 2
        wl = dict(get_weights(l, "attn", h))
        if l == 0:
            xn = _rmsnorm_fwd(h, row(small["norm_mix"], l), br=512, name=f"norm_mix_fwd{l}")
        mn = _rmsnorm_fwd(mem, row(small["norm_mem"], l), br=mt, name=f"norm_mem_fwd{l}")
        proj = _mm(xn, wl["w_in"], "nn" if kind == 0 else "nt", tm=1024, tn=384 if kind == 0 else 512, layer=0, name=f"in_proj{l}")
        kvm = _mm(mn, wl["w_mem_kv"], "nn", tm=mt, tn=512, layer=0, name=f"mem_kv{l}")
        st = dict(h=h, xn=xn, mn=mn, proj=proj, kvm=kvm, w=wl)
        if kind == 0:
            b_pad = jnp.pad(small["b_forget"][slot], (0, LANES - N_MIX_HEADS))[None, :]
            c = _forget_cumsum(proj, b_pad, name=f"forget_cumsum{l}")
            ck6 = _bias_layout(c)
            mix, lse, heads = _attn_fwd(proj, proj, ck6, q_col=0, k_col=N_MIX_HP, v_col=2 * N_MIX_HP, n_hp=N_MIX_HP,
                                        causal=True, bq=min(s, 1024), bk=min(s, 1024), name=f"fox_fwd{l}")
            st.update(b_pad=b_pad, ck6=ck6, mix=mix, lse=lse)
        else:
            qk_r = _rope(proj, cos_t, sin_t, n_cols=2 * N_MIX_HP, out_dtype=F32, br=512, name=f"rope_fwd{l}")
            mix, lses, heads = _dil_fwd(qk_r, proj, name=f"dil_fwd{l}")
            st.update(qk_r=qk_r, lses=lses, mix=mix)
        mo, lse_m, heads = _attn_fwd(proj, kvm, None, q_col=QM_COL, k_col=0, v_col=N_MEM_HP, n_hp=N_MEM_HP, causal=False,
                                     bq=min(s, 2048), bk=mt, heads=heads, heads_col=N_MIX_HP, name=f"mem_fwd{l}")
        h1, xf = _mm(heads, wl["w_out"], "nn", tm=1024, tn=D_MODEL, res=h, layer=0, norm_gain=row(small["norm_ffn"], l),
                     name=f"out_proj{l}")
        wl.update(get_weights(l, "ffn", xf))
        u = _mm(xf, wl["w_up"], "nt", tm=1024, tn=FF_CHUNK, layer=0, chunk="b", name=f"up_proj{l}")
        a = _conv_fwd(u, wl["conv_w"], cb8[l], 0, name=f"conv_fwd{l}")
        st.update(mo=mo, lse_m=lse_m, heads=heads, h1=h1, xf=xf, u=u, a=a)
        saved.append(st)
        if l + 1 < DEPTH:
            h, xn = _mm(a, wl["w_down"], "nn", tm=512, tn=D_MODEL, res=h1, layer=0, chunk="reduce",
                        norm_gain=row(small["norm_mix"], l + 1), name=f"down_proj{l}")
        else:
            h = _mm(a, wl["w_down"], "nn", tm=1024, tn=512, res=h1, layer=0, chunk="reduce", name=f"down_proj{l}")

    dh, dhb, dg_final, loss = _loss_head(h, target, small["norm_final"][None, :], br=512, name="loss_head")
    gs = {k: [None] * DEPTH for k in ("norm_mix", "norm_mem", "norm_ffn", "conv_b")}
    gs["b_forget"] = [None] * 2
    dep = 0.0
    for l in reversed(range(DEPTH)):
        st = saved[l]
        wl = st["w"]
        gw = {}
        kind, slot = l % 2, l // 2
        da = _mm(dhb, wl["w_down"], "nt", tm=1024, tn=FF_CHUNK, layer=0, chunk="b", name=f"down_dx{l}")
        gw["w_down"] = _mm(st["a"], dhb, "tn", tm=FF_CHUNK, tn=512, out_dtype=BF16, chunk="a", name=f"down_dw{l}")
        du, dwb = _conv_bwd(st["u"], wl["conv_w"], cb8[l] + dep, da, 0, name=f"conv_bwd{l}")
        du = du.reshape(N_DEV, s, FF_CHUNK)
        dwb = dwb.reshape(N_DEV, 8, FF_CHUNK)
        gw["conv_w"] = dwb
        gs["conv_b"][l] = dwb[:, 3, :].reshape(-1)
        dh1, dh1b, dgf = _mm(du, wl["w_up"], "nn", tm=256, tn=D_MODEL, layer=0, chunk="reduce",
                             norm_bwd=(st["h1"], row(small["norm_ffn"], l), dh), name=f"up_dx{l}")
        gw["w_up"] = _mm(du, st["xf"], "tn", tm=FF_CHUNK, tn=512, out_dtype=BF16, chunk="a", name=f"up_dw{l}")
        gs["norm_ffn"][l] = dgf[0]
        dheads = _mm(dh1b, wl["w_out"], "nt", tm=1024, tn=512, layer=0, name=f"out_dx{l}")
        gw["w_out"] = _mm(st["heads"], dh1b, "tn", tm=512, tn=512, out_dtype=BF16, name=f"out_dw{l}")
        dqm, dkm, dvm = _attn_bwd(st["proj"], st["kvm"], st["mo"], dheads, st["lse_m"], None, q_col=QM_COL, k_col=0,
                                  v_col=N_MEM_HP, o_col=N_MIX_HP, n_hp=N_MEM_HP, causal=False, bq=min(s, 1024), bk=mt,
                                  name=f"mem_bwd{l}")
        dkvm = jnp.concatenate([dkm, dvm], axis=1)
        gw["w_mem_kv"] = _mm(st["mn"], dkvm, "tn", tm=512, tn=512, out_dtype=BF16, name=f"mem_kv_dw{l}")
        dmn = _mm(dkvm, wl["w_mem_kv"], "nt", tm=mt, tn=512, layer=0, name=f"mem_kv_dx{l}")
        dep_early = put_grads(l, "ffn", gw)
        _, _, dgm = _rmsnorm_bwd(mem, dmn, row(small["norm_mem"], l), None, br=mt, name=f"norm_mem_bwd{l}")
        gs["norm_mem"][l] = dgm[0]
        if kind == 0:
            dq, dk, dv, dcq6, dck6 = _attn_bwd(st["proj"], st["proj"], st["mix"], dheads, st["lse"], st["ck6"] + dep_early, q_col=0,
                                               k_col=N_MIX_HP, v_col=2 * N_MIX_HP, o_col=0, n_hp=N_MIX_HP, causal=True,
                                               bq=bq, bk=bq, name=f"fox_bwd{l}")
            dz, db = _forget_cumsum_bwd(st["proj"], st["b_pad"], _bias_grad(dcq6), _bias_grad(dck6), name=f"forget_cumsum_bwd{l}")
            gs["b_forget"][slot] = db[0, :N_MIX_HEADS]
            dproj = jnp.concatenate([dq.astype(BF16), dk, dv, dqm.astype(BF16), dz], axis=1)
        else:
            dq_r, dk_r, dv = _dil_bwd(st["qk_r"], st["proj"], st["mix"], dheads, st["lses"], name=f"dil_bwd{l}")
            dq = _rope(dq_r, cos_t + dep_early, -sin_t, n_cols=N_MIX_HP, out_dtype=BF16, br=512, name=f"rope_bwd_q{l}")
            dk = _rope(dk_r, cos_t, -sin_t, n_cols=N_MIX_HP, out_dtype=BF16, br=512, name=f"rope_bwd_k{l}")
            dproj = jnp.concatenate([dq, dk, dv.astype(BF16), dqm.astype(BF16)], axis=1)
        if kind == 0:
            gw["w_in"] = _mm(st["xn"], dproj, "tn", tm=512, tn=384, out_dtype=BF16, name=f"in_dw{l}")
        else:
            gw["w_in"] = _mm(dproj, st["xn"], "tn", tm=512, tn=512, out_dtype=BF16, name=f"in_dw{l}")
        dh, dhb, dgx = _mm(dproj, wl["w_in"], "nt" if kind == 0 else "nn", tm=512, tn=D_MODEL, layer=0,
                           norm_bwd=(st["h"], row(small["norm_mix"], l), dh1), name=f"in_dx{l}")
        gs["norm_mix"][l] = dgx[0]
        dep = put_grads(l, "attn", gw)

    grads_s = {k: jnp.stack(v) for k, v in gs.items()}
    grads_s["norm_final"] = dg_final[0]
    return loss[0, 0], dh, grads_s


def kernel(x, mem, norm_mix, norm_mem, norm_ffn, w_in_fox, b_forget, w_in_dil, w_mem_kv, w_out, w_up, conv_w, conv_b, w_down, norm_final, loss_target, m_norm_mix, m_norm_mem, m_norm_ffn, m_w_in_fox, m_b_forget, m_w_in_dil, m_w_mem_kv, m_w_out, m_w_up, m_conv_w, m_conv_b, m_w_down, m_norm_final, v_norm_mix, v_norm_mem, v_norm_ffn, v_w_in_fox, v_b_forget, v_w_in_dil, v_w_mem_kv, v_w_out, v_w_up, v_conv_w, v_conv_b, v_w_down, v_norm_final):
    names = ["norm_mix", "norm_mem", "norm_ffn", "w_in_fox", "b_forget", "w_in_dil", "w_mem_kv", "w_out", "w_up", "conv_w", "conv_b",
             "w_down", "norm_final"]
    w = dict(zip(names, (norm_mix, norm_mem, norm_ffn, w_in_fox, b_forget, w_in_dil, w_mem_kv, w_out, w_up, conv_w, conv_b, w_down, norm_final)))
    m = dict(zip(names, (m_norm_mix, m_norm_mem, m_norm_ffn, m_w_in_fox, m_b_forget, m_w_in_dil, m_w_mem_kv, m_w_out, m_w_up, m_conv_w,
                         m_conv_b, m_w_down, m_norm_final)))
    v = dict(zip(names, (v_norm_mix, v_norm_mem, v_norm_ffn, v_w_in_fox, v_b_forget, v_w_in_dil, v_w_mem_kv, v_w_out, v_w_up, v_conv_w,
                         v_conv_b, v_w_down, v_norm_final)))
    big = ("w_in_fox", "w_in_dil", "w_mem_kv", "w_out", "w_up", "w_down", "conv_w")
    small_shapes = [w[k].shape for k in SMALL]
    dil_c = w_in_dil.shape[2]
    rows = {k: w[k].shape[1] for k in ("w_in_fox", "w_mem_kv", "w_out", "w_down")}

    def places(l):
        w_in_place = _rows_of(rows["w_in_fox"]) if l % 2 == 0 else _slot1
        return [w_in_place, _rows_of(rows["w_mem_kv"]), _rows_of(rows["w_out"]), _slot1, _rows_of(rows["w_down"]), _slot1]

    def full_shapes(l):
        w_in_shape = (1, D_MODEL, FOX_P) if l % 2 == 0 else (1, N_DEV, dil_c, D_MODEL)
        return [w_in_shape, (1, D_MODEL, 2 * D_MEMQ), (1, D_MODEL, D_MODEL), (1, N_DEV, FF_CHUNK, D_MODEL), (1, D_FF, D_MODEL),
                (1, N_DEV, 3, FF_CHUNK)]

    transposed = lambda t: jnp.swapaxes(t, 1, 2)
    to_wire = {"w_in_fox": lambda t: _fox_permute(t).astype(BF16), "w_in_dil": lambda t: t.T.astype(BF16),
               "w_mem_kv": lambda t: t.astype(BF16), "w_out": lambda t: t.astype(BF16), "w_up": lambda t: t.T.astype(BF16),
               "w_down": lambda t: t.astype(BF16), "conv_w": lambda t: t}
    shard = lambda k, i: to_wire[k](w[k][i])
    shard_shape = lambda k: jax.eval_shape(lambda: shard(k, 0)).shape
    everything = (0, 1, 2, 3, 4, 5)
    gather_groups = {l: ((0, 1, 2), (3, 4, 5)) if l == 0 else (everything,) for l in range(DEPTH)}
    scatter_groups = {l: ((1, 2, 3, 4, 5), (0,)) if l == 0 else (everything,) for l in range(DEPTH)}
    pick = lambda seq, group: [seq[i] for i in group]
    tag = lambda l, group: f"{l}" + ("" if group == everything else "_" + "".join(str(i) for i in group))

    gathers, after = {}, norm_final
    for l in range(DEPTH):
        w_in_shard = shard("w_in_fox" if l % 2 == 0 else "w_in_dil", l // 2)[None]
        shards = [w_in_shard] + [shard(k, l)[None] for k in ("w_mem_kv", "w_out", "w_up", "w_down", "conv_w")]
        for group in gather_groups[l]:
            gathers[l, group] = _exchange_start("gather", pick(shards, group), pick(full_shapes(l), group), pick(places(l), group), after,
                                                name=f"weights_gather_start{tag(l, group)}")
            after = gathers[l, group]["token"]
    started = sum(g["token"][0, 0] for g in gathers.values())
    small = {k: w[k] for k in SMALL}
    small["norm_mix"] = norm_mix + started
    landed = {}

    def get_weights(l, part, h):
        group = [g for g in gather_groups[l] if (0 if part == "attn" else 3) in g][0]
        if (l, group) not in landed:
            lands = _exchange_wait("gather", gathers[l, group], pick(places(l), group), h, name=f"weights_gather_wait{tag(l, group)}")
            landed[l, group] = dict(zip(group, lands))
        got = landed[l, group]
        if part == "ffn":
            return dict(w_up=got[3], w_down=got[4].reshape(1, FF_HALF, FF_CHUNK, D_MODEL), conv_w=got[5])
        w_in = got[0] if l % 2 == 0 else got[0].reshape(1, N_DEV * dil_c, D_MODEL)
        return dict(w_in=w_in, w_mem_kv=got[1], w_out=got[2])

    scatters, pending = {}, {}

    def put_grads(l, part, g):
        pending.setdefault(l, {}).update(g)
        if part == "ffn" and len(scatter_groups[l]) == 1:
            return 0.0
        group = scatter_groups[l][0 if part == "ffn" else -1]
        have = pending[l]
        srcs = {3: lambda: have["w_up"][None], 4: lambda: have["w_down"].reshape(1, D_FF, D_MODEL), 5: lambda: have["conv_w"][None],
                1: lambda: have["w_mem_kv"][None], 2: lambda: have["w_out"][None]}
        if l % 2 == 0:
            srcs[0] = lambda: have["w_in"][None]
        else:
            srcs[0] = lambda: have["w_in"].reshape(1, N_DEV, dil_c, D_MODEL)
        shard_shapes = [shard_shape("w_in_fox" if l % 2 == 0 else "w_in_dil")] + \
            [shard_shape(k) for k in ("w_mem_kv", "w_out", "w_up", "w_down")] + [(8, FF_CHUNK)]
        sources = [srcs[i]() for i in group]
        scatters[l, group] = _exchange_start("scatter", sources, [(N_DEV, 1) + tuple(s) for s in pick(shard_shapes, group)],
                                             pick(places(l), group), sources[0], name=f"grads_scatter_start{tag(l, group)}")
        return scatters[l, group]["token"][0, 0]

    loss, grad_x, gs = _device_step(x[0], mem[0], loss_target[0], small, get_weights, put_grads)

    recv = {}

    def wait_scatter(l, group, after_):
        lands = _exchange_wait("scatter", scatters[l, group], pick(places(l), group), after_, name=f"grads_scatter_wait{tag(l, group)}")
        recv.setdefault(l, {}).update(zip(group, lands))

    for l in reversed(range(1, DEPTH)):
        wait_scatter(l, everything, grad_x)
    wait_scatter(0, scatter_groups[0][0], grad_x)
    s_pack = _pack([gs[k] for k in SMALL])
    (s_recv,) = _all_gather([s_pack], [(N_DEV,) + s_pack.shape], [_slot0], name="small_grads_all_gather")

    layer_tensors = ("w_in", "w_mem_kv", "w_out", "w_up", "w_down", "conv_w")
    layer_parts = lambda k: [recv[l][layer_tensors.index(k)] for l in range(DEPTH)]
    to_local = {k: (lambda t: t) for k in big}
    to_local["w_in_fox"] = _fox_permute
    to_local["w_in_dil"] = to_local["w_up"] = transposed
    from_local = {k: (lambda t: t) for k in big}
    from_local["w_in_fox"] = _fox_unpermute
    from_local["w_in_dil"] = from_local["w_up"] = transposed
    blocks = {"w_in_fox": rows["w_in_fox"], "w_in_dil": dil_c, "w_mem_kv": rows["w_mem_kv"], "w_out": rows["w_out"], "w_up": FF_CHUNK // 4,
              "w_down": rows["w_down"] // 2, "conv_w": 3}
    outs = {}

    def update(k, parts):
        f = to_local[k]
        outs[k] = [from_local[k](t) for t in _adamw(parts, f(w[k]), f(m[k]), f(v[k]), br=blocks[k], name=f"adamw_{k}")]

    update("w_in_dil", [recv[l][0] for l in range(1, DEPTH, 2)])
    update("w_up", layer_parts("w_up"))
    update("w_down", layer_parts("w_down"))
    update("conv_w", [p[:, :, :3, :] for p in layer_parts("conv_w")])
    update("w_mem_kv", layer_parts("w_mem_kv"))
    update("w_out", layer_parts("w_out"))
    wait_scatter(0, scatter_groups[0][-1], outs["w_out"][1])
    update("w_in_fox", [recv[l][0] for l in range(0, DEPTH, 2)])
    small_outs = _adamw([s_recv], _pack([w[k] for k in SMALL]), _pack([m[k] for k in SMALL]), _pack([v[k] for k in SMALL]),
                        br=s_pack.shape[1], name="adamw_small")
    res = []
    for i, os_ in enumerate(small_outs):
        d = {k: outs[k][i] for k in big}
        d.update(zip(SMALL, _unpack(os_, small_shapes)))
        res.append([d[k] for k in names])
    loss = lax.psum(loss, ("x", "y", "c"))
    return (loss, grad_x[None], *res[0], *res[1], *res[2], *res[3])
```

```python
import functools
import math

import jax
import jax.numpy as jnp
from jax import lax
from jax.experimental import pallas as pl
from jax.experimental.pallas import tpu as pltpu

F32 = jnp.float32
BF16 = jnp.bfloat16

D_MODEL = 1024
HEAD_DIM = 64
N_MIX_HEADS = 12
N_MEM_HEADS = 4
D_MIX = N_MIX_HEADS * HEAD_DIM
D_MEMQ = N_MEM_HEADS * HEAD_DIM
D_FF = 2816
DEPTH = 4
FOX_IN = 3 * D_MIX + N_MIX_HEADS + D_MEMQ
DIL_IN = 3 * D_MIX + D_MEMQ
LANES = 128
FOX_P = DIL_IN + LANES
N_MIX_HP = D_MIX // LANES
N_MEM_HP = D_MEMQ // LANES
QM_COL = 3 * N_MIX_HP
F_COL = DIL_IN // LANES
DILATED_BRANCHES = ((128, 1), (512, 4), (2048, 16))
DIL_L = 128
DIL_UNROLL_FWD = 8
DIL_UNROLL_BWD = 8
ROPE_THETA = 10000.0
NORM_EPS = 1e-6
NEG = -1e30
SCALE = HEAD_DIM ** -0.5
N_DEV = 8

ADAM_LR = 0.001
ADAM_B1 = 0.9
ADAM_B2 = 0.999
ADAM_EPS = 1e-08
ADAM_WD = 0.01
ADAM_STEP = 10

VMEM_LIMIT = 56 * 1024 * 1024
PACK_W = 1024
PACK_ROW_ALIGN = 8

MESH = pl.DeviceIdType.MESH
NT = (((1,), (1,)), ((), ()))
NN = (((1,), (0,)), ((), ()))
TN = (((0,), (0,)), ((), ()))


def _params(*sem):
    return pltpu.CompilerParams(dimension_semantics=sem, vmem_limit_bytes=VMEM_LIMIT)


def _lane_lo(shape):
    return lax.broadcasted_iota(jnp.int32, shape, len(shape) - 1) < HEAD_DIM


def _pair(lo, a, b):
    return jnp.where(lo, a, b)


def _mm(a, b, mode, *, tm, tn, name, out_dtype=F32, res=None, layer=None, chunk=None, norm_gain=None, norm_bwd=None, rope=None):
    lead = () if layer is None else (layer,)
    nl = (None,) * len(lead)
    bs = b.shape[len(lead):]
    dims = {"nn": NN, "nt": NT, "tn": TN}[mode]
    reduce_n = 0
    if chunk is None:
        (m, k) = a.shape[::-1] if mode == "tn" else a.shape
        n = bs[0] if mode == "nt" else bs[1]
        grid = (m // tm, n // tn)
        a_spec = pl.BlockSpec((k, tm), lambda i, j: (0, i)) if mode == "tn" else pl.BlockSpec((tm, k), lambda i, j: (i, 0))
        b_spec = pl.BlockSpec(nl + ((tn, k) if mode == "nt" else (k, tn)), lambda i, j: lead + ((j, 0) if mode == "nt" else (0, j)))
        o_spec = pl.BlockSpec((tm, tn), lambda i, j: (i, j))
        out_shape = (m, n)
    elif chunk == "b":
        (m, k) = a.shape[::-1] if mode == "tn" else a.shape
        c, nc = bs[0], (bs[1] if mode == "nt" else bs[2])
        grid = (m // tm, c)
        a_spec = pl.BlockSpec((k, tm), lambda i, j: (0, i)) if mode == "tn" else pl.BlockSpec((tm, k), lambda i, j: (i, 0))
        b_spec = pl.BlockSpec(nl + (None,) + tuple(bs[1:]), lambda i, j: lead + (j, 0, 0))
        o_spec = pl.BlockSpec((None, tm, nc), lambda i, j: (j, i, 0))
        out_shape = (c, m, nc)
    elif chunk == "a":
        assert mode == "tn"
        c, k, mc = a.shape
        n = bs[1]
        grid = (c, n // tn)
        a_spec = pl.BlockSpec((None, k, mc), lambda i, j: (i, 0, 0))
        b_spec = pl.BlockSpec(nl + (k, tn), lambda i, j: lead + (0, j))
        o_spec = pl.BlockSpec((None, mc, tn), lambda i, j: (i, 0, j))
        out_shape = (c, mc, n)
    else:
        reduce_n, m, kc = a.shape
        n = bs[1] if mode == "nt" else bs[2]
        grid = (m // tm, n // tn)
        a_spec = pl.BlockSpec((reduce_n, tm, kc), lambda i, j: (0, i, 0))
        b_spec = pl.BlockSpec(nl + ((reduce_n, tn, kc) if mode == "nt" else (reduce_n, kc, tn)),
                              lambda i, j: lead + ((0, j, 0) if mode == "nt" else (0, 0, j)))
        o_spec = pl.BlockSpec((tm, tn), lambda i, j: (i, j))
        out_shape = (m, n)

    if norm_gain is not None or norm_bwd is not None:
        assert chunk in (None, "reduce") and tn == n, "the RMSNorm of the result needs whole rows in a block"

    def body(*refs):
        a_ref, b_ref = refs[0], refs[1]
        dot = lambda x, y: lax.dot_general(x.astype(BF16), y.astype(BF16), dims, preferred_element_type=F32)
        if reduce_n:
            acc = dot(a_ref[0], b_ref[0])
            for r in range(1, reduce_n):
                acc = acc + dot(a_ref[r], b_ref[r])
        else:
            acc = dot(a_ref[...], b_ref[...])
        if norm_bwd is not None:
            x_ref, g_ref, r_ref = refs[2:5]
            dx_ref, dxb_ref, dg_ref = refs[-3:]
            dx, dg = _rms_bwd_math(x_ref[...], acc, g_ref[...])
            dx = dx + r_ref[...]
            dx_ref[...] = dx
            dxb_ref[...] = dx.astype(BF16)

            @pl.when(pl.program_id(0) == 0)
            def _():
                dg_ref[...] = jnp.zeros_like(dg_ref)

            dg_ref[0:1, :] += dg
            return
        o_ref = refs[-2] if (norm_gain is not None or rope is not None) else refs[-1]
        if res is not None:
            acc = acc + refs[2][...]
        o_ref[...] = acc.astype(o_ref.dtype)
        if norm_gain is not None:
            rs = lax.rsqrt(jnp.mean(acc * acc, axis=-1, keepdims=True) + NORM_EPS)
            refs[-1][...] = (acc * rs * refs[3][...]).astype(BF16)
        if rope is not None:
            @pl.when(pl.program_id(1) < rope[2] // tn)
            def _():
                cos, sin = refs[2][...], refs[3][...]
                for c in range(tn // LANES):
                    lanes = slice(c * LANES, (c + 1) * LANES)
                    refs[-1][:, lanes] = _rope_tile(acc[:, lanes], cos, sin)

    ins = [a, b] + ([res] if res is not None else [])
    specs = [a_spec, b_spec] + ([o_spec] if res is not None else [])
    out_shapes, out_specs = jax.ShapeDtypeStruct(out_shape, out_dtype), o_spec
    sem = ("parallel", "parallel")
    if rope is not None:
        assert chunk is None and res is None and norm_gain is None and norm_bwd is None and rope[2] % tn == 0
        tab = pl.BlockSpec((tm, LANES), lambda i, j: (i, 0))
        ins += [rope[0], rope[1]]
        specs += [tab, tab]
        r_spec = pl.BlockSpec((tm, tn), lambda i, j: (i, jnp.minimum(j, rope[2] // tn - 1)))
        out_shapes, out_specs = (out_shapes, jax.ShapeDtypeStruct((m, rope[2]), F32)), (o_spec, r_spec)
        sem = ("parallel", "arbitrary")
    if norm_gain is not None:
        assert res is not None
        ins.append(norm_gain)
        specs.append(pl.BlockSpec((1, n), lambda i, j: (0, 0)))
        out_shapes, out_specs = (out_shapes, jax.ShapeDtypeStruct(out_shape, BF16)), (o_spec, o_spec)
    if norm_bwd is not None:
        assert res is None and norm_gain is None
        x_in, gain, resid = norm_bwd
        ins += [x_in, gain, resid]
        specs += [o_spec, pl.BlockSpec((1, n), lambda i, j: (0, 0)), o_spec]
        out_shapes = (jax.ShapeDtypeStruct(out_shape, F32), jax.ShapeDtypeStruct(out_shape, BF16), jax.ShapeDtypeStruct((8, n), F32))
        out_specs = (o_spec, o_spec, pl.BlockSpec((8, n), lambda i, j: (0, 0)))
        sem = ("arbitrary", "arbitrary")
    return pl.pallas_call(body, out_shape=out_shapes, grid=grid, in_specs=specs, out_specs=out_specs,
                          compiler_params=_params(*sem), name=name)(*ins)


def _rmsnorm_fwd(x, g, *, br, name):
    r, d = x.shape

    def body(x_ref, g_ref, o_ref):
        xf = x_ref[...]
        rs = lax.rsqrt(jnp.mean(xf * xf, axis=-1, keepdims=True) + NORM_EPS)
        o_ref[...] = (xf * rs * g_ref[...]).astype(BF16)

    return pl.pallas_call(body, out_shape=jax.ShapeDtypeStruct((r, d), BF16), grid=(r // br,),
                          in_specs=[pl.BlockSpec((br, d), lambda i: (i, 0)), pl.BlockSpec((1, d), lambda i: (0, 0))],
                          out_specs=pl.BlockSpec((br, d), lambda i: (i, 0)), compiler_params=_params("parallel"), name=name)(x, g)


def _rms_bwd_math(x, dy, g):
    d = x.shape[-1]
    rs = lax.rsqrt(jnp.mean(x * x, axis=-1, keepdims=True) + NORM_EPS)
    gy = dy * g
    proj = jnp.sum(x * gy, axis=-1, keepdims=True) * (1.0 / d)
    dx = rs * gy - x * (rs * rs * rs) * proj
    dg = jnp.sum(dy * (x * rs), axis=0, keepdims=True)
    return dx, dg


def _rmsnorm_bwd(x, dy, g, res, *, br, name):
    r, d = x.shape
    has_res = res is not None

    def body(*refs):
        x_ref, dy_ref, g_ref = refs[:3]
        dx_ref, dxb_ref, dg_ref = refs[-3:]
        dx, dg = _rms_bwd_math(x_ref[...], dy_ref[...], g_ref[...])
        if has_res:
            dx = dx + refs[3][...]
        dx_ref[...] = dx
        dxb_ref[...] = dx.astype(BF16)

        @pl.when(pl.program_id(0) == 0)
        def _():
            dg_ref[...] = jnp.zeros_like(dg_ref)

        dg_ref[0:1, :] += dg

    row = pl.BlockSpec((br, d), lambda i: (i, 0))
    ins = [x, dy, g] + ([res] if has_res else [])
    specs = [row, row, pl.BlockSpec((1, d), lambda i: (0, 0))] + ([row] if has_res else [])
    return pl.pallas_call(
        body, out_shape=(jax.ShapeDtypeStruct((r, d), F32), jax.ShapeDtypeStruct((r, d), BF16), jax.ShapeDtypeStruct((8, d), F32)),
        grid=(r // br,), in_specs=specs, out_specs=(row, row, pl.BlockSpec((8, d), lambda i: (0, 0))),
        compiler_params=_params("arbitrary"), name=name)(*ins)


def _loss_head(h, target, g, *, br, name):
    r, d = h.shape

    def body(x_ref, t_ref, g_ref, dx_ref, dxb_ref, dg_ref, loss_ref):
        x = x_ref[...]
        gg = g_ref[...]
        rs = lax.rsqrt(jnp.mean(x * x, axis=-1, keepdims=True) + NORM_EPS)
        err = x * rs * gg - t_ref[...]
        part = jnp.sum(jnp.sum(err * err, axis=1, keepdims=True), axis=0, keepdims=True) * (0.5 / d)
        dx, dg = _rms_bwd_math(x, err * (1.0 / d), gg)
        dx_ref[...] = dx
        dxb_ref[...] = dx.astype(BF16)

        @pl.when(pl.program_id(0) == 0)
        def _():
            dg_ref[...] = jnp.zeros_like(dg_ref)
            loss_ref[...] = jnp.zeros_like(loss_ref)

        dg_ref[0:1, :] += dg
        loss_ref[...] += jnp.broadcast_to(part, loss_ref.shape)

    row = pl.BlockSpec((br, d), lambda i: (i, 0))
    return pl.pallas_call(
        body, out_shape=(jax.ShapeDtypeStruct((r, d), F32), jax.ShapeDtypeStruct((r, d), BF16),
                         jax.ShapeDtypeStruct((8, d), F32), jax.ShapeDtypeStruct((8, LANES), F32)),
        grid=(r // br,), in_specs=[row, row, pl.BlockSpec((1, d), lambda i: (0, 0))],
        out_specs=(row, row, pl.BlockSpec((8, d), lambda i: (0, 0)), pl.BlockSpec((8, LANES), lambda i: (0, 0))),
        compiler_params=_params("arbitrary"), name=name)(h, target, g)


def _split3(x):
    hi = x.astype(BF16)
    r1 = x - hi.astype(F32)
    mid = r1.astype(BF16)
    lo = (r1 - mid.astype(F32)).astype(BF16)
    return hi, mid, lo


def _tri_sum(tri, x):
    hi, mid, lo = _split3(x)
    dot = lambda t: jnp.dot(tri, t, preferred_element_type=F32)
    return dot(hi) + dot(mid) + dot(lo)


def _forget_cumsum(proj, b_pad, *, name):
    s = proj.shape[0]
    blk = LANES

    def body(f_ref, b_ref, c_ref):
        ri = lax.broadcasted_iota(jnp.int32, (blk, blk), 0)
        ci = lax.broadcasted_iota(jnp.int32, (blk, blk), 1)
        tri = (ci <= ri).astype(BF16)
        bias = b_ref[...]

        def step(t, carry):
            rows = pl.ds(pl.multiple_of(t * blk, blk), blk)
            z = f_ref[rows, :] + bias
            lf = jnp.minimum(z, 0.0) - jnp.log(1.0 + jnp.exp(-jnp.abs(z)))
            cs = _tri_sum(tri, lf) + carry
            c_ref[rows, :] = cs
            return cs[blk - 1:blk, :]

        lax.fori_loop(0, s // blk, step, jnp.zeros((1, blk), F32))

    return pl.pallas_call(body, out_shape=jax.ShapeDtypeStruct((s, LANES), F32), grid=(1,),
                          in_specs=[pl.BlockSpec((s, LANES), lambda i: (0, F_COL)), pl.BlockSpec((1, LANES), lambda i: (0, 0))],
                          out_specs=pl.BlockSpec((s, LANES), lambda i: (0, 0)), compiler_params=_params("arbitrary"), name=name)(proj, b_pad)


def _forget_cumsum_bwd(proj, b_pad, dcq, dck, *, name):
    s = proj.shape[0]
    blk = LANES
    nblk = s // blk

    def body(f_ref, b_ref, dcq_ref, dck_ref, dz_ref, db_ref):
        ri = lax.broadcasted_iota(jnp.int32, (blk, blk), 0)
        ci = lax.broadcasted_iota(jnp.int32, (blk, blk), 1)
        triu = (ci >= ri).astype(BF16)
        bias = b_ref[...]

        def step(t, carry):
            tail, dbs = carry
            rows = pl.ds(pl.multiple_of((nblk - 1 - t) * blk, blk), blk)
            dc = dcq_ref[rows, :] - dck_ref[rows, :]
            dlf = _tri_sum(triu, dc) + tail
            z = f_ref[rows, :] + bias
            e = jnp.exp(-jnp.abs(z))
            sig_neg = jnp.where(z >= 0.0, e, 1.0) / (1.0 + e)
            dz = dlf * sig_neg
            dz_ref[rows, :] = dz.astype(BF16)
            return dlf[0:1, :], dbs + jnp.sum(dz, axis=0, keepdims=True)

        _, dbs = lax.fori_loop(0, nblk, step, (jnp.zeros((1, blk), F32), jnp.zeros((1, blk), F32)))
        db_ref[...] = jnp.broadcast_to(dbs, db_ref.shape)

    full = pl.BlockSpec((s, LANES), lambda i: (0, 0))
    return pl.pallas_call(body, out_shape=(jax.ShapeDtypeStruct((s, LANES), BF16), jax.ShapeDtypeStruct((8, LANES), F32)), grid=(1,),
                          in_specs=[pl.BlockSpec((s, LANES), lambda i: (0, F_COL)), pl.BlockSpec((1, LANES), lambda i: (0, 0)), full, full],
                          out_specs=(full, pl.BlockSpec((8, LANES), lambda i: (0, 0))), compiler_params=_params("arbitrary"), name=name)(proj, b_pad, dcq, dck)


def _attn_fwd(q_arr, kv_arr, ck6, *, q_col, k_col, v_col, n_hp, causal, bq, bk, name, heads=None, heads_col=0):
    s = q_arr.shape[0]
    skv = kv_arr.shape[0]
    bias = ck6 is not None
    nq = s // bq
    assert not causal or bq == bk

    def body(*refs):
        q_ref, k_ref, v_ref = refs[:3]
        ck_ref = refs[3] if bias else None
        o_ref, lse_ref, heads_ref = refs[-3:]
        i = pl.program_id(1)
        lo = _lane_lo((bq, LANES))
        q = q_ref[...] * SCALE
        qh = (jnp.where(lo, q, 0.0).astype(BF16), jnp.where(lo, 0.0, q).astype(BF16))

        def block(j, carry, diagonal):
            ks = pl.ds(pl.multiple_of(j * bk, bk), bk)
            k = k_ref[ks, :].astype(BF16)
            v = v_ref[ks, :].astype(BF16)
            if diagonal:
                ok = lax.broadcasted_iota(jnp.int32, (bq, bk), 1) <= lax.broadcasted_iota(jnp.int32, (bq, bk), 0)
            out = []
            for h in range(2):
                m, l, acc = carry[3 * h:3 * h + 3]
                sc = lax.dot_general(qh[h], k, NT, preferred_element_type=F32)
                if bias:
                    sc = sc - ck_ref[0, h:h + 1, ks]
                if diagonal:
                    sc = jnp.where(ok, sc, NEG)
                mn = jnp.maximum(m, jnp.max(sc, axis=1, keepdims=True))
                p = jnp.exp(sc - mn)
                al = jnp.exp(m - mn)
                out += [mn, al * l + jnp.sum(p, axis=1, keepdims=True), al * acc + jnp.dot(p.astype(BF16), v, preferred_element_type=F32)]
            return tuple(out)

        col = lambda v_: jnp.full((bq, 1), v_, F32)
        init = (col(NEG), col(0.0), jnp.zeros((bq, LANES), F32)) * 2
        n_full = i if causal else skv // bk
        carry = lax.fori_loop(0, n_full, functools.partial(block, diagonal=False), init)
        if causal:
            carry = block(i, carry, True)
        m0, l0, a0, m1, l1, a1 = carry
        out = _pair(lo, a0 / l0, a1 / l1)
        o_ref[...] = out
        lse_ref[0] = _pair(lo, m0 + jnp.log(l0), m1 + jnp.log(l1))
        heads_ref[...] = out.astype(BF16)

    specs = [pl.BlockSpec((bq, LANES), lambda h, i: (i, q_col + h)),
             pl.BlockSpec((skv, LANES), lambda h, i: (0, k_col + h)),
             pl.BlockSpec((skv, LANES), lambda h, i: (0, v_col + h))]
    ins = [q_arr, kv_arr, kv_arr]
    if bias:
        specs += [pl.BlockSpec((1, 8, skv), lambda h, i: (h, 0, 0))]
        ins += [ck6]
    aliases = {}
    if heads is not None:
        aliases = {len(ins): 2}
        specs += [pl.BlockSpec(memory_space=pl.ANY)]
        ins += [heads]
    return pl.pallas_call(
        body, out_shape=(jax.ShapeDtypeStruct((s, n_hp * LANES), F32), jax.ShapeDtypeStruct((n_hp, s, LANES), F32),
                         jax.ShapeDtypeStruct((s, D_MODEL), BF16)),
        grid=(n_hp, nq), in_specs=specs,
        out_specs=(pl.BlockSpec((bq, LANES), lambda h, i: (i, h)), pl.BlockSpec((1, bq, LANES), lambda h, i: (h, i, 0)),
                   pl.BlockSpec((bq, LANES), lambda h, i: (i, heads_col + h))),
        input_output_aliases=aliases, compiler_params=_params("parallel", "parallel"), name=name)(*ins)


def _attn_bwd(q_arr, kv_arr, o_arr, do_arr, lse, ck6, *, q_col, k_col, v_col, o_col, n_hp, causal, bq, bk, name):
    s = q_arr.shape[0]
    skv = kv_arr.shape[0]
    bias = ck6 is not None
    nq = s // bq
    assert not causal or bq == bk

    def body(*refs):
        q_ref, k_ref, v_ref, o_ref, do_ref, lse_ref = refs[:6]
        if bias:
            ck_ref = refs[6]
            dq_ref, dk_ref, dv_ref, dcq_ref, dck_ref = refs[-5:]
        else:
            dq_ref, dk_ref, dv_ref = refs[-3:]
        j = pl.program_id(1)
        lo_q = _lane_lo((bq, LANES))
        lo_k = _lane_lo((bk, LANES))
        k = k_ref[...]
        v = v_ref[...].astype(BF16)
        kb = k.astype(BF16)
        kh = (jnp.where(lo_k, k, 0.0).astype(BF16), jnp.where(lo_k, 0.0, k).astype(BF16))
        if bias:
            pick_k = [(lax.broadcasted_iota(jnp.int32, (8, bk), 0) == h).astype(BF16) for h in range(2)]
            pick_q = [(lax.broadcasted_iota(jnp.int32, (8, bq), 0) == h).astype(BF16) for h in range(2)]

        @pl.when(j == 0)
        def _():
            dq_ref[...] = jnp.zeros_like(dq_ref)
            if bias:
                dcq_ref[...] = jnp.zeros_like(dcq_ref)

        def block(i, carry, diagonal):
            dk_acc, dv_acc, cs = carry
            qs = pl.ds(pl.multiple_of(i * bq, bq), bq)
            q = q_ref[qs, :] * SCALE
            do = do_ref[qs, :]
            dd = do * o_ref[qs, :]
            lse_i = lse_ref[0, qs, :]
            qh = (jnp.where(lo_q, q, 0.0).astype(BF16), jnp.where(lo_q, 0.0, q).astype(BF16))
            doh = (jnp.where(lo_q, do, 0.0).astype(BF16), jnp.where(lo_q, 0.0, do).astype(BF16))
            dh = (jnp.sum(jnp.where(lo_q, dd, 0.0), axis=1, keepdims=True), jnp.sum(jnp.where(lo_q, 0.0, dd), axis=1, keepdims=True))
            if diagonal:
                ok = lax.broadcasted_iota(jnp.int32, (bq, bk), 1) <= lax.broadcasted_iota(jnp.int32, (bq, bk), 0)
            dq_blk = None
            rs = None
            for h in range(2):
                sc = lax.dot_general(qh[h], kb, NT, preferred_element_type=F32)
                if bias:
                    sc = sc - ck_ref[0, h:h + 1, :]
                if diagonal:
                    sc = jnp.where(ok, sc, NEG)
                p = jnp.exp(sc - lse_i[:, h * HEAD_DIM:h * HEAD_DIM + 1])
                ds = p * (lax.dot_general(doh[h], v, NT, preferred_element_type=F32) - dh[h])
                dsb = ds.astype(BF16)
                dv_acc = dv_acc + lax.dot_general(p.astype(BF16), doh[h], TN, preferred_element_type=F32)
                dk_acc = dk_acc + lax.dot_general(dsb, qh[h], TN, preferred_element_type=F32)
                part = jnp.dot(dsb, kh[h], preferred_element_type=F32)
                dq_blk = part if dq_blk is None else dq_blk + part
                if bias:
                    cs = cs + jnp.dot(pick_q[h], dsb, preferred_element_type=F32)
                    row_sums = lax.dot_general(pick_k[h], dsb, NT, preferred_element_type=F32)
                    rs = row_sums if rs is None else rs + row_sums
            dq_ref[qs, :] += dq_blk * SCALE
            if bias:
                dcq_ref[0, :, qs] += rs
            return dk_acc, dv_acc, cs

        carry = (jnp.zeros((bk, LANES), F32), jnp.zeros((bk, LANES), F32), jnp.zeros((8, bk), F32))
        if causal:
            carry = block(j, carry, True)
        dk_acc, dv_acc, cs = lax.fori_loop(j + 1 if causal else 0, nq, functools.partial(block, diagonal=False), carry)
        dk_ref[...] = dk_acc.astype(BF16)
        dv_ref[...] = dv_acc.astype(BF16)
        if bias:
            dck_ref[0] = cs

    full_q = lambda c: pl.BlockSpec((s, LANES), lambda h, j: (0, c + h))
    specs = [full_q(q_col),
             pl.BlockSpec((bk, LANES), lambda h, j: (j, k_col + h)),
             pl.BlockSpec((bk, LANES), lambda h, j: (j, v_col + h)),
             full_q(0), full_q(o_col),
             pl.BlockSpec((1, s, LANES), lambda h, j: (h, 0, 0))]
    ins = [q_arr, kv_arr, kv_arr, o_arr, do_arr, lse]
    out_shape = [jax.ShapeDtypeStruct((s, n_hp * LANES), F32), jax.ShapeDtypeStruct((skv, n_hp * LANES), BF16),
                 jax.ShapeDtypeStruct((skv, n_hp * LANES), BF16)]
    out_specs = [full_q(0), pl.BlockSpec((bk, LANES), lambda h, j: (j, h)), pl.BlockSpec((bk, LANES), lambda h, j: (j, h))]
    if bias:
        specs += [pl.BlockSpec((1, 8, bk), lambda h, j: (h, 0, j))]
        ins += [ck6]
        out_shape += [jax.ShapeDtypeStruct((n_hp, 8, s), F32), jax.ShapeDtypeStruct((n_hp, 8, skv), F32)]
        out_specs += [pl.BlockSpec((1, 8, s), lambda h, j: (h, 0, 0)), pl.BlockSpec((1, 8, bk), lambda h, j: (h, 0, j))]
    return pl.pallas_call(body, out_shape=tuple(out_shape), grid=(n_hp, skv // bk), in_specs=specs, out_specs=tuple(out_specs),
                          compiler_params=_params("parallel", "arbitrary"), name=name)(*ins)


def _rope_tables(s):
    inv = 1.0 / (ROPE_THETA ** (jnp.arange(0, HEAD_DIM, 2, dtype=F32) / HEAD_DIM))
    ang = jnp.arange(s, dtype=F32)[:, None] * inv[None, :]
    cos, sin = jnp.cos(ang), jnp.sin(ang)
    return jnp.tile(cos, (1, 4)), jnp.concatenate([-sin, sin, -sin, sin], axis=1)


def _rope_tile(x, cos, sin):
    first = (lax.broadcasted_iota(jnp.int32, x.shape, 1) % HEAD_DIM) < (HEAD_DIM // 2)
    swapped = jnp.where(first, pltpu.roll(x, LANES - HEAD_DIM // 2, 1), pltpu.roll(x, HEAD_DIM // 2, 1))
    return x * cos + swapped * sin


def _rope(x_arr, cos_t, sin_t, *, n_cols, out_dtype, br, name):
    s = x_arr.shape[0]

    def body(x_ref, c_ref, s_ref, o_ref):
        cos, sin = c_ref[...], s_ref[...]
        for j in range(n_cols):
            lanes = slice(j * LANES, (j + 1) * LANES)
            o_ref[:, lanes] = _rope_tile(x_ref[:, lanes].astype(F32), cos, sin).astype(o_ref.dtype)

    tab = pl.BlockSpec((br, LANES), lambda i: (i, 0))
    blk = pl.BlockSpec((br, n_cols * LANES), lambda i: (i, 0))
    return pl.pallas_call(body, out_shape=jax.ShapeDtypeStruct((s, n_cols * LANES), out_dtype), grid=(s // br,),
                          in_specs=[blk, tab, tab], out_specs=blk, compiler_params=_params("parallel"), name=name)(x_arr, cos_t, sin_t)


def _stack_heads(x):
    lo = _lane_lo(x.shape)
    return jnp.concatenate([jnp.where(lo, x, 0.0), jnp.where(lo, 0.0, x)], axis=0).astype(BF16)


def _unstack_heads(x):
    return jnp.where(_lane_lo((DIL_L, LANES)), x[:DIL_L], x[DIL_L:])


def _dil_scores(q_ref, k_ref, cur, prev, has_prev):
    qs = _stack_heads(q_ref[cur, :] * SCALE)
    kk = jnp.concatenate([k_ref[prev, :], k_ref[cur, :]], axis=0).astype(BF16)
    a = lax.broadcasted_iota(jnp.int32, (2 * DIL_L, 2 * DIL_L), 0) & (DIL_L - 1)
    c = lax.broadcasted_iota(jnp.int32, (2 * DIL_L, 2 * DIL_L), 1)
    ok = ((c < DIL_L) & (c >= a) & has_prev) | ((c >= DIL_L) & (c - DIL_L <= a))
    return qs, kk, jnp.where(ok, lax.dot_general(qs, kk, NT, preferred_element_type=F32), NEG)


def _dil_rows(t, dil):
    r, m = t % dil, t // dil
    start = m * (DIL_L * dil) + r
    prev = jnp.maximum(start - DIL_L * dil, 0)
    return pl.ds(start, DIL_L, stride=dil), pl.ds(prev, DIL_L, stride=dil), m > 0


def _softmax3(a, b, c):
    m = jnp.maximum(jnp.maximum(a, b), c)
    ea, eb, ec = jnp.exp(a - m), jnp.exp(b - m), jnp.exp(c - m)
    den = ea + eb + ec
    inv = 1.0 / den
    return ea * inv, eb * inv, ec * inv, m + jnp.log(den)


def _dil_fwd(qk_r, proj, *, name):
    s = qk_r.shape[0]
    nsub = s // DIL_L
    mb = 512

    def body(q_ref, k_ref, v_ref, mix_ref, l1_ref, l2_ref, l3_ref, heads_ref, o1_scr, o2_scr, o3_scr):
        for (_, dil), o_scr, l_ref in zip(DILATED_BRANCHES, (o1_scr, o2_scr, o3_scr), (l1_ref, l2_ref, l3_ref)):
            def step(t, carry, dil=dil, o_scr=o_scr, l_ref=l_ref):
                cur, prev, has_prev = _dil_rows(t, dil)
                _, _, sc = _dil_scores(q_ref, k_ref, cur, prev, has_prev)
                vv = jnp.concatenate([v_ref[prev, :], v_ref[cur, :]], axis=0).astype(BF16)
                m = jnp.max(sc, axis=1, keepdims=True)
                e = jnp.exp(sc - m)
                den = jnp.sum(e, axis=1, keepdims=True)
                o = jnp.dot((e * (1.0 / den)).astype(BF16), vv, preferred_element_type=F32)
                o_scr[cur, :] = _unstack_heads(o)
                l_ref[cur, :] = _unstack_heads(jnp.broadcast_to(m + jnp.log(den), (2 * DIL_L, LANES)))
                return carry

            lax.fori_loop(0, nsub, step, 0, unroll=DIL_UNROLL_FWD)

        def merge(i, carry):
            rows = pl.ds(pl.multiple_of(i * mb, mb), mb)
            wa, wb, wc, _ = _softmax3(l1_ref[rows, :], l2_ref[rows, :], l3_ref[rows, :])
            mix = wa * o1_scr[rows, :] + wb * o2_scr[rows, :] + wc * o3_scr[rows, :]
            mix_ref[rows, :] = mix
            heads_ref[rows, :] = mix.astype(BF16)
            return carry

        lax.fori_loop(0, s // mb, merge, 0)

    col = lambda arr_col: pl.BlockSpec((s, LANES), lambda h: (0, arr_col + h))
    shp = jax.ShapeDtypeStruct((s, D_MIX), F32)
    mix, l1, l2, l3, heads = pl.pallas_call(
        body, out_shape=(shp, shp, shp, shp, jax.ShapeDtypeStruct((s, D_MODEL), BF16)), grid=(N_MIX_HP,),
        in_specs=[col(0), col(N_MIX_HP), col(2 * N_MIX_HP)],
        out_specs=(col(0),) * 5, scratch_shapes=[pltpu.VMEM((s, LANES), F32)] * 3,
        compiler_params=_params("parallel"), name=name)(qk_r, qk_r, proj)
    return mix, (l1, l2, l3), heads


def _dil_bwd(qk_r, proj, mix, dheads, lses, *, name):
    s = qk_r.shape[0]
    nsub = s // DIL_L
    mb = 512

    def body(q_ref, k_ref, v_ref, mix_ref, dm_ref, l1_ref, l2_ref, l3_ref, dq_ref, dk_ref, dv_ref, lt_scr, dd_scr):
        lo = _lane_lo((DIL_L, LANES))
        lo_m = _lane_lo((mb, LANES))

        def prep(i, carry):
            rows = pl.ds(pl.multiple_of(i * mb, mb), mb)
            _, _, _, lt = _softmax3(l1_ref[rows, :], l2_ref[rows, :], l3_ref[rows, :])
            lt_scr[rows, :] = lt
            dd = dm_ref[rows, :] * mix_ref[rows, :]
            dd_scr[rows, :] = _pair(lo_m, jnp.sum(jnp.where(lo_m, dd, 0.0), axis=1, keepdims=True),
                                    jnp.sum(jnp.where(lo_m, 0.0, dd), axis=1, keepdims=True))
            zero = jnp.zeros((mb, LANES), F32)
            dq_ref[rows, :] = zero
            dk_ref[rows, :] = zero
            dv_ref[rows, :] = zero
            return carry

        lax.fori_loop(0, s // mb, prep, 0)

        for (_, dil), l_ref in zip(DILATED_BRANCHES, (l1_ref, l2_ref, l3_ref)):
            def step(t, carry, dil=dil, l_ref=l_ref):
                cur, prev, has_prev = _dil_rows(t, dil)
                qs, kk, sc = _dil_scores(q_ref, k_ref, cur, prev, has_prev)
                vv = jnp.concatenate([v_ref[prev, :], v_ref[cur, :]], axis=0).astype(BF16)
                lg = l_ref[cur, :]
                w = jnp.exp(lg - lt_scr[cur, :])
                wd = w * dd_scr[cur, :]
                column = lambda x: jnp.concatenate([x[:, 0:1], x[:, HEAD_DIM:HEAD_DIM + 1]], axis=0)
                dos = _stack_heads(w * dm_ref[cur, :])
                p = jnp.exp(sc - column(lg))
                ds = (p * (lax.dot_general(dos, vv, NT, preferred_element_type=F32) - column(wd))).astype(BF16)
                dq_ref[cur, :] += _unstack_heads(jnp.dot(ds, kk, preferred_element_type=F32)) * SCALE
                dkk = lax.dot_general(ds, qs, TN, preferred_element_type=F32)
                dvv = lax.dot_general(p.astype(BF16), dos, TN, preferred_element_type=F32)
                dk_ref[cur, :] += dkk[DIL_L:]
                dv_ref[cur, :] += dvv[DIL_L:]
                dk_ref[prev, :] += dkk[:DIL_L]
                dv_ref[prev, :] += dvv[:DIL_L]
                return carry

            lax.fori_loop(0, nsub, step, 0, unroll=DIL_UNROLL_BWD)

    col = lambda arr_col: pl.BlockSpec((s, LANES), lambda h: (0, arr_col + h))
    shp = jax.ShapeDtypeStruct((s, D_MIX), F32)
    return pl.pallas_call(
        body, out_shape=(shp, shp, shp), grid=(N_MIX_HP,),
        in_specs=[col(0), col(N_MIX_HP), col(2 * N_MIX_HP), col(0), col(0), col(0), col(0), col(0)], out_specs=(col(0),) * 3,
        scratch_shapes=[pltpu.VMEM((s, LANES), F32)] * 2,
        compiler_params=_params("parallel"), name=name)(qk_r, qk_r, proj, mix, dheads, *lses)


CONV_BR = 512
FF_CHUNK = 2 * D_FF // N_DEV
FF_HALF = N_DEV // 2
HALO = 8


def _shift_down(x, halo, k):
    row = lax.broadcasted_iota(jnp.int32, x.shape, 0)
    y = pltpu.roll(x, k, 0)
    for r in range(k):
        y = jnp.where(row == r, halo[HALO - k + r:HALO - k + r + 1, :], y)
    return y


def _shift_up(x, halo, k):
    n = x.shape[0]
    row = lax.broadcasted_iota(jnp.int32, x.shape, 0)
    y = pltpu.roll(x, n - k, 0)
    for r in range(k):
        y = jnp.where(row == n - k + r, halo[r:r + 1, :], y)
    return y


def _conv_vals(u, halo, w, b):
    s1 = _shift_down(u, halo, 1)
    s2 = _shift_down(u, halo, 2)
    return b + w[0:1, :] * s2 + w[1:2, :] * s1 + w[2:3, :] * u, s1, s2


def _conv_in_specs(order, layer):
    rc = (lambda i, j: (i, j)) if order == "rc" else (lambda j, i: (i, j))
    per = CONV_BR // HALO
    main = lambda off: pl.BlockSpec((None, CONV_BR, FF_CHUNK), lambda *g: (off + rc(*g)[1], rc(*g)[0], 0))
    halo = lambda off: pl.BlockSpec((None, HALO, FF_CHUNK), lambda *g: (off + rc(*g)[1], jnp.maximum(rc(*g)[0] * per - 1, 0), 0))
    wspec = lambda off: pl.BlockSpec((None, None, 3, FF_CHUNK), lambda *g: (layer, off + rc(*g)[1], 0, 0))
    bspec = lambda off: pl.BlockSpec((None, 1, FF_CHUNK), lambda *g: (off + rc(*g)[1], 0, 0))
    return [main(0), halo(0), main(FF_HALF), halo(FF_HALF), wspec(0), wspec(FF_HALF), bspec(0), bspec(FF_HALF)]


def _conv_fwd(u, cw, cb, layer, *, name):
    s = u.shape[1]

    def body(uv_ref, hv_ref, ug_ref, hg_ref, wv_ref, wg_ref, bv_ref, bg_ref, o_ref):
        first = pl.program_id(0) == 0
        hv = jnp.where(first, 0.0, hv_ref[...])
        hg = jnp.where(first, 0.0, hg_ref[...])
        val, _, _ = _conv_vals(uv_ref[...], hv, wv_ref[...], bv_ref[...])
        gate, _, _ = _conv_vals(ug_ref[...], hg, wg_ref[...], bg_ref[...])
        o_ref[...] = (gate / (1.0 + jnp.exp(-gate)) * val).astype(BF16)

    return pl.pallas_call(body, out_shape=jax.ShapeDtypeStruct((FF_HALF, s, FF_CHUNK), BF16), grid=(s // CONV_BR, FF_HALF),
                          in_specs=_conv_in_specs("rc", layer), out_specs=pl.BlockSpec((None, CONV_BR, FF_CHUNK), lambda i, j: (j, i, 0)),
                          compiler_params=_params("parallel", "parallel"), name=name)(u, u, u, u, cw, cw, cb, cb)


def _swiglu_bwd(val, gate, da):
    sg = 1.0 / (1.0 + jnp.exp(-gate))
    return da * (gate * sg), da * val * (sg * (1.0 + gate * (1.0 - sg)))


def _conv_bwd(u, cw, cb, da, layer, *, name):
    s = u.shape[1]
    nrow = s // CONV_BR
    per = CONV_BR // HALO

    def body(uv_ref, hv_ref, ug_ref, hg_ref, wv_ref, wg_ref, bv_ref, bg_ref, da_ref, nv_ref, ng_ref, nda_ref, du_ref, dwb_ref):
        i = pl.program_id(1)
        first, last = i == 0, i == nrow - 1
        hv = jnp.where(first, 0.0, hv_ref[...])
        hg = jnp.where(first, 0.0, hg_ref[...])
        uv, ug = uv_ref[...], ug_ref[...]
        wv, wg, bv, bg = wv_ref[...], wg_ref[...], bv_ref[...], bg_ref[...]
        val, v1, v2 = _conv_vals(uv, hv, wv, bv)
        gate, g1, g2 = _conv_vals(ug, hg, wg, bg)
        dval, dgate = _swiglu_bwd(val, gate, da_ref[...])
        val_n, _, _ = _conv_vals(nv_ref[...], uv[CONV_BR - HALO:, :], wv, bv)
        gate_n, _, _ = _conv_vals(ng_ref[...], ug[CONV_BR - HALO:, :], wg, bg)
        dval_n, dgate_n = _swiglu_bwd(val_n, gate_n, nda_ref[...])
        dval_n = jnp.where(last, 0.0, dval_n)
        dgate_n = jnp.where(last, 0.0, dgate_n)
        back = lambda dc, dc_n, w: w[2:3, :] * dc + w[1:2, :] * _shift_up(dc, dc_n, 1) + w[0:1, :] * _shift_up(dc, dc_n, 2)
        du_ref[0] = back(dval, dval_n, wv).astype(BF16)
        du_ref[1] = back(dgate, dgate_n, wg).astype(BF16)

        @pl.when(first)
        def _():
            dwb_ref[...] = jnp.zeros_like(dwb_ref)

        cs = lambda t: jnp.sum(t, axis=0, keepdims=True)
        r8 = lax.broadcasted_iota(jnp.int32, (8, FF_CHUNK), 0)
        rows4 = lambda a, b, c, d: jnp.where(r8 == 0, a, jnp.where(r8 == 1, b, jnp.where(r8 == 2, c, jnp.where(r8 == 3, d, 0.0))))
        dwb_ref[0] += rows4(cs(dval * v2), cs(dval * v1), cs(dval * uv), cs(dval))
        dwb_ref[1] += rows4(cs(dgate * g2), cs(dgate * g1), cs(dgate * ug), cs(dgate))

    nxt = lambda off: pl.BlockSpec((None, HALO, FF_CHUNK), lambda j, i: (off + j, jnp.minimum((i + 1) * per, nrow * per - 1), 0))
    specs = _conv_in_specs("cr", layer) + [pl.BlockSpec((None, CONV_BR, FF_CHUNK), lambda j, i: (j, i, 0)), nxt(0), nxt(FF_HALF), nxt(0)]
    return pl.pallas_call(
        body, out_shape=(jax.ShapeDtypeStruct((2, FF_HALF, s, FF_CHUNK), BF16), jax.ShapeDtypeStruct((2, FF_HALF, 8, FF_CHUNK), F32)),
        grid=(FF_HALF, nrow), in_specs=specs,
        out_specs=(pl.BlockSpec((2, None, CONV_BR, FF_CHUNK), lambda j, i: (0, j, i, 0)),
                   pl.BlockSpec((2, None, 8, FF_CHUNK), lambda j, i: (0, j, 0, 0))),
        compiler_params=_params("parallel", "arbitrary"), name=name)(u, u, u, u, cw, cw, cb, cb, da, u, u, da)


def _rows_of(r):
    return lambda ref, idx: ref.at[:, pl.ds(idx * r, r), :]


def _slot1(ref, idx):
    return ref.at[:, idx]


def _slot0(ref, idx):
    return ref.at[idx]


def _all_gather(shards, full_shapes, places, *, name):
    n = len(shards)

    def body(*refs):
        ins, outs = refs[:n], refs[n:2 * n]
        send_sems, recv_sems, local_sems = refs[2 * n:]
        mx, my, mc = lax.axis_index("x"), lax.axis_index("y"), lax.axis_index("c")
        me, sibling = (mx, my, mc), (mx, my, 1 - mc)
        chips = [(1 - mx, my), (mx, 1 - my), (1 - mx, 1 - my)]

        def win(t, px, py, pc):
            return places[t](outs[t], 4 * px + 2 * py + pc)

        def copy(t, k, block, to, src=None):
            return pltpu.make_async_remote_copy(src_ref=win(t, *block) if src is None else src, dst_ref=win(t, *block),
                                                send_sem=send_sems.at[t, k], recv_sem=recv_sems.at[t, k], device_id=to, device_id_type=MESH)

        mine = [pltpu.make_async_copy(ins[t], win(t, *me), local_sems.at[t]) for t in range(n)]
        for cp in mine:
            cp.start()
        first = []
        for t in range(n):
            first += [copy(t, 0, me, sibling, src=ins[t])] + [copy(t, 1 + j, me, (*chip, mc), src=ins[t]) for j, chip in enumerate(chips)]
        for cp in first:
            cp.start()
        passed = []
        for j, chip in enumerate(chips):
            for t in range(n):
                copy(t, 1 + j, (*chip, mc), me).wait_recv()
                fwd = copy(t, 4 + j, (*chip, mc), sibling)
                fwd.start()
                passed.append(fwd)
        for t in range(n):
            copy(t, 0, sibling, me).wait_recv()
            for j, chip in enumerate(chips):
                copy(t, 4 + j, (*chip, 1 - mc), me).wait_recv()
        for cp in first + passed:
            cp.wait_send()
        for cp in mine:
            cp.wait()

    hbm = pl.BlockSpec(memory_space=pl.ANY)
    return pl.pallas_call(
        body, out_shape=tuple(jax.ShapeDtypeStruct(s, x.dtype) for s, x in zip(full_shapes, shards)),
        in_specs=[hbm] * n, out_specs=(hbm,) * n,
        scratch_shapes=[pltpu.SemaphoreType.DMA((n, 7)), pltpu.SemaphoreType.DMA((n, 7)), pltpu.SemaphoreType.DMA((n,))],
        name=name)(*shards)


FLIPS = [(fx, fy, fc) for fx in (0, 1) for fy in (0, 1) for fc in (0, 1)][1:]


def _exchange_copies(kind, places, src, land, send_sems, recv_sems, local_sems):
    mx, my, mc = lax.axis_index("x"), lax.axis_index("y"), lax.axis_index("c")
    me = 4 * mx + 2 * my + mc
    n = len(src)
    local, remote = [], []
    for t in range(n):
        if kind == "gather":
            local.append(pltpu.make_async_copy(src[t], places[t](land[t], me), local_sems.at[t]))
        else:
            local.append(pltpu.make_async_copy(places[t](src[t], me), land[t].at[me], local_sems.at[t]))
    for k, (fx, fy, fc) in enumerate(FLIPS):
        px, py, pc = mx ^ fx, my ^ fy, mc ^ fc
        peer = 4 * px + 2 * py + pc
        for t in range(n):
            sems = dict(send_sem=send_sems.at[7 * t + k], recv_sem=recv_sems.at[7 * t + k], device_id=(px, py, pc), device_id_type=MESH)
            if kind == "gather":
                pair = [(src[t], places[t](land[t], me)), (src[t], places[t](land[t], peer))]
            else:
                pair = [(places[t](src[t], peer), land[t].at[me]), (places[t](src[t], peer), land[t].at[peer])]
            remote.append([functools.partial(pltpu.make_async_remote_copy, src_ref=s_, dst_ref=d_, **sems) for s_, d_ in pair])
    return local, remote


HBM_SPEC = pl.BlockSpec(memory_space=pltpu.HBM)
SEM_SPEC = pl.BlockSpec(memory_space=pltpu.SEMAPHORE)
SIDE_EFFECT = pltpu.SideEffectType.DATAFLOW_SIDE_EFFECTING


def _exchange_start(kind, srcs, land_shapes, places, after, *, name):
    n = len(srcs)

    def body(*refs):
        src, land = refs[:n], refs[n:2 * n]
        send_sems, recv_sems, local_sems = refs[2 * n + 1:2 * n + 4]
        token = refs[-1]
        local, remote = _exchange_copies(kind, places, src, land, send_sems, recv_sems, local_sems)
        for cp in local:
            cp.start()
        for send, _ in remote:
            send().start()
        token[...] = jnp.zeros_like(token)

    hbm = lambda t: pltpu.with_memory_space_constraint(t, pltpu.HBM)
    lands = [hbm(lax.empty(tuple(s), x.dtype)) for s, x in zip(land_shapes, srcs)]
    out_shape = (pltpu.SemaphoreType.DMA((7 * n,)), pltpu.SemaphoreType.DMA((7 * n,)), pltpu.SemaphoreType.DMA((n,)),
                 *[pltpu.HBM(x.shape, x.dtype) for x in srcs], *[pltpu.HBM(tuple(s), x.dtype) for s, x in zip(land_shapes, srcs)],
                 jax.ShapeDtypeStruct((8, LANES), F32))
    outs = pl.pallas_call(
        body, name=name, out_shape=out_shape, in_specs=[HBM_SPEC] * (2 * n) + [pl.BlockSpec(memory_space=pl.ANY)],
        out_specs=(SEM_SPEC, SEM_SPEC, SEM_SPEC) + (HBM_SPEC,) * (2 * n) + (pl.BlockSpec(memory_space=pltpu.VMEM),),
        input_output_aliases={i: 3 + i for i in range(2 * n)},
        compiler_params=pltpu.CompilerParams(has_side_effects=SIDE_EFFECT))(*[hbm(x) for x in srcs], *lands, after)
    return dict(sems=outs[:3], src=outs[3:3 + n], land=outs[3 + n:3 + 2 * n], token=outs[-1])


def _exchange_wait(kind, started, places, after, *, name):
    n = len(started["src"])

    def body(*refs):
        src, land = refs[:n], refs[n:2 * n]
        send_sems, recv_sems, local_sems = refs[2 * n:2 * n + 3]
        local, remote = _exchange_copies(kind, places, src, land, send_sems, recv_sems, local_sems)
        for cp in local:
            cp.wait()
        for send, arrival in remote:
            send().wait_send()
            arrival().wait_recv()

    out_shape = tuple(pltpu.HBM(x.shape, x.dtype) for x in started["src"]) + tuple(pltpu.HBM(x.shape, x.dtype) for x in started["land"])
    outs = pl.pallas_call(
        body, name=name, out_shape=out_shape,
        in_specs=[HBM_SPEC] * (2 * n) + [SEM_SPEC] * 3 + [pl.BlockSpec(memory_space=pl.ANY)], out_specs=(HBM_SPEC,) * (2 * n),
        input_output_aliases={i: i for i in range(2 * n)},
        compiler_params=pltpu.CompilerParams(has_side_effects=SIDE_EFFECT))(*started["src"], *started["land"], *started["sems"], after)
    return list(outs[n:])


def _adamw(parts, w, m, v, *, br, name):
    layers, r, wd = w.shape
    assert len(parts) == layers

    def body(*refs):
        p_refs = refs[:layers]
        w_ref, m_ref, v_ref, g_ref, d_ref, nm_ref, nv_ref = refs[layers:]
        for k in range(layers):
            @pl.when(pl.program_id(0) == k)
            def _(p_ref=p_refs[k]):
                g = p_ref[0].astype(F32)
                for dev in range(1, N_DEV):
                    g = g + p_ref[dev].astype(F32)
                mm = ADAM_B1 * m_ref[...] + (1.0 - ADAM_B1) * g
                vv = ADAM_B2 * v_ref[...] + (1.0 - ADAM_B2) * (g * g)
                m_hat = mm / (1.0 - ADAM_B1 ** ADAM_STEP)
                v_hat = vv / (1.0 - ADAM_B2 ** ADAM_STEP)
                g_ref[...] = g
                d_ref[...] = -ADAM_LR * (m_hat / (jnp.sqrt(v_hat) + ADAM_EPS) + ADAM_WD * w_ref[...])
                nm_ref[...] = mm
                nv_ref[...] = vv

    p_spec = lambda k: pl.BlockSpec((N_DEV, None, br, wd), lambda l, i: (0, 0, jnp.where(l == k, i, 0), 0))
    blk = pl.BlockSpec((None, br, wd), lambda l, i: (l, i, 0))
    shp = jax.ShapeDtypeStruct((layers, r, wd), F32)
    return pl.pallas_call(body, out_shape=(shp, shp, shp, shp), grid=(layers, r // br),
                          in_specs=[p_spec(k) for k in range(layers)] + [blk, blk, blk], out_specs=(blk, blk, blk, blk),
                          compiler_params=_params("arbitrary", "arbitrary"), name=name)(*parts, w, m, v)


SMALL = ("norm_mix", "norm_mem", "norm_ffn", "b_forget", "conv_b", "norm_final")


def _pack(tensors):
    flat = jnp.concatenate([t.reshape(-1) for t in tensors])
    rows = -(-flat.shape[0] // (PACK_W * PACK_ROW_ALIGN)) * PACK_ROW_ALIGN
    flat = jnp.pad(flat, (0, rows * PACK_W - flat.shape[0]))
    return flat.reshape(1, rows, PACK_W)


def _unpack(buf, shapes):
    flat = buf.reshape(-1)
    out, off = [], 0
    for shp in shapes:
        n = math.prod(shp)
        out.append(flat[off:off + n].reshape(tuple(shp)))
        off += n
    return out


def _fox_permute(w):
    pad = jnp.zeros(w.shape[:-1] + (FOX_P - FOX_IN,), w.dtype)
    return jnp.concatenate([w[..., :3 * D_MIX], w[..., 3 * D_MIX + N_MIX_HEADS:], w[..., 3 * D_MIX:3 * D_MIX + N_MIX_HEADS], pad], axis=-1)


def _fox_unpermute(w):
    return jnp.concatenate([w[..., :3 * D_MIX], w[..., DIL_IN:DIL_IN + N_MIX_HEADS], w[..., 3 * D_MIX:DIL_IN]], axis=-1)


def _bias_layout(c):
    s = c.shape[0]
    ct = c[:, :N_MIX_HEADS].T.reshape(N_MIX_HP, 2, s)
    return jnp.pad(ct, ((0, 0), (0, 6), (0, 0)))


def _bias_grad(dck6):
    s = dck6.shape[2]
    dk = dck6[:, :2, :].reshape(N_MIX_HEADS, s).T
    return jnp.pad(dk, ((0, 0), (0, LANES - N_MIX_HEADS)))


def _device_step(x, mem, target, small, get_weights, put_grads):
    s = x.shape[0]
    mt = mem.shape[0]
    bq = 512
    cos_t, sin_t = _rope_tables(s)
    row = lambda t, l: t[l][None, :]
    saved = []
    h = x
    cb8 = small["conv_b"].reshape(DEPTH, N_DEV, 1, FF_CHUNK)
    for l in range(DEPTH):
        kind, slot = l % 2, l // 2
        wl = dict(get_weights(l, "attn", h))
        if l == 0:
            xn = _rmsnorm_fwd(h, row(small["norm_mix"], l), br=512, name=f"norm_mix_fwd{l}")
        mn = _rmsnorm_fwd(mem, row(small["norm_mem"], l), br=mt, name=f"norm_mem_fwd{l}")
        if kind == 0:
            proj = _mm(xn, wl["w_in"], "nn", tm=1024, tn=384, layer=0, name=f"in_proj{l}")
        else:
            proj, qk_r = _mm(xn, wl["w_in"], "nt", tm=1024, tn=512, layer=0, rope=(cos_t, sin_t, 2 * D_MIX), name=f"in_proj{l}")
        kvm = _mm(mn, wl["w_mem_kv"], "nn", tm=mt, tn=512, layer=0, name=f"mem_kv{l}")
        st = dict(h=h, xn=xn, mn=mn, proj=proj, kvm=kvm, w=wl)
        if kind == 0:
            b_pad = jnp.pad(small["b_forget"][slot], (0, LANES - N_MIX_HEADS))[None, :]
            c = _forget_cumsum(proj, b_pad, name=f"forget_cumsum{l}")
            ck6 = _bias_layout(c)
            mix, lse, heads = _attn_fwd(proj, proj, ck6, q_col=0, k_col=N_MIX_HP, v_col=2 * N_MIX_HP, n_hp=N_MIX_HP,
                                        causal=True, bq=min(s, 1024), bk=min(s, 1024), name=f"fox_fwd{l}")
            st.update(b_pad=b_pad, ck6=ck6, mix=mix, lse=lse)
        else:
            mix, lses, heads = _dil_fwd(qk_r, proj, name=f"dil_fwd{l}")
            st.update(qk_r=qk_r, lses=lses, mix=mix)
        mo, lse_m, heads = _attn_fwd(proj, kvm, None, q_col=QM_COL, k_col=0, v_col=N_MEM_HP, n_hp=N_MEM_HP, causal=False,
                                     bq=min(s, 2048), bk=mt, heads=heads, heads_col=N_MIX_HP, name=f"mem_fwd{l}")
        h1, xf = _mm(heads, wl["w_out"], "nn", tm=1024, tn=D_MODEL, res=h, layer=0, norm_gain=row(small["norm_ffn"], l),
                     name=f"out_proj{l}")
        wl.update(get_weights(l, "ffn", xf))
        u = _mm(xf, wl["w_up"], "nt", tm=1024, tn=FF_CHUNK, layer=0, chunk="b", name=f"up_proj{l}")
        a = _conv_fwd(u, wl["conv_w"], cb8[l], 0, name=f"conv_fwd{l}")
        st.update(mo=mo, lse_m=lse_m, heads=heads, h1=h1, xf=xf, u=u, a=a)
        saved.append(st)
        if l + 1 < DEPTH:
            h, xn = _mm(a, wl["w_down"], "nn", tm=512, tn=D_MODEL, res=h1, layer=0, chunk="reduce",
                        norm_gain=row(small["norm_mix"], l + 1), name=f"down_proj{l}")
        else:
            h = _mm(a, wl["w_down"], "nn", tm=1024, tn=512, res=h1, layer=0, chunk="reduce", name=f"down_proj{l}")

    dh, dhb, dg_final, loss = _loss_head(h, target, small["norm_final"][None, :], br=512, name="loss_head")
    gs = {k: [None] * DEPTH for k in ("norm_mix", "norm_mem", "norm_ffn", "conv_b")}
    gs["b_forget"] = [None] * 2
    dep = 0.0
    for l in reversed(range(DEPTH)):
        st = saved[l]
        wl = st["w"]
        gw = {}
        kind, slot = l % 2, l // 2
        da = _mm(dhb, wl["w_down"], "nt", tm=1024, tn=FF_CHUNK, layer=0, chunk="b", name=f"down_dx{l}")
        gw["w_down"] = _mm(st["a"], dhb, "tn", tm=FF_CHUNK, tn=512, out_dtype=BF16, chunk="a", name=f"down_dw{l}")
        du, dwb = _conv_bwd(st["u"], wl["conv_w"], cb8[l] + dep, da, 0, name=f"conv_bwd{l}")
        du = du.reshape(N_DEV, s, FF_CHUNK)
        dwb = dwb.reshape(N_DEV, 8, FF_CHUNK)
        gw["conv_w"] = dwb
        gs["conv_b"][l] = dwb[:, 3, :].reshape(-1)
        dh1, dh1b, dgf = _mm(du, wl["w_up"], "nn", tm=256, tn=D_MODEL, layer=0, chunk="reduce",
                             norm_bwd=(st["h1"], row(small["norm_ffn"], l), dh), name=f"up_dx{l}")
        gw["w_up"] = _mm(du, st["xf"], "tn", tm=FF_CHUNK, tn=512, out_dtype=BF16, chunk="a", name=f"up_dw{l}")
        gs["norm_ffn"][l] = dgf[0]
        dheads = _mm(dh1b, wl["w_out"], "nt", tm=1024, tn=512, layer=0, name=f"out_dx{l}")
        gw["w_out"] = _mm(st["heads"], dh1b, "tn", tm=512, tn=512, out_dtype=BF16, name=f"out_dw{l}")
        dqm, dkm, dvm = _attn_bwd(st["proj"], st["kvm"], st["mo"], dheads, st["lse_m"], None, q_col=QM_COL, k_col=0,
                                  v_col=N_MEM_HP, o_col=N_MIX_HP, n_hp=N_MEM_HP, causal=False, bq=min(s, 1024), bk=mt,
                                  name=f"mem_bwd{l}")
        dkvm = jnp.concatenate([dkm, dvm], axis=1)
        gw["w_mem_kv"] = _mm(st["mn"], dkvm, "tn", tm=512, tn=512, out_dtype=BF16, name=f"mem_kv_dw{l}")
        dmn = _mm(dkvm, wl["w_mem_kv"], "nt", tm=mt, tn=512, layer=0, name=f"mem_kv_dx{l}")
        dep_early = put_grads(l, "ffn", gw)
        _, _, dgm = _rmsnorm_bwd(mem, dmn, row(small["norm_mem"], l), None, br=mt, name=f"norm_mem_bwd{l}")
        gs["norm_mem"][l] = dgm[0]
        if kind == 0:
            dq, dk, dv, dcq6, dck6 = _attn_bwd(st["proj"], st["proj"], st["mix"], dheads, st["lse"], st["ck6"] + dep_early, q_col=0,
                                               k_col=N_MIX_HP, v_col=2 * N_MIX_HP, o_col=0, n_hp=N_MIX_HP, causal=True,
                                               bq=bq, bk=bq, name=f"fox_bwd{l}")
            dz, db = _forget_cumsum_bwd(st["proj"], st["b_pad"], _bias_grad(dcq6), _bias_grad(dck6), name=f"forget_cumsum_bwd{l}")
            gs["b_forget"][slot] = db[0, :N_MIX_HEADS]
            dproj = jnp.concatenate([dq.astype(BF16), dk, dv, dqm.astype(BF16), dz], axis=1)
        else:
            dq_r, dk_r, dv = _dil_bwd(st["qk_r"], st["proj"], st["mix"], dheads, st["lses"], name=f"dil_bwd{l}")
            dq = _rope(dq_r, cos_t + dep_early, -sin_t, n_cols=N_MIX_HP, out_dtype=BF16, br=512, name=f"rope_bwd_q{l}")
            dk = _rope(dk_r, cos_t, -sin_t, n_cols=N_MIX_HP, out_dtype=BF16, br=512, name=f"rope_bwd_k{l}")
            dproj = jnp.concatenate([dq, dk, dv.astype(BF16), dqm.astype(BF16)], axis=1)
        if kind == 0:
            gw["w_in"] = _mm(st["xn"], dproj, "tn", tm=512, tn=384, out_dtype=BF16, name=f"in_dw{l}")
        else:
            gw["w_in"] = _mm(dproj, st["xn"], "tn", tm=512, tn=512, out_dtype=BF16, name=f"in_dw{l}")
        dh, dhb, dgx = _mm(dproj, wl["w_in"], "nt" if kind == 0 else "nn", tm=512, tn=D_MODEL, layer=0,
                           norm_bwd=(st["h"], row(small["norm_mix"], l), dh1), name=f"in_dx{l}")
        gs["norm_mix"][l] = dgx[0]
        dep = put_grads(l, "attn", gw)

    grads_s = {k: jnp.stack(v) for k, v in gs.items()}
    grads_s["norm_final"] = dg_final[0]
    return loss[0, 0], dh, grads_s


def kernel(x, mem, norm_mix, norm_mem, norm_ffn, w_in_fox, b_forget, w_in_dil, w_mem_kv, w_out, w_up, conv_w, conv_b, w_down, norm_final, loss_target, m_norm_mix, m_norm_mem, m_norm_ffn, m_w_in_fox, m_b_forget, m_w_in_dil, m_w_mem_kv, m_w_out, m_w_up, m_conv_w, m_conv_b, m_w_down, m_norm_final, v_norm_mix, v_norm_mem, v_norm_ffn, v_w_in_fox, v_b_forget, v_w_in_dil, v_w_mem_kv, v_w_out, v_w_up, v_conv_w, v_conv_b, v_w_down, v_norm_final):
    names = ["norm_mix", "norm_mem", "norm_ffn", "w_in_fox", "b_forget", "w_in_dil", "w_mem_kv", "w_out", "w_up", "conv_w", "conv_b",
             "w_down", "norm_final"]
    w = dict(zip(names, (norm_mix, norm_mem, norm_ffn, w_in_fox, b_forget, w_in_dil, w_mem_kv, w_out, w_up, conv_w, conv_b, w_down, norm_final)))
    m = dict(zip(names, (m_norm_mix, m_norm_mem, m_norm_ffn, m_w_in_fox, m_b_forget, m_w_in_dil, m_w_mem_kv, m_w_out, m_w_up, m_conv_w,
                         m_conv_b, m_w_down, m_norm_final)))
    v = dict(zip(names, (v_norm_mix, v_norm_mem, v_norm_ffn, v_w_in_fox, v_b_forget, v_w_in_dil, v_w_mem_kv, v_w_out, v_w_up, v_conv_w,
                         v_conv_b, v_w_down, v_norm_final)))
    big = ("w_in_fox", "w_in_dil", "w_mem_kv", "w_out", "w_up", "w_down", "conv_w")
    small_shapes = [w[k].shape for k in SMALL]
    dil_c = w_in_dil.shape[2]
    rows = {k: w[k].shape[1] for k in ("w_in_fox", "w_mem_kv", "w_out", "w_down")}

    def places(l):
        w_in_place = _rows_of(rows["w_in_fox"]) if l % 2 == 0 else _slot1
        return [w_in_place, _rows_of(rows["w_mem_kv"]), _rows_of(rows["w_out"]), _slot1, _rows_of(rows["w_down"]), _slot1]

    def full_shapes(l):
        w_in_shape = (1, D_MODEL, FOX_P) if l % 2 == 0 else (1, N_DEV, dil_c, D_MODEL)
        return [w_in_shape, (1, D_MODEL, 2 * D_MEMQ), (1, D_MODEL, D_MODEL), (1, N_DEV, FF_CHUNK, D_MODEL), (1, D_FF, D_MODEL),
                (1, N_DEV, 3, FF_CHUNK)]

    transposed = lambda t: jnp.swapaxes(t, 1, 2)
    to_wire = {"w_in_fox": lambda t: _fox_permute(t).astype(BF16), "w_in_dil": lambda t: t.T.astype(BF16),
               "w_mem_kv": lambda t: t.astype(BF16), "w_out": lambda t: t.astype(BF16), "w_up": lambda t: t.T.astype(BF16),
               "w_down": lambda t: t.astype(BF16), "conv_w": lambda t: t}
    shard = lambda k, i: to_wire[k](w[k][i])
    shard_shape = lambda k: jax.eval_shape(lambda: shard(k, 0)).shape
    everything = (0, 1, 2, 3, 4, 5)
    gather_groups = {l: ((0, 1, 2), (3, 4, 5)) if l == 0 else (everything,) for l in range(DEPTH)}
    scatter_groups = {l: ((1, 2, 3, 4, 5), (0,)) if l == 0 else (everything,) for l in range(DEPTH)}
    pick = lambda seq, group: [seq[i] for i in group]
    tag = lambda l, group: f"{l}" + ("" if group == everything else "_" + "".join(str(i) for i in group))

    gathers, after = {}, norm_final
    for l in range(DEPTH):
        w_in_shard = shard("w_in_fox" if l % 2 == 0 else "w_in_dil", l // 2)[None]
        shards = [w_in_shard] + [shard(k, l)[None] for k in ("w_mem_kv", "w_out", "w_up", "w_down", "conv_w")]
        for group in gather_groups[l]:
            gathers[l, group] = _exchange_start("gather", pick(shards, group), pick(full_shapes(l), group), pick(places(l), group), after,
                                                name=f"weights_gather_start{tag(l, group)}")
            after = gathers[l, group]["token"]
    started = sum(g["token"][0, 0] for g in gathers.values())
    small = {k: w[k] for k in SMALL}
    small["norm_mix"] = norm_mix + started
    landed = {}

    def get_weights(l, part, h):
        group = [g for g in gather_groups[l] if (0 if part == "attn" else 3) in g][0]
        if (l, group) not in landed:
            lands = _exchange_wait("gather", gathers[l, group], pick(places(l), group), h, name=f"weights_gather_wait{tag(l, group)}")
            landed[l, group] = dict(zip(group, lands))
        got = landed[l, group]
        if part == "ffn":
            return dict(w_up=got[3], w_down=got[4].reshape(1, FF_HALF, FF_CHUNK, D_MODEL), conv_w=got[5])
        w_in = got[0] if l % 2 == 0 else got[0].reshape(1, N_DEV * dil_c, D_MODEL)
        return dict(w_in=w_in, w_mem_kv=got[1], w_out=got[2])

    scatters, pending = {}, {}

    def put_grads(l, part, g):
        pending.setdefault(l, {}).update(g)
        if part == "ffn" and len(scatter_groups[l]) == 1:
            return 0.0
        group = scatter_groups[l][0 if part == "ffn" else -1]
        have = pending[l]
        srcs = {3: lambda: have["w_up"][None], 4: lambda: have["w_down"].reshape(1, D_FF, D_MODEL), 5: lambda: have["conv_w"][None],
                1: lambda: have["w_mem_kv"][None], 2: lambda: have["w_out"][None]}
        if l % 2 == 0:
            srcs[0] = lambda: have["w_in"][None]
        else:
            srcs[0] = lambda: have["w_in"].reshape(1, N_DEV, dil_c, D_MODEL)
        shard_shapes = [shard_shape("w_in_fox" if l % 2 == 0 else "w_in_dil")] + \
            [shard_shape(k) for k in ("w_mem_kv", "w_out", "w_up", "w_down")] + [(8, FF_CHUNK)]
        sources = [srcs[i]() for i in group]
        scatters[l, group] = _exchange_start("scatter", sources, [(N_DEV, 1) + tuple(s) for s in pick(shard_shapes, group)],
                                             pick(places(l), group), sources[0], name=f"grads_scatter_start{tag(l, group)}")
        return scatters[l, group]["token"][0, 0]

    loss, grad_x, gs = _device_step(x[0], mem[0], loss_target[0], small, get_weights, put_grads)

    recv = {}

    def wait_scatter(l, group, after_):
        lands = _exchange_wait("scatter", scatters[l, group], pick(places(l), group), after_, name=f"grads_scatter_wait{tag(l, group)}")
        recv.setdefault(l, {}).update(zip(group, lands))

    for l in reversed(range(1, DEPTH)):
        wait_scatter(l, everything, grad_x)
    wait_scatter(0, scatter_groups[0][0], grad_x)
    s_pack = _pack([gs[k] for k in SMALL])
    (s_recv,) = _all_gather([s_pack], [(N_DEV,) + s_pack.shape], [_slot0], name="small_grads_all_gather")

    layer_tensors = ("w_in", "w_mem_kv", "w_out", "w_up", "w_down", "conv_w")
    layer_parts = lambda k: [recv[l][layer_tensors.index(k)] for l in range(DEPTH)]
    to_local = {k: (lambda t: t) for k in big}
    to_local["w_in_fox"] = _fox_permute
    to_local["w_in_dil"] = to_local["w_up"] = transposed
    from_local = {k: (lambda t: t) for k in big}
    from_local["w_in_fox"] = _fox_unpermute
    from_local["w_in_dil"] = from_local["w_up"] = transposed
    blocks = {"w_in_fox": rows["w_in_fox"], "w_in_dil": dil_c, "w_mem_kv": rows["w_mem_kv"], "w_out": rows["w_out"], "w_up": FF_CHUNK // 4,
              "w_down": rows["w_down"] // 2, "conv_w": 3}
    outs = {}

    def update(k, parts):
        f = to_local[k]
        outs[k] = [from_local[k](t) for t in _adamw(parts, f(w[k]), f(m[k]), f(v[k]), br=blocks[k], name=f"adamw_{k}")]

    update("w_in_dil", [recv[l][0] for l in range(1, DEPTH, 2)])
    update("w_up", layer_parts("w_up"))
    update("w_down", layer_parts("w_down"))
    update("conv_w", [p[:, :, :3, :] for p in layer_parts("conv_w")])
    update("w_mem_kv", layer_parts("w_mem_kv"))
    update("w_out", layer_parts("w_out"))
    wait_scatter(0, scatter_groups[0][-1], outs["w_out"][1])
    update("w_in_fox", [recv[l][0] for l in range(0, DEPTH, 2)])
    small_outs = _adamw([s_recv], _pack([w[k] for k in SMALL]), _pack([m[k] for k in SMALL]), _pack([v[k] for k in SMALL]),
                        br=s_pack.shape[1], name="adamw_small")
    res = []
    for i, os_ in enumerate(small_outs):
        d = {k: outs[k][i] for k in big}
        d.update(zip(SMALL, _unpack(os_, small_shapes)))
        res.append([d[k] for k in names])
    loss = lax.psum(loss, ("x", "y", "c"))
    return (loss, grad_x[None], *res[0], *res[1], *res[2], *res[3])
```

```python
import functools
import math

import jax
import jax.numpy as jnp
from jax import lax
from jax.experimental import pallas as pl
from jax.experimental.pallas import tpu as pltpu

F32 = jnp.float32
BF16 = jnp.bfloat16

D_MODEL = 1024
HEAD_DIM = 64
N_MIX_HEADS = 12
N_MEM_HEADS = 4
D_MIX = N_MIX_HEADS * HEAD_DIM
D_MEMQ = N_MEM_HEADS * HEAD_DIM
D_FF = 2816
DEPTH = 4
FOX_IN = 3 * D_MIX + N_MIX_HEADS + D_MEMQ
DIL_IN = 3 * D_MIX + D_MEMQ
LANES = 128
FOX_P = DIL_IN + LANES
N_MIX_HP = D_MIX // LANES
N_MEM_HP = D_MEMQ // LANES
QM_COL = 3 * N_MIX_HP
F_COL = DIL_IN // LANES
DILATED_BRANCHES = ((128, 1), (512, 4), (2048, 16))
DIL_L = 128
DIL_UNROLL_FWD = 32
DIL_UNROLL_BWD = 8
ROPE_THETA = 10000.0
NORM_EPS = 1e-6
NEG = -1e30
SCALE = HEAD_DIM ** -0.5
N_DEV = 8

ADAM_LR = 0.001
ADAM_B1 = 0.9
ADAM_B2 = 0.999
ADAM_EPS = 1e-08
ADAM_WD = 0.01
ADAM_STEP = 10

VMEM_LIMIT = 56 * 1024 * 1024
PACK_W = 1024
PACK_ROW_ALIGN = 8

MESH = pl.DeviceIdType.MESH
NT = (((1,), (1,)), ((), ()))
NN = (((1,), (0,)), ((), ()))
TN = (((0,), (0,)), ((), ()))


def _params(*sem):
    return pltpu.CompilerParams(dimension_semantics=sem, vmem_limit_bytes=VMEM_LIMIT)


def _lane_lo(shape):
    return lax.broadcasted_iota(jnp.int32, shape, len(shape) - 1) < HEAD_DIM


def _pair(lo, a, b):
    return jnp.where(lo, a, b)


def _mm(a, b, mode, *, tm, tn, name, out_dtype=F32, res=None, layer=None, chunk=None, norm_gain=None, norm_bwd=None, rope=None):
    lead = () if layer is None else (layer,)
    nl = (None,) * len(lead)
    bs = b.shape[len(lead):]
    dims = {"nn": NN, "nt": NT, "tn": TN}[mode]
    reduce_n = 0
    if chunk is None:
        (m, k) = a.shape[::-1] if mode == "tn" else a.shape
        n = bs[0] if mode == "nt" else bs[1]
        grid = (m // tm, n // tn)
        a_spec = pl.BlockSpec((k, tm), lambda i, j: (0, i)) if mode == "tn" else pl.BlockSpec((tm, k), lambda i, j: (i, 0))
        b_spec = pl.BlockSpec(nl + ((tn, k) if mode == "nt" else (k, tn)), lambda i, j: lead + ((j, 0) if mode == "nt" else (0, j)))
        o_spec = pl.BlockSpec((tm, tn), lambda i, j: (i, j))
        out_shape = (m, n)
    elif chunk == "b":
        (m, k) = a.shape[::-1] if mode == "tn" else a.shape
        c, nc = bs[0], (bs[1] if mode == "nt" else bs[2])
        grid = (m // tm, c)
        a_spec = pl.BlockSpec((k, tm), lambda i, j: (0, i)) if mode == "tn" else pl.BlockSpec((tm, k), lambda i, j: (i, 0))
        b_spec = pl.BlockSpec(nl + (None,) + tuple(bs[1:]), lambda i, j: lead + (j, 0, 0))
        o_spec = pl.BlockSpec((None, tm, nc), lambda i, j: (j, i, 0))
        out_shape = (c, m, nc)
    elif chunk == "a":
        assert mode == "tn"
        c, k, mc = a.shape
        n = bs[1]
        grid = (c, n // tn)
        a_spec = pl.BlockSpec((None, k, mc), lambda i, j: (i, 0, 0))
        b_spec = pl.BlockSpec(nl + (k, tn), lambda i, j: lead + (0, j))
        o_spec = pl.BlockSpec((None, mc, tn), lambda i, j: (i, 0, j))
        out_shape = (c, mc, n)
    else:
        reduce_n, m, kc = a.shape
        n = bs[1] if mode == "nt" else bs[2]
        grid = (m // tm, n // tn)
        a_spec = pl.BlockSpec((reduce_n, tm, kc), lambda i, j: (0, i, 0))
        b_spec = pl.BlockSpec(nl + ((reduce_n, tn, kc) if mode == "nt" else (reduce_n, kc, tn)),
                              lambda i, j: lead + ((0, j, 0) if mode == "nt" else (0, 0, j)))
        o_spec = pl.BlockSpec((tm, tn), lambda i, j: (i, j))
        out_shape = (m, n)

    if norm_gain is not None or norm_bwd is not None:
        assert chunk in (None, "reduce") and tn == n, "the RMSNorm of the result needs whole rows in a block"

    def body(*refs):
        a_ref, b_ref = refs[0], refs[1]
        dot = lambda x, y: lax.dot_general(x.astype(BF16), y.astype(BF16), dims, preferred_element_type=F32)
        if reduce_n:
            acc = dot(a_ref[0], b_ref[0])
            for r in range(1, reduce_n):
                acc = acc + dot(a_ref[r], b_ref[r])
        else:
            acc = dot(a_ref[...], b_ref[...])
        if norm_bwd is not None:
            x_ref, g_ref, r_ref = refs[2:5]
            dx_ref, dxb_ref, dg_ref = refs[-3:]
            dx, dg = _rms_bwd_math(x_ref[...], acc, g_ref[...])
            dx = dx + r_ref[...]
            dx_ref[...] = dx
            dxb_ref[...] = dx.astype(BF16)

            @pl.when(pl.program_id(0) == 0)
            def _():
                dg_ref[...] = jnp.zeros_like(dg_ref)

            dg_ref[0:1, :] += dg
            return
        o_ref = refs[-2] if (norm_gain is not None or rope is not None) else refs[-1]
        if res is not None:
            acc = acc + refs[2][...]
        o_ref[...] = acc.astype(o_ref.dtype)
        if norm_gain is not None:
            rs = lax.rsqrt(jnp.mean(acc * acc, axis=-1, keepdims=True) + NORM_EPS)
            refs[-1][...] = (acc * rs * refs[3][...]).astype(BF16)
        if rope is not None:
            @pl.when(pl.program_id(1) < rope[2] // tn)
            def _():
                cos, sin = refs[2][...], refs[3][...]
                for c in range(tn // LANES):
                    lanes = slice(c * LANES, (c + 1) * LANES)
                    refs[-1][:, lanes] = _rope_tile(acc[:, lanes], cos, sin)

    ins = [a, b] + ([res] if res is not None else [])
    specs = [a_spec, b_spec] + ([o_spec] if res is not None else [])
    out_shapes, out_specs = jax.ShapeDtypeStruct(out_shape, out_dtype), o_spec
    sem = ("parallel", "parallel")
    if rope is not None:
        assert chunk is None and res is None and norm_gain is None and norm_bwd is None and rope[2] % tn == 0
        tab = pl.BlockSpec((tm, LANES), lambda i, j: (i, 0))
        ins += [rope[0], rope[1]]
        specs += [tab, tab]
        r_spec = pl.BlockSpec((tm, tn), lambda i, j: (i, jnp.minimum(j, rope[2] // tn - 1)))
        out_shapes, out_specs = (out_shapes, jax.ShapeDtypeStruct((m, rope[2]), F32)), (o_spec, r_spec)
        sem = ("parallel", "arbitrary")
    if norm_gain is not None:
        assert res is not None
        ins.append(norm_gain)
        specs.append(pl.BlockSpec((1, n), lambda i, j: (0, 0)))
        out_shapes, out_specs = (out_shapes, jax.ShapeDtypeStruct(out_shape, BF16)), (o_spec, o_spec)
    if norm_bwd is not None:
        assert res is None and norm_gain is None
        x_in, gain, resid = norm_bwd
        ins += [x_in, gain, resid]
        specs += [o_spec, pl.BlockSpec((1, n), lambda i, j: (0, 0)), o_spec]
        out_shapes = (jax.ShapeDtypeStruct(out_shape, F32), jax.ShapeDtypeStruct(out_shape, BF16), jax.ShapeDtypeStruct((8, n), F32))
        out_specs = (o_spec, o_spec, pl.BlockSpec((8, n), lambda i, j: (0, 0)))
        sem = ("arbitrary", "arbitrary")
    return pl.pallas_call(body, out_shape=out_shapes, grid=grid, in_specs=specs, out_specs=out_specs,
                          compiler_params=_params(*sem), name=name)(*ins)


def _rmsnorm_fwd(x, g, *, br, name):
    r, d = x.shape

    def body(x_ref, g_ref, o_ref):
        xf = x_ref[...]
        rs = lax.rsqrt(jnp.mean(xf * xf, axis=-1, keepdims=True) + NORM_EPS)
        o_ref[...] = (xf * rs * g_ref[...]).astype(BF16)

    return pl.pallas_call(body, out_shape=jax.ShapeDtypeStruct((r, d), BF16), grid=(r // br,),
                          in_specs=[pl.BlockSpec((br, d), lambda i: (i, 0)), pl.BlockSpec((1, d), lambda i: (0, 0))],
                          out_specs=pl.BlockSpec((br, d), lambda i: (i, 0)), compiler_params=_params("parallel"), name=name)(x, g)


def _rms_bwd_math(x, dy, g):
    d = x.shape[-1]
    rs = lax.rsqrt(jnp.mean(x * x, axis=-1, keepdims=True) + NORM_EPS)
    gy = dy * g
    proj = jnp.sum(x * gy, axis=-1, keepdims=True) * (1.0 / d)
    dx = rs * gy - x * (rs * rs * rs) * proj
    dg = jnp.sum(dy * (x * rs), axis=0, keepdims=True)
    return dx, dg


def _rmsnorm_bwd(x, dy, g, res, *, br, name):
    r, d = x.shape
    has_res = res is not None

    def body(*refs):
        x_ref, dy_ref, g_ref = refs[:3]
        dx_ref, dxb_ref, dg_ref = refs[-3:]
        dx, dg = _rms_bwd_math(x_ref[...], dy_ref[...], g_ref[...])
        if has_res:
            dx = dx + refs[3][...]
        dx_ref[...] = dx
        dxb_ref[...] = dx.astype(BF16)

        @pl.when(pl.program_id(0) == 0)
        def _():
            dg_ref[...] = jnp.zeros_like(dg_ref)

        dg_ref[0:1, :] += dg

    row = pl.BlockSpec((br, d), lambda i: (i, 0))
    ins = [x, dy, g] + ([res] if has_res else [])
    specs = [row, row, pl.BlockSpec((1, d), lambda i: (0, 0))] + ([row] if has_res else [])
    return pl.pallas_call(
        body, out_shape=(jax.ShapeDtypeStruct((r, d), F32), jax.ShapeDtypeStruct((r, d), BF16), jax.ShapeDtypeStruct((8, d), F32)),
        grid=(r // br,), in_specs=specs, out_specs=(row, row, pl.BlockSpec((8, d), lambda i: (0, 0))),
        compiler_params=_params("arbitrary"), name=name)(*ins)


def _loss_head(h, target, g, *, br, name):
    r, d = h.shape

    def body(x_ref, t_ref, g_ref, dx_ref, dxb_ref, dg_ref, loss_ref):
        x = x_ref[...]
        gg = g_ref[...]
        rs = lax.rsqrt(jnp.mean(x * x, axis=-1, keepdims=True) + NORM_EPS)
        err = x * rs * gg - t_ref[...]
        part = jnp.sum(jnp.sum(err * err, axis=1, keepdims=True), axis=0, keepdims=True) * (0.5 / d)
        dx, dg = _rms_bwd_math(x, err * (1.0 / d), gg)
        dx_ref[...] = dx
        dxb_ref[...] = dx.astype(BF16)

        @pl.when(pl.program_id(0) == 0)
        def _():
            dg_ref[...] = jnp.zeros_like(dg_ref)
            loss_ref[...] = jnp.zeros_like(loss_ref)

        dg_ref[0:1, :] += dg
        loss_ref[...] += jnp.broadcast_to(part, loss_ref.shape)

    row = pl.BlockSpec((br, d), lambda i: (i, 0))
    return pl.pallas_call(
        body, out_shape=(jax.ShapeDtypeStruct((r, d), F32), jax.ShapeDtypeStruct((r, d), BF16),
                         jax.ShapeDtypeStruct((8, d), F32), jax.ShapeDtypeStruct((8, LANES), F32)),
        grid=(r // br,), in_specs=[row, row, pl.BlockSpec((1, d), lambda i: (0, 0))],
        out_specs=(row, row, pl.BlockSpec((8, d), lambda i: (0, 0)), pl.BlockSpec((8, LANES), lambda i: (0, 0))),
        compiler_params=_params("arbitrary"), name=name)(h, target, g)


def _split3(x):
    hi = x.astype(BF16)
    r1 = x - hi.astype(F32)
    mid = r1.astype(BF16)
    lo = (r1 - mid.astype(F32)).astype(BF16)
    return hi, mid, lo


def _tri_sum(tri, x):
    hi, mid, lo = _split3(x)
    dot = lambda t: jnp.dot(tri, t, preferred_element_type=F32)
    return dot(hi) + dot(mid) + dot(lo)


def _forget_cumsum(proj, b_pad, *, name):
    s = proj.shape[0]
    blk = LANES

    def body(f_ref, b_ref, c_ref):
        ri = lax.broadcasted_iota(jnp.int32, (blk, blk), 0)
        ci = lax.broadcasted_iota(jnp.int32, (blk, blk), 1)
        tri = (ci <= ri).astype(BF16)
        bias = b_ref[...]

        def step(t, carry):
            rows = pl.ds(pl.multiple_of(t * blk, blk), blk)
            z = f_ref[rows, :] + bias
            lf = jnp.minimum(z, 0.0) - jnp.log(1.0 + jnp.exp(-jnp.abs(z)))
            cs = _tri_sum(tri, lf) + carry
            c_ref[rows, :] = cs
            return cs[blk - 1:blk, :]

        lax.fori_loop(0, s // blk, step, jnp.zeros((1, blk), F32))

    return pl.pallas_call(body, out_shape=jax.ShapeDtypeStruct((s, LANES), F32), grid=(1,),
                          in_specs=[pl.BlockSpec((s, LANES), lambda i: (0, F_COL)), pl.BlockSpec((1, LANES), lambda i: (0, 0))],
                          out_specs=pl.BlockSpec((s, LANES), lambda i: (0, 0)), compiler_params=_params("arbitrary"), name=name)(proj, b_pad)


def _forget_cumsum_bwd(proj, b_pad, dcq, dck, *, name):
    s = proj.shape[0]
    blk = LANES
    nblk = s // blk

    def body(f_ref, b_ref, dcq_ref, dck_ref, dz_ref, db_ref):
        ri = lax.broadcasted_iota(jnp.int32, (blk, blk), 0)
        ci = lax.broadcasted_iota(jnp.int32, (blk, blk), 1)
        triu = (ci >= ri).astype(BF16)
        bias = b_ref[...]

        def step(t, carry):
            tail, dbs = carry
            rows = pl.ds(pl.multiple_of((nblk - 1 - t) * blk, blk), blk)
            dc = dcq_ref[rows, :] - dck_ref[rows, :]
            dlf = _tri_sum(triu, dc) + tail
            z = f_ref[rows, :] + bias
            e = jnp.exp(-jnp.abs(z))
            sig_neg = jnp.where(z >= 0.0, e, 1.0) / (1.0 + e)
            dz = dlf * sig_neg
            dz_ref[rows, :] = dz.astype(BF16)
            return dlf[0:1, :], dbs + jnp.sum(dz, axis=0, keepdims=True)

        _, dbs = lax.fori_loop(0, nblk, step, (jnp.zeros((1, blk), F32), jnp.zeros((1, blk), F32)))
        db_ref[...] = jnp.broadcast_to(dbs, db_ref.shape)

    full = pl.BlockSpec((s, LANES), lambda i: (0, 0))
    return pl.pallas_call(body, out_shape=(jax.ShapeDtypeStruct((s, LANES), BF16), jax.ShapeDtypeStruct((8, LANES), F32)), grid=(1,),
                          in_specs=[pl.BlockSpec((s, LANES), lambda i: (0, F_COL)), pl.BlockSpec((1, LANES), lambda i: (0, 0)), full, full],
                          out_specs=(full, pl.BlockSpec((8, LANES), lambda i: (0, 0))), compiler_params=_params("arbitrary"), name=name)(proj, b_pad, dcq, dck)


def _attn_fwd(q_arr, kv_arr, ck6, *, q_col, k_col, v_col, n_hp, causal, bq, bk, name, heads=None, heads_col=0):
    s = q_arr.shape[0]
    skv = kv_arr.shape[0]
    bias = ck6 is not None
    nq = s // bq
    assert not causal or bq == bk

    def body(*refs):
        q_ref, k_ref, v_ref = refs[:3]
        ck_ref = refs[3] if bias else None
        o_ref, lse_ref, heads_ref = refs[-3:]
        i = pl.program_id(1)
        lo = _lane_lo((bq, LANES))
        q = q_ref[...] * SCALE
        qh = (jnp.where(lo, q, 0.0).astype(BF16), jnp.where(lo, 0.0, q).astype(BF16))

        def block(j, carry, diagonal):
            ks = pl.ds(pl.multiple_of(j * bk, bk), bk)
            k = k_ref[ks, :].astype(BF16)
            v = v_ref[ks, :].astype(BF16)
            if diagonal:
                ok = lax.broadcasted_iota(jnp.int32, (bq, bk), 1) <= lax.broadcasted_iota(jnp.int32, (bq, bk), 0)
            out = []
            for h in range(2):
                m, l, acc = carry[3 * h:3 * h + 3]
                sc = lax.dot_general(qh[h], k, NT, preferred_element_type=F32)
                if bias:
                    sc = sc - ck_ref[0, h:h + 1, ks]
                if diagonal:
                    sc = jnp.where(ok, sc, NEG)
                mn = jnp.maximum(m, jnp.max(sc, axis=1, keepdims=True))
                p = jnp.exp(sc - mn)
                al = jnp.exp(m - mn)
                out += [mn, al * l + jnp.sum(p, axis=1, keepdims=True), al * acc + jnp.dot(p.astype(BF16), v, preferred_element_type=F32)]
            return tuple(out)

        col = lambda v_: jnp.full((bq, 1), v_, F32)
        init = (col(NEG), col(0.0), jnp.zeros((bq, LANES), F32)) * 2
        n_full = i if causal else skv // bk
        carry = lax.fori_loop(0, n_full, functools.partial(block, diagonal=False), init)
        if causal:
            carry = block(i, carry, True)
        m0, l0, a0, m1, l1, a1 = carry
        out = _pair(lo, a0 / l0, a1 / l1)
        o_ref[...] = out
        lse_ref[0] = _pair(lo, m0 + jnp.log(l0), m1 + jnp.log(l1))
        heads_ref[...] = out.astype(BF16)

    specs = [pl.BlockSpec((bq, LANES), lambda h, i: (i, q_col + h)),
             pl.BlockSpec((skv, LANES), lambda h, i: (0, k_col + h)),
             pl.BlockSpec((skv, LANES), lambda h, i: (0, v_col + h))]
    ins = [q_arr, kv_arr, kv_arr]
    if bias:
        specs += [pl.BlockSpec((1, 8, skv), lambda h, i: (h, 0, 0))]
        ins += [ck6]
    aliases = {}
    if heads is not None:
        aliases = {len(ins): 2}
        specs += [pl.BlockSpec(memory_space=pl.ANY)]
        ins += [heads]
    return pl.pallas_call(
        body, out_shape=(jax.ShapeDtypeStruct((s, n_hp * LANES), F32), jax.ShapeDtypeStruct((n_hp, s, LANES), F32),
                         jax.ShapeDtypeStruct((s, D_MODEL), BF16)),
        grid=(n_hp, nq), in_specs=specs,
        out_specs=(pl.BlockSpec((bq, LANES), lambda h, i: (i, h)), pl.BlockSpec((1, bq, LANES), lambda h, i: (h, i, 0)),
                   pl.BlockSpec((bq, LANES), lambda h, i: (i, heads_col + h))),
        input_output_aliases=aliases, compiler_params=_params("parallel", "parallel"), name=name)(*ins)


def _attn_bwd(q_arr, kv_arr, o_arr, do_arr, lse, ck6, *, q_col, k_col, v_col, o_col, n_hp, causal, bq, bk, name):
    s = q_arr.shape[0]
    skv = kv_arr.shape[0]
    bias = ck6 is not None
    nq = s // bq
    assert not causal or bq == bk

    def body(*refs):
        q_ref, k_ref, v_ref, o_ref, do_ref, lse_ref = refs[:6]
        if bias:
            ck_ref = refs[6]
            dq_ref, dk_ref, dv_ref, dcq_ref, dck_ref = refs[-5:]
        else:
            dq_ref, dk_ref, dv_ref = refs[-3:]
        j = pl.program_id(1)
        lo_q = _lane_lo((bq, LANES))
        lo_k = _lane_lo((bk, LANES))
        k = k_ref[...]
        v = v_ref[...].astype(BF16)
        kb = k.astype(BF16)
        kh = (jnp.where(lo_k, k, 0.0).astype(BF16), jnp.where(lo_k, 0.0, k).astype(BF16))
        if bias:
            pick_k = [(lax.broadcasted_iota(jnp.int32, (8, bk), 0) == h).astype(BF16) for h in range(2)]
            pick_q = [(lax.broadcasted_iota(jnp.int32, (8, bq), 0) == h).astype(BF16) for h in range(2)]

        @pl.when(j == 0)
        def _():
            dq_ref[...] = jnp.zeros_like(dq_ref)
            if bias:
                dcq_ref[...] = jnp.zeros_like(dcq_ref)

        def block(i, carry, diagonal):
            dk_acc, dv_acc, cs = carry
            qs = pl.ds(pl.multiple_of(i * bq, bq), bq)
            q = q_ref[qs, :] * SCALE
            do = do_ref[qs, :]
            dd = do * o_ref[qs, :]
            lse_i = lse_ref[0, qs, :]
            qh = (jnp.where(lo_q, q, 0.0).astype(BF16), jnp.where(lo_q, 0.0, q).astype(BF16))
            doh = (jnp.where(lo_q, do, 0.0).astype(BF16), jnp.where(lo_q, 0.0, do).astype(BF16))
            dh = (jnp.sum(jnp.where(lo_q, dd, 0.0), axis=1, keepdims=True), jnp.sum(jnp.where(lo_q, 0.0, dd), axis=1, keepdims=True))
            if diagonal:
                ok = lax.broadcasted_iota(jnp.int32, (bq, bk), 1) <= lax.broadcasted_iota(jnp.int32, (bq, bk), 0)
            dq_blk = None
            rs = None
            for h in range(2):
                sc = lax.dot_general(qh[h], kb, NT, preferred_element_type=F32)
                if bias:
                    sc = sc - ck_ref[0, h:h + 1, :]
                if diagonal:
                    sc = jnp.where(ok, sc, NEG)
                p = jnp.exp(sc - lse_i[:, h * HEAD_DIM:h * HEAD_DIM + 1])
                ds = p * (lax.dot_general(doh[h], v, NT, preferred_element_type=F32) - dh[h])
                dsb = ds.astype(BF16)
                dv_acc = dv_acc + lax.dot_general(p.astype(BF16), doh[h], TN, preferred_element_type=F32)
                dk_acc = dk_acc + lax.dot_general(dsb, qh[h], TN, preferred_element_type=F32)
                part = jnp.dot(dsb, kh[h], preferred_element_type=F32)
                dq_blk = part if dq_blk is None else dq_blk + part
                if bias:
                    cs = cs + jnp.dot(pick_q[h], dsb, preferred_element_type=F32)
                    row_sums = lax.dot_general(pick_k[h], dsb, NT, preferred_element_type=F32)
                    rs = row_sums if rs is None else rs + row_sums
            dq_ref[qs, :] += dq_blk * SCALE
            if bias:
                dcq_ref[0, :, qs] += rs
            return dk_acc, dv_acc, cs

        carry = (jnp.zeros((bk, LANES), F32), jnp.zeros((bk, LANES), F32), jnp.zeros((8, bk), F32))
        if causal:
            carry = block(j, carry, True)
        dk_acc, dv_acc, cs = lax.fori_loop(j + 1 if causal else 0, nq, functools.partial(block, diagonal=False), carry)
        dk_ref[...] = dk_acc.astype(BF16)
        dv_ref[...] = dv_acc.astype(BF16)
        if bias:
            dck_ref[0] = cs

    full_q = lambda c: pl.BlockSpec((s, LANES), lambda h, j: (0, c + h))
    specs = [full_q(q_col),
             pl.BlockSpec((bk, LANES), lambda h, j: (j, k_col + h)),
             pl.BlockSpec((bk, LANES), lambda h, j: (j, v_col + h)),
             full_q(0), full_q(o_col),
             pl.BlockSpec((1, s, LANES), lambda h, j: (h, 0, 0))]
    ins = [q_arr, kv_arr, kv_arr, o_arr, do_arr, lse]
    out_shape = [jax.ShapeDtypeStruct((s, n_hp * LANES), F32), jax.ShapeDtypeStruct((skv, n_hp * LANES), BF16),
                 jax.ShapeDtypeStruct((skv, n_hp * LANES), BF16)]
    out_specs = [full_q(0), pl.BlockSpec((bk, LANES), lambda h, j: (j, h)), pl.BlockSpec((bk, LANES), lambda h, j: (j, h))]
    if bias:
        specs += [pl.BlockSpec((1, 8, bk), lambda h, j: (h, 0, j))]
        ins += [ck6]
        out_shape += [jax.ShapeDtypeStruct((n_hp, 8, s), F32), jax.ShapeDtypeStruct((n_hp, 8, skv), F32)]
        out_specs += [pl.BlockSpec((1, 8, s), lambda h, j: (h, 0, 0)), pl.BlockSpec((1, 8, bk), lambda h, j: (h, 0, j))]
    return pl.pallas_call(body, out_shape=tuple(out_shape), grid=(n_hp, skv // bk), in_specs=specs, out_specs=tuple(out_specs),
                          compiler_params=_params("parallel", "arbitrary"), name=name)(*ins)


def _rope_tables(s):
    inv = 1.0 / (ROPE_THETA ** (jnp.arange(0, HEAD_DIM, 2, dtype=F32) / HEAD_DIM))
    ang = jnp.arange(s, dtype=F32)[:, None] * inv[None, :]
    cos, sin = jnp.cos(ang), jnp.sin(ang)
    return jnp.tile(cos, (1, 4)), jnp.concatenate([-sin, sin, -sin, sin], axis=1)


def _rope_tile(x, cos, sin):
    first = (lax.broadcasted_iota(jnp.int32, x.shape, 1) % HEAD_DIM) < (HEAD_DIM // 2)
    swapped = jnp.where(first, pltpu.roll(x, LANES - HEAD_DIM // 2, 1), pltpu.roll(x, HEAD_DIM // 2, 1))
    return x * cos + swapped * sin


def _rope(x_arr, cos_t, sin_t, *, n_cols, out_dtype, br, name):
    s = x_arr.shape[0]

    def body(x_ref, c_ref, s_ref, o_ref):
        cos, sin = c_ref[...], s_ref[...]
        for j in range(n_cols):
            lanes = slice(j * LANES, (j + 1) * LANES)
            o_ref[:, lanes] = _rope_tile(x_ref[:, lanes].astype(F32), cos, sin).astype(o_ref.dtype)

    tab = pl.BlockSpec((br, LANES), lambda i: (i, 0))
    blk = pl.BlockSpec((br, n_cols * LANES), lambda i: (i, 0))
    return pl.pallas_call(body, out_shape=jax.ShapeDtypeStruct((s, n_cols * LANES), out_dtype), grid=(s // br,),
                          in_specs=[blk, tab, tab], out_specs=blk, compiler_params=_params("parallel"), name=name)(x_arr, cos_t, sin_t)


def _stack_heads(x):
    lo = _lane_lo(x.shape)
    return jnp.concatenate([jnp.where(lo, x, 0.0), jnp.where(lo, 0.0, x)], axis=0).astype(BF16)


def _unstack_heads(x):
    return jnp.where(_lane_lo((DIL_L, LANES)), x[:DIL_L], x[DIL_L:])


def _dil_scores(q_ref, k_ref, cur, prev, has_prev):
    qs = _stack_heads(q_ref[cur, :] * SCALE)
    kk = jnp.concatenate([k_ref[prev, :], k_ref[cur, :]], axis=0).astype(BF16)
    a = lax.broadcasted_iota(jnp.int32, (2 * DIL_L, 2 * DIL_L), 0) & (DIL_L - 1)
    c = lax.broadcasted_iota(jnp.int32, (2 * DIL_L, 2 * DIL_L), 1)
    ok = ((c < DIL_L) & (c >= a) & has_prev) | ((c >= DIL_L) & (c - DIL_L <= a))
    return qs, kk, jnp.where(ok, lax.dot_general(qs, kk, NT, preferred_element_type=F32), NEG)


def _dil_rows(t, dil):
    r, m = t % dil, t // dil
    start = m * (DIL_L * dil) + r
    prev = jnp.maximum(start - DIL_L * dil, 0)
    return pl.ds(start, DIL_L, stride=dil), pl.ds(prev, DIL_L, stride=dil), m > 0


def _softmax3(a, b, c):
    m = jnp.maximum(jnp.maximum(a, b), c)
    ea, eb, ec = jnp.exp(a - m), jnp.exp(b - m), jnp.exp(c - m)
    den = ea + eb + ec
    inv = 1.0 / den
    return ea * inv, eb * inv, ec * inv, m + jnp.log(den)


def _dil_fwd(qk_r, proj, *, name):
    s = qk_r.shape[0]
    nsub = s // DIL_L
    mb = 512

    def body(q_ref, k_ref, v_ref, mix_ref, l1_ref, l2_ref, l3_ref, heads_ref, o1_scr, o2_scr, o3_scr):
        for (_, dil), o_scr, l_ref in zip(DILATED_BRANCHES, (o1_scr, o2_scr, o3_scr), (l1_ref, l2_ref, l3_ref)):
            def step(t, carry, dil=dil, o_scr=o_scr, l_ref=l_ref):
                cur, prev, has_prev = _dil_rows(t, dil)
                _, _, sc = _dil_scores(q_ref, k_ref, cur, prev, has_prev)
                vv = jnp.concatenate([v_ref[prev, :], v_ref[cur, :]], axis=0).astype(BF16)
                m = jnp.max(sc, axis=1, keepdims=True)
                e = jnp.exp(sc - m)
                den = jnp.sum(e, axis=1, keepdims=True)
                o = jnp.dot((e * (1.0 / den)).astype(BF16), vv, preferred_element_type=F32)
                o_scr[cur, :] = _unstack_heads(o)
                l_ref[cur, :] = _unstack_heads(jnp.broadcast_to(m + jnp.log(den), (2 * DIL_L, LANES)))
                return carry

            lax.fori_loop(0, nsub, step, 0, unroll=min(DIL_UNROLL_FWD, nsub))

        def merge(i, carry):
            rows = pl.ds(pl.multiple_of(i * mb, mb), mb)
            wa, wb, wc, _ = _softmax3(l1_ref[rows, :], l2_ref[rows, :], l3_ref[rows, :])
            mix = wa * o1_scr[rows, :] + wb * o2_scr[rows, :] + wc * o3_scr[rows, :]
            mix_ref[rows, :] = mix
            heads_ref[rows, :] = mix.astype(BF16)
            return carry

        lax.fori_loop(0, s // mb, merge, 0)

    col = lambda arr_col: pl.BlockSpec((s, LANES), lambda h: (0, arr_col + h))
    shp = jax.ShapeDtypeStruct((s, D_MIX), F32)
    mix, l1, l2, l3, heads = pl.pallas_call(
        body, out_shape=(shp, shp, shp, shp, jax.ShapeDtypeStruct((s, D_MODEL), BF16)), grid=(N_MIX_HP,),
        in_specs=[col(0), col(N_MIX_HP), col(2 * N_MIX_HP)],
        out_specs=(col(0),) * 5, scratch_shapes=[pltpu.VMEM((s, LANES), F32)] * 3,
        compiler_params=_params("parallel"), name=name)(qk_r, qk_r, proj)
    return mix, (l1, l2, l3), heads


def _dil_bwd(qk_r, proj, mix, dheads, lses, *, name):
    s = qk_r.shape[0]
    nsub = s // DIL_L
    mb = 512

    def body(q_ref, k_ref, v_ref, mix_ref, dm_ref, l1_ref, l2_ref, l3_ref, dq_ref, dk_ref, dv_ref, lt_scr, dd_scr):
        lo = _lane_lo((DIL_L, LANES))
        lo_m = _lane_lo((mb, LANES))

        def prep(i, carry):
            rows = pl.ds(pl.multiple_of(i * mb, mb), mb)
            _, _, _, lt = _softmax3(l1_ref[rows, :], l2_ref[rows, :], l3_ref[rows, :])
            lt_scr[rows, :] = lt
            dd = dm_ref[rows, :] * mix_ref[rows, :]
            dd_scr[rows, :] = _pair(lo_m, jnp.sum(jnp.where(lo_m, dd, 0.0), axis=1, keepdims=True),
                                    jnp.sum(jnp.where(lo_m, 0.0, dd), axis=1, keepdims=True))
            zero = jnp.zeros((mb, LANES), F32)
            dq_ref[rows, :] = zero
            dk_ref[rows, :] = zero
            dv_ref[rows, :] = zero
            return carry

        lax.fori_loop(0, s // mb, prep, 0)

        for (_, dil), l_ref in zip(DILATED_BRANCHES, (l1_ref, l2_ref, l3_ref)):
            def step(t, carry, dil=dil, l_ref=l_ref):
                cur, prev, has_prev = _dil_rows(t, dil)
                qs, kk, sc = _dil_scores(q_ref, k_ref, cur, prev, has_prev)
                vv = jnp.concatenate([v_ref[prev, :], v_ref[cur, :]], axis=0).astype(BF16)
                lg = l_ref[cur, :]
                w = jnp.exp(lg - lt_scr[cur, :])
                wd = w * dd_scr[cur, :]
                column = lambda x: jnp.concatenate([x[:, 0:1], x[:, HEAD_DIM:HEAD_DIM + 1]], axis=0)
                dos = _stack_heads(w * dm_ref[cur, :])
                p = jnp.exp(sc - column(lg))
                ds = (p * (lax.dot_general(dos, vv, NT, preferred_element_type=F32) - column(wd))).astype(BF16)
                dq_ref[cur, :] += _unstack_heads(jnp.dot(ds, kk, preferred_element_type=F32)) * SCALE
                dkk = lax.dot_general(ds, qs, TN, preferred_element_type=F32)
                dvv = lax.dot_general(p.astype(BF16), dos, TN, preferred_element_type=F32)
                dk_ref[cur, :] += dkk[DIL_L:]
                dv_ref[cur, :] += dvv[DIL_L:]
                dk_ref[prev, :] += dkk[:DIL_L]
                dv_ref[prev, :] += dvv[:DIL_L]
                return carry

            lax.fori_loop(0, nsub, step, 0, unroll=DIL_UNROLL_BWD)

    col = lambda arr_col: pl.BlockSpec((s, LANES), lambda h: (0, arr_col + h))
    shp = jax.ShapeDtypeStruct((s, D_MIX), F32)
    return pl.pallas_call(
        body, out_shape=(shp, shp, shp), grid=(N_MIX_HP,),
        in_specs=[col(0), col(N_MIX_HP), col(2 * N_MIX_HP), col(0), col(0), col(0), col(0), col(0)], out_specs=(col(0),) * 3,
        scratch_shapes=[pltpu.VMEM((s, LANES), F32)] * 2,
        compiler_params=_params("parallel"), name=name)(qk_r, qk_r, proj, mix, dheads, *lses)


CONV_BR = 512
FF_CHUNK = 2 * D_FF // N_DEV
FF_HALF = N_DEV // 2
HALO = 8


def _shift_down(x, halo, k):
    row = lax.broadcasted_iota(jnp.int32, x.shape, 0)
    y = pltpu.roll(x, k, 0)
    for r in range(k):
        y = jnp.where(row == r, halo[HALO - k + r:HALO - k + r + 1, :], y)
    return y


def _shift_up(x, halo, k):
    n = x.shape[0]
    row = lax.broadcasted_iota(jnp.int32, x.shape, 0)
    y = pltpu.roll(x, n - k, 0)
    for r in range(k):
        y = jnp.where(row == n - k + r, halo[r:r + 1, :], y)
    return y


def _conv_vals(u, halo, w, b):
    s1 = _shift_down(u, halo, 1)
    s2 = _shift_down(u, halo, 2)
    return b + w[0:1, :] * s2 + w[1:2, :] * s1 + w[2:3, :] * u, s1, s2


def _conv_in_specs(order, layer):
    rc = (lambda i, j: (i, j)) if order == "rc" else (lambda j, i: (i, j))
    per = CONV_BR // HALO
    main = lambda off: pl.BlockSpec((None, CONV_BR, FF_CHUNK), lambda *g: (off + rc(*g)[1], rc(*g)[0], 0))
    halo = lambda off: pl.BlockSpec((None, HALO, FF_CHUNK), lambda *g: (off + rc(*g)[1], jnp.maximum(rc(*g)[0] * per - 1, 0), 0))
    wspec = lambda off: pl.BlockSpec((None, None, 3, FF_CHUNK), lambda *g: (layer, off + rc(*g)[1], 0, 0))
    bspec = lambda off: pl.BlockSpec((None, 1, FF_CHUNK), lambda *g: (off + rc(*g)[1], 0, 0))
    return [main(0), halo(0), main(FF_HALF), halo(FF_HALF), wspec(0), wspec(FF_HALF), bspec(0), bspec(FF_HALF)]


def _conv_fwd(u, cw, cb, layer, *, name):
    s = u.shape[1]

    def body(uv_ref, hv_ref, ug_ref, hg_ref, wv_ref, wg_ref, bv_ref, bg_ref, o_ref):
        first = pl.program_id(0) == 0
        hv = jnp.where(first, 0.0, hv_ref[...])
        hg = jnp.where(first, 0.0, hg_ref[...])
        val, _, _ = _conv_vals(uv_ref[...], hv, wv_ref[...], bv_ref[...])
        gate, _, _ = _conv_vals(ug_ref[...], hg, wg_ref[...], bg_ref[...])
        o_ref[...] = (gate / (1.0 + jnp.exp(-gate)) * val).astype(BF16)

    return pl.pallas_call(body, out_shape=jax.ShapeDtypeStruct((FF_HALF, s, FF_CHUNK), BF16), grid=(s // CONV_BR, FF_HALF),
                          in_specs=_conv_in_specs("rc", layer), out_specs=pl.BlockSpec((None, CONV_BR, FF_CHUNK), lambda i, j: (j, i, 0)),
                          compiler_params=_params("parallel", "parallel"), name=name)(u, u, u, u, cw, cw, cb, cb)


def _swiglu_bwd(val, gate, da):
    sg = 1.0 / (1.0 + jnp.exp(-gate))
    return da * (gate * sg), da * val * (sg * (1.0 + gate * (1.0 - sg)))


def _conv_bwd(u, cw, cb, da, layer, *, name):
    s = u.shape[1]
    nrow = s // CONV_BR
    per = CONV_BR // HALO

    def body(uv_ref, hv_ref, ug_ref, hg_ref, wv_ref, wg_ref, bv_ref, bg_ref, da_ref, nv_ref, ng_ref, nda_ref, du_ref, dwb_ref):
        i = pl.program_id(1)
        first, last = i == 0, i == nrow - 1
        hv = jnp.where(first, 0.0, hv_ref[...])
        hg = jnp.where(first, 0.0, hg_ref[...])
        uv, ug = uv_ref[...], ug_ref[...]
        wv, wg, bv, bg = wv_ref[...], wg_ref[...], bv_ref[...], bg_ref[...]
        val, v1, v2 = _conv_vals(uv, hv, wv, bv)
        gate, g1, g2 = _conv_vals(ug, hg, wg, bg)
        dval, dgate = _swiglu_bwd(val, gate, da_ref[...])
        val_n, _, _ = _conv_vals(nv_ref[...], uv[CONV_BR - HALO:, :], wv, bv)
        gate_n, _, _ = _conv_vals(ng_ref[...], ug[CONV_BR - HALO:, :], wg, bg)
        dval_n, dgate_n = _swiglu_bwd(val_n, gate_n, nda_ref[...])
        dval_n = jnp.where(last, 0.0, dval_n)
        dgate_n = jnp.where(last, 0.0, dgate_n)
        back = lambda dc, dc_n, w: w[2:3, :] * dc + w[1:2, :] * _shift_up(dc, dc_n, 1) + w[0:1, :] * _shift_up(dc, dc_n, 2)
        du_ref[0] = back(dval, dval_n, wv).astype(BF16)
        du_ref[1] = back(dgate, dgate_n, wg).astype(BF16)

        @pl.when(first)
        def _():
            dwb_ref[...] = jnp.zeros_like(dwb_ref)

        cs = lambda t: jnp.sum(t, axis=0, keepdims=True)
        r8 = lax.broadcasted_iota(jnp.int32, (8, FF_CHUNK), 0)
        rows4 = lambda a, b, c, d: jnp.where(r8 == 0, a, jnp.where(r8 == 1, b, jnp.where(r8 == 2, c, jnp.where(r8 == 3, d, 0.0))))
        dwb_ref[0] += rows4(cs(dval * v2), cs(dval * v1), cs(dval * uv), cs(dval))
        dwb_ref[1] += rows4(cs(dgate * g2), cs(dgate * g1), cs(dgate * ug), cs(dgate))

    nxt = lambda off: pl.BlockSpec((None, HALO, FF_CHUNK), lambda j, i: (off + j, jnp.minimum((i + 1) * per, nrow * per - 1), 0))
    specs = _conv_in_specs("cr", layer) + [pl.BlockSpec((None, CONV_BR, FF_CHUNK), lambda j, i: (j, i, 0)), nxt(0), nxt(FF_HALF), nxt(0)]
    return pl.pallas_call(
        body, out_shape=(jax.ShapeDtypeStruct((2, FF_HALF, s, FF_CHUNK), BF16), jax.ShapeDtypeStruct((2, FF_HALF, 8, FF_CHUNK), F32)),
        grid=(FF_HALF, nrow), in_specs=specs,
        out_specs=(pl.BlockSpec((2, None, CONV_BR, FF_CHUNK), lambda j, i: (0, j, i, 0)),
                   pl.BlockSpec((2, None, 8, FF_CHUNK), lambda j, i: (0, j, 0, 0))),
        compiler_params=_params("parallel", "arbitrary"), name=name)(u, u, u, u, cw, cw, cb, cb, da, u, u, da)


def _rows_of(r):
    return lambda ref, idx: ref.at[:, pl.ds(idx * r, r), :]


def _slot1(ref, idx):
    return ref.at[:, idx]


def _slot0(ref, idx):
    return ref.at[idx]


def _all_gather(shards, full_shapes, places, *, name):
    n = len(shards)

    def body(*refs):
        ins, outs = refs[:n], refs[n:2 * n]
        send_sems, recv_sems, local_sems = refs[2 * n:]
        mx, my, mc = lax.axis_index("x"), lax.axis_index("y"), lax.axis_index("c")
        me, sibling = (mx, my, mc), (mx, my, 1 - mc)
        chips = [(1 - mx, my), (mx, 1 - my), (1 - mx, 1 - my)]

        def win(t, px, py, pc):
            return places[t](outs[t], 4 * px + 2 * py + pc)

        def copy(t, k, block, to, src=None):
            return pltpu.make_async_remote_copy(src_ref=win(t, *block) if src is None else src, dst_ref=win(t, *block),
                                                send_sem=send_sems.at[t, k], recv_sem=recv_sems.at[t, k], device_id=to, device_id_type=MESH)

        mine = [pltpu.make_async_copy(ins[t], win(t, *me), local_sems.at[t]) for t in range(n)]
        for cp in mine:
            cp.start()
        first = []
        for t in range(n):
            first += [copy(t, 0, me, sibling, src=ins[t])] + [copy(t, 1 + j, me, (*chip, mc), src=ins[t]) for j, chip in enumerate(chips)]
        for cp in first:
            cp.start()
        passed = []
        for j, chip in enumerate(chips):
            for t in range(n):
                copy(t, 1 + j, (*chip, mc), me).wait_recv()
                fwd = copy(t, 4 + j, (*chip, mc), sibling)
                fwd.start()
                passed.append(fwd)
        for t in range(n):
            copy(t, 0, sibling, me).wait_recv()
            for j, chip in enumerate(chips):
                copy(t, 4 + j, (*chip, 1 - mc), me).wait_recv()
        for cp in first + passed:
            cp.wait_send()
        for cp in mine:
            cp.wait()

    hbm = pl.BlockSpec(memory_space=pl.ANY)
    return pl.pallas_call(
        body, out_shape=tuple(jax.ShapeDtypeStruct(s, x.dtype) for s, x in zip(full_shapes, shards)),
        in_specs=[hbm] * n, out_specs=(hbm,) * n,
        scratch_shapes=[pltpu.SemaphoreType.DMA((n, 7)), pltpu.SemaphoreType.DMA((n, 7)), pltpu.SemaphoreType.DMA((n,))],
        name=name)(*shards)


FLIPS = [(fx, fy, fc) for fx in (0, 1) for fy in (0, 1) for fc in (0, 1)][1:]


def _exchange_copies(kind, places, src, land, send_sems, recv_sems, local_sems):
    mx, my, mc = lax.axis_index("x"), lax.axis_index("y"), lax.axis_index("c")
    me = 4 * mx + 2 * my + mc
    n = len(src)
    local, remote = [], []
    for t in range(n):
        if kind == "gather":
            local.append(pltpu.make_async_copy(src[t], places[t](land[t], me), local_sems.at[t]))
        else:
            local.append(pltpu.make_async_copy(places[t](src[t], me), land[t].at[me], local_sems.at[t]))
    for k, (fx, fy, fc) in enumerate(FLIPS):
        px, py, pc = mx ^ fx, my ^ fy, mc ^ fc
        peer = 4 * px + 2 * py + pc
        for t in range(n):
            sems = dict(send_sem=send_sems.at[7 * t + k], recv_sem=recv_sems.at[7 * t + k], device_id=(px, py, pc), device_id_type=MESH)
            if kind == "gather":
                pair = [(src[t], places[t](land[t], me)), (src[t], places[t](land[t], peer))]
            else:
                pair = [(places[t](src[t], peer), land[t].at[me]), (places[t](src[t], peer), land[t].at[peer])]
            remote.append([functools.partial(pltpu.make_async_remote_copy, src_ref=s_, dst_ref=d_, **sems) for s_, d_ in pair])
    return local, remote


HBM_SPEC = pl.BlockSpec(memory_space=pltpu.HBM)
SEM_SPEC = pl.BlockSpec(memory_space=pltpu.SEMAPHORE)
SIDE_EFFECT = pltpu.SideEffectType.DATAFLOW_SIDE_EFFECTING


def _exchange_start(kind, srcs, land_shapes, places, after, *, name):
    n = len(srcs)

    def body(*refs):
        src, land = refs[:n], refs[n:2 * n]
        send_sems, recv_sems, local_sems = refs[2 * n + 1:2 * n + 4]
        token = refs[-1]
        local, remote = _exchange_copies(kind, places, src, land, send_sems, recv_sems, local_sems)
        for cp in local:
            cp.start()
        for send, _ in remote:
            send().start()
        token[...] = jnp.zeros_like(token)

    hbm = lambda t: pltpu.with_memory_space_constraint(t, pltpu.HBM)
    lands = [hbm(lax.empty(tuple(s), x.dtype)) for s, x in zip(land_shapes, srcs)]
    out_shape = (pltpu.SemaphoreType.DMA((7 * n,)), pltpu.SemaphoreType.DMA((7 * n,)), pltpu.SemaphoreType.DMA((n,)),
                 *[pltpu.HBM(x.shape, x.dtype) for x in srcs], *[pltpu.HBM(tuple(s), x.dtype) for s, x in zip(land_shapes, srcs)],
                 jax.ShapeDtypeStruct((8, LANES), F32))
    outs = pl.pallas_call(
        body, name=name, out_shape=out_shape, in_specs=[HBM_SPEC] * (2 * n) + [pl.BlockSpec(memory_space=pl.ANY)],
        out_specs=(SEM_SPEC, SEM_SPEC, SEM_SPEC) + (HBM_SPEC,) * (2 * n) + (pl.BlockSpec(memory_space=pltpu.VMEM),),
        input_output_aliases={i: 3 + i for i in range(2 * n)},
        compiler_params=pltpu.CompilerParams(has_side_effects=SIDE_EFFECT))(*[hbm(x) for x in srcs], *lands, after)
    return dict(sems=outs[:3], src=outs[3:3 + n], land=outs[3 + n:3 + 2 * n], token=outs[-1])


def _exchange_wait(kind, started, places, after, *, name):
    n = len(started["src"])

    def body(*refs):
        src, land = refs[:n], refs[n:2 * n]
        send_sems, recv_sems, local_sems = refs[2 * n:2 * n + 3]
        local, remote = _exchange_copies(kind, places, src, land, send_sems, recv_sems, local_sems)
        for cp in local:
            cp.wait()
        for send, arrival in remote:
            send().wait_send()
            arrival().wait_recv()

    out_shape = tuple(pltpu.HBM(x.shape, x.dtype) for x in started["src"]) + tuple(pltpu.HBM(x.shape, x.dtype) for x in started["land"])
    outs = pl.pallas_call(
        body, name=name, out_shape=out_shape,
        in_specs=[HBM_SPEC] * (2 * n) + [SEM_SPEC] * 3 + [pl.BlockSpec(memory_space=pl.ANY)], out_specs=(HBM_SPEC,) * (2 * n),
        input_output_aliases={i: i for i in range(2 * n)},
        compiler_params=pltpu.CompilerParams(has_side_effects=SIDE_EFFECT))(*started["src"], *started["land"], *started["sems"], after)
    return list(outs[n:])


def _adamw(parts, w, m, v, *, br, name):
    layers, r, wd = w.shape
    assert len(parts) == layers

    def body(*refs):
        p_refs = refs[:layers]
        w_ref, m_ref, v_ref, g_ref, d_ref, nm_ref, nv_ref = refs[layers:]
        for k in range(layers):
            @pl.when(pl.program_id(0) == k)
            def _(p_ref=p_refs[k]):
                g = p_ref[0].astype(F32)
                for dev in range(1, N_DEV):
                    g = g + p_ref[dev].astype(F32)
                mm = ADAM_B1 * m_ref[...] + (1.0 - ADAM_B1) * g
                vv = ADAM_B2 * v_ref[...] + (1.0 - ADAM_B2) * (g * g)
                m_hat = mm / (1.0 - ADAM_B1 ** ADAM_STEP)
                v_hat = vv / (1.0 - ADAM_B2 ** ADAM_STEP)
                g_ref[...] = g
                d_ref[...] = -ADAM_LR * (m_hat / (jnp.sqrt(v_hat) + ADAM_EPS) + ADAM_WD * w_ref[...])
                nm_ref[...] = mm
                nv_ref[...] = vv

    p_spec = lambda k: pl.BlockSpec((N_DEV, None, br, wd), lambda l, i: (0, 0, jnp.where(l == k, i, 0), 0))
    blk = pl.BlockSpec((None, br, wd), lambda l, i: (l, i, 0))
    shp = jax.ShapeDtypeStruct((layers, r, wd), F32)
    return pl.pallas_call(body, out_shape=(shp, shp, shp, shp), grid=(layers, r // br),
                          in_specs=[p_spec(k) for k in range(layers)] + [blk, blk, blk], out_specs=(blk, blk, blk, blk),
                          compiler_params=_params("arbitrary", "arbitrary"), name=name)(*parts, w, m, v)


SMALL = ("norm_mix", "norm_mem", "norm_ffn", "b_forget", "conv_b", "norm_final")


def _pack(tensors):
    flat = jnp.concatenate([t.reshape(-1) for t in tensors])
    rows = -(-flat.shape[0] // (PACK_W * PACK_ROW_ALIGN)) * PACK_ROW_ALIGN
    flat = jnp.pad(flat, (0, rows * PACK_W - flat.shape[0]))
    return flat.reshape(1, rows, PACK_W)


def _unpack(buf, shapes):
    flat = buf.reshape(-1)
    out, off = [], 0
    for shp in shapes:
        n = math.prod(shp)
        out.append(flat[off:off + n].reshape(tuple(shp)))
        off += n
    return out


def _fox_permute(w):
    pad = jnp.zeros(w.shape[:-1] + (FOX_P - FOX_IN,), w.dtype)
    return jnp.concatenate([w[..., :3 * D_MIX], w[..., 3 * D_MIX + N_MIX_HEADS:], w[..., 3 * D_MIX:3 * D_MIX + N_MIX_HEADS], pad], axis=-1)


def _fox_unpermute(w):
    return jnp.concatenate([w[..., :3 * D_MIX], w[..., DIL_IN:DIL_IN + N_MIX_HEADS], w[..., 3 * D_MIX:DIL_IN]], axis=-1)


def _bias_layout(c):
    s = c.shape[0]
    ct = c[:, :N_MIX_HEADS].T.reshape(N_MIX_HP, 2, s)
    return jnp.pad(ct, ((0, 0), (0, 6), (0, 0)))


def _bias_grad(dck6):
    s = dck6.shape[2]
    dk = dck6[:, :2, :].reshape(N_MIX_HEADS, s).T
    return jnp.pad(dk, ((0, 0), (0, LANES - N_MIX_HEADS)))


def _device_step(x, mem, target, small, get_weights, put_grads):
    s = x.shape[0]
    mt = mem.shape[0]
    bq = 512
    cos_t, sin_t = _rope_tables(s)
    row = lambda t, l: t[l][None, :]
    saved = []
    h = x
    cb8 = small["conv_b"].reshape(DEPTH, N_DEV, 1, FF_CHUNK)
    for l in range(DEPTH):
        kind, slot = l % 2, l // 2
        wl = dict(get_weights(l, "attn", h))
        if l == 0:
            xn = _rmsnorm_fwd(h, row(small["norm_mix"], l), br=512, name=f"norm_mix_fwd{l}")
        mn = _rmsnorm_fwd(mem, row(small["norm_mem"], l), br=mt, name=f"norm_mem_fwd{l}")
        if kind == 0:
            proj = _mm(xn, wl["w_in"], "nn", tm=1024, tn=384, layer=0, name=f"in_proj{l}")
        else:
            proj, qk_r = _mm(xn, wl["w_in"], "nt", tm=1024, tn=512, layer=0, rope=(cos_t, sin_t, 2 * D_MIX), name=f"in_proj{l}")
        kvm = _mm(mn, wl["w_mem_kv"], "nn", tm=mt, tn=512, layer=0, name=f"mem_kv{l}")
        st = dict(h=h, xn=xn, mn=mn, proj=proj, kvm=kvm, w=wl)
        if kind == 0:
            b_pad = jnp.pad(small["b_forget"][slot], (0, LANES - N_MIX_HEADS))[None, :]
            c = _forget_cumsum(proj, b_pad, name=f"forget_cumsum{l}")
            ck6 = _bias_layout(c)
            mix, lse, heads = _attn_fwd(proj, proj, ck6, q_col=0, k_col=N_MIX_HP, v_col=2 * N_MIX_HP, n_hp=N_MIX_HP,
                                        causal=True, bq=min(s, 1024), bk=min(s, 1024), name=f"fox_fwd{l}")
            st.update(b_pad=b_pad, ck6=ck6, mix=mix, lse=lse)
        else:
            mix, lses, heads = _dil_fwd(qk_r, proj, name=f"dil_fwd{l}")
            st.update(qk_r=qk_r, lses=lses, mix=mix)
        mo, lse_m, heads = _attn_fwd(proj, kvm, None, q_col=QM_COL, k_col=0, v_col=N_MEM_HP, n_hp=N_MEM_HP, causal=False,
                                     bq=min(s, 2048), bk=mt, heads=heads, heads_col=N_MIX_HP, name=f"mem_fwd{l}")
        h1, xf = _mm(heads, wl["w_out"], "nn", tm=1024, tn=D_MODEL, res=h, layer=0, norm_gain=row(small["norm_ffn"], l),
                     name=f"out_proj{l}")
        wl.update(get_weights(l, "ffn", xf))
        u = _mm(xf, wl["w_up"], "nt", tm=1024, tn=FF_CHUNK, layer=0, chunk="b", name=f"up_proj{l}")
        a = _conv_fwd(u, wl["conv_w"], cb8[l], 0, name=f"conv_fwd{l}")
        st.update(mo=mo, lse_m=lse_m, heads=heads, h1=h1, xf=xf, u=u, a=a)
        saved.append(st)
        if l + 1 < DEPTH:
            h, xn = _mm(a, wl["w_down"], "nn", tm=512, tn=D_MODEL, res=h1, layer=0, chunk="reduce",
                        norm_gain=row(small["norm_mix"], l + 1), name=f"down_proj{l}")
        else:
            h = _mm(a, wl["w_down"], "nn", tm=1024, tn=512, res=h1, layer=0, chunk="reduce", name=f"down_proj{l}")

    dh, dhb, dg_final, loss = _loss_head(h, target, small["norm_final"][None, :], br=512, name="loss_head")
    gs = {k: [None] * DEPTH for k in ("norm_mix", "norm_mem", "norm_ffn", "conv_b")}
    gs["b_forget"] = [None] * 2
    dep = 0.0
    for l in reversed(range(DEPTH)):
        st = saved[l]
        wl = st["w"]
        gw = {}
        kind, slot = l % 2, l // 2
        da = _mm(dhb, wl["w_down"], "nt", tm=1024, tn=FF_CHUNK, layer=0, chunk="b", name=f"down_dx{l}")
        gw["w_down"] = _mm(st["a"], dhb, "tn", tm=FF_CHUNK, tn=512, out_dtype=BF16, chunk="a", name=f"down_dw{l}")
        du, dwb = _conv_bwd(st["u"], wl["conv_w"], cb8[l] + dep, da, 0, name=f"conv_bwd{l}")
        du = du.reshape(N_DEV, s, FF_CHUNK)
        dwb = dwb.reshape(N_DEV, 8, FF_CHUNK)
        gw["conv_w"] = dwb
        gs["conv_b"][l] = dwb[:, 3, :].reshape(-1)
        dh1, dh1b, dgf = _mm(du, wl["w_up"], "nn", tm=256, tn=D_MODEL, layer=0, chunk="reduce",
                             norm_bwd=(st["h1"], row(small["norm_ffn"], l), dh), name=f"up_dx{l}")
        gw["w_up"] = _mm(du, st["xf"], "tn", tm=FF_CHUNK, tn=512, out_dtype=BF16, chunk="a", name=f"up_dw{l}")
        gs["norm_ffn"][l] = dgf[0]
        dheads = _mm(dh1b, wl["w_out"], "nt", tm=1024, tn=512, layer=0, name=f"out_dx{l}")
        gw["w_out"] = _mm(st["heads"], dh1b, "tn", tm=512, tn=512, out_dtype=BF16, name=f"out_dw{l}")
        dqm, dkm, dvm = _attn_bwd(st["proj"], st["kvm"], st["mo"], dheads, st["lse_m"], None, q_col=QM_COL, k_col=0,
                                  v_col=N_MEM_HP, o_col=N_MIX_HP, n_hp=N_MEM_HP, causal=False, bq=min(s, 1024), bk=mt,
                                  name=f"mem_bwd{l}")
        dkvm = jnp.concatenate([dkm, dvm], axis=1)
        gw["w_mem_kv"] = _mm(st["mn"], dkvm, "tn", tm=512, tn=512, out_dtype=BF16, name=f"mem_kv_dw{l}")
        dmn = _mm(dkvm, wl["w_mem_kv"], "nt", tm=mt, tn=512, layer=0, name=f"mem_kv_dx{l}")
        dep_early = put_grads(l, "ffn", gw)
        _, _, dgm = _rmsnorm_bwd(mem, dmn, row(small["norm_mem"], l), None, br=mt, name=f"norm_mem_bwd{l}")
        gs["norm_mem"][l] = dgm[0]
        if kind == 0:
            dq, dk, dv, dcq6, dck6 = _attn_bwd(st["proj"], st["proj"], st["mix"], dheads, st["lse"], st["ck6"] + dep_early, q_col=0,
                                               k_col=N_MIX_HP, v_col=2 * N_MIX_HP, o_col=0, n_hp=N_MIX_HP, causal=True,
                                               bq=bq, bk=bq, name=f"fox_bwd{l}")
            dz, db = _forget_cumsum_bwd(st["proj"], st["b_pad"], _bias_grad(dcq6), _bias_grad(dck6), name=f"forget_cumsum_bwd{l}")
            gs["b_forget"][slot] = db[0, :N_MIX_HEADS]
            dproj = jnp.concatenate([dq.astype(BF16), dk, dv, dqm.astype(BF16), dz], axis=1)
        else:
            dq_r, dk_r, dv = _dil_bwd(st["qk_r"], st["proj"], st["mix"], dheads, st["lses"], name=f"dil_bwd{l}")
            dq = _rope(dq_r, cos_t + dep_early, -sin_t, n_cols=N_MIX_HP, out_dtype=BF16, br=512, name=f"rope_bwd_q{l}")
            dk = _rope(dk_r, cos_t, -sin_t, n_cols=N_MIX_HP, out_dtype=BF16, br=512, name=f"rope_bwd_k{l}")
            dproj = jnp.concatenate([dq, dk, dv.astype(BF16), dqm.astype(BF16)], axis=1)
        if kind == 0:
            gw["w_in"] = _mm(st["xn"], dproj, "tn", tm=512, tn=384, out_dtype=BF16, name=f"in_dw{l}")
        else:
            gw["w_in"] = _mm(dproj, st["xn"], "tn", tm=512, tn=512, out_dtype=BF16, name=f"in_dw{l}")
        dh, dhb, dgx = _mm(dproj, wl["w_in"], "nt" if kind == 0 else "nn", tm=512, tn=D_MODEL, layer=0,
                           norm_bwd=(st["h"], row(small["norm_mix"], l), dh1), name=f"in_dx{l}")
        gs["norm_mix"][l] = dgx[0]
        dep = put_grads(l, "attn", gw)

    grads_s = {k: jnp.stack(v) for k, v in gs.items()}
    grads_s["norm_final"] = dg_final[0]
    return loss[0, 0], dh, grads_s


def kernel(x, mem, norm_mix, norm_mem, norm_ffn, w_in_fox, b_forget, w_in_dil, w_mem_kv, w_out, w_up, conv_w, conv_b, w_down, norm_final, loss_target, m_norm_mix, m_norm_mem, m_norm_ffn, m_w_in_fox, m_b_forget, m_w_in_dil, m_w_mem_kv, m_w_out, m_w_up, m_conv_w, m_conv_b, m_w_down, m_norm_final, v_norm_mix, v_norm_mem, v_norm_ffn, v_w_in_fox, v_b_forget, v_w_in_dil, v_w_mem_kv, v_w_out, v_w_up, v_conv_w, v_conv_b, v_w_down, v_norm_final):
    names = ["norm_mix", "norm_mem", "norm_ffn", "w_in_fox", "b_forget", "w_in_dil", "w_mem_kv", "w_out", "w_up", "conv_w", "conv_b",
             "w_down", "norm_final"]
    w = dict(zip(names, (norm_mix, norm_mem, norm_ffn, w_in_fox, b_forget, w_in_dil, w_mem_kv, w_out, w_up, conv_w, conv_b, w_down, norm_final)))
    m = dict(zip(names, (m_norm_mix, m_norm_mem, m_norm_ffn, m_w_in_fox, m_b_forget, m_w_in_dil, m_w_mem_kv, m_w_out, m_w_up, m_conv_w,
                         m_conv_b, m_w_down, m_norm_final)))
    v = dict(zip(names, (v_norm_mix, v_norm_mem, v_norm_ffn, v_w_in_fox, v_b_forget, v_w_in_dil, v_w_mem_kv, v_w_out, v_w_up, v_conv_w,
                         v_conv_b, v_w_down, v_norm_final)))
    big = ("w_in_fox", "w_in_dil", "w_mem_kv", "w_out", "w_up", "w_down", "conv_w")
    small_shapes = [w[k].shape for k in SMALL]
    dil_c = w_in_dil.shape[2]
    rows = {k: w[k].shape[1] for k in ("w_in_fox", "w_mem_kv", "w_out", "w_down")}

    def places(l):
        w_in_place = _rows_of(rows["w_in_fox"]) if l % 2 == 0 else _slot1
        return [w_in_place, _rows_of(rows["w_mem_kv"]), _rows_of(rows["w_out"]), _slot1, _rows_of(rows["w_down"]), _slot1]

    def full_shapes(l):
        w_in_shape = (1, D_MODEL, FOX_P) if l % 2 == 0 else (1, N_DEV, dil_c, D_MODEL)
        return [w_in_shape, (1, D_MODEL, 2 * D_MEMQ), (1, D_MODEL, D_MODEL), (1, N_DEV, FF_CHUNK, D_MODEL), (1, D_FF, D_MODEL),
                (1, N_DEV, 3, FF_CHUNK)]

    transposed = lambda t: jnp.swapaxes(t, 1, 2)
    to_wire = {"w_in_fox": lambda t: _fox_permute(t).astype(BF16), "w_in_dil": lambda t: t.T.astype(BF16),
               "w_mem_kv": lambda t: t.astype(BF16), "w_out": lambda t: t.astype(BF16), "w_up": lambda t: t.T.astype(BF16),
               "w_down": lambda t: t.astype(BF16), "conv_w": lambda t: t}
    shard = lambda k, i: to_wire[k](w[k][i])
    shard_shape = lambda k: jax.eval_shape(lambda: shard(k, 0)).shape
    everything = (0, 1, 2, 3, 4, 5)
    gather_groups = {l: ((0, 1, 2), (3, 4, 5)) if l == 0 else (everything,) for l in range(DEPTH)}
    scatter_groups = {l: ((1, 2, 3, 4, 5), (0,)) if l == 0 else (everything,) for l in range(DEPTH)}
    pick = lambda seq, group: [seq[i] for i in group]
    tag = lambda l, group: f"{l}" + ("" if group == everything else "_" + "".join(str(i) for i in group))

    gathers, after = {}, norm_final
    for l in range(DEPTH):
        w_in_shard = shard("w_in_fox" if l % 2 == 0 else "w_in_dil", l // 2)[None]
        shards = [w_in_shard] + [shard(k, l)[None] for k in ("w_mem_kv", "w_out", "w_up", "w_down", "conv_w")]
        for group in gather_groups[l]:
            gathers[l, group] = _exchange_start("gather", pick(shards, group), pick(full_shapes(l), group), pick(places(l), group), after,
                                                name=f"weights_gather_start{tag(l, group)}")
            after = gathers[l, group]["token"]
    started = sum(g["token"][0, 0] for g in gathers.values())
    small = {k: w[k] for k in SMALL}
    small["norm_mix"] = norm_mix + started
    landed = {}

    def get_weights(l, part, h):
        group = [g for g in gather_groups[l] if (0 if part == "attn" else 3) in g][0]
        if (l, group) not in landed:
            lands = _exchange_wait("gather", gathers[l, group], pick(places(l), group), h, name=f"weights_gather_wait{tag(l, group)}")
            landed[l, group] = dict(zip(group, lands))
        got = landed[l, group]
        if part == "ffn":
            return dict(w_up=got[3], w_down=got[4].reshape(1, FF_HALF, FF_CHUNK, D_MODEL), conv_w=got[5])
        w_in = got[0] if l % 2 == 0 else got[0].reshape(1, N_DEV * dil_c, D_MODEL)
        return dict(w_in=w_in, w_mem_kv=got[1], w_out=got[2])

    scatters, pending = {}, {}

    def put_grads(l, part, g):
        pending.setdefault(l, {}).update(g)
        if part == "ffn" and len(scatter_groups[l]) == 1:
            return 0.0
        group = scatter_groups[l][0 if part == "ffn" else -1]
        have = pending[l]
        srcs = {3: lambda: have["w_up"][None], 4: lambda: have["w_down"].reshape(1, D_FF, D_MODEL), 5: lambda: have["conv_w"][None],
                1: lambda: have["w_mem_kv"][None], 2: lambda: have["w_out"][None]}
        if l % 2 == 0:
            srcs[0] = lambda: have["w_in"][None]
        else:
            srcs[0] = lambda: have["w_in"].reshape(1, N_DEV, dil_c, D_MODEL)
        shard_shapes = [shard_shape("w_in_fox" if l % 2 == 0 else "w_in_dil")] + \
            [shard_shape(k) for k in ("w_mem_kv", "w_out", "w_up", "w_down")] + [(8, FF_CHUNK)]
        sources = [srcs[i]() for i in group]
        scatters[l, group] = _exchange_start("scatter", sources, [(N_DEV, 1) + tuple(s) for s in pick(shard_shapes, group)],
                                             pick(places(l), group), sources[0], name=f"grads_scatter_start{tag(l, group)}")
        return scatters[l, group]["token"][0, 0]

    loss, grad_x, gs = _device_step(x[0], mem[0], loss_target[0], small, get_weights, put_grads)

    recv = {}

    def wait_scatter(l, group, after_):
        lands = _exchange_wait("scatter", scatters[l, group], pick(places(l), group), after_, name=f"grads_scatter_wait{tag(l, group)}")
        recv.setdefault(l, {}).update(zip(group, lands))

    for l in reversed(range(1, DEPTH)):
        wait_scatter(l, everything, grad_x)
    wait_scatter(0, scatter_groups[0][0], grad_x)
    s_pack = _pack([gs[k] for k in SMALL])
    (s_recv,) = _all_gather([s_pack], [(N_DEV,) + s_pack.shape], [_slot0], name="small_grads_all_gather")

    layer_tensors = ("w_in", "w_mem_kv", "w_out", "w_up", "w_down", "conv_w")
    layer_parts = lambda k: [recv[l][layer_tensors.index(k)] for l in range(DEPTH)]
    to_local = {k: (lambda t: t) for k in big}
    to_local["w_in_fox"] = _fox_permute
    to_local["w_in_dil"] = to_local["w_up"] = transposed
    from_local = {k: (lambda t: t) for k in big}
    from_local["w_in_fox"] = _fox_unpermute
    from_local["w_in_dil"] = from_local["w_up"] = transposed
    blocks = {"w_in_fox": rows["w_in_fox"], "w_in_dil": dil_c, "w_mem_kv": rows["w_mem_kv"], "w_out": rows["w_out"], "w_up": FF_CHUNK // 4,
              "w_down": rows["w_down"] // 2, "conv_w": 3}
    outs = {}

    def update(k, parts):
        f = to_local[k]
        outs[k] = [from_local[k](t) for t in _adamw(parts, f(w[k]), f(m[k]), f(v[k]), br=blocks[k], name=f"adamw_{k}")]

    update("w_in_dil", [recv[l][0] for l in range(1, DEPTH, 2)])
    update("w_up", layer_parts("w_up"))
    update("w_down", layer_parts("w_down"))
    update("conv_w", [p[:, :, :3, :] for p in layer_parts("conv_w")])
    update("w_mem_kv", layer_parts("w_mem_kv"))
    update("w_out", layer_parts("w_out"))
    wait_scatter(0, scatter_groups[0][-1], outs["w_out"][1])
    update("w_in_fox", [recv[l][0] for l in range(0, DEPTH, 2)])
    small_outs = _adamw([s_recv], _pack([w[k] for k in SMALL]), _pack([m[k] for k in SMALL]), _pack([v[k] for k in SMALL]),
                        br=s_pack.shape[1], name="adamw_small")
    res = []
    for i, os_ in enumerate(small_outs):
        d = {k: outs[k][i] for k in big}
        d.update(zip(SMALL, _unpack(os_, small_shapes)))
        res.append([d[k] for k in names])
    loss = lax.psum(loss, ("x", "y", "c"))
    return (loss, grad_x[None], *res[0], *res[1], *res[2], *res[3])
```

```python
import functools
import math

import jax
import jax.numpy as jnp
from jax import lax
from jax.experimental import pallas as pl
from jax.experimental.pallas import tpu as pltpu

F32 = jnp.float32
BF16 = jnp.bfloat16

D_MODEL = 1024
HEAD_DIM = 64
N_MIX_HEADS = 12
N_MEM_HEADS = 4
D_MIX = N_MIX_HEADS * HEAD_DIM
D_MEMQ = N_MEM_HEADS * HEAD_DIM
D_FF = 2816
DEPTH = 4
FOX_IN = 3 * D_MIX + N_MIX_HEADS + D_MEMQ
DIL_IN = 3 * D_MIX + D_MEMQ
LANES = 128
FOX_P = DIL_IN + LANES
N_MIX_HP = D_MIX // LANES
N_MEM_HP = D_MEMQ // LANES
QM_COL = 3 * N_MIX_HP
F_COL = DIL_IN // LANES
DILATED_BRANCHES = ((128, 1), (512, 4), (2048, 16))
DIL_L = 128
DIL_UNROLL_FWD = 32
DIL_UNROLL_BWD = 32
ROPE_THETA = 10000.0
NORM_EPS = 1e-6
NEG = -1e30
SCALE = HEAD_DIM ** -0.5
N_DEV = 8

ADAM_LR = 0.001
ADAM_B1 = 0.9
ADAM_B2 = 0.999
ADAM_EPS = 1e-08
ADAM_WD = 0.01
ADAM_STEP = 10

VMEM_LIMIT = 56 * 1024 * 1024
PACK_W = 1024
PACK_ROW_ALIGN = 8

MESH = pl.DeviceIdType.MESH
NT = (((1,), (1,)), ((), ()))
NN = (((1,), (0,)), ((), ()))
TN = (((0,), (0,)), ((), ()))


def _params(*sem):
    return pltpu.CompilerParams(dimension_semantics=sem, vmem_limit_bytes=VMEM_LIMIT)


def _lane_lo(shape):
    return lax.broadcasted_iota(jnp.int32, shape, len(shape) - 1) < HEAD_DIM


def _pair(lo, a, b):
    return jnp.where(lo, a, b)


def _mm(a, b, mode, *, tm, tn, name, out_dtype=F32, res=None, layer=None, chunk=None, norm_gain=None, norm_bwd=None, rope=None):
    lead = () if layer is None else (layer,)
    nl = (None,) * len(lead)
    bs = b.shape[len(lead):]
    dims = {"nn": NN, "nt": NT, "tn": TN}[mode]
    reduce_n = 0
    if chunk is None:
        (m, k) = a.shape[::-1] if mode == "tn" else a.shape
        n = bs[0] if mode == "nt" else bs[1]
        grid = (m // tm, n // tn)
        a_spec = pl.BlockSpec((k, tm), lambda i, j: (0, i)) if mode == "tn" else pl.BlockSpec((tm, k), lambda i, j: (i, 0))
        b_spec = pl.BlockSpec(nl + ((tn, k) if mode == "nt" else (k, tn)), lambda i, j: lead + ((j, 0) if mode == "nt" else (0, j)))
        o_spec = pl.BlockSpec((tm, tn), lambda i, j: (i, j))
        out_shape = (m, n)
    elif chunk == "b":
        (m, k) = a.shape[::-1] if mode == "tn" else a.shape
        c, nc = bs[0], (bs[1] if mode == "nt" else bs[2])
        grid = (m // tm, c)
        a_spec = pl.BlockSpec((k, tm), lambda i, j: (0, i)) if mode == "tn" else pl.BlockSpec((tm, k), lambda i, j: (i, 0))
        b_spec = pl.BlockSpec(nl + (None,) + tuple(bs[1:]), lambda i, j: lead + (j, 0, 0))
        o_spec = pl.BlockSpec((None, tm, nc), lambda i, j: (j, i, 0))
        out_shape = (c, m, nc)
    elif chunk == "a":
        assert mode == "tn"
        c, k, mc = a.shape
        n = bs[1]
        grid = (c, n // tn)
        a_spec = pl.BlockSpec((None, k, mc), lambda i, j: (i, 0, 0))
        b_spec = pl.BlockSpec(nl + (k, tn), lambda i, j: lead + (0, j))
        o_spec = pl.BlockSpec((None, mc, tn), lambda i, j: (i, 0, j))
        out_shape = (c, mc, n)
    else:
        reduce_n, m, kc = a.shape
        n = bs[1] if mode == "nt" else bs[2]
        grid = (m // tm, n // tn)
        a_spec = pl.BlockSpec((reduce_n, tm, kc), lambda i, j: (0, i, 0))
        b_spec = pl.BlockSpec(nl + ((reduce_n, tn, kc) if mode == "nt" else (reduce_n, kc, tn)),
                              lambda i, j: lead + ((0, j, 0) if mode == "nt" else (0, 0, j)))
        o_spec = pl.BlockSpec((tm, tn), lambda i, j: (i, j))
        out_shape = (m, n)

    if norm_gain is not None or norm_bwd is not None:
        assert chunk in (None, "reduce") and tn == n, "the RMSNorm of the result needs whole rows in a block"

    def body(*refs):
        a_ref, b_ref = refs[0], refs[1]
        dot = lambda x, y: lax.dot_general(x.astype(BF16), y.astype(BF16), dims, preferred_element_type=F32)
        if reduce_n:
            acc = dot(a_ref[0], b_ref[0])
            for r in range(1, reduce_n):
                acc = acc + dot(a_ref[r], b_ref[r])
        else:
            acc = dot(a_ref[...], b_ref[...])
        if norm_bwd is not None:
            x_ref, g_ref, r_ref = refs[2:5]
            dx_ref, dxb_ref, dg_ref = refs[-3:]
            dx, dg = _rms_bwd_math(x_ref[...], acc, g_ref[...])
            dx = dx + r_ref[...]
            dx_ref[...] = dx
            dxb_ref[...] = dx.astype(BF16)

            @pl.when(pl.program_id(0) == 0)
            def _():
                dg_ref[...] = jnp.zeros_like(dg_ref)

            dg_ref[0:1, :] += dg
            return
        o_ref = refs[-2] if (norm_gain is not None or rope is not None) else refs[-1]
        if res is not None:
            acc = acc + refs[2][...]
        o_ref[...] = acc.astype(o_ref.dtype)
        if norm_gain is not None:
            rs = lax.rsqrt(jnp.mean(acc * acc, axis=-1, keepdims=True) + NORM_EPS)
            refs[-1][...] = (acc * rs * refs[3][...]).astype(BF16)
        if rope is not None:
            @pl.when(pl.program_id(1) < rope[2] // tn)
            def _():
                cos, sin = refs[2][...], refs[3][...]
                for c in range(tn // LANES):
                    lanes = slice(c * LANES, (c + 1) * LANES)
                    refs[-1][:, lanes] = _rope_tile(acc[:, lanes], cos, sin)

    ins = [a, b] + ([res] if res is not None else [])
    specs = [a_spec, b_spec] + ([o_spec] if res is not None else [])
    out_shapes, out_specs = jax.ShapeDtypeStruct(out_shape, out_dtype), o_spec
    sem = ("parallel", "parallel")
    if rope is not None:
        assert chunk is None and res is None and norm_gain is None and norm_bwd is None and rope[2] % tn == 0
        tab = pl.BlockSpec((tm, LANES), lambda i, j: (i, 0))
        ins += [rope[0], rope[1]]
        specs += [tab, tab]
        r_spec = pl.BlockSpec((tm, tn), lambda i, j: (i, jnp.minimum(j, rope[2] // tn - 1)))
        out_shapes, out_specs = (out_shapes, jax.ShapeDtypeStruct((m, rope[2]), F32)), (o_spec, r_spec)
        sem = ("parallel", "arbitrary")
    if norm_gain is not None:
        assert res is not None
        ins.append(norm_gain)
        specs.append(pl.BlockSpec((1, n), lambda i, j: (0, 0)))
        out_shapes, out_specs = (out_shapes, jax.ShapeDtypeStruct(out_shape, BF16)), (o_spec, o_spec)
    if norm_bwd is not None:
        assert res is None and norm_gain is None
        x_in, gain, resid = norm_bwd
        ins += [x_in, gain, resid]
        specs += [o_spec, pl.BlockSpec((1, n), lambda i, j: (0, 0)), o_spec]
        out_shapes = (jax.ShapeDtypeStruct(out_shape, F32), jax.ShapeDtypeStruct(out_shape, BF16), jax.ShapeDtypeStruct((8, n), F32))
        out_specs = (o_spec, o_spec, pl.BlockSpec((8, n), lambda i, j: (0, 0)))
        sem = ("arbitrary", "arbitrary")
    return pl.pallas_call(body, out_shape=out_shapes, grid=grid, in_specs=specs, out_specs=out_specs,
                          compiler_params=_params(*sem), name=name)(*ins)


def _rmsnorm_fwd(x, g, *, br, name):
    r, d = x.shape

    def body(x_ref, g_ref, o_ref):
        xf = x_ref[...]
        rs = lax.rsqrt(jnp.mean(xf * xf, axis=-1, keepdims=True) + NORM_EPS)
        o_ref[...] = (xf * rs * g_ref[...]).astype(BF16)

    return pl.pallas_call(body, out_shape=jax.ShapeDtypeStruct((r, d), BF16), grid=(r // br,),
                          in_specs=[pl.BlockSpec((br, d), lambda i: (i, 0)), pl.BlockSpec((1, d), lambda i: (0, 0))],
                          out_specs=pl.BlockSpec((br, d), lambda i: (i, 0)), compiler_params=_params("parallel"), name=name)(x, g)


def _rms_bwd_math(x, dy, g):
    d = x.shape[-1]
    rs = lax.rsqrt(jnp.mean(x * x, axis=-1, keepdims=True) + NORM_EPS)
    gy = dy * g
    proj = jnp.sum(x * gy, axis=-1, keepdims=True) * (1.0 / d)
    dx = rs * gy - x * (rs * rs * rs) * proj
    dg = jnp.sum(dy * (x * rs), axis=0, keepdims=True)
    return dx, dg


def _rmsnorm_bwd(x, dy, g, res, *, br, name):
    r, d = x.shape
    has_res = res is not None

    def body(*refs):
        x_ref, dy_ref, g_ref = refs[:3]
        dx_ref, dxb_ref, dg_ref = refs[-3:]
        dx, dg = _rms_bwd_math(x_ref[...], dy_ref[...], g_ref[...])
        if has_res:
            dx = dx + refs[3][...]
        dx_ref[...] = dx
        dxb_ref[...] = dx.astype(BF16)

        @pl.when(pl.program_id(0) == 0)
        def _():
            dg_ref[...] = jnp.zeros_like(dg_ref)

        dg_ref[0:1, :] += dg

    row = pl.BlockSpec((br, d), lambda i: (i, 0))
    ins = [x, dy, g] + ([res] if has_res else [])
    specs = [row, row, pl.BlockSpec((1, d), lambda i: (0, 0))] + ([row] if has_res else [])
    return pl.pallas_call(
        body, out_shape=(jax.ShapeDtypeStruct((r, d), F32), jax.ShapeDtypeStruct((r, d), BF16), jax.ShapeDtypeStruct((8, d), F32)),
        grid=(r // br,), in_specs=specs, out_specs=(row, row, pl.BlockSpec((8, d), lambda i: (0, 0))),
        compiler_params=_params("arbitrary"), name=name)(*ins)


def _loss_head(h, target, g, *, br, name):
    r, d = h.shape

    def body(x_ref, t_ref, g_ref, dx_ref, dxb_ref, dg_ref, loss_ref):
        x = x_ref[...]
        gg = g_ref[...]
        rs = lax.rsqrt(jnp.mean(x * x, axis=-1, keepdims=True) + NORM_EPS)
        err = x * rs * gg - t_ref[...]
        part = jnp.sum(jnp.sum(err * err, axis=1, keepdims=True), axis=0, keepdims=True) * (0.5 / d)
        dx, dg = _rms_bwd_math(x, err * (1.0 / d), gg)
        dx_ref[...] = dx
        dxb_ref[...] = dx.astype(BF16)

        @pl.when(pl.program_id(0) == 0)
        def _():
            dg_ref[...] = jnp.zeros_like(dg_ref)
            loss_ref[...] = jnp.zeros_like(loss_ref)

        dg_ref[0:1, :] += dg
        loss_ref[...] += jnp.broadcast_to(part, loss_ref.shape)

    row = pl.BlockSpec((br, d), lambda i: (i, 0))
    return pl.pallas_call(
        body, out_shape=(jax.ShapeDtypeStruct((r, d), F32), jax.ShapeDtypeStruct((r, d), BF16),
                         jax.ShapeDtypeStruct((8, d), F32), jax.ShapeDtypeStruct((8, LANES), F32)),
        grid=(r // br,), in_specs=[row, row, pl.BlockSpec((1, d), lambda i: (0, 0))],
        out_specs=(row, row, pl.BlockSpec((8, d), lambda i: (0, 0)), pl.BlockSpec((8, LANES), lambda i: (0, 0))),
        compiler_params=_params("arbitrary"), name=name)(h, target, g)


def _split3(x):
    hi = x.astype(BF16)
    r1 = x - hi.astype(F32)
    mid = r1.astype(BF16)
    lo = (r1 - mid.astype(F32)).astype(BF16)
    return hi, mid, lo


def _tri_sum(tri, x):
    hi, mid, lo = _split3(x)
    dot = lambda t: jnp.dot(tri, t, preferred_element_type=F32)
    return dot(hi) + dot(mid) + dot(lo)


def _forget_cumsum(proj, b_pad, *, name):
    s = proj.shape[0]
    blk = LANES

    def body(f_ref, b_ref, c_ref):
        ri = lax.broadcasted_iota(jnp.int32, (blk, blk), 0)
        ci = lax.broadcasted_iota(jnp.int32, (blk, blk), 1)
        tri = (ci <= ri).astype(BF16)
        bias = b_ref[...]

        def step(t, carry):
            rows = pl.ds(pl.multiple_of(t * blk, blk), blk)
            z = f_ref[rows, :] + bias
            lf = jnp.minimum(z, 0.0) - jnp.log(1.0 + jnp.exp(-jnp.abs(z)))
            cs = _tri_sum(tri, lf) + carry
            c_ref[rows, :] = cs
            return cs[blk - 1:blk, :]

        lax.fori_loop(0, s // blk, step, jnp.zeros((1, blk), F32))

    return pl.pallas_call(body, out_shape=jax.ShapeDtypeStruct((s, LANES), F32), grid=(1,),
                          in_specs=[pl.BlockSpec((s, LANES), lambda i: (0, F_COL)), pl.BlockSpec((1, LANES), lambda i: (0, 0))],
                          out_specs=pl.BlockSpec((s, LANES), lambda i: (0, 0)), compiler_params=_params("arbitrary"), name=name)(proj, b_pad)


def _forget_cumsum_bwd(proj, b_pad, dcq, dck, *, name):
    s = proj.shape[0]
    blk = LANES
    nblk = s // blk

    def body(f_ref, b_ref, dcq_ref, dck_ref, dz_ref, db_ref):
        ri = lax.broadcasted_iota(jnp.int32, (blk, blk), 0)
        ci = lax.broadcasted_iota(jnp.int32, (blk, blk), 1)
        triu = (ci >= ri).astype(BF16)
        bias = b_ref[...]

        def step(t, carry):
            tail, dbs = carry
            rows = pl.ds(pl.multiple_of((nblk - 1 - t) * blk, blk), blk)
            dc = dcq_ref[rows, :] - dck_ref[rows, :]
            dlf = _tri_sum(triu, dc) + tail
            z = f_ref[rows, :] + bias
            e = jnp.exp(-jnp.abs(z))
            sig_neg = jnp.where(z >= 0.0, e, 1.0) / (1.0 + e)
            dz = dlf * sig_neg
            dz_ref[rows, :] = dz.astype(BF16)
            return dlf[0:1, :], dbs + jnp.sum(dz, axis=0, keepdims=True)

        _, dbs = lax.fori_loop(0, nblk, step, (jnp.zeros((1, blk), F32), jnp.zeros((1, blk), F32)))
        db_ref[...] = jnp.broadcast_to(dbs, db_ref.shape)

    full = pl.BlockSpec((s, LANES), lambda i: (0, 0))
    return pl.pallas_call(body, out_shape=(jax.ShapeDtypeStruct((s, LANES), BF16), jax.ShapeDtypeStruct((8, LANES), F32)), grid=(1,),
                          in_specs=[pl.BlockSpec((s, LANES), lambda i: (0, F_COL)), pl.BlockSpec((1, LANES), lambda i: (0, 0)), full, full],
                          out_specs=(full, pl.BlockSpec((8, LANES), lambda i: (0, 0))), compiler_params=_params("arbitrary"), name=name)(proj, b_pad, dcq, dck)


def _attn_fwd(q_arr, kv_arr, ck6, *, q_col, k_col, v_col, n_hp, causal, bq, bk, name, heads=None, heads_col=0):
    s = q_arr.shape[0]
    skv = kv_arr.shape[0]
    bias = ck6 is not None
    nq = s // bq
    assert not causal or bq == bk

    def body(*refs):
        q_ref, k_ref, v_ref = refs[:3]
        ck_ref = refs[3] if bias else None
        o_ref, lse_ref, heads_ref = refs[-3:]
        i = pl.program_id(1)
        lo = _lane_lo((bq, LANES))
        q = q_ref[...] * SCALE
        qh = (jnp.where(lo, q, 0.0).astype(BF16), jnp.where(lo, 0.0, q).astype(BF16))

        def block(j, carry, diagonal):
            ks = pl.ds(pl.multiple_of(j * bk, bk), bk)
            k = k_ref[ks, :].astype(BF16)
            v = v_ref[ks, :].astype(BF16)
            if diagonal:
                ok = lax.broadcasted_iota(jnp.int32, (bq, bk), 1) <= lax.broadcasted_iota(jnp.int32, (bq, bk), 0)
            out = []
            for h in range(2):
                m, l, acc = carry[3 * h:3 * h + 3]
                sc = lax.dot_general(qh[h], k, NT, preferred_element_type=F32)
                if bias:
                    sc = sc - ck_ref[0, h:h + 1, ks]
                if diagonal:
                    sc = jnp.where(ok, sc, NEG)
                mn = jnp.maximum(m, jnp.max(sc, axis=1, keepdims=True))
                p = jnp.exp(sc - mn)
                al = jnp.exp(m - mn)
                out += [mn, al * l + jnp.sum(p, axis=1, keepdims=True), al * acc + jnp.dot(p.astype(BF16), v, preferred_element_type=F32)]
            return tuple(out)

        col = lambda v_: jnp.full((bq, 1), v_, F32)
        init = (col(NEG), col(0.0), jnp.zeros((bq, LANES), F32)) * 2
        n_full = i if causal else skv // bk
        carry = lax.fori_loop(0, n_full, functools.partial(block, diagonal=False), init)
        if causal:
            carry = block(i, carry, True)
        m0, l0, a0, m1, l1, a1 = carry
        out = _pair(lo, a0 / l0, a1 / l1)
        o_ref[...] = out
        lse_ref[0] = _pair(lo, m0 + jnp.log(l0), m1 + jnp.log(l1))
        heads_ref[...] = out.astype(BF16)

    specs = [pl.BlockSpec((bq, LANES), lambda h, i: (i, q_col + h)),
             pl.BlockSpec((skv, LANES), lambda h, i: (0, k_col + h)),
             pl.BlockSpec((skv, LANES), lambda h, i: (0, v_col + h))]
    ins = [q_arr, kv_arr, kv_arr]
    if bias:
        specs += [pl.BlockSpec((1, 8, skv), lambda h, i: (h, 0, 0))]
        ins += [ck6]
    aliases = {}
    if heads is not None:
        aliases = {len(ins): 2}
        specs += [pl.BlockSpec(memory_space=pl.ANY)]
        ins += [heads]
    return pl.pallas_call(
        body, out_shape=(jax.ShapeDtypeStruct((s, n_hp * LANES), F32), jax.ShapeDtypeStruct((n_hp, s, LANES), F32),
                         jax.ShapeDtypeStruct((s, D_MODEL), BF16)),
        grid=(n_hp, nq), in_specs=specs,
        out_specs=(pl.BlockSpec((bq, LANES), lambda h, i: (i, h)), pl.BlockSpec((1, bq, LANES), lambda h, i: (h, i, 0)),
                   pl.BlockSpec((bq, LANES), lambda h, i: (i, heads_col + h))),
        input_output_aliases=aliases, compiler_params=_params("parallel", "parallel"), name=name)(*ins)


def _attn_bwd(q_arr, kv_arr, o_arr, do_arr, lse, ck6, *, q_col, k_col, v_col, o_col, n_hp, causal, bq, bk, name):
    s = q_arr.shape[0]
    skv = kv_arr.shape[0]
    bias = ck6 is not None
    nq = s // bq
    assert not causal or bq == bk

    def body(*refs):
        q_ref, k_ref, v_ref, o_ref, do_ref, lse_ref = refs[:6]
        if bias:
            ck_ref = refs[6]
            dq_ref, dk_ref, dv_ref, dcq_ref, dck_ref = refs[-5:]
        else:
            dq_ref, dk_ref, dv_ref = refs[-3:]
        j = pl.program_id(1)
        lo_q = _lane_lo((bq, LANES))
        lo_k = _lane_lo((bk, LANES))
        k = k_ref[...]
        v = v_ref[...].astype(BF16)
        kb = k.astype(BF16)
        kh = (jnp.where(lo_k, k, 0.0).astype(BF16), jnp.where(lo_k, 0.0, k).astype(BF16))
        if bias:
            pick_k = [(lax.broadcasted_iota(jnp.int32, (8, bk), 0) == h).astype(BF16) for h in range(2)]
            pick_q = [(lax.broadcasted_iota(jnp.int32, (8, bq), 0) == h).astype(BF16) for h in range(2)]

        @pl.when(j == 0)
        def _():
            dq_ref[...] = jnp.zeros_like(dq_ref)
            if bias:
                dcq_ref[...] = jnp.zeros_like(dcq_ref)

        def block(i, carry, diagonal):
            dk_acc, dv_acc, cs = carry
            qs = pl.ds(pl.multiple_of(i * bq, bq), bq)
            q = q_ref[qs, :] * SCALE
            do = do_ref[qs, :]
            dd = do * o_ref[qs, :]
            lse_i = lse_ref[0, qs, :]
            qh = (jnp.where(lo_q, q, 0.0).astype(BF16), jnp.where(lo_q, 0.0, q).astype(BF16))
            doh = (jnp.where(lo_q, do, 0.0).astype(BF16), jnp.where(lo_q, 0.0, do).astype(BF16))
            dh = (jnp.sum(jnp.where(lo_q, dd, 0.0), axis=1, keepdims=True), jnp.sum(jnp.where(lo_q, 0.0, dd), axis=1, keepdims=True))
            if diagonal:
                ok = lax.broadcasted_iota(jnp.int32, (bq, bk), 1) <= lax.broadcasted_iota(jnp.int32, (bq, bk), 0)
            dq_blk = None
            rs = None
            for h in range(2):
                sc = lax.dot_general(qh[h], kb, NT, preferred_element_type=F32)
                if bias:
                    sc = sc - ck_ref[0, h:h + 1, :]
                if diagonal:
                    sc = jnp.where(ok, sc, NEG)
                p = jnp.exp(sc - lse_i[:, h * HEAD_DIM:h * HEAD_DIM + 1])
                ds = p * (lax.dot_general(doh[h], v, NT, preferred_element_type=F32) - dh[h])
                dsb = ds.astype(BF16)
                dv_acc = dv_acc + lax.dot_general(p.astype(BF16), doh[h], TN, preferred_element_type=F32)
                dk_acc = dk_acc + lax.dot_general(dsb, qh[h], TN, preferred_element_type=F32)
                part = jnp.dot(dsb, kh[h], preferred_element_type=F32)
                dq_blk = part if dq_blk is None else dq_blk + part
                if bias:
                    cs = cs + jnp.dot(pick_q[h], dsb, preferred_element_type=F32)
                    row_sums = lax.dot_general(pick_k[h], dsb, NT, preferred_element_type=F32)
                    rs = row_sums if rs is None else rs + row_sums
            dq_ref[qs, :] += dq_blk * SCALE
            if bias:
                dcq_ref[0, :, qs] += rs
            return dk_acc, dv_acc, cs

        carry = (jnp.zeros((bk, LANES), F32), jnp.zeros((bk, LANES), F32), jnp.zeros((8, bk), F32))
        if causal:
            carry = block(j, carry, True)
        dk_acc, dv_acc, cs = lax.fori_loop(j + 1 if causal else 0, nq, functools.partial(block, diagonal=False), carry)
        dk_ref[...] = dk_acc.astype(BF16)
        dv_ref[...] = dv_acc.astype(BF16)
        if bias:
            dck_ref[0] = cs

    full_q = lambda c: pl.BlockSpec((s, LANES), lambda h, j: (0, c + h))
    specs = [full_q(q_col),
             pl.BlockSpec((bk, LANES), lambda h, j: (j, k_col + h)),
             pl.BlockSpec((bk, LANES), lambda h, j: (j, v_col + h)),
             full_q(0), full_q(o_col),
             pl.BlockSpec((1, s, LANES), lambda h, j: (h, 0, 0))]
    ins = [q_arr, kv_arr, kv_arr, o_arr, do_arr, lse]
    out_shape = [jax.ShapeDtypeStruct((s, n_hp * LANES), F32), jax.ShapeDtypeStruct((skv, n_hp * LANES), BF16),
                 jax.ShapeDtypeStruct((skv, n_hp * LANES), BF16)]
    out_specs = [full_q(0), pl.BlockSpec((bk, LANES), lambda h, j: (j, h)), pl.BlockSpec((bk, LANES), lambda h, j: (j, h))]
    if bias:
        specs += [pl.BlockSpec((1, 8, bk), lambda h, j: (h, 0, j))]
        ins += [ck6]
        out_shape += [jax.ShapeDtypeStruct((n_hp, 8, s), F32), jax.ShapeDtypeStruct((n_hp, 8, skv), F32)]
        out_specs += [pl.BlockSpec((1, 8, s), lambda h, j: (h, 0, 0)), pl.BlockSpec((1, 8, bk), lambda h, j: (h, 0, j))]
    return pl.pallas_call(body, out_shape=tuple(out_shape), grid=(n_hp, skv // bk), in_specs=specs, out_specs=tuple(out_specs),
                          compiler_params=_params("parallel", "arbitrary"), name=name)(*ins)


def _rope_tables(s):
    inv = 1.0 / (ROPE_THETA ** (jnp.arange(0, HEAD_DIM, 2, dtype=F32) / HEAD_DIM))
    ang = jnp.arange(s, dtype=F32)[:, None] * inv[None, :]
    cos, sin = jnp.cos(ang), jnp.sin(ang)
    return jnp.tile(cos, (1, 4)), jnp.concatenate([-sin, sin, -sin, sin], axis=1)


def _rope_tile(x, cos, sin):
    first = (lax.broadcasted_iota(jnp.int32, x.shape, 1) % HEAD_DIM) < (HEAD_DIM // 2)
    swapped = jnp.where(first, pltpu.roll(x, LANES - HEAD_DIM // 2, 1), pltpu.roll(x, HEAD_DIM // 2, 1))
    return x * cos + swapped * sin


def _rope(x_arr, cos_t, sin_t, *, n_cols, out_dtype, br, name):
    s = x_arr.shape[0]

    def body(x_ref, c_ref, s_ref, o_ref):
        cos, sin = c_ref[...], s_ref[...]
        for j in range(n_cols):
            lanes = slice(j * LANES, (j + 1) * LANES)
            o_ref[:, lanes] = _rope_tile(x_ref[:, lanes].astype(F32), cos, sin).astype(o_ref.dtype)

    tab = pl.BlockSpec((br, LANES), lambda i: (i, 0))
    blk = pl.BlockSpec((br, n_cols * LANES), lambda i: (i, 0))
    return pl.pallas_call(body, out_shape=jax.ShapeDtypeStruct((s, n_cols * LANES), out_dtype), grid=(s // br,),
                          in_specs=[blk, tab, tab], out_specs=blk, compiler_params=_params("parallel"), name=name)(x_arr, cos_t, sin_t)


def _stack_heads(x):
    lo = _lane_lo(x.shape)
    return jnp.concatenate([jnp.where(lo, x, 0.0), jnp.where(lo, 0.0, x)], axis=0).astype(BF16)


def _unstack_heads(x):
    return jnp.where(_lane_lo((DIL_L, LANES)), x[:DIL_L], x[DIL_L:])


def _dil_scores(q_ref, k_ref, cur, prev, has_prev):
    qs = _stack_heads(q_ref[cur, :] * SCALE)
    kk = jnp.concatenate([k_ref[prev, :], k_ref[cur, :]], axis=0).astype(BF16)
    a = lax.broadcasted_iota(jnp.int32, (2 * DIL_L, 2 * DIL_L), 0) & (DIL_L - 1)
    c = lax.broadcasted_iota(jnp.int32, (2 * DIL_L, 2 * DIL_L), 1)
    ok = ((c < DIL_L) & (c >= a) & has_prev) | ((c >= DIL_L) & (c - DIL_L <= a))
    return qs, kk, jnp.where(ok, lax.dot_general(qs, kk, NT, preferred_element_type=F32), NEG)


def _dil_rows(t, dil):
    r, m = t % dil, t // dil
    start = m * (DIL_L * dil) + r
    prev = jnp.maximum(start - DIL_L * dil, 0)
    return pl.ds(start, DIL_L, stride=dil), pl.ds(prev, DIL_L, stride=dil), m > 0


def _softmax3(a, b, c):
    m = jnp.maximum(jnp.maximum(a, b), c)
    ea, eb, ec = jnp.exp(a - m), jnp.exp(b - m), jnp.exp(c - m)
    den = ea + eb + ec
    inv = 1.0 / den
    return ea * inv, eb * inv, ec * inv, m + jnp.log(den)


def _dil_fwd(qk_r, proj, *, name):
    s = qk_r.shape[0]
    nsub = s // DIL_L
    mb = 512

    def body(q_ref, k_ref, v_ref, mix_ref, l1_ref, l2_ref, l3_ref, heads_ref, o1_scr, o2_scr, o3_scr):
        for (_, dil), o_scr, l_ref in zip(DILATED_BRANCHES, (o1_scr, o2_scr, o3_scr), (l1_ref, l2_ref, l3_ref)):
            def step(t, carry, dil=dil, o_scr=o_scr, l_ref=l_ref):
                cur, prev, has_prev = _dil_rows(t, dil)
                _, _, sc = _dil_scores(q_ref, k_ref, cur, prev, has_prev)
                vv = jnp.concatenate([v_ref[prev, :], v_ref[cur, :]], axis=0).astype(BF16)
                m = jnp.max(sc, axis=1, keepdims=True)
                e = jnp.exp(sc - m)
                den = jnp.sum(e, axis=1, keepdims=True)
                o = jnp.dot((e * (1.0 / den)).astype(BF16), vv, preferred_element_type=F32)
                o_scr[cur, :] = _unstack_heads(o)
                l_ref[cur, :] = _unstack_heads(jnp.broadcast_to(m + jnp.log(den), (2 * DIL_L, LANES)))
                return carry

            lax.fori_loop(0, nsub, step, 0, unroll=min(DIL_UNROLL_FWD, nsub))

        def merge(i, carry):
            rows = pl.ds(pl.multiple_of(i * mb, mb), mb)
            wa, wb, wc, _ = _softmax3(l1_ref[rows, :], l2_ref[rows, :], l3_ref[rows, :])
            mix = wa * o1_scr[rows, :] + wb * o2_scr[rows, :] + wc * o3_scr[rows, :]
            mix_ref[rows, :] = mix
            heads_ref[rows, :] = mix.astype(BF16)
            return carry

        lax.fori_loop(0, s // mb, merge, 0)

    col = lambda arr_col: pl.BlockSpec((s, LANES), lambda h: (0, arr_col + h))
    shp = jax.ShapeDtypeStruct((s, D_MIX), F32)
    mix, l1, l2, l3, heads = pl.pallas_call(
        body, out_shape=(shp, shp, shp, shp, jax.ShapeDtypeStruct((s, D_MODEL), BF16)), grid=(N_MIX_HP,),
        in_specs=[col(0), col(N_MIX_HP), col(2 * N_MIX_HP)],
        out_specs=(col(0),) * 5, scratch_shapes=[pltpu.VMEM((s, LANES), F32)] * 3,
        compiler_params=_params("parallel"), name=name)(qk_r, qk_r, proj)
    return mix, (l1, l2, l3), heads


def _dil_bwd(qk_r, proj, mix, dheads, lses, *, name):
    s = qk_r.shape[0]
    nsub = s // DIL_L
    mb = 512

    def body(q_ref, k_ref, v_ref, mix_ref, dm_ref, l1_ref, l2_ref, l3_ref, dq_ref, dk_ref, dv_ref, lt_scr, dd_scr):
        lo = _lane_lo((DIL_L, LANES))
        lo_m = _lane_lo((mb, LANES))

        def prep(i, carry):
            rows = pl.ds(pl.multiple_of(i * mb, mb), mb)
            _, _, _, lt = _softmax3(l1_ref[rows, :], l2_ref[rows, :], l3_ref[rows, :])
            lt_scr[rows, :] = lt
            dd = dm_ref[rows, :] * mix_ref[rows, :]
            dd_scr[rows, :] = _pair(lo_m, jnp.sum(jnp.where(lo_m, dd, 0.0), axis=1, keepdims=True),
                                    jnp.sum(jnp.where(lo_m, 0.0, dd), axis=1, keepdims=True))
            zero = jnp.zeros((mb, LANES), F32)
            dq_ref[rows, :] = zero
            dk_ref[rows, :] = zero
            dv_ref[rows, :] = zero
            return carry

        lax.fori_loop(0, s // mb, prep, 0)

        for (_, dil), l_ref in zip(DILATED_BRANCHES, (l1_ref, l2_ref, l3_ref)):
            def step(t, carry, dil=dil, l_ref=l_ref):
                cur, prev, has_prev = _dil_rows(t, dil)
                qs, kk, sc = _dil_scores(q_ref, k_ref, cur, prev, has_prev)
                vv = jnp.concatenate([v_ref[prev, :], v_ref[cur, :]], axis=0).astype(BF16)
                lg = l_ref[cur, :]
                w = jnp.exp(lg - lt_scr[cur, :])
                wd = w * dd_scr[cur, :]
                column = lambda x: jnp.concatenate([x[:, 0:1], x[:, HEAD_DIM:HEAD_DIM + 1]], axis=0)
                dos = _stack_heads(w * dm_ref[cur, :])
                p = jnp.exp(sc - column(lg))
                ds = (p * (lax.dot_general(dos, vv, NT, preferred_element_type=F32) - column(wd))).astype(BF16)
                dq_ref[cur, :] += _unstack_heads(jnp.dot(ds, kk, preferred_element_type=F32)) * SCALE
                dkk = lax.dot_general(ds, qs, TN, preferred_element_type=F32)
                dvv = lax.dot_general(p.astype(BF16), dos, TN, preferred_element_type=F32)
                dk_ref[cur, :] += dkk[DIL_L:]
                dv_ref[cur, :] += dvv[DIL_L:]
                dk_ref[prev, :] += dkk[:DIL_L]
                dv_ref[prev, :] += dvv[:DIL_L]
                return carry

            lax.fori_loop(0, nsub, step, 0, unroll=min(DIL_UNROLL_BWD, nsub))

    col = lambda arr_col: pl.BlockSpec((s, LANES), lambda h: (0, arr_col + h))
    shp = jax.ShapeDtypeStruct((s, D_MIX), F32)
    return pl.pallas_call(
        body, out_shape=(shp, shp, shp), grid=(N_MIX_HP,),
        in_specs=[col(0), col(N_MIX_HP), col(2 * N_MIX_HP), col(0), col(0), col(0), col(0), col(0)], out_specs=(col(0),) * 3,
        scratch_shapes=[pltpu.VMEM((s, LANES), F32)] * 2,
        compiler_params=_params("parallel"), name=name)(qk_r, qk_r, proj, mix, dheads, *lses)


CONV_BR = 512
FF_CHUNK = 2 * D_FF // N_DEV
FF_HALF = N_DEV // 2
HALO = 8


def _shift_down(x, halo, k):
    row = lax.broadcasted_iota(jnp.int32, x.shape, 0)
    y = pltpu.roll(x, k, 0)
    for r in range(k):
        y = jnp.where(row == r, halo[HALO - k + r:HALO - k + r + 1, :], y)
    return y


def _shift_up(x, halo, k):
    n = x.shape[0]
    row = lax.broadcasted_iota(jnp.int32, x.shape, 0)
    y = pltpu.roll(x, n - k, 0)
    for r in range(k):
        y = jnp.where(row == n - k + r, halo[r:r + 1, :], y)
    return y


def _conv_vals(u, halo, w, b):
    s1 = _shift_down(u, halo, 1)
    s2 = _shift_down(u, halo, 2)
    return b + w[0:1, :] * s2 + w[1:2, :] * s1 + w[2:3, :] * u, s1, s2


def _conv_in_specs(order, layer):
    rc = (lambda i, j: (i, j)) if order == "rc" else (lambda j, i: (i, j))
    per = CONV_BR // HALO
    main = lambda off: pl.BlockSpec((None, CONV_BR, FF_CHUNK), lambda *g: (off + rc(*g)[1], rc(*g)[0], 0))
    halo = lambda off: pl.BlockSpec((None, HALO, FF_CHUNK), lambda *g: (off + rc(*g)[1], jnp.maximum(rc(*g)[0] * per - 1, 0), 0))
    wspec = lambda off: pl.BlockSpec((None, None, 3, FF_CHUNK), lambda *g: (layer, off + rc(*g)[1], 0, 0))
    bspec = lambda off: pl.BlockSpec((None, 1, FF_CHUNK), lambda *g: (off + rc(*g)[1], 0, 0))
    return [main(0), halo(0), main(FF_HALF), halo(FF_HALF), wspec(0), wspec(FF_HALF), bspec(0), bspec(FF_HALF)]


def _conv_fwd(u, cw, cb, layer, *, name):
    s = u.shape[1]

    def body(uv_ref, hv_ref, ug_ref, hg_ref, wv_ref, wg_ref, bv_ref, bg_ref, o_ref):
        first = pl.program_id(0) == 0
        hv = jnp.where(first, 0.0, hv_ref[...])
        hg = jnp.where(first, 0.0, hg_ref[...])
        val, _, _ = _conv_vals(uv_ref[...], hv, wv_ref[...], bv_ref[...])
        gate, _, _ = _conv_vals(ug_ref[...], hg, wg_ref[...], bg_ref[...])
        o_ref[...] = (gate / (1.0 + jnp.exp(-gate)) * val).astype(BF16)

    return pl.pallas_call(body, out_shape=jax.ShapeDtypeStruct((FF_HALF, s, FF_CHUNK), BF16), grid=(s // CONV_BR, FF_HALF),
                          in_specs=_conv_in_specs("rc", layer), out_specs=pl.BlockSpec((None, CONV_BR, FF_CHUNK), lambda i, j: (j, i, 0)),
                          compiler_params=_params("parallel", "parallel"), name=name)(u, u, u, u, cw, cw, cb, cb)


def _swiglu_bwd(val, gate, da):
    sg = 1.0 / (1.0 + jnp.exp(-gate))
    return da * (gate * sg), da * val * (sg * (1.0 + gate * (1.0 - sg)))


def _conv_bwd(u, cw, cb, da, layer, *, name):
    s = u.shape[1]
    nrow = s // CONV_BR
    per = CONV_BR // HALO

    def body(uv_ref, hv_ref, ug_ref, hg_ref, wv_ref, wg_ref, bv_ref, bg_ref, da_ref, nv_ref, ng_ref, nda_ref, du_ref, dwb_ref):
        i = pl.program_id(1)
        first, last = i == 0, i == nrow - 1
        hv = jnp.where(first, 0.0, hv_ref[...])
        hg = jnp.where(first, 0.0, hg_ref[...])
        uv, ug = uv_ref[...], ug_ref[...]
        wv, wg, bv, bg = wv_ref[...], wg_ref[...], bv_ref[...], bg_ref[...]
        val, v1, v2 = _conv_vals(uv, hv, wv, bv)
        gate, g1, g2 = _conv_vals(ug, hg, wg, bg)
        dval, dgate = _swiglu_bwd(val, gate, da_ref[...])
        val_n, _, _ = _conv_vals(nv_ref[...], uv[CONV_BR - HALO:, :], wv, bv)
        gate_n, _, _ = _conv_vals(ng_ref[...], ug[CONV_BR - HALO:, :], wg, bg)
        dval_n, dgate_n = _swiglu_bwd(val_n, gate_n, nda_ref[...])
        dval_n = jnp.where(last, 0.0, dval_n)
        dgate_n = jnp.where(last, 0.0, dgate_n)
        back = lambda dc, dc_n, w: w[2:3, :] * dc + w[1:2, :] * _shift_up(dc, dc_n, 1) + w[0:1, :] * _shift_up(dc, dc_n, 2)
        du_ref[0] = back(dval, dval_n, wv).astype(BF16)
        du_ref[1] = back(dgate, dgate_n, wg).astype(BF16)

        @pl.when(first)
        def _():
            dwb_ref[...] = jnp.zeros_like(dwb_ref)

        cs = lambda t: jnp.sum(t, axis=0, keepdims=True)
        r8 = lax.broadcasted_iota(jnp.int32, (8, FF_CHUNK), 0)
        rows4 = lambda a, b, c, d: jnp.where(r8 == 0, a, jnp.where(r8 == 1, b, jnp.where(r8 == 2, c, jnp.where(r8 == 3, d, 0.0))))
        dwb_ref[0] += rows4(cs(dval * v2), cs(dval * v1), cs(dval * uv), cs(dval))
        dwb_ref[1] += rows4(cs(dgate * g2), cs(dgate * g1), cs(dgate * ug), cs(dgate))

    nxt = lambda off: pl.BlockSpec((None, HALO, FF_CHUNK), lambda j, i: (off + j, jnp.minimum((i + 1) * per, nrow * per - 1), 0))
    specs = _conv_in_specs("cr", layer) + [pl.BlockSpec((None, CONV_BR, FF_CHUNK), lambda j, i: (j, i, 0)), nxt(0), nxt(FF_HALF), nxt(0)]
    return pl.pallas_call(
        body, out_shape=(jax.ShapeDtypeStruct((2, FF_HALF, s, FF_CHUNK), BF16), jax.ShapeDtypeStruct((2, FF_HALF, 8, FF_CHUNK), F32)),
        grid=(FF_HALF, nrow), in_specs=specs,
        out_specs=(pl.BlockSpec((2, None, CONV_BR, FF_CHUNK), lambda j, i: (0, j, i, 0)),
                   pl.BlockSpec((2, None, 8, FF_CHUNK), lambda j, i: (0, j, 0, 0))),
        compiler_params=_params("parallel", "arbitrary"), name=name)(u, u, u, u, cw, cw, cb, cb, da, u, u, da)


def _rows_of(r):
    return lambda ref, idx: ref.at[:, pl.ds(idx * r, r), :]


def _slot1(ref, idx):
    return ref.at[:, idx]


def _slot0(ref, idx):
    return ref.at[idx]


def _all_gather(shards, full_shapes, places, *, name):
    n = len(shards)

    def body(*refs):
        ins, outs = refs[:n], refs[n:2 * n]
        send_sems, recv_sems, local_sems = refs[2 * n:]
        mx, my, mc = lax.axis_index("x"), lax.axis_index("y"), lax.axis_index("c")
        me, sibling = (mx, my, mc), (mx, my, 1 - mc)
        chips = [(1 - mx, my), (mx, 1 - my), (1 - mx, 1 - my)]

        def win(t, px, py, pc):
            return places[t](outs[t], 4 * px + 2 * py + pc)

        def copy(t, k, block, to, src=None):
            return pltpu.make_async_remote_copy(src_ref=win(t, *block) if src is None else src, dst_ref=win(t, *block),
                                                send_sem=send_sems.at[t, k], recv_sem=recv_sems.at[t, k], device_id=to, device_id_type=MESH)

        mine = [pltpu.make_async_copy(ins[t], win(t, *me), local_sems.at[t]) for t in range(n)]
        for cp in mine:
            cp.start()
        first = []
        for t in range(n):
            first += [copy(t, 0, me, sibling, src=ins[t])] + [copy(t, 1 + j, me, (*chip, mc), src=ins[t]) for j, chip in enumerate(chips)]
        for cp in first:
            cp.start()
        passed = []
        for j, chip in enumerate(chips):
            for t in range(n):
                copy(t, 1 + j, (*chip, mc), me).wait_recv()
                fwd = copy(t, 4 + j, (*chip, mc), sibling)
                fwd.start()
                passed.append(fwd)
        for t in range(n):
            copy(t, 0, sibling, me).wait_recv()
            for j, chip in enumerate(chips):
                copy(t, 4 + j, (*chip, 1 - mc), me).wait_recv()
        for cp in first + passed:
            cp.wait_send()
        for cp in mine:
            cp.wait()

    hbm = pl.BlockSpec(memory_space=pl.ANY)
    return pl.pallas_call(
        body, out_shape=tuple(jax.ShapeDtypeStruct(s, x.dtype) for s, x in zip(full_shapes, shards)),
        in_specs=[hbm] * n, out_specs=(hbm,) * n,
        scratch_shapes=[pltpu.SemaphoreType.DMA((n, 7)), pltpu.SemaphoreType.DMA((n, 7)), pltpu.SemaphoreType.DMA((n,))],
        name=name)(*shards)


FLIPS = [(fx, fy, fc) for fx in (0, 1) for fy in (0, 1) for fc in (0, 1)][1:]


def _exchange_copies(kind, places, src, land, send_sems, recv_sems, local_sems):
    mx, my, mc = lax.axis_index("x"), lax.axis_index("y"), lax.axis_index("c")
    me = 4 * mx + 2 * my + mc
    n = len(src)
    local, remote = [], []
    for t in range(n):
        if kind == "gather":
            local.append(pltpu.make_async_copy(src[t], places[t](land[t], me), local_sems.at[t]))
        else:
            local.append(pltpu.make_async_copy(places[t](src[t], me), land[t].at[me], local_sems.at[t]))
    for k, (fx, fy, fc) in enumerate(FLIPS):
        px, py, pc = mx ^ fx, my ^ fy, mc ^ fc
        peer = 4 * px + 2 * py + pc
        for t in range(n):
            sems = dict(send_sem=send_sems.at[7 * t + k], recv_sem=recv_sems.at[7 * t + k], device_id=(px, py, pc), device_id_type=MESH)
            if kind == "gather":
                pair = [(src[t], places[t](land[t], me)), (src[t], places[t](land[t], peer))]
            else:
                pair = [(places[t](src[t], peer), land[t].at[me]), (places[t](src[t], peer), land[t].at[peer])]
            remote.append([functools.partial(pltpu.make_async_remote_copy, src_ref=s_, dst_ref=d_, **sems) for s_, d_ in pair])
    return local, remote


HBM_SPEC = pl.BlockSpec(memory_space=pltpu.HBM)
SEM_SPEC = pl.BlockSpec(memory_space=pltpu.SEMAPHORE)
SIDE_EFFECT = pltpu.SideEffectType.DATAFLOW_SIDE_EFFECTING


def _exchange_start(kind, srcs, land_shapes, places, after, *, name):
    n = len(srcs)

    def body(*refs):
        src, land = refs[:n], refs[n:2 * n]
        send_sems, recv_sems, local_sems = refs[2 * n + 1:2 * n + 4]
        token = refs[-1]
        local, remote = _exchange_copies(kind, places, src, land, send_sems, recv_sems, local_sems)
        for cp in local:
            cp.start()
        for send, _ in remote:
            send().start()
        token[...] = jnp.zeros_like(token)

    hbm = lambda t: pltpu.with_memory_space_constraint(t, pltpu.HBM)
    lands = [hbm(lax.empty(tuple(s), x.dtype)) for s, x in zip(land_shapes, srcs)]
    out_shape = (pltpu.SemaphoreType.DMA((7 * n,)), pltpu.SemaphoreType.DMA((7 * n,)), pltpu.SemaphoreType.DMA((n,)),
                 *[pltpu.HBM(x.shape, x.dtype) for x in srcs], *[pltpu.HBM(tuple(s), x.dtype) for s, x in zip(land_shapes, srcs)],
                 jax.ShapeDtypeStruct((8, LANES), F32))
    outs = pl.pallas_call(
        body, name=name, out_shape=out_shape, in_specs=[HBM_SPEC] * (2 * n) + [pl.BlockSpec(memory_space=pl.ANY)],
        out_specs=(SEM_SPEC, SEM_SPEC, SEM_SPEC) + (HBM_SPEC,) * (2 * n) + (pl.BlockSpec(memory_space=pltpu.VMEM),),
        input_output_aliases={i: 3 + i for i in range(2 * n)},
        compiler_params=pltpu.CompilerParams(has_side_effects=SIDE_EFFECT))(*[hbm(x) for x in srcs], *lands, after)
    return dict(sems=outs[:3], src=outs[3:3 + n], land=outs[3 + n:3 + 2 * n], token=outs[-1])


def _exchange_wait(kind, started, places, after, *, name):
    n = len(started["src"])

    def body(*refs):
        src, land = refs[:n], refs[n:2 * n]
        send_sems, recv_sems, local_sems = refs[2 * n:2 * n + 3]
        local, remote = _exchange_copies(kind, places, src, land, send_sems, recv_sems, local_sems)
        for cp in local:
            cp.wait()
        for send, arrival in remote:
            send().wait_send()
            arrival().wait_recv()

    out_shape = tuple(pltpu.HBM(x.shape, x.dtype) for x in started["src"]) + tuple(pltpu.HBM(x.shape, x.dtype) for x in started["land"])
    outs = pl.pallas_call(
        body, name=name, out_shape=out_shape,
        in_specs=[HBM_SPEC] * (2 * n) + [SEM_SPEC] * 3 + [pl.BlockSpec(memory_space=pl.ANY)], out_specs=(HBM_SPEC,) * (2 * n),
        input_output_aliases={i: i for i in range(2 * n)},
        compiler_params=pltpu.CompilerParams(has_side_effects=SIDE_EFFECT))(*started["src"], *started["land"], *started["sems"], after)
    return list(outs[n:])


def _adamw(parts, w, m, v, *, br, name):
    layers, r, wd = w.shape
    assert len(parts) == layers

    def body(*refs):
        p_refs = refs[:layers]
        w_ref, m_ref, v_ref, g_ref, d_ref, nm_ref, nv_ref = refs[layers:]
        for k in range(layers):
            @pl.when(pl.program_id(0) == k)
            def _(p_ref=p_refs[k]):
                g = p_ref[0].astype(F32)
                for dev in range(1, N_DEV):
                    g = g + p_ref[dev].astype(F32)
                mm = ADAM_B1 * m_ref[...] + (1.0 - ADAM_B1) * g
                vv = ADAM_B2 * v_ref[...] + (1.0 - ADAM_B2) * (g * g)
                m_hat = mm / (1.0 - ADAM_B1 ** ADAM_STEP)
                v_hat = vv / (1.0 - ADAM_B2 ** ADAM_STEP)
                g_ref[...] = g
                d_ref[...] = -ADAM_LR * (m_hat / (jnp.sqrt(v_hat) + ADAM_EPS) + ADAM_WD * w_ref[...])
                nm_ref[...] = mm
                nv_ref[...] = vv

    p_spec = lambda k: pl.BlockSpec((N_DEV, None, br, wd), lambda l, i: (0, 0, jnp.where(l == k, i, 0), 0))
    blk = pl.BlockSpec((None, br, wd), lambda l, i: (l, i, 0))
    shp = jax.ShapeDtypeStruct((layers, r, wd), F32)
    return pl.pallas_call(body, out_shape=(shp, shp, shp, shp), grid=(layers, r // br),
                          in_specs=[p_spec(k) for k in range(layers)] + [blk, blk, blk], out_specs=(blk, blk, blk, blk),
                          compiler_params=_params("arbitrary", "arbitrary"), name=name)(*parts, w, m, v)


SMALL = ("norm_mix", "norm_mem", "norm_ffn", "b_forget", "conv_b", "norm_final")


def _pack(tensors):
    flat = jnp.concatenate([t.reshape(-1) for t in tensors])
    rows = -(-flat.shape[0] // (PACK_W * PACK_ROW_ALIGN)) * PACK_ROW_ALIGN
    flat = jnp.pad(flat, (0, rows * PACK_W - flat.shape[0]))
    return flat.reshape(1, rows, PACK_W)


def _unpack(buf, shapes):
    flat = buf.reshape(-1)
    out, off = [], 0
    for shp in shapes:
        n = math.prod(shp)
        out.append(flat[off:off + n].reshape(tuple(shp)))
        off += n
    return out


def _fox_permute(w):
    pad = jnp.zeros(w.shape[:-1] + (FOX_P - FOX_IN,), w.dtype)
    return jnp.concatenate([w[..., :3 * D_MIX], w[..., 3 * D_MIX + N_MIX_HEADS:], w[..., 3 * D_MIX:3 * D_MIX + N_MIX_HEADS], pad], axis=-1)


def _fox_unpermute(w):
    return jnp.concatenate([w[..., :3 * D_MIX], w[..., DIL_IN:DIL_IN + N_MIX_HEADS], w[..., 3 * D_MIX:DIL_IN]], axis=-1)


def _bias_layout(c):
    s = c.shape[0]
    ct = c[:, :N_MIX_HEADS].T.reshape(N_MIX_HP, 2, s)
    return jnp.pad(ct, ((0, 0), (0, 6), (0, 0)))


def _bias_grad(dck6):
    s = dck6.shape[2]
    dk = dck6[:, :2, :].reshape(N_MIX_HEADS, s).T
    return jnp.pad(dk, ((0, 0), (0, LANES - N_MIX_HEADS)))


def _device_step(x, mem, target, small, get_weights, put_grads):
    s = x.shape[0]
    mt = mem.shape[0]
    bq = 512
    cos_t, sin_t = _rope_tables(s)
    row = lambda t, l: t[l][None, :]
    saved = []
    h = x
    cb8 = small["conv_b"].reshape(DEPTH, N_DEV, 1, FF_CHUNK)
    for l in range(DEPTH):
        kind, slot = l % 2, l // 2
        wl = dict(get_weights(l, "attn", h))
        if l == 0:
            xn = _rmsnorm_fwd(h, row(small["norm_mix"], l), br=512, name=f"norm_mix_fwd{l}")
        mn = _rmsnorm_fwd(mem, row(small["norm_mem"], l), br=mt, name=f"norm_mem_fwd{l}")
        if kind == 0:
            proj = _mm(xn, wl["w_in"], "nn", tm=1024, tn=384, layer=0, name=f"in_proj{l}")
        else:
            proj, qk_r = _mm(xn, wl["w_in"], "nt", tm=1024, tn=512, layer=0, rope=(cos_t, sin_t, 2 * D_MIX), name=f"in_proj{l}")
        kvm = _mm(mn, wl["w_mem_kv"], "nn", tm=mt, tn=512, layer=0, name=f"mem_kv{l}")
        st = dict(h=h, xn=xn, mn=mn, proj=proj, kvm=kvm, w=wl)
        if kind == 0:
            b_pad = jnp.pad(small["b_forget"][slot], (0, LANES - N_MIX_HEADS))[None, :]
            c = _forget_cumsum(proj, b_pad, name=f"forget_cumsum{l}")
            ck6 = _bias_layout(c)
            mix, lse, heads = _attn_fwd(proj, proj, ck6, q_col=0, k_col=N_MIX_HP, v_col=2 * N_MIX_HP, n_hp=N_MIX_HP,
                                        causal=True, bq=min(s, 1024), bk=min(s, 1024), name=f"fox_fwd{l}")
            st.update(b_pad=b_pad, ck6=ck6, mix=mix, lse=lse)
        else:
            mix, lses, heads = _dil_fwd(qk_r, proj, name=f"dil_fwd{l}")
            st.update(qk_r=qk_r, lses=lses, mix=mix)
        mo, lse_m, heads = _attn_fwd(proj, kvm, None, q_col=QM_COL, k_col=0, v_col=N_MEM_HP, n_hp=N_MEM_HP, causal=False,
                                     bq=min(s, 2048), bk=mt, heads=heads, heads_col=N_MIX_HP, name=f"mem_fwd{l}")
        h1, xf = _mm(heads, wl["w_out"], "nn", tm=1024, tn=D_MODEL, res=h, layer=0, norm_gain=row(small["norm_ffn"], l),
                     name=f"out_proj{l}")
        wl.update(get_weights(l, "ffn", xf))
        u = _mm(xf, wl["w_up"], "nt", tm=1024, tn=FF_CHUNK, layer=0, chunk="b", name=f"up_proj{l}")
        a = _conv_fwd(u, wl["conv_w"], cb8[l], 0, name=f"conv_fwd{l}")
        st.update(mo=mo, lse_m=lse_m, heads=heads, h1=h1, xf=xf, u=u, a=a)
        saved.append(st)
        if l + 1 < DEPTH:
            h, xn = _mm(a, wl["w_down"], "nn", tm=512, tn=D_MODEL, res=h1, layer=0, chunk="reduce",
                        norm_gain=row(small["norm_mix"], l + 1), name=f"down_proj{l}")
        else:
            h = _mm(a, wl["w_down"], "nn", tm=1024, tn=512, res=h1, layer=0, chunk="reduce", name=f"down_proj{l}")

    dh, dhb, dg_final, loss = _loss_head(h, target, small["norm_final"][None, :], br=512, name="loss_head")
    gs = {k: [None] * DEPTH for k in ("norm_mix", "norm_mem", "norm_ffn", "conv_b")}
    gs["b_forget"] = [None] * 2
    dep = 0.0
    for l in reversed(range(DEPTH)):
        st = saved[l]
        wl = st["w"]
        gw = {}
        kind, slot = l % 2, l // 2
        da = _mm(dhb, wl["w_down"], "nt", tm=1024, tn=FF_CHUNK, layer=0, chunk="b", name=f"down_dx{l}")
        gw["w_down"] = _mm(st["a"], dhb, "tn", tm=FF_CHUNK, tn=512, out_dtype=BF16, chunk="a", name=f"down_dw{l}")
        du, dwb = _conv_bwd(st["u"], wl["conv_w"], cb8[l] + dep, da, 0, name=f"conv_bwd{l}")
        du = du.reshape(N_DEV, s, FF_CHUNK)
        dwb = dwb.reshape(N_DEV, 8, FF_CHUNK)
        gw["conv_w"] = dwb
        gs["conv_b"][l] = dwb[:, 3, :].reshape(-1)
        dh1, dh1b, dgf = _mm(du, wl["w_up"], "nn", tm=256, tn=D_MODEL, layer=0, chunk="reduce",
                             norm_bwd=(st["h1"], row(small["norm_ffn"], l), dh), name=f"up_dx{l}")
        gw["w_up"] = _mm(du, st["xf"], "tn", tm=FF_CHUNK, tn=512, out_dtype=BF16, chunk="a", name=f"up_dw{l}")
        gs["norm_ffn"][l] = dgf[0]
        dheads = _mm(dh1b, wl["w_out"], "nt", tm=1024, tn=512, layer=0, name=f"out_dx{l}")
        gw["w_out"] = _mm(st["heads"], dh1b, "tn", tm=512, tn=512, out_dtype=BF16, name=f"out_dw{l}")
        dqm, dkm, dvm = _attn_bwd(st["proj"], st["kvm"], st["mo"], dheads, st["lse_m"], None, q_col=QM_COL, k_col=0,
                                  v_col=N_MEM_HP, o_col=N_MIX_HP, n_hp=N_MEM_HP, causal=False, bq=min(s, 1024), bk=mt,
                                  name=f"mem_bwd{l}")
        dkvm = jnp.concatenate([dkm, dvm], axis=1)
        gw["w_mem_kv"] = _mm(st["mn"], dkvm, "tn", tm=512, tn=512, out_dtype=BF16, name=f"mem_kv_dw{l}")
        dmn = _mm(dkvm, wl["w_mem_kv"], "nt", tm=mt, tn=512, layer=0, name=f"mem_kv_dx{l}")
        dep_early = put_grads(l, "ffn", gw)
        _, _, dgm = _rmsnorm_bwd(mem, dmn, row(small["norm_mem"], l), None, br=mt, name=f"norm_mem_bwd{l}")
        gs["norm_mem"][l] = dgm[0]
        if kind == 0:
            dq, dk, dv, dcq6, dck6 = _attn_bwd(st["proj"], st["proj"], st["mix"], dheads, st["lse"], st["ck6"] + dep_early, q_col=0,
                                               k_col=N_MIX_HP, v_col=2 * N_MIX_HP, o_col=0, n_hp=N_MIX_HP, causal=True,
                                               bq=bq, bk=bq, name=f"fox_bwd{l}")
            dz, db = _forget_cumsum_bwd(st["proj"], st["b_pad"], _bias_grad(dcq6), _bias_grad(dck6), name=f"forget_cumsum_bwd{l}")
            gs["b_forget"][slot] = db[0, :N_MIX_HEADS]
            dproj = jnp.concatenate([dq.astype(BF16), dk, dv, dqm.astype(BF16), dz], axis=1)
        else:
            dq_r, dk_r, dv = _dil_bwd(st["qk_r"], st["proj"], st["mix"], dheads, st["lses"], name=f"dil_bwd{l}")
            dq = _rope(dq_r, cos_t + dep_early, -sin_t, n_cols=N_MIX_HP, out_dtype=BF16, br=512, name=f"rope_bwd_q{l}")
            dk = _rope(dk_r, cos_t, -sin_t, n_cols=N_MIX_HP, out_dtype=BF16, br=512, name=f"rope_bwd_k{l}")
            dproj = jnp.concatenate([dq, dk, dv.astype(BF16), dqm.astype(BF16)], axis=1)
        if kind == 0:
            gw["w_in"] = _mm(st["xn"], dproj, "tn", tm=512, tn=384, out_dtype=BF16, name=f"in_dw{l}")
        else:
            gw["w_in"] = _mm(dproj, st["xn"], "tn", tm=512, tn=512, out_dtype=BF16, name=f"in_dw{l}")
        dh, dhb, dgx = _mm(dproj, wl["w_in"], "nt" if kind == 0 else "nn", tm=512, tn=D_MODEL, layer=0,
                           norm_bwd=(st["h"], row(small["norm_mix"], l), dh1), name=f"in_dx{l}")
        gs["norm_mix"][l] = dgx[0]
        dep = put_grads(l, "attn", gw)

    grads_s = {k: jnp.stack(v) for k, v in gs.items()}
    grads_s["norm_final"] = dg_final[0]
    return loss[0, 0], dh, grads_s


def kernel(x, mem, norm_mix, norm_mem, norm_ffn, w_in_fox, b_forget, w_in_dil, w_mem_kv, w_out, w_up, conv_w, conv_b, w_down, norm_final, loss_target, m_norm_mix, m_norm_mem, m_norm_ffn, m_w_in_fox, m_b_forget, m_w_in_dil, m_w_mem_kv, m_w_out, m_w_up, m_conv_w, m_conv_b, m_w_down, m_norm_final, v_norm_mix, v_norm_mem, v_norm_ffn, v_w_in_fox, v_b_forget, v_w_in_dil, v_w_mem_kv, v_w_out, v_w_up, v_conv_w, v_conv_b, v_w_down, v_norm_final):
    names = ["norm_mix", "norm_mem", "norm_ffn", "w_in_fox", "b_forget", "w_in_dil", "w_mem_kv", "w_out", "w_up", "conv_w", "conv_b",
             "w_down", "norm_final"]
    w = dict(zip(names, (norm_mix, norm_mem, norm_ffn, w_in_fox, b_forget, w_in_dil, w_mem_kv, w_out, w_up, conv_w, conv_b, w_down, norm_final)))
    m = dict(zip(names, (m_norm_mix, m_norm_mem, m_norm_ffn, m_w_in_fox, m_b_forget, m_w_in_dil, m_w_mem_kv, m_w_out, m_w_up, m_conv_w,
                         m_conv_b, m_w_down, m_norm_final)))
    v = dict(zip(names, (v_norm_mix, v_norm_mem, v_norm_ffn, v_w_in_fox, v_b_forget, v_w_in_dil, v_w_mem_kv, v_w_out, v_w_up, v_conv_w,
                         v_conv_b, v_w_down, v_norm_final)))
    big = ("w_in_fox", "w_in_dil", "w_mem_kv", "w_out", "w_up", "w_down", "conv_w")
    small_shapes = [w[k].shape for k in SMALL]
    dil_c = w_in_dil.shape[2]
    rows = {k: w[k].shape[1] for k in ("w_in_fox", "w_mem_kv", "w_out", "w_down")}

    def places(l):
        w_in_place = _rows_of(rows["w_in_fox"]) if l % 2 == 0 else _slot1
        return [w_in_place, _rows_of(rows["w_mem_kv"]), _rows_of(rows["w_out"]), _slot1, _rows_of(rows["w_down"]), _slot1]

    def full_shapes(l):
        w_in_shape = (1, D_MODEL, FOX_P) if l % 2 == 0 else (1, N_DEV, dil_c, D_MODEL)
        return [w_in_shape, (1, D_MODEL, 2 * D_MEMQ), (1, D_MODEL, D_MODEL), (1, N_DEV, FF_CHUNK, D_MODEL), (1, D_FF, D_MODEL),
                (1, N_DEV, 3, FF_CHUNK)]

    transposed = lambda t: jnp.swapaxes(t, 1, 2)
    to_wire = {"w_in_fox": lambda t: _fox_permute(t).astype(BF16), "w_in_dil": lambda t: t.T.astype(BF16),
               "w_mem_kv": lambda t: t.astype(BF16), "w_out": lambda t: t.astype(BF16), "w_up": lambda t: t.T.astype(BF16),
               "w_down": lambda t: t.astype(BF16), "conv_w": lambda t: t}
    shard = lambda k, i: to_wire[k](w[k][i])
    shard_shape = lambda k: jax.eval_shape(lambda: shard(k, 0)).shape
    everything = (0, 1, 2, 3, 4, 5)
    gather_groups = {l: ((0, 1, 2), (3, 4, 5)) if l == 0 else (everything,) for l in range(DEPTH)}
    scatter_groups = {l: ((1, 2, 3, 4, 5), (0,)) if l == 0 else (everything,) for l in range(DEPTH)}
    pick = lambda seq, group: [seq[i] for i in group]
    tag = lambda l, group: f"{l}" + ("" if group == everything else "_" + "".join(str(i) for i in group))

    gathers, after = {}, norm_final
    for l in range(DEPTH):
        w_in_shard = shard("w_in_fox" if l % 2 == 0 else "w_in_dil", l // 2)[None]
        shards = [w_in_shard] + [shard(k, l)[None] for k in ("w_mem_kv", "w_out", "w_up", "w_down", "conv_w")]
        for group in gather_groups[l]:
            gathers[l, group] = _exchange_start("gather", pick(shards, group), pick(full_shapes(l), group), pick(places(l), group), after,
                                                name=f"weights_gather_start{tag(l, group)}")
            after = gathers[l, group]["token"]
    started = sum(g["token"][0, 0] for g in gathers.values())
    small = {k: w[k] for k in SMALL}
    small["norm_mix"] = norm_mix + started
    landed = {}

    def get_weights(l, part, h):
        group = [g for g in gather_groups[l] if (0 if part == "attn" else 3) in g][0]
        if (l, group) not in landed:
            lands = _exchange_wait("gather", gathers[l, group], pick(places(l), group), h, name=f"weights_gather_wait{tag(l, group)}")
            landed[l, group] = dict(zip(group, lands))
        got = landed[l, group]
        if part == "ffn":
            return dict(w_up=got[3], w_down=got[4].reshape(1, FF_HALF, FF_CHUNK, D_MODEL), conv_w=got[5])
        w_in = got[0] if l % 2 == 0 else got[0].reshape(1, N_DEV * dil_c, D_MODEL)
        return dict(w_in=w_in, w_mem_kv=got[1], w_out=got[2])

    scatters, pending = {}, {}

    def put_grads(l, part, g):
        pending.setdefault(l, {}).update(g)
        if part == "ffn" and len(scatter_groups[l]) == 1:
            return 0.0
        group = scatter_groups[l][0 if part == "ffn" else -1]
        have = pending[l]
        srcs = {3: lambda: have["w_up"][None], 4: lambda: have["w_down"].reshape(1, D_FF, D_MODEL), 5: lambda: have["conv_w"][None],
                1: lambda: have["w_mem_kv"][None], 2: lambda: have["w_out"][None]}
        if l % 2 == 0:
            srcs[0] = lambda: have["w_in"][None]
        else:
            srcs[0] = lambda: have["w_in"].reshape(1, N_DEV, dil_c, D_MODEL)
        shard_shapes = [shard_shape("w_in_fox" if l % 2 == 0 else "w_in_dil")] + \
            [shard_shape(k) for k in ("w_mem_kv", "w_out", "w_up", "w_down")] + [(8, FF_CHUNK)]
        sources = [srcs[i]() for i in group]
        scatters[l, group] = _exchange_start("scatter", sources, [(N_DEV, 1) + tuple(s) for s in pick(shard_shapes, group)],
                                             pick(places(l), group), sources[0], name=f"grads_scatter_start{tag(l, group)}")
        return scatters[l, group]["token"][0, 0]

    loss, grad_x, gs = _device_step(x[0], mem[0], loss_target[0], small, get_weights, put_grads)

    recv = {}

    def wait_scatter(l, group, after_):
        lands = _exchange_wait("scatter", scatters[l, group], pick(places(l), group), after_, name=f"grads_scatter_wait{tag(l, group)}")
        recv.setdefault(l, {}).update(zip(group, lands))

    for l in reversed(range(1, DEPTH)):
        wait_scatter(l, everything, grad_x)
    wait_scatter(0, scatter_groups[0][0], grad_x)
    s_pack = _pack([gs[k] for k in SMALL])
    (s_recv,) = _all_gather([s_pack], [(N_DEV,) + s_pack.shape], [_slot0], name="small_grads_all_gather")

    layer_tensors = ("w_in", "w_mem_kv", "w_out", "w_up", "w_down", "conv_w")
    layer_parts = lambda k: [recv[l][layer_tensors.index(k)] for l in range(DEPTH)]
    to_local = {k: (lambda t: t) for k in big}
    to_local["w_in_fox"] = _fox_permute
    to_local["w_in_dil"] = to_local["w_up"] = transposed
    from_local = {k: (lambda t: t) for k in big}
    from_local["w_in_fox"] = _fox_unpermute
    from_local["w_in_dil"] = from_local["w_up"] = transposed
    blocks = {"w_in_fox": rows["w_in_fox"], "w_in_dil": dil_c, "w_mem_kv": rows["w_mem_kv"], "w_out": rows["w_out"], "w_up": FF_CHUNK // 4,
              "w_down": rows["w_down"] // 2, "conv_w": 3}
    outs = {}

    def update(k, parts):
        f = to_local[k]
        outs[k] = [from_local[k](t) for t in _adamw(parts, f(w[k]), f(m[k]), f(v[k]), br=blocks[k], name=f"adamw_{k}")]

    update("w_in_dil", [recv[l][0] for l in range(1, DEPTH, 2)])
    update("w_up", layer_parts("w_up"))
    update("w_down", layer_parts("w_down"))
    update("conv_w", [p[:, :, :3, :] for p in layer_parts("conv_w")])
    update("w_mem_kv", layer_parts("w_mem_kv"))
    update("w_out", layer_parts("w_out"))
    wait_scatter(0, scatter_groups[0][-1], outs["w_out"][1])
    update("w_in_fox", [recv[l][0] for l in range(0, DEPTH, 2)])
    small_outs = _adamw([s_recv], _pack([w[k] for k in SMALL]), _pack([m[k] for k in SMALL]), _pack([v[k] for k in SMALL]),
                        br=s_pack.shape[1], name="adamw_small")
    res = []
    for i, os_ in enumerate(small_outs):
        d = {k: outs[k][i] for k in big}
        d.update(zip(SMALL, _unpack(os_, small_shapes)))
        res.append([d[k] for k in names])
    loss = lax.psum(loss, ("x", "y", "c"))
    return (loss, grad_x[None], *res[0], *res[1], *res[2], *res[3])
```

```python
import functools
import math

import jax
import jax.numpy as jnp
from jax import lax
from jax.experimental import pallas as pl
from jax.experimental.pallas import tpu as pltpu

F32 = jnp.float32
BF16 = jnp.bfloat16

D_MODEL = 1024
HEAD_DIM = 64
N_MIX_HEADS = 12
N_MEM_HEADS = 4
D_MIX = N_MIX_HEADS * HEAD_DIM
D_MEMQ = N_MEM_HEADS * HEAD_DIM
D_FF = 2816
DEPTH = 4
FOX_IN = 3 * D_MIX + N_MIX_HEADS + D_MEMQ
DIL_IN = 3 * D_MIX + D_MEMQ
LANES = 128
FOX_P = DIL_IN + LANES
N_MIX_HP = D_MIX // LANES
N_MEM_HP = D_MEMQ // LANES
QM_COL = 3 * N_MIX_HP
F_COL = DIL_IN // LANES
DILATED_BRANCHES = ((128, 1), (512, 4), (2048, 16))
DIL_L = 128
DIL_UNROLL_FWD = 32
DIL_UNROLL_BWD = 8
ROPE_THETA = 10000.0
NORM_EPS = 1e-6
NEG = -1e30
SCALE = HEAD_DIM ** -0.5
N_DEV = 8

ADAM_LR = 0.001
ADAM_B1 = 0.9
ADAM_B2 = 0.999
ADAM_EPS = 1e-08
ADAM_WD = 0.01
ADAM_STEP = 10

VMEM_LIMIT = 56 * 1024 * 1024
PACK_W = 1024
PACK_ROW_ALIGN = 8

MESH = pl.DeviceIdType.MESH
NT = (((1,), (1,)), ((), ()))
NN = (((1,), (0,)), ((), ()))
TN = (((0,), (0,)), ((), ()))


def _params(*sem):
    return pltpu.CompilerParams(dimension_semantics=sem, vmem_limit_bytes=VMEM_LIMIT)


def _lane_lo(shape):
    return lax.broadcasted_iota(jnp.int32, shape, len(shape) - 1) < HEAD_DIM


def _pair(lo, a, b):
    return jnp.where(lo, a, b)


def _mm(a, b, mode, *, tm, tn, name, out_dtype=F32, res=None, layer=None, chunk=None, norm_gain=None, norm_bwd=None, rope=None):
    lead = () if layer is None else (layer,)
    nl = (None,) * len(lead)
    bs = b.shape[len(lead):]
    dims = {"nn": NN, "nt": NT, "tn": TN}[mode]
    reduce_n = 0
    if chunk is None:
        (m, k) = a.shape[::-1] if mode == "tn" else a.shape
        n = bs[0] if mode == "nt" else bs[1]
        grid = (m // tm, n // tn)
        a_spec = pl.BlockSpec((k, tm), lambda i, j: (0, i)) if mode == "tn" else pl.BlockSpec((tm, k), lambda i, j: (i, 0))
        b_spec = pl.BlockSpec(nl + ((tn, k) if mode == "nt" else (k, tn)), lambda i, j: lead + ((j, 0) if mode == "nt" else (0, j)))
        o_spec = pl.BlockSpec((tm, tn), lambda i, j: (i, j))
        out_shape = (m, n)
    elif chunk == "b":
        (m, k) = a.shape[::-1] if mode == "tn" else a.shape
        c, nc = bs[0], (bs[1] if mode == "nt" else bs[2])
        grid = (m // tm, c)
        a_spec = pl.BlockSpec((k, tm), lambda i, j: (0, i)) if mode == "tn" else pl.BlockSpec((tm, k), lambda i, j: (i, 0))
        b_spec = pl.BlockSpec(nl + (None,) + tuple(bs[1:]), lambda i, j: lead + (j, 0, 0))
        o_spec = pl.BlockSpec((None, tm, nc), lambda i, j: (j, i, 0))
        out_shape = (c, m, nc)
    elif chunk == "a":
        assert mode == "tn"
        c, k, mc = a.shape
        n = bs[1]
        grid = (c, n // tn)
        a_spec = pl.BlockSpec((None, k, mc), lambda i, j: (i, 0, 0))
        b_spec = pl.BlockSpec(nl + (k, tn), lambda i, j: lead + (0, j))
        o_spec = pl.BlockSpec((None, mc, tn), lambda i, j: (i, 0, j))
        out_shape = (c, mc, n)
    else:
        reduce_n, m, kc = a.shape
        n = bs[1] if mode == "nt" else bs[2]
        grid = (m // tm, n // tn)
        a_spec = pl.BlockSpec((reduce_n, tm, kc), lambda i, j: (0, i, 0))
        b_spec = pl.BlockSpec(nl + ((reduce_n, tn, kc) if mode == "nt" else (reduce_n, kc, tn)),
                              lambda i, j: lead + ((0, j, 0) if mode == "nt" else (0, 0, j)))
        o_spec = pl.BlockSpec((tm, tn), lambda i, j: (i, j))
        out_shape = (m, n)

    if norm_gain is not None or norm_bwd is not None:
        assert chunk in (None, "reduce") and tn == n, "the RMSNorm of the result needs whole rows in a block"

    def body(*refs):
        a_ref, b_ref = refs[0], refs[1]
        dot = lambda x, y: lax.dot_general(x.astype(BF16), y.astype(BF16), dims, preferred_element_type=F32)
        if reduce_n:
            acc = dot(a_ref[0], b_ref[0])
            for r in range(1, reduce_n):
                acc = acc + dot(a_ref[r], b_ref[r])
        else:
            acc = dot(a_ref[...], b_ref[...])
        if norm_bwd is not None:
            x_ref, g_ref, r_ref = refs[2:5]
            dx_ref, dxb_ref, dg_ref = refs[-3:]
            dx, dg = _rms_bwd_math(x_ref[...], acc, g_ref[...])
            dx = dx + r_ref[...]
            dx_ref[...] = dx
            dxb_ref[...] = dx.astype(BF16)

            @pl.when(pl.program_id(0) == 0)
            def _():
                dg_ref[...] = jnp.zeros_like(dg_ref)

            dg_ref[0:1, :] += dg
            return
        o_ref = refs[-2] if (norm_gain is not None or rope is not None) else refs[-1]
        if res is not None:
            acc = acc + refs[2][...]
        o_ref[...] = acc.astype(o_ref.dtype)
        if norm_gain is not None:
            rs = lax.rsqrt(jnp.mean(acc * acc, axis=-1, keepdims=True) + NORM_EPS)
            refs[-1][...] = (acc * rs * refs[3][...]).astype(BF16)
        if rope is not None:
            @pl.when(pl.program_id(1) < rope[2] // tn)
            def _():
                cos, sin = refs[2][...], refs[3][...]
                for c in range(tn // LANES):
                    lanes = slice(c * LANES, (c + 1) * LANES)
                    refs[-1][:, lanes] = _rope_tile(acc[:, lanes], cos, sin)

    ins = [a, b] + ([res] if res is not None else [])
    specs = [a_spec, b_spec] + ([o_spec] if res is not None else [])
    out_shapes, out_specs = jax.ShapeDtypeStruct(out_shape, out_dtype), o_spec
    sem = ("parallel", "parallel")
    if rope is not None:
        assert chunk is None and res is None and norm_gain is None and norm_bwd is None and rope[2] % tn == 0
        tab = pl.BlockSpec((tm, LANES), lambda i, j: (i, 0))
        ins += [rope[0], rope[1]]
        specs += [tab, tab]
        r_spec = pl.BlockSpec((tm, tn), lambda i, j: (i, jnp.minimum(j, rope[2] // tn - 1)))
        out_shapes, out_specs = (out_shapes, jax.ShapeDtypeStruct((m, rope[2]), F32)), (o_spec, r_spec)
        sem = ("parallel", "arbitrary")
    if norm_gain is not None:
        assert res is not None
        ins.append(norm_gain)
        specs.append(pl.BlockSpec((1, n), lambda i, j: (0, 0)))
        out_shapes, out_specs = (out_shapes, jax.ShapeDtypeStruct(out_shape, BF16)), (o_spec, o_spec)
    if norm_bwd is not None:
        assert res is None and norm_gain is None
        x_in, gain, resid = norm_bwd
        ins += [x_in, gain, resid]
        specs += [o_spec, pl.BlockSpec((1, n), lambda i, j: (0, 0)), o_spec]
        out_shapes = (jax.ShapeDtypeStruct(out_shape, F32), jax.ShapeDtypeStruct(out_shape, BF16), jax.ShapeDtypeStruct((8, n), F32))
        out_specs = (o_spec, o_spec, pl.BlockSpec((8, n), lambda i, j: (0, 0)))
        sem = ("arbitrary", "arbitrary")
    return pl.pallas_call(body, out_shape=out_shapes, grid=grid, in_specs=specs, out_specs=out_specs,
                          compiler_params=_params(*sem), name=name)(*ins)


def _rmsnorm_fwd(x, g, *, br, name):
    r, d = x.shape

    def body(x_ref, g_ref, o_ref):
        xf = x_ref[...]
        rs = lax.rsqrt(jnp.mean(xf * xf, axis=-1, keepdims=True) + NORM_EPS)
        o_ref[...] = (xf * rs * g_ref[...]).astype(BF16)

    return pl.pallas_call(body, out_shape=jax.ShapeDtypeStruct((r, d), BF16), grid=(r // br,),
                          in_specs=[pl.BlockSpec((br, d), lambda i: (i, 0)), pl.BlockSpec((1, d), lambda i: (0, 0))],
                          out_specs=pl.BlockSpec((br, d), lambda i: (i, 0)), compiler_params=_params("parallel"), name=name)(x, g)


def _rms_bwd_math(x, dy, g):
    d = x.shape[-1]
    rs = lax.rsqrt(jnp.mean(x * x, axis=-1, keepdims=True) + NORM_EPS)
    gy = dy * g
    proj = jnp.sum(x * gy, axis=-1, keepdims=True) * (1.0 / d)
    dx = rs * gy - x * (rs * rs * rs) * proj
    dg = jnp.sum(dy * (x * rs), axis=0, keepdims=True)
    return dx, dg


def _rmsnorm_bwd(x, dy, g, res, *, br, name):
    r, d = x.shape
    has_res = res is not None

    def body(*refs):
        x_ref, dy_ref, g_ref = refs[:3]
        dx_ref, dxb_ref, dg_ref = refs[-3:]
        dx, dg = _rms_bwd_math(x_ref[...], dy_ref[...], g_ref[...])
        if has_res:
            dx = dx + refs[3][...]
        dx_ref[...] = dx
        dxb_ref[...] = dx.astype(BF16)

        @pl.when(pl.program_id(0) == 0)
        def _():
            dg_ref[...] = jnp.zeros_like(dg_ref)

        dg_ref[0:1, :] += dg

    row = pl.BlockSpec((br, d), lambda i: (i, 0))
    ins = [x, dy, g] + ([res] if has_res else [])
    specs = [row, row, pl.BlockSpec((1, d), lambda i: (0, 0))] + ([row] if has_res else [])
    return pl.pallas_call(
        body, out_shape=(jax.ShapeDtypeStruct((r, d), F32), jax.ShapeDtypeStruct((r, d), BF16), jax.ShapeDtypeStruct((8, d), F32)),
        grid=(r // br,), in_specs=specs, out_specs=(row, row, pl.BlockSpec((8, d), lambda i: (0, 0))),
        compiler_params=_params("arbitrary"), name=name)(*ins)


def _loss_head(h, target, g, *, br, name):
    r, d = h.shape

    def body(x_ref, t_ref, g_ref, dx_ref, dxb_ref, dg_ref, loss_ref):
        x = x_ref[...]
        gg = g_ref[...]
        rs = lax.rsqrt(jnp.mean(x * x, axis=-1, keepdims=True) + NORM_EPS)
        err = x * rs * gg - t_ref[...]
        part = jnp.sum(jnp.sum(err * err, axis=1, keepdims=True), axis=0, keepdims=True) * (0.5 / d)
        dx, dg = _rms_bwd_math(x, err * (1.0 / d), gg)
        dx_ref[...] = dx
        dxb_ref[...] = dx.astype(BF16)

        @pl.when(pl.program_id(0) == 0)
        def _():
            dg_ref[...] = jnp.zeros_like(dg_ref)
            loss_ref[...] = jnp.zeros_like(loss_ref)

        dg_ref[0:1, :] += dg
        loss_ref[...] += jnp.broadcast_to(part, loss_ref.shape)

    row = pl.BlockSpec((br, d), lambda i: (i, 0))
    return pl.pallas_call(
        body, out_shape=(jax.ShapeDtypeStruct((r, d), F32), jax.ShapeDtypeStruct((r, d), BF16),
                         jax.ShapeDtypeStruct((8, d), F32), jax.ShapeDtypeStruct((8, LANES), F32)),
        grid=(r // br,), in_specs=[row, row, pl.BlockSpec((1, d), lambda i: (0, 0))],
        out_specs=(row, row, pl.BlockSpec((8, d), lambda i: (0, 0)), pl.BlockSpec((8, LANES), lambda i: (0, 0))),
        compiler_params=_params("arbitrary"), name=name)(h, target, g)


def _split3(x):
    hi = x.astype(BF16)
    r1 = x - hi.astype(F32)
    mid = r1.astype(BF16)
    lo = (r1 - mid.astype(F32)).astype(BF16)
    return hi, mid, lo


def _tri_sum(tri, x):
    hi, mid, lo = _split3(x)
    dot = lambda t: jnp.dot(tri, t, preferred_element_type=F32)
    return dot(hi) + dot(mid) + dot(lo)


def _forget_cumsum(proj, b_pad, *, name):
    s = proj.shape[0]
    blk = LANES

    def body(f_ref, b_ref, c_ref):
        ri = lax.broadcasted_iota(jnp.int32, (blk, blk), 0)
        ci = lax.broadcasted_iota(jnp.int32, (blk, blk), 1)
        tri = (ci <= ri).astype(BF16)
        bias = b_ref[...]

        def step(t, carry):
            rows = pl.ds(pl.multiple_of(t * blk, blk), blk)
            z = f_ref[rows, :] + bias
            lf = jnp.minimum(z, 0.0) - jnp.log(1.0 + jnp.exp(-jnp.abs(z)))
            cs = _tri_sum(tri, lf) + carry
            c_ref[rows, :] = cs
            return cs[blk - 1:blk, :]

        lax.fori_loop(0, s // blk, step, jnp.zeros((1, blk), F32))

    return pl.pallas_call(body, out_shape=jax.ShapeDtypeStruct((s, LANES), F32), grid=(1,),
                          in_specs=[pl.BlockSpec((s, LANES), lambda i: (0, F_COL)), pl.BlockSpec((1, LANES), lambda i: (0, 0))],
                          out_specs=pl.BlockSpec((s, LANES), lambda i: (0, 0)), compiler_params=_params("arbitrary"), name=name)(proj, b_pad)


def _forget_cumsum_bwd(proj, b_pad, dcq, dck, *, name):
    s = proj.shape[0]
    blk = LANES
    nblk = s // blk

    def body(f_ref, b_ref, dcq_ref, dck_ref, dz_ref, db_ref):
        ri = lax.broadcasted_iota(jnp.int32, (blk, blk), 0)
        ci = lax.broadcasted_iota(jnp.int32, (blk, blk), 1)
        triu = (ci >= ri).astype(BF16)
        bias = b_ref[...]

        def step(t, carry):
            tail, dbs = carry
            rows = pl.ds(pl.multiple_of((nblk - 1 - t) * blk, blk), blk)
            dc = dcq_ref[rows, :] - dck_ref[rows, :]
            dlf = _tri_sum(triu, dc) + tail
            z = f_ref[rows, :] + bias
            e = jnp.exp(-jnp.abs(z))
            sig_neg = jnp.where(z >= 0.0, e, 1.0) / (1.0 + e)
            dz = dlf * sig_neg
            dz_ref[rows, :] = dz.astype(BF16)
            return dlf[0:1, :], dbs + jnp.sum(dz, axis=0, keepdims=True)

        _, dbs = lax.fori_loop(0, nblk, step, (jnp.zeros((1, blk), F32), jnp.zeros((1, blk), F32)))
        db_ref[...] = jnp.broadcast_to(dbs, db_ref.shape)

    full = pl.BlockSpec((s, LANES), lambda i: (0, 0))
    return pl.pallas_call(body, out_shape=(jax.ShapeDtypeStruct((s, LANES), BF16), jax.ShapeDtypeStruct((8, LANES), F32)), grid=(1,),
                          in_specs=[pl.BlockSpec((s, LANES), lambda i: (0, F_COL)), pl.BlockSpec((1, LANES), lambda i: (0, 0)), full, full],
                          out_specs=(full, pl.BlockSpec((8, LANES), lambda i: (0, 0))), compiler_params=_params("arbitrary"), name=name)(proj, b_pad, dcq, dck)


def _attn_fwd(q_arr, kv_arr, ck6, *, q_col, k_col, v_col, n_hp, causal, bq, bk, name, heads=None, heads_col=0):
    s = q_arr.shape[0]
    skv = kv_arr.shape[0]
    bias = ck6 is not None
    nq = s // bq
    assert not causal or bq == bk

    def body(*refs):
        q_ref, k_ref, v_ref = refs[:3]
        ck_ref = refs[3] if bias else None
        o_ref, lse_ref, heads_ref = refs[-3:]
        i = pl.program_id(1)
        lo = _lane_lo((bq, LANES))
        q = q_ref[...] * SCALE
        qh = (jnp.where(lo, q, 0.0).astype(BF16), jnp.where(lo, 0.0, q).astype(BF16))

        def block(j, carry, diagonal):
            ks = pl.ds(pl.multiple_of(j * bk, bk), bk)
            k = k_ref[ks, :].astype(BF16)
            v = v_ref[ks, :].astype(BF16)
            if diagonal:
                ok = lax.broadcasted_iota(jnp.int32, (bq, bk), 1) <= lax.broadcasted_iota(jnp.int32, (bq, bk), 0)
            out = []
            for h in range(2):
                m, l, acc = carry[3 * h:3 * h + 3]
                sc = lax.dot_general(qh[h], k, NT, preferred_element_type=F32)
                if bias:
                    sc = sc - ck_ref[0, h:h + 1, ks]
                if diagonal:
                    sc = jnp.where(ok, sc, NEG)
                mn = jnp.maximum(m, jnp.max(sc, axis=1, keepdims=True))
                p = jnp.exp(sc - mn)
                al = jnp.exp(m - mn)
                out += [mn, al * l + jnp.sum(p, axis=1, keepdims=True), al * acc + jnp.dot(p.astype(BF16), v, preferred_element_type=F32)]
            return tuple(out)

        col = lambda v_: jnp.full((bq, 1), v_, F32)
        init = (col(NEG), col(0.0), jnp.zeros((bq, LANES), F32)) * 2
        n_full = i if causal else skv // bk
        carry = lax.fori_loop(0, n_full, functools.partial(block, diagonal=False), init)
        if causal:
            carry = block(i, carry, True)
        m0, l0, a0, m1, l1, a1 = carry
        out = _pair(lo, a0 / l0, a1 / l1)
        o_ref[...] = out
        lse_ref[0] = _pair(lo, m0 + jnp.log(l0), m1 + jnp.log(l1))
        heads_ref[...] = out.astype(BF16)

    specs = [pl.BlockSpec((bq, LANES), lambda h, i: (i, q_col + h)),
             pl.BlockSpec((skv, LANES), lambda h, i: (0, k_col + h)),
             pl.BlockSpec((skv, LANES), lambda h, i: (0, v_col + h))]
    ins = [q_arr, kv_arr, kv_arr]
    if bias:
        specs += [pl.BlockSpec((1, 8, skv), lambda h, i: (h, 0, 0))]
        ins += [ck6]
    aliases = {}
    if heads is not None:
        aliases = {len(ins): 2}
        specs += [pl.BlockSpec(memory_space=pl.ANY)]
        ins += [heads]
    return pl.pallas_call(
        body, out_shape=(jax.ShapeDtypeStruct((s, n_hp * LANES), F32), jax.ShapeDtypeStruct((n_hp, s, LANES), F32),
                         jax.ShapeDtypeStruct((s, D_MODEL), BF16)),
        grid=(n_hp, nq), in_specs=specs,
        out_specs=(pl.BlockSpec((bq, LANES), lambda h, i: (i, h)), pl.BlockSpec((1, bq, LANES), lambda h, i: (h, i, 0)),
                   pl.BlockSpec((bq, LANES), lambda h, i: (i, heads_col + h))),
        input_output_aliases=aliases, compiler_params=_params("parallel", "parallel"), name=name)(*ins)


def _attn_bwd(q_arr, kv_arr, o_arr, do_arr, lse, ck6, *, q_col, k_col, v_col, o_col, n_hp, causal, bq, bk, name):
    s = q_arr.shape[0]
    skv = kv_arr.shape[0]
    bias = ck6 is not None
    nq = s // bq
    assert not causal or bq == bk

    def body(*refs):
        q_ref, k_ref, v_ref, o_ref, do_ref, lse_ref = refs[:6]
        if bias:
            ck_ref = refs[6]
            dq_ref, dk_ref, dv_ref, dcq_ref, dck_ref = refs[-5:]
        else:
            dq_ref, dk_ref, dv_ref = refs[-3:]
        j = pl.program_id(1)
        lo_q = _lane_lo((bq, LANES))
        lo_k = _lane_lo((bk, LANES))
        k = k_ref[...]
        v = v_ref[...].astype(BF16)
        kb = k.astype(BF16)
        kh = (jnp.where(lo_k, k, 0.0).astype(BF16), jnp.where(lo_k, 0.0, k).astype(BF16))
        if bias:
            pick_k = [(lax.broadcasted_iota(jnp.int32, (8, bk), 0) == h).astype(BF16) for h in range(2)]
            pick_q = [(lax.broadcasted_iota(jnp.int32, (8, bq), 0) == h).astype(BF16) for h in range(2)]

        @pl.when(j == 0)
        def _():
            dq_ref[...] = jnp.zeros_like(dq_ref)
            if bias:
                dcq_ref[...] = jnp.zeros_like(dcq_ref)

        def block(i, carry, diagonal):
            dk_acc, dv_acc, cs = carry
            qs = pl.ds(pl.multiple_of(i * bq, bq), bq)
            q = q_ref[qs, :] * SCALE
            do = do_ref[qs, :]
            dd = do * o_ref[qs, :]
            lse_i = lse_ref[0, qs, :]
            qh = (jnp.where(lo_q, q, 0.0).astype(BF16), jnp.where(lo_q, 0.0, q).astype(BF16))
            doh = (jnp.where(lo_q, do, 0.0).astype(BF16), jnp.where(lo_q, 0.0, do).astype(BF16))
            dh = (jnp.sum(jnp.where(lo_q, dd, 0.0), axis=1, keepdims=True), jnp.sum(jnp.where(lo_q, 0.0, dd), axis=1, keepdims=True))
            if diagonal:
                ok = lax.broadcasted_iota(jnp.int32, (bq, bk), 1) <= lax.broadcasted_iota(jnp.int32, (bq, bk), 0)
            dq_blk = None
            rs = None
            for h in range(2):
                sc = lax.dot_general(qh[h], kb, NT, preferred_element_type=F32)
                if bias:
                    sc = sc - ck_ref[0, h:h + 1, :]
                if diagonal:
                    sc = jnp.where(ok, sc, NEG)
                p = jnp.exp(sc - lse_i[:, h * HEAD_DIM:h * HEAD_DIM + 1])
                ds = p * (lax.dot_general(doh[h], v, NT, preferred_element_type=F32) - dh[h])
                dsb = ds.astype(BF16)
                dv_acc = dv_acc + lax.dot_general(p.astype(BF16), doh[h], TN, preferred_element_type=F32)
                dk_acc = dk_acc + lax.dot_general(dsb, qh[h], TN, preferred_element_type=F32)
                part = jnp.dot(dsb, kh[h], preferred_element_type=F32)
                dq_blk = part if dq_blk is None else dq_blk + part
                if bias:
                    cs = cs + jnp.dot(pick_q[h], dsb, preferred_element_type=F32)
                    row_sums = lax.dot_general(pick_k[h], dsb, NT, preferred_element_type=F32)
                    rs = row_sums if rs is None else rs + row_sums
            dq_ref[qs, :] += dq_blk * SCALE
            if bias:
                dcq_ref[0, :, qs] += rs
            return dk_acc, dv_acc, cs

        carry = (jnp.zeros((bk, LANES), F32), jnp.zeros((bk, LANES), F32), jnp.zeros((8, bk), F32))
        if causal:
            carry = block(j, carry, True)
        dk_acc, dv_acc, cs = lax.fori_loop(j + 1 if causal else 0, nq, functools.partial(block, diagonal=False), carry)
        dk_ref[...] = dk_acc.astype(BF16)
        dv_ref[...] = dv_acc.astype(BF16)
        if bias:
            dck_ref[0] = cs

    full_q = lambda c: pl.BlockSpec((s, LANES), lambda h, j: (0, c + h))
    specs = [full_q(q_col),
             pl.BlockSpec((bk, LANES), lambda h, j: (j, k_col + h)),
             pl.BlockSpec((bk, LANES), lambda h, j: (j, v_col + h)),
             full_q(0), full_q(o_col),
             pl.BlockSpec((1, s, LANES), lambda h, j: (h, 0, 0))]
    ins = [q_arr, kv_arr, kv_arr, o_arr, do_arr, lse]
    out_shape = [jax.ShapeDtypeStruct((s, n_hp * LANES), F32), jax.ShapeDtypeStruct((skv, n_hp * LANES), BF16),
                 jax.ShapeDtypeStruct((skv, n_hp * LANES), BF16)]
    out_specs = [full_q(0), pl.BlockSpec((bk, LANES), lambda h, j: (j, h)), pl.BlockSpec((bk, LANES), lambda h, j: (j, h))]
    if bias:
        specs += [pl.BlockSpec((1, 8, bk), lambda h, j: (h, 0, j))]
        ins += [ck6]
        out_shape += [jax.ShapeDtypeStruct((n_hp, 8, s), F32), jax.ShapeDtypeStruct((n_hp, 8, skv), F32)]
        out_specs += [pl.BlockSpec((1, 8, s), lambda h, j: (h, 0, 0)), pl.BlockSpec((1, 8, bk), lambda h, j: (h, 0, j))]
    return pl.pallas_call(body, out_shape=tuple(out_shape), grid=(n_hp, skv // bk), in_specs=specs, out_specs=tuple(out_specs),
                          compiler_params=_params("parallel", "arbitrary"), name=name)(*ins)


def _rope_tables(s):
    inv = 1.0 / (ROPE_THETA ** (jnp.arange(0, HEAD_DIM, 2, dtype=F32) / HEAD_DIM))
    ang = jnp.arange(s, dtype=F32)[:, None] * inv[None, :]
    cos, sin = jnp.cos(ang), jnp.sin(ang)
    return jnp.tile(cos, (1, 4)), jnp.concatenate([-sin, sin, -sin, sin], axis=1)


def _rope_tile(x, cos, sin):
    first = (lax.broadcasted_iota(jnp.int32, x.shape, 1) % HEAD_DIM) < (HEAD_DIM // 2)
    swapped = jnp.where(first, pltpu.roll(x, LANES - HEAD_DIM // 2, 1), pltpu.roll(x, HEAD_DIM // 2, 1))
    return x * cos + swapped * sin


def _rope(x_arr, cos_t, sin_t, *, n_cols, out_dtype, br, name):
    s = x_arr.shape[0]

    def body(x_ref, c_ref, s_ref, o_ref):
        cos, sin = c_ref[...], s_ref[...]
        for j in range(n_cols):
            lanes = slice(j * LANES, (j + 1) * LANES)
            o_ref[:, lanes] = _rope_tile(x_ref[:, lanes].astype(F32), cos, sin).astype(o_ref.dtype)

    tab = pl.BlockSpec((br, LANES), lambda i: (i, 0))
    blk = pl.BlockSpec((br, n_cols * LANES), lambda i: (i, 0))
    return pl.pallas_call(body, out_shape=jax.ShapeDtypeStruct((s, n_cols * LANES), out_dtype), grid=(s // br,),
                          in_specs=[blk, tab, tab], out_specs=blk, compiler_params=_params("parallel"), name=name)(x_arr, cos_t, sin_t)


def _stack_heads(x):
    lo = _lane_lo(x.shape)
    return jnp.concatenate([jnp.where(lo, x, 0.0), jnp.where(lo, 0.0, x)], axis=0).astype(BF16)


def _unstack_heads(x):
    return jnp.where(_lane_lo((DIL_L, LANES)), x[:DIL_L], x[DIL_L:])


def _dil_scores(q_ref, k_ref, cur, prev, has_prev):
    qs = _stack_heads(q_ref[cur, :] * SCALE)
    kk = jnp.concatenate([k_ref[prev, :], k_ref[cur, :]], axis=0).astype(BF16)
    a = lax.broadcasted_iota(jnp.int32, (2 * DIL_L, 2 * DIL_L), 0) & (DIL_L - 1)
    c = lax.broadcasted_iota(jnp.int32, (2 * DIL_L, 2 * DIL_L), 1)
    ok = ((c < DIL_L) & (c >= a) & has_prev) | ((c >= DIL_L) & (c - DIL_L <= a))
    return qs, kk, jnp.where(ok, lax.dot_general(qs, kk, NT, preferred_element_type=F32), NEG)


def _dil_rows(t, dil):
    r, m = t % dil, t // dil
    start = m * (DIL_L * dil) + r
    prev = jnp.maximum(start - DIL_L * dil, 0)
    return pl.ds(start, DIL_L, stride=dil), pl.ds(prev, DIL_L, stride=dil), m > 0


def _softmax3(a, b, c):
    m = jnp.maximum(jnp.maximum(a, b), c)
    ea, eb, ec = jnp.exp(a - m), jnp.exp(b - m), jnp.exp(c - m)
    den = ea + eb + ec
    inv = 1.0 / den
    return ea * inv, eb * inv, ec * inv, m + jnp.log(den)


def _dil_fwd(qk_r, proj, *, name):
    s = qk_r.shape[0]
    nsub = s // DIL_L
    mb = 512

    def body(q_ref, k_ref, v_ref, mix_ref, l1_ref, l2_ref, l3_ref, heads_ref, o1_scr, o2_scr, o3_scr):
        for (_, dil), o_scr, l_ref in zip(DILATED_BRANCHES, (o1_scr, o2_scr, o3_scr), (l1_ref, l2_ref, l3_ref)):
            def step(t, carry, dil=dil, o_scr=o_scr, l_ref=l_ref):
                cur, prev, has_prev = _dil_rows(t, dil)
                _, _, sc = _dil_scores(q_ref, k_ref, cur, prev, has_prev)
                vv = jnp.concatenate([v_ref[prev, :], v_ref[cur, :]], axis=0).astype(BF16)
                m = jnp.max(sc, axis=1, keepdims=True)
                e = jnp.exp(sc - m)
                den = jnp.sum(e, axis=1, keepdims=True)
                o = jnp.dot((e * (1.0 / den)).astype(BF16), vv, preferred_element_type=F32)
                o_scr[cur, :] = _unstack_heads(o)
                l_ref[cur, :] = _unstack_heads(jnp.broadcast_to(m + jnp.log(den), (2 * DIL_L, LANES)))
                return carry

            lax.fori_loop(0, nsub, step, 0, unroll=min(DIL_UNROLL_FWD, nsub))

        def merge(i, carry):
            rows = pl.ds(pl.multiple_of(i * mb, mb), mb)
            wa, wb, wc, _ = _softmax3(l1_ref[rows, :], l2_ref[rows, :], l3_ref[rows, :])
            mix = wa * o1_scr[rows, :] + wb * o2_scr[rows, :] + wc * o3_scr[rows, :]
            mix_ref[rows, :] = mix
            heads_ref[rows, :] = mix.astype(BF16)
            return carry

        lax.fori_loop(0, s // mb, merge, 0)

    col = lambda arr_col: pl.BlockSpec((s, LANES), lambda h: (0, arr_col + h))
    shp = jax.ShapeDtypeStruct((s, D_MIX), F32)
    mix, l1, l2, l3, heads = pl.pallas_call(
        body, out_shape=(shp, shp, shp, shp, jax.ShapeDtypeStruct((s, D_MODEL), BF16)), grid=(N_MIX_HP,),
        in_specs=[col(0), col(N_MIX_HP), col(2 * N_MIX_HP)],
        out_specs=(col(0),) * 5, scratch_shapes=[pltpu.VMEM((s, LANES), F32)] * 3,
        compiler_params=_params("parallel"), name=name)(qk_r, qk_r, proj)
    return mix, (l1, l2, l3), heads


def _dil_bwd(qk_r, proj, mix, dheads, lses, *, name):
    s = qk_r.shape[0]
    nsub = s // DIL_L
    mb = 512

    def body(q_ref, k_ref, v_ref, mix_ref, dm_ref, l1_ref, l2_ref, l3_ref, dq_ref, dk_ref, dv_ref, lt_scr, dd_scr):
        lo = _lane_lo((DIL_L, LANES))
        lo_m = _lane_lo((mb, LANES))

        def prep(i, carry):
            rows = pl.ds(pl.multiple_of(i * mb, mb), mb)
            _, _, _, lt = _softmax3(l1_ref[rows, :], l2_ref[rows, :], l3_ref[rows, :])
            lt_scr[rows, :] = lt
            dd = dm_ref[rows, :] * mix_ref[rows, :]
            dd_scr[rows, :] = _pair(lo_m, jnp.sum(jnp.where(lo_m, dd, 0.0), axis=1, keepdims=True),
                                    jnp.sum(jnp.where(lo_m, 0.0, dd), axis=1, keepdims=True))
            zero = jnp.zeros((mb, LANES), F32)
            dq_ref[rows, :] = zero
            dk_ref[rows, :] = zero
            dv_ref[rows, :] = zero
            return carry

        lax.fori_loop(0, s // mb, prep, 0)

        for (_, dil), l_ref in zip(DILATED_BRANCHES, (l1_ref, l2_ref, l3_ref)):
            def step(t, carry, dil=dil, l_ref=l_ref):
                cur, prev, has_prev = _dil_rows(t, dil)
                qs, kk, sc = _dil_scores(q_ref, k_ref, cur, prev, has_prev)
                vv = jnp.concatenate([v_ref[prev, :], v_ref[cur, :]], axis=0).astype(BF16)
                lg = l_ref[cur, :]
                w = jnp.exp(lg - lt_scr[cur, :])
                wd = w * dd_scr[cur, :]
                column = lambda x: jnp.concatenate([x[:, 0:1], x[:, HEAD_DIM:HEAD_DIM + 1]], axis=0)
                dos = _stack_heads(w * dm_ref[cur, :])
                p = jnp.exp(sc - column(lg))
                ds = (p * (lax.dot_general(dos, vv, NT, preferred_element_type=F32) - column(wd))).astype(BF16)
                dq_ref[cur, :] += _unstack_heads(jnp.dot(ds, kk, preferred_element_type=F32)) * SCALE
                dkk = lax.dot_general(ds, qs, TN, preferred_element_type=F32)
                dvv = lax.dot_general(p.astype(BF16), dos, TN, preferred_element_type=F32)
                dk_ref[cur, :] += dkk[DIL_L:]
                dv_ref[cur, :] += dvv[DIL_L:]
                dk_ref[prev, :] += dkk[:DIL_L]
                dv_ref[prev, :] += dvv[:DIL_L]
                return carry

            lax.fori_loop(0, nsub, step, 0, unroll=DIL_UNROLL_BWD)

    col = lambda arr_col: pl.BlockSpec((s, LANES), lambda h: (0, arr_col + h))
    shp = jax.ShapeDtypeStruct((s, D_MIX), F32)
    return pl.pallas_call(
        body, out_shape=(shp, shp, shp), grid=(N_MIX_HP,),
        in_specs=[col(0), col(N_MIX_HP), col(2 * N_MIX_HP), col(0), col(0), col(0), col(0), col(0)], out_specs=(col(0),) * 3,
        scratch_shapes=[pltpu.VMEM((s, LANES), F32)] * 2,
        compiler_params=_params("parallel"), name=name)(qk_r, qk_r, proj, mix, dheads, *lses)


CONV_BR = 512
FF_CHUNK = 2 * D_FF // N_DEV
FF_HALF = N_DEV // 2
HALO = 8


def _shift_down(x, halo, k):
    row = lax.broadcasted_iota(jnp.int32, x.shape, 0)
    y = pltpu.roll(x, k, 0)
    for r in range(k):
        y = jnp.where(row == r, halo[HALO - k + r:HALO - k + r + 1, :], y)
    return y


def _shift_up(x, halo, k):
    n = x.shape[0]
    row = lax.broadcasted_iota(jnp.int32, x.shape, 0)
    y = pltpu.roll(x, n - k, 0)
    for r in range(k):
        y = jnp.where(row == n - k + r, halo[r:r + 1, :], y)
    return y


def _conv_vals(u, halo, w, b):
    s1 = _shift_down(u, halo, 1)
    s2 = _shift_down(u, halo, 2)
    return b + w[0:1, :] * s2 + w[1:2, :] * s1 + w[2:3, :] * u, s1, s2


def _conv_in_specs(order, layer):
    rc = (lambda i, j: (i, j)) if order == "rc" else (lambda j, i: (i, j))
    per = CONV_BR // HALO
    main = lambda off: pl.BlockSpec((None, CONV_BR, FF_CHUNK), lambda *g: (off + rc(*g)[1], rc(*g)[0], 0))
    halo = lambda off: pl.BlockSpec((None, HALO, FF_CHUNK), lambda *g: (off + rc(*g)[1], jnp.maximum(rc(*g)[0] * per - 1, 0), 0))
    wspec = lambda off: pl.BlockSpec((None, None, 3, FF_CHUNK), lambda *g: (layer, off + rc(*g)[1], 0, 0))
    bspec = lambda off: pl.BlockSpec((None, 1, FF_CHUNK), lambda *g: (off + rc(*g)[1], 0, 0))
    return [main(0), halo(0), main(FF_HALF), halo(FF_HALF), wspec(0), wspec(FF_HALF), bspec(0), bspec(FF_HALF)]


def _conv_fwd(u, cw, cb, layer, *, name):
    s = u.shape[1]

    def body(uv_ref, hv_ref, ug_ref, hg_ref, wv_ref, wg_ref, bv_ref, bg_ref, o_ref):
        first = pl.program_id(0) == 0
        hv = jnp.where(first, 0.0, hv_ref[...])
        hg = jnp.where(first, 0.0, hg_ref[...])
        val, _, _ = _conv_vals(uv_ref[...], hv, wv_ref[...], bv_ref[...])
        gate, _, _ = _conv_vals(ug_ref[...], hg, wg_ref[...], bg_ref[...])
        o_ref[...] = (gate / (1.0 + jnp.exp(-gate)) * val).astype(BF16)

    return pl.pallas_call(body, out_shape=jax.ShapeDtypeStruct((FF_HALF, s, FF_CHUNK), BF16), grid=(s // CONV_BR, FF_HALF),
                          in_specs=_conv_in_specs("rc", layer), out_specs=pl.BlockSpec((None, CONV_BR, FF_CHUNK), lambda i, j: (j, i, 0)),
                          compiler_params=_params("parallel", "parallel"), name=name)(u, u, u, u, cw, cw, cb, cb)


def _swiglu_bwd(val, gate, da):
    sg = 1.0 / (1.0 + jnp.exp(-gate))
    return da * (gate * sg), da * val * (sg * (1.0 + gate * (1.0 - sg)))


def _conv_bwd(u, cw, cb, da, layer, *, name):
    s = u.shape[1]
    nrow = s // CONV_BR
    per = CONV_BR // HALO

    def body(uv_ref, hv_ref, ug_ref, hg_ref, wv_ref, wg_ref, bv_ref, bg_ref, da_ref, nv_ref, ng_ref, nda_ref, du_ref, dwb_ref):
        i = pl.program_id(1)
        first, last = i == 0, i == nrow - 1
        hv = jnp.where(first, 0.0, hv_ref[...])
        hg = jnp.where(first, 0.0, hg_ref[...])
        uv, ug = uv_ref[...], ug_ref[...]
        wv, wg, bv, bg = wv_ref[...], wg_ref[...], bv_ref[...], bg_ref[...]
        val, v1, v2 = _conv_vals(uv, hv, wv, bv)
        gate, g1, g2 = _conv_vals(ug, hg, wg, bg)
        dval, dgate = _swiglu_bwd(val, gate, da_ref[...])
        val_n, _, _ = _conv_vals(nv_ref[...], uv[CONV_BR - HALO:, :], wv, bv)
        gate_n, _, _ = _conv_vals(ng_ref[...], ug[CONV_BR - HALO:, :], wg, bg)
        dval_n, dgate_n = _swiglu_bwd(val_n, gate_n, nda_ref[...])
        dval_n = jnp.where(last, 0.0, dval_n)
        dgate_n = jnp.where(last, 0.0, dgate_n)
        back = lambda dc, dc_n, w: w[2:3, :] * dc + w[1:2, :] * _shift_up(dc, dc_n, 1) + w[0:1, :] * _shift_up(dc, dc_n, 2)
        du_ref[0] = back(dval, dval_n, wv).astype(BF16)
        du_ref[1] = back(dgate, dgate_n, wg).astype(BF16)

        @pl.when(first)
        def _():
            dwb_ref[...] = jnp.zeros_like(dwb_ref)

        cs = lambda t: jnp.sum(t, axis=0, keepdims=True)
        r8 = lax.broadcasted_iota(jnp.int32, (8, FF_CHUNK), 0)
        rows4 = lambda a, b, c, d: jnp.where(r8 == 0, a, jnp.where(r8 == 1, b, jnp.where(r8 == 2, c, jnp.where(r8 == 3, d, 0.0))))
        dwb_ref[0] += rows4(cs(dval * v2), cs(dval * v1), cs(dval * uv), cs(dval))
        dwb_ref[1] += rows4(cs(dgate * g2), cs(dgate * g1), cs(dgate * ug), cs(dgate))

    nxt = lambda off: pl.BlockSpec((None, HALO, FF_CHUNK), lambda j, i: (off + j, jnp.minimum((i + 1) * per, nrow * per - 1), 0))
    specs = _conv_in_specs("cr", layer) + [pl.BlockSpec((None, CONV_BR, FF_CHUNK), lambda j, i: (j, i, 0)), nxt(0), nxt(FF_HALF), nxt(0)]
    return pl.pallas_call(
        body, out_shape=(jax.ShapeDtypeStruct((2, FF_HALF, s, FF_CHUNK), BF16), jax.ShapeDtypeStruct((2, FF_HALF, 8, FF_CHUNK), F32)),
        grid=(FF_HALF, nrow), in_specs=specs,
        out_specs=(pl.BlockSpec((2, None, CONV_BR, FF_CHUNK), lambda j, i: (0, j, i, 0)),
                   pl.BlockSpec((2, None, 8, FF_CHUNK), lambda j, i: (0, j, 0, 0))),
        compiler_params=_params("parallel", "arbitrary"), name=name)(u, u, u, u, cw, cw, cb, cb, da, u, u, da)


def _rows_of(r):
    return lambda ref, idx: ref.at[:, pl.ds(idx * r, r), :]


def _slot1(ref, idx):
    return ref.at[:, idx]


def _slot0(ref, idx):
    return ref.at[idx]


def _all_gather(shards, full_shapes, places, *, name):
    n = len(shards)

    def body(*refs):
        ins, outs = refs[:n], refs[n:2 * n]
        send_sems, recv_sems, local_sems = refs[2 * n:]
        mx, my, mc = lax.axis_index("x"), lax.axis_index("y"), lax.axis_index("c")
        me, sibling = (mx, my, mc), (mx, my, 1 - mc)
        chips = [(1 - mx, my), (mx, 1 - my), (1 - mx, 1 - my)]

        def win(t, px, py, pc):
            return places[t](outs[t], 4 * px + 2 * py + pc)

        def copy(t, k, block, to, src=None):
            return pltpu.make_async_remote_copy(src_ref=win(t, *block) if src is None else src, dst_ref=win(t, *block),
                                                send_sem=send_sems.at[t, k], recv_sem=recv_sems.at[t, k], device_id=to, device_id_type=MESH)

        mine = [pltpu.make_async_copy(ins[t], win(t, *me), local_sems.at[t]) for t in range(n)]
        for cp in mine:
            cp.start()
        first = []
        for t in range(n):
            first += [copy(t, 0, me, sibling, src=ins[t])] + [copy(t, 1 + j, me, (*chip, mc), src=ins[t]) for j, chip in enumerate(chips)]
        for cp in first:
            cp.start()
        passed = []
        for j, chip in enumerate(chips):
            for t in range(n):
                copy(t, 1 + j, (*chip, mc), me).wait_recv()
                fwd = copy(t, 4 + j, (*chip, mc), sibling)
                fwd.start()
                passed.append(fwd)
        for t in range(n):
            copy(t, 0, sibling, me).wait_recv()
            for j, chip in enumerate(chips):
                copy(t, 4 + j, (*chip, 1 - mc), me).wait_recv()
        for cp in first + passed:
            cp.wait_send()
        for cp in mine:
            cp.wait()

    hbm = pl.BlockSpec(memory_space=pl.ANY)
    return pl.pallas_call(
        body, out_shape=tuple(jax.ShapeDtypeStruct(s, x.dtype) for s, x in zip(full_shapes, shards)),
        in_specs=[hbm] * n, out_specs=(hbm,) * n,
        scratch_shapes=[pltpu.SemaphoreType.DMA((n, 7)), pltpu.SemaphoreType.DMA((n, 7)), pltpu.SemaphoreType.DMA((n,))],
        name=name)(*shards)


FLIPS = [(fx, fy, fc) for fx in (0, 1) for fy in (0, 1) for fc in (0, 1)][1:]


def _exchange_copies(kind, places, src, land, send_sems, recv_sems, local_sems):
    mx, my, mc = lax.axis_index("x"), lax.axis_index("y"), lax.axis_index("c")
    me = 4 * mx + 2 * my + mc
    n = len(src)
    local, remote = [], []
    for t in range(n):
        if kind == "gather":
            local.append(pltpu.make_async_copy(src[t], places[t](land[t], me), local_sems.at[t]))
        else:
            local.append(pltpu.make_async_copy(places[t](src[t], me), land[t].at[me], local_sems.at[t]))
    for k, (fx, fy, fc) in enumerate(FLIPS):
        px, py, pc = mx ^ fx, my ^ fy, mc ^ fc
        peer = 4 * px + 2 * py + pc
        for t in range(n):
            sems = dict(send_sem=send_sems.at[7 * t + k], recv_sem=recv_sems.at[7 * t + k], device_id=(px, py, pc), device_id_type=MESH)
            if kind == "gather":
                pair = [(src[t], places[t](land[t], me)), (src[t], places[t](land[t], peer))]
            else:
                pair = [(places[t](src[t], peer), land[t].at[me]), (places[t](src[t], peer), land[t].at[peer])]
            remote.append([functools.partial(pltpu.make_async_remote_copy, src_ref=s_, dst_ref=d_, **sems) for s_, d_ in pair])
    return local, remote


HBM_SPEC = pl.BlockSpec(memory_space=pltpu.HBM)
SEM_SPEC = pl.BlockSpec(memory_space=pltpu.SEMAPHORE)
SIDE_EFFECT = pltpu.SideEffectType.DATAFLOW_SIDE_EFFECTING


def _exchange_start(kind, srcs, land_shapes, places, after, *, name):
    n = len(srcs)

    def body(*refs):
        src, land = refs[:n], refs[n:2 * n]
        send_sems, recv_sems, local_sems = refs[2 * n + 1:2 * n + 4]
        token = refs[-1]
        local, remote = _exchange_copies(kind, places, src, land, send_sems, recv_sems, local_sems)
        for cp in local:
            cp.start()
        for send, _ in remote:
            send().start()
        token[...] = jnp.zeros_like(token)

    hbm = lambda t: pltpu.with_memory_space_constraint(t, pltpu.HBM)
    lands = [hbm(lax.empty(tuple(s), x.dtype)) for s, x in zip(land_shapes, srcs)]
    out_shape = (pltpu.SemaphoreType.DMA((7 * n,)), pltpu.SemaphoreType.DMA((7 * n,)), pltpu.SemaphoreType.DMA((n,)),
                 *[pltpu.HBM(x.shape, x.dtype) for x in srcs], *[pltpu.HBM(tuple(s), x.dtype) for s, x in zip(land_shapes, srcs)],
                 jax.ShapeDtypeStruct((8, LANES), F32))
    outs = pl.pallas_call(
        body, name=name, out_shape=out_shape, in_specs=[HBM_SPEC] * (2 * n) + [pl.BlockSpec(memory_space=pl.ANY)],
        out_specs=(SEM_SPEC, SEM_SPEC, SEM_SPEC) + (HBM_SPEC,) * (2 * n) + (pl.BlockSpec(memory_space=pltpu.VMEM),),
        input_output_aliases={i: 3 + i for i in range(2 * n)},
        compiler_params=pltpu.CompilerParams(has_side_effects=SIDE_EFFECT))(*[hbm(x) for x in srcs], *lands, after)
    return dict(sems=outs[:3], src=outs[3:3 + n], land=outs[3 + n:3 + 2 * n], token=outs[-1])


def _exchange_wait(kind, started, places, after, *, name):
    n = len(started["src"])

    def body(*refs):
        src, land = refs[:n], refs[n:2 * n]
        send_sems, recv_sems, local_sems = refs[2 * n:2 * n + 3]
        local, remote = _exchange_copies(kind, places, src, land, send_sems, recv_sems, local_sems)
        for cp in local:
            cp.wait()
        for send, arrival in remote:
            send().wait_send()
            arrival().wait_recv()

    out_shape = tuple(pltpu.HBM(x.shape, x.dtype) for x in started["src"]) + tuple(pltpu.HBM(x.shape, x.dtype) for x in started["land"])
    outs = pl.pallas_call(
        body, name=name, out_shape=out_shape,
        in_specs=[HBM_SPEC] * (2 * n) + [SEM_SPEC] * 3 + [pl.BlockSpec(memory_space=pl.ANY)], out_specs=(HBM_SPEC,) * (2 * n),
        input_output_aliases={i: i for i in range(2 * n)},
        compiler_params=pltpu.CompilerParams(has_side_effects=SIDE_EFFECT))(*started["src"], *started["land"], *started["sems"], after)
    return list(outs[n:])


def _adamw(parts, w, m, v, *, br, name):
    layers, r, wd = w.shape
    assert len(parts) == layers

    def body(*refs):
        p_refs = refs[:layers]
        w_ref, m_ref, v_ref, g_ref, d_ref, nm_ref, nv_ref = refs[layers:]
        for k in range(layers):
            @pl.when(pl.program_id(0) == k)
            def _(p_ref=p_refs[k]):
                g = p_ref[0].astype(F32)
                for dev in range(1, N_DEV):
                    g = g + p_ref[dev].astype(F32)
                mm = ADAM_B1 * m_ref[...] + (1.0 - ADAM_B1) * g
                vv = ADAM_B2 * v_ref[...] + (1.0 - ADAM_B2) * (g * g)
                m_hat = mm / (1.0 - ADAM_B1 ** ADAM_STEP)
                v_hat = vv / (1.0 - ADAM_B2 ** ADAM_STEP)
                g_ref[...] = g
                d_ref[...] = -ADAM_LR * (m_hat / (jnp.sqrt(v_hat) + ADAM_EPS) + ADAM_WD * w_ref[...])
                nm_ref[...] = mm
                nv_ref[...] = vv

    p_spec = lambda k: pl.BlockSpec((N_DEV, None, br, wd), lambda l, i: (0, 0, jnp.where(l == k, i, 0), 0))
    blk = pl.BlockSpec((None, br, wd), lambda l, i: (l, i, 0))
    shp = jax.ShapeDtypeStruct((layers, r, wd), F32)
    return pl.pallas_call(body, out_shape=(shp, shp, shp, shp), grid=(layers, r // br),
                          in_specs=[p_spec(k) for k in range(layers)] + [blk, blk, blk], out_specs=(blk, blk, blk, blk),
                          compiler_params=_params("arbitrary", "arbitrary"), name=name)(*parts, w, m, v)


SMALL = ("norm_mix", "norm_mem", "norm_ffn", "b_forget", "conv_b", "norm_final")


def _pack(tensors):
    flat = jnp.concatenate([t.reshape(-1) for t in tensors])
    rows = -(-flat.shape[0] // (PACK_W * PACK_ROW_ALIGN)) * PACK_ROW_ALIGN
    flat = jnp.pad(flat, (0, rows * PACK_W - flat.shape[0]))
    return flat.reshape(1, rows, PACK_W)


def _unpack(buf, shapes):
    flat = buf.reshape(-1)
    out, off = [], 0
    for shp in shapes:
        n = math.prod(shp)
        out.append(flat[off:off + n].reshape(tuple(shp)))
        off += n
    return out


def _fox_permute(w):
    pad = jnp.zeros(w.shape[:-1] + (FOX_P - FOX_IN,), w.dtype)
    return jnp.concatenate([w[..., :3 * D_MIX], w[..., 3 * D_MIX + N_MIX_HEADS:], w[..., 3 * D_MIX:3 * D_MIX + N_MIX_HEADS], pad], axis=-1)


def _fox_unpermute(w):
    return jnp.concatenate([w[..., :3 * D_MIX], w[..., DIL_IN:DIL_IN + N_MIX_HEADS], w[..., 3 * D_MIX:DIL_IN]], axis=-1)


def _bias_layout(c):
    s = c.shape[0]
    ct = c[:, :N_MIX_HEADS].T.reshape(N_MIX_HP, 2, s)
    return jnp.pad(ct, ((0, 0), (0, 6), (0, 0)))


def _bias_grad(dck6):
    s = dck6.shape[2]
    dk = dck6[:, :2, :].reshape(N_MIX_HEADS, s).T
    return jnp.pad(dk, ((0, 0), (0, LANES - N_MIX_HEADS)))


def _device_step(x, mem, target, small, get_weights, put_grads):
    s = x.shape[0]
    mt = mem.shape[0]
    bq = 512
    cos_t, sin_t = _rope_tables(s)
    row = lambda t, l: t[l][None, :]
    saved = []
    h = x
    cb8 = small["conv_b"].reshape(DEPTH, N_DEV, 1, FF_CHUNK)
    for l in range(DEPTH):
        kind, slot = l % 2, l // 2
        wl = dict(get_weights(l, "attn", h))
        if l == 0:
            xn = _rmsnorm_fwd(h, row(small["norm_mix"], l), br=512, name=f"norm_mix_fwd{l}")
        mn = _rmsnorm_fwd(mem, row(small["norm_mem"], l), br=mt, name=f"norm_mem_fwd{l}")
        if kind == 0:
            proj = _mm(xn, wl["w_in"], "nn", tm=1024, tn=384, layer=0, name=f"in_proj{l}")
        else:
            proj, qk_r = _mm(xn, wl["w_in"], "nt", tm=1024, tn=512, layer=0, rope=(cos_t, sin_t, 2 * D_MIX), name=f"in_proj{l}")
        kvm = _mm(mn, wl["w_mem_kv"], "nn", tm=mt, tn=512, layer=0, name=f"mem_kv{l}")
        st = dict(h=h, xn=xn, mn=mn, proj=proj, kvm=kvm, w=wl)
        if kind == 0:
            b_pad = jnp.pad(small["b_forget"][slot], (0, LANES - N_MIX_HEADS))[None, :]
            c = _forget_cumsum(proj, b_pad, name=f"forget_cumsum{l}")
            ck6 = _bias_layout(c)
            mix, lse, heads = _attn_fwd(proj, proj, ck6, q_col=0, k_col=N_MIX_HP, v_col=2 * N_MIX_HP, n_hp=N_MIX_HP,
                                        causal=True, bq=min(s, 1024), bk=min(s, 1024), name=f"fox_fwd{l}")
            st.update(b_pad=b_pad, ck6=ck6, mix=mix, lse=lse)
        else:
            mix, lses, heads = _dil_fwd(qk_r, proj, name=f"dil_fwd{l}")
            st.update(qk_r=qk_r, lses=lses, mix=mix)
        mo, lse_m, heads = _attn_fwd(proj, kvm, None, q_col=QM_COL, k_col=0, v_col=N_MEM_HP, n_hp=N_MEM_HP, causal=False,
                                     bq=min(s, 2048), bk=mt, heads=heads, heads_col=N_MIX_HP, name=f"mem_fwd{l}")
        h1, xf = _mm(heads, wl["w_out"], "nn", tm=1024, tn=D_MODEL, res=h, layer=0, norm_gain=row(small["norm_ffn"], l),
                     name=f"out_proj{l}")
        wl.update(get_weights(l, "ffn", xf))
        u = _mm(xf, wl["w_up"], "nt", tm=1024, tn=FF_CHUNK, layer=0, chunk="b", name=f"up_proj{l}")
        a = _conv_fwd(u, wl["conv_w"], cb8[l], 0, name=f"conv_fwd{l}")
        st.update(mo=mo, lse_m=lse_m, heads=heads, h1=h1, xf=xf, u=u, a=a)
        saved.append(st)
        if l + 1 < DEPTH:
            h, xn = _mm(a, wl["w_down"], "nn", tm=512, tn=D_MODEL, res=h1, layer=0, chunk="reduce",
                        norm_gain=row(small["norm_mix"], l + 1), name=f"down_proj{l}")
        else:
            h = _mm(a, wl["w_down"], "nn", tm=1024, tn=512, res=h1, layer=0, chunk="reduce", name=f"down_proj{l}")

    dh, dhb, dg_final, loss = _loss_head(h, target, small["norm_final"][None, :], br=512, name="loss_head")
    gs = {k: [None] * DEPTH for k in ("norm_mix", "norm_mem", "norm_ffn", "conv_b")}
    gs["b_forget"] = [None] * 2
    dep = 0.0
    for l in reversed(range(DEPTH)):
        st = saved[l]
        wl = st["w"]
        gw = {}
        kind, slot = l % 2, l // 2
        da = _mm(dhb, wl["w_down"], "nt", tm=1024, tn=FF_CHUNK, layer=0, chunk="b", name=f"down_dx{l}")
        gw["w_down"] = _mm(st["a"], dhb, "tn", tm=FF_CHUNK, tn=512, out_dtype=BF16, chunk="a", name=f"down_dw{l}")
        du, dwb = _conv_bwd(st["u"], wl["conv_w"], cb8[l] + dep, da, 0, name=f"conv_bwd{l}")
        du = du.reshape(N_DEV, s, FF_CHUNK)
        dwb = dwb.reshape(N_DEV, 8, FF_CHUNK)
        gw["conv_w"] = dwb
        gs["conv_b"][l] = dwb[:, 3, :].reshape(-1)
        dh1, dh1b, dgf = _mm(du, wl["w_up"], "nn", tm=256, tn=D_MODEL, layer=0, chunk="reduce",
                             norm_bwd=(st["h1"], row(small["norm_ffn"], l), dh), name=f"up_dx{l}")
        gw["w_up"] = _mm(du, st["xf"], "tn", tm=FF_CHUNK, tn=512, out_dtype=BF16, chunk="a", name=f"up_dw{l}")
        gs["norm_ffn"][l] = dgf[0]
        dheads = _mm(dh1b, wl["w_out"], "nt", tm=1024, tn=512, layer=0, name=f"out_dx{l}")
        gw["w_out"] = _mm(st["heads"], dh1b, "tn", tm=512, tn=512, out_dtype=BF16, name=f"out_dw{l}")
        dqm, dkm, dvm = _attn_bwd(st["proj"], st["kvm"], st["mo"], dheads, st["lse_m"], None, q_col=QM_COL, k_col=0,
                                  v_col=N_MEM_HP, o_col=N_MIX_HP, n_hp=N_MEM_HP, causal=False, bq=min(s, 1024), bk=mt,
                                  name=f"mem_bwd{l}")
        dkvm = jnp.concatenate([dkm, dvm], axis=1)
        gw["w_mem_kv"] = _mm(st["mn"], dkvm, "tn", tm=512, tn=512, out_dtype=BF16, name=f"mem_kv_dw{l}")
        dmn = _mm(dkvm, wl["w_mem_kv"], "nt", tm=mt, tn=512, layer=0, name=f"mem_kv_dx{l}")
        dep_early = put_grads(l, "ffn", gw)
        _, _, dgm = _rmsnorm_bwd(mem, dmn, row(small["norm_mem"], l), None, br=mt, name=f"norm_mem_bwd{l}")
        gs["norm_mem"][l] = dgm[0]
        if kind == 0:
            dq, dk, dv, dcq6, dck6 = _attn_bwd(st["proj"], st["proj"], st["mix"], dheads, st["lse"], st["ck6"] + dep_early, q_col=0,
                                               k_col=N_MIX_HP, v_col=2 * N_MIX_HP, o_col=0, n_hp=N_MIX_HP, causal=True,
                                               bq=min(s, 1024), bk=min(s, 1024), name=f"fox_bwd{l}")
            dz, db = _forget_cumsum_bwd(st["proj"], st["b_pad"], _bias_grad(dcq6), _bias_grad(dck6), name=f"forget_cumsum_bwd{l}")
            gs["b_forget"][slot] = db[0, :N_MIX_HEADS]
            dproj = jnp.concatenate([dq.astype(BF16), dk, dv, dqm.astype(BF16), dz], axis=1)
        else:
            dq_r, dk_r, dv = _dil_bwd(st["qk_r"], st["proj"], st["mix"], dheads, st["lses"], name=f"dil_bwd{l}")
            dq = _rope(dq_r, cos_t + dep_early, -sin_t, n_cols=N_MIX_HP, out_dtype=BF16, br=512, name=f"rope_bwd_q{l}")
            dk = _rope(dk_r, cos_t, -sin_t, n_cols=N_MIX_HP, out_dtype=BF16, br=512, name=f"rope_bwd_k{l}")
            dproj = jnp.concatenate([dq, dk, dv.astype(BF16), dqm.astype(BF16)], axis=1)
        if kind == 0:
            gw["w_in"] = _mm(st["xn"], dproj, "tn", tm=512, tn=384, out_dtype=BF16, name=f"in_dw{l}")
        else:
            gw["w_in"] = _mm(dproj, st["xn"], "tn", tm=512, tn=512, out_dtype=BF16, name=f"in_dw{l}")
        dh, dhb, dgx = _mm(dproj, wl["w_in"], "nt" if kind == 0 else "nn", tm=512, tn=D_MODEL, layer=0,
                           norm_bwd=(st["h"], row(small["norm_mix"], l), dh1), name=f"in_dx{l}")
        gs["norm_mix"][l] = dgx[0]
        dep = put_grads(l, "attn", gw)

    grads_s = {k: jnp.stack(v) for k, v in gs.items()}
    grads_s["norm_final"] = dg_final[0]
    return loss[0, 0], dh, grads_s


def kernel(x, mem, norm_mix, norm_mem, norm_ffn, w_in_fox, b_forget, w_in_dil, w_mem_kv, w_out, w_up, conv_w, conv_b, w_down, norm_final, loss_target, m_norm_mix, m_norm_mem, m_norm_ffn, m_w_in_fox, m_b_forget, m_w_in_dil, m_w_mem_kv, m_w_out, m_w_up, m_conv_w, m_conv_b, m_w_down, m_norm_final, v_norm_mix, v_norm_mem, v_norm_ffn, v_w_in_fox, v_b_forget, v_w_in_dil, v_w_mem_kv, v_w_out, v_w_up, v_conv_w, v_conv_b, v_w_down, v_norm_final):
    names = ["norm_mix", "norm_mem", "norm_ffn", "w_in_fox", "b_forget", "w_in_dil", "w_mem_kv", "w_out", "w_up", "conv_w", "conv_b",
             "w_down", "norm_final"]
    w = dict(zip(names, (norm_mix, norm_mem, norm_ffn, w_in_fox, b_forget, w_in_dil, w_mem_kv, w_out, w_up, conv_w, conv_b, w_down, norm_final)))
    m = dict(zip(names, (m_norm_mix, m_norm_mem, m_norm_ffn, m_w_in_fox, m_b_forget, m_w_in_dil, m_w_mem_kv, m_w_out, m_w_up, m_conv_w,
                         m_conv_b, m_w_down, m_norm_final)))
    v = dict(zip(names, (v_norm_mix, v_norm_mem, v_norm_ffn, v_w_in_fox, v_b_forget, v_w_in_dil, v_w_mem_kv, v_w_out, v_w_up, v_conv_w,
                         v_conv_b, v_w_down, v_norm_final)))
    big = ("w_in_fox", "w_in_dil", "w_mem_kv", "w_out", "w_up", "w_down", "conv_w")
    small_shapes = [w[k].shape for k in SMALL]
    dil_c = w_in_dil.shape[2]
    rows = {k: w[k].shape[1] for k in ("w_in_fox", "w_mem_kv", "w_out", "w_down")}

    def places(l):
        w_in_place = _rows_of(rows["w_in_fox"]) if l % 2 == 0 else _slot1
        return [w_in_place, _rows_of(rows["w_mem_kv"]), _rows_of(rows["w_out"]), _slot1, _rows_of(rows["w_down"]), _slot1]

    def full_shapes(l):
        w_in_shape = (1, D_MODEL, FOX_P) if l % 2 == 0 else (1, N_DEV, dil_c, D_MODEL)
        return [w_in_shape, (1, D_MODEL, 2 * D_MEMQ), (1, D_MODEL, D_MODEL), (1, N_DEV, FF_CHUNK, D_MODEL), (1, D_FF, D_MODEL),
                (1, N_DEV, 3, FF_CHUNK)]

    transposed = lambda t: jnp.swapaxes(t, 1, 2)
    to_wire = {"w_in_fox": lambda t: _fox_permute(t).astype(BF16), "w_in_dil": lambda t: t.T.astype(BF16),
               "w_mem_kv": lambda t: t.astype(BF16), "w_out": lambda t: t.astype(BF16), "w_up": lambda t: t.T.astype(BF16),
               "w_down": lambda t: t.astype(BF16), "conv_w": lambda t: t}
    shard = lambda k, i: to_wire[k](w[k][i])
    shard_shape = lambda k: jax.eval_shape(lambda: shard(k, 0)).shape
    everything = (0, 1, 2, 3, 4, 5)
    gather_groups = {l: ((0, 1, 2), (3, 4, 5)) if l == 0 else (everything,) for l in range(DEPTH)}
    scatter_groups = {l: ((1, 2, 3, 4, 5), (0,)) if l == 0 else (everything,) for l in range(DEPTH)}
    pick = lambda seq, group: [seq[i] for i in group]
    tag = lambda l, group: f"{l}" + ("" if group == everything else "_" + "".join(str(i) for i in group))

    gathers, after = {}, norm_final
    for l in range(DEPTH):
        w_in_shard = shard("w_in_fox" if l % 2 == 0 else "w_in_dil", l // 2)[None]
        shards = [w_in_shard] + [shard(k, l)[None] for k in ("w_mem_kv", "w_out", "w_up", "w_down", "conv_w")]
        for group in gather_groups[l]:
            gathers[l, group] = _exchange_start("gather", pick(shards, group), pick(full_shapes(l), group), pick(places(l), group), after,
                                                name=f"weights_gather_start{tag(l, group)}")
            after = gathers[l, group]["token"]
    started = sum(g["token"][0, 0] for g in gathers.values())
    small = {k: w[k] for k in SMALL}
    small["norm_mix"] = norm_mix + started
    landed = {}

    def get_weights(l, part, h):
        group = [g for g in gather_groups[l] if (0 if part == "attn" else 3) in g][0]
        if (l, group) not in landed:
            lands = _exchange_wait("gather", gathers[l, group], pick(places(l), group), h, name=f"weights_gather_wait{tag(l, group)}")
            landed[l, group] = dict(zip(group, lands))
        got = landed[l, group]
        if part == "ffn":
            return dict(w_up=got[3], w_down=got[4].reshape(1, FF_HALF, FF_CHUNK, D_MODEL), conv_w=got[5])
        w_in = got[0] if l % 2 == 0 else got[0].reshape(1, N_DEV * dil_c, D_MODEL)
        return dict(w_in=w_in, w_mem_kv=got[1], w_out=got[2])

    scatters, pending = {}, {}

    def put_grads(l, part, g):
        pending.setdefault(l, {}).update(g)
        if part == "ffn" and len(scatter_groups[l]) == 1:
            return 0.0
        group = scatter_groups[l][0 if part == "ffn" else -1]
        have = pending[l]
        srcs = {3: lambda: have["w_up"][None], 4: lambda: have["w_down"].reshape(1, D_FF, D_MODEL), 5: lambda: have["conv_w"][None],
                1: lambda: have["w_mem_kv"][None], 2: lambda: have["w_out"][None]}
        if l % 2 == 0:
            srcs[0] = lambda: have["w_in"][None]
        else:
            srcs[0] = lambda: have["w_in"].reshape(1, N_DEV, dil_c, D_MODEL)
        shard_shapes = [shard_shape("w_in_fox" if l % 2 == 0 else "w_in_dil")] + \
            [shard_shape(k) for k in ("w_mem_kv", "w_out", "w_up", "w_down")] + [(8, FF_CHUNK)]
        sources = [srcs[i]() for i in group]
        scatters[l, group] = _exchange_start("scatter", sources, [(N_DEV, 1) + tuple(s) for s in pick(shard_shapes, group)],
                                             pick(places(l), group), sources[0], name=f"grads_scatter_start{tag(l, group)}")
        return scatters[l, group]["token"][0, 0]

    loss, grad_x, gs = _device_step(x[0], mem[0], loss_target[0], small, get_weights, put_grads)

    recv = {}

    def wait_scatter(l, group, after_):
        lands = _exchange_wait("scatter", scatters[l, group], pick(places(l), group), after_, name=f"grads_scatter_wait{tag(l, group)}")
        recv.setdefault(l, {}).update(zip(group, lands))

    for l in reversed(range(1, DEPTH)):
        wait_scatter(l, everything, grad_x)
    wait_scatter(0, scatter_groups[0][0], grad_x)
    s_pack = _pack([gs[k] for k in SMALL])
    (s_recv,) = _all_gather([s_pack], [(N_DEV,) + s_pack.shape], [_slot0], name="small_grads_all_gather")

    layer_tensors = ("w_in", "w_mem_kv", "w_out", "w_up", "w_down", "conv_w")
    layer_parts = lambda k: [recv[l][layer_tensors.index(k)] for l in range(DEPTH)]
    to_local = {k: (lambda t: t) for k in big}
    to_local["w_in_fox"] = _fox_permute
    to_local["w_in_dil"] = to_local["w_up"] = transposed
    from_local = {k: (lambda t: t) for k in big}
    from_local["w_in_fox"] = _fox_unpermute
    from_local["w_in_dil"] = from_local["w_up"] = transposed
    blocks = {"w_in_fox": rows["w_in_fox"], "w_in_dil": dil_c, "w_mem_kv": rows["w_mem_kv"], "w_out": rows["w_out"], "w_up": FF_CHUNK // 4,
              "w_down": rows["w_down"] // 2, "conv_w": 3}
    outs = {}

    def update(k, parts):
        f = to_local[k]
        outs[k] = [from_local[k](t) for t in _adamw(parts, f(w[k]), f(m[k]), f(v[k]), br=blocks[k], name=f"adamw_{k}")]

    update("w_in_dil", [recv[l][0] for l in range(1, DEPTH, 2)])
    update("w_up", layer_parts("w_up"))
    update("w_down", layer_parts("w_down"))
    update("conv_w", [p[:, :, :3, :] for p in layer_parts("conv_w")])
    update("w_mem_kv", layer_parts("w_mem_kv"))
    update("w_out", layer_parts("w_out"))
    wait_scatter(0, scatter_groups[0][-1], outs["w_out"][1])
    update("w_in_fox", [recv[l][0] for l in range(0, DEPTH, 2)])
    small_outs = _adamw([s_recv], _pack([w[k] for k in SMALL]), _pack([m[k] for k in SMALL]), _pack([v[k] for k in SMALL]),
                        br=s_pack.shape[1], name="adamw_small")
    res = []
    for i, os_ in enumerate(small_outs):
        d = {k: outs[k][i] for k in big}
        d.update(zip(SMALL, _unpack(os_, small_shapes)))
        res.append([d[k] for k in names])
    loss = lax.psum(loss, ("x", "y", "c"))
    return (loss, grad_x[None], *res[0], *res[1], *res[2], *res[3])
```

```python
import functools
import math

import jax
import jax.numpy as jnp
from jax import lax
from jax.experimental import pallas as pl
from jax.experimental.pallas import tpu as pltpu

F32 = jnp.float32
BF16 = jnp.bfloat16

D_MODEL = 1024
HEAD_DIM = 64
N_MIX_HEADS = 12
N_MEM_HEADS = 4
D_MIX = N_MIX_HEADS * HEAD_DIM
D_MEMQ = N_MEM_HEADS * HEAD_DIM
D_FF = 2816
DEPTH = 4
FOX_IN = 3 * D_MIX + N_MIX_HEADS + D_MEMQ
DIL_IN = 3 * D_MIX + D_MEMQ
LANES = 128
FOX_P = DIL_IN + LANES
N_MIX_HP = D_MIX // LANES
N_MEM_HP = D_MEMQ // LANES
QM_COL = 3 * N_MIX_HP
F_COL = DIL_IN // LANES
DILATED_BRANCHES = ((128, 1), (512, 4), (2048, 16))
DIL_L = 128
DIL_UNROLL_FWD = 32
DIL_UNROLL_BWD = 8
ROPE_THETA = 10000.0
NORM_EPS = 1e-6
NEG = -1e30
SCALE = HEAD_DIM ** -0.5
N_DEV = 8

ADAM_LR = 0.001
ADAM_B1 = 0.9
ADAM_B2 = 0.999
ADAM_EPS = 1e-08
ADAM_WD = 0.01
ADAM_STEP = 10

VMEM_LIMIT = 56 * 1024 * 1024
PACK_W = 1024
PACK_ROW_ALIGN = 8

MESH = pl.DeviceIdType.MESH
NT = (((1,), (1,)), ((), ()))
NN = (((1,), (0,)), ((), ()))
TN = (((0,), (0,)), ((), ()))


def _params(*sem):
    return pltpu.CompilerParams(dimension_semantics=sem, vmem_limit_bytes=VMEM_LIMIT)


def _lane_lo(shape):
    return lax.broadcasted_iota(jnp.int32, shape, len(shape) - 1) < HEAD_DIM


def _pair(lo, a, b):
    return jnp.where(lo, a, b)


def _mm(a, b, mode, *, tm, tn, name, out_dtype=F32, res=None, layer=None, chunk=None, norm_gain=None, norm_bwd=None, rope=None):
    lead = () if layer is None else (layer,)
    nl = (None,) * len(lead)
    bs = b.shape[len(lead):]
    dims = {"nn": NN, "nt": NT, "tn": TN}[mode]
    reduce_n = 0
    if chunk is None:
        (m, k) = a.shape[::-1] if mode == "tn" else a.shape
        n = bs[0] if mode == "nt" else bs[1]
        grid = (m // tm, n // tn)
        a_spec = pl.BlockSpec((k, tm), lambda i, j: (0, i)) if mode == "tn" else pl.BlockSpec((tm, k), lambda i, j: (i, 0))
        b_spec = pl.BlockSpec(nl + ((tn, k) if mode == "nt" else (k, tn)), lambda i, j: lead + ((j, 0) if mode == "nt" else (0, j)))
        o_spec = pl.BlockSpec((tm, tn), lambda i, j: (i, j))
        out_shape = (m, n)
    elif chunk == "b":
        (m, k) = a.shape[::-1] if mode == "tn" else a.shape
        c, nc = bs[0], (bs[1] if mode == "nt" else bs[2])
        grid = (m // tm, c)
        a_spec = pl.BlockSpec((k, tm), lambda i, j: (0, i)) if mode == "tn" else pl.BlockSpec((tm, k), lambda i, j: (i, 0))
        b_spec = pl.BlockSpec(nl + (None,) + tuple(bs[1:]), lambda i, j: lead + (j, 0, 0))
        o_spec = pl.BlockSpec((None, tm, nc), lambda i, j: (j, i, 0))
        out_shape = (c, m, nc)
    elif chunk == "a":
        assert mode == "tn"
        c, k, mc = a.shape
        n = bs[1]
        grid = (c, n // tn)
        a_spec = pl.BlockSpec((None, k, mc), lambda i, j: (i, 0, 0))
        b_spec = pl.BlockSpec(nl + (k, tn), lambda i, j: lead + (0, j))
        o_spec = pl.BlockSpec((None, mc, tn), lambda i, j: (i, 0, j))
        out_shape = (c, mc, n)
    else:
        reduce_n, m, kc = a.shape
        n = bs[1] if mode == "nt" else bs[2]
        grid = (m // tm, n // tn)
        a_spec = pl.BlockSpec((reduce_n, tm, kc), lambda i, j: (0, i, 0))
        b_spec = pl.BlockSpec(nl + ((reduce_n, tn, kc) if mode == "nt" else (reduce_n, kc, tn)),
                              lambda i, j: lead + ((0, j, 0) if mode == "nt" else (0, 0, j)))
        o_spec = pl.BlockSpec((tm, tn), lambda i, j: (i, j))
        out_shape = (m, n)

    if norm_gain is not None or norm_bwd is not None:
        assert chunk in (None, "reduce") and tn == n, "the RMSNorm of the result needs whole rows in a block"

    def body(*refs):
        a_ref, b_ref = refs[0], refs[1]
        dot = lambda x, y: lax.dot_general(x.astype(BF16), y.astype(BF16), dims, preferred_element_type=F32)
        if reduce_n:
            acc = dot(a_ref[0], b_ref[0])
            for r in range(1, reduce_n):
                acc = acc + dot(a_ref[r], b_ref[r])
        else:
            acc = dot(a_ref[...], b_ref[...])
        if norm_bwd is not None:
            x_ref, g_ref, r_ref = refs[2:5]
            dx_ref, dxb_ref, dg_ref = refs[-3:]
            dx, dg = _rms_bwd_math(x_ref[...], acc, g_ref[...])
            dx = dx + r_ref[...]
            dx_ref[...] = dx
            dxb_ref[...] = dx.astype(BF16)

            @pl.when(pl.program_id(0) == 0)
            def _():
                dg_ref[...] = jnp.zeros_like(dg_ref)

            dg_ref[0:1, :] += dg
            return
        o_ref = refs[-2] if (norm_gain is not None or rope is not None) else refs[-1]
        if res is not None:
            acc = acc + refs[2][...]
        o_ref[...] = acc.astype(o_ref.dtype)
        if norm_gain is not None:
            rs = lax.rsqrt(jnp.mean(acc * acc, axis=-1, keepdims=True) + NORM_EPS)
            refs[-1][...] = (acc * rs * refs[3][...]).astype(BF16)
        if rope is not None:
            @pl.when(pl.program_id(1) < rope[2] // tn)
            def _():
                cos, sin = refs[2][...], refs[3][...]
                for c in range(tn // LANES):
                    lanes = slice(c * LANES, (c + 1) * LANES)
                    refs[-1][:, lanes] = _rope_tile(acc[:, lanes], cos, sin)

    ins = [a, b] + ([res] if res is not None else [])
    specs = [a_spec, b_spec] + ([o_spec] if res is not None else [])
    out_shapes, out_specs = jax.ShapeDtypeStruct(out_shape, out_dtype), o_spec
    sem = ("parallel", "parallel")
    if rope is not None:
        assert chunk is None and res is None and norm_gain is None and norm_bwd is None and rope[2] % tn == 0
        tab = pl.BlockSpec((tm, LANES), lambda i, j: (i, 0))
        ins += [rope[0], rope[1]]
        specs += [tab, tab]
        r_spec = pl.BlockSpec((tm, tn), lambda i, j: (i, jnp.minimum(j, rope[2] // tn - 1)))
        out_shapes, out_specs = (out_shapes, jax.ShapeDtypeStruct((m, rope[2]), F32)), (o_spec, r_spec)
        sem = ("parallel", "arbitrary")
    if norm_gain is not None:
        assert res is not None
        ins.append(norm_gain)
        specs.append(pl.BlockSpec((1, n), lambda i, j: (0, 0)))
        out_shapes, out_specs = (out_shapes, jax.ShapeDtypeStruct(out_shape, BF16)), (o_spec, o_spec)
    if norm_bwd is not None:
        assert res is None and norm_gain is None
        x_in, gain, resid = norm_bwd
        ins += [x_in, gain, resid]
        specs += [o_spec, pl.BlockSpec((1, n), lambda i, j: (0, 0)), o_spec]
        out_shapes = (jax.ShapeDtypeStruct(out_shape, F32), jax.ShapeDtypeStruct(out_shape, BF16), jax.ShapeDtypeStruct((8, n), F32))
        out_specs = (o_spec, o_spec, pl.BlockSpec((8, n), lambda i, j: (0, 0)))
        sem = ("arbitrary", "arbitrary")
    return pl.pallas_call(body, out_shape=out_shapes, grid=grid, in_specs=specs, out_specs=out_specs,
                          compiler_params=_params(*sem), name=name)(*ins)


def _rmsnorm_fwd(x, g, *, br, name):
    r, d = x.shape

    def body(x_ref, g_ref, o_ref):
        xf = x_ref[...]
        rs = lax.rsqrt(jnp.mean(xf * xf, axis=-1, keepdims=True) + NORM_EPS)
        o_ref[...] = (xf * rs * g_ref[...]).astype(BF16)

    return pl.pallas_call(body, out_shape=jax.ShapeDtypeStruct((r, d), BF16), grid=(r // br,),
                          in_specs=[pl.BlockSpec((br, d), lambda i: (i, 0)), pl.BlockSpec((1, d), lambda i: (0, 0))],
                          out_specs=pl.BlockSpec((br, d), lambda i: (i, 0)), compiler_params=_params("parallel"), name=name)(x, g)


def _rms_bwd_math(x, dy, g):
    d = x.shape[-1]
    rs = lax.rsqrt(jnp.mean(x * x, axis=-1, keepdims=True) + NORM_EPS)
    gy = dy * g
    proj = jnp.sum(x * gy, axis=-1, keepdims=True) * (1.0 / d)
    dx = rs * gy - x * (rs * rs * rs) * proj
    dg = jnp.sum(dy * (x * rs), axis=0, keepdims=True)
    return dx, dg


def _rmsnorm_bwd(x, dy, g, res, *, br, name):
    r, d = x.shape
    has_res = res is not None

    def body(*refs):
        x_ref, dy_ref, g_ref = refs[:3]
        dx_ref, dxb_ref, dg_ref = refs[-3:]
        dx, dg = _rms_bwd_math(x_ref[...], dy_ref[...], g_ref[...])
        if has_res:
            dx = dx + refs[3][...]
        dx_ref[...] = dx
        dxb_ref[...] = dx.astype(BF16)

        @pl.when(pl.program_id(0) == 0)
        def _():
            dg_ref[...] = jnp.zeros_like(dg_ref)

        dg_ref[0:1, :] += dg

    row = pl.BlockSpec((br, d), lambda i: (i, 0))
    ins = [x, dy, g] + ([res] if has_res else [])
    specs = [row, row, pl.BlockSpec((1, d), lambda i: (0, 0))] + ([row] if has_res else [])
    return pl.pallas_call(
        body, out_shape=(jax.ShapeDtypeStruct((r, d), F32), jax.ShapeDtypeStruct((r, d), BF16), jax.ShapeDtypeStruct((8, d), F32)),
        grid=(r // br,), in_specs=specs, out_specs=(row, row, pl.BlockSpec((8, d), lambda i: (0, 0))),
        compiler_params=_params("arbitrary"), name=name)(*ins)


def _loss_head(h, target, g, *, br, name):
    r, d = h.shape

    def body(x_ref, t_ref, g_ref, dx_ref, dxb_ref, dg_ref, loss_ref):
        x = x_ref[...]
        gg = g_ref[...]
        rs = lax.rsqrt(jnp.mean(x * x, axis=-1, keepdims=True) + NORM_EPS)
        err = x * rs * gg - t_ref[...]
        part = jnp.sum(jnp.sum(err * err, axis=1, keepdims=True), axis=0, keepdims=True) * (0.5 / d)
        dx, dg = _rms_bwd_math(x, err * (1.0 / d), gg)
        dx_ref[...] = dx
        dxb_ref[...] = dx.astype(BF16)

        @pl.when(pl.program_id(0) == 0)
        def _():
            dg_ref[...] = jnp.zeros_like(dg_ref)
            loss_ref[...] = jnp.zeros_like(loss_ref)

        dg_ref[0:1, :] += dg
        loss_ref[...] += jnp.broadcast_to(part, loss_ref.shape)

    row = pl.BlockSpec((br, d), lambda i: (i, 0))
    return pl.pallas_call(
        body, out_shape=(jax.ShapeDtypeStruct((r, d), F32), jax.ShapeDtypeStruct((r, d), BF16),
                         jax.ShapeDtypeStruct((8, d), F32), jax.ShapeDtypeStruct((8, LANES), F32)),
        grid=(r // br,), in_specs=[row, row, pl.BlockSpec((1, d), lambda i: (0, 0))],
        out_specs=(row, row, pl.BlockSpec((8, d), lambda i: (0, 0)), pl.BlockSpec((8, LANES), lambda i: (0, 0))),
        compiler_params=_params("arbitrary"), name=name)(h, target, g)


def _split3(x):
    hi = x.astype(BF16)
    r1 = x - hi.astype(F32)
    mid = r1.astype(BF16)
    lo = (r1 - mid.astype(F32)).astype(BF16)
    return hi, mid, lo


def _tri_sum(tri, x):
    hi, mid, lo = _split3(x)
    dot = lambda t: jnp.dot(tri, t, preferred_element_type=F32)
    return dot(hi) + dot(mid) + dot(lo)


def _forget_cumsum(proj, b_pad, *, name):
    s = proj.shape[0]
    blk = LANES

    def body(f_ref, b_ref, c_ref):
        ri = lax.broadcasted_iota(jnp.int32, (blk, blk), 0)
        ci = lax.broadcasted_iota(jnp.int32, (blk, blk), 1)
        tri = (ci <= ri).astype(BF16)
        bias = b_ref[...]

        def step(t, carry):
            rows = pl.ds(pl.multiple_of(t * blk, blk), blk)
            z = f_ref[rows, :] + bias
            lf = jnp.minimum(z, 0.0) - jnp.log(1.0 + jnp.exp(-jnp.abs(z)))
            cs = _tri_sum(tri, lf) + carry
            c_ref[rows, :] = cs
            return cs[blk - 1:blk, :]

        lax.fori_loop(0, s // blk, step, jnp.zeros((1, blk), F32))

    return pl.pallas_call(body, out_shape=jax.ShapeDtypeStruct((s, LANES), F32), grid=(1,),
                          in_specs=[pl.BlockSpec((s, LANES), lambda i: (0, F_COL)), pl.BlockSpec((1, LANES), lambda i: (0, 0))],
                          out_specs=pl.BlockSpec((s, LANES), lambda i: (0, 0)), compiler_params=_params("arbitrary"), name=name)(proj, b_pad)


def _forget_cumsum_bwd(proj, b_pad, dcq, dck, *, name):
    s = proj.shape[0]
    blk = LANES
    nblk = s // blk

    def body(f_ref, b_ref, dcq_ref, dck_ref, dz_ref, db_ref):
        ri = lax.broadcasted_iota(jnp.int32, (blk, blk), 0)
        ci = lax.broadcasted_iota(jnp.int32, (blk, blk), 1)
        triu = (ci >= ri).astype(BF16)
        bias = b_ref[...]

        def step(t, carry):
            tail, dbs = carry
            rows = pl.ds(pl.multiple_of((nblk - 1 - t) * blk, blk), blk)
            dc = dcq_ref[rows, :] - dck_ref[rows, :]
            dlf = _tri_sum(triu, dc) + tail
            z = f_ref[rows, :] + bias
            e = jnp.exp(-jnp.abs(z))
            sig_neg = jnp.where(z >= 0.0, e, 1.0) / (1.0 + e)
            dz = dlf * sig_neg
            dz_ref[rows, :] = dz.astype(BF16)
            return dlf[0:1, :], dbs + jnp.sum(dz, axis=0, keepdims=True)

        _, dbs = lax.fori_loop(0, nblk, step, (jnp.zeros((1, blk), F32), jnp.zeros((1, blk), F32)))
        db_ref[...] = jnp.broadcast_to(dbs, db_ref.shape)

    full = pl.BlockSpec((s, LANES), lambda i: (0, 0))
    return pl.pallas_call(body, out_shape=(jax.ShapeDtypeStruct((s, LANES), BF16), jax.ShapeDtypeStruct((8, LANES), F32)), grid=(1,),
                          in_specs=[pl.BlockSpec((s, LANES), lambda i: (0, F_COL)), pl.BlockSpec((1, LANES), lambda i: (0, 0)), full, full],
                          out_specs=(full, pl.BlockSpec((8, LANES), lambda i: (0, 0))), compiler_params=_params("arbitrary"), name=name)(proj, b_pad, dcq, dck)


def _attn_fwd(q_arr, kv_arr, ck6, *, q_col, k_col, v_col, n_hp, causal, bq, bk, name, heads=None, heads_col=0):
    s = q_arr.shape[0]
    skv = kv_arr.shape[0]
    bias = ck6 is not None
    nq = s // bq
    assert not causal or bq == bk

    def body(*refs):
        q_ref, k_ref, v_ref = refs[:3]
        ck_ref = refs[3] if bias else None
        o_ref, lse_ref, heads_ref = refs[-3:]
        i = pl.program_id(1)
        lo = _lane_lo((bq, LANES))
        q = q_ref[...] * SCALE
        qh = (jnp.where(lo, q, 0.0).astype(BF16), jnp.where(lo, 0.0, q).astype(BF16))

        def tile(qs, start, width, carry, diagonal):
            ks = pl.ds(start, width)
            k = k_ref[ks, :].astype(BF16)
            v = v_ref[ks, :].astype(BF16)
            if diagonal:
                shape = (qs[0].shape[0], width)
                ok = lax.broadcasted_iota(jnp.int32, shape, 1) <= lax.broadcasted_iota(jnp.int32, shape, 0)
            out = []
            for h in range(2):
                m, l, acc = carry[3 * h:3 * h + 3]
                sc = lax.dot_general(qs[h], k, NT, preferred_element_type=F32)
                if bias:
                    sc = sc - ck_ref[0, h:h + 1, ks]
                if diagonal:
                    sc = jnp.where(ok, sc, NEG)
                mn = jnp.maximum(m, jnp.max(sc, axis=1, keepdims=True))
                p = jnp.exp(sc - mn)
                al = jnp.exp(m - mn)
                out += [mn, al * l + jnp.sum(p, axis=1, keepdims=True), al * acc + jnp.dot(p.astype(BF16), v, preferred_element_type=F32)]
            return tuple(out)

        def finish(rows, carry):
            m0, l0, a0, m1, l1, a1 = carry
            lo_r = _lane_lo(a0.shape)
            out = _pair(lo_r, a0 / l0, a1 / l1)
            o_ref[rows, :] = out
            lse_ref[0, rows, :] = _pair(lo_r, m0 + jnp.log(l0), m1 + jnp.log(l1))
            heads_ref[rows, :] = out.astype(BF16)

        col = lambda v_: jnp.full((bq, 1), v_, F32)
        init = (col(NEG), col(0.0), jnp.zeros((bq, LANES), F32)) * 2
        n_full = i if causal else skv // bk
        carry = lax.fori_loop(0, n_full, lambda j, c: tile(qh, pl.multiple_of(j * bk, bk), bk, c, False), init)
        if not causal:
            finish(pl.ds(0, bq), carry)
        else:
            hq = bq // 2
            base = pl.multiple_of(i * bk, bk)
            top = tile(tuple(x[:hq] for x in qh), base, hq, tuple(c[:hq] for c in carry), True)
            finish(pl.ds(0, hq), top)
            q_bot = tuple(x[hq:] for x in qh)
            bot = tile(q_bot, base, hq, tuple(c[hq:] for c in carry), False)
            bot = tile(q_bot, pl.multiple_of(base + hq, hq), hq, bot, True)
            finish(pl.ds(hq, hq), bot)

    specs = [pl.BlockSpec((bq, LANES), lambda h, i: (i, q_col + h)),
             pl.BlockSpec((skv, LANES), lambda h, i: (0, k_col + h)),
             pl.BlockSpec((skv, LANES), lambda h, i: (0, v_col + h))]
    ins = [q_arr, kv_arr, kv_arr]
    if bias:
        specs += [pl.BlockSpec((1, 8, skv), lambda h, i: (h, 0, 0))]
        ins += [ck6]
    aliases = {}
    if heads is not None:
        aliases = {len(ins): 2}
        specs += [pl.BlockSpec(memory_space=pl.ANY)]
        ins += [heads]
    return pl.pallas_call(
        body, out_shape=(jax.ShapeDtypeStruct((s, n_hp * LANES), F32), jax.ShapeDtypeStruct((n_hp, s, LANES), F32),
                         jax.ShapeDtypeStruct((s, D_MODEL), BF16)),
        grid=(n_hp, nq), in_specs=specs,
        out_specs=(pl.BlockSpec((bq, LANES), lambda h, i: (i, h)), pl.BlockSpec((1, bq, LANES), lambda h, i: (h, i, 0)),
                   pl.BlockSpec((bq, LANES), lambda h, i: (i, heads_col + h))),
        input_output_aliases=aliases, compiler_params=_params("parallel", "parallel"), name=name)(*ins)


def _attn_bwd(q_arr, kv_arr, o_arr, do_arr, lse, ck6, *, q_col, k_col, v_col, o_col, n_hp, causal, bq, bk, name):
    s = q_arr.shape[0]
    skv = kv_arr.shape[0]
    bias = ck6 is not None
    nq = s // bq
    assert not causal or bq == bk

    def body(*refs):
        q_ref, k_ref, v_ref, o_ref, do_ref, lse_ref = refs[:6]
        if bias:
            ck_ref = refs[6]
            dq_ref, dk_ref, dv_ref, dcq_ref, dck_ref = refs[-5:]
        else:
            dq_ref, dk_ref, dv_ref = refs[-3:]
        j = pl.program_id(1)
        lo_q = _lane_lo((bq, LANES))
        lo_k = _lane_lo((bk, LANES))
        k = k_ref[...]
        v = v_ref[...].astype(BF16)
        kb = k.astype(BF16)
        kh = (jnp.where(lo_k, k, 0.0).astype(BF16), jnp.where(lo_k, 0.0, k).astype(BF16))
        if bias:
            pick_k = [(lax.broadcasted_iota(jnp.int32, (8, bk), 0) == h).astype(BF16) for h in range(2)]
            pick_q = [(lax.broadcasted_iota(jnp.int32, (8, bq), 0) == h).astype(BF16) for h in range(2)]

        @pl.when(j == 0)
        def _():
            dq_ref[...] = jnp.zeros_like(dq_ref)
            if bias:
                dcq_ref[...] = jnp.zeros_like(dcq_ref)

        def block(i, carry, diagonal):
            dk_acc, dv_acc, cs = carry
            qs = pl.ds(pl.multiple_of(i * bq, bq), bq)
            q = q_ref[qs, :] * SCALE
            do = do_ref[qs, :]
            dd = do * o_ref[qs, :]
            lse_i = lse_ref[0, qs, :]
            qh = (jnp.where(lo_q, q, 0.0).astype(BF16), jnp.where(lo_q, 0.0, q).astype(BF16))
            doh = (jnp.where(lo_q, do, 0.0).astype(BF16), jnp.where(lo_q, 0.0, do).astype(BF16))
            dh = (jnp.sum(jnp.where(lo_q, dd, 0.0), axis=1, keepdims=True), jnp.sum(jnp.where(lo_q, 0.0, dd), axis=1, keepdims=True))
            if diagonal:
                ok = lax.broadcasted_iota(jnp.int32, (bq, bk), 1) <= lax.broadcasted_iota(jnp.int32, (bq, bk), 0)
            dq_blk = None
            rs = None
            for h in range(2):
                sc = lax.dot_general(qh[h], kb, NT, preferred_element_type=F32)
                if bias:
                    sc = sc - ck_ref[0, h:h + 1, :]
                if diagonal:
                    sc = jnp.where(ok, sc, NEG)
                p = jnp.exp(sc - lse_i[:, h * HEAD_DIM:h * HEAD_DIM + 1])
                ds = p * (lax.dot_general(doh[h], v, NT, preferred_element_type=F32) - dh[h])
                dsb = ds.astype(BF16)
                dv_acc = dv_acc + lax.dot_general(p.astype(BF16), doh[h], TN, preferred_element_type=F32)
                dk_acc = dk_acc + lax.dot_general(dsb, qh[h], TN, preferred_element_type=F32)
                part = jnp.dot(dsb, kh[h], preferred_element_type=F32)
                dq_blk = part if dq_blk is None else dq_blk + part
                if bias:
                    cs = cs + jnp.dot(pick_q[h], dsb, preferred_element_type=F32)
                    row_sums = lax.dot_general(pick_k[h], dsb, NT, preferred_element_type=F32)
                    rs = row_sums if rs is None else rs + row_sums
            dq_ref[qs, :] += dq_blk * SCALE
            if bias:
                dcq_ref[0, :, qs] += rs
            return dk_acc, dv_acc, cs

        carry = (jnp.zeros((bk, LANES), F32), jnp.zeros((bk, LANES), F32), jnp.zeros((8, bk), F32))
        if causal:
            carry = block(j, carry, True)
        dk_acc, dv_acc, cs = lax.fori_loop(j + 1 if causal else 0, nq, functools.partial(block, diagonal=False), carry)
        dk_ref[...] = dk_acc.astype(BF16)
        dv_ref[...] = dv_acc.astype(BF16)
        if bias:
            dck_ref[0] = cs

    full_q = lambda c: pl.BlockSpec((s, LANES), lambda h, j: (0, c + h))
    specs = [full_q(q_col),
             pl.BlockSpec((bk, LANES), lambda h, j: (j, k_col + h)),
             pl.BlockSpec((bk, LANES), lambda h, j: (j, v_col + h)),
             full_q(0), full_q(o_col),
             pl.BlockSpec((1, s, LANES), lambda h, j: (h, 0, 0))]
    ins = [q_arr, kv_arr, kv_arr, o_arr, do_arr, lse]
    out_shape = [jax.ShapeDtypeStruct((s, n_hp * LANES), F32), jax.ShapeDtypeStruct((skv, n_hp * LANES), BF16),
                 jax.ShapeDtypeStruct((skv, n_hp * LANES), BF16)]
    out_specs = [full_q(0), pl.BlockSpec((bk, LANES), lambda h, j: (j, h)), pl.BlockSpec((bk, LANES), lambda h, j: (j, h))]
    if bias:
        specs += [pl.BlockSpec((1, 8, bk), lambda h, j: (h, 0, j))]
        ins += [ck6]
        out_shape += [jax.ShapeDtypeStruct((n_hp, 8, s), F32), jax.ShapeDtypeStruct((n_hp, 8, skv), F32)]
        out_specs += [pl.BlockSpec((1, 8, s), lambda h, j: (h, 0, 0)), pl.BlockSpec((1, 8, bk), lambda h, j: (h, 0, j))]
    return pl.pallas_call(body, out_shape=tuple(out_shape), grid=(n_hp, skv // bk), in_specs=specs, out_specs=tuple(out_specs),
                          compiler_params=_params("parallel", "arbitrary"), name=name)(*ins)


def _rope_tables(s):
    inv = 1.0 / (ROPE_THETA ** (jnp.arange(0, HEAD_DIM, 2, dtype=F32) / HEAD_DIM))
    ang = jnp.arange(s, dtype=F32)[:, None] * inv[None, :]
    cos, sin = jnp.cos(ang), jnp.sin(ang)
    return jnp.tile(cos, (1, 4)), jnp.concatenate([-sin, sin, -sin, sin], axis=1)


def _rope_tile(x, cos, sin):
    first = (lax.broadcasted_iota(jnp.int32, x.shape, 1) % HEAD_DIM) < (HEAD_DIM // 2)
    swapped = jnp.where(first, pltpu.roll(x, LANES - HEAD_DIM // 2, 1), pltpu.roll(x, HEAD_DIM // 2, 1))
    return x * cos + swapped * sin


def _rope(x_arr, cos_t, sin_t, *, n_cols, out_dtype, br, name):
    s = x_arr.shape[0]

    def body(x_ref, c_ref, s_ref, o_ref):
        cos, sin = c_ref[...], s_ref[...]
        for j in range(n_cols):
            lanes = slice(j * LANES, (j + 1) * LANES)
            o_ref[:, lanes] = _rope_tile(x_ref[:, lanes].astype(F32), cos, sin).astype(o_ref.dtype)

    tab = pl.BlockSpec((br, LANES), lambda i: (i, 0))
    blk = pl.BlockSpec((br, n_cols * LANES), lambda i: (i, 0))
    return pl.pallas_call(body, out_shape=jax.ShapeDtypeStruct((s, n_cols * LANES), out_dtype), grid=(s // br,),
                          in_specs=[blk, tab, tab], out_specs=blk, compiler_params=_params("parallel"), name=name)(x_arr, cos_t, sin_t)


def _stack_heads(x):
    lo = _lane_lo(x.shape)
    return jnp.concatenate([jnp.where(lo, x, 0.0), jnp.where(lo, 0.0, x)], axis=0).astype(BF16)


def _unstack_heads(x):
    return jnp.where(_lane_lo((DIL_L, LANES)), x[:DIL_L], x[DIL_L:])


def _dil_scores(q_ref, k_ref, cur, prev, has_prev):
    qs = _stack_heads(q_ref[cur, :] * SCALE)
    kk = jnp.concatenate([k_ref[prev, :], k_ref[cur, :]], axis=0).astype(BF16)
    a = lax.broadcasted_iota(jnp.int32, (2 * DIL_L, 2 * DIL_L), 0) & (DIL_L - 1)
    c = lax.broadcasted_iota(jnp.int32, (2 * DIL_L, 2 * DIL_L), 1)
    ok = ((c < DIL_L) & (c >= a) & has_prev) | ((c >= DIL_L) & (c - DIL_L <= a))
    return qs, kk, jnp.where(ok, lax.dot_general(qs, kk, NT, preferred_element_type=F32), NEG)


def _dil_rows(t, dil):
    r, m = t % dil, t // dil
    start = m * (DIL_L * dil) + r
    prev = jnp.maximum(start - DIL_L * dil, 0)
    return pl.ds(start, DIL_L, stride=dil), pl.ds(prev, DIL_L, stride=dil), m > 0


def _softmax3(a, b, c):
    m = jnp.maximum(jnp.maximum(a, b), c)
    ea, eb, ec = jnp.exp(a - m), jnp.exp(b - m), jnp.exp(c - m)
    den = ea + eb + ec
    inv = 1.0 / den
    return ea * inv, eb * inv, ec * inv, m + jnp.log(den)


def _dil_fwd(qk_r, proj, *, name):
    s = qk_r.shape[0]
    nsub = s // DIL_L
    mb = 512

    def body(q_ref, k_ref, v_ref, mix_ref, l1_ref, l2_ref, l3_ref, heads_ref, o1_scr, o2_scr, o3_scr):
        for (_, dil), o_scr, l_ref in zip(DILATED_BRANCHES, (o1_scr, o2_scr, o3_scr), (l1_ref, l2_ref, l3_ref)):
            def step(t, carry, dil=dil, o_scr=o_scr, l_ref=l_ref):
                cur, prev, has_prev = _dil_rows(t, dil)
                _, _, sc = _dil_scores(q_ref, k_ref, cur, prev, has_prev)
                vv = jnp.concatenate([v_ref[prev, :], v_ref[cur, :]], axis=0).astype(BF16)
                m = jnp.max(sc, axis=1, keepdims=True)
                e = jnp.exp(sc - m)
                den = jnp.sum(e, axis=1, keepdims=True)
                o = jnp.dot((e * (1.0 / den)).astype(BF16), vv, preferred_element_type=F32)
                o_scr[cur, :] = _unstack_heads(o)
                l_ref[cur, :] = _unstack_heads(jnp.broadcast_to(m + jnp.log(den), (2 * DIL_L, LANES)))
                return carry

            lax.fori_loop(0, nsub, step, 0, unroll=min(DIL_UNROLL_FWD, nsub))

        def merge(i, carry):
            rows = pl.ds(pl.multiple_of(i * mb, mb), mb)
            wa, wb, wc, _ = _softmax3(l1_ref[rows, :], l2_ref[rows, :], l3_ref[rows, :])
            mix = wa * o1_scr[rows, :] + wb * o2_scr[rows, :] + wc * o3_scr[rows, :]
            mix_ref[rows, :] = mix
            heads_ref[rows, :] = mix.astype(BF16)
            return carry

        lax.fori_loop(0, s // mb, merge, 0)

    col = lambda arr_col: pl.BlockSpec((s, LANES), lambda h: (0, arr_col + h))
    shp = jax.ShapeDtypeStruct((s, D_MIX), F32)
    mix, l1, l2, l3, heads = pl.pallas_call(
        body, out_shape=(shp, shp, shp, shp, jax.ShapeDtypeStruct((s, D_MODEL), BF16)), grid=(N_MIX_HP,),
        in_specs=[col(0), col(N_MIX_HP), col(2 * N_MIX_HP)],
        out_specs=(col(0),) * 5, scratch_shapes=[pltpu.VMEM((s, LANES), F32)] * 3,
        compiler_params=_params("parallel"), name=name)(qk_r, qk_r, proj)
    return mix, (l1, l2, l3), heads


def _dil_bwd(qk_r, proj, mix, dheads, lses, *, name):
    s = qk_r.shape[0]
    nsub = s // DIL_L
    mb = 512

    def body(q_ref, k_ref, v_ref, mix_ref, dm_ref, l1_ref, l2_ref, l3_ref, dq_ref, dk_ref, dv_ref, lt_scr, dd_scr):
        lo = _lane_lo((DIL_L, LANES))
        lo_m = _lane_lo((mb, LANES))

        def prep(i, carry):
            rows = pl.ds(pl.multiple_of(i * mb, mb), mb)
            _, _, _, lt = _softmax3(l1_ref[rows, :], l2_ref[rows, :], l3_ref[rows, :])
            lt_scr[rows, :] = lt
            dd = dm_ref[rows, :] * mix_ref[rows, :]
            dd_scr[rows, :] = _pair(lo_m, jnp.sum(jnp.where(lo_m, dd, 0.0), axis=1, keepdims=True),
                                    jnp.sum(jnp.where(lo_m, 0.0, dd), axis=1, keepdims=True))
            zero = jnp.zeros((mb, LANES), F32)
            dq_ref[rows, :] = zero
            dk_ref[rows, :] = zero
            dv_ref[rows, :] = zero
            return carry

        lax.fori_loop(0, s // mb, prep, 0)

        for (_, dil), l_ref in zip(DILATED_BRANCHES, (l1_ref, l2_ref, l3_ref)):
            def step(t, carry, dil=dil, l_ref=l_ref):
                cur, prev, has_prev = _dil_rows(t, dil)
                qs, kk, sc = _dil_scores(q_ref, k_ref, cur, prev, has_prev)
                vv = jnp.concatenate([v_ref[prev, :], v_ref[cur, :]], axis=0).astype(BF16)
                lg = l_ref[cur, :]
                w = jnp.exp(lg - lt_scr[cur, :])
                wd = w * dd_scr[cur, :]
                column = lambda x: jnp.concatenate([x[:, 0:1], x[:, HEAD_DIM:HEAD_DIM + 1]], axis=0)
                dos = _stack_heads(w * dm_ref[cur, :])
                p = jnp.exp(sc - column(lg))
                ds = (p * (lax.dot_general(dos, vv, NT, preferred_element_type=F32) - column(wd))).astype(BF16)
                dq_ref[cur, :] += _unstack_heads(jnp.dot(ds, kk, preferred_element_type=F32)) * SCALE
                dkk = lax.dot_general(ds, qs, TN, preferred_element_type=F32)
                dvv = lax.dot_general(p.astype(BF16), dos, TN, preferred_element_type=F32)
                dk_ref[cur, :] += dkk[DIL_L:]
                dv_ref[cur, :] += dvv[DIL_L:]
                dk_ref[prev, :] += dkk[:DIL_L]
                dv_ref[prev, :] += dvv[:DIL_L]
                return carry

            lax.fori_loop(0, nsub, step, 0, unroll=DIL_UNROLL_BWD)

    col = lambda arr_col: pl.BlockSpec((s, LANES), lambda h: (0, arr_col + h))
    shp = jax.ShapeDtypeStruct((s, D_MIX), F32)
    return pl.pallas_call(
        body, out_shape=(shp, shp, shp), grid=(N_MIX_HP,),
        in_specs=[col(0), col(N_MIX_HP), col(2 * N_MIX_HP), col(0), col(0), col(0), col(0), col(0)], out_specs=(col(0),) * 3,
        scratch_shapes=[pltpu.VMEM((s, LANES), F32)] * 2,
        compiler_params=_params("parallel"), name=name)(qk_r, qk_r, proj, mix, dheads, *lses)


CONV_BR = 512
FF_CHUNK = 2 * D_FF // N_DEV
FF_HALF = N_DEV // 2
HALO = 8


def _shift_down(x, halo, k):
    row = lax.broadcasted_iota(jnp.int32, x.shape, 0)
    y = pltpu.roll(x, k, 0)
    for r in range(k):
        y = jnp.where(row == r, halo[HALO - k + r:HALO - k + r + 1, :], y)
    return y


def _shift_up(x, halo, k):
    n = x.shape[0]
    row = lax.broadcasted_iota(jnp.int32, x.shape, 0)
    y = pltpu.roll(x, n - k, 0)
    for r in range(k):
        y = jnp.where(row == n - k + r, halo[r:r + 1, :], y)
    return y


def _conv_vals(u, halo, w, b):
    s1 = _shift_down(u, halo, 1)
    s2 = _shift_down(u, halo, 2)
    return b + w[0:1, :] * s2 + w[1:2, :] * s1 + w[2:3, :] * u, s1, s2


def _conv_in_specs(order, layer):
    rc = (lambda i, j: (i, j)) if order == "rc" else (lambda j, i: (i, j))
    per = CONV_BR // HALO
    main = lambda off: pl.BlockSpec((None, CONV_BR, FF_CHUNK), lambda *g: (off + rc(*g)[1], rc(*g)[0], 0))
    halo = lambda off: pl.BlockSpec((None, HALO, FF_CHUNK), lambda *g: (off + rc(*g)[1], jnp.maximum(rc(*g)[0] * per - 1, 0), 0))
    wspec = lambda off: pl.BlockSpec((None, None, 3, FF_CHUNK), lambda *g: (layer, off + rc(*g)[1], 0, 0))
    bspec = lambda off: pl.BlockSpec((None, 1, FF_CHUNK), lambda *g: (off + rc(*g)[1], 0, 0))
    return [main(0), halo(0), main(FF_HALF), halo(FF_HALF), wspec(0), wspec(FF_HALF), bspec(0), bspec(FF_HALF)]


def _conv_fwd(u, cw, cb, layer, *, name):
    s = u.shape[1]

    def body(uv_ref, hv_ref, ug_ref, hg_ref, wv_ref, wg_ref, bv_ref, bg_ref, o_ref):
        first = pl.program_id(0) == 0
        hv = jnp.where(first, 0.0, hv_ref[...])
        hg = jnp.where(first, 0.0, hg_ref[...])
        val, _, _ = _conv_vals(uv_ref[...], hv, wv_ref[...], bv_ref[...])
        gate, _, _ = _conv_vals(ug_ref[...], hg, wg_ref[...], bg_ref[...])
        o_ref[...] = (gate / (1.0 + jnp.exp(-gate)) * val).astype(BF16)

    return pl.pallas_call(body, out_shape=jax.ShapeDtypeStruct((FF_HALF, s, FF_CHUNK), BF16), grid=(s // CONV_BR, FF_HALF),
                          in_specs=_conv_in_specs("rc", layer), out_specs=pl.BlockSpec((None, CONV_BR, FF_CHUNK), lambda i, j: (j, i, 0)),
                          compiler_params=_params("parallel", "parallel"), name=name)(u, u, u, u, cw, cw, cb, cb)


def _swiglu_bwd(val, gate, da):
    sg = 1.0 / (1.0 + jnp.exp(-gate))
    return da * (gate * sg), da * val * (sg * (1.0 + gate * (1.0 - sg)))


def _conv_bwd(u, cw, cb, da, layer, *, name):
    s = u.shape[1]
    nrow = s // CONV_BR
    per = CONV_BR // HALO

    def body(uv_ref, hv_ref, ug_ref, hg_ref, wv_ref, wg_ref, bv_ref, bg_ref, da_ref, nv_ref, ng_ref, nda_ref, du_ref, dwb_ref):
        i = pl.program_id(1)
        first, last = i == 0, i == nrow - 1
        hv = jnp.where(first, 0.0, hv_ref[...])
        hg = jnp.where(first, 0.0, hg_ref[...])
        uv, ug = uv_ref[...], ug_ref[...]
        wv, wg, bv, bg = wv_ref[...], wg_ref[...], bv_ref[...], bg_ref[...]
        val, v1, v2 = _conv_vals(uv, hv, wv, bv)
        gate, g1, g2 = _conv_vals(ug, hg, wg, bg)
        dval, dgate = _swiglu_bwd(val, gate, da_ref[...])
        val_n, _, _ = _conv_vals(nv_ref[...], uv[CONV_BR - HALO:, :], wv, bv)
        gate_n, _, _ = _conv_vals(ng_ref[...], ug[CONV_BR - HALO:, :], wg, bg)
        dval_n, dgate_n = _swiglu_bwd(val_n, gate_n, nda_ref[...])
        dval_n = jnp.where(last, 0.0, dval_n)
        dgate_n = jnp.where(last, 0.0, dgate_n)
        back = lambda dc, dc_n, w: w[2:3, :] * dc + w[1:2, :] * _shift_up(dc, dc_n, 1) + w[0:1, :] * _shift_up(dc, dc_n, 2)
        du_ref[0] = back(dval, dval_n, wv).astype(BF16)
        du_ref[1] = back(dgate, dgate_n, wg).astype(BF16)

        @pl.when(first)
        def _():
            dwb_ref[...] = jnp.zeros_like(dwb_ref)

        cs = lambda t: jnp.sum(t, axis=0, keepdims=True)
        r8 = lax.broadcasted_iota(jnp.int32, (8, FF_CHUNK), 0)
        rows4 = lambda a, b, c, d: jnp.where(r8 == 0, a, jnp.where(r8 == 1, b, jnp.where(r8 == 2, c, jnp.where(r8 == 3, d, 0.0))))
        dwb_ref[0] += rows4(cs(dval * v2), cs(dval * v1), cs(dval * uv), cs(dval))
        dwb_ref[1] += rows4(cs(dgate * g2), cs(dgate * g1), cs(dgate * ug), cs(dgate))

    nxt = lambda off: pl.BlockSpec((None, HALO, FF_CHUNK), lambda j, i: (off + j, jnp.minimum((i + 1) * per, nrow * per - 1), 0))
    specs = _conv_in_specs("cr", layer) + [pl.BlockSpec((None, CONV_BR, FF_CHUNK), lambda j, i: (j, i, 0)), nxt(0), nxt(FF_HALF), nxt(0)]
    return pl.pallas_call(
        body, out_shape=(jax.ShapeDtypeStruct((2, FF_HALF, s, FF_CHUNK), BF16), jax.ShapeDtypeStruct((2, FF_HALF, 8, FF_CHUNK), F32)),
        grid=(FF_HALF, nrow), in_specs=specs,
        out_specs=(pl.BlockSpec((2, None, CONV_BR, FF_CHUNK), lambda j, i: (0, j, i, 0)),
                   pl.BlockSpec((2, None, 8, FF_CHUNK), lambda j, i: (0, j, 0, 0))),
        compiler_params=_params("parallel", "arbitrary"), name=name)(u, u, u, u, cw, cw, cb, cb, da, u, u, da)


def _rows_of(r):
    return lambda ref, idx: ref.at[:, pl.ds(idx * r, r), :]


def _slot1(ref, idx):
    return ref.at[:, idx]


def _slot0(ref, idx):
    return ref.at[idx]


def _all_gather(shards, full_shapes, places, *, name):
    n = len(shards)

    def body(*refs):
        ins, outs = refs[:n], refs[n:2 * n]
        send_sems, recv_sems, local_sems = refs[2 * n:]
        mx, my, mc = lax.axis_index("x"), lax.axis_index("y"), lax.axis_index("c")
        me, sibling = (mx, my, mc), (mx, my, 1 - mc)
        chips = [(1 - mx, my), (mx, 1 - my), (1 - mx, 1 - my)]

        def win(t, px, py, pc):
            return places[t](outs[t], 4 * px + 2 * py + pc)

        def copy(t, k, block, to, src=None):
            return pltpu.make_async_remote_copy(src_ref=win(t, *block) if src is None else src, dst_ref=win(t, *block),
                                                send_sem=send_sems.at[t, k], recv_sem=recv_sems.at[t, k], device_id=to, device_id_type=MESH)

        mine = [pltpu.make_async_copy(ins[t], win(t, *me), local_sems.at[t]) for t in range(n)]
        for cp in mine:
            cp.start()
        first = []
        for t in range(n):
            first += [copy(t, 0, me, sibling, src=ins[t])] + [copy(t, 1 + j, me, (*chip, mc), src=ins[t]) for j, chip in enumerate(chips)]
        for cp in first:
            cp.start()
        passed = []
        for j, chip in enumerate(chips):
            for t in range(n):
                copy(t, 1 + j, (*chip, mc), me).wait_recv()
                fwd = copy(t, 4 + j, (*chip, mc), sibling)
                fwd.start()
                passed.append(fwd)
        for t in range(n):
            copy(t, 0, sibling, me).wait_recv()
            for j, chip in enumerate(chips):
                copy(t, 4 + j, (*chip, 1 - mc), me).wait_recv()
        for cp in first + passed:
            cp.wait_send()
        for cp in mine:
            cp.wait()

    hbm = pl.BlockSpec(memory_space=pl.ANY)
    return pl.pallas_call(
        body, out_shape=tuple(jax.ShapeDtypeStruct(s, x.dtype) for s, x in zip(full_shapes, shards)),
        in_specs=[hbm] * n, out_specs=(hbm,) * n,
        scratch_shapes=[pltpu.SemaphoreType.DMA((n, 7)), pltpu.SemaphoreType.DMA((n, 7)), pltpu.SemaphoreType.DMA((n,))],
        name=name)(*shards)


FLIPS = [(fx, fy, fc) for fx in (0, 1) for fy in (0, 1) for fc in (0, 1)][1:]


def _exchange_copies(kind, places, src, land, send_sems, recv_sems, local_sems):
    mx, my, mc = lax.axis_index("x"), lax.axis_index("y"), lax.axis_index("c")
    me = 4 * mx + 2 * my + mc
    n = len(src)
    local, remote = [], []
    for t in range(n):
        if kind == "gather":
            local.append(pltpu.make_async_copy(src[t], places[t](land[t], me), local_sems.at[t]))
        else:
            local.append(pltpu.make_async_copy(places[t](src[t], me), land[t].at[me], local_sems.at[t]))
    for k, (fx, fy, fc) in enumerate(FLIPS):
        px, py, pc = mx ^ fx, my ^ fy, mc ^ fc
        peer = 4 * px + 2 * py + pc
        for t in range(n):
            sems = dict(send_sem=send_sems.at[7 * t + k], recv_sem=recv_sems.at[7 * t + k], device_id=(px, py, pc), device_id_type=MESH)
            if kind == "gather":
                pair = [(src[t], places[t](land[t], me)), (src[t], places[t](land[t], peer))]
            else:
                pair = [(places[t](src[t], peer), land[t].at[me]), (places[t](src[t], peer), land[t].at[peer])]
            remote.append([functools.partial(pltpu.make_async_remote_copy, src_ref=s_, dst_ref=d_, **sems) for s_, d_ in pair])
    return local, remote


HBM_SPEC = pl.BlockSpec(memory_space=pltpu.HBM)
SEM_SPEC = pl.BlockSpec(memory_space=pltpu.SEMAPHORE)
SIDE_EFFECT = pltpu.SideEffectType.DATAFLOW_SIDE_EFFECTING


def _exchange_start(kind, srcs, land_shapes, places, after, *, name):
    n = len(srcs)

    def body(*refs):
        src, land = refs[:n], refs[n:2 * n]
        send_sems, recv_sems, local_sems = refs[2 * n + 1:2 * n + 4]
        token = refs[-1]
        local, remote = _exchange_copies(kind, places, src, land, send_sems, recv_sems, local_sems)
        for cp in local:
            cp.start()
        for send, _ in remote:
            send().start()
        token[...] = jnp.zeros_like(token)

    hbm = lambda t: pltpu.with_memory_space_constraint(t, pltpu.HBM)
    lands = [hbm(lax.empty(tuple(s), x.dtype)) for s, x in zip(land_shapes, srcs)]
    out_shape = (pltpu.SemaphoreType.DMA((7 * n,)), pltpu.SemaphoreType.DMA((7 * n,)), pltpu.SemaphoreType.DMA((n,)),
                 *[pltpu.HBM(x.shape, x.dtype) for x in srcs], *[pltpu.HBM(tuple(s), x.dtype) for s, x in zip(land_shapes, srcs)],
                 jax.ShapeDtypeStruct((8, LANES), F32))
    outs = pl.pallas_call(
        body, name=name, out_shape=out_shape, in_specs=[HBM_SPEC] * (2 * n) + [pl.BlockSpec(memory_space=pl.ANY)],
        out_specs=(SEM_SPEC, SEM_SPEC, SEM_SPEC) + (HBM_SPEC,) * (2 * n) + (pl.BlockSpec(memory_space=pltpu.VMEM),),
        input_output_aliases={i: 3 + i for i in range(2 * n)},
        compiler_params=pltpu.CompilerParams(has_side_effects=SIDE_EFFECT))(*[hbm(x) for x in srcs], *lands, after)
    return dict(sems=outs[:3], src=outs[3:3 + n], land=outs[3 + n:3 + 2 * n], token=outs[-1])


def _exchange_wait(kind, started, places, after, *, name):
    n = len(started["src"])

    def body(*refs):
        src, land = refs[:n], refs[n:2 * n]
        send_sems, recv_sems, local_sems = refs[2 * n:2 * n + 3]
        local, remote = _exchange_copies(kind, places, src, land, send_sems, recv_sems, local_sems)
        for cp in local:
            cp.wait()
        for send, arrival in remote:
            send().wait_send()
            arrival().wait_recv()

    out_shape = tuple(pltpu.HBM(x.shape, x.dtype) for x in started["src"]) + tuple(pltpu.HBM(x.shape, x.dtype) for x in started["land"])
    outs = pl.pallas_call(
        body, name=name, out_shape=out_shape,
        in_specs=[HBM_SPEC] * (2 * n) + [SEM_SPEC] * 3 + [pl.BlockSpec(memory_space=pl.ANY)], out_specs=(HBM_SPEC,) * (2 * n),
        input_output_aliases={i: i for i in range(2 * n)},
        compiler_params=pltpu.CompilerParams(has_side_effects=SIDE_EFFECT))(*started["src"], *started["land"], *started["sems"], after)
    return list(outs[n:])


def _adamw(parts, w, m, v, *, br, name):
    layers, r, wd = w.shape
    assert len(parts) == layers

    def body(*refs):
        p_refs = refs[:layers]
        w_ref, m_ref, v_ref, g_ref, d_ref, nm_ref, nv_ref = refs[layers:]
        for k in range(layers):
            @pl.when(pl.program_id(0) == k)
            def _(p_ref=p_refs[k]):
                g = p_ref[0].astype(F32)
                for dev in range(1, N_DEV):
                    g = g + p_ref[dev].astype(F32)
                mm = ADAM_B1 * m_ref[...] + (1.0 - ADAM_B1) * g
                vv = ADAM_B2 * v_ref[...] + (1.0 - ADAM_B2) * (g * g)
                m_hat = mm / (1.0 - ADAM_B1 ** ADAM_STEP)
                v_hat = vv / (1.0 - ADAM_B2 ** ADAM_STEP)
                g_ref[...] = g
                d_ref[...] = -ADAM_LR * (m_hat / (jnp.sqrt(v_hat) + ADAM_EPS) + ADAM_WD * w_ref[...])
                nm_ref[...] = mm
                nv_ref[...] = vv

    p_spec = lambda k: pl.BlockSpec((N_DEV, None, br, wd), lambda l, i: (0, 0, jnp.where(l == k, i, 0), 0))
    blk = pl.BlockSpec((None, br, wd), lambda l, i: (l, i, 0))
    shp = jax.ShapeDtypeStruct((layers, r, wd), F32)
    return pl.pallas_call(body, out_shape=(shp, shp, shp, shp), grid=(layers, r // br),
                          in_specs=[p_spec(k) for k in range(layers)] + [blk, blk, blk], out_specs=(blk, blk, blk, blk),
                          compiler_params=_params("arbitrary", "arbitrary"), name=name)(*parts, w, m, v)


SMALL = ("norm_mix", "norm_mem", "norm_ffn", "b_forget", "conv_b", "norm_final")


def _pack(tensors):
    flat = jnp.concatenate([t.reshape(-1) for t in tensors])
    rows = -(-flat.shape[0] // (PACK_W * PACK_ROW_ALIGN)) * PACK_ROW_ALIGN
    flat = jnp.pad(flat, (0, rows * PACK_W - flat.shape[0]))
    return flat.reshape(1, rows, PACK_W)


def _unpack(buf, shapes):
    flat = buf.reshape(-1)
    out, off = [], 0
    for shp in shapes:
        n = math.prod(shp)
        out.append(flat[off:off + n].reshape(tuple(shp)))
        off += n
    return out


def _fox_permute(w):
    pad = jnp.zeros(w.shape[:-1] + (FOX_P - FOX_IN,), w.dtype)
    return jnp.concatenate([w[..., :3 * D_MIX], w[..., 3 * D_MIX + N_MIX_HEADS:], w[..., 3 * D_MIX:3 * D_MIX + N_MIX_HEADS], pad], axis=-1)


def _fox_unpermute(w):
    return jnp.concatenate([w[..., :3 * D_MIX], w[..., DIL_IN:DIL_IN + N_MIX_HEADS], w[..., 3 * D_MIX:DIL_IN]], axis=-1)


def _bias_layout(c):
    s = c.shape[0]
    ct = c[:, :N_MIX_HEADS].T.reshape(N_MIX_HP, 2, s)
    return jnp.pad(ct, ((0, 0), (0, 6), (0, 0)))


def _bias_grad(dck6):
    s = dck6.shape[2]
    dk = dck6[:, :2, :].reshape(N_MIX_HEADS, s).T
    return jnp.pad(dk, ((0, 0), (0, LANES - N_MIX_HEADS)))


def _device_step(x, mem, target, small, get_weights, put_grads):
    s = x.shape[0]
    mt = mem.shape[0]
    bq = 512
    cos_t, sin_t = _rope_tables(s)
    row = lambda t, l: t[l][None, :]
    saved = []
    h = x
    cb8 = small["conv_b"].reshape(DEPTH, N_DEV, 1, FF_CHUNK)
    for l in range(DEPTH):
        kind, slot = l % 2, l // 2
        wl = dict(get_weights(l, "attn", h))
        if l == 0:
            xn = _rmsnorm_fwd(h, row(small["norm_mix"], l), br=512, name=f"norm_mix_fwd{l}")
        mn = _rmsnorm_fwd(mem, row(small["norm_mem"], l), br=mt, name=f"norm_mem_fwd{l}")
        if kind == 0:
            proj = _mm(xn, wl["w_in"], "nn", tm=1024, tn=384, layer=0, name=f"in_proj{l}")
        else:
            proj, qk_r = _mm(xn, wl["w_in"], "nt", tm=1024, tn=512, layer=0, rope=(cos_t, sin_t, 2 * D_MIX), name=f"in_proj{l}")
        kvm = _mm(mn, wl["w_mem_kv"], "nn", tm=mt, tn=512, layer=0, name=f"mem_kv{l}")
        st = dict(h=h, xn=xn, mn=mn, proj=proj, kvm=kvm, w=wl)
        if kind == 0:
            b_pad = jnp.pad(small["b_forget"][slot], (0, LANES - N_MIX_HEADS))[None, :]
            c = _forget_cumsum(proj, b_pad, name=f"forget_cumsum{l}")
            ck6 = _bias_layout(c)
            mix, lse, heads = _attn_fwd(proj, proj, ck6, q_col=0, k_col=N_MIX_HP, v_col=2 * N_MIX_HP, n_hp=N_MIX_HP,
                                        causal=True, bq=min(s, 1024), bk=min(s, 1024), name=f"fox_fwd{l}")
            st.update(b_pad=b_pad, ck6=ck6, mix=mix, lse=lse)
        else:
            mix, lses, heads = _dil_fwd(qk_r, proj, name=f"dil_fwd{l}")
            st.update(qk_r=qk_r, lses=lses, mix=mix)
        mo, lse_m, heads = _attn_fwd(proj, kvm, None, q_col=QM_COL, k_col=0, v_col=N_MEM_HP, n_hp=N_MEM_HP, causal=False,
                                     bq=min(s, 2048), bk=mt, heads=heads, heads_col=N_MIX_HP, name=f"mem_fwd{l}")
        h1, xf = _mm(heads, wl["w_out"], "nn", tm=1024, tn=D_MODEL, res=h, layer=0, norm_gain=row(small["norm_ffn"], l),
                     name=f"out_proj{l}")
        wl.update(get_weights(l, "ffn", xf))
        u = _mm(xf, wl["w_up"], "nt", tm=1024, tn=FF_CHUNK, layer=0, chunk="b", name=f"up_proj{l}")
        a = _conv_fwd(u, wl["conv_w"], cb8[l], 0, name=f"conv_fwd{l}")
        st.update(mo=mo, lse_m=lse_m, heads=heads, h1=h1, xf=xf, u=u, a=a)
        saved.append(st)
        if l + 1 < DEPTH:
            h, xn = _mm(a, wl["w_down"], "nn", tm=512, tn=D_MODEL, res=h1, layer=0, chunk="reduce",
                        norm_gain=row(small["norm_mix"], l + 1), name=f"down_proj{l}")
        else:
            h = _mm(a, wl["w_down"], "nn", tm=1024, tn=512, res=h1, layer=0, chunk="reduce", name=f"down_proj{l}")

    dh, dhb, dg_final, loss = _loss_head(h, target, small["norm_final"][None, :], br=512, name="loss_head")
    gs = {k: [None] * DEPTH for k in ("norm_mix", "norm_mem", "norm_ffn", "conv_b")}
    gs["b_forget"] = [None] * 2
    dep = 0.0
    for l in reversed(range(DEPTH)):
        st = saved[l]
        wl = st["w"]
        gw = {}
        kind, slot = l % 2, l // 2
        da = _mm(dhb, wl["w_down"], "nt", tm=1024, tn=FF_CHUNK, layer=0, chunk="b", name=f"down_dx{l}")
        gw["w_down"] = _mm(st["a"], dhb, "tn", tm=FF_CHUNK, tn=512, out_dtype=BF16, chunk="a", name=f"down_dw{l}")
        du, dwb = _conv_bwd(st["u"], wl["conv_w"], cb8[l] + dep, da, 0, name=f"conv_bwd{l}")
        du = du.reshape(N_DEV, s, FF_CHUNK)
        dwb = dwb.reshape(N_DEV, 8, FF_CHUNK)
        gw["conv_w"] = dwb
        gs["conv_b"][l] = dwb[:, 3, :].reshape(-1)
        dh1, dh1b, dgf = _mm(du, wl["w_up"], "nn", tm=256, tn=D_MODEL, layer=0, chunk="reduce",
                             norm_bwd=(st["h1"], row(small["norm_ffn"], l), dh), name=f"up_dx{l}")
        gw["w_up"] = _mm(du, st["xf"], "tn", tm=FF_CHUNK, tn=512, out_dtype=BF16, chunk="a", name=f"up_dw{l}")
        gs["norm_ffn"][l] = dgf[0]
        dheads = _mm(dh1b, wl["w_out"], "nt", tm=1024, tn=512, layer=0, name=f"out_dx{l}")
        gw["w_out"] = _mm(st["heads"], dh1b, "tn", tm=512, tn=512, out_dtype=BF16, name=f"out_dw{l}")
        dqm, dkm, dvm = _attn_bwd(st["proj"], st["kvm"], st["mo"], dheads, st["lse_m"], None, q_col=QM_COL, k_col=0,
                                  v_col=N_MEM_HP, o_col=N_MIX_HP, n_hp=N_MEM_HP, causal=False, bq=min(s, 1024), bk=mt,
                                  name=f"mem_bwd{l}")
        dkvm = jnp.concatenate([dkm, dvm], axis=1)
        gw["w_mem_kv"] = _mm(st["mn"], dkvm, "tn", tm=512, tn=512, out_dtype=BF16, name=f"mem_kv_dw{l}")
        dmn = _mm(dkvm, wl["w_mem_kv"], "nt", tm=mt, tn=512, layer=0, name=f"mem_kv_dx{l}")
        dep_early = put_grads(l, "ffn", gw)
        _, _, dgm = _rmsnorm_bwd(mem, dmn, row(small["norm_mem"], l), None, br=mt, name=f"norm_mem_bwd{l}")
        gs["norm_mem"][l] = dgm[0]
        if kind == 0:
            dq, dk, dv, dcq6, dck6 = _attn_bwd(st["proj"], st["proj"], st["mix"], dheads, st["lse"], st["ck6"] + dep_early, q_col=0,
                                               k_col=N_MIX_HP, v_col=2 * N_MIX_HP, o_col=0, n_hp=N_MIX_HP, causal=True,
                                               bq=min(s, 1024), bk=min(s, 1024), name=f"fox_bwd{l}")
            dz, db = _forget_cumsum_bwd(st["proj"], st["b_pad"], _bias_grad(dcq6), _bias_grad(dck6), name=f"forget_cumsum_bwd{l}")
            gs["b_forget"][slot] = db[0, :N_MIX_HEADS]
            dproj = jnp.concatenate([dq.astype(BF16), dk, dv, dqm.astype(BF16), dz], axis=1)
        else:
            dq_r, dk_r, dv = _dil_bwd(st["qk_r"], st["proj"], st["mix"], dheads, st["lses"], name=f"dil_bwd{l}")
            dq = _rope(dq_r, cos_t + dep_early, -sin_t, n_cols=N_MIX_HP, out_dtype=BF16, br=512, name=f"rope_bwd_q{l}")
            dk = _rope(dk_r, cos_t, -sin_t, n_cols=N_MIX_HP, out_dtype=BF16, br=512, name=f"rope_bwd_k{l}")
            dproj = jnp.concatenate([dq, dk, dv.astype(BF16), dqm.astype(BF16)], axis=1)
        if kind == 0:
            gw["w_in"] = _mm(st["xn"], dproj, "tn", tm=512, tn=384, out_dtype=BF16, name=f"in_dw{l}")
        else:
            gw["w_in"] = _mm(dproj, st["xn"], "tn", tm=512, tn=512, out_dtype=BF16, name=f"in_dw{l}")
        dh, dhb, dgx = _mm(dproj, wl["w_in"], "nt" if kind == 0 else "nn", tm=512, tn=D_MODEL, layer=0,
                           norm_bwd=(st["h"], row(small["norm_mix"], l), dh1), name=f"in_dx{l}")
        gs["norm_mix"][l] = dgx[0]
        dep = put_grads(l, "attn", gw)

    grads_s = {k: jnp.stack(v) for k, v in gs.items()}
    grads_s["norm_final"] = dg_final[0]
    return loss[0, 0], dh, grads_s


def kernel(x, mem, norm_mix, norm_mem, norm_ffn, w_in_fox, b_forget, w_in_dil, w_mem_kv, w_out, w_up, conv_w, conv_b, w_down, norm_final, loss_target, m_norm_mix, m_norm_mem, m_norm_ffn, m_w_in_fox, m_b_forget, m_w_in_dil, m_w_mem_kv, m_w_out, m_w_up, m_conv_w, m_conv_b, m_w_down, m_norm_final, v_norm_mix, v_norm_mem, v_norm_ffn, v_w_in_fox, v_b_forget, v_w_in_dil, v_w_mem_kv, v_w_out, v_w_up, v_conv_w, v_conv_b, v_w_down, v_norm_final):
    names = ["norm_mix", "norm_mem", "norm_ffn", "w_in_fox", "b_forget", "w_in_dil", "w_mem_kv", "w_out", "w_up", "conv_w", "conv_b",
             "w_down", "norm_final"]
    w = dict(zip(names, (norm_mix, norm_mem, norm_ffn, w_in_fox, b_forget, w_in_dil, w_mem_kv, w_out, w_up, conv_w, conv_b, w_down, norm_final)))
    m = dict(zip(names, (m_norm_mix, m_norm_mem, m_norm_ffn, m_w_in_fox, m_b_forget, m_w_in_dil, m_w_mem_kv, m_w_out, m_w_up, m_conv_w,
                         m_conv_b, m_w_down, m_norm_final)))
    v = dict(zip(names, (v_norm_mix, v_norm_mem, v_norm_ffn, v_w_in_fox, v_b_forget, v_w_in_dil, v_w_mem_kv, v_w_out, v_w_up, v_conv_w,
                         v_conv_b, v_w_down, v_norm_final)))
    big = ("w_in_fox", "w_in_dil", "w_mem_kv", "w_out", "w_up", "w_down", "conv_w")
    small_shapes = [w[k].shape for k in SMALL]
    dil_c = w_in_dil.shape[2]
    rows = {k: w[k].shape[1] for k in ("w_in_fox", "w_mem_kv", "w_out", "w_down")}

    def places(l):
        w_in_place = _rows_of(rows["w_in_fox"]) if l % 2 == 0 else _slot1
        return [w_in_place, _rows_of(rows["w_mem_kv"]), _rows_of(rows["w_out"]), _slot1, _rows_of(rows["w_down"]), _slot1]

    def full_shapes(l):
        w_in_shape = (1, D_MODEL, FOX_P) if l % 2 == 0 else (1, N_DEV, dil_c, D_MODEL)
        return [w_in_shape, (1, D_MODEL, 2 * D_MEMQ), (1, D_MODEL, D_MODEL), (1, N_DEV, FF_CHUNK, D_MODEL), (1, D_FF, D_MODEL),
                (1, N_DEV, 3, FF_CHUNK)]

    transposed = lambda t: jnp.swapaxes(t, 1, 2)
    to_wire = {"w_in_fox": lambda t: _fox_permute(t).astype(BF16), "w_in_dil": lambda t: t.T.astype(BF16),
               "w_mem_kv": lambda t: t.astype(BF16), "w_out": lambda t: t.astype(BF16), "w_up": lambda t: t.T.astype(BF16),
               "w_down": lambda t: t.astype(BF16), "conv_w": lambda t: t}
    shard = lambda k, i: to_wire[k](w[k][i])
    shard_shape = lambda k: jax.eval_shape(lambda: shard(k, 0)).shape
    everything = (0, 1, 2, 3, 4, 5)
    gather_groups = {l: ((0, 1, 2), (3, 4, 5)) if l == 0 else (everything,) for l in range(DEPTH)}
    scatter_groups = {l: ((1, 2, 3, 4, 5), (0,)) if l == 0 else (everything,) for l in range(DEPTH)}
    pick = lambda seq, group: [seq[i] for i in group]
    tag = lambda l, group: f"{l}" + ("" if group == everything else "_" + "".join(str(i) for i in group))

    gathers, after = {}, norm_final
    for l in range(DEPTH):
        w_in_shard = shard("w_in_fox" if l % 2 == 0 else "w_in_dil", l // 2)[None]
        shards = [w_in_shard] + [shard(k, l)[None] for k in ("w_mem_kv", "w_out", "w_up", "w_down", "conv_w")]
        for group in gather_groups[l]:
            gathers[l, group] = _exchange_start("gather", pick(shards, group), pick(full_shapes(l), group), pick(places(l), group), after,
                                                name=f"weights_gather_start{tag(l, group)}")
            after = gathers[l, group]["token"]
    started = sum(g["token"][0, 0] for g in gathers.values())
    small = {k: w[k] for k in SMALL}
    small["norm_mix"] = norm_mix + started
    landed = {}

    def get_weights(l, part, h):
        group = [g for g in gather_groups[l] if (0 if part == "attn" else 3) in g][0]
        if (l, group) not in landed:
            lands = _exchange_wait("gather", gathers[l, group], pick(places(l), group), h, name=f"weights_gather_wait{tag(l, group)}")
            landed[l, group] = dict(zip(group, lands))
        got = landed[l, group]
        if part == "ffn":
            return dict(w_up=got[3], w_down=got[4].reshape(1, FF_HALF, FF_CHUNK, D_MODEL), conv_w=got[5])
        w_in = got[0] if l % 2 == 0 else got[0].reshape(1, N_DEV * dil_c, D_MODEL)
        return dict(w_in=w_in, w_mem_kv=got[1], w_out=got[2])

    scatters, pending = {}, {}

    def put_grads(l, part, g):
        pending.setdefault(l, {}).update(g)
        if part == "ffn" and len(scatter_groups[l]) == 1:
            return 0.0
        group = scatter_groups[l][0 if part == "ffn" else -1]
        have = pending[l]
        srcs = {3: lambda: have["w_up"][None], 4: lambda: have["w_down"].reshape(1, D_FF, D_MODEL), 5: lambda: have["conv_w"][None],
                1: lambda: have["w_mem_kv"][None], 2: lambda: have["w_out"][None]}
        if l % 2 == 0:
            srcs[0] = lambda: have["w_in"][None]
        else:
            srcs[0] = lambda: have["w_in"].reshape(1, N_DEV, dil_c, D_MODEL)
        shard_shapes = [shard_shape("w_in_fox" if l % 2 == 0 else "w_in_dil")] + \
            [shard_shape(k) for k in ("w_mem_kv", "w_out", "w_up", "w_down")] + [(8, FF_CHUNK)]
        sources = [srcs[i]() for i in group]
        scatters[l, group] = _exchange_start("scatter", sources, [(N_DEV, 1) + tuple(s) for s in pick(shard_shapes, group)],
                                             pick(places(l), group), sources[0], name=f"grads_scatter_start{tag(l, group)}")
        return scatters[l, group]["token"][0, 0]

    loss, grad_x, gs = _device_step(x[0], mem[0], loss_target[0], small, get_weights, put_grads)

    recv = {}

    def wait_scatter(l, group, after_):
        lands = _exchange_wait("scatter", scatters[l, group], pick(places(l), group), after_, name=f"grads_scatter_wait{tag(l, group)}")
        recv.setdefault(l, {}).update(zip(group, lands))

    for l in reversed(range(1, DEPTH)):
        wait_scatter(l, everything, grad_x)
    wait_scatter(0, scatter_groups[0][0], grad_x)
    s_pack = _pack([gs[k] for k in SMALL])
    (s_recv,) = _all_gather([s_pack], [(N_DEV,) + s_pack.shape], [_slot0], name="small_grads_all_gather")

    layer_tensors = ("w_in", "w_mem_kv", "w_out", "w_up", "w_down", "conv_w")
    layer_parts = lambda k: [recv[l][layer_tensors.index(k)] for l in range(DEPTH)]
    to_local = {k: (lambda t: t) for k in big}
    to_local["w_in_fox"] = _fox_permute
    to_local["w_in_dil"] = to_local["w_up"] = transposed
    from_local = {k: (lambda t: t) for k in big}
    from_local["w_in_fox"] = _fox_unpermute
    from_local["w_in_dil"] = from_local["w_up"] = transposed
    blocks = {"w_in_fox": rows["w_in_fox"], "w_in_dil": dil_c, "w_mem_kv": rows["w_mem_kv"], "w_out": rows["w_out"], "w_up": FF_CHUNK // 4,
              "w_down": rows["w_down"] // 2, "conv_w": 3}
    outs = {}

    def update(k, parts):
        f = to_local[k]
        outs[k] = [from_local[k](t) for t in _adamw(parts, f(w[k]), f(m[k]), f(v[k]), br=blocks[k], name=f"adamw_{k}")]

    update("w_in_dil", [recv[l][0] for l in range(1, DEPTH, 2)])
    update("w_up", layer_parts("w_up"))
    update("w_down", layer_parts("w_down"))
    update("conv_w", [p[:, :, :3, :] for p in layer_parts("conv_w")])
    update("w_mem_kv", layer_parts("w_mem_kv"))
    update("w_out", layer_parts("w_out"))
    wait_scatter(0, scatter_groups[0][-1], outs["w_out"][1])
    update("w_in_fox", [recv[l][0] for l in range(0, DEPTH, 2)])
    small_outs = _adamw([s_recv], _pack([w[k] for k in SMALL]), _pack([m[k] for k in SMALL]), _pack([v[k] for k in SMALL]),
                        br=s_pack.shape[1], name="adamw_small")
    res = []
    for i, os_ in enumerate(small_outs):
        d = {k: outs[k][i] for k in big}
        d.update(zip(SMALL, _unpack(os_, small_shapes)))
        res.append([d[k] for k in names])
    loss = lax.psum(loss, ("x", "y", "c"))
    return (loss, grad_x[None], *res[0], *res[1], *res[2], *res[3])
```
